```python
import math
import jax, jax.numpy as jnp
from jax import lax
import numpy as np

D_MODEL = 2048
BATCH = 8
SEQ = 4096
DEPTH = 2

GRID_W = 64
CTX_LEN = 256
N_MIXERS = 2
MIXER_S5 = 0
MIXER_POOL = 1
S5_GROUP = 16
S5_GROUPS = D_MODEL // S5_GROUP
S5_STATE = 64
S5_DT_MIN = 0.001
S5_DT_MAX = 0.1
POOL_WINDOWS = (2, 4, 8, 16)
POOL_GROUPS = len(POOL_WINDOWS)
POOL_CH = D_MODEL // POOL_GROUPS
D_FF = ((8 * D_MODEL // 3 + 255) // 256) * 256
N_MOD = 6
RMS_EPS = 1e-6
POS_BASE = 10000.0
N_S5_LAYERS = (DEPTH + 1) // 2
N_POOL_LAYERS = DEPTH // 2

kernel_name = "hybrid_s5_pool_convffn_dit"


def _rmsnorm(x, g):
    xf = x.astype(jnp.float32)
    y = xf * lax.rsqrt(jnp.mean(xf * xf, axis=-1, keepdims=True) + RMS_EPS)
    return (y * g.astype(jnp.float32)).astype(x.dtype)


def _modulate(xn, shift, scale):
    return xn * (1 + scale) + shift


def _grid_pos_emb(n_tokens, dim):
    rows = n_tokens // GRID_W
    r, col = jnp.meshgrid(jnp.arange(rows, dtype=jnp.float32),
                          jnp.arange(GRID_W, dtype=jnp.float32), indexing="ij")
    quarter = dim // 4
    omega = 1.0 / (POS_BASE ** (jnp.arange(quarter, dtype=jnp.float32) / quarter))

    def enc(p):
        ang = p.reshape(-1, 1) * omega[None, :]
        return jnp.concatenate([jnp.sin(ang), jnp.cos(ang)], axis=-1)

    return jnp.concatenate([enc(r), enc(col)], axis=-1)


def _ssm_combine(e1, e2):
    a1r, a1i, b1r, b1i = e1
    a2r, a2i, b2r, b2i = e2
    return (a2r * a1r - a2i * a1i,
            a2r * a1i + a2i * a1r,
            a2r * b1r - a2i * b1i + b2r,
            a2r * b1i + a2i * b1r + b2i)


def _s5_discretize(lam_re, lam_im, log_step, b_re, b_im):
    f32 = jnp.float32
    lam_re, lam_im = lam_re.astype(f32), lam_im.astype(f32)
    b_re, b_im = b_re.astype(f32), b_im.astype(f32)
    dt = jnp.exp(log_step.astype(f32))[:, None]
    mag = jnp.exp(lam_re * dt)
    abar_re = mag * jnp.cos(lam_im * dt)
    abar_im = mag * jnp.sin(lam_im * dt)
    nr, ni = abar_re - 1.0, abar_im
    den = lam_re * lam_re + lam_im * lam_im
    fr = (nr * lam_re + ni * lam_im) / den
    fi = (ni * lam_re - nr * lam_im) / den
    bbar_re = fr[..., None] * b_re - fi[..., None] * b_im
    bbar_im = fr[..., None] * b_im + fi[..., None] * b_re
    return abar_re, abar_im, bbar_re, bbar_im


def _s5_states(u_g, abar_re, abar_im, bbar_re, bbar_im, reverse, h0=None):
    b_re = jnp.einsum("blgc,gpc->blgp", u_g, bbar_re)
    b_im = jnp.einsum("blgc,gpc->blgp", u_g, bbar_im)
    if h0 is not None:
        h0_re, h0_im = h0
        first = -1 if reverse else 0
        b_re = b_re.at[:, first].add(abar_re * h0_re - abar_im * h0_im)
        b_im = b_im.at[:, first].add(abar_re * h0_im + abar_im * h0_re)
    a_re = jnp.broadcast_to(abar_re, b_re.shape)
    a_im = jnp.broadcast_to(abar_im, b_im.shape)
    _, _, h_re, h_im = lax.associative_scan(_ssm_combine, (a_re, a_im, b_re, b_im),
                                            reverse=reverse, axis=1)
    return h_re, h_im


def _s5_readout(h_re, h_im, c_re, c_im):
    return (jnp.einsum("blgp,gcp->blgc", h_re, c_re.astype(jnp.float32))
            - jnp.einsum("blgp,gcp->blgc", h_im, c_im.astype(jnp.float32)))


def _s5_glu(y, glu_w, dtype):
    z = jax.nn.gelu(y).astype(dtype)
    val, gate = jnp.split(z @ glu_w, 2, axis=-1)
    return val * jax.nn.sigmoid(gate)


def _s5_mixer(u, u_c, lam_re, lam_im, log_step, b_re, b_im, c_re, c_im, d_skip, glu_w,
              with_ctx_out):
    bsz, n_lat, dim = u.shape
    n_ctx = u_c.shape[1]
    u_g = u.astype(jnp.float32).reshape(bsz, n_lat, S5_GROUPS, S5_GROUP)
    uc_g = u_c.astype(jnp.float32).reshape(bsz, n_ctx, S5_GROUPS, S5_GROUP)
    dsk = d_skip.astype(jnp.float32).reshape(S5_GROUPS, S5_GROUP)
    y = u_g * dsk
    y_c = uc_g * dsk if with_ctx_out else None
    for direction, rev in ((0, False), (1, True)):
        disc = _s5_discretize(lam_re[direction], lam_im[direction], log_step[direction],
                              b_re[direction], b_im[direction])
        hc_re, hc_im = _s5_states(uc_g, *disc, reverse=rev)
        end = 0 if rev else -1
        h_re, h_im = _s5_states(u_g, *disc, reverse=rev, h0=(hc_re[:, end], hc_im[:, end]))
        y = y + _s5_readout(h_re, h_im, c_re[direction], c_im[direction])
        if with_ctx_out:
            y_c = y_c + _s5_readout(hc_re, hc_im, c_re[direction], c_im[direction])
    out = _s5_glu(y.reshape(bsz, n_lat, dim), glu_w, u.dtype)
    out_c = _s5_glu(y_c.reshape(bsz, n_ctx, dim), glu_w, u.dtype) if with_ctx_out else None
    return out, out_c


def _pool_mixer(u, w, scale):
    bsz, n_tok, dim = u.shape
    uf = u.astype(jnp.float32)
    csum = jnp.concatenate([jnp.zeros((bsz, 1, dim), jnp.float32),
                            jnp.cumsum(uf, axis=1)], axis=1)
    t = jnp.arange(n_tok)
    groups = []
    for g, win in enumerate(POOL_WINDOWS):
        lo = jnp.clip(t - win // 2, 0, n_tok - 1)
        hi = jnp.clip(t + win // 2 - 1, 0, n_tok - 1)
        ch = slice(g * POOL_CH, (g + 1) * POOL_CH)
        cs = csum[:, :, ch]
        cnt = (hi - lo + 1).astype(jnp.float32)[None, :, None]
        groups.append((cs[:, hi + 1] - cs[:, lo]) / cnt - uf[:, :, ch])
    p = jnp.stack(groups, axis=2).astype(u.dtype)
    y = jnp.einsum("blgc,gcd->blgd", p, w).reshape(bsz, n_tok, dim)
    return y * scale


def _conv_ffn(u, up, conv, conv_b, down):
    h = u @ up
    hp = jnp.pad(h, ((0, 0), (1, 1), (0, 0)))
    h = hp[:, :-2] * conv[0] + hp[:, 1:-1] * conv[1] + hp[:, 2:] * conv[2] + conv_b
    val, gate = jnp.split(h, 2, axis=-1)
    return (jax.nn.silu(gate) * val) @ down


def _fwd_setup_inputs(seed: int = 0) -> dict:
    key = jax.random.key(seed)
    ks = jax.random.split(key, 24)
    f32 = jnp.float32
    nrm = lambda k, shp: jax.random.normal(k, shp, f32)
    G, P, C = S5_GROUPS, S5_STATE, S5_GROUP
    s5_lam_re = -0.5 + 0.01 * nrm(ks[5], (N_S5_LAYERS, 2, G, P))
    s5_lam_im = math.pi * jnp.arange(P, dtype=f32) + 0.01 * nrm(ks[6], (N_S5_LAYERS, 2, G, P))
    s5_log_step = jax.random.uniform(ks[7], (N_S5_LAYERS, 2, G), f32,
                                     math.log(S5_DT_MIN), math.log(S5_DT_MAX))
    ffn_conv = 0.3 * nrm(ks[18], (DEPTH, 3, 2 * D_FF))
    ffn_conv = ffn_conv.at[:, 1].add(1.0)
    return {
        "x": nrm(ks[0], (BATCH, SEQ, D_MODEL)),
        "c": nrm(ks[1], (BATCH, D_MODEL)),
        "ctx": nrm(ks[2], (BATCH, CTX_LEN, D_MODEL)),
        "c_ctx": nrm(ks[3], (D_MODEL,)),
        "ada_w": 0.5 * D_MODEL ** -0.5 * nrm(ks[4], (DEPTH, D_MODEL, N_MOD * D_MODEL)),
        "ada_b": 0.01 * nrm(ks[8], (DEPTH, N_MOD * D_MODEL)),
        "norm_g": 1.0 + 0.05 * nrm(ks[9], (DEPTH, 4, D_MODEL)),
        "s5_lam_re": s5_lam_re,
        "s5_lam_im": s5_lam_im,
        "s5_log_step": s5_log_step,
        "s5_b_re": (2 * C) ** -0.5 * nrm(ks[10], (N_S5_LAYERS, 2, G, P, C)),
        "s5_b_im": (2 * C) ** -0.5 * nrm(ks[11], (N_S5_LAYERS, 2, G, P, C)),
        "s5_c_re": P ** -0.5 * nrm(ks[12], (N_S5_LAYERS, 2, G, C, P)),
        "s5_c_im": P ** -0.5 * nrm(ks[13], (N_S5_LAYERS, 2, G, C, P)),
        "s5_d": nrm(ks[14], (N_S5_LAYERS, D_MODEL)),
        "s5_glu_w": D_MODEL ** -0.5 * nrm(ks[15], (N_S5_LAYERS, D_MODEL, 2 * D_MODEL)),
        "pool_w": POOL_CH ** -0.5 * nrm(ks[16], (N_POOL_LAYERS, POOL_GROUPS, POOL_CH, POOL_CH)),
        "pool_scale": 1.0 + 0.1 * nrm(ks[17], (N_POOL_LAYERS, D_MODEL)),
        "ffn_up": D_MODEL ** -0.5 * nrm(ks[19], (DEPTH, D_MODEL, 2 * D_FF)),
        "ffn_conv": ffn_conv,
        "ffn_conv_b": 0.01 * nrm(ks[20], (DEPTH, 2 * D_FF)),
        "ffn_down": D_FF ** -0.5 * nrm(ks[21], (DEPTH, D_FF, D_MODEL)),
    }


def _fwd_reference(x, c, ctx, c_ctx, ada_w, ada_b, norm_g, s5_lam_re, s5_lam_im, s5_log_step,
              s5_b_re, s5_b_im, s5_c_re, s5_c_im, s5_d, s5_glu_w, pool_w, pool_scale,
              ffn_up, ffn_conv, ffn_conv_b, ffn_down):
    n_lat = x.shape[1]
    x = x + _grid_pos_emb(n_lat, D_MODEL).astype(x.dtype)[None]
    cond = jax.nn.silu(c)
    cond_ctx = jax.nn.silu(c_ctx)
    h_ctx = ctx
    for i in range(DEPTH):
        kind = i % N_MIXERS
        k = i // N_MIXERS
        ctx_read = kind == MIXER_S5
        ctx_later = any(j % N_MIXERS == MIXER_S5 for j in range(i + 1, DEPTH))
        mod = jnp.split((cond @ ada_w[i] + ada_b[i])[:, None, :], N_MOD, axis=-1)
        u = _modulate(_rmsnorm(x, norm_g[i, 0]), mod[0], mod[1])
        if ctx_read or ctx_later:
            mod_c = jnp.split(cond_ctx @ ada_w[i] + ada_b[i], N_MOD)
            u_c = _modulate(_rmsnorm(h_ctx, norm_g[i, 0]), mod_c[0], mod_c[1])
        if kind == MIXER_S5:
            y, y_c = _s5_mixer(u, u_c, s5_lam_re[k], s5_lam_im[k], s5_log_step[k],
                               s5_b_re[k], s5_b_im[k], s5_c_re[k], s5_c_im[k],
                               s5_d[k], s5_glu_w[k], ctx_later)
        else:
            y = _pool_mixer(u, pool_w[k], pool_scale[k])
            y_c = _pool_mixer(u_c, pool_w[k], pool_scale[k]) if ctx_later else None
        x = x + mod[2] * _rmsnorm(y, norm_g[i, 1])
        f = _conv_ffn(_modulate(_rmsnorm(x, norm_g[i, 2]), mod[3], mod[4]),
                      ffn_up[i], ffn_conv[i], ffn_conv_b[i], ffn_down[i])
        x = x + mod[5] * _rmsnorm(f, norm_g[i, 3])
        if ctx_later:
            h_ctx = h_ctx + mod_c[2] * _rmsnorm(y_c, norm_g[i, 1])
            fc = _conv_ffn(_modulate(_rmsnorm(h_ctx, norm_g[i, 2]), mod_c[3], mod_c[4]),
                           ffn_up[i], ffn_conv[i], ffn_conv_b[i], ffn_down[i])
            h_ctx = h_ctx + mod_c[5] * _rmsnorm(fc, norm_g[i, 3])
    return x


import jax as _jax
import jax.numpy as _jnp

TWIN_FORMAT = 'train_step'
FWD_PARAMS = ['x', 'c', 'ctx', 'c_ctx', 'ada_w', 'ada_b', 'norm_g', 's5_lam_re', 's5_lam_im', 's5_log_step', 's5_b_re', 's5_b_im', 's5_c_re', 's5_c_im', 's5_d', 's5_glu_w', 'pool_w', 'pool_scale', 'ffn_up', 'ffn_conv', 'ffn_conv_b', 'ffn_down']
TWIN_WEIGHTS = ['c_ctx', 'ada_w', 'ada_b', 'norm_g', 's5_lam_re', 's5_lam_im', 's5_log_step', 's5_b_re', 's5_b_im', 's5_c_re', 's5_c_im', 's5_d', 's5_glu_w', 'pool_w', 'pool_scale', 'ffn_up', 'ffn_conv', 'ffn_conv_b', 'ffn_down']
TWIN_DIFF_INPUT = 'x'
TWIN_INPUTS = ['x', 'c', 'ctx', 'c_ctx', 'ada_w', 'ada_b', 'norm_g', 's5_lam_re', 's5_lam_im', 's5_log_step', 's5_b_re', 's5_b_im', 's5_c_re', 's5_c_im', 's5_d', 's5_glu_w', 'pool_w', 'pool_scale', 'ffn_up', 'ffn_conv', 'ffn_conv_b', 'ffn_down', 'loss_target', 'm_c_ctx', 'm_ada_w', 'm_ada_b', 'm_norm_g', 'm_s5_lam_re', 'm_s5_lam_im', 'm_s5_log_step', 'm_s5_b_re', 'm_s5_b_im', 'm_s5_c_re', 'm_s5_c_im', 'm_s5_d', 'm_s5_glu_w', 'm_pool_w', 'm_pool_scale', 'm_ffn_up', 'm_ffn_conv', 'm_ffn_conv_b', 'm_ffn_down', 'v_c_ctx', 'v_ada_w', 'v_ada_b', 'v_norm_g', 'v_s5_lam_re', 'v_s5_lam_im', 'v_s5_log_step', 'v_s5_b_re', 'v_s5_b_im', 'v_s5_c_re', 'v_s5_c_im', 'v_s5_d', 'v_s5_glu_w', 'v_pool_w', 'v_pool_scale', 'v_ffn_up', 'v_ffn_conv', 'v_ffn_conv_b', 'v_ffn_down']
TWIN_OUTPUTS = ['loss', 'grad_x', 'grad_c_ctx', 'grad_ada_w', 'grad_ada_b', 'grad_norm_g', 'grad_s5_lam_re', 'grad_s5_lam_im', 'grad_s5_log_step', 'grad_s5_b_re', 'grad_s5_b_im', 'grad_s5_c_re', 'grad_s5_c_im', 'grad_s5_d', 'grad_s5_glu_w', 'grad_pool_w', 'grad_pool_scale', 'grad_ffn_up', 'grad_ffn_conv', 'grad_ffn_conv_b', 'grad_ffn_down', 'delta_c_ctx', 'delta_ada_w', 'delta_ada_b', 'delta_norm_g', 'delta_s5_lam_re', 'delta_s5_lam_im', 'delta_s5_log_step', 'delta_s5_b_re', 'delta_s5_b_im', 'delta_s5_c_re', 'delta_s5_c_im', 'delta_s5_d', 'delta_s5_glu_w', 'delta_pool_w', 'delta_pool_scale', 'delta_ffn_up', 'delta_ffn_conv', 'delta_ffn_conv_b', 'delta_ffn_down', 'new_m_c_ctx', 'new_m_ada_w', 'new_m_ada_b', 'new_m_norm_g', 'new_m_s5_lam_re', 'new_m_s5_lam_im', 'new_m_s5_log_step', 'new_m_s5_b_re', 'new_m_s5_b_im', 'new_m_s5_c_re', 'new_m_s5_c_im', 'new_m_s5_d', 'new_m_s5_glu_w', 'new_m_pool_w', 'new_m_pool_scale', 'new_m_ffn_up', 'new_m_ffn_conv', 'new_m_ffn_conv_b', 'new_m_ffn_down', 'new_v_c_ctx', 'new_v_ada_w', 'new_v_ada_b', 'new_v_norm_g', 'new_v_s5_lam_re', 'new_v_s5_lam_im', 'new_v_s5_log_step', 'new_v_s5_b_re', 'new_v_s5_b_im', 'new_v_s5_c_re', 'new_v_s5_c_im', 'new_v_s5_d', 'new_v_s5_glu_w', 'new_v_pool_w', 'new_v_pool_scale', 'new_v_ffn_up', 'new_v_ffn_conv', 'new_v_ffn_conv_b', 'new_v_ffn_down']
TWIN_LEAF_KINDS = {'loss': 'loss', 'grad_x': 'grad_x', 'grad_c_ctx': 'grad_w', 'grad_ada_w': 'grad_w', 'grad_ada_b': 'grad_w', 'grad_norm_g': 'grad_w', 'grad_s5_lam_re': 'grad_w', 'grad_s5_lam_im': 'grad_w', 'grad_s5_log_step': 'grad_w', 'grad_s5_b_re': 'grad_w', 'grad_s5_b_im': 'grad_w', 'grad_s5_c_re': 'grad_w', 'grad_s5_c_im': 'grad_w', 'grad_s5_d': 'grad_w', 'grad_s5_glu_w': 'grad_w', 'grad_pool_w': 'grad_w', 'grad_pool_scale': 'grad_w', 'grad_ffn_up': 'grad_w', 'grad_ffn_conv': 'grad_w', 'grad_ffn_conv_b': 'grad_w', 'grad_ffn_down': 'grad_w', 'delta_c_ctx': 'delta_w', 'delta_ada_w': 'delta_w', 'delta_ada_b': 'delta_w', 'delta_norm_g': 'delta_w', 'delta_s5_lam_re': 'delta_w', 'delta_s5_lam_im': 'delta_w', 'delta_s5_log_step': 'delta_w', 'delta_s5_b_re': 'delta_w', 'delta_s5_b_im': 'delta_w', 'delta_s5_c_re': 'delta_w', 'delta_s5_c_im': 'delta_w', 'delta_s5_d': 'delta_w', 'delta_s5_glu_w': 'delta_w', 'delta_pool_w': 'delta_w', 'delta_pool_scale': 'delta_w', 'delta_ffn_up': 'delta_w', 'delta_ffn_conv': 'delta_w', 'delta_ffn_conv_b': 'delta_w', 'delta_ffn_down': 'delta_w', 'new_m_c_ctx': 'new_m', 'new_m_ada_w': 'new_m', 'new_m_ada_b': 'new_m', 'new_m_norm_g': 'new_m', 'new_m_s5_lam_re': 'new_m', 'new_m_s5_lam_im': 'new_m', 'new_m_s5_log_step': 'new_m', 'new_m_s5_b_re': 'new_m', 'new_m_s5_b_im': 'new_m', 'new_m_s5_c_re': 'new_m', 'new_m_s5_c_im': 'new_m', 'new_m_s5_d': 'new_m', 'new_m_s5_glu_w': 'new_m', 'new_m_pool_w': 'new_m', 'new_m_pool_scale': 'new_m', 'new_m_ffn_up': 'new_m', 'new_m_ffn_conv': 'new_m', 'new_m_ffn_conv_b': 'new_m', 'new_m_ffn_down': 'new_m', 'new_v_c_ctx': 'new_v', 'new_v_ada_w': 'new_v', 'new_v_ada_b': 'new_v', 'new_v_norm_g': 'new_v', 'new_v_s5_lam_re': 'new_v', 'new_v_s5_lam_im': 'new_v', 'new_v_s5_log_step': 'new_v', 'new_v_s5_b_re': 'new_v', 'new_v_s5_b_im': 'new_v', 'new_v_s5_c_re': 'new_v', 'new_v_s5_c_im': 'new_v', 'new_v_s5_d': 'new_v', 'new_v_s5_glu_w': 'new_v', 'new_v_pool_w': 'new_v', 'new_v_pool_scale': 'new_v', 'new_v_ffn_up': 'new_v', 'new_v_ffn_conv': 'new_v', 'new_v_ffn_conv_b': 'new_v', 'new_v_ffn_down': 'new_v'}


def _forward(args):
    return _fwd_reference(*[args[k] for k in FWD_PARAMS])


def _output_shape():
    def fwd():
        inp = _fwd_setup_inputs(0)
        return _fwd_reference(*[inp[k] for k in FWD_PARAMS])
    out = _jax.eval_shape(fwd)
    return out.shape, out.dtype

N_MICROBATCH = 1
ADAM_LR = 0.001
ADAM_B1 = 0.9
ADAM_B2 = 0.999
ADAM_EPS = 1e-08
ADAM_WD = 0.01
ADAM_STEP = 10
PER_EXAMPLE_BATCH_AXIS = {'x': 0, 'c': 0, 'ctx': 0, 'loss_target': 0}
SHARED_INPUTS = []
_WEIGHT_DTYPES = {'c_ctx': _jnp.float32, 'ada_w': _jnp.float32, 'ada_b': _jnp.float32, 'norm_g': _jnp.float32, 's5_lam_re': _jnp.float32, 's5_lam_im': _jnp.float32, 's5_log_step': _jnp.float32, 's5_b_re': _jnp.float32, 's5_b_im': _jnp.float32, 's5_c_re': _jnp.float32, 's5_c_im': _jnp.float32, 's5_d': _jnp.float32, 's5_glu_w': _jnp.float32, 'pool_w': _jnp.float32, 'pool_scale': _jnp.float32, 'ffn_up': _jnp.float32, 'ffn_conv': _jnp.float32, 'ffn_conv_b': _jnp.float32, 'ffn_down': _jnp.float32}
MOMENT_SCALE = {'c_ctx': 8.789694e-03, 'ada_w': 8.996148e-01, 'ada_b': 2.299912e+00, 'norm_g': 1.309093e+00, 's5_lam_re': 7.553779e-02, 's5_lam_im': 6.520766e-02, 's5_log_step': 2.108901e+00, 's5_b_re': 5.392097e-02, 's5_b_im': 5.576905e-02, 's5_c_re': 7.093846e-02, 's5_c_im': 8.090876e-02, 's5_d': 8.807265e-01, 's5_glu_w': 6.076239e-01, 'pool_w': 9.798312e-02, 'pool_scale': 8.121636e-01, 'ffn_up': 2.034221e-01, 'ffn_conv': 2.073490e-01, 'ffn_conv_b': 3.427723e-01, 'ffn_down': 3.771242e-01}


def _to_microbatches(a, axis):
    t = _jnp.moveaxis(a, axis, 0)
    t = t.reshape((N_MICROBATCH, t.shape[0] // N_MICROBATCH) + t.shape[1:])
    return _jnp.moveaxis(t, 1, axis + 1)


def setup_inputs(seed: int = 0) -> dict:
    inp = _fwd_setup_inputs(seed)
    key = _jax.random.fold_in(_jax.random.key(seed), 7919)
    shape, _ = _output_shape()
    out = dict(inp)
    out["loss_target"] = _jax.random.normal(_jax.random.fold_in(key, 0), shape, _jnp.float32)
    for i, name in enumerate(TWIN_WEIGHTS):
        w = inp[name].astype(_jnp.float32)
        if MOMENT_SCALE is None:
            s = _jnp.sqrt(_jnp.mean(_jnp.square(w)) + 1e-30)
        else:
            s = MOMENT_SCALE[name]
        km, kv = _jax.random.split(_jax.random.fold_in(key, i + 1))
        out[name] = w
        out["m_" + name] = s * _jax.random.normal(km, w.shape, _jnp.float32)
        out["v_" + name] = (s * s) * _jax.random.uniform(kv, w.shape, _jnp.float32, 0.5, 1.5)
    if N_MICROBATCH > 1:
        for name, axis in PER_EXAMPLE_BATCH_AXIS.items():
            out[name] = _to_microbatches(out[name], axis)
    return {'x': out['x'], 'c': out['c'], 'ctx': out['ctx'], 'c_ctx': out['c_ctx'], 'ada_w': out['ada_w'], 'ada_b': out['ada_b'], 'norm_g': out['norm_g'], 's5_lam_re': out['s5_lam_re'], 's5_lam_im': out['s5_lam_im'], 's5_log_step': out['s5_log_step'], 's5_b_re': out['s5_b_re'], 's5_b_im': out['s5_b_im'], 's5_c_re': out['s5_c_re'], 's5_c_im': out['s5_c_im'], 's5_d': out['s5_d'], 's5_glu_w': out['s5_glu_w'], 'pool_w': out['pool_w'], 'pool_scale': out['pool_scale'], 'ffn_up': out['ffn_up'], 'ffn_conv': out['ffn_conv'], 'ffn_conv_b': out['ffn_conv_b'], 'ffn_down': out['ffn_down'], 'loss_target': out['loss_target'], 'm_c_ctx': out['m_c_ctx'], 'm_ada_w': out['m_ada_w'], 'm_ada_b': out['m_ada_b'], 'm_norm_g': out['m_norm_g'], 'm_s5_lam_re': out['m_s5_lam_re'], 'm_s5_lam_im': out['m_s5_lam_im'], 'm_s5_log_step': out['m_s5_log_step'], 'm_s5_b_re': out['m_s5_b_re'], 'm_s5_b_im': out['m_s5_b_im'], 'm_s5_c_re': out['m_s5_c_re'], 'm_s5_c_im': out['m_s5_c_im'], 'm_s5_d': out['m_s5_d'], 'm_s5_glu_w': out['m_s5_glu_w'], 'm_pool_w': out['m_pool_w'], 'm_pool_scale': out['m_pool_scale'], 'm_ffn_up': out['m_ffn_up'], 'm_ffn_conv': out['m_ffn_conv'], 'm_ffn_conv_b': out['m_ffn_conv_b'], 'm_ffn_down': out['m_ffn_down'], 'v_c_ctx': out['v_c_ctx'], 'v_ada_w': out['v_ada_w'], 'v_ada_b': out['v_ada_b'], 'v_norm_g': out['v_norm_g'], 'v_s5_lam_re': out['v_s5_lam_re'], 'v_s5_lam_im': out['v_s5_lam_im'], 'v_s5_log_step': out['v_s5_log_step'], 'v_s5_b_re': out['v_s5_b_re'], 'v_s5_b_im': out['v_s5_b_im'], 'v_s5_c_re': out['v_s5_c_re'], 'v_s5_c_im': out['v_s5_c_im'], 'v_s5_d': out['v_s5_d'], 'v_s5_glu_w': out['v_s5_glu_w'], 'v_pool_w': out['v_pool_w'], 'v_pool_scale': out['v_pool_scale'], 'v_ffn_up': out['v_ffn_up'], 'v_ffn_conv': out['v_ffn_conv'], 'v_ffn_conv_b': out['v_ffn_conv_b'], 'v_ffn_down': out['v_ffn_down']}


def _loss(weights, diff, rest, loss_target):
    with _jax.named_scope("forward"):
        args = {**rest, TWIN_DIFF_INPUT: diff, **{k: w.astype(_WEIGHT_DTYPES[k]) for k, w in weights.items()}}
        y = _forward(args)
    with _jax.named_scope("loss_head"):
        err = _jnp.square(y.astype(_jnp.float32) - loss_target)
        return 0.5 * _jnp.sum(_jnp.mean(err, axis=-1)) if err.ndim else 0.5 * err


def _adamw(w, g, m, v):
    m = ADAM_B1 * m + (1.0 - ADAM_B1) * g
    v = ADAM_B2 * v + (1.0 - ADAM_B2) * _jnp.square(g)
    m_hat = m / (1.0 - ADAM_B1 ** ADAM_STEP)
    v_hat = v / (1.0 - ADAM_B2 ** ADAM_STEP)
    delta = -ADAM_LR * (m_hat / (_jnp.sqrt(v_hat) + ADAM_EPS) + ADAM_WD * w)
    return delta, m, v


def reference(x, c, ctx, c_ctx, ada_w, ada_b, norm_g, s5_lam_re, s5_lam_im, s5_log_step, s5_b_re, s5_b_im, s5_c_re, s5_c_im, s5_d, s5_glu_w, pool_w, pool_scale, ffn_up, ffn_conv, ffn_conv_b, ffn_down, loss_target, m_c_ctx, m_ada_w, m_ada_b, m_norm_g, m_s5_lam_re, m_s5_lam_im, m_s5_log_step, m_s5_b_re, m_s5_b_im, m_s5_c_re, m_s5_c_im, m_s5_d, m_s5_glu_w, m_pool_w, m_pool_scale, m_ffn_up, m_ffn_conv, m_ffn_conv_b, m_ffn_down, v_c_ctx, v_ada_w, v_ada_b, v_norm_g, v_s5_lam_re, v_s5_lam_im, v_s5_log_step, v_s5_b_re, v_s5_b_im, v_s5_c_re, v_s5_c_im, v_s5_d, v_s5_glu_w, v_pool_w, v_pool_scale, v_ffn_up, v_ffn_conv, v_ffn_conv_b, v_ffn_down):
    given = dict(x=x, c=c, ctx=ctx, c_ctx=c_ctx, ada_w=ada_w, ada_b=ada_b, norm_g=norm_g, s5_lam_re=s5_lam_re, s5_lam_im=s5_lam_im, s5_log_step=s5_log_step, s5_b_re=s5_b_re, s5_b_im=s5_b_im, s5_c_re=s5_c_re, s5_c_im=s5_c_im, s5_d=s5_d, s5_glu_w=s5_glu_w, pool_w=pool_w, pool_scale=pool_scale, ffn_up=ffn_up, ffn_conv=ffn_conv, ffn_conv_b=ffn_conv_b, ffn_down=ffn_down, loss_target=loss_target, m_c_ctx=m_c_ctx, m_ada_w=m_ada_w, m_ada_b=m_ada_b, m_norm_g=m_norm_g, m_s5_lam_re=m_s5_lam_re, m_s5_lam_im=m_s5_lam_im, m_s5_log_step=m_s5_log_step, m_s5_b_re=m_s5_b_re, m_s5_b_im=m_s5_b_im, m_s5_c_re=m_s5_c_re, m_s5_c_im=m_s5_c_im, m_s5_d=m_s5_d, m_s5_glu_w=m_s5_glu_w, m_pool_w=m_pool_w, m_pool_scale=m_pool_scale, m_ffn_up=m_ffn_up, m_ffn_conv=m_ffn_conv, m_ffn_conv_b=m_ffn_conv_b, m_ffn_down=m_ffn_down, v_c_ctx=v_c_ctx, v_ada_w=v_ada_w, v_ada_b=v_ada_b, v_norm_g=v_norm_g, v_s5_lam_re=v_s5_lam_re, v_s5_lam_im=v_s5_lam_im, v_s5_log_step=v_s5_log_step, v_s5_b_re=v_s5_b_re, v_s5_b_im=v_s5_b_im, v_s5_c_re=v_s5_c_re, v_s5_c_im=v_s5_c_im, v_s5_d=v_s5_d, v_s5_glu_w=v_s5_glu_w, v_pool_w=v_pool_w, v_pool_scale=v_pool_scale, v_ffn_up=v_ffn_up, v_ffn_conv=v_ffn_conv, v_ffn_conv_b=v_ffn_conv_b, v_ffn_down=v_ffn_down)
    weights = {n: given[n] for n in TWIN_WEIGHTS}
    shared = {n: given[n] for n in SHARED_INPUTS}
    per_example = {n: given[n] for n in ['x', 'c', 'ctx']}
    grad_fn = _jax.value_and_grad(_loss, argnums=(0, 1))

    def one_microbatch(ex, loss_target):
        ex = dict(ex)
        diff = ex.pop(TWIN_DIFF_INPUT)
        return grad_fn(weights, diff, {**shared, **ex}, loss_target)

    if N_MICROBATCH == 1:
        loss, (grad_w, grad_x) = one_microbatch(per_example, given["loss_target"])
    else:
        def body(carry, xs):
            loss_sum, grad_sum = carry
            l_k, (gw_k, gx_k) = one_microbatch(xs[0], xs[1])
            with _jax.named_scope("update"):
                return (loss_sum + l_k, _jax.tree.map(_jnp.add, grad_sum, gw_k)), gx_k

        init = (_jnp.zeros((), _jnp.float32), _jax.tree.map(_jnp.zeros_like, weights))
        (loss, grad_w), grad_x = _jax.lax.scan(body, init, (per_example, given["loss_target"]))
    with _jax.named_scope("update"):
        delta_w, new_m, new_v = {}, {}, {}
        for n in TWIN_WEIGHTS:
            delta_w[n], new_m[n], new_v[n] = _adamw(weights[n], grad_w[n], given["m_" + n], given["v_" + n])
    return (loss, grad_x, *[grad_w[n] for n in TWIN_WEIGHTS], *[delta_w[n] for n in TWIN_WEIGHTS],
            *[new_m[n] for n in TWIN_WEIGHTS], *[new_v[n] for n in TWIN_WEIGHTS])
```

```python
import functools
import math

import jax
import jax.numpy as jnp
from jax import lax
from jax.experimental import pallas as pl
from jax.experimental.pallas import tpu as pltpu

F32 = jnp.float32
_ACT = jnp.bfloat16
N_DEV = 8
NSEG = 8
S5_CH = 16
S5_P = 64
LANE = 128
S5_TILE_CH = LANE
S5_TILE_G = S5_TILE_CH // S5_CH
S5_TILE_W = S5_TILE_G * S5_P
GRID_W = 64
POOL_WINDOWS = (2, 4, 8, 16)
POOL_HALO = 64
RMS_EPS = 1e-6
POS_BASE = 10000.0
ADAM_LR, ADAM_B1, ADAM_B2, ADAM_EPS, ADAM_WD, ADAM_STEP = 0.001, 0.9, 0.999, 1e-08, 0.01, 10
VMEM_LIMIT = 48 * 1024 * 1024
VMEM_LIMIT_BIG = 58 * 1024 * 1024
MESH = pl.DeviceIdType.MESH
ANY = pl.BlockSpec(memory_space=pl.ANY)


def _params(sem, vmem=VMEM_LIMIT):
    return pltpu.CompilerParams(dimension_semantics=sem, vmem_limit_bytes=vmem)


def _my_index():
    return 4 * lax.axis_index("x") + 2 * lax.axis_index("y") + lax.axis_index("c")


def _exchange(groups, gather, name):
    flat = [(g, l, a) for g, grp in enumerate(groups) for l, a in enumerate(grp)]
    n = len(flat)
    outs = []
    for grp in groups:
        piece = grp[0].shape if gather else grp[0].shape[1:]
        outs.append(jax.ShapeDtypeStruct((N_DEV, len(grp)) + tuple(piece), grp[0].dtype))

    def body(*refs):
        ins = refs[:n]
        out_refs = refs[n:n + len(groups)]
        send_sems, recv_sems, local_sems = refs[n + len(groups):]
        x, y, c = lax.axis_index("x"), lax.axis_index("y"), lax.axis_index("c")
        me = 4 * x + 2 * y + c
        local = []
        for s, (g, l, _) in enumerate(flat):
            src = ins[s] if gather else ins[s].at[me]
            cp = pltpu.make_async_copy(src, out_refs[g].at[me, l], local_sems.at[s])
            cp.start()
            local.append(cp)
        sent = []
        for k in range(1, N_DEV):
            px = 1 - x if k & 4 else x
            py = 1 - y if k & 2 else y
            pc = 1 - c if k & 1 else c
            peer = 4 * px + 2 * py + pc
            for s, (g, l, _) in enumerate(flat):
                src = ins[s] if gather else ins[s].at[peer]
                cp = pltpu.make_async_remote_copy(
                    src_ref=src, dst_ref=out_refs[g].at[me, l],
                    send_sem=send_sems.at[s, k - 1], recv_sem=recv_sems.at[s, k - 1],
                    device_id=(px, py, pc), device_id_type=MESH)
                cp.start()
                arrive = pltpu.make_async_remote_copy(
                    src_ref=src, dst_ref=out_refs[g].at[peer, l],
                    send_sem=send_sems.at[s, k - 1], recv_sem=recv_sems.at[s, k - 1],
                    device_id=(px, py, pc), device_id_type=MESH)
                sent.append((cp, arrive))
        for _, arrive in sent:
            arrive.wait_recv()
        for cp, _ in sent:
            cp.wait_send()
        for cp in local:
            cp.wait()

    res = pl.pallas_call(
        body, name=name, out_shape=outs,
        in_specs=[ANY] * n, out_specs=[ANY] * len(groups),
        scratch_shapes=[pltpu.SemaphoreType.DMA((n, N_DEV - 1)), pltpu.SemaphoreType.DMA((n, N_DEV - 1)),
                        pltpu.SemaphoreType.DMA((n,))],
    )(*[a for _, _, a in flat])
    return list(res)


_DIMS = {"nn": (((1,), (0,)), ((), ())), "nt": (((1,), (1,)), ((), ())), "tn": (((0,), (0,)), ((), ()))}


def _mm(name, a, b, a_spec, b_spec, o_spec, out_shape, grid, dims):
    nk = grid[2]
    acc_shape = tuple(d for d in o_spec.block_shape if d is not None)
    dn = _DIMS[dims]

    def tile(ref):
        v = ref[...]
        return v.reshape((-1, v.shape[-1])).astype(_ACT)

    def body(a_ref, b_ref, o_ref, *scratch):
        part = lax.dot_general(tile(a_ref), tile(b_ref), dn, preferred_element_type=F32)
        if nk == 1:
            o_ref[...] = part.reshape(o_ref.shape).astype(o_ref.dtype)
            return
        acc_ref, = scratch
        k = pl.program_id(2)

        @pl.when(k == 0)
        def _():
            acc_ref[...] = part

        @pl.when(k > 0)
        def _():
            acc_ref[...] += part

        @pl.when(k == nk - 1)
        def _():
            o_ref[...] = acc_ref[...].reshape(o_ref.shape).astype(o_ref.dtype)

    acc2d = (math.prod(acc_shape[:-1]), acc_shape[-1])
    return pl.pallas_call(
        body, name=name, out_shape=out_shape, grid=grid,
        in_specs=[a_spec, b_spec], out_specs=o_spec,
        scratch_shapes=[] if nk == 1 else [pltpu.VMEM(acc2d, F32)],
        compiler_params=_params(("parallel", "parallel", "arbitrary")),
    )(a, b)


def _row_tile(n, want):
    t = min(n, want)
    assert n % t == 0, (n, t)
    return t


def _colblock_fwd(name, xa, wg, layer, out_dtype):
    L, K = xa.shape
    nb = wg.shape[3]
    half = N_DEV // 2
    tm = _row_tile(L, 512)
    return _mm(name, xa, wg,
               pl.BlockSpec((tm, K), lambda j, i, k: (i, 0)),
               pl.BlockSpec((None, None, K, nb), lambda j, i, k: (j, layer, 0, 0)),
               pl.BlockSpec((None, tm, nb), lambda j, i, k: (j // half, i, j % half)),
               jax.ShapeDtypeStruct((2, L, half * nb), out_dtype), (N_DEV, L // tm, 1), "nn")


def _colblock_dgrad(name, dh, wg, layer, out_dtype):
    _, L, _ = dh.shape
    K, nb = wg.shape[2], wg.shape[3]
    half = N_DEV // 2
    tm = _row_tile(L, 512)
    return _mm(name, dh, wg,
               pl.BlockSpec((None, tm, nb), lambda i, j, k: (k // half, i, k % half)),
               pl.BlockSpec((None, None, K, nb), lambda i, j, k: (k, layer, 0, 0)),
               pl.BlockSpec((tm, K), lambda i, j, k: (i, 0)),
               jax.ShapeDtypeStruct((L, K), out_dtype), (L // tm, 1, N_DEV), "nt")


def _colblock_wgrad(name, xa, dh):
    L, K = xa.shape
    half = N_DEV // 2
    nb = dh.shape[2] // half
    tm = _row_tile(K, 1024)
    tk = _row_tile(L, 512)
    return _mm(name, xa, dh,
               pl.BlockSpec((tk, tm), lambda j, i, k: (k, i)),
               pl.BlockSpec((None, tk, nb), lambda j, i, k: (j // half, k, j % half)),
               pl.BlockSpec((None, tm, nb), lambda j, i, k: (j, i, 0)),
               jax.ShapeDtypeStruct((N_DEV, K, nb), _ACT), (N_DEV, K // tm, L // tk), "tn")


def _rowblock_fwd(name, xa, wg, layer):
    L, FF = xa.shape
    r, D = wg.shape[2], wg.shape[3]
    tm = _row_tile(L, 512)
    return _mm(name, xa, wg,
               pl.BlockSpec((tm, 2 * r), lambda i, j, k: (i, k)),
               pl.BlockSpec((2, None, r, D), lambda i, j, k: (k, layer, 0, 0)),
               pl.BlockSpec((tm, D), lambda i, j, k: (i, 0)),
               jax.ShapeDtypeStruct((L, D), F32), (L // tm, 1, N_DEV // 2), "nn")


def _rowblock_dgrad(name, dy, wg, layer):
    L, D = dy.shape
    r = wg.shape[2]
    tm = _row_tile(L, 512)
    return _mm(name, dy, wg,
               pl.BlockSpec((tm, D), lambda i, j, k: (i, 0)),
               pl.BlockSpec((2, None, r, D), lambda i, j, k: (j, layer, 0, 0)),
               pl.BlockSpec((tm, 2 * r), lambda i, j, k: (i, j)),
               jax.ShapeDtypeStruct((L, N_DEV * r), _ACT), (L // tm, N_DEV // 2, 1), "nt")


def _rowblock_wgrad(name, xa, dy):
    L, FF = xa.shape
    D = dy.shape[1]
    tm = FF // (N_DEV // 2)
    tk = _row_tile(L, 512)
    return _mm(name, xa, dy,
               pl.BlockSpec((tk, tm), lambda i, j, k: (k, i)),
               pl.BlockSpec((tk, D), lambda i, j, k: (k, 0)),
               pl.BlockSpec((tm, D), lambda i, j, k: (i, 0)),
               jax.ShapeDtypeStruct((FF, D), _ACT), (FF // tm, 1, L // tk), "tn")


def _group_mm(name, xa, w, dims, out_dtype):
    L, D = xa.shape
    ng, pc, _ = w.shape
    tm = _row_tile(L, 512)
    return _mm(name, xa, w,
               pl.BlockSpec((tm, pc), lambda i, g, k: (i, g)),
               pl.BlockSpec((None, pc, pc), lambda i, g, k: (g, 0, 0)),
               pl.BlockSpec((tm, pc), lambda i, g, k: (i, g)),
               jax.ShapeDtypeStruct((L, D), out_dtype), (L // tm, ng, 1), dims)


def _group_wgrad(name, p, dy, ng):
    L, D = p.shape
    pc = D // ng
    tk = _row_tile(L, 512)
    return _mm(name, p, dy,
               pl.BlockSpec((tk, pc), lambda g, j, k: (k, g)),
               pl.BlockSpec((tk, pc), lambda g, j, k: (k, g)),
               pl.BlockSpec((None, pc, pc), lambda g, j, k: (g, 0, 0)),
               jax.ShapeDtypeStruct((ng, pc, pc), _ACT), (ng, 1, L // tk), "tn")


V_GPOST, V_GATE, V_YSCALE, V_GPRE, V_SHIFT, V_SCALE = range(6)
R_SHIFT, R_SCALE, R_GPRE, R_GATE, R_GPOST = range(5)
ROW_TILE = 256


def _rstd(v):
    return lax.rsqrt(jnp.mean(v * v, axis=-1, keepdims=True) + RMS_EPS)


def _rows_fwd(name, xres, y, vec, *, add=False, target=None, want_x=True, u_dtype=None):
    L, D = xres.shape
    tm = _row_tile(L, ROW_TILE)
    has_y = y is not None
    last = target is not None
    has_u = u_dtype is not None

    def body(*refs):
        refs = list(refs)
        xres_ref = refs.pop(0)
        y_ref = refs.pop(0) if has_y else None
        vec_ref = refs.pop(0)
        tgt_ref = refs.pop(0) if last else None
        xnew = xres_ref[...]
        if has_y and add:
            xnew = xnew + y_ref[...]
        elif has_y:
            ye = y_ref[...] * vec_ref[V_YSCALE:V_YSCALE + 1, :]
            xnew = xnew + vec_ref[V_GATE:V_GATE + 1, :] * (ye * _rstd(ye) * vec_ref[V_GPOST:V_GPOST + 1, :])
        if last:
            dx_ref, loss_ref = refs
            diff = xnew - tgt_ref[...]
            dx_ref[...] = diff * (1.0 / D)

            @pl.when(pl.program_id(0) == 0)
            def _():
                loss_ref[...] = jnp.zeros_like(loss_ref)

            loss_ref[...] += jnp.sum(diff * diff) * (0.5 / D)
            return
        if want_x:
            refs.pop(0)[...] = xnew
        if has_u:
            u_ref, = refs
            n = xnew * _rstd(xnew) * vec_ref[V_GPRE:V_GPRE + 1, :]
            u_ref[...] = (n * (1.0 + vec_ref[V_SCALE:V_SCALE + 1, :]) + vec_ref[V_SHIFT:V_SHIFT + 1, :]).astype(u_ref.dtype)

    row = pl.BlockSpec((tm, D), lambda i: (i, 0))
    vspec = pl.BlockSpec((8, D), lambda i: (0, 0))
    ins, in_specs = [xres], [row]
    if has_y:
        ins.append(y)
        in_specs.append(row)
    ins.append(vec)
    in_specs.append(vspec)
    out_shape, out_specs = [], []
    if last:
        ins.append(target)
        in_specs.append(row)
        out_shape = [jax.ShapeDtypeStruct((L, D), F32), jax.ShapeDtypeStruct((8, LANE), F32)]
        out_specs = [row, pl.BlockSpec((8, LANE), lambda i: (0, 0))]
    else:
        if want_x:
            out_shape.append(jax.ShapeDtypeStruct((L, D), F32))
            out_specs.append(row)
        if has_u:
            out_shape.append(jax.ShapeDtypeStruct((L, D), u_dtype))
            out_specs.append(row)
    return pl.pallas_call(body, name=name, out_shape=out_shape, grid=(L // tm,), in_specs=in_specs,
                          out_specs=out_specs, compiler_params=_params(("arbitrary",)))(*ins)


def _rows_bwd(name, dxd, du, xnew, y, vec, dy_dtype=F32, want_dx=True):
    L, D = xnew.shape if xnew is not None else dxd.shape
    tm = _row_tile(L, ROW_TILE)
    has_dxd, has_pre, has_post = dxd is not None, du is not None, y is not None

    def body(*refs):
        refs = list(refs)
        dxd_ref = refs.pop(0) if has_dxd else None
        du_ref = refs.pop(0) if has_pre else None
        xnew_ref = refs.pop(0) if has_pre else None
        y_ref = refs.pop(0) if has_post else None
        vec_ref = refs.pop(0)
        dx_ref = refs.pop(0) if want_dx else None
        dy_ref = refs.pop(0) if has_post else None
        red_ref, = refs

        @pl.when(pl.program_id(0) == 0)
        def _():
            red_ref[...] = jnp.zeros_like(red_ref)

        def acc(rw, val):
            red_ref[rw:rw + 1, :] += jnp.sum(val, axis=0, keepdims=True)

        dxn = dxd_ref[...] if has_dxd else None
        if has_pre:
            xn = xnew_ref[...]
            r = _rstd(xn)
            nh = xn * r
            gpre = vec_ref[V_GPRE:V_GPRE + 1, :]
            dub = du_ref[...].astype(F32)
            acc(R_SHIFT, dub)
            acc(R_SCALE, dub * (nh * gpre))
            drn = dub * (1.0 + vec_ref[V_SCALE:V_SCALE + 1, :])
            acc(R_GPRE, drn * nh)
            dnh = drn * gpre
            t = r * (dnh - nh * jnp.mean(dnh * nh, axis=-1, keepdims=True))
            dxn = t if dxn is None else dxn + t
        if want_dx:
            dx_ref[...] = dxn
        if has_post:
            ye = y_ref[...] * vec_ref[V_YSCALE:V_YSCALE + 1, :]
            ry = _rstd(ye)
            yh = ye * ry
            gpost = vec_ref[V_GPOST:V_GPOST + 1, :]
            acc(R_GATE, dxn * (yh * gpost))
            drn2 = dxn * vec_ref[V_GATE:V_GATE + 1, :]
            acc(R_GPOST, drn2 * yh)
            dyh = drn2 * gpost
            dy_ref[...] = (ry * (dyh - yh * jnp.mean(dyh * yh, axis=-1, keepdims=True))).astype(dy_ref.dtype)

    row = pl.BlockSpec((tm, D), lambda i: (i, 0))
    vspec = pl.BlockSpec((8, D), lambda i: (0, 0))
    ins, in_specs = [], []
    for a in ([dxd] if has_dxd else []) + ([du, xnew] if has_pre else []) + ([y] if has_post else []):
        ins.append(a)
        in_specs.append(row)
    ins.append(vec)
    in_specs.append(vspec)
    out_shape, out_specs = [], []
    if want_dx:
        out_shape.append(jax.ShapeDtypeStruct((L, D), F32))
        out_specs.append(row)
    if has_post:
        out_shape.append(jax.ShapeDtypeStruct((L, D), dy_dtype))
        out_specs.append(row)
    out_shape.append(jax.ShapeDtypeStruct((8, D), F32))
    out_specs.append(vspec)
    return pl.pallas_call(body, name=name, out_shape=out_shape, grid=(L // tm,), in_specs=in_specs,
                          out_specs=out_specs, compiler_params=_params(("arbitrary",)))(*ins)


def _colscale_bwd(name, dy, ypre, scale):
    L, D = dy.shape
    tm = _row_tile(L, ROW_TILE)

    def body(dy_ref, yp_ref, s_ref, o_ref, red_ref):
        @pl.when(pl.program_id(0) == 0)
        def _():
            red_ref[...] = jnp.zeros_like(red_ref)

        d = dy_ref[...]
        o_ref[...] = (d * s_ref[0:1, :]).astype(o_ref.dtype)
        red_ref[0:1, :] += jnp.sum(d * yp_ref[...], axis=0, keepdims=True)

    row = pl.BlockSpec((tm, D), lambda i: (i, 0))
    vspec = pl.BlockSpec((8, D), lambda i: (0, 0))
    return pl.pallas_call(body, name=name, grid=(L // tm,), in_specs=[row, row, vspec], out_specs=[row, vspec],
                          out_shape=[jax.ShapeDtypeStruct((L, D), _ACT), jax.ShapeDtypeStruct((8, D), F32)],
                          compiler_params=_params(("arbitrary",)))(dy, ypre, scale)


def _sigmoid(v):
    return 1.0 / (1.0 + jnp.exp(-v))


def _glu_fwd(name, vg):
    _, L, D = vg.shape
    tm = _row_tile(L, ROW_TILE)

    def body(vg_ref, o_ref):
        o_ref[...] = vg_ref[0].astype(F32) * _sigmoid(vg_ref[1].astype(F32))

    return pl.pallas_call(body, name=name, grid=(L // tm,),
                          in_specs=[pl.BlockSpec((2, tm, D), lambda i: (0, i, 0))],
                          out_specs=pl.BlockSpec((tm, D), lambda i: (i, 0)),
                          out_shape=jax.ShapeDtypeStruct((L, D), F32),
                          compiler_params=_params(("parallel",)))(vg)


def _glu_bwd(name, vg, dout):
    _, L, D = vg.shape
    tm = _row_tile(L, ROW_TILE)

    def body(vg_ref, d_ref, o_ref):
        val, s = vg_ref[0].astype(F32), _sigmoid(vg_ref[1].astype(F32))
        d = d_ref[...]
        o_ref[0] = (d * s).astype(o_ref.dtype)
        o_ref[1] = (d * val * s * (1.0 - s)).astype(o_ref.dtype)

    return pl.pallas_call(body, name=name, grid=(L // tm,),
                          in_specs=[pl.BlockSpec((2, tm, D), lambda i: (0, i, 0)), pl.BlockSpec((tm, D), lambda i: (i, 0))],
                          out_specs=pl.BlockSpec((2, tm, D), lambda i: (0, i, 0)),
                          out_shape=jax.ShapeDtypeStruct((2, L, D), _ACT),
                          compiler_params=_params(("parallel",)))(vg, dout)


CONV_ROWS = 256


def _row_pick(blk, idx):
    rows = lax.broadcasted_iota(jnp.int32, blk.shape, 0)
    return jnp.sum(jnp.where(rows == idx, blk, 0.0), axis=0, keepdims=True)


def _shifted(ref, r0, rc, L):
    cur = ref[pl.ds(r0, rc), :].astype(F32)
    before = ref[pl.ds(pl.multiple_of(jnp.maximum(r0 - 16, 0), 16), 16), :].astype(F32)
    after = ref[pl.ds(pl.multiple_of(jnp.minimum(r0 + rc, L - 16), 16), 16), :].astype(F32)
    prev_row = jnp.where(r0 > 0, _row_pick(before, 15), 0.0)
    next_row = jnp.where(r0 + rc < L, _row_pick(after, 0), 0.0)
    rows = lax.broadcasted_iota(jnp.int32, cur.shape, 0)
    up = jnp.where(rows == 0, prev_row, pltpu.roll(cur, 1, 0))
    down = jnp.where(rows == rc - 1, next_row, pltpu.roll(cur, rc - 1, 0))
    return up, cur, down


def _silu_parts(g):
    s = _sigmoid(g)
    return g * s, s


def _conv_swiglu_fwd(name, h, cw):
    _, L, FF = h.shape
    rc = _row_tile(L, CONV_ROWS)

    def body(h_ref, cw_ref, o_ref):
        def chunk(ci, _):
            r0 = pl.multiple_of(ci * rc, rc)
            hc = []
            for half in range(2):
                up, cur, down = _shifted(h_ref.at[half], r0, rc, L)
                hc.append(up * cw_ref[half, 0:1, :] + cur * cw_ref[half, 1:2, :] + down * cw_ref[half, 2:3, :]
                          + cw_ref[half, 3:4, :])
            o_ref[pl.ds(r0, rc), :] = (_silu_parts(hc[1])[0] * hc[0]).astype(o_ref.dtype)
            return 0

        lax.fori_loop(0, L // rc, chunk, 0)

    return pl.pallas_call(body, name=name, grid=(FF // LANE,),
                          in_specs=[pl.BlockSpec((2, L, LANE), lambda j: (0, 0, j)),
                                    pl.BlockSpec((2, 8, LANE), lambda j: (0, 0, j))],
                          out_specs=pl.BlockSpec((L, LANE), lambda j: (0, j)),
                          out_shape=jax.ShapeDtypeStruct((L, FF), _ACT),
                          compiler_params=_params(("parallel",)))(h, cw)


def _conv_swiglu_bwd(name, h, cw, dact):
    _, L, FF = h.shape
    rc = _row_tile(L, CONV_ROWS)

    def body(h_ref, cw_ref, da_ref, dh_ref, dcw_ref, dhc_ref):
        def chunk(ci, acc):
            r0 = pl.multiple_of(ci * rc, rc)
            taps, hc = [], []
            for half in range(2):
                t = _shifted(h_ref.at[half], r0, rc, L)
                taps.append(t)
                hc.append(t[0] * cw_ref[half, 0:1, :] + t[1] * cw_ref[half, 1:2, :] + t[2] * cw_ref[half, 2:3, :]
                          + cw_ref[half, 3:4, :])
            d = da_ref[pl.ds(r0, rc), :].astype(F32)
            act, s = _silu_parts(hc[1])
            dhc = (d * act, d * hc[0] * (s + act * (1.0 - s)))
            new = []
            for half in range(2):
                dhc_ref[half, pl.ds(r0, rc), :] = dhc[half]
                for k in range(3):
                    new.append(acc[4 * half + k] + jnp.sum(dhc[half] * taps[half][k], axis=0, keepdims=True))
                new.append(acc[4 * half + 3] + jnp.sum(dhc[half], axis=0, keepdims=True))
            return tuple(new)

        zero = jnp.zeros((1, LANE), F32)
        acc = lax.fori_loop(0, L // rc, chunk, (zero,) * 8)
        dcw_ref[...] = jnp.zeros_like(dcw_ref)
        for half in range(2):
            for k in range(4):
                dcw_ref[half, k:k + 1, :] = acc[4 * half + k]

        def chunk2(ci, _):
            r0 = pl.multiple_of(ci * rc, rc)
            for half in range(2):
                up, cur, down = _shifted(dhc_ref.at[half], r0, rc, L)
                dh_ref[half, pl.ds(r0, rc), :] = (down * cw_ref[half, 0:1, :] + cur * cw_ref[half, 1:2, :]
                                                  + up * cw_ref[half, 2:3, :]).astype(dh_ref.dtype)
            return 0

        lax.fori_loop(0, L // rc, chunk2, 0)

    return pl.pallas_call(body, name=name, grid=(FF // LANE,),
                          in_specs=[pl.BlockSpec((2, L, LANE), lambda j: (0, 0, j)),
                                    pl.BlockSpec((2, 8, LANE), lambda j: (0, 0, j)),
                                    pl.BlockSpec((L, LANE), lambda j: (0, j))],
                          out_specs=[pl.BlockSpec((2, L, LANE), lambda j: (0, 0, j)),
                                     pl.BlockSpec((2, 8, LANE), lambda j: (0, 0, j))],
                          out_shape=[jax.ShapeDtypeStruct((2, L, FF), _ACT), jax.ShapeDtypeStruct((2, 8, FF), F32)],
                          scratch_shapes=[pltpu.VMEM((2, L, LANE), F32)],
                          compiler_params=_params(("parallel",)))(h, cw, dact)


POOL_ROWS = 256
POOL_TILE = 256


def _pool_bands(transpose):
    i = jnp.arange(POOL_ROWS)[:, None]
    j = jnp.arange(POOL_ROWS + 2 * POOL_HALO)[None, :] - POOL_HALO
    bands = []
    for w in POOL_WINDOWS:
        lo, hi = (-(w // 2 - 1), w // 2) if transpose else (-(w // 2), w // 2 - 1)
        bands.append(((j - i >= lo) & (j - i <= hi)).astype(_ACT))
    return jnp.stack(bands)


def _pool_window(name, u, transpose, out_dtype):
    L, D = u.shape
    ng = len(POOL_WINDOWS)
    pc = D // ng
    tn = min(POOL_TILE, pc)
    rc = _row_tile(L, POOL_ROWS)
    bands = _pool_bands(transpose)
    if rc != POOL_ROWS:
        bands = bands[:, :rc, :rc + 2 * POOL_HALO]
    halo = POOL_HALO

    def body(u_ref, band_ref, o_ref, hi_ref, lo_ref):
        g = (pl.program_id(0) * tn) // pc
        half = jnp.zeros((1, 1), jnp.int32)
        for k, w in enumerate(POOL_WINDOWS):
            half = jnp.where(g == k, w // 2, half)
        zeros = jnp.zeros((halo, tn), _ACT)
        for ref in (hi_ref, lo_ref):
            ref[0:halo, :] = zeros
            ref[halo + L:2 * halo + L, :] = zeros

        def inv_count(r0):
            t = r0 + lax.broadcasted_iota(jnp.int32, (rc, tn), 0)
            lo = jnp.clip(t - half, 0, L - 1)
            hi = jnp.clip(t + half - 1, 0, L - 1)
            return 1.0 / (hi - lo + 1).astype(F32)

        def split(ci, _):
            r0 = pl.multiple_of(ci * rc, rc)
            v = u_ref[pl.ds(r0, rc), :].astype(F32)
            if transpose:
                v = v * inv_count(r0)
            hi = v.astype(_ACT)
            dst = pl.ds(pl.multiple_of(r0 + halo, halo), rc)
            hi_ref[dst, :] = hi
            lo_ref[dst, :] = (v - hi.astype(F32)).astype(_ACT)
            return 0

        lax.fori_loop(0, L // rc, split, 0)
        band = band_ref[...]

        def chunk(ci, _):
            r0 = pl.multiple_of(ci * rc, rc)
            win = pl.ds(r0, rc + 2 * halo)
            s = (jnp.dot(band, hi_ref[win, :], preferred_element_type=F32)
                 + jnp.dot(band, lo_ref[win, :], preferred_element_type=F32))
            if not transpose:
                s = s * inv_count(r0)
            o_ref[pl.ds(r0, rc), :] = (s - u_ref[pl.ds(r0, rc), :].astype(F32)).astype(o_ref.dtype)
            return 0

        lax.fori_loop(0, L // rc, chunk, 0)

    return pl.pallas_call(body, name=name, grid=(D // tn,),
                          in_specs=[pl.BlockSpec((L, tn), lambda j: (0, j)),
                                    pl.BlockSpec((None, rc, rc + 2 * halo), lambda j: ((j * tn) // pc, 0, 0))],
                          out_specs=pl.BlockSpec((L, tn), lambda j: (0, j)),
                          out_shape=jax.ShapeDtypeStruct((L, D), out_dtype),
                          scratch_shapes=[pltpu.VMEM((L + 2 * halo, tn), _ACT), pltpu.VMEM((L + 2 * halo, tn), _ACT)],
                          compiler_params=_params(("parallel",)))(u, bands)


S5_ROWS = 512


def _slab(start):
    return pl.ds(start if isinstance(start, int) else pl.multiple_of(start, NSEG), NSEG)


def _cmul(ar, ai, br, bi):
    return ar * br - ai * bi, ar * bi + ai * br


def _cpow(ar, ai, n):
    rr, ri = None, None
    br, bi = ar, ai
    while n:
        if n & 1:
            rr, ri = (br, bi) if rr is None else _cmul(rr, ri, br, bi)
        n >>= 1
        if n:
            br, bi = _cmul(br, bi, br, bi)
    return rr, ri


def _seg_scan(sr_ref, si_ref, tmp_ref, row0, n, ar, ai, h0, rev):
    W = ar.shape[1]
    arb, aib = jnp.broadcast_to(ar, (NSEG, W)), jnp.broadcast_to(ai, (NSEG, W))

    def rows(s):
        t = (n - 1 - s) if rev else s
        return _slab(row0 + t * NSEG)

    def step(s, carry):
        hr, hi = carry
        sl = rows(s)
        nr = arb * hr - aib * hi + sr_ref[sl, :]
        ni = arb * hi + aib * hr + si_ref[sl, :]
        sr_ref[sl, :] = nr
        si_ref[sl, :] = ni
        return nr, ni

    zero = jnp.zeros((NSEG, W), F32)
    fr, fi = lax.fori_loop(0, n, step, (zero, zero), unroll=2)
    tmp_ref[0] = fr
    tmp_ref[1] = fi
    anr, ani = _cpow(ar, ai, n)
    cr, ci = h0
    for j in (range(NSEG - 1, -1, -1) if rev else range(NSEG)):
        tmp_ref[2, j:j + 1, :] = cr
        tmp_ref[3, j:j + 1, :] = ci
        pr, pi = _cmul(anr, ani, cr, ci)
        cr, ci = tmp_ref[0, j:j + 1, :] + pr, tmp_ref[1, j:j + 1, :] + pi
    cmr, cmi = tmp_ref[2], tmp_ref[3]

    def fix(s, carry):
        pr, pi = carry
        sl = rows(s)
        sr_ref[sl, :] += pr * cmr - pi * cmi
        si_ref[sl, :] += pr * cmi + pi * cmr
        return pr * arb - pi * aib, pr * aib + pi * arb

    lax.fori_loop(0, n, fix, (arb, aib), unroll=2)
    return (cr, ci), (cmr, cmi)


def _gelu_tanh(y):
    k = math.sqrt(2.0 / math.pi)
    t = jnp.tanh(k * (y + 0.044715 * y * y * y))
    return 0.5 * y * (1.0 + t), t


def _s5_chunks(L):
    rc = _row_tile(L, S5_ROWS)
    return [(r, rc) for r in range(0, L, rc)]


def _s5_project(u_ref, uc_ref, bre, bim, sr_ref, si_ref, L, LC):
    for ref, base, n in ((u_ref, 0, L), (uc_ref, L, LC)):
        for r, rc in _s5_chunks(n):
            ub = ref[r:r + rc, :].astype(_ACT)
            sr_ref[base + r:base + r + rc, :] = jnp.dot(ub, bre, preferred_element_type=F32)
            si_ref[base + r:base + r + rc, :] = jnp.dot(ub, bim, preferred_element_type=F32)


def _s5_states(sr_ref, si_ref, tmp_ref, ar, ai, L, LC, rev):
    W = ar.shape[1]
    zero = (jnp.zeros((1, W), F32), jnp.zeros((1, W), F32))
    hctx, cm_ctx = _seg_scan(sr_ref, si_ref, tmp_ref, L, LC // NSEG, ar, ai, zero, rev)
    _, cm_lat = _seg_scan(sr_ref, si_ref, tmp_ref, 0, L // NSEG, ar, ai, hctx, rev)
    return cm_lat, cm_ctx


def _s5_fwd(u, uc, bblk, cblk, apar, dsk):
    L, D = u.shape
    LC = uc.shape[0]
    NT, W, TC = D // S5_TILE_CH, S5_TILE_W, S5_TILE_CH

    def body(u_ref, uc_ref, b_ref, c_ref, a_ref, d_ref, y_ref, z_ref, sr_ref, si_ref, tmp_ref):
        for r, rc in _s5_chunks(L):
            y_ref[r:r + rc, :] = u_ref[r:r + rc, :].astype(F32) * d_ref[0:1, :]
        for d in range(2):
            ar, ai = a_ref[2 * d:2 * d + 1, :], a_ref[2 * d + 1:2 * d + 2, :]
            _s5_project(u_ref, uc_ref, b_ref[2 * d], b_ref[2 * d + 1], sr_ref, si_ref, L, LC)
            _s5_states(sr_ref, si_ref, tmp_ref, ar, ai, L, LC, rev=(d == 1))
            cre, cim = c_ref[2 * d], c_ref[2 * d + 1]
            for r, rc in _s5_chunks(L):
                y_ref[r:r + rc, :] += (jnp.dot(sr_ref[r:r + rc, :].astype(_ACT), cre, preferred_element_type=F32)
                                       - jnp.dot(si_ref[r:r + rc, :].astype(_ACT), cim, preferred_element_type=F32))
        for r, rc in _s5_chunks(L):
            z_ref[r:r + rc, :] = _gelu_tanh(y_ref[r:r + rc, :])[0].astype(z_ref.dtype)

    col = lambda n: pl.BlockSpec((n, TC), lambda j: (0, j))
    return pl.pallas_call(
        body, name="s5_fwd", grid=(NT,),
        in_specs=[col(L), col(LC),
                  pl.BlockSpec((None, 4, TC, W), lambda j: (j, 0, 0, 0)),
                  pl.BlockSpec((None, 4, W, TC), lambda j: (j, 0, 0, 0)),
                  pl.BlockSpec((None, 8, W), lambda j: (j, 0, 0)),
                  pl.BlockSpec((8, TC), lambda j: (0, j))],
        out_specs=[col(L), col(L)],
        out_shape=[jax.ShapeDtypeStruct((L, D), F32), jax.ShapeDtypeStruct((L, D), _ACT)],
        scratch_shapes=[pltpu.VMEM((L + LC, W), F32), pltpu.VMEM((L + LC, W), F32), pltpu.VMEM((4, NSEG, W), F32)],
        compiler_params=_params(("parallel",)))(u, uc, bblk, cblk, apar, dsk)


def _s5_bwd(u, uc, dz, y, bblk, cblk, apar, dsk):
    L, D = u.shape
    LC = uc.shape[0]
    NT, W, TC = D // S5_TILE_CH, S5_TILE_W, S5_TILE_CH
    nl, nc = L // NSEG, LC // NSEG

    def body(u_ref, uc_ref, dz_ref, y_ref, b_ref, c_ref, a_ref, d_ref,
             du_ref, duc_ref, db_ref, dc_ref, da_ref, dd_ref,
             hr_ref, hi_ref, gr_ref, gi_ref, dy_ref, tmp_ref):
        ddacc = jnp.zeros((1, TC), F32)
        for r, rc in _s5_chunks(L):
            yv = y_ref[r:r + rc, :]
            g, t = _gelu_tanh(yv)
            k = math.sqrt(2.0 / math.pi)
            dg = 0.5 * (1.0 + t) + 0.5 * yv * (1.0 - t * t) * k * (1.0 + 3 * 0.044715 * yv * yv)
            dy = dz_ref[r:r + rc, :].astype(F32) * dg
            uv = u_ref[r:r + rc, :].astype(F32)
            ddacc = ddacc + jnp.sum(dy * uv, axis=0, keepdims=True)
            du_ref[r:r + rc, :] = dy * d_ref[0:1, :]
            dy_ref[r:r + rc, :] = dy.astype(dy_ref.dtype)
        dd_ref[...] = jnp.zeros_like(dd_ref)
        dd_ref[0:1, :] = ddacc
        duc_ref[...] = jnp.zeros_like(duc_ref)
        da_ref[...] = jnp.zeros_like(da_ref)
        nt = (((1,), (1,)), ((), ()))
        tn = (((0,), (0,)), ((), ()))
        for d in range(2):
            rev = d == 1
            ar, ai = a_ref[2 * d:2 * d + 1, :], a_ref[2 * d + 1:2 * d + 2, :]
            bre, bim = b_ref[2 * d], b_ref[2 * d + 1]
            cre, cim = c_ref[2 * d], c_ref[2 * d + 1]
            _s5_project(u_ref, uc_ref, bre, bim, hr_ref, hi_ref, L, LC)
            cm_lat, cm_ctx = _s5_states(hr_ref, hi_ref, tmp_ref, ar, ai, L, LC, rev)
            cml_r, cml_i, cmc_r, cmc_i = cm_lat[0], cm_lat[1], cm_ctx[0], cm_ctx[1]
            dcr = jnp.zeros((W, TC), F32)
            dci = jnp.zeros((W, TC), F32)
            for r, rc in _s5_chunks(L):
                dyb = dy_ref[r:r + rc, :]
                gr_ref[r:r + rc, :] = lax.dot_general(dyb, cre, nt, preferred_element_type=F32)
                gi_ref[r:r + rc, :] = -lax.dot_general(dyb, cim, nt, preferred_element_type=F32)
                dcr = dcr + lax.dot_general(hr_ref[r:r + rc, :].astype(_ACT), dyb, tn, preferred_element_type=F32)
                dci = dci - lax.dot_general(hi_ref[r:r + rc, :].astype(_ACT), dyb, tn, preferred_element_type=F32)
            dc_ref[2 * d] = dcr
            dc_ref[2 * d + 1] = dci
            gr_ref[L:L + LC, :] = jnp.zeros((LC, W), F32)
            gi_ref[L:L + LC, :] = jnp.zeros((LC, W), F32)
            zero = (jnp.zeros((1, W), F32), jnp.zeros((1, W), F32))
            glat, _ = _seg_scan(gr_ref, gi_ref, tmp_ref, 0, nl, ar, -ai, zero, not rev)
            _seg_scan(gr_ref, gi_ref, tmp_ref, L, nc, ar, -ai, glat, not rev)

            def da_part(row0, n, cmr, cmi):
                def rows(s):
                    t = (n - 1 - s) if rev else s
                    return _slab(row0 + t * NSEG)

                g0r, g0i = gr_ref[rows(0), :], gi_ref[rows(0), :]
                acc0 = (cmr * g0r + cmi * g0i, cmr * g0i - cmi * g0r)

                def step(s, acc):
                    hpr, hpi = hr_ref[rows(s - 1), :], hi_ref[rows(s - 1), :]
                    gr, gi = gr_ref[rows(s), :], gi_ref[rows(s), :]
                    return acc[0] + hpr * gr + hpi * gi, acc[1] + hpr * gi - hpi * gr

                return lax.fori_loop(1, n, step, acc0, unroll=2)

            lr, li = da_part(0, nl, cml_r, cml_i)
            qr, qi = da_part(L, nc, cmc_r, cmc_i)
            da_ref[2 * d:2 * d + 1, :] = jnp.sum(lr + qr, axis=0, keepdims=True)
            da_ref[2 * d + 1:2 * d + 2, :] = jnp.sum(li + qi, axis=0, keepdims=True)
            dbr = jnp.zeros((TC, W), F32)
            dbi = jnp.zeros((TC, W), F32)
            for ref, oref, base, n in ((u_ref, du_ref, 0, L), (uc_ref, duc_ref, L, LC)):
                for r, rc in _s5_chunks(n):
                    ub = ref[r:r + rc, :].astype(_ACT)
                    gr = gr_ref[base + r:base + r + rc, :].astype(_ACT)
                    gi = gi_ref[base + r:base + r + rc, :].astype(_ACT)
                    dbr = dbr + lax.dot_general(ub, gr, tn, preferred_element_type=F32)
                    dbi = dbi + lax.dot_general(ub, gi, tn, preferred_element_type=F32)
                    oref[r:r + rc, :] += (lax.dot_general(gr, bre, nt, preferred_element_type=F32)
                                          + lax.dot_general(gi, bim, nt, preferred_element_type=F32))
            db_ref[2 * d] = dbr
            db_ref[2 * d + 1] = dbi

    col = lambda n: pl.BlockSpec((n, TC), lambda j: (0, j))
    bspec = pl.BlockSpec((None, 4, TC, W), lambda j: (j, 0, 0, 0))
    cspec = pl.BlockSpec((None, 4, W, TC), lambda j: (j, 0, 0, 0))
    aspec = pl.BlockSpec((None, 8, W), lambda j: (j, 0, 0))
    return pl.pallas_call(
        body, name="s5_bwd", grid=(NT,),
        in_specs=[col(L), col(LC), col(L), col(L), bspec, cspec, aspec, pl.BlockSpec((8, TC), lambda j: (0, j))],
        out_specs=[col(L), col(LC), bspec, cspec, aspec, pl.BlockSpec((None, 8, TC), lambda j: (j, 0, 0))],
        out_shape=[jax.ShapeDtypeStruct((L, D), F32), jax.ShapeDtypeStruct((LC, D), F32),
                   jax.ShapeDtypeStruct((NT, 4, TC, W), F32), jax.ShapeDtypeStruct((NT, 4, W, TC), F32),
                   jax.ShapeDtypeStruct((NT, 8, W), F32), jax.ShapeDtypeStruct((NT, 8, TC), F32)],
        scratch_shapes=[pltpu.VMEM((L + LC, W), F32), pltpu.VMEM((L + LC, W), F32),
                        pltpu.VMEM((L + LC, W), F32), pltpu.VMEM((L + LC, W), F32),
                        pltpu.VMEM((L, TC), _ACT), pltpu.VMEM((4, NSEG, W), F32)],
        compiler_params=_params(("parallel",), VMEM_LIMIT_BIG))(u, uc, dz, y, bblk, cblk, apar, dsk)


ADA_ROWS = 16


def _silu_rows(c_ref):
    c = c_ref[...]
    return c * _sigmoid(c)


def _ada_fwd(cmat, ada_w, ada_b):
    nl, D, n = ada_w.shape
    tn = _row_tile(n, 512)

    def body(c_ref, w_ref, b_ref, o_ref):
        a = _silu_rows(c_ref).astype(_ACT)
        o_ref[...] = jnp.dot(a, w_ref[...].astype(_ACT), preferred_element_type=F32) + b_ref[...]

    return pl.pallas_call(body, name="ada_fwd", grid=(nl, n // tn),
                          in_specs=[pl.BlockSpec((ADA_ROWS, D), lambda l, j: (0, 0)),
                                    pl.BlockSpec((None, D, tn), lambda l, j: (l, 0, j)),
                                    pl.BlockSpec((None, 1, tn), lambda l, j: (l, 0, j))],
                          out_specs=pl.BlockSpec((None, ADA_ROWS, tn), lambda l, j: (l, 0, j)),
                          out_shape=jax.ShapeDtypeStruct((nl, ADA_ROWS, n), F32),
                          compiler_params=_params(("parallel", "parallel")))(cmat, ada_w, ada_b)


def _ada_bwd(cmat, ada_w, dm):
    nl, D, n = ada_w.shape
    tn = _row_tile(n, 512)
    nj = n // tn

    def body(c_ref, w_ref, dm_ref, dw_ref, dc_ref):
        c = c_ref[...]
        s = _sigmoid(c)
        a = (c * s).astype(_ACT)
        dmb = dm_ref[...].astype(_ACT)
        dw_ref[...] = lax.dot_general(a, dmb, (((0,), (0,)), ((), ())), preferred_element_type=F32)
        part = lax.dot_general(dmb, w_ref[...].astype(_ACT), (((1,), (1,)), ((), ())), preferred_element_type=F32)
        part = part * (s * (1.0 + c * (1.0 - s)))

        @pl.when(pl.program_id(1) == 0)
        def _():
            dc_ref[...] = part

        @pl.when(pl.program_id(1) > 0)
        def _():
            dc_ref[...] += part

    return pl.pallas_call(body, name="ada_bwd", grid=(nl, nj),
                          in_specs=[pl.BlockSpec((ADA_ROWS, D), lambda l, j: (0, 0)),
                                    pl.BlockSpec((None, D, tn), lambda l, j: (l, 0, j)),
                                    pl.BlockSpec((None, ADA_ROWS, tn), lambda l, j: (l, 0, j))],
                          out_specs=[pl.BlockSpec((None, D, tn), lambda l, j: (l, 0, j)),
                                     pl.BlockSpec((None, ADA_ROWS, D), lambda l, j: (l, 0, 0))],
                          out_shape=[jax.ShapeDtypeStruct((nl, D, n), F32), jax.ShapeDtypeStruct((nl, ADA_ROWS, D), F32)],
                          compiler_params=_params(("parallel", "arbitrary")))(cmat, ada_w, dm)


def _adamw(name, gparts, w, m, v):
    n, R, C = gparts.shape
    tr = R
    for cand in (512, 256, 128, 64, 32, 16, 8):
        if R % cand == 0 and cand * C * 4 <= 2 * 1024 * 1024:
            tr = cand
            break
    bc1 = 1.0 - ADAM_B1 ** ADAM_STEP
    bc2 = 1.0 - ADAM_B2 ** ADAM_STEP

    def body(g_ref, w_ref, m_ref, v_ref, go_ref, d_ref, mo_ref, vo_ref):
        g = g_ref[0].astype(F32)
        for j in range(1, n):
            g = g + g_ref[j].astype(F32)
        m2 = ADAM_B1 * m_ref[...] + (1.0 - ADAM_B1) * g
        v2 = ADAM_B2 * v_ref[...] + (1.0 - ADAM_B2) * (g * g)
        go_ref[...] = g
        mo_ref[...] = m2
        vo_ref[...] = v2
        d_ref[...] = -ADAM_LR * ((m2 / bc1) / (jnp.sqrt(v2 / bc2) + ADAM_EPS) + ADAM_WD * w_ref[...])

    row = pl.BlockSpec((tr, C), lambda i: (i, 0))
    out = jax.ShapeDtypeStruct((R, C), F32)
    return pl.pallas_call(body, name=name, grid=(R // tr,),
                          in_specs=[pl.BlockSpec((n, tr, C), lambda i: (0, i, 0)), row, row, row],
                          out_specs=[row, row, row, row], out_shape=[out, out, out, out],
                          compiler_params=_params(("parallel",)))(gparts, w, m, v)


def _sum_parts(name, parts):
    n, R, C = parts.shape

    def body(p_ref, o_ref):
        s = p_ref[0]
        for j in range(1, n):
            s = s + p_ref[j]
        o_ref[...] = s

    return pl.pallas_call(body, name=name, out_shape=jax.ShapeDtypeStruct((R, C), F32),
                          compiler_params=_params(None))(parts)


def _discretize(lam_re, lam_im, log_step, b_re, b_im):
    dt = jnp.exp(log_step)[:, None]
    mag = jnp.exp(lam_re * dt)
    abar_re = mag * jnp.cos(lam_im * dt)
    abar_im = mag * jnp.sin(lam_im * dt)
    nr, ni = abar_re - 1.0, abar_im
    den = lam_re * lam_re + lam_im * lam_im
    fr = (nr * lam_re + ni * lam_im) / den
    fi = (ni * lam_re - nr * lam_im) / den
    bbar_re = fr[..., None] * b_re - fi[..., None] * b_im
    bbar_im = fr[..., None] * b_im + fi[..., None] * b_re
    return abar_re, abar_im, bbar_re, bbar_im


def _s5_pack(abar, bbar, cmat):
    G = abar[0][0].shape[0]
    NT = G // S5_TILE_G
    eye = jnp.eye(S5_TILE_G, dtype=F32)
    rows, bs, cs = [], [], []
    for d in range(2):
        for r in range(2):
            rows.append(abar[d][r].reshape(NT, 1, S5_TILE_W))
            bb = bbar[d][r].reshape(NT, S5_TILE_G, S5_P, S5_CH)
            bs.append(jnp.einsum("jgpc,gh->jgchp", bb, eye).reshape(NT, S5_TILE_CH, S5_TILE_W))
            cc = cmat[d][r].reshape(NT, S5_TILE_G, S5_CH, S5_P)
            cs.append(jnp.einsum("jgcp,gh->jgphc", cc, eye).reshape(NT, S5_TILE_W, S5_TILE_CH))
    apar = jnp.concatenate(rows + [jnp.zeros((NT, 4, S5_TILE_W), F32)], axis=1)
    return apar, jnp.stack(bs, axis=1).astype(_ACT), jnp.stack(cs, axis=1).astype(_ACT)


def _s5_unpack(dapar, dbblk, dcblk, G):
    NT = G // S5_TILE_G
    da = dapar[:, :4, :].reshape(NT, 2, 2, S5_TILE_G, S5_P).transpose(1, 2, 0, 3, 4).reshape(2, 2, G, S5_P)
    idx = jnp.arange(S5_TILE_G)
    db = dbblk.reshape(NT, 2, 2, S5_TILE_G, S5_CH, S5_TILE_G, S5_P)[:, :, :, idx, :, idx, :]
    db = db.transpose(2, 3, 1, 0, 5, 4).reshape(2, 2, G, S5_P, S5_CH)
    dc = dcblk.reshape(NT, 2, 2, S5_TILE_G, S5_P, S5_TILE_G, S5_CH)[:, :, :, idx, :, idx, :]
    dc = dc.transpose(2, 3, 1, 0, 5, 4).reshape(2, 2, G, S5_CH, S5_P)
    return da, db, dc


def _to_segments(a):
    L, D = a.shape
    return a.reshape(NSEG, L // NSEG, D).transpose(1, 0, 2).reshape(L, D)


def _from_segments(a):
    L, D = a.shape
    return a.reshape(L // NSEG, NSEG, D).transpose(1, 0, 2).reshape(L, D)


def _pos_emb(n_tokens, dim):
    rows = n_tokens // GRID_W
    quarter = dim // 4
    omega = 1.0 / (POS_BASE ** (jnp.arange(quarter, dtype=F32) / quarter))

    def enc(p):
        ang = p[:, None] * omega[None, :]
        return jnp.concatenate([jnp.sin(ang), jnp.cos(ang)], axis=-1)

    rtab = enc(jnp.arange(rows, dtype=F32))
    ctab = enc(jnp.arange(GRID_W, dtype=F32))
    return jnp.concatenate([jnp.repeat(rtab, GRID_W, axis=0), jnp.tile(ctab, (rows, 1))], axis=-1)


def _vec(D, **rows):
    names = {"gpost": V_GPOST, "gate": V_GATE, "yscale": V_YSCALE, "gpre": V_GPRE, "shift": V_SHIFT, "scale": V_SCALE}
    out = [jnp.zeros((D,), F32)] * 8
    out[V_YSCALE] = jnp.ones((D,), F32)
    for k, v in rows.items():
        out[names[k]] = v.reshape(D).astype(F32)
    return jnp.stack(out)


def _row0(v, D):
    return jnp.concatenate([v.reshape(1, D).astype(F32), jnp.zeros((7, D), F32)], axis=0)


def _my_block(full, axis, n_local):
    return lax.dynamic_slice_in_dim(full, _my_index() * n_local, n_local, axis)


def kernel(x, c, ctx, c_ctx, ada_w, ada_b, norm_g, s5_lam_re, s5_lam_im, s5_log_step, s5_b_re, s5_b_im, s5_c_re, s5_c_im, s5_d, s5_glu_w, pool_w, pool_scale, ffn_up, ffn_conv, ffn_conv_b, ffn_down, loss_target, m_c_ctx, m_ada_w, m_ada_b, m_norm_g, m_s5_lam_re, m_s5_lam_im, m_s5_log_step, m_s5_b_re, m_s5_b_im, m_s5_c_re, m_s5_c_im, m_s5_d, m_s5_glu_w, m_pool_w, m_pool_scale, m_ffn_up, m_ffn_conv, m_ffn_conv_b, m_ffn_down, v_c_ctx, v_ada_w, v_ada_b, v_norm_g, v_s5_lam_re, v_s5_lam_im, v_s5_log_step, v_s5_b_re, v_s5_b_im, v_s5_c_re, v_s5_c_im, v_s5_d, v_s5_glu_w, v_pool_w, v_pool_scale, v_ffn_up, v_ffn_conv, v_ffn_conv_b, v_ffn_down):
    L, D = x.shape[1], x.shape[2]
    LC = ctx.shape[1]
    G = s5_lam_re.shape[2]
    n_ada = ada_w.shape[2]
    nb_up = ffn_up.shape[2]
    r_down = ffn_down.shape[1]
    FF = N_DEV * r_down
    n_pool = len(POOL_WINDOWS)
    pc = D // n_pool
    pr = pool_w.shape[2]
    ng_loc = norm_g.shape[2]
    me = _my_index()
    axes = ("x", "y", "c")

    up_g, down_g, glu_g, pool_g = _exchange(
        [[ffn_up[0].astype(_ACT), ffn_up[1].astype(_ACT)],
         [ffn_down[0].astype(_ACT), ffn_down[1].astype(_ACT)],
         [s5_glu_w[0].astype(_ACT)],
         [pool_w[0].reshape(n_pool * pr, pc).astype(_ACT)]], gather=True, name="gather_weights")
    pool_full = pool_g.reshape(N_DEV, n_pool, pr, pc).transpose(1, 0, 2, 3).reshape(n_pool, pc, pc)

    small_loc = jnp.concatenate([c.reshape(-1), norm_g.reshape(-1), pool_scale.reshape(-1), ffn_conv.reshape(-1)])
    n_small = small_loc.shape[0]
    small_g, = _exchange([[jnp.pad(small_loc, (0, (-n_small) % LANE)).reshape(1, -1)]], gather=True, name="gather_small")
    small_g = small_g.reshape(N_DEV, -1)
    o = 0
    c_all = small_g[:, o:o + D]
    o += D
    ng_all = small_g[:, o:o + 8 * ng_loc].reshape(N_DEV, 2, 4, ng_loc).transpose(1, 2, 0, 3).reshape(2, 4, D)
    o += 8 * ng_loc
    pscale_all = small_g[:, o:o + ng_loc].reshape(D)
    o += ng_loc
    conv_all = small_g[:, o:o + 6 * nb_up].reshape(N_DEV, 2, 3, nb_up).transpose(1, 2, 0, 3).reshape(2, 3, 2 * FF)

    cmat = jnp.concatenate([c_all, c_ctx.reshape(1, D), jnp.zeros((ADA_ROWS - N_DEV - 1, D), F32)], axis=0)
    ada_b_loc = _my_block(ada_b, 1, n_ada).reshape(2, 1, n_ada)
    mods_loc = _ada_fwd(cmat, ada_w, ada_b_loc)
    mods_g, = _exchange([[mods_loc]], gather=True, name="gather_mods")
    mods_rows = mods_g.reshape(N_DEV, 2, ADA_ROWS, n_ada).transpose(1, 2, 0, 3).reshape(2, ADA_ROWS, 6, D)
    mod = lax.dynamic_index_in_dim(mods_rows, me, axis=1, keepdims=False)
    mod_c = mods_rows[0, N_DEV]

    def disc_all(lr, li, ls, br, bi):
        return [_discretize(lr[d], li[d], ls[d], br[d], bi[d]) for d in range(2)]

    disc, disc_vjp = jax.vjp(disc_all, s5_lam_re[0], s5_lam_im[0], s5_log_step[0], s5_b_re[0], s5_b_im[0])
    apar, bblk, cblk = _s5_pack([(disc[d][0], disc[d][1]) for d in range(2)],
                                [(disc[d][2], disc[d][3]) for d in range(2)],
                                [(s5_c_re[0, d], s5_c_im[0, d]) for d in range(2)])
    dsk = _row0(s5_d[0], D)
    cw = []
    for i in range(2):
        taps = conv_all[i].reshape(3, 2, FF).transpose(1, 0, 2)
        cw.append(jnp.concatenate([taps, ffn_conv_b[i].reshape(2, 1, FF), jnp.zeros((2, 4, FF), F32)], axis=1))

    vecs = {
        "b0": _vec(D, gpre=ng_all[0, 0], shift=mod[0, 0], scale=mod[0, 1]),
        "c0": _vec(D, gpre=ng_all[0, 0], shift=mod_c[0], scale=mod_c[1]),
        "b1": _vec(D, gpost=ng_all[0, 1], gate=mod[0, 2], gpre=ng_all[0, 2], shift=mod[0, 3], scale=mod[0, 4]),
        "b2": _vec(D, gpost=ng_all[0, 3], gate=mod[0, 5], gpre=ng_all[1, 0], shift=mod[1, 0], scale=mod[1, 1]),
        "b3": _vec(D, gpost=ng_all[1, 1], gate=mod[1, 2], yscale=pscale_all, gpre=ng_all[1, 2], shift=mod[1, 3],
                   scale=mod[1, 4]),
        "b4": _vec(D, gpost=ng_all[1, 3], gate=mod[1, 5]),
    }

    x0, u0 = _rows_fwd("rows_fwd_b0", x[0], _pos_emb(L, D), vecs["b0"], add=True, u_dtype=_ACT)
    uc, = _rows_fwd("rows_fwd_ctx", ctx[0], None, vecs["c0"], want_x=False, u_dtype=_ACT)
    u0s, ucs = _to_segments(u0), _to_segments(uc)
    y_s5, z_s5 = _s5_fwd(u0s, ucs, bblk, cblk, apar, dsk)
    vg = _colblock_fwd("glu_fwd_mm", z_s5, glu_g, 0, _ACT)
    mix0 = _from_segments(_glu_fwd("glu_fwd", vg))
    x1, un0 = _rows_fwd("rows_fwd_b1", x0, mix0, vecs["b1"], u_dtype=_ACT)
    h0 = _colblock_fwd("ffn0_up", un0, up_g, 0, _ACT)
    act0 = _conv_swiglu_fwd("ffn0_conv", h0, cw[0])
    f0 = _rowblock_fwd("ffn0_down", act0, down_g, 0)
    x2, u1 = _rows_fwd("rows_fwd_b2", x1, f0, vecs["b2"], u_dtype=F32)
    p1 = _pool_window("pool_fwd", u1, False, _ACT)
    ypre1 = _group_mm("pool_fwd_mm", p1, pool_full, "nn", F32)
    x3, un1 = _rows_fwd("rows_fwd_b3", x2, ypre1, vecs["b3"], u_dtype=_ACT)
    h1 = _colblock_fwd("ffn1_up", un1, up_g, 1, _ACT)
    act1 = _conv_swiglu_fwd("ffn1_conv", h1, cw[1])
    f1 = _rowblock_fwd("ffn1_down", act1, down_g, 1)
    dx4, loss_blk = _rows_fwd("rows_fwd_b4", x3, f1, vecs["b4"], target=loss_target[0])
    loss = lax.psum(loss_blk[0, 0], axes)

    df1, red4 = _rows_bwd("rows_bwd_b4", dx4, None, None, f1, vecs["b4"], dy_dtype=_ACT, want_dx=False)
    dact1 = _rowblock_dgrad("ffn1_down_dgrad", df1, down_g, 1)
    ddown1 = _rowblock_wgrad("ffn1_down_wgrad", act1, df1)
    dh1, dcw1 = _conv_swiglu_bwd("ffn1_conv_bwd", h1, cw[1], dact1)
    dun1 = _colblock_dgrad("ffn1_up_dgrad", dh1, up_g, 1, F32)
    dup1 = _colblock_wgrad("ffn1_up_wgrad", un1, dh1)
    dx3, dy3, red3 = _rows_bwd("rows_bwd_b3", dx4, dun1, x3, ypre1, vecs["b3"])
    dypre1, red_ps = _colscale_bwd("pool_scale_bwd", dy3, ypre1, _row0(pscale_all, D))
    dp1 = _group_mm("pool_dgrad", dypre1, pool_full, "nt", F32)
    dpool = _group_wgrad("pool_wgrad", p1, dypre1, n_pool)
    du1 = _pool_window("pool_bwd", dp1, True, F32)
    dx2, df0, red2 = _rows_bwd("rows_bwd_b2", dx3, du1, x2, f0, vecs["b2"], dy_dtype=_ACT)
    dact0 = _rowblock_dgrad("ffn0_down_dgrad", df0, down_g, 0)
    ddown0 = _rowblock_wgrad("ffn0_down_wgrad", act0, df0)
    dh0, dcw0 = _conv_swiglu_bwd("ffn0_conv_bwd", h0, cw[0], dact0)
    dun0 = _colblock_dgrad("ffn0_up_dgrad", dh0, up_g, 0, F32)
    dup0 = _colblock_wgrad("ffn0_up_wgrad", un0, dh0)
    dx1, dmix0, red1 = _rows_bwd("rows_bwd_b1", dx2, dun0, x1, mix0, vecs["b1"])
    dvg = _glu_bwd("glu_bwd", vg, _to_segments(dmix0))
    dz = _colblock_dgrad("glu_dgrad", dvg, glu_g, 0, _ACT)
    dglu = _colblock_wgrad("glu_wgrad", z_s5, dvg)
    du0s, ducs, dbblk, dcblk, dapar, ddsk = _s5_bwd(u0s, ucs, dz, y_s5, bblk, cblk, apar, dsk)
    grad_x, red0 = _rows_bwd("rows_bwd_b0", dx1, _from_segments(du0s), x0, None, vecs["b0"])
    redc, = _rows_bwd("rows_bwd_ctx", None, _from_segments(ducs), ctx[0], None, vecs["c0"], want_dx=False)

    zero_d = jnp.zeros((D,), F32)
    dmod = jnp.stack([
        jnp.stack([red0[R_SHIFT], red0[R_SCALE], red1[R_GATE], red1[R_SHIFT], red1[R_SCALE], red2[R_GATE]]),
        jnp.stack([red2[R_SHIFT], red2[R_SCALE], red3[R_GATE], red3[R_SHIFT], red3[R_SCALE], red4[R_GATE]])])
    dmod_c = jnp.stack([jnp.stack([redc[R_SHIFT], redc[R_SCALE]] + [zero_d] * 4), jnp.zeros((6, D), F32)])
    dm_g, = _exchange([[jnp.stack([dmod, dmod_c], axis=1).reshape(2, 2, 6 * D)]], gather=True, name="gather_dmods")
    dm_g = dm_g.reshape(N_DEV, 2, 2, 6 * D)
    dm_ctx = _sum_parts("sum_dmod_ctx", dm_g[:, :, 1, :])
    dm_rows = jnp.concatenate([dm_g[:, :, 0, :].transpose(1, 0, 2), dm_ctx[:, None, :]], axis=1)
    grad_ada_b = _sum_parts("sum_ada_b", dm_rows.transpose(1, 0, 2))
    dm_cols = dm_rows.reshape(2, N_DEV + 1, N_DEV, n_ada)
    dm_mine = lax.dynamic_index_in_dim(dm_cols, me, axis=2, keepdims=False)
    dm_mine = jnp.concatenate([dm_mine, jnp.zeros((2, ADA_ROWS - N_DEV - 1, n_ada), F32)], axis=1)
    grad_ada_w, dcond = _ada_bwd(cmat, ada_w, dm_mine)
    dcctx_part = dcond[0, N_DEV] + dcond[1, N_DEV]

    da, db, dc = _s5_unpack(dapar, dbblk, dcblk, G)
    dnorm = jnp.stack([
        jnp.stack([red0[R_GPRE] + redc[R_GPRE], red1[R_GPOST], red1[R_GPRE], red2[R_GPOST]]),
        jnp.stack([red2[R_GPRE], red3[R_GPOST], red3[R_GPRE], red4[R_GPOST]])])
    dconv = jnp.stack([d[:, :3, :].transpose(1, 0, 2).reshape(3, 2 * FF) for d in (dcw0, dcw1)])
    dconv_b = jnp.stack([d[:, 3, :].reshape(2 * FF) for d in (dcw0, dcw1)])
    pieces = [dcctx_part, dnorm, da, db, dc, ddsk[:, 0, :], red_ps[0], dconv, dconv_b]
    flat = jnp.concatenate([p.reshape(-1) for p in pieces])
    n_flat = flat.shape[0]
    per_dev = -(-n_flat // (N_DEV * 8 * LANE)) * 8 * LANE
    flat = jnp.pad(flat, (0, N_DEV * per_dev - n_flat)).reshape(N_DEV, per_dev // LANE, LANE)
    parts, = _exchange([[flat]], gather=False, name="scatter_small_grads")
    mine = _sum_parts("sum_small_grads", parts.reshape(N_DEV, per_dev // LANE, LANE))
    summed, = _exchange([[mine]], gather=True, name="gather_small_grads")
    summed = summed.reshape(-1)
    red_pieces, o = [], 0
    for p in pieces:
        red_pieces.append(summed[o:o + p.size].reshape(p.shape))
        o += p.size
    g_cctx, g_norm, g_a, g_b, g_c, g_d, g_pscale, g_conv, g_conv_b = red_pieces
    cot = [(g_a[d, 0], g_a[d, 1], g_b[d, 0], g_b[d, 1]) for d in range(2)]
    g_lam_re, g_lam_im, g_log_step, g_b_re, g_b_im = disc_vjp(cot)

    dpool_blocks = dpool.reshape(n_pool, N_DEV, pr, pc).transpose(1, 0, 2, 3).reshape(N_DEV, n_pool * pr, pc)
    gp_up, gp_down, gp_glu, gp_pool = _exchange(
        [[dup0, dup1], [ddown0.reshape(N_DEV, r_down, D), ddown1.reshape(N_DEV, r_down, D)], [dglu], [dpool_blocks]],
        gather=False, name="scatter_weight_grads")

    out = {}

    def put(name, res, shape):
        out[name] = tuple(r.reshape(shape) for r in res)

    put("ffn_up", _adamw("adamw_ffn_up", gp_up.reshape(N_DEV, 2 * D, nb_up), ffn_up.reshape(2 * D, nb_up),
                         m_ffn_up.reshape(2 * D, nb_up), v_ffn_up.reshape(2 * D, nb_up)), ffn_up.shape)
    put("ffn_down", _adamw("adamw_ffn_down", gp_down.reshape(N_DEV, 2 * r_down, D), ffn_down.reshape(2 * r_down, D),
                           m_ffn_down.reshape(2 * r_down, D), v_ffn_down.reshape(2 * r_down, D)), ffn_down.shape)
    put("s5_glu_w", _adamw("adamw_glu", gp_glu.reshape(N_DEV, D, -1), s5_glu_w[0], m_s5_glu_w[0], v_s5_glu_w[0]),
        s5_glu_w.shape)
    put("pool_w", _adamw("adamw_pool", gp_pool.reshape(N_DEV, n_pool * pr, pc), pool_w.reshape(n_pool * pr, pc),
                         m_pool_w.reshape(n_pool * pr, pc), v_pool_w.reshape(n_pool * pr, pc)), pool_w.shape)
    put("ada_w", _adamw("adamw_ada_w", grad_ada_w.reshape(1, 2 * D, n_ada), ada_w.reshape(2 * D, n_ada),
                        m_ada_w.reshape(2 * D, n_ada), v_ada_w.reshape(2 * D, n_ada)), ada_w.shape)

    small = [
        ("c_ctx", c_ctx, m_c_ctx, v_c_ctx, g_cctx),
        ("ada_b", ada_b, m_ada_b, v_ada_b, grad_ada_b),
        ("norm_g", norm_g, m_norm_g, v_norm_g, _my_block(g_norm, 2, ng_loc)),
        ("s5_lam_re", s5_lam_re, m_s5_lam_re, v_s5_lam_re, g_lam_re),
        ("s5_lam_im", s5_lam_im, m_s5_lam_im, v_s5_lam_im, g_lam_im),
        ("s5_log_step", s5_log_step, m_s5_log_step, v_s5_log_step, g_log_step),
        ("s5_b_re", s5_b_re, m_s5_b_re, v_s5_b_re, g_b_re),
        ("s5_b_im", s5_b_im, m_s5_b_im, v_s5_b_im, g_b_im),
        ("s5_c_re", s5_c_re, m_s5_c_re, v_s5_c_re, g_c[:, 0]),
        ("s5_c_im", s5_c_im, m_s5_c_im, v_s5_c_im, g_c[:, 1]),
        ("s5_d", s5_d, m_s5_d, v_s5_d, g_d),
        ("pool_scale", pool_scale, m_pool_scale, v_pool_scale, _my_block(g_pscale, 0, ng_loc)),
        ("ffn_conv", ffn_conv, m_ffn_conv, v_ffn_conv, _my_block(g_conv, 2, nb_up)),
        ("ffn_conv_b", ffn_conv_b, m_ffn_conv_b, v_ffn_conv_b, g_conv_b),
    ]
    n_sm = sum(w.size for _, w, _, _, _ in small)
    rows_sm = -(-n_sm // (512 * LANE)) * 512

    def flat_of(k):
        f = jnp.concatenate([t[k].reshape(-1) for t in small])
        return jnp.pad(f, (0, rows_sm * LANE - n_sm)).reshape(rows_sm, LANE)

    res_sm = _adamw("adamw_small", flat_of(4).reshape(1, rows_sm, LANE), flat_of(1), flat_of(2), flat_of(3))
    o = 0
    for name, w, _, _, _ in small:
        out[name] = tuple(r.reshape(-1)[o:o + w.size].reshape(w.shape) for r in res_sm)
        o += w.size

    order = ["c_ctx", "ada_w", "ada_b", "norm_g", "s5_lam_re", "s5_lam_im", "s5_log_step", "s5_b_re", "s5_b_im",
             "s5_c_re", "s5_c_im", "s5_d", "s5_glu_w", "pool_w", "pool_scale", "ffn_up", "ffn_conv", "ffn_conv_b",
             "ffn_down"]
    return (loss, grad_x.reshape(x.shape), *[out[n][0] for n in order], *[out[n][1] for n in order],
            *[out[n][2] for n in order], *[out[n][3] for n in order])
```

```python
import functools
import math

import jax
import jax.numpy as jnp
from jax import lax
from jax.experimental import pallas as pl
from jax.experimental.pallas import tpu as pltpu

F32 = jnp.float32
_ACT = jnp.bfloat16
N_DEV = 8
NSEG = 8
S5_CH = 16
S5_P = 64
LANE = 128
S5_TILE_CH = LANE
S5_TILE_G = S5_TILE_CH // S5_CH
S5_TILE_W = S5_TILE_G * S5_P
GRID_W = 64
POOL_WINDOWS = (2, 4, 8, 16)
POOL_HALO = 64
RMS_EPS = 1e-6
POS_BASE = 10000.0
ADAM_LR, ADAM_B1, ADAM_B2, ADAM_EPS, ADAM_WD, ADAM_STEP = 0.001, 0.9, 0.999, 1e-08, 0.01, 10
VMEM_LIMIT = 48 * 1024 * 1024
VMEM_LIMIT_BIG = 58 * 1024 * 1024
MESH = pl.DeviceIdType.MESH
ANY = pl.BlockSpec(memory_space=pl.ANY)


def _params(sem, vmem=VMEM_LIMIT):
    return pltpu.CompilerParams(dimension_semantics=sem, vmem_limit_bytes=vmem)


def _my_index():
    return 4 * lax.axis_index("x") + 2 * lax.axis_index("y") + lax.axis_index("c")


def _xchg_plan(groups, gather):
    flat = [(g, l, a) for g, grp in enumerate(groups) for l, a in enumerate(grp)]
    outs = []
    for grp in groups:
        piece = grp[0].shape if gather else grp[0].shape[1:]
        outs.append(jax.ShapeDtypeStruct((N_DEV, len(grp)) + tuple(piece), grp[0].dtype))
    return flat, outs


def _xchg_sems(n):
    return [pltpu.SemaphoreType.DMA((n, N_DEV - 1)), pltpu.SemaphoreType.DMA((n, N_DEV - 1)),
            pltpu.SemaphoreType.DMA((n,))]


def _xchg_copies(flat, gather, ins, out_refs, sems, arrivals=True):
    send_sems, recv_sems, local_sems = sems
    x, y, c = lax.axis_index("x"), lax.axis_index("y"), lax.axis_index("c")
    me = 4 * x + 2 * y + c
    local, sent = [], []
    for s, (g, l, _) in enumerate(flat):
        src = ins[s] if gather else ins[s].at[me]
        local.append(pltpu.make_async_copy(src, out_refs[g].at[me, l], local_sems.at[s]))
    for k in range(1, N_DEV):
        px = 1 - x if k & 4 else x
        py = 1 - y if k & 2 else y
        pc = 1 - c if k & 1 else c
        peer = 4 * px + 2 * py + pc
        for s, (g, l, _) in enumerate(flat):
            src = ins[s] if gather else ins[s].at[peer]
            pair = dict(send_sem=send_sems.at[s, k - 1], recv_sem=recv_sems.at[s, k - 1],
                        device_id=(px, py, pc), device_id_type=MESH)
            send = pltpu.make_async_remote_copy(src_ref=src, dst_ref=out_refs[g].at[me, l], **pair)
            arrive = (pltpu.make_async_remote_copy(src_ref=src, dst_ref=out_refs[g].at[peer, l], **pair)
                      if arrivals else None)
            sent.append((send, arrive))
    return local, sent


def _xchg_start(local, sent):
    for cp in local:
        cp.start()
    for send, _ in sent:
        send.start()


def _xchg_wait(local, sent):
    for _, arrive in sent:
        arrive.wait_recv()
    for send, _ in sent:
        send.wait_send()
    for cp in local:
        cp.wait()


def _exchange(groups, gather, name):
    flat, outs = _xchg_plan(groups, gather)
    n = len(flat)

    def body(*refs):
        copies = _xchg_copies(flat, gather, refs[:n], refs[n:n + len(groups)], refs[n + len(groups):])
        _xchg_start(*copies)
        _xchg_wait(*copies)

    res = pl.pallas_call(body, name=name, out_shape=outs, in_specs=[ANY] * n, out_specs=[ANY] * len(groups),
                         scratch_shapes=_xchg_sems(n))(*[a for _, _, a in flat])
    return list(res)


def _pcall(body, *, name, grid, in_specs, out_specs, out_shape, ins, scratch_shapes=(), sem=None, vmem=VMEM_LIMIT,
           rider=None):
    single = not isinstance(out_shape, (list, tuple))
    if rider is None:
        return pl.pallas_call(body, name=name, grid=grid, in_specs=list(in_specs), out_specs=out_specs,
                              out_shape=out_shape, scratch_shapes=list(scratch_shapes),
                              compiler_params=_params(sem, vmem))(*ins)
    groups, gather = rider
    flat, r_outs = _xchg_plan(groups, gather)
    n_in, n_out = len(ins), 1 if single else len(out_shape)
    nr, ng, ns = len(flat), len(groups), len(scratch_shapes)

    def wrapped(*refs):
        o1 = n_in + nr
        o2 = o1 + n_out
        o3 = o2 + ng
        r_in, r_out, sems = refs[n_in:o1], refs[o2:o3], refs[o3 + ns:]
        first = functools.reduce(jnp.logical_and, [pl.program_id(d) == 0 for d in range(len(grid))])
        last = functools.reduce(jnp.logical_and, [pl.program_id(d) == grid[d] - 1 for d in range(len(grid))])

        @pl.when(first)
        def _():
            _xchg_start(*_xchg_copies(flat, gather, r_in, r_out, sems, arrivals=False))

        body(*refs[:n_in], *refs[o1:o2], *refs[o3:o3 + ns])

        @pl.when(last)
        def _():
            _xchg_wait(*_xchg_copies(flat, gather, r_in, r_out, sems))

    outs = pl.pallas_call(
        wrapped, name=name, grid=grid, in_specs=list(in_specs) + [ANY] * nr,
        out_specs=([out_specs] if single else list(out_specs)) + [ANY] * ng,
        out_shape=([out_shape] if single else list(out_shape)) + r_outs,
        scratch_shapes=list(scratch_shapes) + _xchg_sems(nr),
        compiler_params=_params(("arbitrary",) * len(grid), vmem))(*ins, *[a for _, _, a in flat])
    base = list(outs[:n_out])
    return (base[0] if single else base), list(outs[n_out:])


_DIMS = {"nn": (((1,), (0,)), ((), ())), "nt": (((1,), (1,)), ((), ())), "tn": (((0,), (0,)), ((), ()))}


def _mm(name, a, b, a_spec, b_spec, o_spec, out_shape, grid, dims, rider=None):
    nk = grid[2]
    acc_shape = tuple(d for d in o_spec.block_shape if d is not None)
    dn = _DIMS[dims]

    def tile(ref):
        v = ref[...]
        return v.reshape((-1, v.shape[-1])).astype(_ACT)

    def body(a_ref, b_ref, o_ref, *scratch):
        part = lax.dot_general(tile(a_ref), tile(b_ref), dn, preferred_element_type=F32)
        if nk == 1:
            o_ref[...] = part.reshape(o_ref.shape).astype(o_ref.dtype)
            return
        acc_ref, = scratch
        k = pl.program_id(2)

        @pl.when(k == 0)
        def _():
            acc_ref[...] = part

        @pl.when(k > 0)
        def _():
            acc_ref[...] += part

        @pl.when(k == nk - 1)
        def _():
            o_ref[...] = acc_ref[...].reshape(o_ref.shape).astype(o_ref.dtype)

    acc2d = (math.prod(acc_shape[:-1]), acc_shape[-1])
    return _pcall(body, name=name, out_shape=out_shape, grid=grid, in_specs=[a_spec, b_spec], out_specs=o_spec,
                  scratch_shapes=[] if nk == 1 else [pltpu.VMEM(acc2d, F32)], ins=(a, b),
                  sem=("parallel", "parallel", "arbitrary"), rider=rider)


def _row_tile(n, want):
    t = min(n, want)
    assert n % t == 0, (n, t)
    return t


def _colblock_fwd(name, xa, wg, layer, out_dtype, rider=None):
    L, K = xa.shape
    nb = wg.shape[3]
    half = N_DEV // 2
    tm = _row_tile(L, 512)
    return _mm(name, xa, wg,
               pl.BlockSpec((tm, K), lambda j, i, k: (i, 0)),
               pl.BlockSpec((None, None, K, nb), lambda j, i, k: (j, layer, 0, 0)),
               pl.BlockSpec((None, tm, nb), lambda j, i, k: (j // half, i, j % half)),
               jax.ShapeDtypeStruct((2, L, half * nb), out_dtype), (N_DEV, L // tm, 1), "nn", rider=rider)


def _colblock_dgrad(name, dh, wg, layer, out_dtype, rider=None):
    _, L, _ = dh.shape
    K, nb = wg.shape[2], wg.shape[3]
    half = N_DEV // 2
    tm = _row_tile(L, 512)
    return _mm(name, dh, wg,
               pl.BlockSpec((None, tm, nb), lambda i, j, k: (k // half, i, k % half)),
               pl.BlockSpec((None, None, K, nb), lambda i, j, k: (k, layer, 0, 0)),
               pl.BlockSpec((tm, K), lambda i, j, k: (i, 0)),
               jax.ShapeDtypeStruct((L, K), out_dtype), (L // tm, 1, N_DEV), "nt", rider=rider)


def _colblock_wgrad(name, xa, dh, rider=None):
    L, K = xa.shape
    half = N_DEV // 2
    nb = dh.shape[2] // half
    tm = _row_tile(K, 1024)
    tk = _row_tile(L, 512)
    return _mm(name, xa, dh,
               pl.BlockSpec((tk, tm), lambda j, i, k: (k, i)),
               pl.BlockSpec((None, tk, nb), lambda j, i, k: (j // half, k, j % half)),
               pl.BlockSpec((None, tm, nb), lambda j, i, k: (j, i, 0)),
               jax.ShapeDtypeStruct((N_DEV, K, nb), _ACT), (N_DEV, K // tm, L // tk), "tn", rider=rider)


def _rowblock_fwd(name, xa, wg, layer, rider=None):
    L, FF = xa.shape
    r, D = wg.shape[2], wg.shape[3]
    tm = _row_tile(L, 512)
    return _mm(name, xa, wg,
               pl.BlockSpec((tm, 2 * r), lambda i, j, k: (i, k)),
               pl.BlockSpec((2, None, r, D), lambda i, j, k: (k, layer, 0, 0)),
               pl.BlockSpec((tm, D), lambda i, j, k: (i, 0)),
               jax.ShapeDtypeStruct((L, D), F32), (L // tm, 1, N_DEV // 2), "nn", rider=rider)


def _rowblock_dgrad(name, dy, wg, layer, rider=None):
    L, D = dy.shape
    r = wg.shape[2]
    tm = _row_tile(L, 512)
    return _mm(name, dy, wg,
               pl.BlockSpec((tm, D), lambda i, j, k: (i, 0)),
               pl.BlockSpec((2, None, r, D), lambda i, j, k: (j, layer, 0, 0)),
               pl.BlockSpec((tm, 2 * r), lambda i, j, k: (i, j)),
               jax.ShapeDtypeStruct((L, N_DEV * r), _ACT), (L // tm, N_DEV // 2, 1), "nt", rider=rider)


def _rowblock_wgrad(name, xa, dy, rider=None):
    L, FF = xa.shape
    D = dy.shape[1]
    tm = FF // (N_DEV // 2)
    tk = _row_tile(L, 512)
    return _mm(name, xa, dy,
               pl.BlockSpec((tk, tm), lambda i, j, k: (k, i)),
               pl.BlockSpec((tk, D), lambda i, j, k: (k, 0)),
               pl.BlockSpec((tm, D), lambda i, j, k: (i, 0)),
               jax.ShapeDtypeStruct((FF, D), _ACT), (FF // tm, 1, L // tk), "tn", rider=rider)


def _group_mm(name, xa, w, dims, out_dtype):
    L, D = xa.shape
    ng, pc, _ = w.shape
    tm = _row_tile(L, 512)
    return _mm(name, xa, w,
               pl.BlockSpec((tm, pc), lambda i, g, k: (i, g)),
               pl.BlockSpec((None, pc, pc), lambda i, g, k: (g, 0, 0)),
               pl.BlockSpec((tm, pc), lambda i, g, k: (i, g)),
               jax.ShapeDtypeStruct((L, D), out_dtype), (L // tm, ng, 1), dims)


def _group_wgrad(name, p, dy, ng):
    L, D = p.shape
    pc = D // ng
    tk = _row_tile(L, 512)
    return _mm(name, p, dy,
               pl.BlockSpec((tk, pc), lambda g, j, k: (k, g)),
               pl.BlockSpec((tk, pc), lambda g, j, k: (k, g)),
               pl.BlockSpec((None, pc, pc), lambda g, j, k: (g, 0, 0)),
               jax.ShapeDtypeStruct((ng, pc, pc), _ACT), (ng, 1, L // tk), "tn")


V_GPOST, V_GATE, V_YSCALE, V_GPRE, V_SHIFT, V_SCALE = range(6)
R_SHIFT, R_SCALE, R_GPRE, R_GATE, R_GPOST = range(5)
ROW_TILE = 256


def _rstd(v):
    return lax.rsqrt(jnp.mean(v * v, axis=-1, keepdims=True) + RMS_EPS)


def _rows_fwd(name, xres, y, vec, *, add=False, target=None, want_x=True, u_dtype=None):
    L, D = xres.shape
    tm = _row_tile(L, ROW_TILE)
    has_y = y is not None
    last = target is not None
    has_u = u_dtype is not None

    def body(*refs):
        refs = list(refs)
        xres_ref = refs.pop(0)
        y_ref = refs.pop(0) if has_y else None
        vec_ref = refs.pop(0)
        tgt_ref = refs.pop(0) if last else None
        xnew = xres_ref[...]
        if has_y and add:
            xnew = xnew + y_ref[...]
        elif has_y:
            ye = y_ref[...] * vec_ref[V_YSCALE:V_YSCALE + 1, :]
            xnew = xnew + vec_ref[V_GATE:V_GATE + 1, :] * (ye * _rstd(ye) * vec_ref[V_GPOST:V_GPOST + 1, :])
        if last:
            dx_ref, loss_ref = refs
            diff = xnew - tgt_ref[...]
            dx_ref[...] = diff * (1.0 / D)

            @pl.when(pl.program_id(0) == 0)
            def _():
                loss_ref[...] = jnp.zeros_like(loss_ref)

            loss_ref[...] += jnp.sum(diff * diff) * (0.5 / D)
            return
        if want_x:
            refs.pop(0)[...] = xnew
        if has_u:
            u_ref, = refs
            n = xnew * _rstd(xnew) * vec_ref[V_GPRE:V_GPRE + 1, :]
            u_ref[...] = (n * (1.0 + vec_ref[V_SCALE:V_SCALE + 1, :]) + vec_ref[V_SHIFT:V_SHIFT + 1, :]).astype(u_ref.dtype)

    row = pl.BlockSpec((tm, D), lambda i: (i, 0))
    vspec = pl.BlockSpec((8, D), lambda i: (0, 0))
    ins, in_specs = [xres], [row]
    if has_y:
        ins.append(y)
        in_specs.append(row)
    ins.append(vec)
    in_specs.append(vspec)
    out_shape, out_specs = [], []
    if last:
        ins.append(target)
        in_specs.append(row)
        out_shape = [jax.ShapeDtypeStruct((L, D), F32), jax.ShapeDtypeStruct((8, LANE), F32)]
        out_specs = [row, pl.BlockSpec((8, LANE), lambda i: (0, 0))]
    else:
        if want_x:
            out_shape.append(jax.ShapeDtypeStruct((L, D), F32))
            out_specs.append(row)
        if has_u:
            out_shape.append(jax.ShapeDtypeStruct((L, D), u_dtype))
            out_specs.append(row)
    return pl.pallas_call(body, name=name, out_shape=out_shape, grid=(L // tm,), in_specs=in_specs,
                          out_specs=out_specs, compiler_params=_params(("arbitrary",)))(*ins)


def _rows_bwd(name, dxd, du, xnew, y, vec, dy_dtype=F32, want_dx=True):
    L, D = xnew.shape if xnew is not None else dxd.shape
    tm = _row_tile(L, ROW_TILE)
    has_dxd, has_pre, has_post = dxd is not None, du is not None, y is not None

    def body(*refs):
        refs = list(refs)
        dxd_ref = refs.pop(0) if has_dxd else None
        du_ref = refs.pop(0) if has_pre else None
        xnew_ref = refs.pop(0) if has_pre else None
        y_ref = refs.pop(0) if has_post else None
        vec_ref = refs.pop(0)
        dx_ref = refs.pop(0) if want_dx else None
        dy_ref = refs.pop(0) if has_post else None
        red_ref, = refs

        @pl.when(pl.program_id(0) == 0)
        def _():
            red_ref[...] = jnp.zeros_like(red_ref)

        def acc(rw, val):
            red_ref[rw:rw + 1, :] += jnp.sum(val, axis=0, keepdims=True)

        dxn = dxd_ref[...] if has_dxd else None
        if has_pre:
            xn = xnew_ref[...]
            r = _rstd(xn)
            nh = xn * r
            gpre = vec_ref[V_GPRE:V_GPRE + 1, :]
            dub = du_ref[...].astype(F32)
            acc(R_SHIFT, dub)
            acc(R_SCALE, dub * (nh * gpre))
            drn = dub * (1.0 + vec_ref[V_SCALE:V_SCALE + 1, :])
            acc(R_GPRE, drn * nh)
            dnh = drn * gpre
            t = r * (dnh - nh * jnp.mean(dnh * nh, axis=-1, keepdims=True))
            dxn = t if dxn is None else dxn + t
        if want_dx:
            dx_ref[...] = dxn
        if has_post:
            ye = y_ref[...] * vec_ref[V_YSCALE:V_YSCALE + 1, :]
            ry = _rstd(ye)
            yh = ye * ry
            gpost = vec_ref[V_GPOST:V_GPOST + 1, :]
            acc(R_GATE, dxn * (yh * gpost))
            drn2 = dxn * vec_ref[V_GATE:V_GATE + 1, :]
            acc(R_GPOST, drn2 * yh)
            dyh = drn2 * gpost
            dy_ref[...] = (ry * (dyh - yh * jnp.mean(dyh * yh, axis=-1, keepdims=True))).astype(dy_ref.dtype)

    row = pl.BlockSpec((tm, D), lambda i: (i, 0))
    vspec = pl.BlockSpec((8, D), lambda i: (0, 0))
    ins, in_specs = [], []
    for a in ([dxd] if has_dxd else []) + ([du, xnew] if has_pre else []) + ([y] if has_post else []):
        ins.append(a)
        in_specs.append(row)
    ins.append(vec)
    in_specs.append(vspec)
    out_shape, out_specs = [], []
    if want_dx:
        out_shape.append(jax.ShapeDtypeStruct((L, D), F32))
        out_specs.append(row)
    if has_post:
        out_shape.append(jax.ShapeDtypeStruct((L, D), dy_dtype))
        out_specs.append(row)
    out_shape.append(jax.ShapeDtypeStruct((8, D), F32))
    out_specs.append(vspec)
    return pl.pallas_call(body, name=name, out_shape=out_shape, grid=(L // tm,), in_specs=in_specs,
                          out_specs=out_specs, compiler_params=_params(("arbitrary",)))(*ins)


def _colscale_bwd(name, dy, ypre, scale):
    L, D = dy.shape
    tm = _row_tile(L, ROW_TILE)

    def body(dy_ref, yp_ref, s_ref, o_ref, red_ref):
        @pl.when(pl.program_id(0) == 0)
        def _():
            red_ref[...] = jnp.zeros_like(red_ref)

        d = dy_ref[...]
        o_ref[...] = (d * s_ref[0:1, :]).astype(o_ref.dtype)
        red_ref[0:1, :] += jnp.sum(d * yp_ref[...], axis=0, keepdims=True)

    row = pl.BlockSpec((tm, D), lambda i: (i, 0))
    vspec = pl.BlockSpec((8, D), lambda i: (0, 0))
    return pl.pallas_call(body, name=name, grid=(L // tm,), in_specs=[row, row, vspec], out_specs=[row, vspec],
                          out_shape=[jax.ShapeDtypeStruct((L, D), _ACT), jax.ShapeDtypeStruct((8, D), F32)],
                          compiler_params=_params(("arbitrary",)))(dy, ypre, scale)


def _sigmoid(v):
    return 1.0 / (1.0 + jnp.exp(-v))


def _glu_fwd(name, vg):
    _, L, D = vg.shape
    tm = _row_tile(L, ROW_TILE)

    def body(vg_ref, o_ref):
        o_ref[...] = vg_ref[0].astype(F32) * _sigmoid(vg_ref[1].astype(F32))

    return pl.pallas_call(body, name=name, grid=(L // tm,),
                          in_specs=[pl.BlockSpec((2, tm, D), lambda i: (0, i, 0))],
                          out_specs=pl.BlockSpec((tm, D), lambda i: (i, 0)),
                          out_shape=jax.ShapeDtypeStruct((L, D), F32),
                          compiler_params=_params(("parallel",)))(vg)


def _glu_bwd(name, vg, dout):
    _, L, D = vg.shape
    tm = _row_tile(L, ROW_TILE)

    def body(vg_ref, d_ref, o_ref):
        val, s = vg_ref[0].astype(F32), _sigmoid(vg_ref[1].astype(F32))
        d = d_ref[...]
        o_ref[0] = (d * s).astype(o_ref.dtype)
        o_ref[1] = (d * val * s * (1.0 - s)).astype(o_ref.dtype)

    return pl.pallas_call(body, name=name, grid=(L // tm,),
                          in_specs=[pl.BlockSpec((2, tm, D), lambda i: (0, i, 0)), pl.BlockSpec((tm, D), lambda i: (i, 0))],
                          out_specs=pl.BlockSpec((2, tm, D), lambda i: (0, i, 0)),
                          out_shape=jax.ShapeDtypeStruct((2, L, D), _ACT),
                          compiler_params=_params(("parallel",)))(vg, dout)


CONV_ROWS = 256


def _row_pick(blk, idx):
    rows = lax.broadcasted_iota(jnp.int32, blk.shape, 0)
    return jnp.sum(jnp.where(rows == idx, blk, 0.0), axis=0, keepdims=True)


def _shifted(ref, r0, rc, L):
    cur = ref[pl.ds(r0, rc), :].astype(F32)
    before = ref[pl.ds(pl.multiple_of(jnp.maximum(r0 - 16, 0), 16), 16), :].astype(F32)
    after = ref[pl.ds(pl.multiple_of(jnp.minimum(r0 + rc, L - 16), 16), 16), :].astype(F32)
    prev_row = jnp.where(r0 > 0, _row_pick(before, 15), 0.0)
    next_row = jnp.where(r0 + rc < L, _row_pick(after, 0), 0.0)
    rows = lax.broadcasted_iota(jnp.int32, cur.shape, 0)
    up = jnp.where(rows == 0, prev_row, pltpu.roll(cur, 1, 0))
    down = jnp.where(rows == rc - 1, next_row, pltpu.roll(cur, rc - 1, 0))
    return up, cur, down


def _silu_parts(g):
    s = _sigmoid(g)
    return g * s, s


def _conv_swiglu_fwd(name, h, cw):
    _, L, FF = h.shape
    rc = _row_tile(L, CONV_ROWS)

    def body(h_ref, cw_ref, o_ref):
        def chunk(ci, _):
            r0 = pl.multiple_of(ci * rc, rc)
            hc = []
            for half in range(2):
                up, cur, down = _shifted(h_ref.at[half], r0, rc, L)
                hc.append(up * cw_ref[half, 0:1, :] + cur * cw_ref[half, 1:2, :] + down * cw_ref[half, 2:3, :]
                          + cw_ref[half, 3:4, :])
            o_ref[pl.ds(r0, rc), :] = (_silu_parts(hc[1])[0] * hc[0]).astype(o_ref.dtype)
            return 0

        lax.fori_loop(0, L // rc, chunk, 0)

    return pl.pallas_call(body, name=name, grid=(FF // LANE,),
                          in_specs=[pl.BlockSpec((2, L, LANE), lambda j: (0, 0, j)),
                                    pl.BlockSpec((2, 8, LANE), lambda j: (0, 0, j))],
                          out_specs=pl.BlockSpec((L, LANE), lambda j: (0, j)),
                          out_shape=jax.ShapeDtypeStruct((L, FF), _ACT),
                          compiler_params=_params(("parallel",)))(h, cw)


def _conv_swiglu_bwd(name, h, cw, dact, rider=None):
    _, L, FF = h.shape
    rc = _row_tile(L, CONV_ROWS)

    def body(h_ref, cw_ref, da_ref, dh_ref, dcw_ref, dhc_ref):
        def chunk(ci, acc):
            r0 = pl.multiple_of(ci * rc, rc)
            taps, hc = [], []
            for half in range(2):
                t = _shifted(h_ref.at[half], r0, rc, L)
                taps.append(t)
                hc.append(t[0] * cw_ref[half, 0:1, :] + t[1] * cw_ref[half, 1:2, :] + t[2] * cw_ref[half, 2:3, :]
                          + cw_ref[half, 3:4, :])
            d = da_ref[pl.ds(r0, rc), :].astype(F32)
            act, s = _silu_parts(hc[1])
            dhc = (d * act, d * hc[0] * (s + act * (1.0 - s)))
            new = []
            for half in range(2):
                dhc_ref[half, pl.ds(r0, rc), :] = dhc[half]
                for k in range(3):
                    new.append(acc[4 * half + k] + jnp.sum(dhc[half] * taps[half][k], axis=0, keepdims=True))
                new.append(acc[4 * half + 3] + jnp.sum(dhc[half], axis=0, keepdims=True))
            return tuple(new)

        zero = jnp.zeros((1, LANE), F32)
        acc = lax.fori_loop(0, L // rc, chunk, (zero,) * 8)
        dcw_ref[...] = jnp.zeros_like(dcw_ref)
        for half in range(2):
            for k in range(4):
                dcw_ref[half, k:k + 1, :] = acc[4 * half + k]

        def chunk2(ci, _):
            r0 = pl.multiple_of(ci * rc, rc)
            for half in range(2):
                up, cur, down = _shifted(dhc_ref.at[half], r0, rc, L)
                dh_ref[half, pl.ds(r0, rc), :] = (down * cw_ref[half, 0:1, :] + cur * cw_ref[half, 1:2, :]
                                                  + up * cw_ref[half, 2:3, :]).astype(dh_ref.dtype)
            return 0

        lax.fori_loop(0, L // rc, chunk2, 0)

    return _pcall(body, name=name, grid=(FF // LANE,),
                  in_specs=[pl.BlockSpec((2, L, LANE), lambda j: (0, 0, j)),
                            pl.BlockSpec((2, 8, LANE), lambda j: (0, 0, j)),
                            pl.BlockSpec((L, LANE), lambda j: (0, j))],
                  out_specs=[pl.BlockSpec((2, L, LANE), lambda j: (0, 0, j)),
                             pl.BlockSpec((2, 8, LANE), lambda j: (0, 0, j))],
                  out_shape=[jax.ShapeDtypeStruct((2, L, FF), _ACT), jax.ShapeDtypeStruct((2, 8, FF), F32)],
                  scratch_shapes=[pltpu.VMEM((2, L, LANE), F32)], ins=(h, cw, dact), sem=("parallel",), rider=rider)


POOL_ROWS = 256
POOL_TILE = 256


def _pool_bands(transpose):
    i = jnp.arange(POOL_ROWS)[:, None]
    j = jnp.arange(POOL_ROWS + 2 * POOL_HALO)[None, :] - POOL_HALO
    bands = []
    for w in POOL_WINDOWS:
        lo, hi = (-(w // 2 - 1), w // 2) if transpose else (-(w // 2), w // 2 - 1)
        bands.append(((j - i >= lo) & (j - i <= hi)).astype(_ACT))
    return jnp.stack(bands)


def _pool_window(name, u, transpose, out_dtype):
    L, D = u.shape
    ng = len(POOL_WINDOWS)
    pc = D // ng
    tn = min(POOL_TILE, pc)
    rc = _row_tile(L, POOL_ROWS)
    bands = _pool_bands(transpose)
    if rc != POOL_ROWS:
        bands = bands[:, :rc, :rc + 2 * POOL_HALO]
    halo = POOL_HALO

    def body(u_ref, band_ref, o_ref, hi_ref, lo_ref):
        g = (pl.program_id(0) * tn) // pc
        half = jnp.zeros((1, 1), jnp.int32)
        for k, w in enumerate(POOL_WINDOWS):
            half = jnp.where(g == k, w // 2, half)
        zeros = jnp.zeros((halo, tn), _ACT)
        for ref in (hi_ref, lo_ref):
            ref[0:halo, :] = zeros
            ref[halo + L:2 * halo + L, :] = zeros

        def inv_count(r0):
            t = r0 + lax.broadcasted_iota(jnp.int32, (rc, tn), 0)
            lo = jnp.clip(t - half, 0, L - 1)
            hi = jnp.clip(t + half - 1, 0, L - 1)
            return 1.0 / (hi - lo + 1).astype(F32)

        def split(ci, _):
            r0 = pl.multiple_of(ci * rc, rc)
            v = u_ref[pl.ds(r0, rc), :].astype(F32)
            if transpose:
                v = v * inv_count(r0)
            hi = v.astype(_ACT)
            dst = pl.ds(pl.multiple_of(r0 + halo, halo), rc)
            hi_ref[dst, :] = hi
            lo_ref[dst, :] = (v - hi.astype(F32)).astype(_ACT)
            return 0

        lax.fori_loop(0, L // rc, split, 0)
        band = band_ref[...]

        def chunk(ci, _):
            r0 = pl.multiple_of(ci * rc, rc)
            win = pl.ds(r0, rc + 2 * halo)
            s = (jnp.dot(band, hi_ref[win, :], preferred_element_type=F32)
                 + jnp.dot(band, lo_ref[win, :], preferred_element_type=F32))
            if not transpose:
                s = s * inv_count(r0)
            o_ref[pl.ds(r0, rc), :] = (s - u_ref[pl.ds(r0, rc), :].astype(F32)).astype(o_ref.dtype)
            return 0

        lax.fori_loop(0, L // rc, chunk, 0)

    return pl.pallas_call(body, name=name, grid=(D // tn,),
                          in_specs=[pl.BlockSpec((L, tn), lambda j: (0, j)),
                                    pl.BlockSpec((None, rc, rc + 2 * halo), lambda j: ((j * tn) // pc, 0, 0))],
                          out_specs=pl.BlockSpec((L, tn), lambda j: (0, j)),
                          out_shape=jax.ShapeDtypeStruct((L, D), out_dtype),
                          scratch_shapes=[pltpu.VMEM((L + 2 * halo, tn), _ACT), pltpu.VMEM((L + 2 * halo, tn), _ACT)],
                          compiler_params=_params(("parallel",)))(u, bands)


S5_ROWS = 512


def _slab(start):
    return pl.ds(start if isinstance(start, int) else pl.multiple_of(start, NSEG), NSEG)


def _cmul(ar, ai, br, bi):
    return ar * br - ai * bi, ar * bi + ai * br


def _cpow(ar, ai, n):
    rr, ri = None, None
    br, bi = ar, ai
    while n:
        if n & 1:
            rr, ri = (br, bi) if rr is None else _cmul(rr, ri, br, bi)
        n >>= 1
        if n:
            br, bi = _cmul(br, bi, br, bi)
    return rr, ri


def _seg_scan(sr_ref, si_ref, tmp_ref, row0, n, ar, ai, h0, rev):
    W = ar.shape[1]
    arb, aib = jnp.broadcast_to(ar, (NSEG, W)), jnp.broadcast_to(ai, (NSEG, W))

    def rows(s):
        t = (n - 1 - s) if rev else s
        return _slab(row0 + t * NSEG)

    def step(s, carry):
        hr, hi = carry
        sl = rows(s)
        nr = arb * hr - aib * hi + sr_ref[sl, :]
        ni = arb * hi + aib * hr + si_ref[sl, :]
        sr_ref[sl, :] = nr
        si_ref[sl, :] = ni
        return nr, ni

    zero = jnp.zeros((NSEG, W), F32)
    fr, fi = lax.fori_loop(0, n, step, (zero, zero), unroll=2)
    tmp_ref[0] = fr
    tmp_ref[1] = fi
    anr, ani = _cpow(ar, ai, n)
    cr, ci = h0
    for j in (range(NSEG - 1, -1, -1) if rev else range(NSEG)):
        tmp_ref[2, j:j + 1, :] = cr
        tmp_ref[3, j:j + 1, :] = ci
        pr, pi = _cmul(anr, ani, cr, ci)
        cr, ci = tmp_ref[0, j:j + 1, :] + pr, tmp_ref[1, j:j + 1, :] + pi
    cmr, cmi = tmp_ref[2], tmp_ref[3]

    def fix(s, carry):
        pr, pi = carry
        sl = rows(s)
        sr_ref[sl, :] += pr * cmr - pi * cmi
        si_ref[sl, :] += pr * cmi + pi * cmr
        return pr * arb - pi * aib, pr * aib + pi * arb

    lax.fori_loop(0, n, fix, (arb, aib), unroll=2)
    return (cr, ci), (cmr, cmi)


def _gelu_tanh(y):
    k = math.sqrt(2.0 / math.pi)
    t = jnp.tanh(k * (y + 0.044715 * y * y * y))
    return 0.5 * y * (1.0 + t), t


def _s5_chunks(L):
    rc = _row_tile(L, S5_ROWS)
    return [(r, rc) for r in range(0, L, rc)]


def _s5_project(u_ref, uc_ref, bre, bim, sr_ref, si_ref, L, LC):
    for ref, base, n in ((u_ref, 0, L), (uc_ref, L, LC)):
        for r, rc in _s5_chunks(n):
            ub = ref[r:r + rc, :].astype(_ACT)
            sr_ref[base + r:base + r + rc, :] = jnp.dot(ub, bre, preferred_element_type=F32)
            si_ref[base + r:base + r + rc, :] = jnp.dot(ub, bim, preferred_element_type=F32)


def _s5_states(sr_ref, si_ref, tmp_ref, ar, ai, L, LC, rev):
    W = ar.shape[1]
    zero = (jnp.zeros((1, W), F32), jnp.zeros((1, W), F32))
    hctx, cm_ctx = _seg_scan(sr_ref, si_ref, tmp_ref, L, LC // NSEG, ar, ai, zero, rev)
    _, cm_lat = _seg_scan(sr_ref, si_ref, tmp_ref, 0, L // NSEG, ar, ai, hctx, rev)
    return cm_lat, cm_ctx


def _s5_fwd(u, uc, bblk, cblk, apar, dsk, rider=None):
    L, D = u.shape
    LC = uc.shape[0]
    NT, W, TC = D // S5_TILE_CH, S5_TILE_W, S5_TILE_CH

    def body(u_ref, uc_ref, b_ref, c_ref, a_ref, d_ref, y_ref, z_ref, sr_ref, si_ref, tmp_ref):
        for r, rc in _s5_chunks(L):
            y_ref[r:r + rc, :] = u_ref[r:r + rc, :].astype(F32) * d_ref[0:1, :]
        for d in range(2):
            ar, ai = a_ref[2 * d:2 * d + 1, :], a_ref[2 * d + 1:2 * d + 2, :]
            _s5_project(u_ref, uc_ref, b_ref[2 * d], b_ref[2 * d + 1], sr_ref, si_ref, L, LC)
            _s5_states(sr_ref, si_ref, tmp_ref, ar, ai, L, LC, rev=(d == 1))
            cre, cim = c_ref[2 * d], c_ref[2 * d + 1]
            for r, rc in _s5_chunks(L):
                y_ref[r:r + rc, :] += (jnp.dot(sr_ref[r:r + rc, :].astype(_ACT), cre, preferred_element_type=F32)
                                       - jnp.dot(si_ref[r:r + rc, :].astype(_ACT), cim, preferred_element_type=F32))
        for r, rc in _s5_chunks(L):
            z_ref[r:r + rc, :] = _gelu_tanh(y_ref[r:r + rc, :])[0].astype(z_ref.dtype)

    col = lambda n: pl.BlockSpec((n, TC), lambda j: (0, j))
    return _pcall(
        body, name="s5_fwd", grid=(NT,), ins=(u, uc, bblk, cblk, apar, dsk), sem=("parallel",), rider=rider,
        in_specs=[col(L), col(LC),
                  pl.BlockSpec((None, 4, TC, W), lambda j: (j, 0, 0, 0)),
                  pl.BlockSpec((None, 4, W, TC), lambda j: (j, 0, 0, 0)),
                  pl.BlockSpec((None, 8, W), lambda j: (j, 0, 0)),
                  pl.BlockSpec((8, TC), lambda j: (0, j))],
        out_specs=[col(L), col(L)],
        out_shape=[jax.ShapeDtypeStruct((L, D), F32), jax.ShapeDtypeStruct((L, D), _ACT)],
        scratch_shapes=[pltpu.VMEM((L + LC, W), F32), pltpu.VMEM((L + LC, W), F32), pltpu.VMEM((4, NSEG, W), F32)])


def _s5_bwd(u, uc, dz, y, bblk, cblk, apar, dsk, rider=None):
    L, D = u.shape
    LC = uc.shape[0]
    NT, W, TC = D // S5_TILE_CH, S5_TILE_W, S5_TILE_CH
    nl, nc = L // NSEG, LC // NSEG

    def body(u_ref, uc_ref, dz_ref, y_ref, b_ref, c_ref, a_ref, d_ref,
             du_ref, duc_ref, db_ref, dc_ref, da_ref, dd_ref,
             hr_ref, hi_ref, gr_ref, gi_ref, dy_ref, tmp_ref):
        ddacc = jnp.zeros((1, TC), F32)
        for r, rc in _s5_chunks(L):
            yv = y_ref[r:r + rc, :]
            g, t = _gelu_tanh(yv)
            k = math.sqrt(2.0 / math.pi)
            dg = 0.5 * (1.0 + t) + 0.5 * yv * (1.0 - t * t) * k * (1.0 + 3 * 0.044715 * yv * yv)
            dy = dz_ref[r:r + rc, :].astype(F32) * dg
            uv = u_ref[r:r + rc, :].astype(F32)
            ddacc = ddacc + jnp.sum(dy * uv, axis=0, keepdims=True)
            du_ref[r:r + rc, :] = dy * d_ref[0:1, :]
            dy_ref[r:r + rc, :] = dy.astype(dy_ref.dtype)
        dd_ref[...] = jnp.zeros_like(dd_ref)
        dd_ref[0:1, :] = ddacc
        duc_ref[...] = jnp.zeros_like(duc_ref)
        da_ref[...] = jnp.zeros_like(da_ref)
        nt = (((1,), (1,)), ((), ()))
        tn = (((0,), (0,)), ((), ()))
        for d in range(2):
            rev = d == 1
            ar, ai = a_ref[2 * d:2 * d + 1, :], a_ref[2 * d + 1:2 * d + 2, :]
            bre, bim = b_ref[2 * d], b_ref[2 * d + 1]
            cre, cim = c_ref[2 * d], c_ref[2 * d + 1]
            _s5_project(u_ref, uc_ref, bre, bim, hr_ref, hi_ref, L, LC)
            cm_lat, cm_ctx = _s5_states(hr_ref, hi_ref, tmp_ref, ar, ai, L, LC, rev)
            cml_r, cml_i, cmc_r, cmc_i = cm_lat[0], cm_lat[1], cm_ctx[0], cm_ctx[1]
            dcr = jnp.zeros((W, TC), F32)
            dci = jnp.zeros((W, TC), F32)
            for r, rc in _s5_chunks(L):
                dyb = dy_ref[r:r + rc, :]
                gr_ref[r:r + rc, :] = lax.dot_general(dyb, cre, nt, preferred_element_type=F32)
                gi_ref[r:r + rc, :] = -lax.dot_general(dyb, cim, nt, preferred_element_type=F32)
                dcr = dcr + lax.dot_general(hr_ref[r:r + rc, :].astype(_ACT), dyb, tn, preferred_element_type=F32)
                dci = dci - lax.dot_general(hi_ref[r:r + rc, :].astype(_ACT), dyb, tn, preferred_element_type=F32)
            dc_ref[2 * d] = dcr
            dc_ref[2 * d + 1] = dci
            gr_ref[L:L + LC, :] = jnp.zeros((LC, W), F32)
            gi_ref[L:L + LC, :] = jnp.zeros((LC, W), F32)
            zero = (jnp.zeros((1, W), F32), jnp.zeros((1, W), F32))
            glat, _ = _seg_scan(gr_ref, gi_ref, tmp_ref, 0, nl, ar, -ai, zero, not rev)
            _seg_scan(gr_ref, gi_ref, tmp_ref, L, nc, ar, -ai, glat, not rev)

            def da_part(row0, n, cmr, cmi):
                def rows(s):
                    t = (n - 1 - s) if rev else s
                    return _slab(row0 + t * NSEG)

                g0r, g0i = gr_ref[rows(0), :], gi_ref[rows(0), :]
                acc0 = (cmr * g0r + cmi * g0i, cmr * g0i - cmi * g0r)

                def step(s, acc):
                    hpr, hpi = hr_ref[rows(s - 1), :], hi_ref[rows(s - 1), :]
                    gr, gi = gr_ref[rows(s), :], gi_ref[rows(s), :]
                    return acc[0] + hpr * gr + hpi * gi, acc[1] + hpr * gi - hpi * gr

                return lax.fori_loop(1, n, step, acc0, unroll=2)

            lr, li = da_part(0, nl, cml_r, cml_i)
            qr, qi = da_part(L, nc, cmc_r, cmc_i)
            da_ref[2 * d:2 * d + 1, :] = jnp.sum(lr + qr, axis=0, keepdims=True)
            da_ref[2 * d + 1:2 * d + 2, :] = jnp.sum(li + qi, axis=0, keepdims=True)
            dbr = jnp.zeros((TC, W), F32)
            dbi = jnp.zeros((TC, W), F32)
            for ref, oref, base, n in ((u_ref, du_ref, 0, L), (uc_ref, duc_ref, L, LC)):
                for r, rc in _s5_chunks(n):
                    ub = ref[r:r + rc, :].astype(_ACT)
                    gr = gr_ref[base + r:base + r + rc, :].astype(_ACT)
                    gi = gi_ref[base + r:base + r + rc, :].astype(_ACT)
                    dbr = dbr + lax.dot_general(ub, gr, tn, preferred_element_type=F32)
                    dbi = dbi + lax.dot_general(ub, gi, tn, preferred_element_type=F32)
                    oref[r:r + rc, :] += (lax.dot_general(gr, bre, nt, preferred_element_type=F32)
                                          + lax.dot_general(gi, bim, nt, preferred_element_type=F32))
            db_ref[2 * d] = dbr
            db_ref[2 * d + 1] = dbi

    col = lambda n: pl.BlockSpec((n, TC), lambda j: (0, j))
    bspec = pl.BlockSpec((None, 4, TC, W), lambda j: (j, 0, 0, 0))
    cspec = pl.BlockSpec((None, 4, W, TC), lambda j: (j, 0, 0, 0))
    aspec = pl.BlockSpec((None, 8, W), lambda j: (j, 0, 0))
    return _pcall(
        body, name="s5_bwd", grid=(NT,), ins=(u, uc, dz, y, bblk, cblk, apar, dsk), sem=("parallel",),
        vmem=VMEM_LIMIT_BIG, rider=rider,
        in_specs=[col(L), col(LC), col(L), col(L), bspec, cspec, aspec, pl.BlockSpec((8, TC), lambda j: (0, j))],
        out_specs=[col(L), col(LC), bspec, cspec, aspec, pl.BlockSpec((None, 8, TC), lambda j: (j, 0, 0))],
        out_shape=[jax.ShapeDtypeStruct((L, D), F32), jax.ShapeDtypeStruct((LC, D), F32),
                   jax.ShapeDtypeStruct((NT, 4, TC, W), F32), jax.ShapeDtypeStruct((NT, 4, W, TC), F32),
                   jax.ShapeDtypeStruct((NT, 8, W), F32), jax.ShapeDtypeStruct((NT, 8, TC), F32)],
        scratch_shapes=[pltpu.VMEM((L + LC, W), F32), pltpu.VMEM((L + LC, W), F32),
                        pltpu.VMEM((L + LC, W), F32), pltpu.VMEM((L + LC, W), F32),
                        pltpu.VMEM((L, TC), _ACT), pltpu.VMEM((4, NSEG, W), F32)])


ADA_ROWS = 16


def _silu_rows(c_ref):
    c = c_ref[...]
    return c * _sigmoid(c)


def _ada_fwd(cmat, ada_w, ada_b):
    nl, D, n = ada_w.shape
    tn = _row_tile(n, 512)

    def body(c_ref, w_ref, b_ref, o_ref):
        a = _silu_rows(c_ref).astype(_ACT)
        o_ref[...] = jnp.dot(a, w_ref[...].astype(_ACT), preferred_element_type=F32) + b_ref[...]

    return pl.pallas_call(body, name="ada_fwd", grid=(nl, n // tn),
                          in_specs=[pl.BlockSpec((ADA_ROWS, D), lambda l, j: (0, 0)),
                                    pl.BlockSpec((None, D, tn), lambda l, j: (l, 0, j)),
                                    pl.BlockSpec((None, 1, tn), lambda l, j: (l, 0, j))],
                          out_specs=pl.BlockSpec((None, ADA_ROWS, tn), lambda l, j: (l, 0, j)),
                          out_shape=jax.ShapeDtypeStruct((nl, ADA_ROWS, n), F32),
                          compiler_params=_params(("parallel", "parallel")))(cmat, ada_w, ada_b)


def _ada_bwd(cmat, ada_w, dm):
    nl, D, n = ada_w.shape
    tn = _row_tile(n, 512)
    nj = n // tn

    def body(c_ref, w_ref, dm_ref, dw_ref, dc_ref):
        c = c_ref[...]
        s = _sigmoid(c)
        a = (c * s).astype(_ACT)
        dmb = dm_ref[...].astype(_ACT)
        dw_ref[...] = lax.dot_general(a, dmb, (((0,), (0,)), ((), ())), preferred_element_type=F32)
        part = lax.dot_general(dmb, w_ref[...].astype(_ACT), (((1,), (1,)), ((), ())), preferred_element_type=F32)
        part = part * (s * (1.0 + c * (1.0 - s)))

        @pl.when(pl.program_id(1) == 0)
        def _():
            dc_ref[...] = part

        @pl.when(pl.program_id(1) > 0)
        def _():
            dc_ref[...] += part

    return pl.pallas_call(body, name="ada_bwd", grid=(nl, nj),
                          in_specs=[pl.BlockSpec((ADA_ROWS, D), lambda l, j: (0, 0)),
                                    pl.BlockSpec((None, D, tn), lambda l, j: (l, 0, j)),
                                    pl.BlockSpec((None, ADA_ROWS, tn), lambda l, j: (l, 0, j))],
                          out_specs=[pl.BlockSpec((None, D, tn), lambda l, j: (l, 0, j)),
                                     pl.BlockSpec((None, ADA_ROWS, D), lambda l, j: (l, 0, 0))],
                          out_shape=[jax.ShapeDtypeStruct((nl, D, n), F32), jax.ShapeDtypeStruct((nl, ADA_ROWS, D), F32)],
                          compiler_params=_params(("parallel", "arbitrary")))(cmat, ada_w, dm)


def _adamw(name, gparts, w, m, v):
    nl, R, C = w.shape
    n = gparts[0].shape[0]
    tr = R
    for cand in (512, 256, 128, 64, 32, 16, 8):
        if R % cand == 0 and cand * C * 4 <= 2 * 1024 * 1024:
            tr = cand
            break
    nt = R // tr
    bc1 = 1.0 - ADAM_B1 ** ADAM_STEP
    bc2 = 1.0 - ADAM_B2 ** ADAM_STEP

    def body(*refs):
        g_refs = refs[:nl]
        w_ref, m_ref, v_ref, go_ref, d_ref, mo_ref, vo_ref = refs[nl:]
        for l in range(nl):
            @pl.when(pl.program_id(0) == l)
            def _(g_ref=g_refs[l]):
                g = g_ref[0].astype(F32)
                for j in range(1, n):
                    g = g + g_ref[j].astype(F32)
                m2 = ADAM_B1 * m_ref[...] + (1.0 - ADAM_B1) * g
                v2 = ADAM_B2 * v_ref[...] + (1.0 - ADAM_B2) * (g * g)
                go_ref[...] = g
                mo_ref[...] = m2
                vo_ref[...] = v2
                d_ref[...] = -ADAM_LR * ((m2 / bc1) / (jnp.sqrt(v2 / bc2) + ADAM_EPS) + ADAM_WD * w_ref[...])

    def gspec(l):
        return pl.BlockSpec((n, tr, C), lambda lyr, i: (0, jnp.where(lyr < l, 0, jnp.where(lyr > l, nt - 1, i)), 0))

    row = pl.BlockSpec((None, tr, C), lambda lyr, i: (lyr, i, 0))
    out = jax.ShapeDtypeStruct((nl, R, C), F32)
    return _pcall(body, name=name, grid=(nl, nt), in_specs=[gspec(l) for l in range(nl)] + [row, row, row],
                  out_specs=[row, row, row, row], out_shape=[out, out, out, out], ins=(*gparts, w, m, v),
                  sem=("arbitrary", "arbitrary"))


def _sum_parts(name, parts):
    n, R, C = parts.shape

    def body(p_ref, o_ref):
        s = p_ref[0]
        for j in range(1, n):
            s = s + p_ref[j]
        o_ref[...] = s

    return pl.pallas_call(body, name=name, out_shape=jax.ShapeDtypeStruct((R, C), F32),
                          compiler_params=_params(None))(parts)


def _discretize(lam_re, lam_im, log_step, b_re, b_im):
    dt = jnp.exp(log_step)[:, None]
    mag = jnp.exp(lam_re * dt)
    abar_re = mag * jnp.cos(lam_im * dt)
    abar_im = mag * jnp.sin(lam_im * dt)
    nr, ni = abar_re - 1.0, abar_im
    den = lam_re * lam_re + lam_im * lam_im
    fr = (nr * lam_re + ni * lam_im) / den
    fi = (ni * lam_re - nr * lam_im) / den
    bbar_re = fr[..., None] * b_re - fi[..., None] * b_im
    bbar_im = fr[..., None] * b_im + fi[..., None] * b_re
    return abar_re, abar_im, bbar_re, bbar_im


def _s5_pack(abar, bbar, cmat):
    G = abar[0][0].shape[0]
    NT = G // S5_TILE_G
    eye = jnp.eye(S5_TILE_G, dtype=F32)
    rows, bs, cs = [], [], []
    for d in range(2):
        for r in range(2):
            rows.append(abar[d][r].reshape(NT, 1, S5_TILE_W))
            bb = bbar[d][r].reshape(NT, S5_TILE_G, S5_P, S5_CH)
            bs.append(jnp.einsum("jgpc,gh->jgchp", bb, eye).reshape(NT, S5_TILE_CH, S5_TILE_W))
            cc = cmat[d][r].reshape(NT, S5_TILE_G, S5_CH, S5_P)
            cs.append(jnp.einsum("jgcp,gh->jgphc", cc, eye).reshape(NT, S5_TILE_W, S5_TILE_CH))
    apar = jnp.concatenate(rows + [jnp.zeros((NT, 4, S5_TILE_W), F32)], axis=1)
    return apar, jnp.stack(bs, axis=1).astype(_ACT), jnp.stack(cs, axis=1).astype(_ACT)


def _s5_unpack(dapar, dbblk, dcblk, G):
    NT = G // S5_TILE_G
    da = dapar[:, :4, :].reshape(NT, 2, 2, S5_TILE_G, S5_P).transpose(1, 2, 0, 3, 4).reshape(2, 2, G, S5_P)
    idx = jnp.arange(S5_TILE_G)
    db = dbblk.reshape(NT, 2, 2, S5_TILE_G, S5_CH, S5_TILE_G, S5_P)[:, :, :, idx, :, idx, :]
    db = db.transpose(2, 3, 1, 0, 5, 4).reshape(2, 2, G, S5_P, S5_CH)
    dc = dcblk.reshape(NT, 2, 2, S5_TILE_G, S5_P, S5_TILE_G, S5_CH)[:, :, :, idx, :, idx, :]
    dc = dc.transpose(2, 3, 1, 0, 5, 4).reshape(2, 2, G, S5_CH, S5_P)
    return da, db, dc


def _to_segments(a):
    L, D = a.shape
    return a.reshape(NSEG, L // NSEG, D).transpose(1, 0, 2).reshape(L, D)


def _from_segments(a):
    L, D = a.shape
    return a.reshape(L // NSEG, NSEG, D).transpose(1, 0, 2).reshape(L, D)


def _pos_emb(n_tokens, dim):
    rows = n_tokens // GRID_W
    quarter = dim // 4
    omega = 1.0 / (POS_BASE ** (jnp.arange(quarter, dtype=F32) / quarter))

    def enc(p):
        ang = p[:, None] * omega[None, :]
        return jnp.concatenate([jnp.sin(ang), jnp.cos(ang)], axis=-1)

    rtab = enc(jnp.arange(rows, dtype=F32))
    ctab = enc(jnp.arange(GRID_W, dtype=F32))
    return jnp.concatenate([jnp.repeat(rtab, GRID_W, axis=0), jnp.tile(ctab, (rows, 1))], axis=-1)


def _vec(D, **rows):
    names = {"gpost": V_GPOST, "gate": V_GATE, "yscale": V_YSCALE, "gpre": V_GPRE, "shift": V_SHIFT, "scale": V_SCALE}
    out = [jnp.zeros((D,), F32)] * 8
    out[V_YSCALE] = jnp.ones((D,), F32)
    for k, v in rows.items():
        out[names[k]] = v.reshape(D).astype(F32)
    return jnp.stack(out)


def _row0(v, D):
    return jnp.concatenate([v.reshape(1, D).astype(F32), jnp.zeros((7, D), F32)], axis=0)


def _my_block(full, axis, n_local):
    return lax.dynamic_slice_in_dim(full, _my_index() * n_local, n_local, axis)


def kernel(x, c, ctx, c_ctx, ada_w, ada_b, norm_g, s5_lam_re, s5_lam_im, s5_log_step, s5_b_re, s5_b_im, s5_c_re, s5_c_im, s5_d, s5_glu_w, pool_w, pool_scale, ffn_up, ffn_conv, ffn_conv_b, ffn_down, loss_target, m_c_ctx, m_ada_w, m_ada_b, m_norm_g, m_s5_lam_re, m_s5_lam_im, m_s5_log_step, m_s5_b_re, m_s5_b_im, m_s5_c_re, m_s5_c_im, m_s5_d, m_s5_glu_w, m_pool_w, m_pool_scale, m_ffn_up, m_ffn_conv, m_ffn_conv_b, m_ffn_down, v_c_ctx, v_ada_w, v_ada_b, v_norm_g, v_s5_lam_re, v_s5_lam_im, v_s5_log_step, v_s5_b_re, v_s5_b_im, v_s5_c_re, v_s5_c_im, v_s5_d, v_s5_glu_w, v_pool_w, v_pool_scale, v_ffn_up, v_ffn_conv, v_ffn_conv_b, v_ffn_down):
    L, D = x.shape[1], x.shape[2]
    LC = ctx.shape[1]
    G = s5_lam_re.shape[2]
    n_ada = ada_w.shape[2]
    nb_up = ffn_up.shape[2]
    r_down = ffn_down.shape[1]
    FF = N_DEV * r_down
    n_pool = len(POOL_WINDOWS)
    pc = D // n_pool
    pr = pool_w.shape[2]
    ng_loc = norm_g.shape[2]
    me = _my_index()
    axes = ("x", "y", "c")

    up_b = [ffn_up[i].astype(_ACT) for i in range(2)]
    down_b = [ffn_down[i].astype(_ACT) for i in range(2)]
    glu_b = s5_glu_w[0].astype(_ACT)
    pool_b = pool_w[0].reshape(n_pool * pr, pc).astype(_ACT)

    small_loc = jnp.concatenate([c.reshape(-1), norm_g.reshape(-1), pool_scale.reshape(-1), ffn_conv.reshape(-1)])
    n_small = small_loc.shape[0]
    small_g, = _exchange([[jnp.pad(small_loc, (0, (-n_small) % LANE)).reshape(1, -1)]], gather=True, name="gather_small")
    small_g = small_g.reshape(N_DEV, -1)
    o = 0
    c_all = small_g[:, o:o + D]
    o += D
    ng_all = small_g[:, o:o + 8 * ng_loc].reshape(N_DEV, 2, 4, ng_loc).transpose(1, 2, 0, 3).reshape(2, 4, D)
    o += 8 * ng_loc
    pscale_all = small_g[:, o:o + ng_loc].reshape(D)
    o += ng_loc
    conv_all = small_g[:, o:o + 6 * nb_up].reshape(N_DEV, 2, 3, nb_up).transpose(1, 2, 0, 3).reshape(2, 3, 2 * FF)

    cmat = jnp.concatenate([c_all, c_ctx.reshape(1, D), jnp.zeros((ADA_ROWS - N_DEV - 1, D), F32)], axis=0)
    ada_b_loc = _my_block(ada_b, 1, n_ada).reshape(2, 1, n_ada)
    mods_loc = _ada_fwd(cmat, ada_w, ada_b_loc)
    mods_g, = _exchange([[mods_loc]], gather=True, name="gather_mods")
    mods_rows = mods_g.reshape(N_DEV, 2, ADA_ROWS, n_ada).transpose(1, 2, 0, 3).reshape(2, ADA_ROWS, 6, D)
    mod = lax.dynamic_index_in_dim(mods_rows, me, axis=1, keepdims=False)
    mod_c = mods_rows[0, N_DEV]

    def disc_all(lr, li, ls, br, bi):
        return [_discretize(lr[d], li[d], ls[d], br[d], bi[d]) for d in range(2)]

    disc, disc_vjp = jax.vjp(disc_all, s5_lam_re[0], s5_lam_im[0], s5_log_step[0], s5_b_re[0], s5_b_im[0])
    apar, bblk, cblk = _s5_pack([(disc[d][0], disc[d][1]) for d in range(2)],
                                [(disc[d][2], disc[d][3]) for d in range(2)],
                                [(s5_c_re[0, d], s5_c_im[0, d]) for d in range(2)])
    dsk = _row0(s5_d[0], D)
    cw = []
    for i in range(2):
        taps = conv_all[i].reshape(3, 2, FF).transpose(1, 0, 2)
        cw.append(jnp.concatenate([taps, ffn_conv_b[i].reshape(2, 1, FF), jnp.zeros((2, 4, FF), F32)], axis=1))

    vecs = {
        "b0": _vec(D, gpre=ng_all[0, 0], shift=mod[0, 0], scale=mod[0, 1]),
        "c0": _vec(D, gpre=ng_all[0, 0], shift=mod_c[0], scale=mod_c[1]),
        "b1": _vec(D, gpost=ng_all[0, 1], gate=mod[0, 2], gpre=ng_all[0, 2], shift=mod[0, 3], scale=mod[0, 4]),
        "b2": _vec(D, gpost=ng_all[0, 3], gate=mod[0, 5], gpre=ng_all[1, 0], shift=mod[1, 0], scale=mod[1, 1]),
        "b3": _vec(D, gpost=ng_all[1, 1], gate=mod[1, 2], yscale=pscale_all, gpre=ng_all[1, 2], shift=mod[1, 3],
                   scale=mod[1, 4]),
        "b4": _vec(D, gpost=ng_all[1, 3], gate=mod[1, 5]),
    }

    x0, u0 = _rows_fwd("rows_fwd_b0", x[0], _pos_emb(L, D), vecs["b0"], add=True, u_dtype=_ACT)
    uc, = _rows_fwd("rows_fwd_ctx", ctx[0], None, vecs["c0"], want_x=False, u_dtype=_ACT)
    u0s, ucs = _to_segments(u0), _to_segments(uc)
    (y_s5, z_s5), (up_g0, down_g0, glu_g) = _s5_fwd(u0s, ucs, bblk, cblk, apar, dsk,
                                                    rider=([[up_b[0]], [down_b[0]], [glu_b]], True))
    vg = _colblock_fwd("glu_fwd_mm", z_s5, glu_g, 0, _ACT)
    mix0 = _from_segments(_glu_fwd("glu_fwd", vg))
    x1, un0 = _rows_fwd("rows_fwd_b1", x0, mix0, vecs["b1"], u_dtype=_ACT)
    h0, (up_g1,) = _colblock_fwd("ffn0_up", un0, up_g0, 0, _ACT, rider=([[up_b[1]]], True))
    act0 = _conv_swiglu_fwd("ffn0_conv", h0, cw[0])
    f0, (down_g1, pool_g) = _rowblock_fwd("ffn0_down", act0, down_g0, 0, rider=([[down_b[1]], [pool_b]], True))
    pool_full = pool_g.reshape(N_DEV, n_pool, pr, pc).transpose(1, 0, 2, 3).reshape(n_pool, pc, pc)
    x2, u1 = _rows_fwd("rows_fwd_b2", x1, f0, vecs["b2"], u_dtype=F32)
    p1 = _pool_window("pool_fwd", u1, False, _ACT)
    ypre1 = _group_mm("pool_fwd_mm", p1, pool_full, "nn", F32)
    x3, un1 = _rows_fwd("rows_fwd_b3", x2, ypre1, vecs["b3"], u_dtype=_ACT)
    h1 = _colblock_fwd("ffn1_up", un1, up_g1, 0, _ACT)
    act1 = _conv_swiglu_fwd("ffn1_conv", h1, cw[1])
    f1 = _rowblock_fwd("ffn1_down", act1, down_g1, 0)
    dx4, loss_blk = _rows_fwd("rows_fwd_b4", x3, f1, vecs["b4"], target=loss_target[0])
    loss = lax.psum(loss_blk[0, 0], axes)

    df1, red4 = _rows_bwd("rows_bwd_b4", dx4, None, None, f1, vecs["b4"], dy_dtype=_ACT, want_dx=False)
    dact1 = _rowblock_dgrad("ffn1_down_dgrad", df1, down_g1, 0)
    ddown1 = _rowblock_wgrad("ffn1_down_wgrad", act1, df1)
    (dh1, dcw1), (gp_down1,) = _conv_swiglu_bwd("ffn1_conv_bwd", h1, cw[1], dact1,
                                                rider=([[ddown1.reshape(N_DEV, r_down, D)]], False))
    dun1 = _colblock_dgrad("ffn1_up_dgrad", dh1, up_g1, 0, F32)
    dup1 = _colblock_wgrad("ffn1_up_wgrad", un1, dh1)
    dx3, dy3, red3 = _rows_bwd("rows_bwd_b3", dx4, dun1, x3, ypre1, vecs["b3"])
    dypre1, red_ps = _colscale_bwd("pool_scale_bwd", dy3, ypre1, _row0(pscale_all, D))
    dp1 = _group_mm("pool_dgrad", dypre1, pool_full, "nt", F32)
    dpool = _group_wgrad("pool_wgrad", p1, dypre1, n_pool)
    du1 = _pool_window("pool_bwd", dp1, True, F32)
    dx2, df0, red2 = _rows_bwd("rows_bwd_b2", dx3, du1, x2, f0, vecs["b2"], dy_dtype=_ACT)
    dact0 = _rowblock_dgrad("ffn0_down_dgrad", df0, down_g0, 0)
    ddown0 = _rowblock_wgrad("ffn0_down_wgrad", act0, df0)
    dh0, dcw0 = _conv_swiglu_bwd("ffn0_conv_bwd", h0, cw[0], dact0)
    dun0, (gp_up1,) = _colblock_dgrad("ffn0_up_dgrad", dh0, up_g0, 0, F32, rider=([[dup1]], False))
    dup0, (gp_down0,) = _colblock_wgrad("ffn0_up_wgrad", un0, dh0,
                                        rider=([[ddown0.reshape(N_DEV, r_down, D)]], False))
    dx1, dmix0, red1 = _rows_bwd("rows_bwd_b1", dx2, dun0, x1, mix0, vecs["b1"])
    dvg = _glu_bwd("glu_bwd", vg, _to_segments(dmix0))
    dz = _colblock_dgrad("glu_dgrad", dvg, glu_g, 0, _ACT)
    dglu = _colblock_wgrad("glu_wgrad", z_s5, dvg)
    dpool_blocks = dpool.reshape(n_pool, N_DEV, pr, pc).transpose(1, 0, 2, 3).reshape(N_DEV, n_pool * pr, pc)
    (du0s, ducs, dbblk, dcblk, dapar, ddsk), (gp_up0, gp_glu, gp_pool) = _s5_bwd(
        u0s, ucs, dz, y_s5, bblk, cblk, apar, dsk, rider=([[dup0], [dglu], [dpool_blocks]], False))
    grad_x, red0 = _rows_bwd("rows_bwd_b0", dx1, _from_segments(du0s), x0, None, vecs["b0"])
    redc, = _rows_bwd("rows_bwd_ctx", None, _from_segments(ducs), ctx[0], None, vecs["c0"], want_dx=False)

    zero_d = jnp.zeros((D,), F32)
    dmod = jnp.stack([
        jnp.stack([red0[R_SHIFT], red0[R_SCALE], red1[R_GATE], red1[R_SHIFT], red1[R_SCALE], red2[R_GATE]]),
        jnp.stack([red2[R_SHIFT], red2[R_SCALE], red3[R_GATE], red3[R_SHIFT], red3[R_SCALE], red4[R_GATE]])])
    dmod_c = jnp.stack([jnp.stack([redc[R_SHIFT], redc[R_SCALE]] + [zero_d] * 4), jnp.zeros((6, D), F32)])
    dm_g, = _exchange([[jnp.stack([dmod, dmod_c], axis=1).reshape(2, 2, 6 * D)]], gather=True, name="gather_dmods")
    dm_g = dm_g.reshape(N_DEV, 2, 2, 6 * D)
    dm_ctx = _sum_parts("sum_dmod_ctx", dm_g[:, :, 1, :])
    dm_rows = jnp.concatenate([dm_g[:, :, 0, :].transpose(1, 0, 2), dm_ctx[:, None, :]], axis=1)
    grad_ada_b = _sum_parts("sum_ada_b", dm_rows.transpose(1, 0, 2))
    dm_cols = dm_rows.reshape(2, N_DEV + 1, N_DEV, n_ada)
    dm_mine = lax.dynamic_index_in_dim(dm_cols, me, axis=2, keepdims=False)
    dm_mine = jnp.concatenate([dm_mine, jnp.zeros((2, ADA_ROWS - N_DEV - 1, n_ada), F32)], axis=1)
    grad_ada_w, dcond = _ada_bwd(cmat, ada_w, dm_mine)
    dcctx_part = dcond[0, N_DEV] + dcond[1, N_DEV]

    da, db, dc = _s5_unpack(dapar, dbblk, dcblk, G)
    dnorm = jnp.stack([
        jnp.stack([red0[R_GPRE] + redc[R_GPRE], red1[R_GPOST], red1[R_GPRE], red2[R_GPOST]]),
        jnp.stack([red2[R_GPRE], red3[R_GPOST], red3[R_GPRE], red4[R_GPOST]])])
    dconv = jnp.stack([d[:, :3, :].transpose(1, 0, 2).reshape(3, 2 * FF) for d in (dcw0, dcw1)])
    dconv_b = jnp.stack([d[:, 3, :].reshape(2 * FF) for d in (dcw0, dcw1)])
    pieces = [dcctx_part, dnorm, da, db, dc, ddsk[:, 0, :], red_ps[0], dconv, dconv_b]
    flat = jnp.concatenate([p.reshape(-1) for p in pieces])
    n_flat = flat.shape[0]
    per_dev = -(-n_flat // (N_DEV * 8 * LANE)) * 8 * LANE
    flat = jnp.pad(flat, (0, N_DEV * per_dev - n_flat)).reshape(N_DEV, per_dev // LANE, LANE)
    parts, = _exchange([[flat]], gather=False, name="scatter_small_grads")
    mine = _sum_parts("sum_small_grads", parts.reshape(N_DEV, per_dev // LANE, LANE))
    summed, = _exchange([[mine]], gather=True, name="gather_small_grads")
    summed = summed.reshape(-1)
    red_pieces, o = [], 0
    for p in pieces:
        red_pieces.append(summed[o:o + p.size].reshape(p.shape))
        o += p.size
    g_cctx, g_norm, g_a, g_b, g_c, g_d, g_pscale, g_conv, g_conv_b = red_pieces
    cot = [(g_a[d, 0], g_a[d, 1], g_b[d, 0], g_b[d, 1]) for d in range(2)]
    g_lam_re, g_lam_im, g_log_step, g_b_re, g_b_im = disc_vjp(cot)

    out = {}

    def put(name, res, shape):
        out[name] = tuple(r.reshape(shape) for r in res)

    put("ffn_up", _adamw("adamw_ffn_up", [g.reshape(N_DEV, D, nb_up) for g in (gp_up0, gp_up1)],
                         ffn_up, m_ffn_up, v_ffn_up), ffn_up.shape)
    put("ffn_down", _adamw("adamw_ffn_down", [g.reshape(N_DEV, r_down, D) for g in (gp_down0, gp_down1)],
                           ffn_down, m_ffn_down, v_ffn_down), ffn_down.shape)
    put("s5_glu_w", _adamw("adamw_glu", [gp_glu.reshape(N_DEV, D, -1)], s5_glu_w, m_s5_glu_w, v_s5_glu_w),
        s5_glu_w.shape)
    pool_rows = (1, n_pool * pr, pc)
    put("pool_w", _adamw("adamw_pool", [gp_pool.reshape(N_DEV, n_pool * pr, pc)], pool_w.reshape(pool_rows),
                         m_pool_w.reshape(pool_rows), v_pool_w.reshape(pool_rows)), pool_w.shape)
    put("ada_w", _adamw("adamw_ada_w", [grad_ada_w[i][None] for i in range(2)], ada_w, m_ada_w, v_ada_w), ada_w.shape)

    small = [
        ("c_ctx", c_ctx, m_c_ctx, v_c_ctx, g_cctx),
        ("ada_b", ada_b, m_ada_b, v_ada_b, grad_ada_b),
        ("norm_g", norm_g, m_norm_g, v_norm_g, _my_block(g_norm, 2, ng_loc)),
        ("s5_lam_re", s5_lam_re, m_s5_lam_re, v_s5_lam_re, g_lam_re),
        ("s5_lam_im", s5_lam_im, m_s5_lam_im, v_s5_lam_im, g_lam_im),
        ("s5_log_step", s5_log_step, m_s5_log_step, v_s5_log_step, g_log_step),
        ("s5_b_re", s5_b_re, m_s5_b_re, v_s5_b_re, g_b_re),
        ("s5_b_im", s5_b_im, m_s5_b_im, v_s5_b_im, g_b_im),
        ("s5_c_re", s5_c_re, m_s5_c_re, v_s5_c_re, g_c[:, 0]),
        ("s5_c_im", s5_c_im, m_s5_c_im, v_s5_c_im, g_c[:, 1]),
        ("s5_d", s5_d, m_s5_d, v_s5_d, g_d),
        ("pool_scale", pool_scale, m_pool_scale, v_pool_scale, _my_block(g_pscale, 0, ng_loc)),
        ("ffn_conv", ffn_conv, m_ffn_conv, v_ffn_conv, _my_block(g_conv, 2, nb_up)),
        ("ffn_conv_b", ffn_conv_b, m_ffn_conv_b, v_ffn_conv_b, g_conv_b),
    ]
    n_sm = sum(w.size for _, w, _, _, _ in small)
    rows_sm = -(-n_sm // (512 * LANE)) * 512

    def flat_of(k):
        f = jnp.concatenate([t[k].reshape(-1) for t in small])
        return jnp.pad(f, (0, rows_sm * LANE - n_sm)).reshape(rows_sm, LANE)

    res_sm = _adamw("adamw_small", [flat_of(4)[None]], flat_of(1)[None], flat_of(2)[None], flat_of(3)[None])
    o = 0
    for name, w, _, _, _ in small:
        out[name] = tuple(r.reshape(-1)[o:o + w.size].reshape(w.shape) for r in res_sm)
        o += w.size

    order = ["c_ctx", "ada_w", "ada_b", "norm_g", "s5_lam_re", "s5_lam_im", "s5_log_step", "s5_b_re", "s5_b_im",
             "s5_c_re", "s5_c_im", "s5_d", "s5_glu_w", "pool_w", "pool_scale", "ffn_up", "ffn_conv", "ffn_conv_b",
             "ffn_down"]
    return (loss, grad_x.reshape(x.shape), *[out[n][0] for n in order], *[out[n][1] for n in order],
            *[out[n][2] for n in order], *[out[n][3] for n in order])
```

```python
import functools
import math

import jax
import jax.numpy as jnp
from jax import lax
from jax.experimental import pallas as pl
from jax.experimental.pallas import tpu as pltpu

F32 = jnp.float32
_ACT = jnp.bfloat16
N_DEV = 8
NSEG = 8
S5_CH = 16
S5_P = 64
LANE = 128
S5_TILE_CH = LANE
S5_TILE_G = S5_TILE_CH // S5_CH
S5_TILE_W = S5_TILE_G * S5_P
GRID_W = 64
POOL_WINDOWS = (2, 4, 8, 16)
POOL_HALO = 64
RMS_EPS = 1e-6
POS_BASE = 10000.0
ADAM_LR, ADAM_B1, ADAM_B2, ADAM_EPS, ADAM_WD, ADAM_STEP = 0.001, 0.9, 0.999, 1e-08, 0.01, 10
VMEM_LIMIT = 48 * 1024 * 1024
VMEM_LIMIT_BIG = 58 * 1024 * 1024
MESH = pl.DeviceIdType.MESH
ANY = pl.BlockSpec(memory_space=pl.ANY)


def _params(sem, vmem=VMEM_LIMIT):
    return pltpu.CompilerParams(dimension_semantics=sem, vmem_limit_bytes=vmem)


def _my_index():
    return 4 * lax.axis_index("x") + 2 * lax.axis_index("y") + lax.axis_index("c")


def _xchg_plan(groups, mode):
    flat = [(g, l, a) for g, grp in enumerate(groups) for l, a in enumerate(grp)]
    outs = []
    for grp in groups:
        piece = grp[0].shape[1:] if mode == "scatter" else grp[0].shape
        outs.append(jax.ShapeDtypeStruct((N_DEV, len(grp)) + tuple(piece), grp[0].dtype))
    return flat, outs


def _xchg_sems(n):
    return [pltpu.SemaphoreType.DMA((n, N_DEV - 1)), pltpu.SemaphoreType.DMA((n, N_DEV - 1)),
            pltpu.SemaphoreType.DMA((n,))]


def _xchg_copies(flat, mode, ins, out_refs, sems, waiting=True):
    send_sems, recv_sems, local_sems = sems
    x, y, c = lax.axis_index("x"), lax.axis_index("y"), lax.axis_index("c")
    me = 4 * x + 2 * y + c
    local, first, forwards = [], [], []

    def pair(s, j, dev):
        return dict(send_sem=send_sems.at[s, j], recv_sem=recv_sems.at[s, j], device_id=dev, device_id_type=MESH)

    for s, (g, l, _) in enumerate(flat):
        src = ins[s].at[me] if mode == "scatter" else ins[s]
        local.append(pltpu.make_async_copy(src, out_refs[g].at[me, l], local_sems.at[s]))
    if mode == "gather2":
        sib, sib_idx = (x, y, 1 - c), 4 * x + 2 * y + (1 - c)
        for s, (g, l, _) in enumerate(flat):
            slot = lambda dev, g=g, l=l: out_refs[g].at[dev, l]
            targets = [(sib, sib_idx)] + [((qx, qy, c), 4 * qx + 2 * qy + c)
                                          for qx, qy in ((1 - x, y), (x, 1 - y), (1 - x, 1 - y))]
            for j, (dev, idx) in enumerate(targets):
                send = pltpu.make_async_remote_copy(src_ref=ins[s], dst_ref=slot(me), **pair(s, j, dev))
                arrive = pltpu.make_async_remote_copy(src_ref=ins[s], dst_ref=slot(idx), **pair(s, j, dev)) if waiting else None
                first.append((send, arrive))
            if waiting:
                for j, (dev, idx) in enumerate(targets[1:]):
                    other = 4 * dev[0] + 2 * dev[1] + (1 - c)
                    send = pltpu.make_async_remote_copy(src_ref=slot(idx), dst_ref=slot(idx), **pair(s, 4 + j, sib))
                    arrive = pltpu.make_async_remote_copy(src_ref=slot(idx), dst_ref=slot(other), **pair(s, 4 + j, sib))
                    forwards.append((first[len(first) - 3 + j][1], send, arrive))
        return local, first, forwards
    for k in range(1, N_DEV):
        px = 1 - x if k & 4 else x
        py = 1 - y if k & 2 else y
        pc = 1 - c if k & 1 else c
        peer = 4 * px + 2 * py + pc
        for s, (g, l, _) in enumerate(flat):
            src = ins[s].at[peer] if mode == "scatter" else ins[s]
            send = pltpu.make_async_remote_copy(src_ref=src, dst_ref=out_refs[g].at[me, l], **pair(s, k - 1, (px, py, pc)))
            arrive = (pltpu.make_async_remote_copy(src_ref=src, dst_ref=out_refs[g].at[peer, l],
                                                   **pair(s, k - 1, (px, py, pc))) if waiting else None)
            first.append((send, arrive))
    return local, first, forwards


def _xchg_start(local, first, forwards):
    for cp in local:
        cp.start()
    for send, _ in first:
        send.start()


def _xchg_wait(local, first, forwards):
    gates = [gate for gate, _, _ in forwards]
    for gate, send, _ in forwards:
        gate.wait_recv()
        send.start()
    for _, arrive in first:
        if not any(arrive is gate for gate in gates):
            arrive.wait_recv()
    for _, _, arrive in forwards:
        arrive.wait_recv()
    for send, _ in first:
        send.wait_send()
    for _, send, _ in forwards:
        send.wait_send()
    for cp in local:
        cp.wait()


def _exchange(groups, mode, name):
    flat, outs = _xchg_plan(groups, mode)
    n = len(flat)

    def body(*refs):
        copies = _xchg_copies(flat, mode, refs[:n], refs[n:n + len(groups)], refs[n + len(groups):])
        _xchg_start(*copies)
        _xchg_wait(*copies)

    res = pl.pallas_call(body, name=name, out_shape=outs, in_specs=[ANY] * n, out_specs=[ANY] * len(groups),
                         scratch_shapes=_xchg_sems(n))(*[a for _, _, a in flat])
    return list(res)


def _pcall(body, *, name, grid, in_specs, out_specs, out_shape, ins, scratch_shapes=(), sem=None, vmem=VMEM_LIMIT,
           rider=None):
    single = not isinstance(out_shape, (list, tuple))
    if rider is None:
        return pl.pallas_call(body, name=name, grid=grid, in_specs=list(in_specs), out_specs=out_specs,
                              out_shape=out_shape, scratch_shapes=list(scratch_shapes),
                              compiler_params=_params(sem, vmem))(*ins)
    groups, mode = rider
    flat, r_outs = _xchg_plan(groups, mode)
    n_in, n_out = len(ins), 1 if single else len(out_shape)
    nr, ng, ns = len(flat), len(groups), len(scratch_shapes)

    def wrapped(*refs):
        o1 = n_in + nr
        o2 = o1 + n_out
        o3 = o2 + ng
        r_in, r_out, sems = refs[n_in:o1], refs[o2:o3], refs[o3 + ns:]
        first = functools.reduce(jnp.logical_and, [pl.program_id(d) == 0 for d in range(len(grid))])
        last = functools.reduce(jnp.logical_and, [pl.program_id(d) == grid[d] - 1 for d in range(len(grid))])

        @pl.when(first)
        def _():
            _xchg_start(*_xchg_copies(flat, mode, r_in, r_out, sems, waiting=False))

        body(*refs[:n_in], *refs[o1:o2], *refs[o3:o3 + ns])

        @pl.when(last)
        def _():
            _xchg_wait(*_xchg_copies(flat, mode, r_in, r_out, sems))

    outs = pl.pallas_call(
        wrapped, name=name, grid=grid, in_specs=list(in_specs) + [ANY] * nr,
        out_specs=([out_specs] if single else list(out_specs)) + [ANY] * ng,
        out_shape=([out_shape] if single else list(out_shape)) + r_outs,
        scratch_shapes=list(scratch_shapes) + _xchg_sems(nr),
        compiler_params=_params(("arbitrary",) * len(grid), vmem))(*ins, *[a for _, _, a in flat])
    base = list(outs[:n_out])
    return (base[0] if single else base), list(outs[n_out:])


_DIMS = {"nn": (((1,), (0,)), ((), ())), "nt": (((1,), (1,)), ((), ())), "tn": (((0,), (0,)), ((), ()))}


def _mm(name, a, b, a_spec, b_spec, o_spec, out_shape, grid, dims, rider=None):
    nk = grid[2]
    acc_shape = tuple(d for d in o_spec.block_shape if d is not None)
    dn = _DIMS[dims]

    def tile(ref):
        v = ref[...]
        return v.reshape((-1, v.shape[-1])).astype(_ACT)

    def body(a_ref, b_ref, o_ref, *scratch):
        part = lax.dot_general(tile(a_ref), tile(b_ref), dn, preferred_element_type=F32)
        if nk == 1:
            o_ref[...] = part.reshape(o_ref.shape).astype(o_ref.dtype)
            return
        acc_ref, = scratch
        k = pl.program_id(2)

        @pl.when(k == 0)
        def _():
            acc_ref[...] = part

        @pl.when(k > 0)
        def _():
            acc_ref[...] += part

        @pl.when(k == nk - 1)
        def _():
            o_ref[...] = acc_ref[...].reshape(o_ref.shape).astype(o_ref.dtype)

    acc2d = (math.prod(acc_shape[:-1]), acc_shape[-1])
    return _pcall(body, name=name, out_shape=out_shape, grid=grid, in_specs=[a_spec, b_spec], out_specs=o_spec,
                  scratch_shapes=[] if nk == 1 else [pltpu.VMEM(acc2d, F32)], ins=(a, b),
                  sem=("parallel", "parallel", "arbitrary"), rider=rider)


def _row_tile(n, want):
    t = min(n, want)
    assert n % t == 0, (n, t)
    return t


def _colblock_fwd(name, xa, wg, layer, out_dtype, rider=None):
    L, K = xa.shape
    nb = wg.shape[3]
    half = N_DEV // 2
    tm = _row_tile(L, 512)
    return _mm(name, xa, wg,
               pl.BlockSpec((tm, K), lambda j, i, k: (i, 0)),
               pl.BlockSpec((None, None, K, nb), lambda j, i, k: (j, layer, 0, 0)),
               pl.BlockSpec((None, tm, nb), lambda j, i, k: (j // half, i, j % half)),
               jax.ShapeDtypeStruct((2, L, half * nb), out_dtype), (N_DEV, L // tm, 1), "nn", rider=rider)


def _colblock_dgrad(name, dh, wg, layer, out_dtype, rider=None):
    _, L, _ = dh.shape
    K, nb = wg.shape[2], wg.shape[3]
    half = N_DEV // 2
    tm = _row_tile(L, 512)
    return _mm(name, dh, wg,
               pl.BlockSpec((None, tm, nb), lambda i, j, k: (k // half, i, k % half)),
               pl.BlockSpec((None, None, K, nb), lambda i, j, k: (k, layer, 0, 0)),
               pl.BlockSpec((tm, K), lambda i, j, k: (i, 0)),
               jax.ShapeDtypeStruct((L, K), out_dtype), (L // tm, 1, N_DEV), "nt", rider=rider)


def _colblock_wgrad(name, xa, dh, rider=None):
    L, K = xa.shape
    half = N_DEV // 2
    nb = dh.shape[2] // half
    tm = _row_tile(K, 1024)
    tk = _row_tile(L, 512)
    return _mm(name, xa, dh,
               pl.BlockSpec((tk, tm), lambda j, i, k: (k, i)),
               pl.BlockSpec((None, tk, nb), lambda j, i, k: (j // half, k, j % half)),
               pl.BlockSpec((None, tm, nb), lambda j, i, k: (j, i, 0)),
               jax.ShapeDtypeStruct((N_DEV, K, nb), _ACT), (N_DEV, K // tm, L // tk), "tn", rider=rider)


def _rowblock_fwd(name, xa, wg, layer, rider=None):
    L, FF = xa.shape
    r, D = wg.shape[2], wg.shape[3]
    tm = _row_tile(L, 512)
    return _mm(name, xa, wg,
               pl.BlockSpec((tm, 2 * r), lambda i, j, k: (i, k)),
               pl.BlockSpec((2, None, r, D), lambda i, j, k: (k, layer, 0, 0)),
               pl.BlockSpec((tm, D), lambda i, j, k: (i, 0)),
               jax.ShapeDtypeStruct((L, D), F32), (L // tm, 1, N_DEV // 2), "nn", rider=rider)


def _rowblock_dgrad(name, dy, wg, layer, rider=None):
    L, D = dy.shape
    r = wg.shape[2]
    tm = _row_tile(L, 512)
    return _mm(name, dy, wg,
               pl.BlockSpec((tm, D), lambda i, j, k: (i, 0)),
               pl.BlockSpec((2, None, r, D), lambda i, j, k: (j, layer, 0, 0)),
               pl.BlockSpec((tm, 2 * r), lambda i, j, k: (i, j)),
               jax.ShapeDtypeStruct((L, N_DEV * r), _ACT), (L // tm, N_DEV // 2, 1), "nt", rider=rider)


def _rowblock_wgrad(name, xa, dy, rider=None):
    L, FF = xa.shape
    D = dy.shape[1]
    tm = FF // (N_DEV // 2)
    tk = _row_tile(L, 512)
    return _mm(name, xa, dy,
               pl.BlockSpec((tk, tm), lambda i, j, k: (k, i)),
               pl.BlockSpec((tk, D), lambda i, j, k: (k, 0)),
               pl.BlockSpec((tm, D), lambda i, j, k: (i, 0)),
               jax.ShapeDtypeStruct((FF, D), _ACT), (FF // tm, 1, L // tk), "tn", rider=rider)


def _group_mm(name, xa, w, dims, out_dtype):
    L, D = xa.shape
    ng, pc, _ = w.shape
    tm = _row_tile(L, 512)
    return _mm(name, xa, w,
               pl.BlockSpec((tm, pc), lambda i, g, k: (i, g)),
               pl.BlockSpec((None, pc, pc), lambda i, g, k: (g, 0, 0)),
               pl.BlockSpec((tm, pc), lambda i, g, k: (i, g)),
               jax.ShapeDtypeStruct((L, D), out_dtype), (L // tm, ng, 1), dims)


def _group_wgrad(name, p, dy, ng):
    L, D = p.shape
    pc = D // ng
    tk = _row_tile(L, 512)
    return _mm(name, p, dy,
               pl.BlockSpec((tk, pc), lambda g, j, k: (k, g)),
               pl.BlockSpec((tk, pc), lambda g, j, k: (k, g)),
               pl.BlockSpec((None, pc, pc), lambda g, j, k: (g, 0, 0)),
               jax.ShapeDtypeStruct((ng, pc, pc), _ACT), (ng, 1, L // tk), "tn")


V_GPOST, V_GATE, V_YSCALE, V_GPRE, V_SHIFT, V_SCALE = range(6)
R_SHIFT, R_SCALE, R_GPRE, R_GATE, R_GPOST = range(5)
ROW_TILE = 256


def _rstd(v):
    return lax.rsqrt(jnp.mean(v * v, axis=-1, keepdims=True) + RMS_EPS)


def _rows_fwd(name, xres, y, vec, *, add=False, target=None, want_x=True, u_dtype=None):
    L, D = xres.shape
    tm = _row_tile(L, ROW_TILE)
    has_y = y is not None
    last = target is not None
    has_u = u_dtype is not None

    def body(*refs):
        refs = list(refs)
        xres_ref = refs.pop(0)
        y_ref = refs.pop(0) if has_y else None
        vec_ref = refs.pop(0)
        tgt_ref = refs.pop(0) if last else None
        xnew = xres_ref[...]
        if has_y and add:
            xnew = xnew + y_ref[...]
        elif has_y:
            ye = y_ref[...] * vec_ref[V_YSCALE:V_YSCALE + 1, :]
            xnew = xnew + vec_ref[V_GATE:V_GATE + 1, :] * (ye * _rstd(ye) * vec_ref[V_GPOST:V_GPOST + 1, :])
        if last:
            dx_ref, loss_ref = refs
            diff = xnew - tgt_ref[...]
            dx_ref[...] = diff * (1.0 / D)

            @pl.when(pl.program_id(0) == 0)
            def _():
                loss_ref[...] = jnp.zeros_like(loss_ref)

            loss_ref[...] += jnp.sum(diff * diff) * (0.5 / D)
            return
        if want_x:
            refs.pop(0)[...] = xnew
        if has_u:
            u_ref, = refs
            n = xnew * _rstd(xnew) * vec_ref[V_GPRE:V_GPRE + 1, :]
            u_ref[...] = (n * (1.0 + vec_ref[V_SCALE:V_SCALE + 1, :]) + vec_ref[V_SHIFT:V_SHIFT + 1, :]).astype(u_ref.dtype)

    row = pl.BlockSpec((tm, D), lambda i: (i, 0))
    vspec = pl.BlockSpec((8, D), lambda i: (0, 0))
    ins, in_specs = [xres], [row]
    if has_y:
        ins.append(y)
        in_specs.append(row)
    ins.append(vec)
    in_specs.append(vspec)
    out_shape, out_specs = [], []
    if last:
        ins.append(target)
        in_specs.append(row)
        out_shape = [jax.ShapeDtypeStruct((L, D), F32), jax.ShapeDtypeStruct((8, LANE), F32)]
        out_specs = [row, pl.BlockSpec((8, LANE), lambda i: (0, 0))]
    else:
        if want_x:
            out_shape.append(jax.ShapeDtypeStruct((L, D), F32))
            out_specs.append(row)
        if has_u:
            out_shape.append(jax.ShapeDtypeStruct((L, D), u_dtype))
            out_specs.append(row)
    return pl.pallas_call(body, name=name, out_shape=out_shape, grid=(L // tm,), in_specs=in_specs,
                          out_specs=out_specs, compiler_params=_params(("arbitrary",)))(*ins)


def _rows_bwd(name, dxd, du, xnew, y, vec, dy_dtype=F32, want_dx=True):
    L, D = xnew.shape if xnew is not None else dxd.shape
    tm = _row_tile(L, ROW_TILE)
    has_dxd, has_pre, has_post = dxd is not None, du is not None, y is not None

    def body(*refs):
        refs = list(refs)
        dxd_ref = refs.pop(0) if has_dxd else None
        du_ref = refs.pop(0) if has_pre else None
        xnew_ref = refs.pop(0) if has_pre else None
        y_ref = refs.pop(0) if has_post else None
        vec_ref = refs.pop(0)
        dx_ref = refs.pop(0) if want_dx else None
        dy_ref = refs.pop(0) if has_post else None
        red_ref, = refs

        @pl.when(pl.program_id(0) == 0)
        def _():
            red_ref[...] = jnp.zeros_like(red_ref)

        def acc(rw, val):
            red_ref[rw:rw + 1, :] += jnp.sum(val, axis=0, keepdims=True)

        dxn = dxd_ref[...] if has_dxd else None
        if has_pre:
            xn = xnew_ref[...]
            r = _rstd(xn)
            nh = xn * r
            gpre = vec_ref[V_GPRE:V_GPRE + 1, :]
            dub = du_ref[...].astype(F32)
            acc(R_SHIFT, dub)
            acc(R_SCALE, dub * (nh * gpre))
            drn = dub * (1.0 + vec_ref[V_SCALE:V_SCALE + 1, :])
            acc(R_GPRE, drn * nh)
            dnh = drn * gpre
            t = r * (dnh - nh * jnp.mean(dnh * nh, axis=-1, keepdims=True))
            dxn = t if dxn is None else dxn + t
        if want_dx:
            dx_ref[...] = dxn
        if has_post:
            ye = y_ref[...] * vec_ref[V_YSCALE:V_YSCALE + 1, :]
            ry = _rstd(ye)
            yh = ye * ry
            gpost = vec_ref[V_GPOST:V_GPOST + 1, :]
            acc(R_GATE, dxn * (yh * gpost))
            drn2 = dxn * vec_ref[V_GATE:V_GATE + 1, :]
            acc(R_GPOST, drn2 * yh)
            dyh = drn2 * gpost
            dy_ref[...] = (ry * (dyh - yh * jnp.mean(dyh * yh, axis=-1, keepdims=True))).astype(dy_ref.dtype)

    row = pl.BlockSpec((tm, D), lambda i: (i, 0))
    vspec = pl.BlockSpec((8, D), lambda i: (0, 0))
    ins, in_specs = [], []
    for a in ([dxd] if has_dxd else []) + ([du, xnew] if has_pre else []) + ([y] if has_post else []):
        ins.append(a)
        in_specs.append(row)
    ins.append(vec)
    in_specs.append(vspec)
    out_shape, out_specs = [], []
    if want_dx:
        out_shape.append(jax.ShapeDtypeStruct((L, D), F32))
        out_specs.append(row)
    if has_post:
        out_shape.append(jax.ShapeDtypeStruct((L, D), dy_dtype))
        out_specs.append(row)
    out_shape.append(jax.ShapeDtypeStruct((8, D), F32))
    out_specs.append(vspec)
    return pl.pallas_call(body, name=name, out_shape=out_shape, grid=(L // tm,), in_specs=in_specs,
                          out_specs=out_specs, compiler_params=_params(("arbitrary",)))(*ins)


def _colscale_bwd(name, dy, ypre, scale):
    L, D = dy.shape
    tm = _row_tile(L, ROW_TILE)

    def body(dy_ref, yp_ref, s_ref, o_ref, red_ref):
        @pl.when(pl.program_id(0) == 0)
        def _():
            red_ref[...] = jnp.zeros_like(red_ref)

        d = dy_ref[...]
        o_ref[...] = (d * s_ref[0:1, :]).astype(o_ref.dtype)
        red_ref[0:1, :] += jnp.sum(d * yp_ref[...], axis=0, keepdims=True)

    row = pl.BlockSpec((tm, D), lambda i: (i, 0))
    vspec = pl.BlockSpec((8, D), lambda i: (0, 0))
    return pl.pallas_call(body, name=name, grid=(L // tm,), in_specs=[row, row, vspec], out_specs=[row, vspec],
                          out_shape=[jax.ShapeDtypeStruct((L, D), _ACT), jax.ShapeDtypeStruct((8, D), F32)],
                          compiler_params=_params(("arbitrary",)))(dy, ypre, scale)


def _sigmoid(v):
    return 1.0 / (1.0 + jnp.exp(-v))


def _glu_fwd(name, vg):
    _, L, D = vg.shape
    tm = _row_tile(L, ROW_TILE)

    def body(vg_ref, o_ref):
        o_ref[...] = vg_ref[0].astype(F32) * _sigmoid(vg_ref[1].astype(F32))

    return pl.pallas_call(body, name=name, grid=(L // tm,),
                          in_specs=[pl.BlockSpec((2, tm, D), lambda i: (0, i, 0))],
                          out_specs=pl.BlockSpec((tm, D), lambda i: (i, 0)),
                          out_shape=jax.ShapeDtypeStruct((L, D), F32),
                          compiler_params=_params(("parallel",)))(vg)


def _glu_bwd(name, vg, dout):
    _, L, D = vg.shape
    tm = _row_tile(L, ROW_TILE)

    def body(vg_ref, d_ref, o_ref):
        val, s = vg_ref[0].astype(F32), _sigmoid(vg_ref[1].astype(F32))
        d = d_ref[...]
        o_ref[0] = (d * s).astype(o_ref.dtype)
        o_ref[1] = (d * val * s * (1.0 - s)).astype(o_ref.dtype)

    return pl.pallas_call(body, name=name, grid=(L // tm,),
                          in_specs=[pl.BlockSpec((2, tm, D), lambda i: (0, i, 0)), pl.BlockSpec((tm, D), lambda i: (i, 0))],
                          out_specs=pl.BlockSpec((2, tm, D), lambda i: (0, i, 0)),
                          out_shape=jax.ShapeDtypeStruct((2, L, D), _ACT),
                          compiler_params=_params(("parallel",)))(vg, dout)


CONV_ROWS = 256


def _row_pick(blk, idx):
    rows = lax.broadcasted_iota(jnp.int32, blk.shape, 0)
    return jnp.sum(jnp.where(rows == idx, blk, 0.0), axis=0, keepdims=True)


def _shifted(ref, r0, rc, L):
    cur = ref[pl.ds(r0, rc), :].astype(F32)
    before = ref[pl.ds(pl.multiple_of(jnp.maximum(r0 - 16, 0), 16), 16), :].astype(F32)
    after = ref[pl.ds(pl.multiple_of(jnp.minimum(r0 + rc, L - 16), 16), 16), :].astype(F32)
    prev_row = jnp.where(r0 > 0, _row_pick(before, 15), 0.0)
    next_row = jnp.where(r0 + rc < L, _row_pick(after, 0), 0.0)
    rows = lax.broadcasted_iota(jnp.int32, cur.shape, 0)
    up = jnp.where(rows == 0, prev_row, pltpu.roll(cur, 1, 0))
    down = jnp.where(rows == rc - 1, next_row, pltpu.roll(cur, rc - 1, 0))
    return up, cur, down


def _silu_parts(g):
    s = _sigmoid(g)
    return g * s, s


def _conv_swiglu_fwd(name, h, cw):
    _, L, FF = h.shape
    rc = _row_tile(L, CONV_ROWS)

    def body(h_ref, cw_ref, o_ref):
        def chunk(ci, _):
            r0 = pl.multiple_of(ci * rc, rc)
            hc = []
            for half in range(2):
                up, cur, down = _shifted(h_ref.at[half], r0, rc, L)
                hc.append(up * cw_ref[half, 0:1, :] + cur * cw_ref[half, 1:2, :] + down * cw_ref[half, 2:3, :]
                          + cw_ref[half, 3:4, :])
            o_ref[pl.ds(r0, rc), :] = (_silu_parts(hc[1])[0] * hc[0]).astype(o_ref.dtype)
            return 0

        lax.fori_loop(0, L // rc, chunk, 0)

    return pl.pallas_call(body, name=name, grid=(FF // LANE,),
                          in_specs=[pl.BlockSpec((2, L, LANE), lambda j: (0, 0, j)),
                                    pl.BlockSpec((2, 8, LANE), lambda j: (0, 0, j))],
                          out_specs=pl.BlockSpec((L, LANE), lambda j: (0, j)),
                          out_shape=jax.ShapeDtypeStruct((L, FF), _ACT),
                          compiler_params=_params(("parallel",)))(h, cw)


def _conv_swiglu_bwd(name, h, cw, dact, rider=None):
    _, L, FF = h.shape
    rc = _row_tile(L, CONV_ROWS)

    def body(h_ref, cw_ref, da_ref, dh_ref, dcw_ref, dhc_ref):
        def chunk(ci, acc):
            r0 = pl.multiple_of(ci * rc, rc)
            taps, hc = [], []
            for half in range(2):
                t = _shifted(h_ref.at[half], r0, rc, L)
                taps.append(t)
                hc.append(t[0] * cw_ref[half, 0:1, :] + t[1] * cw_ref[half, 1:2, :] + t[2] * cw_ref[half, 2:3, :]
                          + cw_ref[half, 3:4, :])
            d = da_ref[pl.ds(r0, rc), :].astype(F32)
            act, s = _silu_parts(hc[1])
            dhc = (d * act, d * hc[0] * (s + act * (1.0 - s)))
            new = []
            for half in range(2):
                dhc_ref[half, pl.ds(r0, rc), :] = dhc[half]
                for k in range(3):
                    new.append(acc[4 * half + k] + jnp.sum(dhc[half] * taps[half][k], axis=0, keepdims=True))
                new.append(acc[4 * half + 3] + jnp.sum(dhc[half], axis=0, keepdims=True))
            return tuple(new)

        zero = jnp.zeros((1, LANE), F32)
        acc = lax.fori_loop(0, L // rc, chunk, (zero,) * 8)
        dcw_ref[...] = jnp.zeros_like(dcw_ref)
        for half in range(2):
            for k in range(4):
                dcw_ref[half, k:k + 1, :] = acc[4 * half + k]

        def chunk2(ci, _):
            r0 = pl.multiple_of(ci * rc, rc)
            for half in range(2):
                up, cur, down = _shifted(dhc_ref.at[half], r0, rc, L)
                dh_ref[half, pl.ds(r0, rc), :] = (down * cw_ref[half, 0:1, :] + cur * cw_ref[half, 1:2, :]
                                                  + up * cw_ref[half, 2:3, :]).astype(dh_ref.dtype)
            return 0

        lax.fori_loop(0, L // rc, chunk2, 0)

    return _pcall(body, name=name, grid=(FF // LANE,),
                  in_specs=[pl.BlockSpec((2, L, LANE), lambda j: (0, 0, j)),
                            pl.BlockSpec((2, 8, LANE), lambda j: (0, 0, j)),
                            pl.BlockSpec((L, LANE), lambda j: (0, j))],
                  out_specs=[pl.BlockSpec((2, L, LANE), lambda j: (0, 0, j)),
                             pl.BlockSpec((2, 8, LANE), lambda j: (0, 0, j))],
                  out_shape=[jax.ShapeDtypeStruct((2, L, FF), _ACT), jax.ShapeDtypeStruct((2, 8, FF), F32)],
                  scratch_shapes=[pltpu.VMEM((2, L, LANE), F32)], ins=(h, cw, dact), sem=("parallel",), rider=rider)


POOL_ROWS = 256
POOL_TILE = 256


def _pool_bands(transpose):
    i = jnp.arange(POOL_ROWS)[:, None]
    j = jnp.arange(POOL_ROWS + 2 * POOL_HALO)[None, :] - POOL_HALO
    bands = []
    for w in POOL_WINDOWS:
        lo, hi = (-(w // 2 - 1), w // 2) if transpose else (-(w // 2), w // 2 - 1)
        bands.append(((j - i >= lo) & (j - i <= hi)).astype(_ACT))
    return jnp.stack(bands)


def _pool_window(name, u, transpose, out_dtype):
    L, D = u.shape
    ng = len(POOL_WINDOWS)
    pc = D // ng
    tn = min(POOL_TILE, pc)
    rc = _row_tile(L, POOL_ROWS)
    bands = _pool_bands(transpose)
    if rc != POOL_ROWS:
        bands = bands[:, :rc, :rc + 2 * POOL_HALO]
    halo = POOL_HALO

    def body(u_ref, band_ref, o_ref, hi_ref, lo_ref):
        g = (pl.program_id(0) * tn) // pc
        half = jnp.zeros((1, 1), jnp.int32)
        for k, w in enumerate(POOL_WINDOWS):
            half = jnp.where(g == k, w // 2, half)
        zeros = jnp.zeros((halo, tn), _ACT)
        for ref in (hi_ref, lo_ref):
            ref[0:halo, :] = zeros
            ref[halo + L:2 * halo + L, :] = zeros

        def inv_count(r0):
            t = r0 + lax.broadcasted_iota(jnp.int32, (rc, tn), 0)
            lo = jnp.clip(t - half, 0, L - 1)
            hi = jnp.clip(t + half - 1, 0, L - 1)
            return 1.0 / (hi - lo + 1).astype(F32)

        def split(ci, _):
            r0 = pl.multiple_of(ci * rc, rc)
            v = u_ref[pl.ds(r0, rc), :].astype(F32)
            if transpose:
                v = v * inv_count(r0)
            hi = v.astype(_ACT)
            dst = pl.ds(pl.multiple_of(r0 + halo, halo), rc)
            hi_ref[dst, :] = hi
            lo_ref[dst, :] = (v - hi.astype(F32)).astype(_ACT)
            return 0

        lax.fori_loop(0, L // rc, split, 0)
        band = band_ref[...]

        def chunk(ci, _):
            r0 = pl.multiple_of(ci * rc, rc)
            win = pl.ds(r0, rc + 2 * halo)
            s = (jnp.dot(band, hi_ref[win, :], preferred_element_type=F32)
                 + jnp.dot(band, lo_ref[win, :], preferred_element_type=F32))
            if not transpose:
                s = s * inv_count(r0)
            o_ref[pl.ds(r0, rc), :] = (s - u_ref[pl.ds(r0, rc), :].astype(F32)).astype(o_ref.dtype)
            return 0

        lax.fori_loop(0, L // rc, chunk, 0)

    return pl.pallas_call(body, name=name, grid=(D // tn,),
                          in_specs=[pl.BlockSpec((L, tn), lambda j: (0, j)),
                                    pl.BlockSpec((None, rc, rc + 2 * halo), lambda j: ((j * tn) // pc, 0, 0))],
                          out_specs=pl.BlockSpec((L, tn), lambda j: (0, j)),
                          out_shape=jax.ShapeDtypeStruct((L, D), out_dtype),
                          scratch_shapes=[pltpu.VMEM((L + 2 * halo, tn), _ACT), pltpu.VMEM((L + 2 * halo, tn), _ACT)],
                          compiler_params=_params(("parallel",)))(u, bands)


S5_ROWS = 512


def _slab(start):
    return pl.ds(start if isinstance(start, int) else pl.multiple_of(start, NSEG), NSEG)


def _cmul(ar, ai, br, bi):
    return ar * br - ai * bi, ar * bi + ai * br


def _cpow(ar, ai, n):
    rr, ri = None, None
    br, bi = ar, ai
    while n:
        if n & 1:
            rr, ri = (br, bi) if rr is None else _cmul(rr, ri, br, bi)
        n >>= 1
        if n:
            br, bi = _cmul(br, bi, br, bi)
    return rr, ri


def _seg_scan(sr_ref, si_ref, tmp_ref, row0, n, ar, ai, h0, rev):
    W = ar.shape[1]
    arb, aib = jnp.broadcast_to(ar, (NSEG, W)), jnp.broadcast_to(ai, (NSEG, W))

    def rows(s):
        t = (n - 1 - s) if rev else s
        return _slab(row0 + t * NSEG)

    def step(s, carry):
        hr, hi = carry
        sl = rows(s)
        nr = arb * hr - aib * hi + sr_ref[sl, :]
        ni = arb * hi + aib * hr + si_ref[sl, :]
        sr_ref[sl, :] = nr
        si_ref[sl, :] = ni
        return nr, ni

    zero = jnp.zeros((NSEG, W), F32)
    fr, fi = lax.fori_loop(0, n, step, (zero, zero), unroll=2)
    tmp_ref[0] = fr
    tmp_ref[1] = fi
    anr, ani = _cpow(ar, ai, n)
    cr, ci = h0
    for j in (range(NSEG - 1, -1, -1) if rev else range(NSEG)):
        tmp_ref[2, j:j + 1, :] = cr
        tmp_ref[3, j:j + 1, :] = ci
        pr, pi = _cmul(anr, ani, cr, ci)
        cr, ci = tmp_ref[0, j:j + 1, :] + pr, tmp_ref[1, j:j + 1, :] + pi
    cmr, cmi = tmp_ref[2], tmp_ref[3]

    def fix(s, carry):
        pr, pi = carry
        sl = rows(s)
        sr_ref[sl, :] += pr * cmr - pi * cmi
        si_ref[sl, :] += pr * cmi + pi * cmr
        return pr * arb - pi * aib, pr * aib + pi * arb

    lax.fori_loop(0, n, fix, (arb, aib), unroll=2)
    return (cr, ci), (cmr, cmi)


def _gelu_tanh(y):
    k = math.sqrt(2.0 / math.pi)
    t = jnp.tanh(k * (y + 0.044715 * y * y * y))
    return 0.5 * y * (1.0 + t), t


def _s5_chunks(L):
    rc = _row_tile(L, S5_ROWS)
    return [(r, rc) for r in range(0, L, rc)]


def _s5_project(u_ref, uc_ref, bre, bim, sr_ref, si_ref, L, LC):
    for ref, base, n in ((u_ref, 0, L), (uc_ref, L, LC)):
        for r, rc in _s5_chunks(n):
            ub = ref[r:r + rc, :].astype(_ACT)
            sr_ref[base + r:base + r + rc, :] = jnp.dot(ub, bre, preferred_element_type=F32)
            si_ref[base + r:base + r + rc, :] = jnp.dot(ub, bim, preferred_element_type=F32)


def _s5_states(sr_ref, si_ref, tmp_ref, ar, ai, L, LC, rev):
    W = ar.shape[1]
    zero = (jnp.zeros((1, W), F32), jnp.zeros((1, W), F32))
    hctx, cm_ctx = _seg_scan(sr_ref, si_ref, tmp_ref, L, LC // NSEG, ar, ai, zero, rev)
    _, cm_lat = _seg_scan(sr_ref, si_ref, tmp_ref, 0, L // NSEG, ar, ai, hctx, rev)
    return cm_lat, cm_ctx


def _s5_fwd(u, uc, bblk, cblk, apar, dsk, rider=None):
    L, D = u.shape
    LC = uc.shape[0]
    NT, W, TC = D // S5_TILE_CH, S5_TILE_W, S5_TILE_CH

    def body(u_ref, uc_ref, b_ref, c_ref, a_ref, d_ref, y_ref, z_ref, sr_ref, si_ref, tmp_ref):
        for r, rc in _s5_chunks(L):
            y_ref[r:r + rc, :] = u_ref[r:r + rc, :].astype(F32) * d_ref[0:1, :]
        for d in range(2):
            ar, ai = a_ref[2 * d:2 * d + 1, :], a_ref[2 * d + 1:2 * d + 2, :]
            _s5_project(u_ref, uc_ref, b_ref[2 * d], b_ref[2 * d + 1], sr_ref, si_ref, L, LC)
            _s5_states(sr_ref, si_ref, tmp_ref, ar, ai, L, LC, rev=(d == 1))
            cre, cim = c_ref[2 * d], c_ref[2 * d + 1]
            for r, rc in _s5_chunks(L):
                y_ref[r:r + rc, :] += (jnp.dot(sr_ref[r:r + rc, :].astype(_ACT), cre, preferred_element_type=F32)
                                       - jnp.dot(si_ref[r:r + rc, :].astype(_ACT), cim, preferred_element_type=F32))
        for r, rc in _s5_chunks(L):
            z_ref[r:r + rc, :] = _gelu_tanh(y_ref[r:r + rc, :])[0].astype(z_ref.dtype)

    col = lambda n: pl.BlockSpec((n, TC), lambda j: (0, j))
    return _pcall(
        body, name="s5_fwd", grid=(NT,), ins=(u, uc, bblk, cblk, apar, dsk), sem=("parallel",), rider=rider,
        in_specs=[col(L), col(LC),
                  pl.BlockSpec((None, 4, TC, W), lambda j: (j, 0, 0, 0)),
                  pl.BlockSpec((None, 4, W, TC), lambda j: (j, 0, 0, 0)),
                  pl.BlockSpec((None, 8, W), lambda j: (j, 0, 0)),
                  pl.BlockSpec((8, TC), lambda j: (0, j))],
        out_specs=[col(L), col(L)],
        out_shape=[jax.ShapeDtypeStruct((L, D), F32), jax.ShapeDtypeStruct((L, D), _ACT)],
        scratch_shapes=[pltpu.VMEM((L + LC, W), F32), pltpu.VMEM((L + LC, W), F32), pltpu.VMEM((4, NSEG, W), F32)])


def _s5_bwd(u, uc, dz, y, bblk, cblk, apar, dsk, rider=None):
    L, D = u.shape
    LC = uc.shape[0]
    NT, W, TC = D // S5_TILE_CH, S5_TILE_W, S5_TILE_CH
    nl, nc = L // NSEG, LC // NSEG

    def body(u_ref, uc_ref, dz_ref, y_ref, b_ref, c_ref, a_ref, d_ref,
             du_ref, duc_ref, db_ref, dc_ref, da_ref, dd_ref,
             hr_ref, hi_ref, gr_ref, gi_ref, dy_ref, tmp_ref):
        ddacc = jnp.zeros((1, TC), F32)
        for r, rc in _s5_chunks(L):
            yv = y_ref[r:r + rc, :]
            g, t = _gelu_tanh(yv)
            k = math.sqrt(2.0 / math.pi)
            dg = 0.5 * (1.0 + t) + 0.5 * yv * (1.0 - t * t) * k * (1.0 + 3 * 0.044715 * yv * yv)
            dy = dz_ref[r:r + rc, :].astype(F32) * dg
            uv = u_ref[r:r + rc, :].astype(F32)
            ddacc = ddacc + jnp.sum(dy * uv, axis=0, keepdims=True)
            du_ref[r:r + rc, :] = dy * d_ref[0:1, :]
            dy_ref[r:r + rc, :] = dy.astype(dy_ref.dtype)
        dd_ref[...] = jnp.zeros_like(dd_ref)
        dd_ref[0:1, :] = ddacc
        duc_ref[...] = jnp.zeros_like(duc_ref)
        da_ref[...] = jnp.zeros_like(da_ref)
        nt = (((1,), (1,)), ((), ()))
        tn = (((0,), (0,)), ((), ()))
        for d in range(2):
            rev = d == 1
            ar, ai = a_ref[2 * d:2 * d + 1, :], a_ref[2 * d + 1:2 * d + 2, :]
            bre, bim = b_ref[2 * d], b_ref[2 * d + 1]
            cre, cim = c_ref[2 * d], c_ref[2 * d + 1]
            _s5_project(u_ref, uc_ref, bre, bim, hr_ref, hi_ref, L, LC)
            cm_lat, cm_ctx = _s5_states(hr_ref, hi_ref, tmp_ref, ar, ai, L, LC, rev)
            cml_r, cml_i, cmc_r, cmc_i = cm_lat[0], cm_lat[1], cm_ctx[0], cm_ctx[1]
            dcr = jnp.zeros((W, TC), F32)
            dci = jnp.zeros((W, TC), F32)
            for r, rc in _s5_chunks(L):
                dyb = dy_ref[r:r + rc, :]
                gr_ref[r:r + rc, :] = lax.dot_general(dyb, cre, nt, preferred_element_type=F32)
                gi_ref[r:r + rc, :] = -lax.dot_general(dyb, cim, nt, preferred_element_type=F32)
                dcr = dcr + lax.dot_general(hr_ref[r:r + rc, :].astype(_ACT), dyb, tn, preferred_element_type=F32)
                dci = dci - lax.dot_general(hi_ref[r:r + rc, :].astype(_ACT), dyb, tn, preferred_element_type=F32)
            dc_ref[2 * d] = dcr
            dc_ref[2 * d + 1] = dci
            gr_ref[L:L + LC, :] = jnp.zeros((LC, W), F32)
            gi_ref[L:L + LC, :] = jnp.zeros((LC, W), F32)
            zero = (jnp.zeros((1, W), F32), jnp.zeros((1, W), F32))
            glat, _ = _seg_scan(gr_ref, gi_ref, tmp_ref, 0, nl, ar, -ai, zero, not rev)
            _seg_scan(gr_ref, gi_ref, tmp_ref, L, nc, ar, -ai, glat, not rev)

            def da_part(row0, n, cmr, cmi):
                def rows(s):
                    t = (n - 1 - s) if rev else s
                    return _slab(row0 + t * NSEG)

                g0r, g0i = gr_ref[rows(0), :], gi_ref[rows(0), :]
                acc0 = (cmr * g0r + cmi * g0i, cmr * g0i - cmi * g0r)

                def step(s, acc):
                    hpr, hpi = hr_ref[rows(s - 1), :], hi_ref[rows(s - 1), :]
                    gr, gi = gr_ref[rows(s), :], gi_ref[rows(s), :]
                    return acc[0] + hpr * gr + hpi * gi, acc[1] + hpr * gi - hpi * gr

                return lax.fori_loop(1, n, step, acc0, unroll=2)

            lr, li = da_part(0, nl, cml_r, cml_i)
            qr, qi = da_part(L, nc, cmc_r, cmc_i)
            da_ref[2 * d:2 * d + 1, :] = jnp.sum(lr + qr, axis=0, keepdims=True)
            da_ref[2 * d + 1:2 * d + 2, :] = jnp.sum(li + qi, axis=0, keepdims=True)
            dbr = jnp.zeros((TC, W), F32)
            dbi = jnp.zeros((TC, W), F32)
            for ref, oref, base, n in ((u_ref, du_ref, 0, L), (uc_ref, duc_ref, L, LC)):
                for r, rc in _s5_chunks(n):
                    ub = ref[r:r + rc, :].astype(_ACT)
                    gr = gr_ref[base + r:base + r + rc, :].astype(_ACT)
                    gi = gi_ref[base + r:base + r + rc, :].astype(_ACT)
                    dbr = dbr + lax.dot_general(ub, gr, tn, preferred_element_type=F32)
                    dbi = dbi + lax.dot_general(ub, gi, tn, preferred_element_type=F32)
                    oref[r:r + rc, :] += (lax.dot_general(gr, bre, nt, preferred_element_type=F32)
                                          + lax.dot_general(gi, bim, nt, preferred_element_type=F32))
            db_ref[2 * d] = dbr
            db_ref[2 * d + 1] = dbi

    col = lambda n: pl.BlockSpec((n, TC), lambda j: (0, j))
    bspec = pl.BlockSpec((None, 4, TC, W), lambda j: (j, 0, 0, 0))
    cspec = pl.BlockSpec((None, 4, W, TC), lambda j: (j, 0, 0, 0))
    aspec = pl.BlockSpec((None, 8, W), lambda j: (j, 0, 0))
    return _pcall(
        body, name="s5_bwd", grid=(NT,), ins=(u, uc, dz, y, bblk, cblk, apar, dsk), sem=("parallel",),
        vmem=VMEM_LIMIT_BIG, rider=rider,
        in_specs=[col(L), col(LC), col(L), col(L), bspec, cspec, aspec, pl.BlockSpec((8, TC), lambda j: (0, j))],
        out_specs=[col(L), col(LC), bspec, cspec, aspec, pl.BlockSpec((None, 8, TC), lambda j: (j, 0, 0))],
        out_shape=[jax.ShapeDtypeStruct((L, D), F32), jax.ShapeDtypeStruct((LC, D), F32),
                   jax.ShapeDtypeStruct((NT, 4, TC, W), F32), jax.ShapeDtypeStruct((NT, 4, W, TC), F32),
                   jax.ShapeDtypeStruct((NT, 8, W), F32), jax.ShapeDtypeStruct((NT, 8, TC), F32)],
        scratch_shapes=[pltpu.VMEM((L + LC, W), F32), pltpu.VMEM((L + LC, W), F32),
                        pltpu.VMEM((L + LC, W), F32), pltpu.VMEM((L + LC, W), F32),
                        pltpu.VMEM((L, TC), _ACT), pltpu.VMEM((4, NSEG, W), F32)])


ADA_ROWS = 16


def _silu_rows(c_ref):
    c = c_ref[...]
    return c * _sigmoid(c)


def _ada_fwd(cmat, ada_w, ada_b):
    nl, D, n = ada_w.shape
    tn = _row_tile(n, 512)

    def body(c_ref, w_ref, b_ref, o_ref):
        a = _silu_rows(c_ref).astype(_ACT)
        o_ref[...] = jnp.dot(a, w_ref[...].astype(_ACT), preferred_element_type=F32) + b_ref[...]

    return pl.pallas_call(body, name="ada_fwd", grid=(nl, n // tn),
                          in_specs=[pl.BlockSpec((ADA_ROWS, D), lambda l, j: (0, 0)),
                                    pl.BlockSpec((None, D, tn), lambda l, j: (l, 0, j)),
                                    pl.BlockSpec((None, 1, tn), lambda l, j: (l, 0, j))],
                          out_specs=pl.BlockSpec((None, ADA_ROWS, tn), lambda l, j: (l, 0, j)),
                          out_shape=jax.ShapeDtypeStruct((nl, ADA_ROWS, n), F32),
                          compiler_params=_params(("parallel", "parallel")))(cmat, ada_w, ada_b)


def _ada_bwd(cmat, ada_w, dm):
    nl, D, n = ada_w.shape
    tn = _row_tile(n, 512)
    nj = n // tn

    def body(c_ref, w_ref, dm_ref, dw_ref, dc_ref):
        c = c_ref[...]
        s = _sigmoid(c)
        a = (c * s).astype(_ACT)
        dmb = dm_ref[...].astype(_ACT)
        dw_ref[...] = lax.dot_general(a, dmb, (((0,), (0,)), ((), ())), preferred_element_type=F32)
        part = lax.dot_general(dmb, w_ref[...].astype(_ACT), (((1,), (1,)), ((), ())), preferred_element_type=F32)
        part = part * (s * (1.0 + c * (1.0 - s)))

        @pl.when(pl.program_id(1) == 0)
        def _():
            dc_ref[...] = part

        @pl.when(pl.program_id(1) > 0)
        def _():
            dc_ref[...] += part

    return pl.pallas_call(body, name="ada_bwd", grid=(nl, nj),
                          in_specs=[pl.BlockSpec((ADA_ROWS, D), lambda l, j: (0, 0)),
                                    pl.BlockSpec((None, D, tn), lambda l, j: (l, 0, j)),
                                    pl.BlockSpec((None, ADA_ROWS, tn), lambda l, j: (l, 0, j))],
                          out_specs=[pl.BlockSpec((None, D, tn), lambda l, j: (l, 0, j)),
                                     pl.BlockSpec((None, ADA_ROWS, D), lambda l, j: (l, 0, 0))],
                          out_shape=[jax.ShapeDtypeStruct((nl, D, n), F32), jax.ShapeDtypeStruct((nl, ADA_ROWS, D), F32)],
                          compiler_params=_params(("parallel", "arbitrary")))(cmat, ada_w, dm)


def _adamw(name, gparts, w, m, v):
    nl, R, C = w.shape
    n = gparts[0].shape[0]
    tr = R
    for cand in (512, 256, 128, 64, 32, 16, 8):
        if R % cand == 0 and cand * C * 4 <= 2 * 1024 * 1024:
            tr = cand
            break
    nt = R // tr
    bc1 = 1.0 - ADAM_B1 ** ADAM_STEP
    bc2 = 1.0 - ADAM_B2 ** ADAM_STEP

    def body(*refs):
        g_refs = refs[:nl]
        w_ref, m_ref, v_ref, go_ref, d_ref, mo_ref, vo_ref = refs[nl:]
        for l in range(nl):
            @pl.when(pl.program_id(0) == l)
            def _(g_ref=g_refs[l]):
                g = g_ref[0].astype(F32)
                for j in range(1, n):
                    g = g + g_ref[j].astype(F32)
                m2 = ADAM_B1 * m_ref[...] + (1.0 - ADAM_B1) * g
                v2 = ADAM_B2 * v_ref[...] + (1.0 - ADAM_B2) * (g * g)
                go_ref[...] = g
                mo_ref[...] = m2
                vo_ref[...] = v2
                d_ref[...] = -ADAM_LR * ((m2 / bc1) / (jnp.sqrt(v2 / bc2) + ADAM_EPS) + ADAM_WD * w_ref[...])

    def gspec(l):
        return pl.BlockSpec((n, tr, C), lambda lyr, i: (0, jnp.where(lyr < l, 0, jnp.where(lyr > l, nt - 1, i)), 0))

    row = pl.BlockSpec((None, tr, C), lambda lyr, i: (lyr, i, 0))
    out = jax.ShapeDtypeStruct((nl, R, C), F32)
    return _pcall(body, name=name, grid=(nl, nt), in_specs=[gspec(l) for l in range(nl)] + [row, row, row],
                  out_specs=[row, row, row, row], out_shape=[out, out, out, out], ins=(*gparts, w, m, v),
                  sem=("arbitrary", "arbitrary"))


def _sum_parts(name, parts):
    n, R, C = parts.shape

    def body(p_ref, o_ref):
        s = p_ref[0]
        for j in range(1, n):
            s = s + p_ref[j]
        o_ref[...] = s

    return pl.pallas_call(body, name=name, out_shape=jax.ShapeDtypeStruct((R, C), F32),
                          compiler_params=_params(None))(parts)


def _discretize(lam_re, lam_im, log_step, b_re, b_im):
    dt = jnp.exp(log_step)[:, None]
    mag = jnp.exp(lam_re * dt)
    abar_re = mag * jnp.cos(lam_im * dt)
    abar_im = mag * jnp.sin(lam_im * dt)
    nr, ni = abar_re - 1.0, abar_im
    den = lam_re * lam_re + lam_im * lam_im
    fr = (nr * lam_re + ni * lam_im) / den
    fi = (ni * lam_re - nr * lam_im) / den
    bbar_re = fr[..., None] * b_re - fi[..., None] * b_im
    bbar_im = fr[..., None] * b_im + fi[..., None] * b_re
    return abar_re, abar_im, bbar_re, bbar_im


def _s5_pack(abar, bbar, cmat):
    G = abar[0][0].shape[0]
    NT = G // S5_TILE_G
    eye = jnp.eye(S5_TILE_G, dtype=F32)
    rows, bs, cs = [], [], []
    for d in range(2):
        for r in range(2):
            rows.append(abar[d][r].reshape(NT, 1, S5_TILE_W))
            bb = bbar[d][r].reshape(NT, S5_TILE_G, S5_P, S5_CH)
            bs.append(jnp.einsum("jgpc,gh->jgchp", bb, eye).reshape(NT, S5_TILE_CH, S5_TILE_W))
            cc = cmat[d][r].reshape(NT, S5_TILE_G, S5_CH, S5_P)
            cs.append(jnp.einsum("jgcp,gh->jgphc", cc, eye).reshape(NT, S5_TILE_W, S5_TILE_CH))
    apar = jnp.concatenate(rows + [jnp.zeros((NT, 4, S5_TILE_W), F32)], axis=1)
    return apar, jnp.stack(bs, axis=1).astype(_ACT), jnp.stack(cs, axis=1).astype(_ACT)


def _s5_unpack(dapar, dbblk, dcblk, G):
    NT = G // S5_TILE_G
    da = dapar[:, :4, :].reshape(NT, 2, 2, S5_TILE_G, S5_P).transpose(1, 2, 0, 3, 4).reshape(2, 2, G, S5_P)
    idx = jnp.arange(S5_TILE_G)
    db = dbblk.reshape(NT, 2, 2, S5_TILE_G, S5_CH, S5_TILE_G, S5_P)[:, :, :, idx, :, idx, :]
    db = db.transpose(2, 3, 1, 0, 5, 4).reshape(2, 2, G, S5_P, S5_CH)
    dc = dcblk.reshape(NT, 2, 2, S5_TILE_G, S5_P, S5_TILE_G, S5_CH)[:, :, :, idx, :, idx, :]
    dc = dc.transpose(2, 3, 1, 0, 5, 4).reshape(2, 2, G, S5_CH, S5_P)
    return da, db, dc


def _to_segments(a):
    L, D = a.shape
    return a.reshape(NSEG, L // NSEG, D).transpose(1, 0, 2).reshape(L, D)


def _from_segments(a):
    L, D = a.shape
    return a.reshape(L // NSEG, NSEG, D).transpose(1, 0, 2).reshape(L, D)


def _pos_emb(n_tokens, dim):
    rows = n_tokens // GRID_W
    quarter = dim // 4
    omega = 1.0 / (POS_BASE ** (jnp.arange(quarter, dtype=F32) / quarter))

    def enc(p):
        ang = p[:, None] * omega[None, :]
        return jnp.concatenate([jnp.sin(ang), jnp.cos(ang)], axis=-1)

    rtab = enc(jnp.arange(rows, dtype=F32))
    ctab = enc(jnp.arange(GRID_W, dtype=F32))
    return jnp.concatenate([jnp.repeat(rtab, GRID_W, axis=0), jnp.tile(ctab, (rows, 1))], axis=-1)


def _vec(D, **rows):
    names = {"gpost": V_GPOST, "gate": V_GATE, "yscale": V_YSCALE, "gpre": V_GPRE, "shift": V_SHIFT, "scale": V_SCALE}
    out = [jnp.zeros((D,), F32)] * 8
    out[V_YSCALE] = jnp.ones((D,), F32)
    for k, v in rows.items():
        out[names[k]] = v.reshape(D).astype(F32)
    return jnp.stack(out)


def _row0(v, D):
    return jnp.concatenate([v.reshape(1, D).astype(F32), jnp.zeros((7, D), F32)], axis=0)


def _my_block(full, axis, n_local):
    return lax.dynamic_slice_in_dim(full, _my_index() * n_local, n_local, axis)


def kernel(x, c, ctx, c_ctx, ada_w, ada_b, norm_g, s5_lam_re, s5_lam_im, s5_log_step, s5_b_re, s5_b_im, s5_c_re, s5_c_im, s5_d, s5_glu_w, pool_w, pool_scale, ffn_up, ffn_conv, ffn_conv_b, ffn_down, loss_target, m_c_ctx, m_ada_w, m_ada_b, m_norm_g, m_s5_lam_re, m_s5_lam_im, m_s5_log_step, m_s5_b_re, m_s5_b_im, m_s5_c_re, m_s5_c_im, m_s5_d, m_s5_glu_w, m_pool_w, m_pool_scale, m_ffn_up, m_ffn_conv, m_ffn_conv_b, m_ffn_down, v_c_ctx, v_ada_w, v_ada_b, v_norm_g, v_s5_lam_re, v_s5_lam_im, v_s5_log_step, v_s5_b_re, v_s5_b_im, v_s5_c_re, v_s5_c_im, v_s5_d, v_s5_glu_w, v_pool_w, v_pool_scale, v_ffn_up, v_ffn_conv, v_ffn_conv_b, v_ffn_down):
    L, D = x.shape[1], x.shape[2]
    LC = ctx.shape[1]
    G = s5_lam_re.shape[2]
    n_ada = ada_w.shape[2]
    nb_up = ffn_up.shape[2]
    r_down = ffn_down.shape[1]
    FF = N_DEV * r_down
    n_pool = len(POOL_WINDOWS)
    pc = D // n_pool
    pr = pool_w.shape[2]
    ng_loc = norm_g.shape[2]
    me = _my_index()
    axes = ("x", "y", "c")

    up_b = [ffn_up[i].astype(_ACT) for i in range(2)]
    down_b = [ffn_down[i].astype(_ACT) for i in range(2)]
    glu_b = s5_glu_w[0].astype(_ACT)
    pool_b = pool_w[0].reshape(n_pool * pr, pc).astype(_ACT)

    small_loc = jnp.concatenate([c.reshape(-1), norm_g.reshape(-1), pool_scale.reshape(-1), ffn_conv.reshape(-1)])
    n_small = small_loc.shape[0]
    small_g, = _exchange([[jnp.pad(small_loc, (0, (-n_small) % LANE)).reshape(1, -1)]], mode="gather", name="gather_small")
    small_g = small_g.reshape(N_DEV, -1)
    o = 0
    c_all = small_g[:, o:o + D]
    o += D
    ng_all = small_g[:, o:o + 8 * ng_loc].reshape(N_DEV, 2, 4, ng_loc).transpose(1, 2, 0, 3).reshape(2, 4, D)
    o += 8 * ng_loc
    pscale_all = small_g[:, o:o + ng_loc].reshape(D)
    o += ng_loc
    conv_all = small_g[:, o:o + 6 * nb_up].reshape(N_DEV, 2, 3, nb_up).transpose(1, 2, 0, 3).reshape(2, 3, 2 * FF)

    cmat = jnp.concatenate([c_all, c_ctx.reshape(1, D), jnp.zeros((ADA_ROWS - N_DEV - 1, D), F32)], axis=0)
    ada_b_loc = _my_block(ada_b, 1, n_ada).reshape(2, 1, n_ada)
    mods_loc = _ada_fwd(cmat, ada_w, ada_b_loc)
    mods_g, = _exchange([[mods_loc]], mode="gather", name="gather_mods")
    mods_rows = mods_g.reshape(N_DEV, 2, ADA_ROWS, n_ada).transpose(1, 2, 0, 3).reshape(2, ADA_ROWS, 6, D)
    mod = lax.dynamic_index_in_dim(mods_rows, me, axis=1, keepdims=False)
    mod_c = mods_rows[0, N_DEV]

    def disc_all(lr, li, ls, br, bi):
        return [_discretize(lr[d], li[d], ls[d], br[d], bi[d]) for d in range(2)]

    disc, disc_vjp = jax.vjp(disc_all, s5_lam_re[0], s5_lam_im[0], s5_log_step[0], s5_b_re[0], s5_b_im[0])
    apar, bblk, cblk = _s5_pack([(disc[d][0], disc[d][1]) for d in range(2)],
                                [(disc[d][2], disc[d][3]) for d in range(2)],
                                [(s5_c_re[0, d], s5_c_im[0, d]) for d in range(2)])
    dsk = _row0(s5_d[0], D)
    cw = []
    for i in range(2):
        taps = conv_all[i].reshape(3, 2, FF).transpose(1, 0, 2)
        cw.append(jnp.concatenate([taps, ffn_conv_b[i].reshape(2, 1, FF), jnp.zeros((2, 4, FF), F32)], axis=1))

    vecs = {
        "b0": _vec(D, gpre=ng_all[0, 0], shift=mod[0, 0], scale=mod[0, 1]),
        "c0": _vec(D, gpre=ng_all[0, 0], shift=mod_c[0], scale=mod_c[1]),
        "b1": _vec(D, gpost=ng_all[0, 1], gate=mod[0, 2], gpre=ng_all[0, 2], shift=mod[0, 3], scale=mod[0, 4]),
        "b2": _vec(D, gpost=ng_all[0, 3], gate=mod[0, 5], gpre=ng_all[1, 0], shift=mod[1, 0], scale=mod[1, 1]),
        "b3": _vec(D, gpost=ng_all[1, 1], gate=mod[1, 2], yscale=pscale_all, gpre=ng_all[1, 2], shift=mod[1, 3],
                   scale=mod[1, 4]),
        "b4": _vec(D, gpost=ng_all[1, 3], gate=mod[1, 5]),
    }

    x0, u0 = _rows_fwd("rows_fwd_b0", x[0], _pos_emb(L, D), vecs["b0"], add=True, u_dtype=_ACT)
    uc, = _rows_fwd("rows_fwd_ctx", ctx[0], None, vecs["c0"], want_x=False, u_dtype=_ACT)
    u0s, ucs = _to_segments(u0), _to_segments(uc)
    (y_s5, z_s5), (glu_g, up_g0, down_g0) = _s5_fwd(u0s, ucs, bblk, cblk, apar, dsk,
                                                    rider=([[glu_b], [up_b[0]], [down_b[0]]], "gather2"))
    vg = _colblock_fwd("glu_fwd_mm", z_s5, glu_g, 0, _ACT)
    mix0 = _from_segments(_glu_fwd("glu_fwd", vg))
    x1, un0 = _rows_fwd("rows_fwd_b1", x0, mix0, vecs["b1"], u_dtype=_ACT)
    h0, (up_g1,) = _colblock_fwd("ffn0_up", un0, up_g0, 0, _ACT, rider=([[up_b[1]]], "gather2"))
    act0 = _conv_swiglu_fwd("ffn0_conv", h0, cw[0])
    f0, (down_g1, pool_g) = _rowblock_fwd("ffn0_down", act0, down_g0, 0, rider=([[down_b[1]], [pool_b]], "gather2"))
    pool_full = pool_g.reshape(N_DEV, n_pool, pr, pc).transpose(1, 0, 2, 3).reshape(n_pool, pc, pc)
    x2, u1 = _rows_fwd("rows_fwd_b2", x1, f0, vecs["b2"], u_dtype=F32)
    p1 = _pool_window("pool_fwd", u1, False, _ACT)
    ypre1 = _group_mm("pool_fwd_mm", p1, pool_full, "nn", F32)
    x3, un1 = _rows_fwd("rows_fwd_b3", x2, ypre1, vecs["b3"], u_dtype=_ACT)
    h1 = _colblock_fwd("ffn1_up", un1, up_g1, 0, _ACT)
    act1 = _conv_swiglu_fwd("ffn1_conv", h1, cw[1])
    f1 = _rowblock_fwd("ffn1_down", act1, down_g1, 0)
    dx4, loss_blk = _rows_fwd("rows_fwd_b4", x3, f1, vecs["b4"], target=loss_target[0])
    loss = lax.psum(loss_blk[0, 0], axes)

    df1, red4 = _rows_bwd("rows_bwd_b4", dx4, None, None, f1, vecs["b4"], dy_dtype=_ACT, want_dx=False)
    dact1 = _rowblock_dgrad("ffn1_down_dgrad", df1, down_g1, 0)
    ddown1 = _rowblock_wgrad("ffn1_down_wgrad", act1, df1)
    (dh1, dcw1), (gp_down1,) = _conv_swiglu_bwd("ffn1_conv_bwd", h1, cw[1], dact1,
                                                rider=([[ddown1.reshape(N_DEV, r_down, D)]], "scatter"))
    dun1 = _colblock_dgrad("ffn1_up_dgrad", dh1, up_g1, 0, F32)
    dup1 = _colblock_wgrad("ffn1_up_wgrad", un1, dh1)
    dx3, dy3, red3 = _rows_bwd("rows_bwd_b3", dx4, dun1, x3, ypre1, vecs["b3"])
    dypre1, red_ps = _colscale_bwd("pool_scale_bwd", dy3, ypre1, _row0(pscale_all, D))
    dp1 = _group_mm("pool_dgrad", dypre1, pool_full, "nt", F32)
    dpool = _group_wgrad("pool_wgrad", p1, dypre1, n_pool)
    du1 = _pool_window("pool_bwd", dp1, True, F32)
    dx2, df0, red2 = _rows_bwd("rows_bwd_b2", dx3, du1, x2, f0, vecs["b2"], dy_dtype=_ACT)
    dact0 = _rowblock_dgrad("ffn0_down_dgrad", df0, down_g0, 0)
    ddown0 = _rowblock_wgrad("ffn0_down_wgrad", act0, df0)
    dh0, dcw0 = _conv_swiglu_bwd("ffn0_conv_bwd", h0, cw[0], dact0)
    dun0, (gp_down0,) = _colblock_dgrad("ffn0_up_dgrad", dh0, up_g0, 0, F32,
                                        rider=([[ddown0.reshape(N_DEV, r_down, D)]], "scatter"))
    dup0 = _colblock_wgrad("ffn0_up_wgrad", un0, dh0)
    dx1, dmix0, red1 = _rows_bwd("rows_bwd_b1", dx2, dun0, x1, mix0, vecs["b1"])
    dvg = _glu_bwd("glu_bwd", vg, _to_segments(dmix0))
    dz = _colblock_dgrad("glu_dgrad", dvg, glu_g, 0, _ACT)
    dglu = _colblock_wgrad("glu_wgrad", z_s5, dvg)
    dpool_blocks = dpool.reshape(n_pool, N_DEV, pr, pc).transpose(1, 0, 2, 3).reshape(N_DEV, n_pool * pr, pc)
    (du0s, ducs, dbblk, dcblk, dapar, ddsk), (gp_up1, gp_up0, gp_glu, gp_pool) = _s5_bwd(
        u0s, ucs, dz, y_s5, bblk, cblk, apar, dsk, rider=([[dup1], [dup0], [dglu], [dpool_blocks]], "scatter"))
    grad_x, red0 = _rows_bwd("rows_bwd_b0", dx1, _from_segments(du0s), x0, None, vecs["b0"])
    redc, = _rows_bwd("rows_bwd_ctx", None, _from_segments(ducs), ctx[0], None, vecs["c0"], want_dx=False)

    zero_d = jnp.zeros((D,), F32)
    dmod = jnp.stack([
        jnp.stack([red0[R_SHIFT], red0[R_SCALE], red1[R_GATE], red1[R_SHIFT], red1[R_SCALE], red2[R_GATE]]),
        jnp.stack([red2[R_SHIFT], red2[R_SCALE], red3[R_GATE], red3[R_SHIFT], red3[R_SCALE], red4[R_GATE]])])
    dmod_c = jnp.stack([jnp.stack([redc[R_SHIFT], redc[R_SCALE]] + [zero_d] * 4), jnp.zeros((6, D), F32)])
    dm_g, = _exchange([[jnp.stack([dmod, dmod_c], axis=1).reshape(2, 2, 6 * D)]], mode="gather", name="gather_dmods")
    dm_g = dm_g.reshape(N_DEV, 2, 2, 6 * D)
    dm_ctx = _sum_parts("sum_dmod_ctx", dm_g[:, :, 1, :])
    dm_rows = jnp.concatenate([dm_g[:, :, 0, :].transpose(1, 0, 2), dm_ctx[:, None, :]], axis=1)
    grad_ada_b = _sum_parts("sum_ada_b", dm_rows.transpose(1, 0, 2))
    dm_cols = dm_rows.reshape(2, N_DEV + 1, N_DEV, n_ada)
    dm_mine = lax.dynamic_index_in_dim(dm_cols, me, axis=2, keepdims=False)
    dm_mine = jnp.concatenate([dm_mine, jnp.zeros((2, ADA_ROWS - N_DEV - 1, n_ada), F32)], axis=1)
    grad_ada_w, dcond = _ada_bwd(cmat, ada_w, dm_mine)
    dcctx_part = dcond[0, N_DEV] + dcond[1, N_DEV]

    da, db, dc = _s5_unpack(dapar, dbblk, dcblk, G)
    dnorm = jnp.stack([
        jnp.stack([red0[R_GPRE] + redc[R_GPRE], red1[R_GPOST], red1[R_GPRE], red2[R_GPOST]]),
        jnp.stack([red2[R_GPRE], red3[R_GPOST], red3[R_GPRE], red4[R_GPOST]])])
    dconv = jnp.stack([d[:, :3, :].transpose(1, 0, 2).reshape(3, 2 * FF) for d in (dcw0, dcw1)])
    dconv_b = jnp.stack([d[:, 3, :].reshape(2 * FF) for d in (dcw0, dcw1)])
    pieces = [dcctx_part, dnorm, da, db, dc, ddsk[:, 0, :], red_ps[0], dconv, dconv_b]
    flat = jnp.concatenate([p.reshape(-1) for p in pieces])
    n_flat = flat.shape[0]
    per_dev = -(-n_flat // (N_DEV * 8 * LANE)) * 8 * LANE
    flat = jnp.pad(flat, (0, N_DEV * per_dev - n_flat)).reshape(N_DEV, per_dev // LANE, LANE)
    parts, = _exchange([[flat]], mode="scatter", name="scatter_small_grads")
    mine = _sum_parts("sum_small_grads", parts.reshape(N_DEV, per_dev // LANE, LANE))
    summed, = _exchange([[mine]], mode="gather", name="gather_small_grads")
    summed = summed.reshape(-1)
    red_pieces, o = [], 0
    for p in pieces:
        red_pieces.append(summed[o:o + p.size].reshape(p.shape))
        o += p.size
    g_cctx, g_norm, g_a, g_b, g_c, g_d, g_pscale, g_conv, g_conv_b = red_pieces
    cot = [(g_a[d, 0], g_a[d, 1], g_b[d, 0], g_b[d, 1]) for d in range(2)]
    g_lam_re, g_lam_im, g_log_step, g_b_re, g_b_im = disc_vjp(cot)

    out = {}

    def put(name, res, shape):
        out[name] = tuple(r.reshape(shape) for r in res)

    put("ffn_up", _adamw("adamw_ffn_up", [g.reshape(N_DEV, D, nb_up) for g in (gp_up0, gp_up1)],
                         ffn_up, m_ffn_up, v_ffn_up), ffn_up.shape)
    put("ffn_down", _adamw("adamw_ffn_down", [g.reshape(N_DEV, r_down, D) for g in (gp_down0, gp_down1)],
                           ffn_down, m_ffn_down, v_ffn_down), ffn_down.shape)
    put("s5_glu_w", _adamw("adamw_glu", [gp_glu.reshape(N_DEV, D, -1)], s5_glu_w, m_s5_glu_w, v_s5_glu_w),
        s5_glu_w.shape)
    pool_rows = (1, n_pool * pr, pc)
    put("pool_w", _adamw("adamw_pool", [gp_pool.reshape(N_DEV, n_pool * pr, pc)], pool_w.reshape(pool_rows),
                         m_pool_w.reshape(pool_rows), v_pool_w.reshape(pool_rows)), pool_w.shape)
    put("ada_w", _adamw("adamw_ada_w", [grad_ada_w[i][None] for i in range(2)], ada_w, m_ada_w, v_ada_w), ada_w.shape)

    for nm, w, m, v, g in (("s5_b_re", s5_b_re, m_s5_b_re, v_s5_b_re, g_b_re),
                           ("s5_b_im", s5_b_im, m_s5_b_im, v_s5_b_im, g_b_im),
                           ("s5_c_re", s5_c_re, m_s5_c_re, v_s5_c_re, g_c[:, 0]),
                           ("s5_c_im", s5_c_im, m_s5_c_im, v_s5_c_im, g_c[:, 1])):
        rows = (1, w.size // w.shape[-1], w.shape[-1])
        put(nm, _adamw("adamw_" + nm, [g.reshape(rows)], w.reshape(rows), m.reshape(rows), v.reshape(rows)), w.shape)

    small = [
        ("c_ctx", c_ctx, m_c_ctx, v_c_ctx, g_cctx),
        ("ada_b", ada_b, m_ada_b, v_ada_b, grad_ada_b),
        ("norm_g", norm_g, m_norm_g, v_norm_g, _my_block(g_norm, 2, ng_loc)),
        ("s5_lam_re", s5_lam_re, m_s5_lam_re, v_s5_lam_re, g_lam_re),
        ("s5_lam_im", s5_lam_im, m_s5_lam_im, v_s5_lam_im, g_lam_im),
        ("s5_log_step", s5_log_step, m_s5_log_step, v_s5_log_step, g_log_step),
        ("s5_d", s5_d, m_s5_d, v_s5_d, g_d),
        ("pool_scale", pool_scale, m_pool_scale, v_pool_scale, _my_block(g_pscale, 0, ng_loc)),
        ("ffn_conv", ffn_conv, m_ffn_conv, v_ffn_conv, _my_block(g_conv, 2, nb_up)),
        ("ffn_conv_b", ffn_conv_b, m_ffn_conv_b, v_ffn_conv_b, g_conv_b),
    ]
    n_sm = sum(w.size for _, w, _, _, _ in small)
    rows_sm = -(-n_sm // (512 * LANE)) * 512

    def flat_of(k):
        f = jnp.concatenate([t[k].reshape(-1) for t in small])
        return jnp.pad(f, (0, rows_sm * LANE - n_sm)).reshape(rows_sm, LANE)

    res_sm = _adamw("adamw_small", [flat_of(4)[None]], flat_of(1)[None], flat_of(2)[None], flat_of(3)[None])
    o = 0
    for name, w, _, _, _ in small:
        out[name] = tuple(r.reshape(-1)[o:o + w.size].reshape(w.shape) for r in res_sm)
        o += w.size

    order = ["c_ctx", "ada_w", "ada_b", "norm_g", "s5_lam_re", "s5_lam_im", "s5_log_step", "s5_b_re", "s5_b_im",
             "s5_c_re", "s5_c_im", "s5_d", "s5_glu_w", "pool_w", "pool_scale", "ffn_up", "ffn_conv", "ffn_conv_b",
             "ffn_down"]
    return (loss, grad_x.reshape(x.shape), *[out[n][0] for n in order], *[out[n][1] for n in order],
            *[out[n][2] for n in order], *[out[n][3] for n in order])
```

```python
import functools
import math

import jax
import jax.numpy as jnp
from jax import lax
from jax.experimental import pallas as pl
from jax.experimental.pallas import tpu as pltpu

F32 = jnp.float32
_ACT = jnp.bfloat16
N_DEV = 8
NSEG = 8
S5_CH = 16
S5_P = 64
LANE = 128
S5_TILE_CH = LANE
S5_TILE_G = S5_TILE_CH // S5_CH
S5_TILE_W = S5_TILE_G * S5_P
GRID_W = 64
POOL_WINDOWS = (2, 4, 8, 16)
POOL_HALO = 64
RMS_EPS = 1e-6
POS_BASE = 10000.0
ADAM_LR, ADAM_B1, ADAM_B2, ADAM_EPS, ADAM_WD, ADAM_STEP = 0.001, 0.9, 0.999, 1e-08, 0.01, 10
VMEM_LIMIT = 48 * 1024 * 1024
VMEM_LIMIT_BIG = 58 * 1024 * 1024
MESH = pl.DeviceIdType.MESH
ANY = pl.BlockSpec(memory_space=pl.ANY)


def _params(sem, vmem=VMEM_LIMIT):
    return pltpu.CompilerParams(dimension_semantics=sem, vmem_limit_bytes=vmem)


def _my_index():
    return 4 * lax.axis_index("x") + 2 * lax.axis_index("y") + lax.axis_index("c")


def _xchg_plan(groups, mode):
    flat = [(g, l, a) for g, grp in enumerate(groups) for l, a in enumerate(grp)]
    outs = []
    for grp in groups:
        piece = grp[0].shape[1:] if mode == "scatter" else grp[0].shape
        outs.append(jax.ShapeDtypeStruct((N_DEV, len(grp)) + tuple(piece), grp[0].dtype))
    return flat, outs


def _xchg_sems(n):
    return [pltpu.SemaphoreType.DMA((n, N_DEV - 1)), pltpu.SemaphoreType.DMA((n, N_DEV - 1)),
            pltpu.SemaphoreType.DMA((n,))]


def _xchg_copies(flat, mode, ins, out_refs, sems, waiting=True):
    send_sems, recv_sems, local_sems = sems
    x, y, c = lax.axis_index("x"), lax.axis_index("y"), lax.axis_index("c")
    me = 4 * x + 2 * y + c
    local, first, forwards = [], [], []

    def pair(s, j, dev):
        return dict(send_sem=send_sems.at[s, j], recv_sem=recv_sems.at[s, j], device_id=dev, device_id_type=MESH)

    for s, (g, l, _) in enumerate(flat):
        src = ins[s].at[me] if mode == "scatter" else ins[s]
        local.append(pltpu.make_async_copy(src, out_refs[g].at[me, l], local_sems.at[s]))
    if mode == "gather2":
        sib, sib_idx = (x, y, 1 - c), 4 * x + 2 * y + (1 - c)
        for s, (g, l, _) in enumerate(flat):
            slot = lambda dev, g=g, l=l: out_refs[g].at[dev, l]
            targets = [(sib, sib_idx)] + [((qx, qy, c), 4 * qx + 2 * qy + c)
                                          for qx, qy in ((1 - x, y), (x, 1 - y), (1 - x, 1 - y))]
            for j, (dev, idx) in enumerate(targets):
                send = pltpu.make_async_remote_copy(src_ref=ins[s], dst_ref=slot(me), **pair(s, j, dev))
                arrive = pltpu.make_async_remote_copy(src_ref=ins[s], dst_ref=slot(idx), **pair(s, j, dev)) if waiting else None
                first.append((send, arrive))
            if waiting:
                for j, (dev, idx) in enumerate(targets[1:]):
                    other = 4 * dev[0] + 2 * dev[1] + (1 - c)
                    send = pltpu.make_async_remote_copy(src_ref=slot(idx), dst_ref=slot(idx), **pair(s, 4 + j, sib))
                    arrive = pltpu.make_async_remote_copy(src_ref=slot(idx), dst_ref=slot(other), **pair(s, 4 + j, sib))
                    forwards.append((first[len(first) - 3 + j][1], send, arrive))
        return local, first, forwards
    for k in range(1, N_DEV):
        px = 1 - x if k & 4 else x
        py = 1 - y if k & 2 else y
        pc = 1 - c if k & 1 else c
        peer = 4 * px + 2 * py + pc
        for s, (g, l, _) in enumerate(flat):
            src = ins[s].at[peer] if mode == "scatter" else ins[s]
            send = pltpu.make_async_remote_copy(src_ref=src, dst_ref=out_refs[g].at[me, l], **pair(s, k - 1, (px, py, pc)))
            arrive = (pltpu.make_async_remote_copy(src_ref=src, dst_ref=out_refs[g].at[peer, l],
                                                   **pair(s, k - 1, (px, py, pc))) if waiting else None)
            first.append((send, arrive))
    return local, first, forwards


def _xchg_start(local, first, forwards):
    for cp in local:
        cp.start()
    for send, _ in first:
        send.start()


def _xchg_wait(local, first, forwards):
    gates = [gate for gate, _, _ in forwards]
    for gate, send, _ in forwards:
        gate.wait_recv()
        send.start()
    for _, arrive in first:
        if not any(arrive is gate for gate in gates):
            arrive.wait_recv()
    for _, _, arrive in forwards:
        arrive.wait_recv()
    for send, _ in first:
        send.wait_send()
    for _, send, _ in forwards:
        send.wait_send()
    for cp in local:
        cp.wait()


def _exchange(groups, mode, name):
    flat, outs = _xchg_plan(groups, mode)
    n = len(flat)

    def body(*refs):
        copies = _xchg_copies(flat, mode, refs[:n], refs[n:n + len(groups)], refs[n + len(groups):])
        _xchg_start(*copies)
        _xchg_wait(*copies)

    res = pl.pallas_call(body, name=name, out_shape=outs, in_specs=[ANY] * n, out_specs=[ANY] * len(groups),
                         scratch_shapes=_xchg_sems(n))(*[a for _, _, a in flat])
    return list(res)


def _pcall(body, *, name, grid, in_specs, out_specs, out_shape, ins, scratch_shapes=(), sem=None, vmem=VMEM_LIMIT,
           rider=None):
    single = not isinstance(out_shape, (list, tuple))
    if rider is None:
        return pl.pallas_call(body, name=name, grid=grid, in_specs=list(in_specs), out_specs=out_specs,
                              out_shape=out_shape, scratch_shapes=list(scratch_shapes),
                              compiler_params=_params(sem, vmem))(*ins)
    groups, mode = rider
    flat, r_outs = _xchg_plan(groups, mode)
    n_in, n_out = len(ins), 1 if single else len(out_shape)
    nr, ng, ns = len(flat), len(groups), len(scratch_shapes)

    def wrapped(*refs):
        o1 = n_in + nr
        o2 = o1 + n_out
        o3 = o2 + ng
        r_in, r_out, sems = refs[n_in:o1], refs[o2:o3], refs[o3 + ns:]
        first = functools.reduce(jnp.logical_and, [pl.program_id(d) == 0 for d in range(len(grid))])
        last = functools.reduce(jnp.logical_and, [pl.program_id(d) == grid[d] - 1 for d in range(len(grid))])

        @pl.when(first)
        def _():
            _xchg_start(*_xchg_copies(flat, mode, r_in, r_out, sems, waiting=False))

        body(*refs[:n_in], *refs[o1:o2], *refs[o3:o3 + ns])

        @pl.when(last)
        def _():
            _xchg_wait(*_xchg_copies(flat, mode, r_in, r_out, sems))

    outs = pl.pallas_call(
        wrapped, name=name, grid=grid, in_specs=list(in_specs) + [ANY] * nr,
        out_specs=([out_specs] if single else list(out_specs)) + [ANY] * ng,
        out_shape=([out_shape] if single else list(out_shape)) + r_outs,
        scratch_shapes=list(scratch_shapes) + _xchg_sems(nr),
        compiler_params=_params(("arbitrary",) * len(grid), vmem))(*ins, *[a for _, _, a in flat])
    base = list(outs[:n_out])
    return (base[0] if single else base), list(outs[n_out:])


_DIMS = {"nn": (((1,), (0,)), ((), ())), "nt": (((1,), (1,)), ((), ())), "tn": (((0,), (0,)), ((), ()))}


def _mm(name, a, b, a_spec, b_spec, o_spec, out_shape, grid, dims, rider=None):
    nk = grid[2]
    acc_shape = tuple(d for d in o_spec.block_shape if d is not None)
    dn = _DIMS[dims]

    def tile(ref):
        v = ref[...]
        return v.reshape((-1, v.shape[-1])).astype(_ACT)

    def body(a_ref, b_ref, o_ref, *scratch):
        def part():
            return lax.dot_general(tile(a_ref), tile(b_ref), dn, preferred_element_type=F32)

        if nk == 1:
            o_ref[...] = part().reshape(o_ref.shape).astype(o_ref.dtype)
            return
        acc_ref, = scratch
        k = pl.program_id(2)

        @pl.when(k == 0)
        def _():
            acc_ref[...] = part()

        @pl.when(k > 0)
        def _():
            acc_ref[...] += part()

        @pl.when(k == nk - 1)
        def _():
            o_ref[...] = acc_ref[...].reshape(o_ref.shape).astype(o_ref.dtype)

    acc2d = (math.prod(acc_shape[:-1]), acc_shape[-1])
    return _pcall(body, name=name, out_shape=out_shape, grid=grid, in_specs=[a_spec, b_spec], out_specs=o_spec,
                  scratch_shapes=[] if nk == 1 else [pltpu.VMEM(acc2d, F32)], ins=(a, b),
                  sem=("parallel", "parallel", "arbitrary"), rider=rider)


def _row_tile(n, want):
    t = min(n, want)
    assert n % t == 0, (n, t)
    return t


def _colblock_fwd(name, xa, wg, layer, out_dtype, rider=None):
    L, K = xa.shape
    nb = wg.shape[3]
    half = N_DEV // 2
    tm = _row_tile(L, 512)
    return _mm(name, xa, wg,
               pl.BlockSpec((tm, K), lambda j, i, k: (i, 0)),
               pl.BlockSpec((None, None, K, nb), lambda j, i, k: (j, layer, 0, 0)),
               pl.BlockSpec((None, tm, nb), lambda j, i, k: (j // half, i, j % half)),
               jax.ShapeDtypeStruct((2, L, half * nb), out_dtype), (N_DEV, L // tm, 1), "nn", rider=rider)


def _colblock_dgrad(name, dh, wg, layer, out_dtype, rider=None):
    _, L, _ = dh.shape
    K, nb = wg.shape[2], wg.shape[3]
    half = N_DEV // 2
    tm = _row_tile(L, 512)
    return _mm(name, dh, wg,
               pl.BlockSpec((None, tm, nb), lambda i, j, k: (k // half, i, k % half)),
               pl.BlockSpec((None, None, K, nb), lambda i, j, k: (k, layer, 0, 0)),
               pl.BlockSpec((tm, K), lambda i, j, k: (i, 0)),
               jax.ShapeDtypeStruct((L, K), out_dtype), (L // tm, 1, N_DEV), "nt", rider=rider)


def _colblock_wgrad(name, xa, dh, rider=None):
    L, K = xa.shape
    half = N_DEV // 2
    nb = dh.shape[2] // half
    tm = _row_tile(K, 512)
    tk = L
    return _mm(name, xa, dh,
               pl.BlockSpec((tk, tm), lambda j, i, k: (k, i)),
               pl.BlockSpec((None, tk, nb), lambda j, i, k: (j // half, k, j % half)),
               pl.BlockSpec((None, tm, nb), lambda j, i, k: (j, i, 0)),
               jax.ShapeDtypeStruct((N_DEV, K, nb), _ACT), (N_DEV, K // tm, L // tk), "tn", rider=rider)


def _rowblock_fwd(name, xa, wg, layer, rider=None):
    L, FF = xa.shape
    r, D = wg.shape[2], wg.shape[3]
    tm = _row_tile(L, 512)
    return _mm(name, xa, wg,
               pl.BlockSpec((tm, 2 * r), lambda i, j, k: (i, k)),
               pl.BlockSpec((2, None, r, D), lambda i, j, k: (k, layer, 0, 0)),
               pl.BlockSpec((tm, D), lambda i, j, k: (i, 0)),
               jax.ShapeDtypeStruct((L, D), F32), (L // tm, 1, N_DEV // 2), "nn", rider=rider)


def _rowblock_dgrad(name, dy, wg, layer, rider=None):
    L, D = dy.shape
    r = wg.shape[2]
    tm = _row_tile(L, 512)
    return _mm(name, dy, wg,
               pl.BlockSpec((tm, D), lambda i, j, k: (i, 0)),
               pl.BlockSpec((2, None, r, D), lambda i, j, k: (j, layer, 0, 0)),
               pl.BlockSpec((tm, 2 * r), lambda i, j, k: (i, j)),
               jax.ShapeDtypeStruct((L, N_DEV * r), _ACT), (L // tm, N_DEV // 2, 1), "nt", rider=rider)


def _rowblock_wgrad(name, xa, dy, rider=None):
    L, FF = xa.shape
    D = dy.shape[1]
    tm = FF // (N_DEV // 2)
    tn = _row_tile(D, 1024)
    tk = _row_tile(L, 2048)
    return _mm(name, xa, dy,
               pl.BlockSpec((tk, tm), lambda i, j, k: (k, i)),
               pl.BlockSpec((tk, tn), lambda i, j, k: (k, j)),
               pl.BlockSpec((tm, tn), lambda i, j, k: (i, j)),
               jax.ShapeDtypeStruct((FF, D), _ACT), (FF // tm, D // tn, L // tk), "tn", rider=rider)


def _group_mm(name, xa, w, dims, out_dtype):
    L, D = xa.shape
    ng, pc, _ = w.shape
    tm = _row_tile(L, 512)
    return _mm(name, xa, w,
               pl.BlockSpec((tm, pc), lambda i, g, k: (i, g)),
               pl.BlockSpec((None, pc, pc), lambda i, g, k: (g, 0, 0)),
               pl.BlockSpec((tm, pc), lambda i, g, k: (i, g)),
               jax.ShapeDtypeStruct((L, D), out_dtype), (L // tm, ng, 1), dims)


def _group_wgrad(name, p, dy, ng):
    L, D = p.shape
    pc = D // ng
    tk = _row_tile(L, 512)
    return _mm(name, p, dy,
               pl.BlockSpec((tk, pc), lambda g, j, k: (k, g)),
               pl.BlockSpec((tk, pc), lambda g, j, k: (k, g)),
               pl.BlockSpec((None, pc, pc), lambda g, j, k: (g, 0, 0)),
               jax.ShapeDtypeStruct((ng, pc, pc), _ACT), (ng, 1, L // tk), "tn")


V_GPOST, V_GATE, V_YSCALE, V_GPRE, V_SHIFT, V_SCALE = range(6)
R_SHIFT, R_SCALE, R_GPRE, R_GATE, R_GPOST = range(5)
ROW_TILE = 256


def _rstd(v):
    return lax.rsqrt(jnp.mean(v * v, axis=-1, keepdims=True) + RMS_EPS)


def _rows_fwd(name, xres, y, vec, *, add=False, target=None, want_x=True, u_dtype=None):
    L, D = xres.shape
    tm = _row_tile(L, ROW_TILE)
    has_y = y is not None
    last = target is not None
    has_u = u_dtype is not None

    def body(*refs):
        refs = list(refs)
        xres_ref = refs.pop(0)
        y_ref = refs.pop(0) if has_y else None
        vec_ref = refs.pop(0)
        tgt_ref = refs.pop(0) if last else None
        xnew = xres_ref[...]
        if has_y and add:
            xnew = xnew + y_ref[...]
        elif has_y:
            ye = y_ref[...] * vec_ref[V_YSCALE:V_YSCALE + 1, :]
            xnew = xnew + vec_ref[V_GATE:V_GATE + 1, :] * (ye * _rstd(ye) * vec_ref[V_GPOST:V_GPOST + 1, :])
        if last:
            dx_ref, loss_ref = refs
            diff = xnew - tgt_ref[...]
            dx_ref[...] = diff * (1.0 / D)

            @pl.when(pl.program_id(0) == 0)
            def _():
                loss_ref[...] = jnp.zeros_like(loss_ref)

            loss_ref[...] += jnp.sum(diff * diff) * (0.5 / D)
            return
        if want_x:
            refs.pop(0)[...] = xnew
        if has_u:
            u_ref, = refs
            n = xnew * _rstd(xnew) * vec_ref[V_GPRE:V_GPRE + 1, :]
            u_ref[...] = (n * (1.0 + vec_ref[V_SCALE:V_SCALE + 1, :]) + vec_ref[V_SHIFT:V_SHIFT + 1, :]).astype(u_ref.dtype)

    row = pl.BlockSpec((tm, D), lambda i: (i, 0))
    vspec = pl.BlockSpec((8, D), lambda i: (0, 0))
    ins, in_specs = [xres], [row]
    if has_y:
        ins.append(y)
        in_specs.append(row)
    ins.append(vec)
    in_specs.append(vspec)
    out_shape, out_specs = [], []
    if last:
        ins.append(target)
        in_specs.append(row)
        out_shape = [jax.ShapeDtypeStruct((L, D), F32), jax.ShapeDtypeStruct((8, LANE), F32)]
        out_specs = [row, pl.BlockSpec((8, LANE), lambda i: (0, 0))]
    else:
        if want_x:
            out_shape.append(jax.ShapeDtypeStruct((L, D), F32))
            out_specs.append(row)
        if has_u:
            out_shape.append(jax.ShapeDtypeStruct((L, D), u_dtype))
            out_specs.append(row)
    return pl.pallas_call(body, name=name, out_shape=out_shape, grid=(L // tm,), in_specs=in_specs,
                          out_specs=out_specs, compiler_params=_params(("arbitrary",)))(*ins)


def _rows_bwd(name, dxd, du, xnew, y, vec, dy_dtype=F32, want_dx=True):
    L, D = xnew.shape if xnew is not None else dxd.shape
    tm = _row_tile(L, ROW_TILE)
    has_dxd, has_pre, has_post = dxd is not None, du is not None, y is not None

    def body(*refs):
        refs = list(refs)
        dxd_ref = refs.pop(0) if has_dxd else None
        du_ref = refs.pop(0) if has_pre else None
        xnew_ref = refs.pop(0) if has_pre else None
        y_ref = refs.pop(0) if has_post else None
        vec_ref = refs.pop(0)
        dx_ref = refs.pop(0) if want_dx else None
        dy_ref = refs.pop(0) if has_post else None
        red_ref, = refs

        @pl.when(pl.program_id(0) == 0)
        def _():
            red_ref[...] = jnp.zeros_like(red_ref)

        def acc(rw, val):
            red_ref[rw:rw + 1, :] += jnp.sum(val, axis=0, keepdims=True)

        dxn = dxd_ref[...] if has_dxd else None
        if has_pre:
            xn = xnew_ref[...]
            r = _rstd(xn)
            nh = xn * r
            gpre = vec_ref[V_GPRE:V_GPRE + 1, :]
            dub = du_ref[...].astype(F32)
            acc(R_SHIFT, dub)
            acc(R_SCALE, dub * (nh * gpre))
            drn = dub * (1.0 + vec_ref[V_SCALE:V_SCALE + 1, :])
            acc(R_GPRE, drn * nh)
            dnh = drn * gpre
            t = r * (dnh - nh * jnp.mean(dnh * nh, axis=-1, keepdims=True))
            dxn = t if dxn is None else dxn + t
        if want_dx:
            dx_ref[...] = dxn
        if has_post:
            ye = y_ref[...] * vec_ref[V_YSCALE:V_YSCALE + 1, :]
            ry = _rstd(ye)
            yh = ye * ry
            gpost = vec_ref[V_GPOST:V_GPOST + 1, :]
            acc(R_GATE, dxn * (yh * gpost))
            drn2 = dxn * vec_ref[V_GATE:V_GATE + 1, :]
            acc(R_GPOST, drn2 * yh)
            dyh = drn2 * gpost
            dy_ref[...] = (ry * (dyh - yh * jnp.mean(dyh * yh, axis=-1, keepdims=True))).astype(dy_ref.dtype)

    row = pl.BlockSpec((tm, D), lambda i: (i, 0))
    vspec = pl.BlockSpec((8, D), lambda i: (0, 0))
    ins, in_specs = [], []
    for a in ([dxd] if has_dxd else []) + ([du, xnew] if has_pre else []) + ([y] if has_post else []):
        ins.append(a)
        in_specs.append(row)
    ins.append(vec)
    in_specs.append(vspec)
    out_shape, out_specs = [], []
    if want_dx:
        out_shape.append(jax.ShapeDtypeStruct((L, D), F32))
        out_specs.append(row)
    if has_post:
        out_shape.append(jax.ShapeDtypeStruct((L, D), dy_dtype))
        out_specs.append(row)
    out_shape.append(jax.ShapeDtypeStruct((8, D), F32))
    out_specs.append(vspec)
    return pl.pallas_call(body, name=name, out_shape=out_shape, grid=(L // tm,), in_specs=in_specs,
                          out_specs=out_specs, compiler_params=_params(("arbitrary",)))(*ins)


def _colscale_bwd(name, dy, ypre, scale):
    L, D = dy.shape
    tm = _row_tile(L, ROW_TILE)

    def body(dy_ref, yp_ref, s_ref, o_ref, red_ref):
        @pl.when(pl.program_id(0) == 0)
        def _():
            red_ref[...] = jnp.zeros_like(red_ref)

        d = dy_ref[...]
        o_ref[...] = (d * s_ref[0:1, :]).astype(o_ref.dtype)
        red_ref[0:1, :] += jnp.sum(d * yp_ref[...], axis=0, keepdims=True)

    row = pl.BlockSpec((tm, D), lambda i: (i, 0))
    vspec = pl.BlockSpec((8, D), lambda i: (0, 0))
    return pl.pallas_call(body, name=name, grid=(L // tm,), in_specs=[row, row, vspec], out_specs=[row, vspec],
                          out_shape=[jax.ShapeDtypeStruct((L, D), _ACT), jax.ShapeDtypeStruct((8, D), F32)],
                          compiler_params=_params(("arbitrary",)))(dy, ypre, scale)


def _sigmoid(v):
    return 1.0 / (1.0 + jnp.exp(-v))


def _glu_fwd(name, vg):
    _, L, D = vg.shape
    tm = _row_tile(L, ROW_TILE)

    def body(vg_ref, o_ref):
        o_ref[...] = vg_ref[0].astype(F32) * _sigmoid(vg_ref[1].astype(F32))

    return pl.pallas_call(body, name=name, grid=(L // tm,),
                          in_specs=[pl.BlockSpec((2, tm, D), lambda i: (0, i, 0))],
                          out_specs=pl.BlockSpec((tm, D), lambda i: (i, 0)),
                          out_shape=jax.ShapeDtypeStruct((L, D), F32),
                          compiler_params=_params(("parallel",)))(vg)


def _glu_bwd(name, vg, dout):
    _, L, D = vg.shape
    tm = _row_tile(L, ROW_TILE)

    def body(vg_ref, d_ref, o_ref):
        val, s = vg_ref[0].astype(F32), _sigmoid(vg_ref[1].astype(F32))
        d = d_ref[...]
        o_ref[0] = (d * s).astype(o_ref.dtype)
        o_ref[1] = (d * val * s * (1.0 - s)).astype(o_ref.dtype)

    return pl.pallas_call(body, name=name, grid=(L // tm,),
                          in_specs=[pl.BlockSpec((2, tm, D), lambda i: (0, i, 0)), pl.BlockSpec((tm, D), lambda i: (i, 0))],
                          out_specs=pl.BlockSpec((2, tm, D), lambda i: (0, i, 0)),
                          out_shape=jax.ShapeDtypeStruct((2, L, D), _ACT),
                          compiler_params=_params(("parallel",)))(vg, dout)


CONV_ROWS = 256


def _row_pick(blk, idx):
    rows = lax.broadcasted_iota(jnp.int32, blk.shape, 0)
    return jnp.sum(jnp.where(rows == idx, blk, 0.0), axis=0, keepdims=True)


def _shifted(ref, r0, rc, L):
    cur = ref[pl.ds(r0, rc), :].astype(F32)
    before = ref[pl.ds(pl.multiple_of(jnp.maximum(r0 - 16, 0), 16), 16), :].astype(F32)
    after = ref[pl.ds(pl.multiple_of(jnp.minimum(r0 + rc, L - 16), 16), 16), :].astype(F32)
    prev_row = jnp.where(r0 > 0, _row_pick(before, 15), 0.0)
    next_row = jnp.where(r0 + rc < L, _row_pick(after, 0), 0.0)
    rows = lax.broadcasted_iota(jnp.int32, cur.shape, 0)
    up = jnp.where(rows == 0, prev_row, pltpu.roll(cur, 1, 0))
    down = jnp.where(rows == rc - 1, next_row, pltpu.roll(cur, rc - 1, 0))
    return up, cur, down


def _silu_parts(g):
    s = _sigmoid(g)
    return g * s, s


def _conv_swiglu_fwd(name, h, cw):
    _, L, FF = h.shape
    rc = _row_tile(L, CONV_ROWS)

    def body(h_ref, cw_ref, o_ref):
        def chunk(ci, _):
            r0 = pl.multiple_of(ci * rc, rc)
            hc = []
            for half in range(2):
                up, cur, down = _shifted(h_ref.at[half], r0, rc, L)
                hc.append(up * cw_ref[half, 0:1, :] + cur * cw_ref[half, 1:2, :] + down * cw_ref[half, 2:3, :]
                          + cw_ref[half, 3:4, :])
            o_ref[pl.ds(r0, rc), :] = (_silu_parts(hc[1])[0] * hc[0]).astype(o_ref.dtype)
            return 0

        lax.fori_loop(0, L // rc, chunk, 0)

    return pl.pallas_call(body, name=name, grid=(FF // LANE,),
                          in_specs=[pl.BlockSpec((2, L, LANE), lambda j: (0, 0, j)),
                                    pl.BlockSpec((2, 8, LANE), lambda j: (0, 0, j))],
                          out_specs=pl.BlockSpec((L, LANE), lambda j: (0, j)),
                          out_shape=jax.ShapeDtypeStruct((L, FF), _ACT),
                          compiler_params=_params(("parallel",)))(h, cw)


def _conv_swiglu_bwd(name, h, cw, dact, rider=None):
    _, L, FF = h.shape
    rc = _row_tile(L, CONV_ROWS)

    def body(h_ref, cw_ref, da_ref, dh_ref, dcw_ref, dhc_ref):
        def chunk(ci, acc):
            r0 = pl.multiple_of(ci * rc, rc)
            taps, hc = [], []
            for half in range(2):
                t = _shifted(h_ref.at[half], r0, rc, L)
                taps.append(t)
                hc.append(t[0] * cw_ref[half, 0:1, :] + t[1] * cw_ref[half, 1:2, :] + t[2] * cw_ref[half, 2:3, :]
                          + cw_ref[half, 3:4, :])
            d = da_ref[pl.ds(r0, rc), :].astype(F32)
            act, s = _silu_parts(hc[1])
            dhc = (d * act, d * hc[0] * (s + act * (1.0 - s)))
            new = []
            for half in range(2):
                dhc_ref[half, pl.ds(r0, rc), :] = dhc[half]
                for k in range(3):
                    new.append(acc[4 * half + k] + jnp.sum(dhc[half] * taps[half][k], axis=0, keepdims=True))
                new.append(acc[4 * half + 3] + jnp.sum(dhc[half], axis=0, keepdims=True))
            return tuple(new)

        zero = jnp.zeros((1, LANE), F32)
        acc = lax.fori_loop(0, L // rc, chunk, (zero,) * 8)
        dcw_ref[...] = jnp.zeros_like(dcw_ref)
        for half in range(2):
            for k in range(4):
                dcw_ref[half, k:k + 1, :] = acc[4 * half + k]

        def chunk2(ci, _):
            r0 = pl.multiple_of(ci * rc, rc)
            for half in range(2):
                up, cur, down = _shifted(dhc_ref.at[half], r0, rc, L)
                dh_ref[half, pl.ds(r0, rc), :] = (down * cw_ref[half, 0:1, :] + cur * cw_ref[half, 1:2, :]
                                                  + up * cw_ref[half, 2:3, :]).astype(dh_ref.dtype)
            return 0

        lax.fori_loop(0, L // rc, chunk2, 0)

    return _pcall(body, name=name, grid=(FF // LANE,),
                  in_specs=[pl.BlockSpec((2, L, LANE), lambda j: (0, 0, j)),
                            pl.BlockSpec((2, 8, LANE), lambda j: (0, 0, j)),
                            pl.BlockSpec((L, LANE), lambda j: (0, j))],
                  out_specs=[pl.BlockSpec((2, L, LANE), lambda j: (0, 0, j)),
                             pl.BlockSpec((2, 8, LANE), lambda j: (0, 0, j))],
                  out_shape=[jax.ShapeDtypeStruct((2, L, FF), _ACT), jax.ShapeDtypeStruct((2, 8, FF), F32)],
                  scratch_shapes=[pltpu.VMEM((2, L, LANE), F32)], ins=(h, cw, dact), sem=("parallel",), rider=rider)


POOL_ROWS = 256
POOL_TILE = 256


def _pool_bands(transpose):
    i = jnp.arange(POOL_ROWS)[:, None]
    j = jnp.arange(POOL_ROWS + 2 * POOL_HALO)[None, :] - POOL_HALO
    bands = []
    for w in POOL_WINDOWS:
        lo, hi = (-(w // 2 - 1), w // 2) if transpose else (-(w // 2), w // 2 - 1)
        bands.append(((j - i >= lo) & (j - i <= hi)).astype(_ACT))
    return jnp.stack(bands)


def _pool_window(name, u, transpose, out_dtype):
    L, D = u.shape
    ng = len(POOL_WINDOWS)
    pc = D // ng
    tn = min(POOL_TILE, pc)
    rc = _row_tile(L, POOL_ROWS)
    bands = _pool_bands(transpose)
    if rc != POOL_ROWS:
        bands = bands[:, :rc, :rc + 2 * POOL_HALO]
    halo = POOL_HALO

    def body(u_ref, band_ref, o_ref, hi_ref, lo_ref):
        g = (pl.program_id(0) * tn) // pc
        half = jnp.zeros((1, 1), jnp.int32)
        for k, w in enumerate(POOL_WINDOWS):
            half = jnp.where(g == k, w // 2, half)
        zeros = jnp.zeros((halo, tn), _ACT)
        for ref in (hi_ref, lo_ref):
            ref[0:halo, :] = zeros
            ref[halo + L:2 * halo + L, :] = zeros

        def inv_count(r0):
            t = r0 + lax.broadcasted_iota(jnp.int32, (rc, tn), 0)
            lo = jnp.clip(t - half, 0, L - 1)
            hi = jnp.clip(t + half - 1, 0, L - 1)
            return 1.0 / (hi - lo + 1).astype(F32)

        def split(ci, _):
            r0 = pl.multiple_of(ci * rc, rc)
            v = u_ref[pl.ds(r0, rc), :].astype(F32)
            if transpose:
                v = v * inv_count(r0)
            hi = v.astype(_ACT)
            dst = pl.ds(pl.multiple_of(r0 + halo, halo), rc)
            hi_ref[dst, :] = hi
            lo_ref[dst, :] = (v - hi.astype(F32)).astype(_ACT)
            return 0

        lax.fori_loop(0, L // rc, split, 0)
        band = band_ref[...]

        def chunk(ci, _):
            r0 = pl.multiple_of(ci * rc, rc)
            win = pl.ds(r0, rc + 2 * halo)
            s = (jnp.dot(band, hi_ref[win, :], preferred_element_type=F32)
                 + jnp.dot(band, lo_ref[win, :], preferred_element_type=F32))
            if not transpose:
                s = s * inv_count(r0)
            o_ref[pl.ds(r0, rc), :] = (s - u_ref[pl.ds(r0, rc), :].astype(F32)).astype(o_ref.dtype)
            return 0

        lax.fori_loop(0, L // rc, chunk, 0)

    return pl.pallas_call(body, name=name, grid=(D // tn,),
                          in_specs=[pl.BlockSpec((L, tn), lambda j: (0, j)),
                                    pl.BlockSpec((None, rc, rc + 2 * halo), lambda j: ((j * tn) // pc, 0, 0))],
                          out_specs=pl.BlockSpec((L, tn), lambda j: (0, j)),
                          out_shape=jax.ShapeDtypeStruct((L, D), out_dtype),
                          scratch_shapes=[pltpu.VMEM((L + 2 * halo, tn), _ACT), pltpu.VMEM((L + 2 * halo, tn), _ACT)],
                          compiler_params=_params(("parallel",)))(u, bands)


S5_ROWS = 512


def _slab(start):
    return pl.ds(start if isinstance(start, int) else pl.multiple_of(start, NSEG), NSEG)


def _cmul(ar, ai, br, bi):
    return ar * br - ai * bi, ar * bi + ai * br


def _cpow(ar, ai, n):
    rr, ri = None, None
    br, bi = ar, ai
    while n:
        if n & 1:
            rr, ri = (br, bi) if rr is None else _cmul(rr, ri, br, bi)
        n >>= 1
        if n:
            br, bi = _cmul(br, bi, br, bi)
    return rr, ri


def _pow_table(pw_ref, ar, ai, n):
    W = ar.shape[1]
    pr, pi = ar, ai
    for r in range(NSEG):
        pw_ref[0, r:r + 1, :] = pr
        pw_ref[1, r:r + 1, :] = pi
        if r < NSEG - 1:
            pr, pi = _cmul(pr, pi, ar, ai)
    a8r, a8i = (jnp.broadcast_to(v, (NSEG, W)) for v in _cpow(ar, ai, NSEG))

    def step(k, carry):
        nr, ni = _cmul(carry[0], carry[1], a8r, a8i)
        pw_ref[0, _slab(k * NSEG), :] = nr
        pw_ref[1, _slab(k * NSEG), :] = ni
        return nr, ni

    lax.fori_loop(1, n // NSEG, step, (pw_ref[0, 0:NSEG, :], pw_ref[1, 0:NSEG, :]))


def _seg_scan(sr_ref, si_ref, tmp_ref, pw_ref, row0, n, ar, ai, h0, rev, conj=False):
    W = ar.shape[1]
    arb, aib = jnp.broadcast_to(ar, (NSEG, W)), jnp.broadcast_to(ai, (NSEG, W))

    def rows(s):
        t = (n - 1 - s) if rev else s
        return _slab(row0 + t * NSEG)

    def step(s, carry):
        hr, hi = carry
        sl = rows(s)
        nr = arb * hr - aib * hi + sr_ref[sl, :]
        ni = arb * hi + aib * hr + si_ref[sl, :]
        sr_ref[sl, :] = nr
        si_ref[sl, :] = ni
        return nr, ni

    zero = jnp.zeros((NSEG, W), F32)
    fr, fi = lax.fori_loop(0, n, step, (zero, zero), unroll=2)
    tmp_ref[0] = fr
    tmp_ref[1] = fi
    anr, ani = _cpow(ar, ai, n)
    cr, ci = h0
    for j in (range(NSEG - 1, -1, -1) if rev else range(NSEG)):
        tmp_ref[2, j:j + 1, :] = cr
        tmp_ref[3, j:j + 1, :] = ci
        pr, pi = _cmul(anr, ani, cr, ci)
        cr, ci = tmp_ref[0, j:j + 1, :] + pr, tmp_ref[1, j:j + 1, :] + pi
    cmr, cmi = tmp_ref[2], tmp_ref[3]

    def fix(k, _):
        for r in range(NSEG):
            row = pl.ds(pl.multiple_of(k * NSEG, NSEG) + r, 1)
            pr = jnp.broadcast_to(pw_ref[0, row, :], (NSEG, W))
            pi = jnp.broadcast_to(pw_ref[1, row, :], (NSEG, W))
            if conj:
                pi = -pi
            sl = rows(k * NSEG + r)
            sr_ref[sl, :] += pr * cmr - pi * cmi
            si_ref[sl, :] += pr * cmi + pi * cmr
        return 0

    lax.fori_loop(0, n // NSEG, fix, 0)
    return (cr, ci), (cmr, cmi)


def _gelu_tanh(y):
    k = math.sqrt(2.0 / math.pi)
    t = jnp.tanh(k * (y + 0.044715 * y * y * y))
    return 0.5 * y * (1.0 + t), t


def _s5_chunks(L):
    rc = _row_tile(L, S5_ROWS)
    return [(r, rc) for r in range(0, L, rc)]


def _s5_project(u_ref, uc_ref, bre, bim, sr_ref, si_ref, L, LC):
    for ref, base, n in ((u_ref, 0, L), (uc_ref, L, LC)):
        for r, rc in _s5_chunks(n):
            ub = ref[r:r + rc, :].astype(_ACT)
            sr_ref[base + r:base + r + rc, :] = jnp.dot(ub, bre, preferred_element_type=F32)
            si_ref[base + r:base + r + rc, :] = jnp.dot(ub, bim, preferred_element_type=F32)


def _s5_states(sr_ref, si_ref, tmp_ref, pw_ref, ar, ai, L, LC, rev):
    W = ar.shape[1]
    zero = (jnp.zeros((1, W), F32), jnp.zeros((1, W), F32))
    hctx, cm_ctx = _seg_scan(sr_ref, si_ref, tmp_ref, pw_ref, L, LC // NSEG, ar, ai, zero, rev)
    _, cm_lat = _seg_scan(sr_ref, si_ref, tmp_ref, pw_ref, 0, L // NSEG, ar, ai, hctx, rev)
    return cm_lat, cm_ctx


def _s5_fwd(u, uc, bblk, cblk, apar, dsk, rider=None):
    L, D = u.shape
    LC = uc.shape[0]
    NT, W, TC = D // S5_TILE_CH, S5_TILE_W, S5_TILE_CH

    def body(u_ref, uc_ref, b_ref, c_ref, a_ref, d_ref, y_ref, z_ref, sr_ref, si_ref, tmp_ref, pw_ref):
        for r, rc in _s5_chunks(L):
            y_ref[r:r + rc, :] = u_ref[r:r + rc, :].astype(F32) * d_ref[0:1, :]
        for d in range(2):
            ar, ai = a_ref[2 * d:2 * d + 1, :], a_ref[2 * d + 1:2 * d + 2, :]
            _pow_table(pw_ref, ar, ai, L // NSEG)
            _s5_project(u_ref, uc_ref, b_ref[2 * d], b_ref[2 * d + 1], sr_ref, si_ref, L, LC)
            _s5_states(sr_ref, si_ref, tmp_ref, pw_ref, ar, ai, L, LC, rev=(d == 1))
            cre, cim = c_ref[2 * d], c_ref[2 * d + 1]
            for r, rc in _s5_chunks(L):
                y_ref[r:r + rc, :] += (jnp.dot(sr_ref[r:r + rc, :].astype(_ACT), cre, preferred_element_type=F32)
                                       - jnp.dot(si_ref[r:r + rc, :].astype(_ACT), cim, preferred_element_type=F32))
        for r, rc in _s5_chunks(L):
            z_ref[r:r + rc, :] = _gelu_tanh(y_ref[r:r + rc, :])[0].astype(z_ref.dtype)

    col = lambda n: pl.BlockSpec((n, TC), lambda j: (0, j))
    return _pcall(
        body, name="s5_fwd", grid=(NT,), ins=(u, uc, bblk, cblk, apar, dsk), sem=("parallel",), rider=rider,
        in_specs=[col(L), col(LC),
                  pl.BlockSpec((None, 4, TC, W), lambda j: (j, 0, 0, 0)),
                  pl.BlockSpec((None, 4, W, TC), lambda j: (j, 0, 0, 0)),
                  pl.BlockSpec((None, 8, W), lambda j: (j, 0, 0)),
                  pl.BlockSpec((8, TC), lambda j: (0, j))],
        out_specs=[col(L), col(L)],
        out_shape=[jax.ShapeDtypeStruct((L, D), F32), jax.ShapeDtypeStruct((L, D), _ACT)],
        scratch_shapes=[pltpu.VMEM((L + LC, W), F32), pltpu.VMEM((L + LC, W), F32), pltpu.VMEM((4, NSEG, W), F32),
                        pltpu.VMEM((2, L // NSEG, W), F32)])


def _s5_bwd(u, uc, dz, y, bblk, cblk, apar, dsk, rider=None):
    L, D = u.shape
    LC = uc.shape[0]
    NT, W, TC = D // S5_TILE_CH, S5_TILE_W, S5_TILE_CH
    nl, nc = L // NSEG, LC // NSEG

    def body(u_ref, uc_ref, dz_ref, y_ref, b_ref, c_ref, a_ref, d_ref,
             du_ref, duc_ref, db_ref, dc_ref, da_ref, dd_ref,
             hr_ref, hi_ref, gr_ref, gi_ref, dy_ref, tmp_ref, pw_ref):
        ddacc = jnp.zeros((1, TC), F32)
        for r, rc in _s5_chunks(L):
            yv = y_ref[r:r + rc, :]
            g, t = _gelu_tanh(yv)
            k = math.sqrt(2.0 / math.pi)
            dg = 0.5 * (1.0 + t) + 0.5 * yv * (1.0 - t * t) * k * (1.0 + 3 * 0.044715 * yv * yv)
            dy = dz_ref[r:r + rc, :].astype(F32) * dg
            uv = u_ref[r:r + rc, :].astype(F32)
            ddacc = ddacc + jnp.sum(dy * uv, axis=0, keepdims=True)
            du_ref[r:r + rc, :] = dy * d_ref[0:1, :]
            dy_ref[r:r + rc, :] = dy.astype(dy_ref.dtype)
        dd_ref[...] = jnp.zeros_like(dd_ref)
        dd_ref[0:1, :] = ddacc
        duc_ref[...] = jnp.zeros_like(duc_ref)
        da_ref[...] = jnp.zeros_like(da_ref)
        nt = (((1,), (1,)), ((), ()))
        tn = (((0,), (0,)), ((), ()))
        for d in range(2):
            rev = d == 1
            ar, ai = a_ref[2 * d:2 * d + 1, :], a_ref[2 * d + 1:2 * d + 2, :]
            bre, bim = b_ref[2 * d], b_ref[2 * d + 1]
            cre, cim = c_ref[2 * d], c_ref[2 * d + 1]
            _pow_table(pw_ref, ar, ai, nl)
            _s5_project(u_ref, uc_ref, bre, bim, hr_ref, hi_ref, L, LC)
            cm_lat, cm_ctx = _s5_states(hr_ref, hi_ref, tmp_ref, pw_ref, ar, ai, L, LC, rev)
            cml_r, cml_i, cmc_r, cmc_i = cm_lat[0], cm_lat[1], cm_ctx[0], cm_ctx[1]
            dcr = jnp.zeros((W, TC), F32)
            dci = jnp.zeros((W, TC), F32)
            for r, rc in _s5_chunks(L):
                dyb = dy_ref[r:r + rc, :]
                gr_ref[r:r + rc, :] = lax.dot_general(dyb, cre, nt, preferred_element_type=F32)
                gi_ref[r:r + rc, :] = -lax.dot_general(dyb, cim, nt, preferred_element_type=F32)
                dcr = dcr + lax.dot_general(hr_ref[r:r + rc, :].astype(_ACT), dyb, tn, preferred_element_type=F32)
                dci = dci - lax.dot_general(hi_ref[r:r + rc, :].astype(_ACT), dyb, tn, preferred_element_type=F32)
            dc_ref[2 * d] = dcr
            dc_ref[2 * d + 1] = dci
            gr_ref[L:L + LC, :] = jnp.zeros((LC, W), F32)
            gi_ref[L:L + LC, :] = jnp.zeros((LC, W), F32)
            zero = (jnp.zeros((1, W), F32), jnp.zeros((1, W), F32))
            glat, _ = _seg_scan(gr_ref, gi_ref, tmp_ref, pw_ref, 0, nl, ar, -ai, zero, not rev, conj=True)
            _seg_scan(gr_ref, gi_ref, tmp_ref, pw_ref, L, nc, ar, -ai, glat, not rev, conj=True)

            def da_part(row0, n, cmr, cmi):
                def rows(s):
                    t = (n - 1 - s) if rev else s
                    return _slab(row0 + t * NSEG)

                g0r, g0i = gr_ref[rows(0), :], gi_ref[rows(0), :]
                acc0 = (cmr * g0r + cmi * g0i, cmr * g0i - cmi * g0r)

                def step(s, acc):
                    hpr, hpi = hr_ref[rows(s - 1), :], hi_ref[rows(s - 1), :]
                    gr, gi = gr_ref[rows(s), :], gi_ref[rows(s), :]
                    return acc[0] + hpr * gr + hpi * gi, acc[1] + hpr * gi - hpi * gr

                return lax.fori_loop(1, n, step, acc0, unroll=2)

            lr, li = da_part(0, nl, cml_r, cml_i)
            qr, qi = da_part(L, nc, cmc_r, cmc_i)
            da_ref[2 * d:2 * d + 1, :] = jnp.sum(lr + qr, axis=0, keepdims=True)
            da_ref[2 * d + 1:2 * d + 2, :] = jnp.sum(li + qi, axis=0, keepdims=True)
            dbr = jnp.zeros((TC, W), F32)
            dbi = jnp.zeros((TC, W), F32)
            for ref, oref, base, n in ((u_ref, du_ref, 0, L), (uc_ref, duc_ref, L, LC)):
                for r, rc in _s5_chunks(n):
                    ub = ref[r:r + rc, :].astype(_ACT)
                    gr = gr_ref[base + r:base + r + rc, :].astype(_ACT)
                    gi = gi_ref[base + r:base + r + rc, :].astype(_ACT)
                    dbr = dbr + lax.dot_general(ub, gr, tn, preferred_element_type=F32)
                    dbi = dbi + lax.dot_general(ub, gi, tn, preferred_element_type=F32)
                    oref[r:r + rc, :] += (lax.dot_general(gr, bre, nt, preferred_element_type=F32)
                                          + lax.dot_general(gi, bim, nt, preferred_element_type=F32))
            db_ref[2 * d] = dbr
            db_ref[2 * d + 1] = dbi

    col = lambda n: pl.BlockSpec((n, TC), lambda j: (0, j))
    bspec = pl.BlockSpec((None, 4, TC, W), lambda j: (j, 0, 0, 0))
    cspec = pl.BlockSpec((None, 4, W, TC), lambda j: (j, 0, 0, 0))
    aspec = pl.BlockSpec((None, 8, W), lambda j: (j, 0, 0))
    return _pcall(
        body, name="s5_bwd", grid=(NT,), ins=(u, uc, dz, y, bblk, cblk, apar, dsk), sem=("parallel",),
        vmem=VMEM_LIMIT_BIG, rider=rider,
        in_specs=[col(L), col(LC), col(L), col(L), bspec, cspec, aspec, pl.BlockSpec((8, TC), lambda j: (0, j))],
        out_specs=[col(L), col(LC), bspec, cspec, aspec, pl.BlockSpec((None, 8, TC), lambda j: (j, 0, 0))],
        out_shape=[jax.ShapeDtypeStruct((L, D), F32), jax.ShapeDtypeStruct((LC, D), F32),
                   jax.ShapeDtypeStruct((NT, 4, TC, W), F32), jax.ShapeDtypeStruct((NT, 4, W, TC), F32),
                   jax.ShapeDtypeStruct((NT, 8, W), F32), jax.ShapeDtypeStruct((NT, 8, TC), F32)],
        scratch_shapes=[pltpu.VMEM((L + LC, W), F32), pltpu.VMEM((L + LC, W), F32),
                        pltpu.VMEM((L + LC, W), F32), pltpu.VMEM((L + LC, W), F32),
                        pltpu.VMEM((L, TC), _ACT), pltpu.VMEM((4, NSEG, W), F32), pltpu.VMEM((2, nl, W), F32)])


ADA_ROWS = 16


def _silu_rows(c_ref):
    c = c_ref[...]
    return c * _sigmoid(c)


def _ada_fwd(cmat, ada_w, ada_b):
    nl, D, n = ada_w.shape
    tn = _row_tile(n, 512)

    def body(c_ref, w_ref, b_ref, o_ref):
        a = _silu_rows(c_ref).astype(_ACT)
        o_ref[...] = jnp.dot(a, w_ref[...].astype(_ACT), preferred_element_type=F32) + b_ref[...]

    return pl.pallas_call(body, name="ada_fwd", grid=(nl, n // tn),
                          in_specs=[pl.BlockSpec((ADA_ROWS, D), lambda l, j: (0, 0)),
                                    pl.BlockSpec((None, D, tn), lambda l, j: (l, 0, j)),
                                    pl.BlockSpec((None, 1, tn), lambda l, j: (l, 0, j))],
                          out_specs=pl.BlockSpec((None, ADA_ROWS, tn), lambda l, j: (l, 0, j)),
                          out_shape=jax.ShapeDtypeStruct((nl, ADA_ROWS, n), F32),
                          compiler_params=_params(("parallel", "parallel")))(cmat, ada_w, ada_b)


def _ada_bwd(cmat, ada_w, dm):
    nl, D, n = ada_w.shape
    tn = _row_tile(n, 512)
    nj = n // tn

    def body(c_ref, w_ref, dm_ref, dw_ref, dc_ref):
        c = c_ref[...]
        s = _sigmoid(c)
        a = (c * s).astype(_ACT)
        dmb = dm_ref[...].astype(_ACT)
        dw_ref[...] = lax.dot_general(a, dmb, (((0,), (0,)), ((), ())), preferred_element_type=F32)
        part = lax.dot_general(dmb, w_ref[...].astype(_ACT), (((1,), (1,)), ((), ())), preferred_element_type=F32)
        part = part * (s * (1.0 + c * (1.0 - s)))

        @pl.when(pl.program_id(1) == 0)
        def _():
            dc_ref[...] = part

        @pl.when(pl.program_id(1) > 0)
        def _():
            dc_ref[...] += part

    return pl.pallas_call(body, name="ada_bwd", grid=(nl, nj),
                          in_specs=[pl.BlockSpec((ADA_ROWS, D), lambda l, j: (0, 0)),
                                    pl.BlockSpec((None, D, tn), lambda l, j: (l, 0, j)),
                                    pl.BlockSpec((None, ADA_ROWS, tn), lambda l, j: (l, 0, j))],
                          out_specs=[pl.BlockSpec((None, D, tn), lambda l, j: (l, 0, j)),
                                     pl.BlockSpec((None, ADA_ROWS, D), lambda l, j: (l, 0, 0))],
                          out_shape=[jax.ShapeDtypeStruct((nl, D, n), F32), jax.ShapeDtypeStruct((nl, ADA_ROWS, D), F32)],
                          compiler_params=_params(("parallel", "arbitrary")))(cmat, ada_w, dm)


def _adamw(name, gparts, w, m, v):
    nl, R, C = w.shape
    n = gparts[0].shape[0]
    tr = R
    for cand in (512, 256, 128, 64, 32, 16, 8):
        if R % cand == 0 and cand * C * 4 <= 2 * 1024 * 1024:
            tr = cand
            break
    nt = R // tr
    bc1 = 1.0 - ADAM_B1 ** ADAM_STEP
    bc2 = 1.0 - ADAM_B2 ** ADAM_STEP

    def body(*refs):
        g_refs = refs[:nl]
        w_ref, m_ref, v_ref, go_ref, d_ref, mo_ref, vo_ref = refs[nl:]
        for l in range(nl):
            @pl.when(pl.program_id(0) == l)
            def _(g_ref=g_refs[l]):
                g = g_ref[0].astype(F32)
                for j in range(1, n):
                    g = g + g_ref[j].astype(F32)
                m2 = ADAM_B1 * m_ref[...] + (1.0 - ADAM_B1) * g
                v2 = ADAM_B2 * v_ref[...] + (1.0 - ADAM_B2) * (g * g)
                go_ref[...] = g
                mo_ref[...] = m2
                vo_ref[...] = v2
                d_ref[...] = -ADAM_LR * ((m2 / bc1) / (jnp.sqrt(v2 / bc2) + ADAM_EPS) + ADAM_WD * w_ref[...])

    def gspec(l):
        return pl.BlockSpec((n, tr, C), lambda lyr, i: (0, jnp.where(lyr < l, 0, jnp.where(lyr > l, nt - 1, i)), 0))

    row = pl.BlockSpec((None, tr, C), lambda lyr, i: (lyr, i, 0))
    out = jax.ShapeDtypeStruct((nl, R, C), F32)
    return _pcall(body, name=name, grid=(nl, nt), in_specs=[gspec(l) for l in range(nl)] + [row, row, row],
                  out_specs=[row, row, row, row], out_shape=[out, out, out, out], ins=(*gparts, w, m, v),
                  sem=("arbitrary", "arbitrary"))


def _sum_parts(name, parts):
    n, R, C = parts.shape

    def body(p_ref, o_ref):
        s = p_ref[0]
        for j in range(1, n):
            s = s + p_ref[j]
        o_ref[...] = s

    return pl.pallas_call(body, name=name, out_shape=jax.ShapeDtypeStruct((R, C), F32),
                          compiler_params=_params(None))(parts)


def _discretize(lam_re, lam_im, log_step, b_re, b_im):
    dt = jnp.exp(log_step)[:, None]
    mag = jnp.exp(lam_re * dt)
    abar_re = mag * jnp.cos(lam_im * dt)
    abar_im = mag * jnp.sin(lam_im * dt)
    nr, ni = abar_re - 1.0, abar_im
    den = lam_re * lam_re + lam_im * lam_im
    fr = (nr * lam_re + ni * lam_im) / den
    fi = (ni * lam_re - nr * lam_im) / den
    bbar_re = fr[..., None] * b_re - fi[..., None] * b_im
    bbar_im = fr[..., None] * b_im + fi[..., None] * b_re
    return abar_re, abar_im, bbar_re, bbar_im


def _s5_pack(abar, bbar, cmat):
    G = abar[0][0].shape[0]
    NT = G // S5_TILE_G
    eye = jnp.eye(S5_TILE_G, dtype=F32)
    rows, bs, cs = [], [], []
    for d in range(2):
        for r in range(2):
            rows.append(abar[d][r].reshape(NT, 1, S5_TILE_W))
            bb = bbar[d][r].reshape(NT, S5_TILE_G, S5_P, S5_CH)
            bs.append(jnp.einsum("jgpc,gh->jgchp", bb, eye).reshape(NT, S5_TILE_CH, S5_TILE_W))
            cc = cmat[d][r].reshape(NT, S5_TILE_G, S5_CH, S5_P)
            cs.append(jnp.einsum("jgcp,gh->jgphc", cc, eye).reshape(NT, S5_TILE_W, S5_TILE_CH))
    apar = jnp.concatenate(rows + [jnp.zeros((NT, 4, S5_TILE_W), F32)], axis=1)
    return apar, jnp.stack(bs, axis=1).astype(_ACT), jnp.stack(cs, axis=1).astype(_ACT)


def _s5_unpack(dapar, dbblk, dcblk, G):
    NT = G // S5_TILE_G
    da = dapar[:, :4, :].reshape(NT, 2, 2, S5_TILE_G, S5_P).transpose(1, 2, 0, 3, 4).reshape(2, 2, G, S5_P)
    idx = jnp.arange(S5_TILE_G)
    db = dbblk.reshape(NT, 2, 2, S5_TILE_G, S5_CH, S5_TILE_G, S5_P)[:, :, :, idx, :, idx, :]
    db = db.transpose(2, 3, 1, 0, 5, 4).reshape(2, 2, G, S5_P, S5_CH)
    dc = dcblk.reshape(NT, 2, 2, S5_TILE_G, S5_P, S5_TILE_G, S5_CH)[:, :, :, idx, :, idx, :]
    dc = dc.transpose(2, 3, 1, 0, 5, 4).reshape(2, 2, G, S5_CH, S5_P)
    return da, db, dc


def _to_segments(a):
    L, D = a.shape
    return a.reshape(NSEG, L // NSEG, D).transpose(1, 0, 2).reshape(L, D)


def _from_segments(a):
    L, D = a.shape
    return a.reshape(L // NSEG, NSEG, D).transpose(1, 0, 2).reshape(L, D)


def _pos_emb(n_tokens, dim):
    rows = n_tokens // GRID_W
    quarter = dim // 4
    omega = 1.0 / (POS_BASE ** (jnp.arange(quarter, dtype=F32) / quarter))

    def enc(p):
        ang = p[:, None] * omega[None, :]
        return jnp.concatenate([jnp.sin(ang), jnp.cos(ang)], axis=-1)

    rtab = enc(jnp.arange(rows, dtype=F32))
    ctab = enc(jnp.arange(GRID_W, dtype=F32))
    return jnp.concatenate([jnp.repeat(rtab, GRID_W, axis=0), jnp.tile(ctab, (rows, 1))], axis=-1)


def _vec(D, **rows):
    names = {"gpost": V_GPOST, "gate": V_GATE, "yscale": V_YSCALE, "gpre": V_GPRE, "shift": V_SHIFT, "scale": V_SCALE}
    out = [jnp.zeros((D,), F32)] * 8
    out[V_YSCALE] = jnp.ones((D,), F32)
    for k, v in rows.items():
        out[names[k]] = v.reshape(D).astype(F32)
    return jnp.stack(out)


def _row0(v, D):
    return jnp.concatenate([v.reshape(1, D).astype(F32), jnp.zeros((7, D), F32)], axis=0)


def _my_block(full, axis, n_local):
    return lax.dynamic_slice_in_dim(full, _my_index() * n_local, n_local, axis)


def kernel(x, c, ctx, c_ctx, ada_w, ada_b, norm_g, s5_lam_re, s5_lam_im, s5_log_step, s5_b_re, s5_b_im, s5_c_re, s5_c_im, s5_d, s5_glu_w, pool_w, pool_scale, ffn_up, ffn_conv, ffn_conv_b, ffn_down, loss_target, m_c_ctx, m_ada_w, m_ada_b, m_norm_g, m_s5_lam_re, m_s5_lam_im, m_s5_log_step, m_s5_b_re, m_s5_b_im, m_s5_c_re, m_s5_c_im, m_s5_d, m_s5_glu_w, m_pool_w, m_pool_scale, m_ffn_up, m_ffn_conv, m_ffn_conv_b, m_ffn_down, v_c_ctx, v_ada_w, v_ada_b, v_norm_g, v_s5_lam_re, v_s5_lam_im, v_s5_log_step, v_s5_b_re, v_s5_b_im, v_s5_c_re, v_s5_c_im, v_s5_d, v_s5_glu_w, v_pool_w, v_pool_scale, v_ffn_up, v_ffn_conv, v_ffn_conv_b, v_ffn_down):
    L, D = x.shape[1], x.shape[2]
    LC = ctx.shape[1]
    G = s5_lam_re.shape[2]
    n_ada = ada_w.shape[2]
    nb_up = ffn_up.shape[2]
    r_down = ffn_down.shape[1]
    FF = N_DEV * r_down
    n_pool = len(POOL_WINDOWS)
    pc = D // n_pool
    pr = pool_w.shape[2]
    ng_loc = norm_g.shape[2]
    me = _my_index()
    axes = ("x", "y", "c")

    up_b = [ffn_up[i].astype(_ACT) for i in range(2)]
    down_b = [ffn_down[i].astype(_ACT) for i in range(2)]
    glu_b = s5_glu_w[0].astype(_ACT)
    pool_b = pool_w[0].reshape(n_pool * pr, pc).astype(_ACT)

    small_loc = jnp.concatenate([c.reshape(-1), norm_g.reshape(-1), pool_scale.reshape(-1), ffn_conv.reshape(-1)])
    n_small = small_loc.shape[0]
    small_g, = _exchange([[jnp.pad(small_loc, (0, (-n_small) % LANE)).reshape(1, -1)]], mode="gather", name="gather_small")
    small_g = small_g.reshape(N_DEV, -1)
    o = 0
    c_all = small_g[:, o:o + D]
    o += D
    ng_all = small_g[:, o:o + 8 * ng_loc].reshape(N_DEV, 2, 4, ng_loc).transpose(1, 2, 0, 3).reshape(2, 4, D)
    o += 8 * ng_loc
    pscale_all = small_g[:, o:o + ng_loc].reshape(D)
    o += ng_loc
    conv_all = small_g[:, o:o + 6 * nb_up].reshape(N_DEV, 2, 3, nb_up).transpose(1, 2, 0, 3).reshape(2, 3, 2 * FF)

    cmat = jnp.concatenate([c_all, c_ctx.reshape(1, D), jnp.zeros((ADA_ROWS - N_DEV - 1, D), F32)], axis=0)
    ada_b_loc = _my_block(ada_b, 1, n_ada).reshape(2, 1, n_ada)
    mods_loc = _ada_fwd(cmat, ada_w, ada_b_loc)
    mods_g, = _exchange([[mods_loc]], mode="gather", name="gather_mods")
    mods_rows = mods_g.reshape(N_DEV, 2, ADA_ROWS, n_ada).transpose(1, 2, 0, 3).reshape(2, ADA_ROWS, 6, D)
    mod = lax.dynamic_index_in_dim(mods_rows, me, axis=1, keepdims=False)
    mod_c = mods_rows[0, N_DEV]

    def disc_all(lr, li, ls, br, bi):
        return [_discretize(lr[d], li[d], ls[d], br[d], bi[d]) for d in range(2)]

    disc, disc_vjp = jax.vjp(disc_all, s5_lam_re[0], s5_lam_im[0], s5_log_step[0], s5_b_re[0], s5_b_im[0])
    apar, bblk, cblk = _s5_pack([(disc[d][0], disc[d][1]) for d in range(2)],
                                [(disc[d][2], disc[d][3]) for d in range(2)],
                                [(s5_c_re[0, d], s5_c_im[0, d]) for d in range(2)])
    dsk = _row0(s5_d[0], D)
    cw = []
    for i in range(2):
        taps = conv_all[i].reshape(3, 2, FF).transpose(1, 0, 2)
        cw.append(jnp.concatenate([taps, ffn_conv_b[i].reshape(2, 1, FF), jnp.zeros((2, 4, FF), F32)], axis=1))

    vecs = {
        "b0": _vec(D, gpre=ng_all[0, 0], shift=mod[0, 0], scale=mod[0, 1]),
        "c0": _vec(D, gpre=ng_all[0, 0], shift=mod_c[0], scale=mod_c[1]),
        "b1": _vec(D, gpost=ng_all[0, 1], gate=mod[0, 2], gpre=ng_all[0, 2], shift=mod[0, 3], scale=mod[0, 4]),
        "b2": _vec(D, gpost=ng_all[0, 3], gate=mod[0, 5], gpre=ng_all[1, 0], shift=mod[1, 0], scale=mod[1, 1]),
        "b3": _vec(D, gpost=ng_all[1, 1], gate=mod[1, 2], yscale=pscale_all, gpre=ng_all[1, 2], shift=mod[1, 3],
                   scale=mod[1, 4]),
        "b4": _vec(D, gpost=ng_all[1, 3], gate=mod[1, 5]),
    }

    x0, u0 = _rows_fwd("rows_fwd_b0", x[0], _pos_emb(L, D), vecs["b0"], add=True, u_dtype=_ACT)
    uc, = _rows_fwd("rows_fwd_ctx", ctx[0], None, vecs["c0"], want_x=False, u_dtype=_ACT)
    u0s, ucs = _to_segments(u0), _to_segments(uc)
    (y_s5, z_s5), (glu_g, up_g0, down_g0) = _s5_fwd(u0s, ucs, bblk, cblk, apar, dsk,
                                                    rider=([[glu_b], [up_b[0]], [down_b[0]]], "gather2"))
    vg = _colblock_fwd("glu_fwd_mm", z_s5, glu_g, 0, _ACT)
    mix0 = _from_segments(_glu_fwd("glu_fwd", vg))
    x1, un0 = _rows_fwd("rows_fwd_b1", x0, mix0, vecs["b1"], u_dtype=_ACT)
    h0, (up_g1,) = _colblock_fwd("ffn0_up", un0, up_g0, 0, _ACT, rider=([[up_b[1]]], "gather2"))
    act0 = _conv_swiglu_fwd("ffn0_conv", h0, cw[0])
    f0, (down_g1, pool_g) = _rowblock_fwd("ffn0_down", act0, down_g0, 0, rider=([[down_b[1]], [pool_b]], "gather2"))
    pool_full = pool_g.reshape(N_DEV, n_pool, pr, pc).transpose(1, 0, 2, 3).reshape(n_pool, pc, pc)
    x2, u1 = _rows_fwd("rows_fwd_b2", x1, f0, vecs["b2"], u_dtype=F32)
    p1 = _pool_window("pool_fwd", u1, False, _ACT)
    ypre1 = _group_mm("pool_fwd_mm", p1, pool_full, "nn", F32)
    x3, un1 = _rows_fwd("rows_fwd_b3", x2, ypre1, vecs["b3"], u_dtype=_ACT)
    h1 = _colblock_fwd("ffn1_up", un1, up_g1, 0, _ACT)
    act1 = _conv_swiglu_fwd("ffn1_conv", h1, cw[1])
    f1 = _rowblock_fwd("ffn1_down", act1, down_g1, 0)
    dx4, loss_blk = _rows_fwd("rows_fwd_b4", x3, f1, vecs["b4"], target=loss_target[0])
    loss = lax.psum(loss_blk[0, 0], axes)

    df1, red4 = _rows_bwd("rows_bwd_b4", dx4, None, None, f1, vecs["b4"], dy_dtype=_ACT, want_dx=False)
    dact1 = _rowblock_dgrad("ffn1_down_dgrad", df1, down_g1, 0)
    ddown1 = _rowblock_wgrad("ffn1_down_wgrad", act1, df1)
    (dh1, dcw1), (gp_down1,) = _conv_swiglu_bwd("ffn1_conv_bwd", h1, cw[1], dact1,
                                                rider=([[ddown1.reshape(N_DEV, r_down, D)]], "scatter"))
    dun1 = _colblock_dgrad("ffn1_up_dgrad", dh1, up_g1, 0, F32)
    dup1 = _colblock_wgrad("ffn1_up_wgrad", un1, dh1)
    dx3, dy3, red3 = _rows_bwd("rows_bwd_b3", dx4, dun1, x3, ypre1, vecs["b3"])
    dypre1, red_ps = _colscale_bwd("pool_scale_bwd", dy3, ypre1, _row0(pscale_all, D))
    dp1 = _group_mm("pool_dgrad", dypre1, pool_full, "nt", F32)
    dpool = _group_wgrad("pool_wgrad", p1, dypre1, n_pool)
    du1 = _pool_window("pool_bwd", dp1, True, F32)
    dx2, df0, red2 = _rows_bwd("rows_bwd_b2", dx3, du1, x2, f0, vecs["b2"], dy_dtype=_ACT)
    dact0 = _rowblock_dgrad("ffn0_down_dgrad", df0, down_g0, 0)
    ddown0 = _rowblock_wgrad("ffn0_down_wgrad", act0, df0)
    dh0, dcw0 = _conv_swiglu_bwd("ffn0_conv_bwd", h0, cw[0], dact0)
    dun0, (gp_down0,) = _colblock_dgrad("ffn0_up_dgrad", dh0, up_g0, 0, F32,
                                        rider=([[ddown0.reshape(N_DEV, r_down, D)]], "scatter"))
    dup0 = _colblock_wgrad("ffn0_up_wgrad", un0, dh0)
    dx1, dmix0, red1 = _rows_bwd("rows_bwd_b1", dx2, dun0, x1, mix0, vecs["b1"])
    dvg = _glu_bwd("glu_bwd", vg, _to_segments(dmix0))
    dz = _colblock_dgrad("glu_dgrad", dvg, glu_g, 0, _ACT)
    dglu = _colblock_wgrad("glu_wgrad", z_s5, dvg)
    dpool_blocks = dpool.reshape(n_pool, N_DEV, pr, pc).transpose(1, 0, 2, 3).reshape(N_DEV, n_pool * pr, pc)
    (du0s, ducs, dbblk, dcblk, dapar, ddsk), (gp_up1, gp_up0, gp_glu, gp_pool) = _s5_bwd(
        u0s, ucs, dz, y_s5, bblk, cblk, apar, dsk, rider=([[dup1], [dup0], [dglu], [dpool_blocks]], "scatter"))
    grad_x, red0 = _rows_bwd("rows_bwd_b0", dx1, _from_segments(du0s), x0, None, vecs["b0"])
    redc, = _rows_bwd("rows_bwd_ctx", None, _from_segments(ducs), ctx[0], None, vecs["c0"], want_dx=False)

    zero_d = jnp.zeros((D,), F32)
    dmod = jnp.stack([
        jnp.stack([red0[R_SHIFT], red0[R_SCALE], red1[R_GATE], red1[R_SHIFT], red1[R_SCALE], red2[R_GATE]]),
        jnp.stack([red2[R_SHIFT], red2[R_SCALE], red3[R_GATE], red3[R_SHIFT], red3[R_SCALE], red4[R_GATE]])])
    dmod_c = jnp.stack([jnp.stack([redc[R_SHIFT], redc[R_SCALE]] + [zero_d] * 4), jnp.zeros((6, D), F32)])
    dm_g, = _exchange([[jnp.stack([dmod, dmod_c], axis=1).reshape(2, 2, 6 * D)]], mode="gather", name="gather_dmods")
    dm_g = dm_g.reshape(N_DEV, 2, 2, 6 * D)
    dm_ctx = _sum_parts("sum_dmod_ctx", dm_g[:, :, 1, :])
    dm_rows = jnp.concatenate([dm_g[:, :, 0, :].transpose(1, 0, 2), dm_ctx[:, None, :]], axis=1)
    grad_ada_b = _sum_parts("sum_ada_b", dm_rows.transpose(1, 0, 2))
    dm_cols = dm_rows.reshape(2, N_DEV + 1, N_DEV, n_ada)
    dm_mine = lax.dynamic_index_in_dim(dm_cols, me, axis=2, keepdims=False)
    dm_mine = jnp.concatenate([dm_mine, jnp.zeros((2, ADA_ROWS - N_DEV - 1, n_ada), F32)], axis=1)
    grad_ada_w, dcond = _ada_bwd(cmat, ada_w, dm_mine)
    dcctx_part = dcond[0, N_DEV] + dcond[1, N_DEV]

    da, db, dc = _s5_unpack(dapar, dbblk, dcblk, G)
    dnorm = jnp.stack([
        jnp.stack([red0[R_GPRE] + redc[R_GPRE], red1[R_GPOST], red1[R_GPRE], red2[R_GPOST]]),
        jnp.stack([red2[R_GPRE], red3[R_GPOST], red3[R_GPRE], red4[R_GPOST]])])
    dconv = jnp.stack([d[:, :3, :].transpose(1, 0, 2).reshape(3, 2 * FF) for d in (dcw0, dcw1)])
    dconv_b = jnp.stack([d[:, 3, :].reshape(2 * FF) for d in (dcw0, dcw1)])
    pieces = [dcctx_part, dnorm, da, db, dc, ddsk[:, 0, :], red_ps[0], dconv, dconv_b]
    flat = jnp.concatenate([p.reshape(-1) for p in pieces])
    n_flat = flat.shape[0]
    per_dev = -(-n_flat // (N_DEV * 8 * LANE)) * 8 * LANE
    flat = jnp.pad(flat, (0, N_DEV * per_dev - n_flat)).reshape(N_DEV, per_dev // LANE, LANE)
    parts, = _exchange([[flat]], mode="scatter", name="scatter_small_grads")
    mine = _sum_parts("sum_small_grads", parts.reshape(N_DEV, per_dev // LANE, LANE))
    summed, = _exchange([[mine]], mode="gather", name="gather_small_grads")
    summed = summed.reshape(-1)
    red_pieces, o = [], 0
    for p in pieces:
        red_pieces.append(summed[o:o + p.size].reshape(p.shape))
        o += p.size
    g_cctx, g_norm, g_a, g_b, g_c, g_d, g_pscale, g_conv, g_conv_b = red_pieces
    cot = [(g_a[d, 0], g_a[d, 1], g_b[d, 0], g_b[d, 1]) for d in range(2)]
    g_lam_re, g_lam_im, g_log_step, g_b_re, g_b_im = disc_vjp(cot)

    out = {}

    def put(name, res, shape):
        out[name] = tuple(r.reshape(shape) for r in res)

    put("ffn_up", _adamw("adamw_ffn_up", [g.reshape(N_DEV, D, nb_up) for g in (gp_up0, gp_up1)],
                         ffn_up, m_ffn_up, v_ffn_up), ffn_up.shape)
    put("ffn_down", _adamw("adamw_ffn_down", [g.reshape(N_DEV, r_down, D) for g in (gp_down0, gp_down1)],
                           ffn_down, m_ffn_down, v_ffn_down), ffn_down.shape)
    put("s5_glu_w", _adamw("adamw_glu", [gp_glu.reshape(N_DEV, D, -1)], s5_glu_w, m_s5_glu_w, v_s5_glu_w),
        s5_glu_w.shape)
    pool_rows = (1, n_pool * pr, pc)
    put("pool_w", _adamw("adamw_pool", [gp_pool.reshape(N_DEV, n_pool * pr, pc)], pool_w.reshape(pool_rows),
                         m_pool_w.reshape(pool_rows), v_pool_w.reshape(pool_rows)), pool_w.shape)
    put("ada_w", _adamw("adamw_ada_w", [grad_ada_w[i][None] for i in range(2)], ada_w, m_ada_w, v_ada_w), ada_w.shape)

    for nm, w, m, v, g in (("s5_b_re", s5_b_re, m_s5_b_re, v_s5_b_re, g_b_re),
                           ("s5_b_im", s5_b_im, m_s5_b_im, v_s5_b_im, g_b_im),
                           ("s5_c_re", s5_c_re, m_s5_c_re, v_s5_c_re, g_c[:, 0]),
                           ("s5_c_im", s5_c_im, m_s5_c_im, v_s5_c_im, g_c[:, 1])):
        rows = (1, w.size // w.shape[-1], w.shape[-1])
        put(nm, _adamw("adamw_" + nm, [g.reshape(rows)], w.reshape(rows), m.reshape(rows), v.reshape(rows)), w.shape)

    small = [
        ("c_ctx", c_ctx, m_c_ctx, v_c_ctx, g_cctx),
        ("ada_b", ada_b, m_ada_b, v_ada_b, grad_ada_b),
        ("norm_g", norm_g, m_norm_g, v_norm_g, _my_block(g_norm, 2, ng_loc)),
        ("s5_lam_re", s5_lam_re, m_s5_lam_re, v_s5_lam_re, g_lam_re),
        ("s5_lam_im", s5_lam_im, m_s5_lam_im, v_s5_lam_im, g_lam_im),
        ("s5_log_step", s5_log_step, m_s5_log_step, v_s5_log_step, g_log_step),
        ("s5_d", s5_d, m_s5_d, v_s5_d, g_d),
        ("pool_scale", pool_scale, m_pool_scale, v_pool_scale, _my_block(g_pscale, 0, ng_loc)),
        ("ffn_conv", ffn_conv, m_ffn_conv, v_ffn_conv, _my_block(g_conv, 2, nb_up)),
        ("ffn_conv_b", ffn_conv_b, m_ffn_conv_b, v_ffn_conv_b, g_conv_b),
    ]
    n_sm = sum(w.size for _, w, _, _, _ in small)
    rows_sm = -(-n_sm // (512 * LANE)) * 512

    def flat_of(k):
        f = jnp.concatenate([t[k].reshape(-1) for t in small])
        return jnp.pad(f, (0, rows_sm * LANE - n_sm)).reshape(rows_sm, LANE)

    res_sm = _adamw("adamw_small", [flat_of(4)[None]], flat_of(1)[None], flat_of(2)[None], flat_of(3)[None])
    o = 0
    for name, w, _, _, _ in small:
        out[name] = tuple(r.reshape(-1)[o:o + w.size].reshape(w.shape) for r in res_sm)
        o += w.size

    order = ["c_ctx", "ada_w", "ada_b", "norm_g", "s5_lam_re", "s5_lam_im", "s5_log_step", "s5_b_re", "s5_b_im",
             "s5_c_re", "s5_c_im", "s5_d", "s5_glu_w", "pool_w", "pool_scale", "ffn_up", "ffn_conv", "ffn_conv_b",
             "ffn_down"]
    return (loss, grad_x.reshape(x.shape), *[out[n][0] for n in order], *[out[n][1] for n in order],
            *[out[n][2] for n in order], *[out[n][3] for n in order])
```

```python
import functools
import math

import jax
import jax.numpy as jnp
from jax import lax
from jax.experimental import pallas as pl
from jax.experimental.pallas import tpu as pltpu

F32 = jnp.float32
_ACT = jnp.bfloat16
N_DEV = 8
NSEG = 8
S5_CH = 16
S5_P = 64
LANE = 128
S5_TILE_CH = LANE
S5_TILE_G = S5_TILE_CH // S5_CH
S5_TILE_W = S5_TILE_G * S5_P
GRID_W = 64
POOL_WINDOWS = (2, 4, 8, 16)
POOL_HALO = 64
RMS_EPS = 1e-6
POS_BASE = 10000.0
ADAM_LR, ADAM_B1, ADAM_B2, ADAM_EPS, ADAM_WD, ADAM_STEP = 0.001, 0.9, 0.999, 1e-08, 0.01, 10
VMEM_LIMIT = 48 * 1024 * 1024
VMEM_LIMIT_BIG = 58 * 1024 * 1024
MESH = pl.DeviceIdType.MESH
ANY = pl.BlockSpec(memory_space=pl.ANY)


def _params(sem, vmem=VMEM_LIMIT):
    return pltpu.CompilerParams(dimension_semantics=sem, vmem_limit_bytes=vmem)


def _my_index():
    return 4 * lax.axis_index("x") + 2 * lax.axis_index("y") + lax.axis_index("c")


def _xchg_plan(groups, mode):
    flat = [(g, l, a) for g, grp in enumerate(groups) for l, a in enumerate(grp)]
    outs = []
    for grp in groups:
        a, rows = _rows_of(grp[0])
        piece = a.shape[1:] if mode == "scatter" else a.shape
        if rows is not None:
            piece = (rows[1],) + tuple(piece[1:])
        outs.append(jax.ShapeDtypeStruct((N_DEV, len(grp)) + tuple(piece), a.dtype))
    return flat, outs


def _rows_of(entry):
    return entry if isinstance(entry, tuple) else (entry, None)


def _operands(flat):
    return [_rows_of(a)[0] for _, _, a in flat]


def _xchg_sems(n):
    return [pltpu.SemaphoreType.DMA((n, N_DEV - 1)), pltpu.SemaphoreType.DMA((n, N_DEV - 1)),
            pltpu.SemaphoreType.DMA((n,))]


def _xchg_copies(flat, mode, ins, out_refs, sems, waiting=True):
    send_sems, recv_sems, local_sems = sems
    x, y, c = lax.axis_index("x"), lax.axis_index("y"), lax.axis_index("c")
    me = 4 * x + 2 * y + c
    local, first, forwards = [], [], []

    def pair(s, j, dev):
        return dict(send_sem=send_sems.at[s, j], recv_sem=recv_sems.at[s, j], device_id=dev, device_id_type=MESH)

    def block(s, dev):
        rows = _rows_of(flat[s][2])[1]
        ref = ins[s].at[dev]
        return ref if rows is None else ref.at[pl.ds(rows[0], rows[1])]

    for s, (g, l, _) in enumerate(flat):
        src = block(s, me) if mode == "scatter" else ins[s]
        local.append(pltpu.make_async_copy(src, out_refs[g].at[me, l], local_sems.at[s]))
    if mode == "gather2":
        sib, sib_idx = (x, y, 1 - c), 4 * x + 2 * y + (1 - c)
        for s, (g, l, _) in enumerate(flat):
            slot = lambda dev, g=g, l=l: out_refs[g].at[dev, l]
            targets = [(sib, sib_idx)] + [((qx, qy, c), 4 * qx + 2 * qy + c)
                                          for qx, qy in ((1 - x, y), (x, 1 - y), (1 - x, 1 - y))]
            for j, (dev, idx) in enumerate(targets):
                send = pltpu.make_async_remote_copy(src_ref=ins[s], dst_ref=slot(me), **pair(s, j, dev))
                arrive = pltpu.make_async_remote_copy(src_ref=ins[s], dst_ref=slot(idx), **pair(s, j, dev)) if waiting else None
                first.append((send, arrive))
            if waiting:
                for j, (dev, idx) in enumerate(targets[1:]):
                    other = 4 * dev[0] + 2 * dev[1] + (1 - c)
                    send = pltpu.make_async_remote_copy(src_ref=slot(idx), dst_ref=slot(idx), **pair(s, 4 + j, sib))
                    arrive = pltpu.make_async_remote_copy(src_ref=slot(idx), dst_ref=slot(other), **pair(s, 4 + j, sib))
                    forwards.append((first[len(first) - 3 + j][1], send, arrive))
        return local, first, forwards
    for k in range(1, N_DEV):
        px = 1 - x if k & 4 else x
        py = 1 - y if k & 2 else y
        pc = 1 - c if k & 1 else c
        peer = 4 * px + 2 * py + pc
        for s, (g, l, _) in enumerate(flat):
            src = block(s, peer) if mode == "scatter" else ins[s]
            send = pltpu.make_async_remote_copy(src_ref=src, dst_ref=out_refs[g].at[me, l], **pair(s, k - 1, (px, py, pc)))
            arrive = (pltpu.make_async_remote_copy(src_ref=src, dst_ref=out_refs[g].at[peer, l],
                                                   **pair(s, k - 1, (px, py, pc))) if waiting else None)
            first.append((send, arrive))
    return local, first, forwards


def _xchg_start(local, first, forwards):
    for cp in local:
        cp.start()
    for send, _ in first:
        send.start()


def _xchg_wait(local, first, forwards):
    gates = [gate for gate, _, _ in forwards]
    for gate, send, _ in forwards:
        gate.wait_recv()
        send.start()
    for _, arrive in first:
        if not any(arrive is gate for gate in gates):
            arrive.wait_recv()
    for _, _, arrive in forwards:
        arrive.wait_recv()
    for send, _ in first:
        send.wait_send()
    for _, send, _ in forwards:
        send.wait_send()
    for cp in local:
        cp.wait()


def _exchange(groups, mode, name):
    flat, outs = _xchg_plan(groups, mode)
    n = len(flat)

    def body(*refs):
        copies = _xchg_copies(flat, mode, refs[:n], refs[n:n + len(groups)], refs[n + len(groups):])
        _xchg_start(*copies)
        _xchg_wait(*copies)

    res = pl.pallas_call(body, name=name, out_shape=outs, in_specs=[ANY] * n, out_specs=[ANY] * len(groups),
                         scratch_shapes=_xchg_sems(n))(*_operands(flat))
    return list(res)


def _pcall(body, *, name, grid, in_specs, out_specs, out_shape, ins, scratch_shapes=(), sem=None, vmem=VMEM_LIMIT,
           rider=None):
    single = not isinstance(out_shape, (list, tuple))
    if rider is None:
        return pl.pallas_call(body, name=name, grid=grid, in_specs=list(in_specs), out_specs=out_specs,
                              out_shape=out_shape, scratch_shapes=list(scratch_shapes),
                              compiler_params=_params(sem, vmem))(*ins)
    groups, mode = rider
    flat, r_outs = _xchg_plan(groups, mode)
    n_in, n_out = len(ins), 1 if single else len(out_shape)
    nr, ng, ns = len(flat), len(groups), len(scratch_shapes)

    def wrapped(*refs):
        o1 = n_in + nr
        o2 = o1 + n_out
        o3 = o2 + ng
        r_in, r_out, sems = refs[n_in:o1], refs[o2:o3], refs[o3 + ns:]
        first = functools.reduce(jnp.logical_and, [pl.program_id(d) == 0 for d in range(len(grid))])
        last = functools.reduce(jnp.logical_and, [pl.program_id(d) == grid[d] - 1 for d in range(len(grid))])

        @pl.when(first)
        def _():
            _xchg_start(*_xchg_copies(flat, mode, r_in, r_out, sems, waiting=False))

        body(*refs[:n_in], *refs[o1:o2], *refs[o3:o3 + ns])

        @pl.when(last)
        def _():
            _xchg_wait(*_xchg_copies(flat, mode, r_in, r_out, sems))

    outs = pl.pallas_call(
        wrapped, name=name, grid=grid, in_specs=list(in_specs) + [ANY] * nr,
        out_specs=([out_specs] if single else list(out_specs)) + [ANY] * ng,
        out_shape=([out_shape] if single else list(out_shape)) + r_outs,
        scratch_shapes=list(scratch_shapes) + _xchg_sems(nr),
        compiler_params=_params(("arbitrary",) * len(grid), vmem))(*ins, *_operands(flat))
    base = list(outs[:n_out])
    return (base[0] if single else base), list(outs[n_out:])


_DIMS = {"nn": (((1,), (0,)), ((), ())), "nt": (((1,), (1,)), ((), ())), "tn": (((0,), (0,)), ((), ()))}


def _mm(name, a, b, a_spec, b_spec, o_spec, out_shape, grid, dims, rider=None):
    nk = grid[2]
    acc_shape = tuple(d for d in o_spec.block_shape if d is not None)
    dn = _DIMS[dims]

    def tile(ref):
        v = ref[...]
        return v.reshape((-1, v.shape[-1])).astype(_ACT)

    def body(a_ref, b_ref, o_ref, *scratch):
        def part():
            return lax.dot_general(tile(a_ref), tile(b_ref), dn, preferred_element_type=F32)

        if nk == 1:
            o_ref[...] = part().reshape(o_ref.shape).astype(o_ref.dtype)
            return
        acc_ref, = scratch
        k = pl.program_id(2)

        @pl.when(k == 0)
        def _():
            acc_ref[...] = part()

        @pl.when(k > 0)
        def _():
            acc_ref[...] += part()

        @pl.when(k == nk - 1)
        def _():
            o_ref[...] = acc_ref[...].reshape(o_ref.shape).astype(o_ref.dtype)

    acc2d = (math.prod(acc_shape[:-1]), acc_shape[-1])
    return _pcall(body, name=name, out_shape=out_shape, grid=grid, in_specs=[a_spec, b_spec], out_specs=o_spec,
                  scratch_shapes=[] if nk == 1 else [pltpu.VMEM(acc2d, F32)], ins=(a, b),
                  sem=("parallel", "parallel", "arbitrary"), rider=rider)


def _row_tile(n, want):
    t = min(n, want)
    assert n % t == 0, (n, t)
    return t


def _colblock_fwd(name, xa, wg, layer, out_dtype, rider=None):
    L, K = xa.shape
    nb = wg.shape[3]
    half = N_DEV // 2
    tm = _row_tile(L, 512)
    return _mm(name, xa, wg,
               pl.BlockSpec((tm, K), lambda j, i, k: (i, 0)),
               pl.BlockSpec((None, None, K, nb), lambda j, i, k: (j, layer, 0, 0)),
               pl.BlockSpec((None, tm, nb), lambda j, i, k: (j // half, i, j % half)),
               jax.ShapeDtypeStruct((2, L, half * nb), out_dtype), (N_DEV, L // tm, 1), "nn", rider=rider)


def _colblock_dgrad(name, dh, wg, layer, out_dtype, rider=None):
    _, L, _ = dh.shape
    K, nb = wg.shape[2], wg.shape[3]
    half = N_DEV // 2
    tm = _row_tile(L, 512)
    return _mm(name, dh, wg,
               pl.BlockSpec((None, tm, nb), lambda i, j, k: (k // half, i, k % half)),
               pl.BlockSpec((None, None, K, nb), lambda i, j, k: (k, layer, 0, 0)),
               pl.BlockSpec((tm, K), lambda i, j, k: (i, 0)),
               jax.ShapeDtypeStruct((L, K), out_dtype), (L // tm, 1, N_DEV), "nt", rider=rider)


def _colblock_wgrad(name, xa, dh, rider=None):
    L, K = xa.shape
    half = N_DEV // 2
    nb = dh.shape[2] // half
    tm = _row_tile(K, 512)
    tk = L
    return _mm(name, xa, dh,
               pl.BlockSpec((tk, tm), lambda j, i, k: (k, i)),
               pl.BlockSpec((None, tk, nb), lambda j, i, k: (j // half, k, j % half)),
               pl.BlockSpec((None, tm, nb), lambda j, i, k: (j, i, 0)),
               jax.ShapeDtypeStruct((N_DEV, K, nb), _ACT), (N_DEV, K // tm, L // tk), "tn", rider=rider)


def _rowblock_fwd(name, xa, wg, layer, rider=None):
    L, FF = xa.shape
    r, D = wg.shape[2], wg.shape[3]
    tm = _row_tile(L, 512)
    return _mm(name, xa, wg,
               pl.BlockSpec((tm, 2 * r), lambda i, j, k: (i, k)),
               pl.BlockSpec((2, None, r, D), lambda i, j, k: (k, layer, 0, 0)),
               pl.BlockSpec((tm, D), lambda i, j, k: (i, 0)),
               jax.ShapeDtypeStruct((L, D), F32), (L // tm, 1, N_DEV // 2), "nn", rider=rider)


def _rowblock_dgrad(name, dy, wg, layer, rider=None):
    L, D = dy.shape
    r = wg.shape[2]
    tm = _row_tile(L, 512)
    return _mm(name, dy, wg,
               pl.BlockSpec((tm, D), lambda i, j, k: (i, 0)),
               pl.BlockSpec((2, None, r, D), lambda i, j, k: (j, layer, 0, 0)),
               pl.BlockSpec((tm, 2 * r), lambda i, j, k: (i, j)),
               jax.ShapeDtypeStruct((L, N_DEV * r), _ACT), (L // tm, N_DEV // 2, 1), "nt", rider=rider)


def _rowblock_wgrad(name, xa, dy, rider=None):
    L, FF = xa.shape
    D = dy.shape[1]
    tm = FF // (N_DEV // 2)
    tn = _row_tile(D, 1024)
    tk = _row_tile(L, 2048)
    return _mm(name, xa, dy,
               pl.BlockSpec((tk, tm), lambda i, j, k: (k, i)),
               pl.BlockSpec((tk, tn), lambda i, j, k: (k, j)),
               pl.BlockSpec((tm, tn), lambda i, j, k: (i, j)),
               jax.ShapeDtypeStruct((FF, D), _ACT), (FF // tm, D // tn, L // tk), "tn", rider=rider)


def _group_mm(name, xa, w, dims, out_dtype):
    L, D = xa.shape
    ng, pc, _ = w.shape
    tm = _row_tile(L, 512)
    return _mm(name, xa, w,
               pl.BlockSpec((tm, pc), lambda i, g, k: (i, g)),
               pl.BlockSpec((None, pc, pc), lambda i, g, k: (g, 0, 0)),
               pl.BlockSpec((tm, pc), lambda i, g, k: (i, g)),
               jax.ShapeDtypeStruct((L, D), out_dtype), (L // tm, ng, 1), dims)


def _group_wgrad(name, p, dy, ng):
    L, D = p.shape
    pc = D // ng
    tk = _row_tile(L, 512)
    return _mm(name, p, dy,
               pl.BlockSpec((tk, pc), lambda g, j, k: (k, g)),
               pl.BlockSpec((tk, pc), lambda g, j, k: (k, g)),
               pl.BlockSpec((None, pc, pc), lambda g, j, k: (g, 0, 0)),
               jax.ShapeDtypeStruct((ng, pc, pc), _ACT), (ng, 1, L // tk), "tn")


V_GPOST, V_GATE, V_YSCALE, V_GPRE, V_SHIFT, V_SCALE = range(6)
R_SHIFT, R_SCALE, R_GPRE, R_GATE, R_GPOST = range(5)
ROW_TILE = 256


def _rstd(v):
    return lax.rsqrt(jnp.mean(v * v, axis=-1, keepdims=True) + RMS_EPS)


def _rows_fwd(name, xres, y, vec, *, add=False, target=None, want_x=True, u_dtype=None):
    L, D = xres.shape
    tm = _row_tile(L, ROW_TILE)
    has_y = y is not None
    last = target is not None
    has_u = u_dtype is not None

    def body(*refs):
        refs = list(refs)
        xres_ref = refs.pop(0)
        y_ref = refs.pop(0) if has_y else None
        vec_ref = refs.pop(0)
        tgt_ref = refs.pop(0) if last else None
        xnew = xres_ref[...]
        if has_y and add:
            xnew = xnew + y_ref[...]
        elif has_y:
            ye = y_ref[...] * vec_ref[V_YSCALE:V_YSCALE + 1, :]
            xnew = xnew + vec_ref[V_GATE:V_GATE + 1, :] * (ye * _rstd(ye) * vec_ref[V_GPOST:V_GPOST + 1, :])
        if last:
            dx_ref, loss_ref = refs
            diff = xnew - tgt_ref[...]
            dx_ref[...] = diff * (1.0 / D)

            @pl.when(pl.program_id(0) == 0)
            def _():
                loss_ref[...] = jnp.zeros_like(loss_ref)

            loss_ref[...] += jnp.sum(diff * diff) * (0.5 / D)
            return
        if want_x:
            refs.pop(0)[...] = xnew
        if has_u:
            u_ref, = refs
            n = xnew * _rstd(xnew) * vec_ref[V_GPRE:V_GPRE + 1, :]
            u_ref[...] = (n * (1.0 + vec_ref[V_SCALE:V_SCALE + 1, :]) + vec_ref[V_SHIFT:V_SHIFT + 1, :]).astype(u_ref.dtype)

    row = pl.BlockSpec((tm, D), lambda i: (i, 0))
    vspec = pl.BlockSpec((8, D), lambda i: (0, 0))
    ins, in_specs = [xres], [row]
    if has_y:
        ins.append(y)
        in_specs.append(row)
    ins.append(vec)
    in_specs.append(vspec)
    out_shape, out_specs = [], []
    if last:
        ins.append(target)
        in_specs.append(row)
        out_shape = [jax.ShapeDtypeStruct((L, D), F32), jax.ShapeDtypeStruct((8, LANE), F32)]
        out_specs = [row, pl.BlockSpec((8, LANE), lambda i: (0, 0))]
    else:
        if want_x:
            out_shape.append(jax.ShapeDtypeStruct((L, D), F32))
            out_specs.append(row)
        if has_u:
            out_shape.append(jax.ShapeDtypeStruct((L, D), u_dtype))
            out_specs.append(row)
    return pl.pallas_call(body, name=name, out_shape=out_shape, grid=(L // tm,), in_specs=in_specs,
                          out_specs=out_specs, compiler_params=_params(("arbitrary",)))(*ins)


def _rows_bwd(name, dxd, du, xnew, y, vec, dy_dtype=F32, want_dx=True):
    L, D = xnew.shape if xnew is not None else dxd.shape
    tm = _row_tile(L, ROW_TILE)
    has_dxd, has_pre, has_post = dxd is not None, du is not None, y is not None

    def body(*refs):
        refs = list(refs)
        dxd_ref = refs.pop(0) if has_dxd else None
        du_ref = refs.pop(0) if has_pre else None
        xnew_ref = refs.pop(0) if has_pre else None
        y_ref = refs.pop(0) if has_post else None
        vec_ref = refs.pop(0)
        dx_ref = refs.pop(0) if want_dx else None
        dy_ref = refs.pop(0) if has_post else None
        red_ref, = refs

        @pl.when(pl.program_id(0) == 0)
        def _():
            red_ref[...] = jnp.zeros_like(red_ref)

        def acc(rw, val):
            red_ref[rw:rw + 1, :] += jnp.sum(val, axis=0, keepdims=True)

        dxn = dxd_ref[...] if has_dxd else None
        if has_pre:
            xn = xnew_ref[...]
            r = _rstd(xn)
            nh = xn * r
            gpre = vec_ref[V_GPRE:V_GPRE + 1, :]
            dub = du_ref[...].astype(F32)
            acc(R_SHIFT, dub)
            acc(R_SCALE, dub * (nh * gpre))
            drn = dub * (1.0 + vec_ref[V_SCALE:V_SCALE + 1, :])
            acc(R_GPRE, drn * nh)
            dnh = drn * gpre
            t = r * (dnh - nh * jnp.mean(dnh * nh, axis=-1, keepdims=True))
            dxn = t if dxn is None else dxn + t
        if want_dx:
            dx_ref[...] = dxn
        if has_post:
            ye = y_ref[...] * vec_ref[V_YSCALE:V_YSCALE + 1, :]
            ry = _rstd(ye)
            yh = ye * ry
            gpost = vec_ref[V_GPOST:V_GPOST + 1, :]
            acc(R_GATE, dxn * (yh * gpost))
            drn2 = dxn * vec_ref[V_GATE:V_GATE + 1, :]
            acc(R_GPOST, drn2 * yh)
            dyh = drn2 * gpost
            dy_ref[...] = (ry * (dyh - yh * jnp.mean(dyh * yh, axis=-1, keepdims=True))).astype(dy_ref.dtype)

    row = pl.BlockSpec((tm, D), lambda i: (i, 0))
    vspec = pl.BlockSpec((8, D), lambda i: (0, 0))
    ins, in_specs = [], []
    for a in ([dxd] if has_dxd else []) + ([du, xnew] if has_pre else []) + ([y] if has_post else []):
        ins.append(a)
        in_specs.append(row)
    ins.append(vec)
    in_specs.append(vspec)
    out_shape, out_specs = [], []
    if want_dx:
        out_shape.append(jax.ShapeDtypeStruct((L, D), F32))
        out_specs.append(row)
    if has_post:
        out_shape.append(jax.ShapeDtypeStruct((L, D), dy_dtype))
        out_specs.append(row)
    out_shape.append(jax.ShapeDtypeStruct((8, D), F32))
    out_specs.append(vspec)
    return pl.pallas_call(body, name=name, out_shape=out_shape, grid=(L // tm,), in_specs=in_specs,
                          out_specs=out_specs, compiler_params=_params(("arbitrary",)))(*ins)


def _colscale_bwd(name, dy, ypre, scale):
    L, D = dy.shape
    tm = _row_tile(L, ROW_TILE)

    def body(dy_ref, yp_ref, s_ref, o_ref, red_ref):
        @pl.when(pl.program_id(0) == 0)
        def _():
            red_ref[...] = jnp.zeros_like(red_ref)

        d = dy_ref[...]
        o_ref[...] = (d * s_ref[0:1, :]).astype(o_ref.dtype)
        red_ref[0:1, :] += jnp.sum(d * yp_ref[...], axis=0, keepdims=True)

    row = pl.BlockSpec((tm, D), lambda i: (i, 0))
    vspec = pl.BlockSpec((8, D), lambda i: (0, 0))
    return pl.pallas_call(body, name=name, grid=(L // tm,), in_specs=[row, row, vspec], out_specs=[row, vspec],
                          out_shape=[jax.ShapeDtypeStruct((L, D), _ACT), jax.ShapeDtypeStruct((8, D), F32)],
                          compiler_params=_params(("arbitrary",)))(dy, ypre, scale)


def _sigmoid(v):
    return 1.0 / (1.0 + jnp.exp(-v))


def _glu_fwd(name, vg):
    _, L, D = vg.shape
    tm = _row_tile(L, ROW_TILE)

    def body(vg_ref, o_ref):
        o_ref[...] = vg_ref[0].astype(F32) * _sigmoid(vg_ref[1].astype(F32))

    return pl.pallas_call(body, name=name, grid=(L // tm,),
                          in_specs=[pl.BlockSpec((2, tm, D), lambda i: (0, i, 0))],
                          out_specs=pl.BlockSpec((tm, D), lambda i: (i, 0)),
                          out_shape=jax.ShapeDtypeStruct((L, D), F32),
                          compiler_params=_params(("parallel",)))(vg)


def _glu_bwd(name, vg, dout):
    _, L, D = vg.shape
    tm = _row_tile(L, ROW_TILE)

    def body(vg_ref, d_ref, o_ref):
        val, s = vg_ref[0].astype(F32), _sigmoid(vg_ref[1].astype(F32))
        d = d_ref[...]
        o_ref[0] = (d * s).astype(o_ref.dtype)
        o_ref[1] = (d * val * s * (1.0 - s)).astype(o_ref.dtype)

    return pl.pallas_call(body, name=name, grid=(L // tm,),
                          in_specs=[pl.BlockSpec((2, tm, D), lambda i: (0, i, 0)), pl.BlockSpec((tm, D), lambda i: (i, 0))],
                          out_specs=pl.BlockSpec((2, tm, D), lambda i: (0, i, 0)),
                          out_shape=jax.ShapeDtypeStruct((2, L, D), _ACT),
                          compiler_params=_params(("parallel",)))(vg, dout)


CONV_ROWS = 256


def _row_pick(blk, idx):
    rows = lax.broadcasted_iota(jnp.int32, blk.shape, 0)
    return jnp.sum(jnp.where(rows == idx, blk, 0.0), axis=0, keepdims=True)


def _shifted(ref, r0, rc, L):
    cur = ref[pl.ds(r0, rc), :].astype(F32)
    before = ref[pl.ds(pl.multiple_of(jnp.maximum(r0 - 16, 0), 16), 16), :].astype(F32)
    after = ref[pl.ds(pl.multiple_of(jnp.minimum(r0 + rc, L - 16), 16), 16), :].astype(F32)
    prev_row = jnp.where(r0 > 0, _row_pick(before, 15), 0.0)
    next_row = jnp.where(r0 + rc < L, _row_pick(after, 0), 0.0)
    rows = lax.broadcasted_iota(jnp.int32, cur.shape, 0)
    up = jnp.where(rows == 0, prev_row, pltpu.roll(cur, 1, 0))
    down = jnp.where(rows == rc - 1, next_row, pltpu.roll(cur, rc - 1, 0))
    return up, cur, down


def _silu_parts(g):
    s = _sigmoid(g)
    return g * s, s


def _conv_swiglu_fwd(name, h, cw):
    _, L, FF = h.shape
    rc = _row_tile(L, CONV_ROWS)

    def body(h_ref, cw_ref, o_ref):
        def chunk(ci, _):
            r0 = pl.multiple_of(ci * rc, rc)
            hc = []
            for half in range(2):
                up, cur, down = _shifted(h_ref.at[half], r0, rc, L)
                hc.append(up * cw_ref[half, 0:1, :] + cur * cw_ref[half, 1:2, :] + down * cw_ref[half, 2:3, :]
                          + cw_ref[half, 3:4, :])
            o_ref[pl.ds(r0, rc), :] = (_silu_parts(hc[1])[0] * hc[0]).astype(o_ref.dtype)
            return 0

        lax.fori_loop(0, L // rc, chunk, 0)

    return pl.pallas_call(body, name=name, grid=(FF // LANE,),
                          in_specs=[pl.BlockSpec((2, L, LANE), lambda j: (0, 0, j)),
                                    pl.BlockSpec((2, 8, LANE), lambda j: (0, 0, j))],
                          out_specs=pl.BlockSpec((L, LANE), lambda j: (0, j)),
                          out_shape=jax.ShapeDtypeStruct((L, FF), _ACT),
                          compiler_params=_params(("parallel",)))(h, cw)


def _conv_swiglu_bwd(name, h, cw, dact, rider=None):
    _, L, FF = h.shape
    rc = _row_tile(L, CONV_ROWS)

    def body(h_ref, cw_ref, da_ref, dh_ref, dcw_ref, dhc_ref):
        def chunk(ci, acc):
            r0 = pl.multiple_of(ci * rc, rc)
            taps, hc = [], []
            for half in range(2):
                t = _shifted(h_ref.at[half], r0, rc, L)
                taps.append(t)
                hc.append(t[0] * cw_ref[half, 0:1, :] + t[1] * cw_ref[half, 1:2, :] + t[2] * cw_ref[half, 2:3, :]
                          + cw_ref[half, 3:4, :])
            d = da_ref[pl.ds(r0, rc), :].astype(F32)
            act, s = _silu_parts(hc[1])
            dhc = (d * act, d * hc[0] * (s + act * (1.0 - s)))
            new = []
            for half in range(2):
                dhc_ref[half, pl.ds(r0, rc), :] = dhc[half]
                for k in range(3):
                    new.append(acc[4 * half + k] + jnp.sum(dhc[half] * taps[half][k], axis=0, keepdims=True))
                new.append(acc[4 * half + 3] + jnp.sum(dhc[half], axis=0, keepdims=True))
            return tuple(new)

        zero = jnp.zeros((1, LANE), F32)
        acc = lax.fori_loop(0, L // rc, chunk, (zero,) * 8)
        dcw_ref[...] = jnp.zeros_like(dcw_ref)
        for half in range(2):
            for k in range(4):
                dcw_ref[half, k:k + 1, :] = acc[4 * half + k]

        def chunk2(ci, _):
            r0 = pl.multiple_of(ci * rc, rc)
            for half in range(2):
                up, cur, down = _shifted(dhc_ref.at[half], r0, rc, L)
                dh_ref[half, pl.ds(r0, rc), :] = (down * cw_ref[half, 0:1, :] + cur * cw_ref[half, 1:2, :]
                                                  + up * cw_ref[half, 2:3, :]).astype(dh_ref.dtype)
            return 0

        lax.fori_loop(0, L // rc, chunk2, 0)

    return _pcall(body, name=name, grid=(FF // LANE,),
                  in_specs=[pl.BlockSpec((2, L, LANE), lambda j: (0, 0, j)),
                            pl.BlockSpec((2, 8, LANE), lambda j: (0, 0, j)),
                            pl.BlockSpec((L, LANE), lambda j: (0, j))],
                  out_specs=[pl.BlockSpec((2, L, LANE), lambda j: (0, 0, j)),
                             pl.BlockSpec((2, 8, LANE), lambda j: (0, 0, j))],
                  out_shape=[jax.ShapeDtypeStruct((2, L, FF), _ACT), jax.ShapeDtypeStruct((2, 8, FF), F32)],
                  scratch_shapes=[pltpu.VMEM((2, L, LANE), F32)], ins=(h, cw, dact), sem=("parallel",), rider=rider)


POOL_ROWS = 256
POOL_TILE = 256


def _pool_bands(transpose):
    i = jnp.arange(POOL_ROWS)[:, None]
    j = jnp.arange(POOL_ROWS + 2 * POOL_HALO)[None, :] - POOL_HALO
    bands = []
    for w in POOL_WINDOWS:
        lo, hi = (-(w // 2 - 1), w // 2) if transpose else (-(w // 2), w // 2 - 1)
        bands.append(((j - i >= lo) & (j - i <= hi)).astype(_ACT))
    return jnp.stack(bands)


def _pool_window(name, u, transpose, out_dtype):
    L, D = u.shape
    ng = len(POOL_WINDOWS)
    pc = D // ng
    tn = min(POOL_TILE, pc)
    rc = _row_tile(L, POOL_ROWS)
    bands = _pool_bands(transpose)
    if rc != POOL_ROWS:
        bands = bands[:, :rc, :rc + 2 * POOL_HALO]
    halo = POOL_HALO

    def body(u_ref, band_ref, o_ref, hi_ref, lo_ref):
        g = (pl.program_id(0) * tn) // pc
        half = jnp.zeros((1, 1), jnp.int32)
        for k, w in enumerate(POOL_WINDOWS):
            half = jnp.where(g == k, w // 2, half)
        zeros = jnp.zeros((halo, tn), _ACT)
        for ref in (hi_ref, lo_ref):
            ref[0:halo, :] = zeros
            ref[halo + L:2 * halo + L, :] = zeros

        def inv_count(r0):
            t = r0 + lax.broadcasted_iota(jnp.int32, (rc, tn), 0)
            lo = jnp.clip(t - half, 0, L - 1)
            hi = jnp.clip(t + half - 1, 0, L - 1)
            return 1.0 / (hi - lo + 1).astype(F32)

        def split(ci, _):
            r0 = pl.multiple_of(ci * rc, rc)
            v = u_ref[pl.ds(r0, rc), :].astype(F32)
            if transpose:
                v = v * inv_count(r0)
            hi = v.astype(_ACT)
            dst = pl.ds(pl.multiple_of(r0 + halo, halo), rc)
            hi_ref[dst, :] = hi
            lo_ref[dst, :] = (v - hi.astype(F32)).astype(_ACT)
            return 0

        lax.fori_loop(0, L // rc, split, 0)
        band = band_ref[...]

        def chunk(ci, _):
            r0 = pl.multiple_of(ci * rc, rc)
            win = pl.ds(r0, rc + 2 * halo)
            s = (jnp.dot(band, hi_ref[win, :], preferred_element_type=F32)
                 + jnp.dot(band, lo_ref[win, :], preferred_element_type=F32))
            if not transpose:
                s = s * inv_count(r0)
            o_ref[pl.ds(r0, rc), :] = (s - u_ref[pl.ds(r0, rc), :].astype(F32)).astype(o_ref.dtype)
            return 0

        lax.fori_loop(0, L // rc, chunk, 0)

    return pl.pallas_call(body, name=name, grid=(D // tn,),
                          in_specs=[pl.BlockSpec((L, tn), lambda j: (0, j)),
                                    pl.BlockSpec((None, rc, rc + 2 * halo), lambda j: ((j * tn) // pc, 0, 0))],
                          out_specs=pl.BlockSpec((L, tn), lambda j: (0, j)),
                          out_shape=jax.ShapeDtypeStruct((L, D), out_dtype),
                          scratch_shapes=[pltpu.VMEM((L + 2 * halo, tn), _ACT), pltpu.VMEM((L + 2 * halo, tn), _ACT)],
                          compiler_params=_params(("parallel",)))(u, bands)


S5_ROWS = 512


def _slab(start):
    return pl.ds(start if isinstance(start, int) else pl.multiple_of(start, NSEG), NSEG)


def _cmul(ar, ai, br, bi):
    return ar * br - ai * bi, ar * bi + ai * br


def _cpow(ar, ai, n):
    rr, ri = None, None
    br, bi = ar, ai
    while n:
        if n & 1:
            rr, ri = (br, bi) if rr is None else _cmul(rr, ri, br, bi)
        n >>= 1
        if n:
            br, bi = _cmul(br, bi, br, bi)
    return rr, ri


def _pow_table(pw_ref, ar, ai, n):
    W = ar.shape[1]
    pr, pi = ar, ai
    for r in range(NSEG):
        pw_ref[0, r:r + 1, :] = pr
        pw_ref[1, r:r + 1, :] = pi
        if r < NSEG - 1:
            pr, pi = _cmul(pr, pi, ar, ai)
    a8r, a8i = (jnp.broadcast_to(v, (NSEG, W)) for v in _cpow(ar, ai, NSEG))

    def step(k, carry):
        nr, ni = _cmul(carry[0], carry[1], a8r, a8i)
        pw_ref[0, _slab(k * NSEG), :] = nr
        pw_ref[1, _slab(k * NSEG), :] = ni
        return nr, ni

    lax.fori_loop(1, n // NSEG, step, (pw_ref[0, 0:NSEG, :], pw_ref[1, 0:NSEG, :]))


def _seg_scan(sr_ref, si_ref, tmp_ref, pw_ref, row0, n, ar, ai, h0, rev, conj=False):
    W = ar.shape[1]
    arb, aib = jnp.broadcast_to(ar, (NSEG, W)), jnp.broadcast_to(ai, (NSEG, W))

    def rows(s):
        t = (n - 1 - s) if rev else s
        return _slab(row0 + t * NSEG)

    def step(s, carry):
        hr, hi = carry
        sl = rows(s)
        nr = arb * hr - aib * hi + sr_ref[sl, :]
        ni = arb * hi + aib * hr + si_ref[sl, :]
        sr_ref[sl, :] = nr
        si_ref[sl, :] = ni
        return nr, ni

    zero = jnp.zeros((NSEG, W), F32)
    fr, fi = lax.fori_loop(0, n, step, (zero, zero), unroll=2)
    tmp_ref[0] = fr
    tmp_ref[1] = fi
    anr, ani = _cpow(ar, ai, n)
    cr, ci = h0
    for j in (range(NSEG - 1, -1, -1) if rev else range(NSEG)):
        tmp_ref[2, j:j + 1, :] = cr
        tmp_ref[3, j:j + 1, :] = ci
        pr, pi = _cmul(anr, ani, cr, ci)
        cr, ci = tmp_ref[0, j:j + 1, :] + pr, tmp_ref[1, j:j + 1, :] + pi
    cmr, cmi = tmp_ref[2], tmp_ref[3]

    def fix(k, _):
        for r in range(NSEG):
            row = pl.ds(pl.multiple_of(k * NSEG, NSEG) + r, 1)
            pr = jnp.broadcast_to(pw_ref[0, row, :], (NSEG, W))
            pi = jnp.broadcast_to(pw_ref[1, row, :], (NSEG, W))
            if conj:
                pi = -pi
            sl = rows(k * NSEG + r)
            sr_ref[sl, :] += pr * cmr - pi * cmi
            si_ref[sl, :] += pr * cmi + pi * cmr
        return 0

    lax.fori_loop(0, n // NSEG, fix, 0)
    return (cr, ci), (cmr, cmi)


def _gelu_tanh(y):
    k = math.sqrt(2.0 / math.pi)
    t = jnp.tanh(k * (y + 0.044715 * y * y * y))
    return 0.5 * y * (1.0 + t), t


def _s5_chunks(L):
    rc = _row_tile(L, S5_ROWS)
    return [(r, rc) for r in range(0, L, rc)]


def _s5_project(u_ref, uc_ref, bre, bim, sr_ref, si_ref, L, LC):
    for ref, base, n in ((u_ref, 0, L), (uc_ref, L, LC)):
        for r, rc in _s5_chunks(n):
            ub = ref[r:r + rc, :].astype(_ACT)
            sr_ref[base + r:base + r + rc, :] = jnp.dot(ub, bre, preferred_element_type=F32)
            si_ref[base + r:base + r + rc, :] = jnp.dot(ub, bim, preferred_element_type=F32)


def _s5_states(sr_ref, si_ref, tmp_ref, pw_ref, ar, ai, L, LC, rev):
    W = ar.shape[1]
    zero = (jnp.zeros((1, W), F32), jnp.zeros((1, W), F32))
    hctx, cm_ctx = _seg_scan(sr_ref, si_ref, tmp_ref, pw_ref, L, LC // NSEG, ar, ai, zero, rev)
    _, cm_lat = _seg_scan(sr_ref, si_ref, tmp_ref, pw_ref, 0, L // NSEG, ar, ai, hctx, rev)
    return cm_lat, cm_ctx


def _s5_fwd(u, uc, bblk, cblk, apar, dsk, rider=None):
    L, D = u.shape
    LC = uc.shape[0]
    NT, W, TC = D // S5_TILE_CH, S5_TILE_W, S5_TILE_CH

    def body(u_ref, uc_ref, b_ref, c_ref, a_ref, d_ref, y_ref, z_ref, sr_ref, si_ref, tmp_ref, pw_ref):
        for r, rc in _s5_chunks(L):
            y_ref[r:r + rc, :] = u_ref[r:r + rc, :].astype(F32) * d_ref[0:1, :]
        for d in range(2):
            ar, ai = a_ref[2 * d:2 * d + 1, :], a_ref[2 * d + 1:2 * d + 2, :]
            _pow_table(pw_ref, ar, ai, L // NSEG)
            _s5_project(u_ref, uc_ref, b_ref[2 * d], b_ref[2 * d + 1], sr_ref, si_ref, L, LC)
            _s5_states(sr_ref, si_ref, tmp_ref, pw_ref, ar, ai, L, LC, rev=(d == 1))
            cre, cim = c_ref[2 * d], c_ref[2 * d + 1]
            for r, rc in _s5_chunks(L):
                y_ref[r:r + rc, :] += (jnp.dot(sr_ref[r:r + rc, :].astype(_ACT), cre, preferred_element_type=F32)
                                       - jnp.dot(si_ref[r:r + rc, :].astype(_ACT), cim, preferred_element_type=F32))
        for r, rc in _s5_chunks(L):
            z_ref[r:r + rc, :] = _gelu_tanh(y_ref[r:r + rc, :])[0].astype(z_ref.dtype)

    col = lambda n: pl.BlockSpec((n, TC), lambda j: (0, j))
    return _pcall(
        body, name="s5_fwd", grid=(NT,), ins=(u, uc, bblk, cblk, apar, dsk), sem=("parallel",), rider=rider,
        in_specs=[col(L), col(LC),
                  pl.BlockSpec((None, 4, TC, W), lambda j: (j, 0, 0, 0)),
                  pl.BlockSpec((None, 4, W, TC), lambda j: (j, 0, 0, 0)),
                  pl.BlockSpec((None, 8, W), lambda j: (j, 0, 0)),
                  pl.BlockSpec((8, TC), lambda j: (0, j))],
        out_specs=[col(L), col(L)],
        out_shape=[jax.ShapeDtypeStruct((L, D), F32), jax.ShapeDtypeStruct((L, D), _ACT)],
        scratch_shapes=[pltpu.VMEM((L + LC, W), F32), pltpu.VMEM((L + LC, W), F32), pltpu.VMEM((4, NSEG, W), F32),
                        pltpu.VMEM((2, L // NSEG, W), F32)])


def _s5_bwd(u, uc, dz, y, bblk, cblk, apar, dsk, rider=None):
    L, D = u.shape
    LC = uc.shape[0]
    NT, W, TC = D // S5_TILE_CH, S5_TILE_W, S5_TILE_CH
    nl, nc = L // NSEG, LC // NSEG

    def body(u_ref, uc_ref, dz_ref, y_ref, b_ref, c_ref, a_ref, d_ref,
             du_ref, duc_ref, db_ref, dc_ref, da_ref, dd_ref,
             hr_ref, hi_ref, gr_ref, gi_ref, dy_ref, tmp_ref, pw_ref):
        ddacc = jnp.zeros((1, TC), F32)
        for r, rc in _s5_chunks(L):
            yv = y_ref[r:r + rc, :]
            g, t = _gelu_tanh(yv)
            k = math.sqrt(2.0 / math.pi)
            dg = 0.5 * (1.0 + t) + 0.5 * yv * (1.0 - t * t) * k * (1.0 + 3 * 0.044715 * yv * yv)
            dy = dz_ref[r:r + rc, :].astype(F32) * dg
            uv = u_ref[r:r + rc, :].astype(F32)
            ddacc = ddacc + jnp.sum(dy * uv, axis=0, keepdims=True)
            du_ref[r:r + rc, :] = dy * d_ref[0:1, :]
            dy_ref[r:r + rc, :] = dy.astype(dy_ref.dtype)
        dd_ref[...] = jnp.zeros_like(dd_ref)
        dd_ref[0:1, :] = ddacc
        duc_ref[...] = jnp.zeros_like(duc_ref)
        da_ref[...] = jnp.zeros_like(da_ref)
        nt = (((1,), (1,)), ((), ()))
        tn = (((0,), (0,)), ((), ()))
        for d in range(2):
            rev = d == 1
            ar, ai = a_ref[2 * d:2 * d + 1, :], a_ref[2 * d + 1:2 * d + 2, :]
            bre, bim = b_ref[2 * d], b_ref[2 * d + 1]
            cre, cim = c_ref[2 * d], c_ref[2 * d + 1]
            _pow_table(pw_ref, ar, ai, nl)
            _s5_project(u_ref, uc_ref, bre, bim, hr_ref, hi_ref, L, LC)
            cm_lat, cm_ctx = _s5_states(hr_ref, hi_ref, tmp_ref, pw_ref, ar, ai, L, LC, rev)
            cml_r, cml_i, cmc_r, cmc_i = cm_lat[0], cm_lat[1], cm_ctx[0], cm_ctx[1]
            dcr = jnp.zeros((W, TC), F32)
            dci = jnp.zeros((W, TC), F32)
            for r, rc in _s5_chunks(L):
                dyb = dy_ref[r:r + rc, :]
                gr_ref[r:r + rc, :] = lax.dot_general(dyb, cre, nt, preferred_element_type=F32)
                gi_ref[r:r + rc, :] = -lax.dot_general(dyb, cim, nt, preferred_element_type=F32)
                dcr = dcr + lax.dot_general(hr_ref[r:r + rc, :].astype(_ACT), dyb, tn, preferred_element_type=F32)
                dci = dci - lax.dot_general(hi_ref[r:r + rc, :].astype(_ACT), dyb, tn, preferred_element_type=F32)
            dc_ref[2 * d] = dcr
            dc_ref[2 * d + 1] = dci
            gr_ref[L:L + LC, :] = jnp.zeros((LC, W), F32)
            gi_ref[L:L + LC, :] = jnp.zeros((LC, W), F32)
            zero = (jnp.zeros((1, W), F32), jnp.zeros((1, W), F32))
            glat, _ = _seg_scan(gr_ref, gi_ref, tmp_ref, pw_ref, 0, nl, ar, -ai, zero, not rev, conj=True)
            _seg_scan(gr_ref, gi_ref, tmp_ref, pw_ref, L, nc, ar, -ai, glat, not rev, conj=True)

            def da_part(row0, n, cmr, cmi):
                def rows(s):
                    t = (n - 1 - s) if rev else s
                    return _slab(row0 + t * NSEG)

                g0r, g0i = gr_ref[rows(0), :], gi_ref[rows(0), :]
                acc0 = (cmr * g0r + cmi * g0i, cmr * g0i - cmi * g0r)

                def step(s, acc):
                    hpr, hpi = hr_ref[rows(s - 1), :], hi_ref[rows(s - 1), :]
                    gr, gi = gr_ref[rows(s), :], gi_ref[rows(s), :]
                    return acc[0] + hpr * gr + hpi * gi, acc[1] + hpr * gi - hpi * gr

                return lax.fori_loop(1, n, step, acc0, unroll=2)

            lr, li = da_part(0, nl, cml_r, cml_i)
            qr, qi = da_part(L, nc, cmc_r, cmc_i)
            da_ref[2 * d:2 * d + 1, :] = jnp.sum(lr + qr, axis=0, keepdims=True)
            da_ref[2 * d + 1:2 * d + 2, :] = jnp.sum(li + qi, axis=0, keepdims=True)
            dbr = jnp.zeros((TC, W), F32)
            dbi = jnp.zeros((TC, W), F32)
            for ref, oref, base, n in ((u_ref, du_ref, 0, L), (uc_ref, duc_ref, L, LC)):
                for r, rc in _s5_chunks(n):
                    ub = ref[r:r + rc, :].astype(_ACT)
                    gr = gr_ref[base + r:base + r + rc, :].astype(_ACT)
                    gi = gi_ref[base + r:base + r + rc, :].astype(_ACT)
                    dbr = dbr + lax.dot_general(ub, gr, tn, preferred_element_type=F32)
                    dbi = dbi + lax.dot_general(ub, gi, tn, preferred_element_type=F32)
                    oref[r:r + rc, :] += (lax.dot_general(gr, bre, nt, preferred_element_type=F32)
                                          + lax.dot_general(gi, bim, nt, preferred_element_type=F32))
            db_ref[2 * d] = dbr
            db_ref[2 * d + 1] = dbi

    col = lambda n: pl.BlockSpec((n, TC), lambda j: (0, j))
    bspec = pl.BlockSpec((None, 4, TC, W), lambda j: (j, 0, 0, 0))
    cspec = pl.BlockSpec((None, 4, W, TC), lambda j: (j, 0, 0, 0))
    aspec = pl.BlockSpec((None, 8, W), lambda j: (j, 0, 0))
    return _pcall(
        body, name="s5_bwd", grid=(NT,), ins=(u, uc, dz, y, bblk, cblk, apar, dsk), sem=("parallel",),
        vmem=VMEM_LIMIT_BIG, rider=rider,
        in_specs=[col(L), col(LC), col(L), col(L), bspec, cspec, aspec, pl.BlockSpec((8, TC), lambda j: (0, j))],
        out_specs=[col(L), col(LC), bspec, cspec, aspec, pl.BlockSpec((None, 8, TC), lambda j: (j, 0, 0))],
        out_shape=[jax.ShapeDtypeStruct((L, D), F32), jax.ShapeDtypeStruct((LC, D), F32),
                   jax.ShapeDtypeStruct((NT, 4, TC, W), F32), jax.ShapeDtypeStruct((NT, 4, W, TC), F32),
                   jax.ShapeDtypeStruct((NT, 8, W), F32), jax.ShapeDtypeStruct((NT, 8, TC), F32)],
        scratch_shapes=[pltpu.VMEM((L + LC, W), F32), pltpu.VMEM((L + LC, W), F32),
                        pltpu.VMEM((L + LC, W), F32), pltpu.VMEM((L + LC, W), F32),
                        pltpu.VMEM((L, TC), _ACT), pltpu.VMEM((4, NSEG, W), F32), pltpu.VMEM((2, nl, W), F32)])


ADA_ROWS = 16


def _silu_rows(c_ref):
    c = c_ref[...]
    return c * _sigmoid(c)


def _ada_fwd(cmat, ada_w, ada_b):
    nl, D, n = ada_w.shape
    tn = _row_tile(n, 512)

    def body(c_ref, w_ref, b_ref, o_ref):
        a = _silu_rows(c_ref).astype(_ACT)
        o_ref[...] = jnp.dot(a, w_ref[...].astype(_ACT), preferred_element_type=F32) + b_ref[...]

    return pl.pallas_call(body, name="ada_fwd", grid=(nl, n // tn),
                          in_specs=[pl.BlockSpec((ADA_ROWS, D), lambda l, j: (0, 0)),
                                    pl.BlockSpec((None, D, tn), lambda l, j: (l, 0, j)),
                                    pl.BlockSpec((None, 1, tn), lambda l, j: (l, 0, j))],
                          out_specs=pl.BlockSpec((None, ADA_ROWS, tn), lambda l, j: (l, 0, j)),
                          out_shape=jax.ShapeDtypeStruct((nl, ADA_ROWS, n), F32),
                          compiler_params=_params(("parallel", "parallel")))(cmat, ada_w, ada_b)


def _ada_bwd(cmat, ada_w, dm):
    nl, D, n = ada_w.shape
    tn = _row_tile(n, 512)
    nj = n // tn

    def body(c_ref, w_ref, dm_ref, dw_ref, dc_ref):
        c = c_ref[...]
        s = _sigmoid(c)
        a = (c * s).astype(_ACT)
        dmb = dm_ref[...].astype(_ACT)
        dw_ref[...] = lax.dot_general(a, dmb, (((0,), (0,)), ((), ())), preferred_element_type=F32)
        part = lax.dot_general(dmb, w_ref[...].astype(_ACT), (((1,), (1,)), ((), ())), preferred_element_type=F32)
        part = part * (s * (1.0 + c * (1.0 - s)))

        @pl.when(pl.program_id(1) == 0)
        def _():
            dc_ref[...] = part

        @pl.when(pl.program_id(1) > 0)
        def _():
            dc_ref[...] += part

    return pl.pallas_call(body, name="ada_bwd", grid=(nl, nj),
                          in_specs=[pl.BlockSpec((ADA_ROWS, D), lambda l, j: (0, 0)),
                                    pl.BlockSpec((None, D, tn), lambda l, j: (l, 0, j)),
                                    pl.BlockSpec((None, ADA_ROWS, tn), lambda l, j: (l, 0, j))],
                          out_specs=[pl.BlockSpec((None, D, tn), lambda l, j: (l, 0, j)),
                                     pl.BlockSpec((None, ADA_ROWS, D), lambda l, j: (l, 0, 0))],
                          out_shape=[jax.ShapeDtypeStruct((nl, D, n), F32), jax.ShapeDtypeStruct((nl, ADA_ROWS, D), F32)],
                          compiler_params=_params(("parallel", "arbitrary")))(cmat, ada_w, dm)


def _adamw(name, gparts, w, m, v):
    nl, R, C = w.shape
    gparts = [g if isinstance(g, tuple) else (g, 0) for g in gparts]
    n = gparts[0][0].shape[0]
    tr = R
    part_bytes = nl * n * C * gparts[0][0].dtype.itemsize * 2
    for cand in (512, 256, 128, 64, 32, 16, 8):
        if R % cand == 0 and cand * C * 4 <= 2 * 1024 * 1024 and cand * part_bytes <= VMEM_LIMIT // 2:
            tr = cand
            break
    nt = R // tr
    bc1 = 1.0 - ADAM_B1 ** ADAM_STEP
    bc2 = 1.0 - ADAM_B2 ** ADAM_STEP

    def body(*refs):
        g_refs = refs[:nl]
        w_ref, m_ref, v_ref, go_ref, d_ref, mo_ref, vo_ref = refs[nl:]
        for l in range(nl):
            @pl.when(pl.program_id(0) == l)
            def _(g_ref=g_refs[l]):
                g = g_ref[0].astype(F32)
                for j in range(1, n):
                    g = g + g_ref[j].astype(F32)
                m2 = ADAM_B1 * m_ref[...] + (1.0 - ADAM_B1) * g
                v2 = ADAM_B2 * v_ref[...] + (1.0 - ADAM_B2) * (g * g)
                go_ref[...] = g
                mo_ref[...] = m2
                vo_ref[...] = v2
                d_ref[...] = -ADAM_LR * ((m2 / bc1) / (jnp.sqrt(v2 / bc2) + ADAM_EPS) + ADAM_WD * w_ref[...])

    def gspec(l):
        first = gparts[l][1] // tr
        return pl.BlockSpec((n, tr, C),
                            lambda lyr, i: (0, first + jnp.where(lyr < l, 0, jnp.where(lyr > l, nt - 1, i)), 0))

    row = pl.BlockSpec((None, tr, C), lambda lyr, i: (lyr, i, 0))
    out = jax.ShapeDtypeStruct((nl, R, C), F32)
    return _pcall(body, name=name, grid=(nl, nt), in_specs=[gspec(l) for l in range(nl)] + [row, row, row],
                  out_specs=[row, row, row, row], out_shape=[out, out, out, out],
                  ins=(*[g for g, _ in gparts], w, m, v), sem=("arbitrary", "arbitrary"))


def _sum_parts(name, parts):
    n, R, C = parts.shape

    def body(p_ref, o_ref):
        s = p_ref[0]
        for j in range(1, n):
            s = s + p_ref[j]
        o_ref[...] = s

    return pl.pallas_call(body, name=name, out_shape=jax.ShapeDtypeStruct((R, C), F32),
                          compiler_params=_params(None))(parts)


def _discretize(lam_re, lam_im, log_step, b_re, b_im):
    dt = jnp.exp(log_step)[:, None]
    mag = jnp.exp(lam_re * dt)
    abar_re = mag * jnp.cos(lam_im * dt)
    abar_im = mag * jnp.sin(lam_im * dt)
    nr, ni = abar_re - 1.0, abar_im
    den = lam_re * lam_re + lam_im * lam_im
    fr = (nr * lam_re + ni * lam_im) / den
    fi = (ni * lam_re - nr * lam_im) / den
    bbar_re = fr[..., None] * b_re - fi[..., None] * b_im
    bbar_im = fr[..., None] * b_im + fi[..., None] * b_re
    return abar_re, abar_im, bbar_re, bbar_im


def _s5_pack(abar, bbar, cmat):
    G = abar[0][0].shape[0]
    NT = G // S5_TILE_G
    eye = jnp.eye(S5_TILE_G, dtype=F32)
    a4 = jnp.stack([abar[d][r] for d in range(2) for r in range(2)]).reshape(4, NT, S5_TILE_W).transpose(1, 0, 2)
    apar = jnp.concatenate([a4, jnp.zeros((NT, 4, S5_TILE_W), F32)], axis=1)
    b4 = jnp.stack([bbar[d][r] for d in range(2) for r in range(2)]).reshape(4, NT, S5_TILE_G, S5_P, S5_CH)
    bblk = jnp.einsum("kjgpc,gh->jkgchp", b4, eye).reshape(NT, 4, S5_TILE_CH, S5_TILE_W)
    c4 = jnp.stack([cmat[d][r] for d in range(2) for r in range(2)]).reshape(4, NT, S5_TILE_G, S5_CH, S5_P)
    cblk = jnp.einsum("kjgcp,gh->jkgphc", c4, eye).reshape(NT, 4, S5_TILE_W, S5_TILE_CH)
    return apar, bblk.astype(_ACT), cblk.astype(_ACT)


def _s5_unpack(dapar, dbblk, dcblk, G):
    NT = G // S5_TILE_G
    eye = jnp.eye(S5_TILE_G, dtype=F32)
    da = dapar[:, :4, :].reshape(NT, 2, 2, S5_TILE_G, S5_P).transpose(1, 2, 0, 3, 4).reshape(2, 2, G, S5_P)
    db = jnp.einsum("jkgchp,gh->kjgpc", dbblk.reshape(NT, 4, S5_TILE_G, S5_CH, S5_TILE_G, S5_P), eye,
                    precision=lax.Precision.HIGHEST)
    dc = jnp.einsum("jkgphc,gh->kjgcp", dcblk.reshape(NT, 4, S5_TILE_G, S5_P, S5_TILE_G, S5_CH), eye,
                    precision=lax.Precision.HIGHEST)
    return da, db.reshape(2, 2, G, S5_P, S5_CH), dc.reshape(2, 2, G, S5_CH, S5_P)


def _to_segments(a):
    L, D = a.shape
    return a.reshape(NSEG, L // NSEG, D).transpose(1, 0, 2).reshape(L, D)


def _from_segments(a):
    L, D = a.shape
    return a.reshape(L // NSEG, NSEG, D).transpose(1, 0, 2).reshape(L, D)


def _pos_emb(n_tokens, dim):
    rows = n_tokens // GRID_W
    quarter = dim // 4
    omega = 1.0 / (POS_BASE ** (jnp.arange(quarter, dtype=F32) / quarter))

    def enc(p):
        ang = p[:, None] * omega[None, :]
        return jnp.concatenate([jnp.sin(ang), jnp.cos(ang)], axis=-1)

    rtab = enc(jnp.arange(rows, dtype=F32))
    ctab = enc(jnp.arange(GRID_W, dtype=F32))
    return jnp.concatenate([jnp.repeat(rtab, GRID_W, axis=0), jnp.tile(ctab, (rows, 1))], axis=-1)


def _vec(D, **rows):
    names = {"gpost": V_GPOST, "gate": V_GATE, "yscale": V_YSCALE, "gpre": V_GPRE, "shift": V_SHIFT, "scale": V_SCALE}
    out = [jnp.zeros((D,), F32)] * 8
    out[V_YSCALE] = jnp.ones((D,), F32)
    for k, v in rows.items():
        out[names[k]] = v.reshape(D).astype(F32)
    return jnp.stack(out)


def _row0(v, D):
    return jnp.concatenate([v.reshape(1, D).astype(F32), jnp.zeros((7, D), F32)], axis=0)


def _my_block(full, axis, n_local):
    return lax.dynamic_slice_in_dim(full, _my_index() * n_local, n_local, axis)


def kernel(x, c, ctx, c_ctx, ada_w, ada_b, norm_g, s5_lam_re, s5_lam_im, s5_log_step, s5_b_re, s5_b_im, s5_c_re, s5_c_im, s5_d, s5_glu_w, pool_w, pool_scale, ffn_up, ffn_conv, ffn_conv_b, ffn_down, loss_target, m_c_ctx, m_ada_w, m_ada_b, m_norm_g, m_s5_lam_re, m_s5_lam_im, m_s5_log_step, m_s5_b_re, m_s5_b_im, m_s5_c_re, m_s5_c_im, m_s5_d, m_s5_glu_w, m_pool_w, m_pool_scale, m_ffn_up, m_ffn_conv, m_ffn_conv_b, m_ffn_down, v_c_ctx, v_ada_w, v_ada_b, v_norm_g, v_s5_lam_re, v_s5_lam_im, v_s5_log_step, v_s5_b_re, v_s5_b_im, v_s5_c_re, v_s5_c_im, v_s5_d, v_s5_glu_w, v_pool_w, v_pool_scale, v_ffn_up, v_ffn_conv, v_ffn_conv_b, v_ffn_down):
    L, D = x.shape[1], x.shape[2]
    LC = ctx.shape[1]
    G = s5_lam_re.shape[2]
    n_ada = ada_w.shape[2]
    nb_up = ffn_up.shape[2]
    r_down = ffn_down.shape[1]
    FF = N_DEV * r_down
    n_pool = len(POOL_WINDOWS)
    pc = D // n_pool
    pr = pool_w.shape[2]
    ng_loc = norm_g.shape[2]
    me = _my_index()
    axes = ("x", "y", "c")

    up_b = [ffn_up[i].astype(_ACT) for i in range(2)]
    down_b = [ffn_down[i].astype(_ACT) for i in range(2)]
    glu_b = s5_glu_w[0].astype(_ACT)
    pool_b = pool_w[0].reshape(n_pool * pr, pc).astype(_ACT)

    small_loc = jnp.concatenate([c.reshape(-1), norm_g.reshape(-1), pool_scale.reshape(-1), ffn_conv.reshape(-1)])
    n_small = small_loc.shape[0]
    small_g, = _exchange([[jnp.pad(small_loc, (0, (-n_small) % LANE)).reshape(1, -1)]], mode="gather", name="gather_small")
    small_g = small_g.reshape(N_DEV, -1)
    o = 0
    c_all = small_g[:, o:o + D]
    o += D
    ng_all = small_g[:, o:o + 8 * ng_loc].reshape(N_DEV, 2, 4, ng_loc).transpose(1, 2, 0, 3).reshape(2, 4, D)
    o += 8 * ng_loc
    pscale_all = small_g[:, o:o + ng_loc].reshape(D)
    o += ng_loc
    conv_all = small_g[:, o:o + 6 * nb_up].reshape(N_DEV, 2, 3, nb_up).transpose(1, 2, 0, 3).reshape(2, 3, 2 * FF)

    cmat = jnp.concatenate([c_all, c_ctx.reshape(1, D), jnp.zeros((ADA_ROWS - N_DEV - 1, D), F32)], axis=0)
    ada_b_loc = _my_block(ada_b, 1, n_ada).reshape(2, 1, n_ada)
    mods_loc = _ada_fwd(cmat, ada_w, ada_b_loc)
    mods_g, = _exchange([[mods_loc]], mode="gather", name="gather_mods")
    mods_rows = mods_g.reshape(N_DEV, 2, ADA_ROWS, n_ada).transpose(1, 2, 0, 3).reshape(2, ADA_ROWS, 6, D)
    mod = lax.dynamic_index_in_dim(mods_rows, me, axis=1, keepdims=False)
    mod_c = mods_rows[0, N_DEV]

    def disc_all(lr, li, ls, br, bi):
        return [_discretize(lr[d], li[d], ls[d], br[d], bi[d]) for d in range(2)]

    disc, disc_vjp = jax.vjp(disc_all, s5_lam_re[0], s5_lam_im[0], s5_log_step[0], s5_b_re[0], s5_b_im[0])
    apar, bblk, cblk = _s5_pack([(disc[d][0], disc[d][1]) for d in range(2)],
                                [(disc[d][2], disc[d][3]) for d in range(2)],
                                [(s5_c_re[0, d], s5_c_im[0, d]) for d in range(2)])
    dsk = _row0(s5_d[0], D)
    cw = []
    for i in range(2):
        taps = conv_all[i].reshape(3, 2, FF).transpose(1, 0, 2)
        cw.append(jnp.concatenate([taps, ffn_conv_b[i].reshape(2, 1, FF), jnp.zeros((2, 4, FF), F32)], axis=1))

    vecs = {
        "b0": _vec(D, gpre=ng_all[0, 0], shift=mod[0, 0], scale=mod[0, 1]),
        "c0": _vec(D, gpre=ng_all[0, 0], shift=mod_c[0], scale=mod_c[1]),
        "b1": _vec(D, gpost=ng_all[0, 1], gate=mod[0, 2], gpre=ng_all[0, 2], shift=mod[0, 3], scale=mod[0, 4]),
        "b2": _vec(D, gpost=ng_all[0, 3], gate=mod[0, 5], gpre=ng_all[1, 0], shift=mod[1, 0], scale=mod[1, 1]),
        "b3": _vec(D, gpost=ng_all[1, 1], gate=mod[1, 2], yscale=pscale_all, gpre=ng_all[1, 2], shift=mod[1, 3],
                   scale=mod[1, 4]),
        "b4": _vec(D, gpost=ng_all[1, 3], gate=mod[1, 5]),
    }

    x0, u0 = _rows_fwd("rows_fwd_b0", x[0], _pos_emb(L, D), vecs["b0"], add=True, u_dtype=_ACT)
    uc, = _rows_fwd("rows_fwd_ctx", ctx[0], None, vecs["c0"], want_x=False, u_dtype=_ACT)
    u0s, ucs = _to_segments(u0), _to_segments(uc)
    (y_s5, z_s5), (glu_g, up_g0, down_g0) = _s5_fwd(u0s, ucs, bblk, cblk, apar, dsk,
                                                    rider=([[glu_b], [up_b[0]], [down_b[0]]], "gather2"))
    vg = _colblock_fwd("glu_fwd_mm", z_s5, glu_g, 0, _ACT)
    mix0 = _from_segments(_glu_fwd("glu_fwd", vg))
    x1, un0 = _rows_fwd("rows_fwd_b1", x0, mix0, vecs["b1"], u_dtype=_ACT)
    h0, (up_g1,) = _colblock_fwd("ffn0_up", un0, up_g0, 0, _ACT, rider=([[up_b[1]]], "gather2"))
    act0 = _conv_swiglu_fwd("ffn0_conv", h0, cw[0])
    f0, (down_g1, pool_g) = _rowblock_fwd("ffn0_down", act0, down_g0, 0, rider=([[down_b[1]], [pool_b]], "gather2"))
    pool_full = pool_g.reshape(N_DEV, n_pool, pr, pc).transpose(1, 0, 2, 3).reshape(n_pool, pc, pc)
    x2, u1 = _rows_fwd("rows_fwd_b2", x1, f0, vecs["b2"], u_dtype=F32)
    p1 = _pool_window("pool_fwd", u1, False, _ACT)
    ypre1 = _group_mm("pool_fwd_mm", p1, pool_full, "nn", F32)
    x3, un1 = _rows_fwd("rows_fwd_b3", x2, ypre1, vecs["b3"], u_dtype=_ACT)
    h1 = _colblock_fwd("ffn1_up", un1, up_g1, 0, _ACT)
    act1 = _conv_swiglu_fwd("ffn1_conv", h1, cw[1])
    f1 = _rowblock_fwd("ffn1_down", act1, down_g1, 0)
    dx4, loss_blk = _rows_fwd("rows_fwd_b4", x3, f1, vecs["b4"], target=loss_target[0])
    loss = lax.psum(loss_blk[0, 0], axes)

    df1, red4 = _rows_bwd("rows_bwd_b4", dx4, None, None, f1, vecs["b4"], dy_dtype=_ACT, want_dx=False)
    dact1 = _rowblock_dgrad("ffn1_down_dgrad", df1, down_g1, 0)
    ddown1 = _rowblock_wgrad("ffn1_down_wgrad", act1, df1)
    (dh1, dcw1), (gp_down1,) = _conv_swiglu_bwd("ffn1_conv_bwd", h1, cw[1], dact1,
                                                rider=([[ddown1.reshape(N_DEV, r_down, D)]], "scatter"))
    dun1 = _colblock_dgrad("ffn1_up_dgrad", dh1, up_g1, 0, F32)
    dup1 = _colblock_wgrad("ffn1_up_wgrad", un1, dh1)
    dx3, dy3, red3 = _rows_bwd("rows_bwd_b3", dx4, dun1, x3, ypre1, vecs["b3"])
    dypre1, red_ps = _colscale_bwd("pool_scale_bwd", dy3, ypre1, _row0(pscale_all, D))
    dp1 = _group_mm("pool_dgrad", dypre1, pool_full, "nt", F32)
    dpool = _group_wgrad("pool_wgrad", p1, dypre1, n_pool)
    du1 = _pool_window("pool_bwd", dp1, True, F32)
    dx2, df0, red2 = _rows_bwd("rows_bwd_b2", dx3, du1, x2, f0, vecs["b2"], dy_dtype=_ACT)
    dact0 = _rowblock_dgrad("ffn0_down_dgrad", df0, down_g0, 0)
    ddown0 = _rowblock_wgrad("ffn0_down_wgrad", act0, df0)
    (dh0, dcw0), (gp_down0,) = _conv_swiglu_bwd("ffn0_conv_bwd", h0, cw[0], dact0,
                                                rider=([[ddown0.reshape(N_DEV, r_down, D)]], "scatter"))
    dun0, (gp_up1a,) = _colblock_dgrad("ffn0_up_dgrad", dh0, up_g0, 0, F32,
                                       rider=([[(dup1, (0, D // 2))]], "scatter"))
    dup0, (gp_up1b,) = _colblock_wgrad("ffn0_up_wgrad", un0, dh0, rider=([[(dup1, (D // 2, D // 2))]], "scatter"))
    dx1, dmix0, red1 = _rows_bwd("rows_bwd_b1", dx2, dun0, x1, mix0, vecs["b1"])
    dvg = _glu_bwd("glu_bwd", vg, _to_segments(dmix0))
    dz = _colblock_dgrad("glu_dgrad", dvg, glu_g, 0, _ACT)
    dglu = _colblock_wgrad("glu_wgrad", z_s5, dvg)
    dpool_blocks = dpool.reshape(n_pool, N_DEV, pr, pc).transpose(1, 0, 2, 3).reshape(N_DEV, n_pool * pr, pc)
    (du0s, ducs, dbblk, dcblk, dapar, ddsk), (gp_up0, gp_glu, gp_pool) = _s5_bwd(
        u0s, ucs, dz, y_s5, bblk, cblk, apar, dsk, rider=([[dup0], [dglu], [dpool_blocks]], "scatter"))
    grad_x, red0 = _rows_bwd("rows_bwd_b0", dx1, _from_segments(du0s), x0, None, vecs["b0"])
    redc, = _rows_bwd("rows_bwd_ctx", None, _from_segments(ducs), ctx[0], None, vecs["c0"], want_dx=False)

    zero_d = jnp.zeros((D,), F32)
    dmod = jnp.stack([
        jnp.stack([red0[R_SHIFT], red0[R_SCALE], red1[R_GATE], red1[R_SHIFT], red1[R_SCALE], red2[R_GATE]]),
        jnp.stack([red2[R_SHIFT], red2[R_SCALE], red3[R_GATE], red3[R_SHIFT], red3[R_SCALE], red4[R_GATE]])])
    dmod_c = jnp.stack([jnp.stack([redc[R_SHIFT], redc[R_SCALE]] + [zero_d] * 4), jnp.zeros((6, D), F32)])
    dm_g, = _exchange([[jnp.stack([dmod, dmod_c], axis=1).reshape(2, 2, 6 * D)]], mode="gather", name="gather_dmods")
    dm_g = dm_g.reshape(N_DEV, 2, 2, 6 * D)
    dm_ctx = _sum_parts("sum_dmod_ctx", dm_g[:, :, 1, :])
    dm_rows = jnp.concatenate([dm_g[:, :, 0, :].transpose(1, 0, 2), dm_ctx[:, None, :]], axis=1)
    grad_ada_b = _sum_parts("sum_ada_b", dm_rows.transpose(1, 0, 2))
    dm_cols = dm_rows.reshape(2, N_DEV + 1, N_DEV, n_ada)
    dm_mine = lax.dynamic_index_in_dim(dm_cols, me, axis=2, keepdims=False)
    dm_mine = jnp.concatenate([dm_mine, jnp.zeros((2, ADA_ROWS - N_DEV - 1, n_ada), F32)], axis=1)
    grad_ada_w, dcond = _ada_bwd(cmat, ada_w, dm_mine)
    dcctx_part = dcond[0, N_DEV] + dcond[1, N_DEV]

    da, db, dc = _s5_unpack(dapar, dbblk, dcblk, G)
    dnorm = jnp.stack([
        jnp.stack([red0[R_GPRE] + redc[R_GPRE], red1[R_GPOST], red1[R_GPRE], red2[R_GPOST]]),
        jnp.stack([red2[R_GPRE], red3[R_GPOST], red3[R_GPRE], red4[R_GPOST]])])
    dconv = jnp.stack([d[:, :3, :].transpose(1, 0, 2).reshape(3, 2 * FF) for d in (dcw0, dcw1)])
    dconv_b = jnp.stack([d[:, 3, :].reshape(2 * FF) for d in (dcw0, dcw1)])
    pieces = [dcctx_part, dnorm, da, db, dc, ddsk[:, 0, :], red_ps[0], dconv, dconv_b]
    flat = jnp.concatenate([p.reshape(-1) for p in pieces])
    n_flat = flat.shape[0]
    per_dev = -(-n_flat // (N_DEV * 8 * LANE)) * 8 * LANE
    flat = jnp.pad(flat, (0, N_DEV * per_dev - n_flat)).reshape(N_DEV, per_dev // LANE, LANE)
    parts, = _exchange([[flat]], mode="scatter", name="scatter_small_grads")
    mine = _sum_parts("sum_small_grads", parts.reshape(N_DEV, per_dev // LANE, LANE))
    summed, = _exchange([[mine]], mode="gather", name="gather_small_grads")
    summed = summed.reshape(-1)
    red_pieces, o = [], 0
    for p in pieces:
        red_pieces.append(summed[o:o + p.size].reshape(p.shape))
        o += p.size
    g_cctx, g_norm, g_a, g_b, g_c, g_d, g_pscale, g_conv, g_conv_b = red_pieces
    cot = [(g_a[d, 0], g_a[d, 1], g_b[d, 0], g_b[d, 1]) for d in range(2)]
    g_lam_re, g_lam_im, g_log_step, g_b_re, g_b_im = disc_vjp(cot)

    out = {}

    def put(name, res, shape):
        out[name] = tuple(r.reshape(shape) for r in res)

    gp_up0 = gp_up0.reshape(N_DEV, D, nb_up)
    halves = (4, D // 2, nb_up)
    put("ffn_up", _adamw("adamw_ffn_up", [(gp_up0, 0), (gp_up0, D // 2), gp_up1a.reshape(N_DEV, D // 2, nb_up),
                                          gp_up1b.reshape(N_DEV, D // 2, nb_up)],
                         ffn_up.reshape(halves), m_ffn_up.reshape(halves), v_ffn_up.reshape(halves)), ffn_up.shape)
    put("ffn_down", _adamw("adamw_ffn_down", [g.reshape(N_DEV, r_down, D) for g in (gp_down0, gp_down1)],
                           ffn_down, m_ffn_down, v_ffn_down), ffn_down.shape)
    put("s5_glu_w", _adamw("adamw_glu", [gp_glu.reshape(N_DEV, D, -1)], s5_glu_w, m_s5_glu_w, v_s5_glu_w),
        s5_glu_w.shape)
    pool_rows = (1, n_pool * pr, pc)
    put("pool_w", _adamw("adamw_pool", [gp_pool.reshape(N_DEV, n_pool * pr, pc)], pool_w.reshape(pool_rows),
                         m_pool_w.reshape(pool_rows), v_pool_w.reshape(pool_rows)), pool_w.shape)
    put("ada_w", _adamw("adamw_ada_w", [grad_ada_w[i][None] for i in range(2)], ada_w, m_ada_w, v_ada_w), ada_w.shape)

    for nm, w, m, v, g in (("s5_b_re", s5_b_re, m_s5_b_re, v_s5_b_re, g_b_re),
                           ("s5_b_im", s5_b_im, m_s5_b_im, v_s5_b_im, g_b_im),
                           ("s5_c_re", s5_c_re, m_s5_c_re, v_s5_c_re, g_c[:, 0]),
                           ("s5_c_im", s5_c_im, m_s5_c_im, v_s5_c_im, g_c[:, 1])):
        rows = (1, w.size // w.shape[-1], w.shape[-1])
        put(nm, _adamw("adamw_" + nm, [g.reshape(rows)], w.reshape(rows), m.reshape(rows), v.reshape(rows)), w.shape)

    small = [
        ("c_ctx", c_ctx, m_c_ctx, v_c_ctx, g_cctx),
        ("ada_b", ada_b, m_ada_b, v_ada_b, grad_ada_b),
        ("norm_g", norm_g, m_norm_g, v_norm_g, _my_block(g_norm, 2, ng_loc)),
        ("s5_lam_re", s5_lam_re, m_s5_lam_re, v_s5_lam_re, g_lam_re),
        ("s5_lam_im", s5_lam_im, m_s5_lam_im, v_s5_lam_im, g_lam_im),
        ("s5_log_step", s5_log_step, m_s5_log_step, v_s5_log_step, g_log_step),
        ("s5_d", s5_d, m_s5_d, v_s5_d, g_d),
        ("pool_scale", pool_scale, m_pool_scale, v_pool_scale, _my_block(g_pscale, 0, ng_loc)),
        ("ffn_conv", ffn_conv, m_ffn_conv, v_ffn_conv, _my_block(g_conv, 2, nb_up)),
        ("ffn_conv_b", ffn_conv_b, m_ffn_conv_b, v_ffn_conv_b, g_conv_b),
    ]
    n_sm = sum(w.size for _, w, _, _, _ in small)
    rows_sm = -(-n_sm // (512 * LANE)) * 512

    def flat_of(k):
        f = jnp.concatenate([t[k].reshape(-1) for t in small])
        return jnp.pad(f, (0, rows_sm * LANE - n_sm)).reshape(rows_sm, LANE)

    res_sm = _adamw("adamw_small", [flat_of(4)[None]], flat_of(1)[None], flat_of(2)[None], flat_of(3)[None])
    o = 0
    for name, w, _, _, _ in small:
        out[name] = tuple(r.reshape(-1)[o:o + w.size].reshape(w.shape) for r in res_sm)
        o += w.size

    order = ["c_ctx", "ada_w", "ada_b", "norm_g", "s5_lam_re", "s5_lam_im", "s5_log_step", "s5_b_re", "s5_b_im",
             "s5_c_re", "s5_c_im", "s5_d", "s5_glu_w", "pool_w", "pool_scale", "ffn_up", "ffn_conv", "ffn_conv_b",
             "ffn_down"]
    return (loss, grad_x.reshape(x.shape), *[out[n][0] for n in order], *[out[n][1] for n in order],
            *[out[n][2] for n in order], *[out[n][3] for n in order])
```

```python
import functools
import math

import jax
import jax.numpy as jnp
from jax import lax
from jax.experimental import pallas as pl
from jax.experimental.pallas import tpu as pltpu

F32 = jnp.float32
_ACT = jnp.bfloat16
N_DEV = 8
NSEG = 8
S5_CH = 16
S5_P = 64
LANE = 128
S5_TILE_CH = LANE
S5_TILE_G = S5_TILE_CH // S5_CH
S5_TILE_W = S5_TILE_G * S5_P
GRID_W = 64
POOL_WINDOWS = (2, 4, 8, 16)
POOL_HALO = 64
RMS_EPS = 1e-6
POS_BASE = 10000.0
ADAM_LR, ADAM_B1, ADAM_B2, ADAM_EPS, ADAM_WD, ADAM_STEP = 0.001, 0.9, 0.999, 1e-08, 0.01, 10
VMEM_LIMIT = 48 * 1024 * 1024
VMEM_LIMIT_BIG = 58 * 1024 * 1024
MESH = pl.DeviceIdType.MESH
ANY = pl.BlockSpec(memory_space=pl.ANY)


def _params(sem, vmem=VMEM_LIMIT):
    return pltpu.CompilerParams(dimension_semantics=sem, vmem_limit_bytes=vmem)


def _my_index():
    return 4 * lax.axis_index("x") + 2 * lax.axis_index("y") + lax.axis_index("c")


def _xchg_plan(groups, mode):
    flat = [(g, l, a) for g, grp in enumerate(groups) for l, a in enumerate(grp)]
    outs = []
    for grp in groups:
        a, rows = _rows_of(grp[0])
        piece = a.shape[1:] if mode == "scatter" else a.shape
        if rows is not None:
            piece = (rows[1],) + tuple(piece[1:])
        outs.append(jax.ShapeDtypeStruct((N_DEV, len(grp)) + tuple(piece), a.dtype))
    return flat, outs


def _rows_of(entry):
    return entry if isinstance(entry, tuple) else (entry, None)


def _operands(flat):
    return [_rows_of(a)[0] for _, _, a in flat]


def _xchg_sems(n):
    return [pltpu.SemaphoreType.DMA((n, N_DEV - 1)), pltpu.SemaphoreType.DMA((n, N_DEV - 1)),
            pltpu.SemaphoreType.DMA((n,))]


def _xchg_copies(flat, mode, ins, out_refs, sems, waiting=True):
    send_sems, recv_sems, local_sems = sems
    x, y, c = lax.axis_index("x"), lax.axis_index("y"), lax.axis_index("c")
    me = 4 * x + 2 * y + c
    local, first, forwards = [], [], []

    def pair(s, j, dev):
        return dict(send_sem=send_sems.at[s, j], recv_sem=recv_sems.at[s, j], device_id=dev, device_id_type=MESH)

    def block(s, dev):
        rows = _rows_of(flat[s][2])[1]
        ref = ins[s].at[dev]
        return ref if rows is None else ref.at[pl.ds(rows[0], rows[1])]

    for s, (g, l, _) in enumerate(flat):
        src = block(s, me) if mode == "scatter" else ins[s]
        local.append(pltpu.make_async_copy(src, out_refs[g].at[me, l], local_sems.at[s]))
    if mode == "gather2":
        sib, sib_idx = (x, y, 1 - c), 4 * x + 2 * y + (1 - c)
        for s, (g, l, _) in enumerate(flat):
            slot = lambda dev, g=g, l=l: out_refs[g].at[dev, l]
            targets = [(sib, sib_idx)] + [((qx, qy, c), 4 * qx + 2 * qy + c)
                                          for qx, qy in ((1 - x, y), (x, 1 - y), (1 - x, 1 - y))]
            for j, (dev, idx) in enumerate(targets):
                send = pltpu.make_async_remote_copy(src_ref=ins[s], dst_ref=slot(me), **pair(s, j, dev))
                arrive = pltpu.make_async_remote_copy(src_ref=ins[s], dst_ref=slot(idx), **pair(s, j, dev)) if waiting else None
                first.append((send, arrive))
            if waiting:
                for j, (dev, idx) in enumerate(targets[1:]):
                    other = 4 * dev[0] + 2 * dev[1] + (1 - c)
                    send = pltpu.make_async_remote_copy(src_ref=slot(idx), dst_ref=slot(idx), **pair(s, 4 + j, sib))
                    arrive = pltpu.make_async_remote_copy(src_ref=slot(idx), dst_ref=slot(other), **pair(s, 4 + j, sib))
                    forwards.append((first[len(first) - 3 + j][1], send, arrive))
        return local, first, forwards
    for k in range(1, N_DEV):
        px = 1 - x if k & 4 else x
        py = 1 - y if k & 2 else y
        pc = 1 - c if k & 1 else c
        peer = 4 * px + 2 * py + pc
        for s, (g, l, _) in enumerate(flat):
            src = block(s, peer) if mode == "scatter" else ins[s]
            send = pltpu.make_async_remote_copy(src_ref=src, dst_ref=out_refs[g].at[me, l], **pair(s, k - 1, (px, py, pc)))
            arrive = (pltpu.make_async_remote_copy(src_ref=src, dst_ref=out_refs[g].at[peer, l],
                                                   **pair(s, k - 1, (px, py, pc))) if waiting else None)
            first.append((send, arrive))
    return local, first, forwards


def _xchg_start(local, first, forwards):
    for cp in local:
        cp.start()
    for send, _ in first:
        send.start()


def _xchg_wait(local, first, forwards):
    gates = [gate for gate, _, _ in forwards]
    for gate, send, _ in forwards:
        gate.wait_recv()
        send.start()
    for _, arrive in first:
        if not any(arrive is gate for gate in gates):
            arrive.wait_recv()
    for _, _, arrive in forwards:
        arrive.wait_recv()
    for send, _ in first:
        send.wait_send()
    for _, send, _ in forwards:
        send.wait_send()
    for cp in local:
        cp.wait()


def _exchange(groups, mode, name):
    flat, outs = _xchg_plan(groups, mode)
    n = len(flat)

    def body(*refs):
        copies = _xchg_copies(flat, mode, refs[:n], refs[n:n + len(groups)], refs[n + len(groups):])
        _xchg_start(*copies)
        _xchg_wait(*copies)

    res = pl.pallas_call(body, name=name, out_shape=outs, in_specs=[ANY] * n, out_specs=[ANY] * len(groups),
                         scratch_shapes=_xchg_sems(n))(*_operands(flat))
    return list(res)


def _pcall(body, *, name, grid, in_specs, out_specs, out_shape, ins, scratch_shapes=(), sem=None, vmem=VMEM_LIMIT,
           rider=None):
    single = not isinstance(out_shape, (list, tuple))
    if rider is None:
        return pl.pallas_call(body, name=name, grid=grid, in_specs=list(in_specs), out_specs=out_specs,
                              out_shape=out_shape, scratch_shapes=list(scratch_shapes),
                              compiler_params=_params(sem, vmem))(*ins)
    groups, mode = rider
    flat, r_outs = _xchg_plan(groups, mode)
    n_in, n_out = len(ins), 1 if single else len(out_shape)
    nr, ng, ns = len(flat), len(groups), len(scratch_shapes)

    def wrapped(*refs):
        o1 = n_in + nr
        o2 = o1 + n_out
        o3 = o2 + ng
        r_in, r_out, sems = refs[n_in:o1], refs[o2:o3], refs[o3 + ns:]
        first = functools.reduce(jnp.logical_and, [pl.program_id(d) == 0 for d in range(len(grid))])
        last = functools.reduce(jnp.logical_and, [pl.program_id(d) == grid[d] - 1 for d in range(len(grid))])

        @pl.when(first)
        def _():
            _xchg_start(*_xchg_copies(flat, mode, r_in, r_out, sems, waiting=False))

        body(*refs[:n_in], *refs[o1:o2], *refs[o3:o3 + ns])

        @pl.when(last)
        def _():
            _xchg_wait(*_xchg_copies(flat, mode, r_in, r_out, sems))

    outs = pl.pallas_call(
        wrapped, name=name, grid=grid, in_specs=list(in_specs) + [ANY] * nr,
        out_specs=([out_specs] if single else list(out_specs)) + [ANY] * ng,
        out_shape=([out_shape] if single else list(out_shape)) + r_outs,
        scratch_shapes=list(scratch_shapes) + _xchg_sems(nr),
        compiler_params=_params(("arbitrary",) * len(grid), vmem))(*ins, *_operands(flat))
    base = list(outs[:n_out])
    return (base[0] if single else base), list(outs[n_out:])


_DIMS = {"nn": (((1,), (0,)), ((), ())), "nt": (((1,), (1,)), ((), ())), "tn": (((0,), (0,)), ((), ()))}


def _mm(name, a, b, a_spec, b_spec, o_spec, out_shape, grid, dims, rider=None):
    nk = grid[2]
    acc_shape = tuple(d for d in o_spec.block_shape if d is not None)
    dn = _DIMS[dims]

    def tile(ref):
        v = ref[...]
        return v.reshape((-1, v.shape[-1])).astype(_ACT)

    def body(a_ref, b_ref, o_ref, *scratch):
        def part():
            return lax.dot_general(tile(a_ref), tile(b_ref), dn, preferred_element_type=F32)

        if nk == 1:
            o_ref[...] = part().reshape(o_ref.shape).astype(o_ref.dtype)
            return
        acc_ref, = scratch
        k = pl.program_id(2)

        @pl.when(k == 0)
        def _():
            acc_ref[...] = part()

        @pl.when(k > 0)
        def _():
            acc_ref[...] += part()

        @pl.when(k == nk - 1)
        def _():
            o_ref[...] = acc_ref[...].reshape(o_ref.shape).astype(o_ref.dtype)

    acc2d = (math.prod(acc_shape[:-1]), acc_shape[-1])
    return _pcall(body, name=name, out_shape=out_shape, grid=grid, in_specs=[a_spec, b_spec], out_specs=o_spec,
                  scratch_shapes=[] if nk == 1 else [pltpu.VMEM(acc2d, F32)], ins=(a, b),
                  sem=("parallel", "parallel", "arbitrary"), rider=rider)


def _row_tile(n, want):
    t = min(n, want)
    assert n % t == 0, (n, t)
    return t


def _colblock_fwd(name, xa, wg, layer, out_dtype, rider=None):
    L, K = xa.shape
    nb = wg.shape[3]
    half = N_DEV // 2
    tm = _row_tile(L, 512)
    return _mm(name, xa, wg,
               pl.BlockSpec((tm, K), lambda j, i, k: (i, 0)),
               pl.BlockSpec((None, None, K, nb), lambda j, i, k: (j, layer, 0, 0)),
               pl.BlockSpec((None, tm, nb), lambda j, i, k: (j // half, i, j % half)),
               jax.ShapeDtypeStruct((2, L, half * nb), out_dtype), (N_DEV, L // tm, 1), "nn", rider=rider)


def _colblock_dgrad(name, dh, wg, layer, out_dtype, rider=None):
    _, L, _ = dh.shape
    K, nb = wg.shape[2], wg.shape[3]
    half = N_DEV // 2
    tm = _row_tile(L, 512)
    return _mm(name, dh, wg,
               pl.BlockSpec((None, tm, nb), lambda i, j, k: (k // half, i, k % half)),
               pl.BlockSpec((None, None, K, nb), lambda i, j, k: (k, layer, 0, 0)),
               pl.BlockSpec((tm, K), lambda i, j, k: (i, 0)),
               jax.ShapeDtypeStruct((L, K), out_dtype), (L // tm, 1, N_DEV), "nt", rider=rider)


def _colblock_wgrad(name, xa, dh, rider=None):
    L, K = xa.shape
    half = N_DEV // 2
    nb = dh.shape[2] // half
    tm = _row_tile(K, 512)
    tk = L
    return _mm(name, xa, dh,
               pl.BlockSpec((tk, tm), lambda j, i, k: (k, i)),
               pl.BlockSpec((None, tk, nb), lambda j, i, k: (j // half, k, j % half)),
               pl.BlockSpec((None, tm, nb), lambda j, i, k: (j, i, 0)),
               jax.ShapeDtypeStruct((N_DEV, K, nb), _ACT), (N_DEV, K // tm, L // tk), "tn", rider=rider)


def _rowblock_fwd(name, xa, wg, layer, rider=None):
    L, FF = xa.shape
    r, D = wg.shape[2], wg.shape[3]
    tm = _row_tile(L, 512)
    return _mm(name, xa, wg,
               pl.BlockSpec((tm, 2 * r), lambda i, j, k: (i, k)),
               pl.BlockSpec((2, None, r, D), lambda i, j, k: (k, layer, 0, 0)),
               pl.BlockSpec((tm, D), lambda i, j, k: (i, 0)),
               jax.ShapeDtypeStruct((L, D), F32), (L // tm, 1, N_DEV // 2), "nn", rider=rider)


def _rowblock_dgrad(name, dy, wg, layer, rider=None):
    L, D = dy.shape
    r = wg.shape[2]
    tm = _row_tile(L, 512)
    return _mm(name, dy, wg,
               pl.BlockSpec((tm, D), lambda i, j, k: (i, 0)),
               pl.BlockSpec((2, None, r, D), lambda i, j, k: (j, layer, 0, 0)),
               pl.BlockSpec((tm, 2 * r), lambda i, j, k: (i, j)),
               jax.ShapeDtypeStruct((L, N_DEV * r), _ACT), (L // tm, N_DEV // 2, 1), "nt", rider=rider)


def _rowblock_wgrad(name, xa, dy, rider=None):
    L, FF = xa.shape
    D = dy.shape[1]
    tm = FF // (N_DEV // 2)
    tn = _row_tile(D, 1024)
    tk = _row_tile(L, 2048)
    return _mm(name, xa, dy,
               pl.BlockSpec((tk, tm), lambda i, j, k: (k, i)),
               pl.BlockSpec((tk, tn), lambda i, j, k: (k, j)),
               pl.BlockSpec((tm, tn), lambda i, j, k: (i, j)),
               jax.ShapeDtypeStruct((FF, D), _ACT), (FF // tm, D // tn, L // tk), "tn", rider=rider)


def _group_mm(name, xa, w, dims, out_dtype):
    L, D = xa.shape
    ng, pc, _ = w.shape
    tm = _row_tile(L, 512)
    return _mm(name, xa, w,
               pl.BlockSpec((tm, pc), lambda i, g, k: (i, g)),
               pl.BlockSpec((None, pc, pc), lambda i, g, k: (g, 0, 0)),
               pl.BlockSpec((tm, pc), lambda i, g, k: (i, g)),
               jax.ShapeDtypeStruct((L, D), out_dtype), (L // tm, ng, 1), dims)


def _group_wgrad(name, p, dy, ng):
    L, D = p.shape
    pc = D // ng
    tk = _row_tile(L, 512)
    return _mm(name, p, dy,
               pl.BlockSpec((tk, pc), lambda g, j, k: (k, g)),
               pl.BlockSpec((tk, pc), lambda g, j, k: (k, g)),
               pl.BlockSpec((None, pc, pc), lambda g, j, k: (g, 0, 0)),
               jax.ShapeDtypeStruct((ng, pc, pc), _ACT), (ng, 1, L // tk), "tn")


V_GPOST, V_GATE, V_YSCALE, V_GPRE, V_SHIFT, V_SCALE = range(6)
R_SHIFT, R_SCALE, R_GPRE, R_GATE, R_GPOST = range(5)
ROW_TILE = 256


def _rstd(v):
    return lax.rsqrt(jnp.mean(v * v, axis=-1, keepdims=True) + RMS_EPS)


def _rows_fwd(name, xres, y, vec, *, add=False, target=None, want_x=True, u_dtype=None):
    L, D = xres.shape
    tm = _row_tile(L, ROW_TILE)
    has_y = y is not None
    last = target is not None
    has_u = u_dtype is not None

    def body(*refs):
        refs = list(refs)
        xres_ref = refs.pop(0)
        y_ref = refs.pop(0) if has_y else None
        vec_ref = refs.pop(0)
        tgt_ref = refs.pop(0) if last else None
        xnew = xres_ref[...]
        if has_y and add:
            xnew = xnew + y_ref[...]
        elif has_y:
            ye = y_ref[...] * vec_ref[V_YSCALE:V_YSCALE + 1, :]
            xnew = xnew + vec_ref[V_GATE:V_GATE + 1, :] * (ye * _rstd(ye) * vec_ref[V_GPOST:V_GPOST + 1, :])
        if last:
            dx_ref, loss_ref = refs
            diff = xnew - tgt_ref[...]
            dx_ref[...] = diff * (1.0 / D)

            @pl.when(pl.program_id(0) == 0)
            def _():
                loss_ref[...] = jnp.zeros_like(loss_ref)

            loss_ref[...] += jnp.sum(diff * diff) * (0.5 / D)
            return
        if want_x:
            refs.pop(0)[...] = xnew
        if has_u:
            u_ref, = refs
            n = xnew * _rstd(xnew) * vec_ref[V_GPRE:V_GPRE + 1, :]
            u_ref[...] = (n * (1.0 + vec_ref[V_SCALE:V_SCALE + 1, :]) + vec_ref[V_SHIFT:V_SHIFT + 1, :]).astype(u_ref.dtype)

    row = pl.BlockSpec((tm, D), lambda i: (i, 0))
    vspec = pl.BlockSpec((8, D), lambda i: (0, 0))
    ins, in_specs = [xres], [row]
    if has_y:
        ins.append(y)
        in_specs.append(row)
    ins.append(vec)
    in_specs.append(vspec)
    out_shape, out_specs = [], []
    if last:
        ins.append(target)
        in_specs.append(row)
        out_shape = [jax.ShapeDtypeStruct((L, D), F32), jax.ShapeDtypeStruct((8, LANE), F32)]
        out_specs = [row, pl.BlockSpec((8, LANE), lambda i: (0, 0))]
    else:
        if want_x:
            out_shape.append(jax.ShapeDtypeStruct((L, D), F32))
            out_specs.append(row)
        if has_u:
            out_shape.append(jax.ShapeDtypeStruct((L, D), u_dtype))
            out_specs.append(row)
    return pl.pallas_call(body, name=name, out_shape=out_shape, grid=(L // tm,), in_specs=in_specs,
                          out_specs=out_specs, compiler_params=_params(("arbitrary",)))(*ins)


def _rows_bwd(name, dxd, du, xnew, y, vec, dy_dtype=F32, want_dx=True):
    L, D = xnew.shape if xnew is not None else dxd.shape
    tm = _row_tile(L, ROW_TILE)
    has_dxd, has_pre, has_post = dxd is not None, du is not None, y is not None

    def body(*refs):
        refs = list(refs)
        dxd_ref = refs.pop(0) if has_dxd else None
        du_ref = refs.pop(0) if has_pre else None
        xnew_ref = refs.pop(0) if has_pre else None
        y_ref = refs.pop(0) if has_post else None
        vec_ref = refs.pop(0)
        dx_ref = refs.pop(0) if want_dx else None
        dy_ref = refs.pop(0) if has_post else None
        red_ref, = refs

        @pl.when(pl.program_id(0) == 0)
        def _():
            red_ref[...] = jnp.zeros_like(red_ref)

        def acc(rw, val):
            red_ref[rw:rw + 1, :] += jnp.sum(val, axis=0, keepdims=True)

        dxn = dxd_ref[...] if has_dxd else None
        if has_pre:
            xn = xnew_ref[...]
            r = _rstd(xn)
            nh = xn * r
            gpre = vec_ref[V_GPRE:V_GPRE + 1, :]
            dub = du_ref[...].astype(F32)
            acc(R_SHIFT, dub)
            acc(R_SCALE, dub * (nh * gpre))
            drn = dub * (1.0 + vec_ref[V_SCALE:V_SCALE + 1, :])
            acc(R_GPRE, drn * nh)
            dnh = drn * gpre
            t = r * (dnh - nh * jnp.mean(dnh * nh, axis=-1, keepdims=True))
            dxn = t if dxn is None else dxn + t
        if want_dx:
            dx_ref[...] = dxn
        if has_post:
            ye = y_ref[...] * vec_ref[V_YSCALE:V_YSCALE + 1, :]
            ry = _rstd(ye)
            yh = ye * ry
            gpost = vec_ref[V_GPOST:V_GPOST + 1, :]
            acc(R_GATE, dxn * (yh * gpost))
            drn2 = dxn * vec_ref[V_GATE:V_GATE + 1, :]
            acc(R_GPOST, drn2 * yh)
            dyh = drn2 * gpost
            dy_ref[...] = (ry * (dyh - yh * jnp.mean(dyh * yh, axis=-1, keepdims=True))).astype(dy_ref.dtype)

    row = pl.BlockSpec((tm, D), lambda i: (i, 0))
    vspec = pl.BlockSpec((8, D), lambda i: (0, 0))
    ins, in_specs = [], []
    for a in ([dxd] if has_dxd else []) + ([du, xnew] if has_pre else []) + ([y] if has_post else []):
        ins.append(a)
        in_specs.append(row)
    ins.append(vec)
    in_specs.append(vspec)
    out_shape, out_specs = [], []
    if want_dx:
        out_shape.append(jax.ShapeDtypeStruct((L, D), F32))
        out_specs.append(row)
    if has_post:
        out_shape.append(jax.ShapeDtypeStruct((L, D), dy_dtype))
        out_specs.append(row)
    out_shape.append(jax.ShapeDtypeStruct((8, D), F32))
    out_specs.append(vspec)
    return pl.pallas_call(body, name=name, out_shape=out_shape, grid=(L // tm,), in_specs=in_specs,
                          out_specs=out_specs, compiler_params=_params(("arbitrary",)))(*ins)


def _colscale_bwd(name, dy, ypre, scale):
    L, D = dy.shape
    tm = _row_tile(L, ROW_TILE)

    def body(dy_ref, yp_ref, s_ref, o_ref, red_ref):
        @pl.when(pl.program_id(0) == 0)
        def _():
            red_ref[...] = jnp.zeros_like(red_ref)

        d = dy_ref[...]
        o_ref[...] = (d * s_ref[0:1, :]).astype(o_ref.dtype)
        red_ref[0:1, :] += jnp.sum(d * yp_ref[...], axis=0, keepdims=True)

    row = pl.BlockSpec((tm, D), lambda i: (i, 0))
    vspec = pl.BlockSpec((8, D), lambda i: (0, 0))
    return pl.pallas_call(body, name=name, grid=(L // tm,), in_specs=[row, row, vspec], out_specs=[row, vspec],
                          out_shape=[jax.ShapeDtypeStruct((L, D), _ACT), jax.ShapeDtypeStruct((8, D), F32)],
                          compiler_params=_params(("arbitrary",)))(dy, ypre, scale)


def _sigmoid(v):
    return 1.0 / (1.0 + jnp.exp(-v))


def _glu_fwd(name, vg):
    _, L, D = vg.shape
    tm = _row_tile(L, ROW_TILE)

    def body(vg_ref, o_ref):
        o_ref[...] = vg_ref[0].astype(F32) * _sigmoid(vg_ref[1].astype(F32))

    return pl.pallas_call(body, name=name, grid=(L // tm,),
                          in_specs=[pl.BlockSpec((2, tm, D), lambda i: (0, i, 0))],
                          out_specs=pl.BlockSpec((tm, D), lambda i: (i, 0)),
                          out_shape=jax.ShapeDtypeStruct((L, D), F32),
                          compiler_params=_params(("parallel",)))(vg)


def _glu_bwd(name, vg, dout):
    _, L, D = vg.shape
    tm = _row_tile(L, ROW_TILE)

    def body(vg_ref, d_ref, o_ref):
        val, s = vg_ref[0].astype(F32), _sigmoid(vg_ref[1].astype(F32))
        d = d_ref[...]
        o_ref[0] = (d * s).astype(o_ref.dtype)
        o_ref[1] = (d * val * s * (1.0 - s)).astype(o_ref.dtype)

    return pl.pallas_call(body, name=name, grid=(L // tm,),
                          in_specs=[pl.BlockSpec((2, tm, D), lambda i: (0, i, 0)), pl.BlockSpec((tm, D), lambda i: (i, 0))],
                          out_specs=pl.BlockSpec((2, tm, D), lambda i: (0, i, 0)),
                          out_shape=jax.ShapeDtypeStruct((2, L, D), _ACT),
                          compiler_params=_params(("parallel",)))(vg, dout)


CONV_ROWS = 256


def _row_pick(blk, idx):
    rows = lax.broadcasted_iota(jnp.int32, blk.shape, 0)
    return jnp.sum(jnp.where(rows == idx, blk, 0.0), axis=0, keepdims=True)


def _shifted(ref, r0, rc, L):
    cur = ref[pl.ds(r0, rc), :].astype(F32)
    before = ref[pl.ds(pl.multiple_of(jnp.maximum(r0 - 16, 0), 16), 16), :].astype(F32)
    after = ref[pl.ds(pl.multiple_of(jnp.minimum(r0 + rc, L - 16), 16), 16), :].astype(F32)
    prev_row = jnp.where(r0 > 0, _row_pick(before, 15), 0.0)
    next_row = jnp.where(r0 + rc < L, _row_pick(after, 0), 0.0)
    rows = lax.broadcasted_iota(jnp.int32, cur.shape, 0)
    up = jnp.where(rows == 0, prev_row, pltpu.roll(cur, 1, 0))
    down = jnp.where(rows == rc - 1, next_row, pltpu.roll(cur, rc - 1, 0))
    return up, cur, down


def _silu_parts(g):
    s = _sigmoid(g)
    return g * s, s


def _conv_swiglu_fwd(name, h, cw):
    _, L, FF = h.shape
    rc = _row_tile(L, CONV_ROWS)

    def body(h_ref, cw_ref, o_ref):
        def chunk(ci, _):
            r0 = pl.multiple_of(ci * rc, rc)
            hc = []
            for half in range(2):
                up, cur, down = _shifted(h_ref.at[half], r0, rc, L)
                hc.append(up * cw_ref[half, 0:1, :] + cur * cw_ref[half, 1:2, :] + down * cw_ref[half, 2:3, :]
                          + cw_ref[half, 3:4, :])
            o_ref[pl.ds(r0, rc), :] = (_silu_parts(hc[1])[0] * hc[0]).astype(o_ref.dtype)
            return 0

        lax.fori_loop(0, L // rc, chunk, 0)

    return pl.pallas_call(body, name=name, grid=(FF // LANE,),
                          in_specs=[pl.BlockSpec((2, L, LANE), lambda j: (0, 0, j)),
                                    pl.BlockSpec((2, 8, LANE), lambda j: (0, 0, j))],
                          out_specs=pl.BlockSpec((L, LANE), lambda j: (0, j)),
                          out_shape=jax.ShapeDtypeStruct((L, FF), _ACT),
                          compiler_params=_params(("parallel",)))(h, cw)


def _conv_swiglu_bwd(name, h, cw, dact, rider=None):
    _, L, FF = h.shape
    rc = _row_tile(L, CONV_ROWS)

    def body(h_ref, cw_ref, da_ref, dh_ref, dcw_ref, dhc_ref):
        def chunk(ci, acc):
            r0 = pl.multiple_of(ci * rc, rc)
            taps, hc = [], []
            for half in range(2):
                t = _shifted(h_ref.at[half], r0, rc, L)
                taps.append(t)
                hc.append(t[0] * cw_ref[half, 0:1, :] + t[1] * cw_ref[half, 1:2, :] + t[2] * cw_ref[half, 2:3, :]
                          + cw_ref[half, 3:4, :])
            d = da_ref[pl.ds(r0, rc), :].astype(F32)
            act, s = _silu_parts(hc[1])
            dhc = (d * act, d * hc[0] * (s + act * (1.0 - s)))
            new = []
            for half in range(2):
                dhc_ref[half, pl.ds(r0, rc), :] = dhc[half]
                for k in range(3):
                    new.append(acc[4 * half + k] + jnp.sum(dhc[half] * taps[half][k], axis=0, keepdims=True))
                new.append(acc[4 * half + 3] + jnp.sum(dhc[half], axis=0, keepdims=True))
            return tuple(new)

        zero = jnp.zeros((1, LANE), F32)
        acc = lax.fori_loop(0, L // rc, chunk, (zero,) * 8)
        dcw_ref[...] = jnp.zeros_like(dcw_ref)
        for half in range(2):
            for k in range(4):
                dcw_ref[half, k:k + 1, :] = acc[4 * half + k]

        def chunk2(ci, _):
            r0 = pl.multiple_of(ci * rc, rc)
            for half in range(2):
                up, cur, down = _shifted(dhc_ref.at[half], r0, rc, L)
                dh_ref[half, pl.ds(r0, rc), :] = (down * cw_ref[half, 0:1, :] + cur * cw_ref[half, 1:2, :]
                                                  + up * cw_ref[half, 2:3, :]).astype(dh_ref.dtype)
            return 0

        lax.fori_loop(0, L // rc, chunk2, 0)

    return _pcall(body, name=name, grid=(FF // LANE,),
                  in_specs=[pl.BlockSpec((2, L, LANE), lambda j: (0, 0, j)),
                            pl.BlockSpec((2, 8, LANE), lambda j: (0, 0, j)),
                            pl.BlockSpec((L, LANE), lambda j: (0, j))],
                  out_specs=[pl.BlockSpec((2, L, LANE), lambda j: (0, 0, j)),
                             pl.BlockSpec((2, 8, LANE), lambda j: (0, 0, j))],
                  out_shape=[jax.ShapeDtypeStruct((2, L, FF), _ACT), jax.ShapeDtypeStruct((2, 8, FF), F32)],
                  scratch_shapes=[pltpu.VMEM((2, L, LANE), F32)], ins=(h, cw, dact), sem=("parallel",), rider=rider)


POOL_ROWS = 256
POOL_TILE = 256


def _pool_bands(transpose):
    i = jnp.arange(POOL_ROWS)[:, None]
    j = jnp.arange(POOL_ROWS + 2 * POOL_HALO)[None, :] - POOL_HALO
    bands = []
    for w in POOL_WINDOWS:
        lo, hi = (-(w // 2 - 1), w // 2) if transpose else (-(w // 2), w // 2 - 1)
        bands.append(((j - i >= lo) & (j - i <= hi)).astype(_ACT))
    return jnp.stack(bands)


def _pool_window(name, u, transpose, out_dtype):
    L, D = u.shape
    ng = len(POOL_WINDOWS)
    pc = D // ng
    tn = min(POOL_TILE, pc)
    rc = _row_tile(L, POOL_ROWS)
    bands = _pool_bands(transpose)
    if rc != POOL_ROWS:
        bands = bands[:, :rc, :rc + 2 * POOL_HALO]
    halo = POOL_HALO

    def body(u_ref, band_ref, o_ref, hi_ref, lo_ref):
        g = (pl.program_id(0) * tn) // pc
        half = jnp.zeros((1, 1), jnp.int32)
        for k, w in enumerate(POOL_WINDOWS):
            half = jnp.where(g == k, w // 2, half)
        zeros = jnp.zeros((halo, tn), _ACT)
        for ref in (hi_ref, lo_ref):
            ref[0:halo, :] = zeros
            ref[halo + L:2 * halo + L, :] = zeros

        def inv_count(r0):
            t = r0 + lax.broadcasted_iota(jnp.int32, (rc, tn), 0)
            lo = jnp.clip(t - half, 0, L - 1)
            hi = jnp.clip(t + half - 1, 0, L - 1)
            return 1.0 / (hi - lo + 1).astype(F32)

        def split(ci, _):
            r0 = pl.multiple_of(ci * rc, rc)
            v = u_ref[pl.ds(r0, rc), :].astype(F32)
            if transpose:
                v = v * inv_count(r0)
            hi = v.astype(_ACT)
            dst = pl.ds(pl.multiple_of(r0 + halo, halo), rc)
            hi_ref[dst, :] = hi
            lo_ref[dst, :] = (v - hi.astype(F32)).astype(_ACT)
            return 0

        lax.fori_loop(0, L // rc, split, 0)
        band = band_ref[...]

        def chunk(ci, _):
            r0 = pl.multiple_of(ci * rc, rc)
            win = pl.ds(r0, rc + 2 * halo)
            s = (jnp.dot(band, hi_ref[win, :], preferred_element_type=F32)
                 + jnp.dot(band, lo_ref[win, :], preferred_element_type=F32))
            if not transpose:
                s = s * inv_count(r0)
            o_ref[pl.ds(r0, rc), :] = (s - u_ref[pl.ds(r0, rc), :].astype(F32)).astype(o_ref.dtype)
            return 0

        lax.fori_loop(0, L // rc, chunk, 0)

    return pl.pallas_call(body, name=name, grid=(D // tn,),
                          in_specs=[pl.BlockSpec((L, tn), lambda j: (0, j)),
                                    pl.BlockSpec((None, rc, rc + 2 * halo), lambda j: ((j * tn) // pc, 0, 0))],
                          out_specs=pl.BlockSpec((L, tn), lambda j: (0, j)),
                          out_shape=jax.ShapeDtypeStruct((L, D), out_dtype),
                          scratch_shapes=[pltpu.VMEM((L + 2 * halo, tn), _ACT), pltpu.VMEM((L + 2 * halo, tn), _ACT)],
                          compiler_params=_params(("parallel",)))(u, bands)


S5_ROWS = 512


def _slab(start):
    return pl.ds(start if isinstance(start, int) else pl.multiple_of(start, NSEG), NSEG)


def _cmul(ar, ai, br, bi):
    return ar * br - ai * bi, ar * bi + ai * br


def _cpow(ar, ai, n):
    rr, ri = None, None
    br, bi = ar, ai
    while n:
        if n & 1:
            rr, ri = (br, bi) if rr is None else _cmul(rr, ri, br, bi)
        n >>= 1
        if n:
            br, bi = _cmul(br, bi, br, bi)
    return rr, ri


def _pow_table(pw_ref, ar, ai, n):
    W = ar.shape[1]
    pr, pi = ar, ai
    for r in range(NSEG):
        pw_ref[0, r:r + 1, :] = pr
        pw_ref[1, r:r + 1, :] = pi
        if r < NSEG - 1:
            pr, pi = _cmul(pr, pi, ar, ai)
    a8r, a8i = (jnp.broadcast_to(v, (NSEG, W)) for v in _cpow(ar, ai, NSEG))

    def step(k, carry):
        nr, ni = _cmul(carry[0], carry[1], a8r, a8i)
        pw_ref[0, _slab(k * NSEG), :] = nr
        pw_ref[1, _slab(k * NSEG), :] = ni
        return nr, ni

    lax.fori_loop(1, n // NSEG, step, (pw_ref[0, 0:NSEG, :], pw_ref[1, 0:NSEG, :]))


def _seg_scan(sr_ref, si_ref, tmp_ref, pw_ref, row0, n, ar, ai, h0, rev, conj=False, pair_with=None):
    W = ar.shape[1]
    arb, aib = jnp.broadcast_to(ar, (NSEG, W)), jnp.broadcast_to(ai, (NSEG, W))

    def rows(s):
        t = (n - 1 - s) if rev else s
        return _slab(row0 + t * NSEG)

    def step(s, carry):
        hr, hi = carry
        sl = rows(s)
        nr = arb * hr - aib * hi + sr_ref[sl, :]
        ni = arb * hi + aib * hr + si_ref[sl, :]
        sr_ref[sl, :] = nr
        si_ref[sl, :] = ni
        return nr, ni

    zero = jnp.zeros((NSEG, W), F32)
    fr, fi = lax.fori_loop(0, n, step, (zero, zero), unroll=2)
    tmp_ref[0] = fr
    tmp_ref[1] = fi
    anr, ani = _cpow(ar, ai, n)
    cr, ci = h0
    for j in (range(NSEG - 1, -1, -1) if rev else range(NSEG)):
        tmp_ref[2, j:j + 1, :] = cr
        tmp_ref[3, j:j + 1, :] = ci
        pr, pi = _cmul(anr, ani, cr, ci)
        cr, ci = tmp_ref[0, j:j + 1, :] + pr, tmp_ref[1, j:j + 1, :] + pi
    cmr, cmi = tmp_ref[2], tmp_ref[3]

    def fix(k, acc):
        for r in range(NSEG):
            row = pl.ds(pl.multiple_of(k * NSEG, NSEG) + r, 1)
            pr = jnp.broadcast_to(pw_ref[0, row, :], (NSEG, W))
            pi = jnp.broadcast_to(pw_ref[1, row, :], (NSEG, W))
            if conj:
                pi = -pi
            s = k * NSEG + r
            sl = rows(s)
            gr = sr_ref[sl, :] + (pr * cmr - pi * cmi)
            gi = si_ref[sl, :] + (pr * cmi + pi * cmr)
            sr_ref[sl, :] = gr
            si_ref[sl, :] = gi
            if pair_with is not None:
                h_r, h_i, (hcr, hci) = pair_with
                prev = rows(jnp.minimum(s + 1, n - 1))
                first = s == n - 1
                hpr = jnp.where(first, hcr, h_r[prev, :])
                hpi = jnp.where(first, hci, h_i[prev, :])
                acc = (acc[0] + hpr * gr + hpi * gi, acc[1] + hpr * gi - hpi * gr)
        return acc

    zero_acc = (zero, zero) if pair_with is not None else 0
    total = lax.fori_loop(0, n // NSEG, fix, zero_acc)
    return (cr, ci), (cmr, cmi), (total if pair_with is not None else None)


def _gelu_tanh(y):
    k = math.sqrt(2.0 / math.pi)
    t = jnp.tanh(k * (y + 0.044715 * y * y * y))
    return 0.5 * y * (1.0 + t), t


def _s5_chunks(L):
    rc = _row_tile(L, S5_ROWS)
    return [(r, rc) for r in range(0, L, rc)]


def _s5_project(u_ref, uc_ref, bre, bim, sr_ref, si_ref, L, LC):
    for ref, base, n in ((u_ref, 0, L), (uc_ref, L, LC)):
        for r, rc in _s5_chunks(n):
            ub = ref[r:r + rc, :].astype(_ACT)
            sr_ref[base + r:base + r + rc, :] = jnp.dot(ub, bre, preferred_element_type=F32)
            si_ref[base + r:base + r + rc, :] = jnp.dot(ub, bim, preferred_element_type=F32)


def _s5_states(sr_ref, si_ref, tmp_ref, pw_ref, ar, ai, L, LC, rev):
    W = ar.shape[1]
    zero = (jnp.zeros((1, W), F32), jnp.zeros((1, W), F32))
    hctx, cm_ctx, _ = _seg_scan(sr_ref, si_ref, tmp_ref, pw_ref, L, LC // NSEG, ar, ai, zero, rev)
    _, cm_lat, _ = _seg_scan(sr_ref, si_ref, tmp_ref, pw_ref, 0, L // NSEG, ar, ai, hctx, rev)
    return cm_lat, cm_ctx


def _s5_fwd(u, uc, bblk, cblk, apar, dsk, rider=None):
    L, D = u.shape
    LC = uc.shape[0]
    NT, W, TC = D // S5_TILE_CH, S5_TILE_W, S5_TILE_CH

    def body(u_ref, uc_ref, b_ref, c_ref, a_ref, d_ref, y_ref, z_ref, sr_ref, si_ref, tmp_ref, pw_ref):
        for r, rc in _s5_chunks(L):
            y_ref[r:r + rc, :] = u_ref[r:r + rc, :].astype(F32) * d_ref[0:1, :]
        for d in range(2):
            ar, ai = a_ref[2 * d:2 * d + 1, :], a_ref[2 * d + 1:2 * d + 2, :]
            _pow_table(pw_ref, ar, ai, L // NSEG)
            _s5_project(u_ref, uc_ref, b_ref[2 * d], b_ref[2 * d + 1], sr_ref, si_ref, L, LC)
            _s5_states(sr_ref, si_ref, tmp_ref, pw_ref, ar, ai, L, LC, rev=(d == 1))
            cre, cim = c_ref[2 * d], c_ref[2 * d + 1]
            for r, rc in _s5_chunks(L):
                y_ref[r:r + rc, :] += (jnp.dot(sr_ref[r:r + rc, :].astype(_ACT), cre, preferred_element_type=F32)
                                       - jnp.dot(si_ref[r:r + rc, :].astype(_ACT), cim, preferred_element_type=F32))
        for r, rc in _s5_chunks(L):
            z_ref[r:r + rc, :] = _gelu_tanh(y_ref[r:r + rc, :])[0].astype(z_ref.dtype)

    col = lambda n: pl.BlockSpec((n, TC), lambda j: (0, j))
    return _pcall(
        body, name="s5_fwd", grid=(NT,), ins=(u, uc, bblk, cblk, apar, dsk), sem=("parallel",), rider=rider,
        in_specs=[col(L), col(LC),
                  pl.BlockSpec((None, 4, TC, W), lambda j: (j, 0, 0, 0)),
                  pl.BlockSpec((None, 4, W, TC), lambda j: (j, 0, 0, 0)),
                  pl.BlockSpec((None, 8, W), lambda j: (j, 0, 0)),
                  pl.BlockSpec((8, TC), lambda j: (0, j))],
        out_specs=[col(L), col(L)],
        out_shape=[jax.ShapeDtypeStruct((L, D), F32), jax.ShapeDtypeStruct((L, D), _ACT)],
        scratch_shapes=[pltpu.VMEM((L + LC, W), F32), pltpu.VMEM((L + LC, W), F32), pltpu.VMEM((4, NSEG, W), F32),
                        pltpu.VMEM((2, L // NSEG, W), F32)])


def _s5_bwd(u, uc, dz, y, bblk, cblk, apar, dsk, rider=None):
    L, D = u.shape
    LC = uc.shape[0]
    NT, W, TC = D // S5_TILE_CH, S5_TILE_W, S5_TILE_CH
    nl, nc = L // NSEG, LC // NSEG

    def body(u_ref, uc_ref, dz_ref, y_ref, b_ref, c_ref, a_ref, d_ref,
             du_ref, duc_ref, db_ref, dc_ref, da_ref, dd_ref,
             hr_ref, hi_ref, gr_ref, gi_ref, dy_ref, tmp_ref, pw_ref):
        ddacc = jnp.zeros((1, TC), F32)
        for r, rc in _s5_chunks(L):
            yv = y_ref[r:r + rc, :]
            g, t = _gelu_tanh(yv)
            k = math.sqrt(2.0 / math.pi)
            dg = 0.5 * (1.0 + t) + 0.5 * yv * (1.0 - t * t) * k * (1.0 + 3 * 0.044715 * yv * yv)
            dy = dz_ref[r:r + rc, :].astype(F32) * dg
            uv = u_ref[r:r + rc, :].astype(F32)
            ddacc = ddacc + jnp.sum(dy * uv, axis=0, keepdims=True)
            du_ref[r:r + rc, :] = dy * d_ref[0:1, :]
            dy_ref[r:r + rc, :] = dy.astype(dy_ref.dtype)
        dd_ref[...] = jnp.zeros_like(dd_ref)
        dd_ref[0:1, :] = ddacc
        duc_ref[...] = jnp.zeros_like(duc_ref)
        da_ref[...] = jnp.zeros_like(da_ref)
        nt = (((1,), (1,)), ((), ()))
        tn = (((0,), (0,)), ((), ()))
        for d in range(2):
            rev = d == 1
            ar, ai = a_ref[2 * d:2 * d + 1, :], a_ref[2 * d + 1:2 * d + 2, :]
            bre, bim = b_ref[2 * d], b_ref[2 * d + 1]
            cre, cim = c_ref[2 * d], c_ref[2 * d + 1]
            _pow_table(pw_ref, ar, ai, nl)
            _s5_project(u_ref, uc_ref, bre, bim, hr_ref, hi_ref, L, LC)
            cm_lat, cm_ctx = _s5_states(hr_ref, hi_ref, tmp_ref, pw_ref, ar, ai, L, LC, rev)
            cml_r, cml_i, cmc_r, cmc_i = cm_lat[0], cm_lat[1], cm_ctx[0], cm_ctx[1]
            dcr = jnp.zeros((TC, W), F32)
            dci = jnp.zeros((TC, W), F32)
            for r, rc in _s5_chunks(L):
                dyb = dy_ref[r:r + rc, :]
                gr_ref[r:r + rc, :] = lax.dot_general(dyb, cre, nt, preferred_element_type=F32)
                gi_ref[r:r + rc, :] = -lax.dot_general(dyb, cim, nt, preferred_element_type=F32)
                dcr = dcr + lax.dot_general(dyb, hr_ref[r:r + rc, :].astype(_ACT), tn, preferred_element_type=F32)
                dci = dci - lax.dot_general(dyb, hi_ref[r:r + rc, :].astype(_ACT), tn, preferred_element_type=F32)
            dc_ref[2 * d] = dcr
            dc_ref[2 * d + 1] = dci
            gr_ref[L:L + LC, :] = jnp.zeros((LC, W), F32)
            gi_ref[L:L + LC, :] = jnp.zeros((LC, W), F32)
            zero = (jnp.zeros((1, W), F32), jnp.zeros((1, W), F32))
            glat, _, (lr, li) = _seg_scan(gr_ref, gi_ref, tmp_ref, pw_ref, 0, nl, ar, -ai, zero, not rev, conj=True,
                                          pair_with=(hr_ref, hi_ref, (cml_r, cml_i)))
            _, _, (qr, qi) = _seg_scan(gr_ref, gi_ref, tmp_ref, pw_ref, L, nc, ar, -ai, glat, not rev, conj=True,
                                       pair_with=(hr_ref, hi_ref, (cmc_r, cmc_i)))
            da_ref[2 * d:2 * d + 1, :] = jnp.sum(lr + qr, axis=0, keepdims=True)
            da_ref[2 * d + 1:2 * d + 2, :] = jnp.sum(li + qi, axis=0, keepdims=True)
            dbr = jnp.zeros((TC, W), F32)
            dbi = jnp.zeros((TC, W), F32)
            for ref, oref, base, n in ((u_ref, du_ref, 0, L), (uc_ref, duc_ref, L, LC)):
                for r, rc in _s5_chunks(n):
                    ub = ref[r:r + rc, :].astype(_ACT)
                    gr = gr_ref[base + r:base + r + rc, :].astype(_ACT)
                    gi = gi_ref[base + r:base + r + rc, :].astype(_ACT)
                    dbr = dbr + lax.dot_general(ub, gr, tn, preferred_element_type=F32)
                    dbi = dbi + lax.dot_general(ub, gi, tn, preferred_element_type=F32)
                    oref[r:r + rc, :] += (lax.dot_general(gr, bre, nt, preferred_element_type=F32)
                                          + lax.dot_general(gi, bim, nt, preferred_element_type=F32))
            db_ref[2 * d] = dbr
            db_ref[2 * d + 1] = dbi

    col = lambda n: pl.BlockSpec((n, TC), lambda j: (0, j))
    bspec = pl.BlockSpec((None, 4, TC, W), lambda j: (j, 0, 0, 0))
    cspec = pl.BlockSpec((None, 4, W, TC), lambda j: (j, 0, 0, 0))
    aspec = pl.BlockSpec((None, 8, W), lambda j: (j, 0, 0))
    return _pcall(
        body, name="s5_bwd", grid=(NT,), ins=(u, uc, dz, y, bblk, cblk, apar, dsk), sem=("parallel",),
        vmem=VMEM_LIMIT_BIG, rider=rider,
        in_specs=[col(L), col(LC), col(L), col(L), bspec, cspec, aspec, pl.BlockSpec((8, TC), lambda j: (0, j))],
        out_specs=[col(L), col(LC), bspec, bspec, aspec, pl.BlockSpec((None, 8, TC), lambda j: (j, 0, 0))],
        out_shape=[jax.ShapeDtypeStruct((L, D), F32), jax.ShapeDtypeStruct((LC, D), F32),
                   jax.ShapeDtypeStruct((NT, 4, TC, W), F32), jax.ShapeDtypeStruct((NT, 4, TC, W), F32),
                   jax.ShapeDtypeStruct((NT, 8, W), F32), jax.ShapeDtypeStruct((NT, 8, TC), F32)],
        scratch_shapes=[pltpu.VMEM((L + LC, W), F32), pltpu.VMEM((L + LC, W), F32),
                        pltpu.VMEM((L + LC, W), F32), pltpu.VMEM((L + LC, W), F32),
                        pltpu.VMEM((L, TC), _ACT), pltpu.VMEM((4, NSEG, W), F32), pltpu.VMEM((2, nl, W), F32)])


ADA_ROWS = 16


def _silu_rows(c_ref):
    c = c_ref[...]
    return c * _sigmoid(c)


def _ada_fwd(cmat, ada_w, ada_b):
    nl, D, n = ada_w.shape
    tn = _row_tile(n, 512)

    def body(c_ref, w_ref, b_ref, o_ref):
        a = _silu_rows(c_ref).astype(_ACT)
        o_ref[...] = jnp.dot(a, w_ref[...].astype(_ACT), preferred_element_type=F32) + b_ref[...]

    return pl.pallas_call(body, name="ada_fwd", grid=(nl, n // tn),
                          in_specs=[pl.BlockSpec((ADA_ROWS, D), lambda l, j: (0, 0)),
                                    pl.BlockSpec((None, D, tn), lambda l, j: (l, 0, j)),
                                    pl.BlockSpec((None, 1, tn), lambda l, j: (l, 0, j))],
                          out_specs=pl.BlockSpec((None, ADA_ROWS, tn), lambda l, j: (l, 0, j)),
                          out_shape=jax.ShapeDtypeStruct((nl, ADA_ROWS, n), F32),
                          compiler_params=_params(("parallel", "parallel")))(cmat, ada_w, ada_b)


def _ada_bwd(cmat, ada_w, dm):
    nl, D, n = ada_w.shape
    tn = _row_tile(n, 512)
    nj = n // tn

    def body(c_ref, w_ref, dm_ref, dw_ref, dc_ref):
        c = c_ref[...]
        s = _sigmoid(c)
        a = (c * s).astype(_ACT)
        dmb = dm_ref[...].astype(_ACT)
        dw_ref[...] = lax.dot_general(a, dmb, (((0,), (0,)), ((), ())), preferred_element_type=F32)
        part = lax.dot_general(dmb, w_ref[...].astype(_ACT), (((1,), (1,)), ((), ())), preferred_element_type=F32)
        part = part * (s * (1.0 + c * (1.0 - s)))

        @pl.when(pl.program_id(1) == 0)
        def _():
            dc_ref[...] = part

        @pl.when(pl.program_id(1) > 0)
        def _():
            dc_ref[...] += part

    return pl.pallas_call(body, name="ada_bwd", grid=(nl, nj),
                          in_specs=[pl.BlockSpec((ADA_ROWS, D), lambda l, j: (0, 0)),
                                    pl.BlockSpec((None, D, tn), lambda l, j: (l, 0, j)),
                                    pl.BlockSpec((None, ADA_ROWS, tn), lambda l, j: (l, 0, j))],
                          out_specs=[pl.BlockSpec((None, D, tn), lambda l, j: (l, 0, j)),
                                     pl.BlockSpec((None, ADA_ROWS, D), lambda l, j: (l, 0, 0))],
                          out_shape=[jax.ShapeDtypeStruct((nl, D, n), F32), jax.ShapeDtypeStruct((nl, ADA_ROWS, D), F32)],
                          compiler_params=_params(("parallel", "arbitrary")))(cmat, ada_w, dm)


def _adamw(name, gparts, w, m, v):
    nl, R, C = w.shape
    gparts = [g if isinstance(g, tuple) else (g, 0) for g in gparts]
    n = gparts[0][0].shape[0]
    tr = R
    part_bytes = nl * n * C * gparts[0][0].dtype.itemsize * 2
    for cand in (4096, 2048, 1024, 512, 256, 128, 64, 32, 16, 8):
        if R % cand == 0 and cand * max(C, LANE) * 4 <= 2 * 1024 * 1024 and cand * part_bytes <= VMEM_LIMIT // 2:
            tr = cand
            break
    nt = R // tr
    bc1 = 1.0 - ADAM_B1 ** ADAM_STEP
    bc2 = 1.0 - ADAM_B2 ** ADAM_STEP

    def body(*refs):
        g_refs = refs[:nl]
        w_ref, m_ref, v_ref, go_ref, d_ref, mo_ref, vo_ref = refs[nl:]
        for l in range(nl):
            @pl.when(pl.program_id(0) == l)
            def _(g_ref=g_refs[l]):
                g = g_ref[0].astype(F32)
                for j in range(1, n):
                    g = g + g_ref[j].astype(F32)
                m2 = ADAM_B1 * m_ref[...] + (1.0 - ADAM_B1) * g
                v2 = ADAM_B2 * v_ref[...] + (1.0 - ADAM_B2) * (g * g)
                go_ref[...] = g
                mo_ref[...] = m2
                vo_ref[...] = v2
                d_ref[...] = -ADAM_LR * ((m2 / bc1) / (jnp.sqrt(v2 / bc2) + ADAM_EPS) + ADAM_WD * w_ref[...])

    def gspec(l):
        first = gparts[l][1] // tr
        return pl.BlockSpec((n, tr, C),
                            lambda lyr, i: (0, first + jnp.where(lyr < l, 0, jnp.where(lyr > l, nt - 1, i)), 0))

    row = pl.BlockSpec((None, tr, C), lambda lyr, i: (lyr, i, 0))
    out = jax.ShapeDtypeStruct((nl, R, C), F32)
    return _pcall(body, name=name, grid=(nl, nt), in_specs=[gspec(l) for l in range(nl)] + [row, row, row],
                  out_specs=[row, row, row, row], out_shape=[out, out, out, out],
                  ins=(*[g for g, _ in gparts], w, m, v), sem=("arbitrary", "arbitrary"))


def _sum_parts(name, parts):
    n, R, C = parts.shape

    def body(p_ref, o_ref):
        s = p_ref[0]
        for j in range(1, n):
            s = s + p_ref[j]
        o_ref[...] = s

    return pl.pallas_call(body, name=name, out_shape=jax.ShapeDtypeStruct((R, C), F32),
                          compiler_params=_params(None))(parts)


def _discretize(lam_re, lam_im, log_step, b_re, b_im):
    dt = jnp.exp(log_step)[:, None]
    mag = jnp.exp(lam_re * dt)
    abar_re = mag * jnp.cos(lam_im * dt)
    abar_im = mag * jnp.sin(lam_im * dt)
    nr, ni = abar_re - 1.0, abar_im
    den = lam_re * lam_re + lam_im * lam_im
    fr = (nr * lam_re + ni * lam_im) / den
    fi = (ni * lam_re - nr * lam_im) / den
    bbar_re = fr[..., None] * b_re - fi[..., None] * b_im
    bbar_im = fr[..., None] * b_im + fi[..., None] * b_re
    return abar_re, abar_im, bbar_re, bbar_im


def _s5_pack(abar, bbar, cmat):
    G = abar[0][0].shape[0]
    NT = G // S5_TILE_G
    eye = jnp.eye(S5_TILE_G, dtype=F32)
    a4 = jnp.stack([abar[d][r] for d in range(2) for r in range(2)]).reshape(4, NT, S5_TILE_W).transpose(1, 0, 2)
    apar = jnp.concatenate([a4, jnp.zeros((NT, 4, S5_TILE_W), F32)], axis=1)
    b4 = jnp.stack([bbar[d][r] for d in range(2) for r in range(2)]).reshape(4, NT, S5_TILE_G, S5_P, S5_CH)
    bblk = jnp.einsum("kjgpc,gh->jkgchp", b4, eye).reshape(NT, 4, S5_TILE_CH, S5_TILE_W)
    c4 = jnp.stack([cmat[d][r] for d in range(2) for r in range(2)]).reshape(4, NT, S5_TILE_G, S5_CH, S5_P)
    cblk = jnp.einsum("kjgcp,gh->jkgphc", c4, eye).reshape(NT, 4, S5_TILE_W, S5_TILE_CH)
    return apar, bblk.astype(_ACT), cblk.astype(_ACT)


def _s5_unpack(dapar, dbblk, dcblk, G):
    NT = G // S5_TILE_G
    eye = jnp.eye(S5_TILE_G, dtype=F32)
    da = dapar[:, :4, :].reshape(NT, 2, 2, S5_TILE_G, S5_P).transpose(1, 2, 0, 3, 4).reshape(2, 2, G, S5_P)
    db = jnp.einsum("jkgchp,gh->kjgpc", dbblk.reshape(NT, 4, S5_TILE_G, S5_CH, S5_TILE_G, S5_P), eye,
                    precision=lax.Precision.HIGHEST)
    dc = jnp.einsum("jkgchp,gh->kjgcp", dcblk.reshape(NT, 4, S5_TILE_G, S5_CH, S5_TILE_G, S5_P), eye,
                    precision=lax.Precision.HIGHEST)
    return da, db.reshape(2, 2, G, S5_P, S5_CH), dc.reshape(2, 2, G, S5_CH, S5_P)


def _to_segments(a):
    L, D = a.shape
    return a.reshape(NSEG, L // NSEG, D).transpose(1, 0, 2).reshape(L, D)


def _from_segments(a):
    L, D = a.shape
    return a.reshape(L // NSEG, NSEG, D).transpose(1, 0, 2).reshape(L, D)


def _pos_emb(n_tokens, dim):
    rows = n_tokens // GRID_W
    quarter = dim // 4
    omega = 1.0 / (POS_BASE ** (jnp.arange(quarter, dtype=F32) / quarter))

    def enc(p):
        ang = p[:, None] * omega[None, :]
        return jnp.concatenate([jnp.sin(ang), jnp.cos(ang)], axis=-1)

    rtab = enc(jnp.arange(rows, dtype=F32))
    ctab = enc(jnp.arange(GRID_W, dtype=F32))
    return jnp.concatenate([jnp.repeat(rtab, GRID_W, axis=0), jnp.tile(ctab, (rows, 1))], axis=-1)


def _vec(D, **rows):
    names = {"gpost": V_GPOST, "gate": V_GATE, "yscale": V_YSCALE, "gpre": V_GPRE, "shift": V_SHIFT, "scale": V_SCALE}
    out = [jnp.zeros((D,), F32)] * 8
    out[V_YSCALE] = jnp.ones((D,), F32)
    for k, v in rows.items():
        out[names[k]] = v.reshape(D).astype(F32)
    return jnp.stack(out)


def _row0(v, D):
    return jnp.concatenate([v.reshape(1, D).astype(F32), jnp.zeros((7, D), F32)], axis=0)


def _my_block(full, axis, n_local):
    return lax.dynamic_slice_in_dim(full, _my_index() * n_local, n_local, axis)


def kernel(x, c, ctx, c_ctx, ada_w, ada_b, norm_g, s5_lam_re, s5_lam_im, s5_log_step, s5_b_re, s5_b_im, s5_c_re, s5_c_im, s5_d, s5_glu_w, pool_w, pool_scale, ffn_up, ffn_conv, ffn_conv_b, ffn_down, loss_target, m_c_ctx, m_ada_w, m_ada_b, m_norm_g, m_s5_lam_re, m_s5_lam_im, m_s5_log_step, m_s5_b_re, m_s5_b_im, m_s5_c_re, m_s5_c_im, m_s5_d, m_s5_glu_w, m_pool_w, m_pool_scale, m_ffn_up, m_ffn_conv, m_ffn_conv_b, m_ffn_down, v_c_ctx, v_ada_w, v_ada_b, v_norm_g, v_s5_lam_re, v_s5_lam_im, v_s5_log_step, v_s5_b_re, v_s5_b_im, v_s5_c_re, v_s5_c_im, v_s5_d, v_s5_glu_w, v_pool_w, v_pool_scale, v_ffn_up, v_ffn_conv, v_ffn_conv_b, v_ffn_down):
    L, D = x.shape[1], x.shape[2]
    LC = ctx.shape[1]
    G = s5_lam_re.shape[2]
    n_ada = ada_w.shape[2]
    nb_up = ffn_up.shape[2]
    r_down = ffn_down.shape[1]
    FF = N_DEV * r_down
    n_pool = len(POOL_WINDOWS)
    pc = D // n_pool
    pr = pool_w.shape[2]
    ng_loc = norm_g.shape[2]
    me = _my_index()
    axes = ("x", "y", "c")

    up_b = [ffn_up[i].astype(_ACT) for i in range(2)]
    down_b = [ffn_down[i].astype(_ACT) for i in range(2)]
    glu_b = s5_glu_w[0].astype(_ACT)
    pool_b = pool_w[0].reshape(n_pool * pr, pc).astype(_ACT)

    small_loc = jnp.concatenate([c.reshape(-1), norm_g.reshape(-1), pool_scale.reshape(-1), ffn_conv.reshape(-1)])
    n_small = small_loc.shape[0]
    small_g, = _exchange([[jnp.pad(small_loc, (0, (-n_small) % LANE)).reshape(1, -1)]], mode="gather", name="gather_small")
    small_g = small_g.reshape(N_DEV, -1)
    o = 0
    c_all = small_g[:, o:o + D]
    o += D
    ng_all = small_g[:, o:o + 8 * ng_loc].reshape(N_DEV, 2, 4, ng_loc).transpose(1, 2, 0, 3).reshape(2, 4, D)
    o += 8 * ng_loc
    pscale_all = small_g[:, o:o + ng_loc].reshape(D)
    o += ng_loc
    conv_all = small_g[:, o:o + 6 * nb_up].reshape(N_DEV, 2, 3, nb_up).transpose(1, 2, 0, 3).reshape(2, 3, 2 * FF)

    cmat = jnp.concatenate([c_all, c_ctx.reshape(1, D), jnp.zeros((ADA_ROWS - N_DEV - 1, D), F32)], axis=0)
    ada_b_loc = _my_block(ada_b, 1, n_ada).reshape(2, 1, n_ada)
    mods_loc = _ada_fwd(cmat, ada_w, ada_b_loc)
    mods_g, = _exchange([[mods_loc]], mode="gather", name="gather_mods")
    mods_rows = mods_g.reshape(N_DEV, 2, ADA_ROWS, n_ada).transpose(1, 2, 0, 3).reshape(2, ADA_ROWS, 6, D)
    mod = lax.dynamic_index_in_dim(mods_rows, me, axis=1, keepdims=False)
    mod_c = mods_rows[0, N_DEV]

    def disc_all(lr, li, ls, br, bi):
        return [_discretize(lr[d], li[d], ls[d], br[d], bi[d]) for d in range(2)]

    disc, disc_vjp = jax.vjp(disc_all, s5_lam_re[0], s5_lam_im[0], s5_log_step[0], s5_b_re[0], s5_b_im[0])
    apar, bblk, cblk = _s5_pack([(disc[d][0], disc[d][1]) for d in range(2)],
                                [(disc[d][2], disc[d][3]) for d in range(2)],
                                [(s5_c_re[0, d], s5_c_im[0, d]) for d in range(2)])
    dsk = _row0(s5_d[0], D)
    cw = []
    for i in range(2):
        taps = conv_all[i].reshape(3, 2, FF).transpose(1, 0, 2)
        cw.append(jnp.concatenate([taps, ffn_conv_b[i].reshape(2, 1, FF), jnp.zeros((2, 4, FF), F32)], axis=1))

    vecs = {
        "b0": _vec(D, gpre=ng_all[0, 0], shift=mod[0, 0], scale=mod[0, 1]),
        "c0": _vec(D, gpre=ng_all[0, 0], shift=mod_c[0], scale=mod_c[1]),
        "b1": _vec(D, gpost=ng_all[0, 1], gate=mod[0, 2], gpre=ng_all[0, 2], shift=mod[0, 3], scale=mod[0, 4]),
        "b2": _vec(D, gpost=ng_all[0, 3], gate=mod[0, 5], gpre=ng_all[1, 0], shift=mod[1, 0], scale=mod[1, 1]),
        "b3": _vec(D, gpost=ng_all[1, 1], gate=mod[1, 2], yscale=pscale_all, gpre=ng_all[1, 2], shift=mod[1, 3],
                   scale=mod[1, 4]),
        "b4": _vec(D, gpost=ng_all[1, 3], gate=mod[1, 5]),
    }

    x0, u0 = _rows_fwd("rows_fwd_b0", x[0], _pos_emb(L, D), vecs["b0"], add=True, u_dtype=_ACT)
    uc, = _rows_fwd("rows_fwd_ctx", ctx[0], None, vecs["c0"], want_x=False, u_dtype=_ACT)
    u0s, ucs = _to_segments(u0), _to_segments(uc)
    (y_s5, z_s5), (glu_g, up_g0, down_g0) = _s5_fwd(u0s, ucs, bblk, cblk, apar, dsk,
                                                    rider=([[glu_b], [up_b[0]], [down_b[0]]], "gather2"))
    vg = _colblock_fwd("glu_fwd_mm", z_s5, glu_g, 0, _ACT)
    mix0 = _from_segments(_glu_fwd("glu_fwd", vg))
    x1, un0 = _rows_fwd("rows_fwd_b1", x0, mix0, vecs["b1"], u_dtype=_ACT)
    h0, (up_g1,) = _colblock_fwd("ffn0_up", un0, up_g0, 0, _ACT, rider=([[up_b[1]]], "gather2"))
    act0 = _conv_swiglu_fwd("ffn0_conv", h0, cw[0])
    f0, (down_g1, pool_g) = _rowblock_fwd("ffn0_down", act0, down_g0, 0, rider=([[down_b[1]], [pool_b]], "gather2"))
    pool_full = pool_g.reshape(N_DEV, n_pool, pr, pc).transpose(1, 0, 2, 3).reshape(n_pool, pc, pc)
    x2, u1 = _rows_fwd("rows_fwd_b2", x1, f0, vecs["b2"], u_dtype=F32)
    p1 = _pool_window("pool_fwd", u1, False, _ACT)
    ypre1 = _group_mm("pool_fwd_mm", p1, pool_full, "nn", F32)
    x3, un1 = _rows_fwd("rows_fwd_b3", x2, ypre1, vecs["b3"], u_dtype=_ACT)
    h1 = _colblock_fwd("ffn1_up", un1, up_g1, 0, _ACT)
    act1 = _conv_swiglu_fwd("ffn1_conv", h1, cw[1])
    f1 = _rowblock_fwd("ffn1_down", act1, down_g1, 0)
    dx4, loss_blk = _rows_fwd("rows_fwd_b4", x3, f1, vecs["b4"], target=loss_target[0])
    loss = lax.psum(loss_blk[0, 0], axes)

    df1, red4 = _rows_bwd("rows_bwd_b4", dx4, None, None, f1, vecs["b4"], dy_dtype=_ACT, want_dx=False)
    dact1 = _rowblock_dgrad("ffn1_down_dgrad", df1, down_g1, 0)
    ddown1 = _rowblock_wgrad("ffn1_down_wgrad", act1, df1)
    ddown1 = ddown1.reshape(N_DEV, r_down, D)
    (dh1, dcw1), (gp_down1a,) = _conv_swiglu_bwd("ffn1_conv_bwd", h1, cw[1], dact1,
                                                rider=([[(ddown1, (0, r_down // 2))]], "scatter"))
    dun1, (gp_down1b,) = _colblock_dgrad("ffn1_up_dgrad", dh1, up_g1, 0, F32,
                                         rider=([[(ddown1, (r_down // 2, r_down // 2))]], "scatter"))
    dup1 = _colblock_wgrad("ffn1_up_wgrad", un1, dh1)
    dx3, dy3, red3 = _rows_bwd("rows_bwd_b3", dx4, dun1, x3, ypre1, vecs["b3"])
    dypre1, red_ps = _colscale_bwd("pool_scale_bwd", dy3, ypre1, _row0(pscale_all, D))
    dp1 = _group_mm("pool_dgrad", dypre1, pool_full, "nt", F32)
    dpool = _group_wgrad("pool_wgrad", p1, dypre1, n_pool)
    du1 = _pool_window("pool_bwd", dp1, True, F32)
    dx2, df0, red2 = _rows_bwd("rows_bwd_b2", dx3, du1, x2, f0, vecs["b2"], dy_dtype=_ACT)
    dact0 = _rowblock_dgrad("ffn0_down_dgrad", df0, down_g0, 0)
    ddown0 = _rowblock_wgrad("ffn0_down_wgrad", act0, df0)
    ddown0 = ddown0.reshape(N_DEV, r_down, D)
    (dh0, dcw0), (gp_down0a,) = _conv_swiglu_bwd("ffn0_conv_bwd", h0, cw[0], dact0,
                                                 rider=([[(ddown0, (0, r_down // 2))]], "scatter"))
    dun0, (gp_up1a,) = _colblock_dgrad("ffn0_up_dgrad", dh0, up_g0, 0, F32,
                                       rider=([[(dup1, (0, D // 2))]], "scatter"))
    dup0, (gp_up1b,) = _colblock_wgrad("ffn0_up_wgrad", un0, dh0, rider=([[(dup1, (D // 2, D // 4))]], "scatter"))
    dx1, dmix0, red1 = _rows_bwd("rows_bwd_b1", dx2, dun0, x1, mix0, vecs["b1"])
    dvg = _glu_bwd("glu_bwd", vg, _to_segments(dmix0))
    dz = _colblock_dgrad("glu_dgrad", dvg, glu_g, 0, _ACT)
    dglu = _colblock_wgrad("glu_wgrad", z_s5, dvg)
    dpool_blocks = dpool.reshape(n_pool, N_DEV, pr, pc).transpose(1, 0, 2, 3).reshape(N_DEV, n_pool * pr, pc)
    (du0s, ducs, dbblk, dcblk, dapar, ddsk), (gp_up0, gp_glu, gp_pool, gp_up1c, gp_down0b) = _s5_bwd(
        u0s, ucs, dz, y_s5, bblk, cblk, apar, dsk,
        rider=([[dup0], [dglu], [dpool_blocks], [(dup1, (3 * D // 4, D // 4))], [(ddown0, (r_down // 2, r_down // 2))]],
               "scatter"))
    grad_x, red0 = _rows_bwd("rows_bwd_b0", dx1, _from_segments(du0s), x0, None, vecs["b0"])
    redc, = _rows_bwd("rows_bwd_ctx", None, _from_segments(ducs), ctx[0], None, vecs["c0"], want_dx=False)

    zero_d = jnp.zeros((D,), F32)
    dmod = jnp.stack([
        jnp.stack([red0[R_SHIFT], red0[R_SCALE], red1[R_GATE], red1[R_SHIFT], red1[R_SCALE], red2[R_GATE]]),
        jnp.stack([red2[R_SHIFT], red2[R_SCALE], red3[R_GATE], red3[R_SHIFT], red3[R_SCALE], red4[R_GATE]])])
    dmod_c = jnp.stack([jnp.stack([redc[R_SHIFT], redc[R_SCALE]] + [zero_d] * 4), jnp.zeros((6, D), F32)])
    dm_g, = _exchange([[jnp.stack([dmod, dmod_c], axis=1).reshape(2, 2, 6 * D)]], mode="gather", name="gather_dmods")
    dm_g = dm_g.reshape(N_DEV, 2, 2, 6 * D)
    dm_ctx = _sum_parts("sum_dmod_ctx", dm_g[:, :, 1, :])
    dm_rows = jnp.concatenate([dm_g[:, :, 0, :].transpose(1, 0, 2), dm_ctx[:, None, :]], axis=1)
    grad_ada_b = _sum_parts("sum_ada_b", dm_rows.transpose(1, 0, 2))
    dm_cols = dm_rows.reshape(2, N_DEV + 1, N_DEV, n_ada)
    dm_mine = lax.dynamic_index_in_dim(dm_cols, me, axis=2, keepdims=False)
    dm_mine = jnp.concatenate([dm_mine, jnp.zeros((2, ADA_ROWS - N_DEV - 1, n_ada), F32)], axis=1)
    grad_ada_w, dcond = _ada_bwd(cmat, ada_w, dm_mine)
    dcctx_part = dcond[0, N_DEV] + dcond[1, N_DEV]

    da, db, dc = _s5_unpack(dapar, dbblk, dcblk, G)
    dnorm = jnp.stack([
        jnp.stack([red0[R_GPRE] + redc[R_GPRE], red1[R_GPOST], red1[R_GPRE], red2[R_GPOST]]),
        jnp.stack([red2[R_GPRE], red3[R_GPOST], red3[R_GPRE], red4[R_GPOST]])])
    dconv = jnp.stack([d[:, :3, :].transpose(1, 0, 2).reshape(3, 2 * FF) for d in (dcw0, dcw1)])
    dconv_b = jnp.stack([d[:, 3, :].reshape(2 * FF) for d in (dcw0, dcw1)])
    pieces = [dcctx_part, dnorm, da, db, dc, ddsk[:, 0, :], red_ps[0], dconv, dconv_b]
    flat = jnp.concatenate([p.reshape(-1) for p in pieces])
    n_flat = flat.shape[0]
    per_dev = -(-n_flat // (N_DEV * 8 * LANE)) * 8 * LANE
    flat = jnp.pad(flat, (0, N_DEV * per_dev - n_flat)).reshape(N_DEV, per_dev // LANE, LANE)
    parts, = _exchange([[flat]], mode="scatter", name="scatter_small_grads")
    mine = _sum_parts("sum_small_grads", parts.reshape(N_DEV, per_dev // LANE, LANE))
    summed, = _exchange([[mine]], mode="gather", name="gather_small_grads")
    summed = summed.reshape(-1)
    red_pieces, o = [], 0
    for p in pieces:
        red_pieces.append(summed[o:o + p.size].reshape(p.shape))
        o += p.size
    g_cctx, g_norm, g_a, g_b, g_c, g_d, g_pscale, g_conv, g_conv_b = red_pieces
    cot = [(g_a[d, 0], g_a[d, 1], g_b[d, 0], g_b[d, 1]) for d in range(2)]
    g_lam_re, g_lam_im, g_log_step, g_b_re, g_b_im = disc_vjp(cot)

    out = {}

    def put(name, res, shape):
        out[name] = tuple(r.reshape(shape) for r in res)

    gp_up0, gp_up1a = gp_up0.reshape(N_DEV, D, nb_up), gp_up1a.reshape(N_DEV, D // 2, nb_up)
    quarters = (8, D // 4, nb_up)
    put("ffn_up", _adamw("adamw_ffn_up",
                         [(gp_up0, q * D // 4) for q in range(4)] + [(gp_up1a, 0), (gp_up1a, D // 4),
                                                                     gp_up1b.reshape(N_DEV, D // 4, nb_up),
                                                                     gp_up1c.reshape(N_DEV, D // 4, nb_up)],
                         ffn_up.reshape(quarters), m_ffn_up.reshape(quarters), v_ffn_up.reshape(quarters)),
        ffn_up.shape)
    halves = (4, r_down // 2, D)
    put("ffn_down", _adamw("adamw_ffn_down",
                           [g.reshape(N_DEV, r_down // 2, D) for g in (gp_down0a, gp_down0b, gp_down1a, gp_down1b)],
                           ffn_down.reshape(halves), m_ffn_down.reshape(halves), v_ffn_down.reshape(halves)),
        ffn_down.shape)
    put("s5_glu_w", _adamw("adamw_glu", [gp_glu.reshape(N_DEV, D, -1)], s5_glu_w, m_s5_glu_w, v_s5_glu_w),
        s5_glu_w.shape)
    pool_rows = (1, n_pool * pr, pc)
    put("pool_w", _adamw("adamw_pool", [gp_pool.reshape(N_DEV, n_pool * pr, pc)], pool_w.reshape(pool_rows),
                         m_pool_w.reshape(pool_rows), v_pool_w.reshape(pool_rows)), pool_w.shape)
    put("ada_w", _adamw("adamw_ada_w", [grad_ada_w[i][None] for i in range(2)], ada_w, m_ada_w, v_ada_w), ada_w.shape)

    for nm, w, m, v, g in (("s5_b_re", s5_b_re, m_s5_b_re, v_s5_b_re, g_b_re),
                           ("s5_b_im", s5_b_im, m_s5_b_im, v_s5_b_im, g_b_im),
                           ("s5_c_re", s5_c_re, m_s5_c_re, v_s5_c_re, g_c[:, 0]),
                           ("s5_c_im", s5_c_im, m_s5_c_im, v_s5_c_im, g_c[:, 1])):
        rows = (1, w.size // w.shape[-1], w.shape[-1])
        put(nm, _adamw("adamw_" + nm, [g.reshape(rows)], w.reshape(rows), m.reshape(rows), v.reshape(rows)), w.shape)

    small = [
        ("c_ctx", c_ctx, m_c_ctx, v_c_ctx, g_cctx),
        ("ada_b", ada_b, m_ada_b, v_ada_b, grad_ada_b),
        ("norm_g", norm_g, m_norm_g, v_norm_g, _my_block(g_norm, 2, ng_loc)),
        ("s5_lam_re", s5_lam_re, m_s5_lam_re, v_s5_lam_re, g_lam_re),
        ("s5_lam_im", s5_lam_im, m_s5_lam_im, v_s5_lam_im, g_lam_im),
        ("s5_log_step", s5_log_step, m_s5_log_step, v_s5_log_step, g_log_step),
        ("s5_d", s5_d, m_s5_d, v_s5_d, g_d),
        ("pool_scale", pool_scale, m_pool_scale, v_pool_scale, _my_block(g_pscale, 0, ng_loc)),
        ("ffn_conv", ffn_conv, m_ffn_conv, v_ffn_conv, _my_block(g_conv, 2, nb_up)),
        ("ffn_conv_b", ffn_conv_b, m_ffn_conv_b, v_ffn_conv_b, g_conv_b),
    ]
    n_sm = sum(w.size for _, w, _, _, _ in small)
    rows_sm = -(-n_sm // (512 * LANE)) * 512

    def flat_of(k):
        f = jnp.concatenate([t[k].reshape(-1) for t in small])
        return jnp.pad(f, (0, rows_sm * LANE - n_sm)).reshape(rows_sm, LANE)

    res_sm = _adamw("adamw_small", [flat_of(4)[None]], flat_of(1)[None], flat_of(2)[None], flat_of(3)[None])
    o = 0
    for name, w, _, _, _ in small:
        out[name] = tuple(r.reshape(-1)[o:o + w.size].reshape(w.shape) for r in res_sm)
        o += w.size

    order = ["c_ctx", "ada_w", "ada_b", "norm_g", "s5_lam_re", "s5_lam_im", "s5_log_step", "s5_b_re", "s5_b_im",
             "s5_c_re", "s5_c_im", "s5_d", "s5_glu_w", "pool_w", "pool_scale", "ffn_up", "ffn_conv", "ffn_conv_b",
             "ffn_down"]
    return (loss, grad_x.reshape(x.shape), *[out[n][0] for n in order], *[out[n][1] for n in order],
            *[out[n][2] for n in order], *[out[n][3] for n in order])
```

```python
import functools
import math

import jax
import jax.numpy as jnp
from jax import lax
from jax.experimental import pallas as pl
from jax.experimental.pallas import tpu as pltpu

F32 = jnp.float32
_ACT = jnp.bfloat16
N_DEV = 8
NSEG = 8
S5_CH = 16
S5_P = 64
LANE = 128
S5_TILE_CH = LANE
S5_TILE_G = S5_TILE_CH // S5_CH
S5_TILE_W = S5_TILE_G * S5_P
GRID_W = 64
POOL_WINDOWS = (2, 4, 8, 16)
POOL_HALO = 64
RMS_EPS = 1e-6
POS_BASE = 10000.0
ADAM_LR, ADAM_B1, ADAM_B2, ADAM_EPS, ADAM_WD, ADAM_STEP = 0.001, 0.9, 0.999, 1e-08, 0.01, 10
VMEM_LIMIT = 48 * 1024 * 1024
VMEM_LIMIT_BIG = 58 * 1024 * 1024
MESH = pl.DeviceIdType.MESH
ANY = pl.BlockSpec(memory_space=pl.ANY)


def _params(sem, vmem=VMEM_LIMIT):
    return pltpu.CompilerParams(dimension_semantics=sem, vmem_limit_bytes=vmem)


def _my_index():
    return 4 * lax.axis_index("x") + 2 * lax.axis_index("y") + lax.axis_index("c")


def _xchg_plan(groups, mode):
    flat = [(g, l, a) for g, grp in enumerate(groups) for l, a in enumerate(grp)]
    outs = []
    for grp in groups:
        a, rows = _rows_of(grp[0])
        piece = a.shape[1:] if mode == "scatter" else a.shape
        if rows is not None:
            piece = (rows[1],) + tuple(piece[1:])
        outs.append(jax.ShapeDtypeStruct((N_DEV, len(grp)) + tuple(piece), a.dtype))
    return flat, outs


def _rows_of(entry):
    return entry if isinstance(entry, tuple) else (entry, None)


def _operands(flat):
    return [_rows_of(a)[0] for _, _, a in flat]


def _xchg_sems(n):
    return [pltpu.SemaphoreType.DMA((n, N_DEV - 1)), pltpu.SemaphoreType.DMA((n, N_DEV - 1)),
            pltpu.SemaphoreType.DMA((n,))]


def _xchg_copies(flat, mode, ins, out_refs, sems, waiting=True):
    send_sems, recv_sems, local_sems = sems
    x, y, c = lax.axis_index("x"), lax.axis_index("y"), lax.axis_index("c")
    me = 4 * x + 2 * y + c
    local, first, forwards = [], [], []

    def pair(s, j, dev):
        return dict(send_sem=send_sems.at[s, j], recv_sem=recv_sems.at[s, j], device_id=dev, device_id_type=MESH)

    def block(s, dev):
        rows = _rows_of(flat[s][2])[1]
        ref = ins[s].at[dev]
        return ref if rows is None else ref.at[pl.ds(rows[0], rows[1])]

    for s, (g, l, _) in enumerate(flat):
        src = block(s, me) if mode == "scatter" else ins[s]
        local.append(pltpu.make_async_copy(src, out_refs[g].at[me, l], local_sems.at[s]))
    if mode == "gather2":
        sib, sib_idx = (x, y, 1 - c), 4 * x + 2 * y + (1 - c)
        for s, (g, l, _) in enumerate(flat):
            slot = lambda dev, g=g, l=l: out_refs[g].at[dev, l]
            targets = [(sib, sib_idx)] + [((qx, qy, c), 4 * qx + 2 * qy + c)
                                          for qx, qy in ((1 - x, y), (x, 1 - y), (1 - x, 1 - y))]
            for j, (dev, idx) in enumerate(targets):
                send = pltpu.make_async_remote_copy(src_ref=ins[s], dst_ref=slot(me), **pair(s, j, dev))
                arrive = pltpu.make_async_remote_copy(src_ref=ins[s], dst_ref=slot(idx), **pair(s, j, dev)) if waiting else None
                first.append((send, arrive))
            if waiting:
                for j, (dev, idx) in enumerate(targets[1:]):
                    other = 4 * dev[0] + 2 * dev[1] + (1 - c)
                    send = pltpu.make_async_remote_copy(src_ref=slot(idx), dst_ref=slot(idx), **pair(s, 4 + j, sib))
                    arrive = pltpu.make_async_remote_copy(src_ref=slot(idx), dst_ref=slot(other), **pair(s, 4 + j, sib))
                    forwards.append((first[len(first) - 3 + j][1], send, arrive))
        return local, first, forwards
    for k in range(1, N_DEV):
        px = 1 - x if k & 4 else x
        py = 1 - y if k & 2 else y
        pc = 1 - c if k & 1 else c
        peer = 4 * px + 2 * py + pc
        for s, (g, l, _) in enumerate(flat):
            src = block(s, peer) if mode == "scatter" else ins[s]
            send = pltpu.make_async_remote_copy(src_ref=src, dst_ref=out_refs[g].at[me, l], **pair(s, k - 1, (px, py, pc)))
            arrive = (pltpu.make_async_remote_copy(src_ref=src, dst_ref=out_refs[g].at[peer, l],
                                                   **pair(s, k - 1, (px, py, pc))) if waiting else None)
            first.append((send, arrive))
    return local, first, forwards


def _xchg_start(local, first, forwards):
    for cp in local:
        cp.start()
    for send, _ in first:
        send.start()


def _xchg_wait(local, first, forwards):
    gates = [gate for gate, _, _ in forwards]
    for gate, send, _ in forwards:
        gate.wait_recv()
        send.start()
    for _, arrive in first:
        if not any(arrive is gate for gate in gates):
            arrive.wait_recv()
    for _, _, arrive in forwards:
        arrive.wait_recv()
    for send, _ in first:
        send.wait_send()
    for _, send, _ in forwards:
        send.wait_send()
    for cp in local:
        cp.wait()


def _exchange(groups, mode, name):
    flat, outs = _xchg_plan(groups, mode)
    n = len(flat)

    def body(*refs):
        copies = _xchg_copies(flat, mode, refs[:n], refs[n:n + len(groups)], refs[n + len(groups):])
        _xchg_start(*copies)
        _xchg_wait(*copies)

    res = pl.pallas_call(body, name=name, out_shape=outs, in_specs=[ANY] * n, out_specs=[ANY] * len(groups),
                         scratch_shapes=_xchg_sems(n))(*_operands(flat))
    return list(res)


def _pcall(body, *, name, grid, in_specs, out_specs, out_shape, ins, scratch_shapes=(), sem=None, vmem=VMEM_LIMIT,
           rider=None):
    single = not isinstance(out_shape, (list, tuple))
    if rider is None:
        return pl.pallas_call(body, name=name, grid=grid, in_specs=list(in_specs), out_specs=out_specs,
                              out_shape=out_shape, scratch_shapes=list(scratch_shapes),
                              compiler_params=_params(sem, vmem))(*ins)
    groups, mode = rider
    flat, r_outs = _xchg_plan(groups, mode)
    n_in, n_out = len(ins), 1 if single else len(out_shape)
    nr, ng, ns = len(flat), len(groups), len(scratch_shapes)

    def wrapped(*refs):
        o1 = n_in + nr
        o2 = o1 + n_out
        o3 = o2 + ng
        r_in, r_out, sems = refs[n_in:o1], refs[o2:o3], refs[o3 + ns:]
        first = functools.reduce(jnp.logical_and, [pl.program_id(d) == 0 for d in range(len(grid))])
        last = functools.reduce(jnp.logical_and, [pl.program_id(d) == grid[d] - 1 for d in range(len(grid))])

        @pl.when(first)
        def _():
            _xchg_start(*_xchg_copies(flat, mode, r_in, r_out, sems, waiting=False))

        body(*refs[:n_in], *refs[o1:o2], *refs[o3:o3 + ns])

        @pl.when(last)
        def _():
            _xchg_wait(*_xchg_copies(flat, mode, r_in, r_out, sems))

    outs = pl.pallas_call(
        wrapped, name=name, grid=grid, in_specs=list(in_specs) + [ANY] * nr,
        out_specs=([out_specs] if single else list(out_specs)) + [ANY] * ng,
        out_shape=([out_shape] if single else list(out_shape)) + r_outs,
        scratch_shapes=list(scratch_shapes) + _xchg_sems(nr),
        compiler_params=_params(("arbitrary",) * len(grid), vmem))(*ins, *_operands(flat))
    base = list(outs[:n_out])
    return (base[0] if single else base), list(outs[n_out:])


_DIMS = {"nn": (((1,), (0,)), ((), ())), "nt": (((1,), (1,)), ((), ())), "tn": (((0,), (0,)), ((), ()))}


def _mm(name, a, b, a_spec, b_spec, o_spec, out_shape, grid, dims, rider=None):
    nk = grid[2]
    acc_shape = tuple(d for d in o_spec.block_shape if d is not None)
    dn = _DIMS[dims]

    def tile(ref):
        v = ref[...]
        return v.reshape((-1, v.shape[-1])).astype(_ACT)

    def body(a_ref, b_ref, o_ref, *scratch):
        def part():
            return lax.dot_general(tile(a_ref), tile(b_ref), dn, preferred_element_type=F32)

        if nk == 1:
            o_ref[...] = part().reshape(o_ref.shape).astype(o_ref.dtype)
            return
        acc_ref, = scratch
        k = pl.program_id(2)

        @pl.when(k == 0)
        def _():
            acc_ref[...] = part()

        @pl.when(k > 0)
        def _():
            acc_ref[...] += part()

        @pl.when(k == nk - 1)
        def _():
            o_ref[...] = acc_ref[...].reshape(o_ref.shape).astype(o_ref.dtype)

    acc2d = (math.prod(acc_shape[:-1]), acc_shape[-1])
    return _pcall(body, name=name, out_shape=out_shape, grid=grid, in_specs=[a_spec, b_spec], out_specs=o_spec,
                  scratch_shapes=[] if nk == 1 else [pltpu.VMEM(acc2d, F32)], ins=(a, b),
                  sem=("parallel", "parallel", "arbitrary"), rider=rider)


def _row_tile(n, want):
    t = min(n, want)
    assert n % t == 0, (n, t)
    return t


def _colblock_fwd(name, xa, wg, layer, out_dtype, rider=None):
    L, K = xa.shape
    nb = wg.shape[3]
    half = N_DEV // 2
    tm = _row_tile(L, 512)
    return _mm(name, xa, wg,
               pl.BlockSpec((tm, K), lambda j, i, k: (i, 0)),
               pl.BlockSpec((None, None, K, nb), lambda j, i, k: (j, layer, 0, 0)),
               pl.BlockSpec((None, tm, nb), lambda j, i, k: (j // half, i, j % half)),
               jax.ShapeDtypeStruct((2, L, half * nb), out_dtype), (N_DEV, L // tm, 1), "nn", rider=rider)


def _colblock_dgrad(name, dh, wg, layer, out_dtype, rider=None):
    _, L, _ = dh.shape
    K, nb = wg.shape[2], wg.shape[3]
    half = N_DEV // 2
    tm = _row_tile(L, 512)
    return _mm(name, dh, wg,
               pl.BlockSpec((None, tm, nb), lambda i, j, k: (k // half, i, k % half)),
               pl.BlockSpec((None, None, K, nb), lambda i, j, k: (k, layer, 0, 0)),
               pl.BlockSpec((tm, K), lambda i, j, k: (i, 0)),
               jax.ShapeDtypeStruct((L, K), out_dtype), (L // tm, 1, N_DEV), "nt", rider=rider)


def _colblock_wgrad(name, xa, dh, rider=None):
    L, K = xa.shape
    half = N_DEV // 2
    nb = dh.shape[2] // half
    tm = _row_tile(K, 512)
    tk = L
    return _mm(name, xa, dh,
               pl.BlockSpec((tk, tm), lambda j, i, k: (k, i)),
               pl.BlockSpec((None, tk, nb), lambda j, i, k: (j // half, k, j % half)),
               pl.BlockSpec((None, tm, nb), lambda j, i, k: (j, i, 0)),
               jax.ShapeDtypeStruct((N_DEV, K, nb), _ACT), (N_DEV, K // tm, L // tk), "tn", rider=rider)


def _rowblock_fwd(name, xa, wg, layer, rider=None):
    L, FF = xa.shape
    r, D = wg.shape[2], wg.shape[3]
    tm = _row_tile(L, 512)
    return _mm(name, xa, wg,
               pl.BlockSpec((tm, 2 * r), lambda i, j, k: (i, k)),
               pl.BlockSpec((2, None, r, D), lambda i, j, k: (k, layer, 0, 0)),
               pl.BlockSpec((tm, D), lambda i, j, k: (i, 0)),
               jax.ShapeDtypeStruct((L, D), F32), (L // tm, 1, N_DEV // 2), "nn", rider=rider)


def _rowblock_dgrad(name, dy, wg, layer, rider=None):
    L, D = dy.shape
    r = wg.shape[2]
    tm = _row_tile(L, 512)
    return _mm(name, dy, wg,
               pl.BlockSpec((tm, D), lambda i, j, k: (i, 0)),
               pl.BlockSpec((2, None, r, D), lambda i, j, k: (j, layer, 0, 0)),
               pl.BlockSpec((tm, 2 * r), lambda i, j, k: (i, j)),
               jax.ShapeDtypeStruct((L, N_DEV * r), _ACT), (L // tm, N_DEV // 2, 1), "nt", rider=rider)


def _rowblock_wgrad(name, xa, dy, rider=None):
    L, FF = xa.shape
    D = dy.shape[1]
    tm = FF // (N_DEV // 2)
    tn = _row_tile(D, 1024)
    tk = _row_tile(L, 2048)
    return _mm(name, xa, dy,
               pl.BlockSpec((tk, tm), lambda i, j, k: (k, i)),
               pl.BlockSpec((tk, tn), lambda i, j, k: (k, j)),
               pl.BlockSpec((tm, tn), lambda i, j, k: (i, j)),
               jax.ShapeDtypeStruct((FF, D), _ACT), (FF // tm, D // tn, L // tk), "tn", rider=rider)


def _group_mm(name, xa, w, dims, out_dtype):
    L, D = xa.shape
    ng, pc, _ = w.shape
    tm = _row_tile(L, 512)
    return _mm(name, xa, w,
               pl.BlockSpec((tm, pc), lambda i, g, k: (i, g)),
               pl.BlockSpec((None, pc, pc), lambda i, g, k: (g, 0, 0)),
               pl.BlockSpec((tm, pc), lambda i, g, k: (i, g)),
               jax.ShapeDtypeStruct((L, D), out_dtype), (L // tm, ng, 1), dims)


def _group_wgrad(name, p, dy, ng):
    L, D = p.shape
    pc = D // ng
    tk = _row_tile(L, 512)
    return _mm(name, p, dy,
               pl.BlockSpec((tk, pc), lambda g, j, k: (k, g)),
               pl.BlockSpec((tk, pc), lambda g, j, k: (k, g)),
               pl.BlockSpec((None, pc, pc), lambda g, j, k: (g, 0, 0)),
               jax.ShapeDtypeStruct((ng, pc, pc), _ACT), (ng, 1, L // tk), "tn")


V_GPOST, V_GATE, V_YSCALE, V_GPRE, V_SHIFT, V_SCALE = range(6)
R_SHIFT, R_SCALE, R_GPRE, R_GATE, R_GPOST = range(5)
ROW_TILE = 256


def _rstd(v):
    return lax.rsqrt(jnp.mean(v * v, axis=-1, keepdims=True) + RMS_EPS)


def _rows_fwd(name, xres, y, vec, *, add=False, target=None, want_x=True, u_dtype=None):
    L, D = xres.shape
    tm = _row_tile(L, ROW_TILE)
    has_y = y is not None
    last = target is not None
    has_u = u_dtype is not None

    def body(*refs):
        refs = list(refs)
        xres_ref = refs.pop(0)
        y_ref = refs.pop(0) if has_y else None
        vec_ref = refs.pop(0)
        tgt_ref = refs.pop(0) if last else None
        xnew = xres_ref[...]
        if has_y and add:
            xnew = xnew + y_ref[...]
        elif has_y:
            ye = y_ref[...] * vec_ref[V_YSCALE:V_YSCALE + 1, :]
            xnew = xnew + vec_ref[V_GATE:V_GATE + 1, :] * (ye * _rstd(ye) * vec_ref[V_GPOST:V_GPOST + 1, :])
        if last:
            dx_ref, loss_ref = refs
            diff = xnew - tgt_ref[...]
            dx_ref[...] = diff * (1.0 / D)

            @pl.when(pl.program_id(0) == 0)
            def _():
                loss_ref[...] = jnp.zeros_like(loss_ref)

            loss_ref[...] += jnp.sum(diff * diff) * (0.5 / D)
            return
        if want_x:
            refs.pop(0)[...] = xnew
        if has_u:
            u_ref, = refs
            n = xnew * _rstd(xnew) * vec_ref[V_GPRE:V_GPRE + 1, :]
            u_ref[...] = (n * (1.0 + vec_ref[V_SCALE:V_SCALE + 1, :]) + vec_ref[V_SHIFT:V_SHIFT + 1, :]).astype(u_ref.dtype)

    row = pl.BlockSpec((tm, D), lambda i: (i, 0))
    vspec = pl.BlockSpec((8, D), lambda i: (0, 0))
    ins, in_specs = [xres], [row]
    if has_y:
        ins.append(y)
        in_specs.append(row)
    ins.append(vec)
    in_specs.append(vspec)
    out_shape, out_specs = [], []
    if last:
        ins.append(target)
        in_specs.append(row)
        out_shape = [jax.ShapeDtypeStruct((L, D), F32), jax.ShapeDtypeStruct((8, LANE), F32)]
        out_specs = [row, pl.BlockSpec((8, LANE), lambda i: (0, 0))]
    else:
        if want_x:
            out_shape.append(jax.ShapeDtypeStruct((L, D), F32))
            out_specs.append(row)
        if has_u:
            out_shape.append(jax.ShapeDtypeStruct((L, D), u_dtype))
            out_specs.append(row)
    return pl.pallas_call(body, name=name, out_shape=out_shape, grid=(L // tm,), in_specs=in_specs,
                          out_specs=out_specs, compiler_params=_params(("arbitrary",)))(*ins)


def _rows_bwd(name, dxd, du, xnew, y, vec, dy_dtype=F32, want_dx=True):
    L, D = xnew.shape if xnew is not None else dxd.shape
    tm = _row_tile(L, ROW_TILE)
    has_dxd, has_pre, has_post = dxd is not None, du is not None, y is not None

    def body(*refs):
        refs = list(refs)
        dxd_ref = refs.pop(0) if has_dxd else None
        du_ref = refs.pop(0) if has_pre else None
        xnew_ref = refs.pop(0) if has_pre else None
        y_ref = refs.pop(0) if has_post else None
        vec_ref = refs.pop(0)
        dx_ref = refs.pop(0) if want_dx else None
        dy_ref = refs.pop(0) if has_post else None
        red_ref, = refs

        @pl.when(pl.program_id(0) == 0)
        def _():
            red_ref[...] = jnp.zeros_like(red_ref)

        def acc(rw, val):
            red_ref[rw:rw + 1, :] += jnp.sum(val, axis=0, keepdims=True)

        dxn = dxd_ref[...] if has_dxd else None
        if has_pre:
            xn = xnew_ref[...]
            r = _rstd(xn)
            nh = xn * r
            gpre = vec_ref[V_GPRE:V_GPRE + 1, :]
            dub = du_ref[...].astype(F32)
            acc(R_SHIFT, dub)
            acc(R_SCALE, dub * (nh * gpre))
            drn = dub * (1.0 + vec_ref[V_SCALE:V_SCALE + 1, :])
            acc(R_GPRE, drn * nh)
            dnh = drn * gpre
            t = r * (dnh - nh * jnp.mean(dnh * nh, axis=-1, keepdims=True))
            dxn = t if dxn is None else dxn + t
        if want_dx:
            dx_ref[...] = dxn
        if has_post:
            ye = y_ref[...] * vec_ref[V_YSCALE:V_YSCALE + 1, :]
            ry = _rstd(ye)
            yh = ye * ry
            gpost = vec_ref[V_GPOST:V_GPOST + 1, :]
            acc(R_GATE, dxn * (yh * gpost))
            drn2 = dxn * vec_ref[V_GATE:V_GATE + 1, :]
            acc(R_GPOST, drn2 * yh)
            dyh = drn2 * gpost
            dy_ref[...] = (ry * (dyh - yh * jnp.mean(dyh * yh, axis=-1, keepdims=True))).astype(dy_ref.dtype)

    row = pl.BlockSpec((tm, D), lambda i: (i, 0))
    vspec = pl.BlockSpec((8, D), lambda i: (0, 0))
    ins, in_specs = [], []
    for a in ([dxd] if has_dxd else []) + ([du, xnew] if has_pre else []) + ([y] if has_post else []):
        ins.append(a)
        in_specs.append(row)
    ins.append(vec)
    in_specs.append(vspec)
    out_shape, out_specs = [], []
    if want_dx:
        out_shape.append(jax.ShapeDtypeStruct((L, D), F32))
        out_specs.append(row)
    if has_post:
        out_shape.append(jax.ShapeDtypeStruct((L, D), dy_dtype))
        out_specs.append(row)
    out_shape.append(jax.ShapeDtypeStruct((8, D), F32))
    out_specs.append(vspec)
    return pl.pallas_call(body, name=name, out_shape=out_shape, grid=(L // tm,), in_specs=in_specs,
                          out_specs=out_specs, compiler_params=_params(("arbitrary",)))(*ins)


def _colscale_bwd(name, dy, ypre, scale):
    L, D = dy.shape
    tm = _row_tile(L, ROW_TILE)

    def body(dy_ref, yp_ref, s_ref, o_ref, red_ref):
        @pl.when(pl.program_id(0) == 0)
        def _():
            red_ref[...] = jnp.zeros_like(red_ref)

        d = dy_ref[...]
        o_ref[...] = (d * s_ref[0:1, :]).astype(o_ref.dtype)
        red_ref[0:1, :] += jnp.sum(d * yp_ref[...], axis=0, keepdims=True)

    row = pl.BlockSpec((tm, D), lambda i: (i, 0))
    vspec = pl.BlockSpec((8, D), lambda i: (0, 0))
    return pl.pallas_call(body, name=name, grid=(L // tm,), in_specs=[row, row, vspec], out_specs=[row, vspec],
                          out_shape=[jax.ShapeDtypeStruct((L, D), _ACT), jax.ShapeDtypeStruct((8, D), F32)],
                          compiler_params=_params(("arbitrary",)))(dy, ypre, scale)


def _sigmoid(v):
    return 1.0 / (1.0 + jnp.exp(-v))


def _glu_fwd(name, vg):
    _, L, D = vg.shape
    tm = _row_tile(L, ROW_TILE)

    def body(vg_ref, o_ref):
        o_ref[...] = vg_ref[0].astype(F32) * _sigmoid(vg_ref[1].astype(F32))

    return pl.pallas_call(body, name=name, grid=(L // tm,),
                          in_specs=[pl.BlockSpec((2, tm, D), lambda i: (0, i, 0))],
                          out_specs=pl.BlockSpec((tm, D), lambda i: (i, 0)),
                          out_shape=jax.ShapeDtypeStruct((L, D), F32),
                          compiler_params=_params(("parallel",)))(vg)


def _glu_bwd(name, vg, dout):
    _, L, D = vg.shape
    tm = _row_tile(L, ROW_TILE)

    def body(vg_ref, d_ref, o_ref):
        val, s = vg_ref[0].astype(F32), _sigmoid(vg_ref[1].astype(F32))
        d = d_ref[...]
        o_ref[0] = (d * s).astype(o_ref.dtype)
        o_ref[1] = (d * val * s * (1.0 - s)).astype(o_ref.dtype)

    return pl.pallas_call(body, name=name, grid=(L // tm,),
                          in_specs=[pl.BlockSpec((2, tm, D), lambda i: (0, i, 0)), pl.BlockSpec((tm, D), lambda i: (i, 0))],
                          out_specs=pl.BlockSpec((2, tm, D), lambda i: (0, i, 0)),
                          out_shape=jax.ShapeDtypeStruct((2, L, D), _ACT),
                          compiler_params=_params(("parallel",)))(vg, dout)


CONV_ROWS = 256


def _row_pick(blk, idx):
    rows = lax.broadcasted_iota(jnp.int32, blk.shape, 0)
    return jnp.sum(jnp.where(rows == idx, blk, 0.0), axis=0, keepdims=True)


def _shifted(ref, r0, rc, L):
    cur = ref[pl.ds(r0, rc), :].astype(F32)
    before = ref[pl.ds(pl.multiple_of(jnp.maximum(r0 - 16, 0), 16), 16), :].astype(F32)
    after = ref[pl.ds(pl.multiple_of(jnp.minimum(r0 + rc, L - 16), 16), 16), :].astype(F32)
    prev_row = jnp.where(r0 > 0, _row_pick(before, 15), 0.0)
    next_row = jnp.where(r0 + rc < L, _row_pick(after, 0), 0.0)
    rows = lax.broadcasted_iota(jnp.int32, cur.shape, 0)
    up = jnp.where(rows == 0, prev_row, pltpu.roll(cur, 1, 0))
    down = jnp.where(rows == rc - 1, next_row, pltpu.roll(cur, rc - 1, 0))
    return up, cur, down


def _silu_parts(g):
    s = _sigmoid(g)
    return g * s, s


def _conv_swiglu_fwd(name, h, cw, rider=None):
    _, L, FF = h.shape
    rc = _row_tile(L, CONV_ROWS)

    def body(h_ref, cw_ref, o_ref):
        def chunk(ci, _):
            r0 = pl.multiple_of(ci * rc, rc)
            hc = []
            for half in range(2):
                up, cur, down = _shifted(h_ref.at[half], r0, rc, L)
                hc.append(up * cw_ref[half, 0:1, :] + cur * cw_ref[half, 1:2, :] + down * cw_ref[half, 2:3, :]
                          + cw_ref[half, 3:4, :])
            o_ref[pl.ds(r0, rc), :] = (_silu_parts(hc[1])[0] * hc[0]).astype(o_ref.dtype)
            return 0

        lax.fori_loop(0, L // rc, chunk, 0)

    return _pcall(body, name=name, grid=(FF // LANE,),
                  in_specs=[pl.BlockSpec((2, L, LANE), lambda j: (0, 0, j)),
                            pl.BlockSpec((2, 8, LANE), lambda j: (0, 0, j))],
                  out_specs=pl.BlockSpec((L, LANE), lambda j: (0, j)),
                  out_shape=jax.ShapeDtypeStruct((L, FF), _ACT), ins=(h, cw), sem=("parallel",), rider=rider)


def _conv_swiglu_bwd(name, h, cw, dact, rider=None):
    _, L, FF = h.shape
    rc = _row_tile(L, CONV_ROWS)

    def body(h_ref, cw_ref, da_ref, dh_ref, dcw_ref, dhc_ref):
        def chunk(ci, acc):
            r0 = pl.multiple_of(ci * rc, rc)
            taps, hc = [], []
            for half in range(2):
                t = _shifted(h_ref.at[half], r0, rc, L)
                taps.append(t)
                hc.append(t[0] * cw_ref[half, 0:1, :] + t[1] * cw_ref[half, 1:2, :] + t[2] * cw_ref[half, 2:3, :]
                          + cw_ref[half, 3:4, :])
            d = da_ref[pl.ds(r0, rc), :].astype(F32)
            act, s = _silu_parts(hc[1])
            dhc = (d * act, d * hc[0] * (s + act * (1.0 - s)))
            new = []
            for half in range(2):
                dhc_ref[half, pl.ds(r0, rc), :] = dhc[half]
                for k in range(3):
                    new.append(acc[4 * half + k] + jnp.sum(dhc[half] * taps[half][k], axis=0, keepdims=True))
                new.append(acc[4 * half + 3] + jnp.sum(dhc[half], axis=0, keepdims=True))
            return tuple(new)

        zero = jnp.zeros((1, LANE), F32)
        acc = lax.fori_loop(0, L // rc, chunk, (zero,) * 8)
        dcw_ref[...] = jnp.zeros_like(dcw_ref)
        for half in range(2):
            for k in range(4):
                dcw_ref[half, k:k + 1, :] = acc[4 * half + k]

        def chunk2(ci, _):
            r0 = pl.multiple_of(ci * rc, rc)
            for half in range(2):
                up, cur, down = _shifted(dhc_ref.at[half], r0, rc, L)
                dh_ref[half, pl.ds(r0, rc), :] = (down * cw_ref[half, 0:1, :] + cur * cw_ref[half, 1:2, :]
                                                  + up * cw_ref[half, 2:3, :]).astype(dh_ref.dtype)
            return 0

        lax.fori_loop(0, L // rc, chunk2, 0)

    return _pcall(body, name=name, grid=(FF // LANE,),
                  in_specs=[pl.BlockSpec((2, L, LANE), lambda j: (0, 0, j)),
                            pl.BlockSpec((2, 8, LANE), lambda j: (0, 0, j)),
                            pl.BlockSpec((L, LANE), lambda j: (0, j))],
                  out_specs=[pl.BlockSpec((2, L, LANE), lambda j: (0, 0, j)),
                             pl.BlockSpec((2, 8, LANE), lambda j: (0, 0, j))],
                  out_shape=[jax.ShapeDtypeStruct((2, L, FF), _ACT), jax.ShapeDtypeStruct((2, 8, FF), F32)],
                  scratch_shapes=[pltpu.VMEM((2, L, LANE), F32)], ins=(h, cw, dact), sem=("parallel",), rider=rider)


POOL_ROWS = 256
POOL_TILE = 256


def _pool_bands(transpose):
    i = jnp.arange(POOL_ROWS)[:, None]
    j = jnp.arange(POOL_ROWS + 2 * POOL_HALO)[None, :] - POOL_HALO
    bands = []
    for w in POOL_WINDOWS:
        lo, hi = (-(w // 2 - 1), w // 2) if transpose else (-(w // 2), w // 2 - 1)
        bands.append(((j - i >= lo) & (j - i <= hi)).astype(_ACT))
    return jnp.stack(bands)


def _pool_window(name, u, transpose, out_dtype):
    L, D = u.shape
    ng = len(POOL_WINDOWS)
    pc = D // ng
    tn = min(POOL_TILE, pc)
    rc = _row_tile(L, POOL_ROWS)
    bands = _pool_bands(transpose)
    if rc != POOL_ROWS:
        bands = bands[:, :rc, :rc + 2 * POOL_HALO]
    halo = POOL_HALO

    def body(u_ref, band_ref, o_ref, hi_ref, lo_ref):
        g = (pl.program_id(0) * tn) // pc
        half = jnp.zeros((1, 1), jnp.int32)
        for k, w in enumerate(POOL_WINDOWS):
            half = jnp.where(g == k, w // 2, half)
        zeros = jnp.zeros((halo, tn), _ACT)
        for ref in (hi_ref, lo_ref):
            ref[0:halo, :] = zeros
            ref[halo + L:2 * halo + L, :] = zeros

        def inv_count(r0):
            t = r0 + lax.broadcasted_iota(jnp.int32, (rc, tn), 0)
            lo = jnp.clip(t - half, 0, L - 1)
            hi = jnp.clip(t + half - 1, 0, L - 1)
            return 1.0 / (hi - lo + 1).astype(F32)

        def split(ci, _):
            r0 = pl.multiple_of(ci * rc, rc)
            v = u_ref[pl.ds(r0, rc), :].astype(F32)
            if transpose:
                v = v * inv_count(r0)
            hi = v.astype(_ACT)
            dst = pl.ds(pl.multiple_of(r0 + halo, halo), rc)
            hi_ref[dst, :] = hi
            lo_ref[dst, :] = (v - hi.astype(F32)).astype(_ACT)
            return 0

        lax.fori_loop(0, L // rc, split, 0)
        band = band_ref[...]

        def chunk(ci, _):
            r0 = pl.multiple_of(ci * rc, rc)
            win = pl.ds(r0, rc + 2 * halo)
            s = (jnp.dot(band, hi_ref[win, :], preferred_element_type=F32)
                 + jnp.dot(band, lo_ref[win, :], preferred_element_type=F32))
            if not transpose:
                s = s * inv_count(r0)
            o_ref[pl.ds(r0, rc), :] = (s - u_ref[pl.ds(r0, rc), :].astype(F32)).astype(o_ref.dtype)
            return 0

        lax.fori_loop(0, L // rc, chunk, 0)

    return pl.pallas_call(body, name=name, grid=(D // tn,),
                          in_specs=[pl.BlockSpec((L, tn), lambda j: (0, j)),
                                    pl.BlockSpec((None, rc, rc + 2 * halo), lambda j: ((j * tn) // pc, 0, 0))],
                          out_specs=pl.BlockSpec((L, tn), lambda j: (0, j)),
                          out_shape=jax.ShapeDtypeStruct((L, D), out_dtype),
                          scratch_shapes=[pltpu.VMEM((L + 2 * halo, tn), _ACT), pltpu.VMEM((L + 2 * halo, tn), _ACT)],
                          compiler_params=_params(("parallel",)))(u, bands)


S5_ROWS = 512


def _slab(start):
    return pl.ds(start if isinstance(start, int) else pl.multiple_of(start, NSEG), NSEG)


def _cmul(ar, ai, br, bi):
    return ar * br - ai * bi, ar * bi + ai * br


def _cpow(ar, ai, n):
    rr, ri = None, None
    br, bi = ar, ai
    while n:
        if n & 1:
            rr, ri = (br, bi) if rr is None else _cmul(rr, ri, br, bi)
        n >>= 1
        if n:
            br, bi = _cmul(br, bi, br, bi)
    return rr, ri


def _pow_table(pw_ref, ar, ai, n):
    W = ar.shape[1]
    pr, pi = ar, ai
    for r in range(NSEG):
        pw_ref[0, r:r + 1, :] = pr
        pw_ref[1, r:r + 1, :] = pi
        if r < NSEG - 1:
            pr, pi = _cmul(pr, pi, ar, ai)
    a8r, a8i = (jnp.broadcast_to(v, (NSEG, W)) for v in _cpow(ar, ai, NSEG))

    def step(k, carry):
        nr, ni = _cmul(carry[0], carry[1], a8r, a8i)
        pw_ref[0, _slab(k * NSEG), :] = nr
        pw_ref[1, _slab(k * NSEG), :] = ni
        return nr, ni

    lax.fori_loop(1, n // NSEG, step, (pw_ref[0, 0:NSEG, :], pw_ref[1, 0:NSEG, :]))


def _seg_scan(sr_ref, si_ref, tmp_ref, pw_ref, row0, n, ar, ai, h0, rev, conj=False, pair_with=None):
    W = ar.shape[1]
    arb, aib = jnp.broadcast_to(ar, (NSEG, W)), jnp.broadcast_to(ai, (NSEG, W))

    def rows(s):
        t = (n - 1 - s) if rev else s
        return _slab(row0 + t * NSEG)

    def step(s, carry):
        hr, hi = carry
        sl = rows(s)
        nr = arb * hr - aib * hi + sr_ref[sl, :]
        ni = arb * hi + aib * hr + si_ref[sl, :]
        sr_ref[sl, :] = nr
        si_ref[sl, :] = ni
        return nr, ni

    zero = jnp.zeros((NSEG, W), F32)
    fr, fi = lax.fori_loop(0, n, step, (zero, zero), unroll=2)
    tmp_ref[0] = fr
    tmp_ref[1] = fi
    anr, ani = _cpow(ar, ai, n)
    cr, ci = h0
    for j in (range(NSEG - 1, -1, -1) if rev else range(NSEG)):
        tmp_ref[2, j:j + 1, :] = cr
        tmp_ref[3, j:j + 1, :] = ci
        pr, pi = _cmul(anr, ani, cr, ci)
        cr, ci = tmp_ref[0, j:j + 1, :] + pr, tmp_ref[1, j:j + 1, :] + pi
    cmr, cmi = tmp_ref[2], tmp_ref[3]

    def fix(k, acc):
        for r in range(NSEG):
            row = pl.ds(pl.multiple_of(k * NSEG, NSEG) + r, 1)
            pr = jnp.broadcast_to(pw_ref[0, row, :], (NSEG, W))
            pi = jnp.broadcast_to(pw_ref[1, row, :], (NSEG, W))
            if conj:
                pi = -pi
            s = k * NSEG + r
            sl = rows(s)
            gr = sr_ref[sl, :] + (pr * cmr - pi * cmi)
            gi = si_ref[sl, :] + (pr * cmi + pi * cmr)
            sr_ref[sl, :] = gr
            si_ref[sl, :] = gi
            if pair_with is not None:
                h_r, h_i, (hcr, hci) = pair_with
                prev = rows(jnp.minimum(s + 1, n - 1))
                first = s == n - 1
                hpr = jnp.where(first, hcr, h_r[prev, :])
                hpi = jnp.where(first, hci, h_i[prev, :])
                acc = (acc[0] + hpr * gr + hpi * gi, acc[1] + hpr * gi - hpi * gr)
        return acc

    zero_acc = (zero, zero) if pair_with is not None else 0
    total = lax.fori_loop(0, n // NSEG, fix, zero_acc)
    return (cr, ci), (cmr, cmi), (total if pair_with is not None else None)


def _gelu_tanh(y):
    k = math.sqrt(2.0 / math.pi)
    t = jnp.tanh(k * (y + 0.044715 * y * y * y))
    return 0.5 * y * (1.0 + t), t


def _s5_chunks(L):
    rc = _row_tile(L, S5_ROWS)
    return [(r, rc) for r in range(0, L, rc)]


_NT_DIMS = (((1,), (1,)), ((), ()))
_TN_DIMS = (((0,), (0,)), ((), ()))
_LOG_P, _LOG_CH = S5_P.bit_length() - 1, S5_CH.bit_length() - 1


def _same_group(shape, row_shift, col_shift):
    rows = lax.broadcasted_iota(jnp.int32, shape, 0)
    cols = lax.broadcasted_iota(jnp.int32, shape, 1)
    return lax.shift_right_logical(rows, row_shift) == lax.shift_right_logical(cols, col_shift)


def _s5_bt(bt):
    full = jnp.concatenate([bt] * S5_TILE_G, axis=0)
    return jnp.where(_same_group(full.shape, _LOG_P, _LOG_CH), full, 0.0).astype(_ACT)


def _s5_ct(ct):
    full = jnp.concatenate([ct] * S5_TILE_G, axis=0)
    return jnp.where(_same_group(full.shape, _LOG_CH, _LOG_P), full, 0.0).astype(_ACT)


def _s5_diag(m):
    m = jnp.where(_same_group(m.shape, _LOG_CH, _LOG_P), m, 0.0)
    rows = lax.broadcasted_iota(jnp.int32, (S5_TILE_W, S5_P), 0)
    cols = lax.broadcasted_iota(jnp.int32, (S5_TILE_W, S5_P), 1)
    pick = (jnp.bitwise_and(rows, S5_P - 1) == cols).astype(_ACT)
    hi = m.astype(_ACT)
    lo = (m - hi.astype(F32)).astype(_ACT)
    return jnp.dot(hi, pick, preferred_element_type=F32) + jnp.dot(lo, pick, preferred_element_type=F32)


def _s5_project(u_ref, uc_ref, bre, bim, sr_ref, si_ref, L, LC):
    for ref, base, n in ((u_ref, 0, L), (uc_ref, L, LC)):
        for r, rc in _s5_chunks(n):
            ub = ref[r:r + rc, :].astype(_ACT)
            sr_ref[base + r:base + r + rc, :] = lax.dot_general(ub, bre, _NT_DIMS, preferred_element_type=F32)
            si_ref[base + r:base + r + rc, :] = lax.dot_general(ub, bim, _NT_DIMS, preferred_element_type=F32)


def _s5_states(sr_ref, si_ref, tmp_ref, pw_ref, ar, ai, L, LC, rev):
    W = ar.shape[1]
    zero = (jnp.zeros((1, W), F32), jnp.zeros((1, W), F32))
    hctx, cm_ctx, _ = _seg_scan(sr_ref, si_ref, tmp_ref, pw_ref, L, LC // NSEG, ar, ai, zero, rev)
    _, cm_lat, _ = _seg_scan(sr_ref, si_ref, tmp_ref, pw_ref, 0, L // NSEG, ar, ai, hctx, rev)
    return cm_lat, cm_ctx


def _s5_fwd(u, uc, bblk, cblk, apar, dsk, rider=None):
    L, D = u.shape
    LC = uc.shape[0]
    NT, W, TC = D // S5_TILE_CH, S5_TILE_W, S5_TILE_CH

    def body(u_ref, uc_ref, b_ref, c_ref, a_ref, d_ref, y_ref, z_ref, sr_ref, si_ref, tmp_ref, pw_ref):
        for r, rc in _s5_chunks(L):
            y_ref[r:r + rc, :] = u_ref[r:r + rc, :].astype(F32) * d_ref[0:1, :]
        for d in range(2):
            ar, ai = a_ref[2 * d:2 * d + 1, :], a_ref[2 * d + 1:2 * d + 2, :]
            _pow_table(pw_ref, ar, ai, L // NSEG)
            _s5_project(u_ref, uc_ref, _s5_bt(b_ref[2 * d]), _s5_bt(b_ref[2 * d + 1]), sr_ref, si_ref, L, LC)
            _s5_states(sr_ref, si_ref, tmp_ref, pw_ref, ar, ai, L, LC, rev=(d == 1))
            cre, cim = _s5_ct(c_ref[2 * d]), _s5_ct(c_ref[2 * d + 1])
            for r, rc in _s5_chunks(L):
                y_ref[r:r + rc, :] += (
                    lax.dot_general(sr_ref[r:r + rc, :].astype(_ACT), cre, _NT_DIMS, preferred_element_type=F32)
                    - lax.dot_general(si_ref[r:r + rc, :].astype(_ACT), cim, _NT_DIMS, preferred_element_type=F32))
        for r, rc in _s5_chunks(L):
            z_ref[r:r + rc, :] = _gelu_tanh(y_ref[r:r + rc, :])[0].astype(z_ref.dtype)

    col = lambda n: pl.BlockSpec((n, TC), lambda j: (0, j))
    return _pcall(
        body, name="s5_fwd", grid=(NT,), ins=(u, uc, bblk, cblk, apar, dsk), sem=("parallel",), rider=rider,
        in_specs=[col(L), col(LC),
                  pl.BlockSpec((None, 4, S5_P, TC), lambda j: (j, 0, 0, 0)),
                  pl.BlockSpec((None, 4, S5_CH, W), lambda j: (j, 0, 0, 0)),
                  pl.BlockSpec((None, 8, W), lambda j: (j, 0, 0)),
                  pl.BlockSpec((8, TC), lambda j: (0, j))],
        out_specs=[col(L), col(L)],
        out_shape=[jax.ShapeDtypeStruct((L, D), F32), jax.ShapeDtypeStruct((L, D), _ACT)],
        scratch_shapes=[pltpu.VMEM((L + LC, W), F32), pltpu.VMEM((L + LC, W), F32), pltpu.VMEM((4, NSEG, W), F32),
                        pltpu.VMEM((2, L // NSEG, W), F32)])


def _s5_bwd(u, uc, dz, y, bblk, cblk, apar, dsk, rider=None):
    L, D = u.shape
    LC = uc.shape[0]
    NT, W, TC = D // S5_TILE_CH, S5_TILE_W, S5_TILE_CH
    nl, nc = L // NSEG, LC // NSEG

    def body(u_ref, uc_ref, dz_ref, y_ref, b_ref, c_ref, a_ref, d_ref,
             du_ref, duc_ref, db_ref, dc_ref, da_ref, dd_ref,
             hr_ref, hi_ref, gr_ref, gi_ref, dy_ref, tmp_ref, pw_ref):
        ddacc = jnp.zeros((1, TC), F32)
        for r, rc in _s5_chunks(L):
            yv = y_ref[r:r + rc, :]
            g, t = _gelu_tanh(yv)
            k = math.sqrt(2.0 / math.pi)
            dg = 0.5 * (1.0 + t) + 0.5 * yv * (1.0 - t * t) * k * (1.0 + 3 * 0.044715 * yv * yv)
            dy = dz_ref[r:r + rc, :].astype(F32) * dg
            uv = u_ref[r:r + rc, :].astype(F32)
            ddacc = ddacc + jnp.sum(dy * uv, axis=0, keepdims=True)
            du_ref[r:r + rc, :] = dy * d_ref[0:1, :]
            dy_ref[r:r + rc, :] = dy.astype(dy_ref.dtype)
        dd_ref[...] = jnp.zeros_like(dd_ref)
        dd_ref[0:1, :] = ddacc
        duc_ref[...] = jnp.zeros_like(duc_ref)
        da_ref[...] = jnp.zeros_like(da_ref)
        tn = _TN_DIMS
        for d in range(2):
            rev = d == 1
            ar, ai = a_ref[2 * d:2 * d + 1, :], a_ref[2 * d + 1:2 * d + 2, :]
            bre, bim = _s5_bt(b_ref[2 * d]), _s5_bt(b_ref[2 * d + 1])
            cre, cim = _s5_ct(c_ref[2 * d]), _s5_ct(c_ref[2 * d + 1])
            _pow_table(pw_ref, ar, ai, nl)
            _s5_project(u_ref, uc_ref, bre, bim, hr_ref, hi_ref, L, LC)
            cm_lat, cm_ctx = _s5_states(hr_ref, hi_ref, tmp_ref, pw_ref, ar, ai, L, LC, rev)
            cml_r, cml_i, cmc_r, cmc_i = cm_lat[0], cm_lat[1], cm_ctx[0], cm_ctx[1]
            dcr = jnp.zeros((TC, W), F32)
            dci = jnp.zeros((TC, W), F32)
            for r, rc in _s5_chunks(L):
                dyb = dy_ref[r:r + rc, :]
                gr_ref[r:r + rc, :] = jnp.dot(dyb, cre, preferred_element_type=F32)
                gi_ref[r:r + rc, :] = -jnp.dot(dyb, cim, preferred_element_type=F32)
                dcr = dcr + lax.dot_general(dyb, hr_ref[r:r + rc, :].astype(_ACT), tn, preferred_element_type=F32)
                dci = dci - lax.dot_general(dyb, hi_ref[r:r + rc, :].astype(_ACT), tn, preferred_element_type=F32)
            dc_ref[2 * d] = _s5_diag(dcr)
            dc_ref[2 * d + 1] = _s5_diag(dci)
            gr_ref[L:L + LC, :] = jnp.zeros((LC, W), F32)
            gi_ref[L:L + LC, :] = jnp.zeros((LC, W), F32)
            zero = (jnp.zeros((1, W), F32), jnp.zeros((1, W), F32))
            glat, _, (lr, li) = _seg_scan(gr_ref, gi_ref, tmp_ref, pw_ref, 0, nl, ar, -ai, zero, not rev, conj=True,
                                          pair_with=(hr_ref, hi_ref, (cml_r, cml_i)))
            _, _, (qr, qi) = _seg_scan(gr_ref, gi_ref, tmp_ref, pw_ref, L, nc, ar, -ai, glat, not rev, conj=True,
                                       pair_with=(hr_ref, hi_ref, (cmc_r, cmc_i)))
            da_ref[2 * d:2 * d + 1, :] = jnp.sum(lr + qr, axis=0, keepdims=True)
            da_ref[2 * d + 1:2 * d + 2, :] = jnp.sum(li + qi, axis=0, keepdims=True)
            dbr = jnp.zeros((TC, W), F32)
            dbi = jnp.zeros((TC, W), F32)
            for ref, oref, base, n in ((u_ref, du_ref, 0, L), (uc_ref, duc_ref, L, LC)):
                for r, rc in _s5_chunks(n):
                    ub = ref[r:r + rc, :].astype(_ACT)
                    gr = gr_ref[base + r:base + r + rc, :].astype(_ACT)
                    gi = gi_ref[base + r:base + r + rc, :].astype(_ACT)
                    dbr = dbr + lax.dot_general(ub, gr, tn, preferred_element_type=F32)
                    dbi = dbi + lax.dot_general(ub, gi, tn, preferred_element_type=F32)
                    oref[r:r + rc, :] += (jnp.dot(gr, bre, preferred_element_type=F32)
                                          + jnp.dot(gi, bim, preferred_element_type=F32))
            db_ref[2 * d] = _s5_diag(dbr)
            db_ref[2 * d + 1] = _s5_diag(dbi)

    col = lambda n: pl.BlockSpec((n, TC), lambda j: (0, j))
    bspec = pl.BlockSpec((None, 4, S5_P, TC), lambda j: (j, 0, 0, 0))
    cspec = pl.BlockSpec((None, 4, S5_CH, W), lambda j: (j, 0, 0, 0))
    gspec = pl.BlockSpec((None, 4, TC, S5_P), lambda j: (j, 0, 0, 0))
    aspec = pl.BlockSpec((None, 8, W), lambda j: (j, 0, 0))
    return _pcall(
        body, name="s5_bwd", grid=(NT,), ins=(u, uc, dz, y, bblk, cblk, apar, dsk), sem=("parallel",),
        vmem=VMEM_LIMIT_BIG, rider=rider,
        in_specs=[col(L), col(LC), col(L), col(L), bspec, cspec, aspec, pl.BlockSpec((8, TC), lambda j: (0, j))],
        out_specs=[col(L), col(LC), gspec, gspec, aspec, pl.BlockSpec((None, 8, TC), lambda j: (j, 0, 0))],
        out_shape=[jax.ShapeDtypeStruct((L, D), F32), jax.ShapeDtypeStruct((LC, D), F32),
                   jax.ShapeDtypeStruct((NT, 4, TC, S5_P), F32), jax.ShapeDtypeStruct((NT, 4, TC, S5_P), F32),
                   jax.ShapeDtypeStruct((NT, 8, W), F32), jax.ShapeDtypeStruct((NT, 8, TC), F32)],
        scratch_shapes=[pltpu.VMEM((L + LC, W), F32), pltpu.VMEM((L + LC, W), F32),
                        pltpu.VMEM((L + LC, W), F32), pltpu.VMEM((L + LC, W), F32),
                        pltpu.VMEM((L, TC), _ACT), pltpu.VMEM((4, NSEG, W), F32), pltpu.VMEM((2, nl, W), F32)])


ADA_ROWS = 16


def _silu_rows(c_ref):
    c = c_ref[...]
    return c * _sigmoid(c)


def _ada_fwd(cmat, ada_w, ada_b):
    nl, D, n = ada_w.shape
    tn = _row_tile(n, 512)

    def body(c_ref, w_ref, b_ref, o_ref):
        a = _silu_rows(c_ref).astype(_ACT)
        o_ref[...] = jnp.dot(a, w_ref[...].astype(_ACT), preferred_element_type=F32) + b_ref[...]

    return pl.pallas_call(body, name="ada_fwd", grid=(nl, n // tn),
                          in_specs=[pl.BlockSpec((ADA_ROWS, D), lambda l, j: (0, 0)),
                                    pl.BlockSpec((None, D, tn), lambda l, j: (l, 0, j)),
                                    pl.BlockSpec((None, 1, tn), lambda l, j: (l, 0, j))],
                          out_specs=pl.BlockSpec((None, ADA_ROWS, tn), lambda l, j: (l, 0, j)),
                          out_shape=jax.ShapeDtypeStruct((nl, ADA_ROWS, n), F32),
                          compiler_params=_params(("parallel", "parallel")))(cmat, ada_w, ada_b)


def _ada_bwd(cmat, ada_w, dm):
    nl, D, n = ada_w.shape
    tn = _row_tile(n, 512)
    nj = n // tn

    def body(c_ref, w_ref, dm_ref, dw_ref, dc_ref):
        c = c_ref[...]
        s = _sigmoid(c)
        a = (c * s).astype(_ACT)
        dmb = dm_ref[...].astype(_ACT)
        dw_ref[...] = lax.dot_general(a, dmb, (((0,), (0,)), ((), ())), preferred_element_type=F32)
        part = lax.dot_general(dmb, w_ref[...].astype(_ACT), (((1,), (1,)), ((), ())), preferred_element_type=F32)
        part = part * (s * (1.0 + c * (1.0 - s)))

        @pl.when(pl.program_id(1) == 0)
        def _():
            dc_ref[...] = part

        @pl.when(pl.program_id(1) > 0)
        def _():
            dc_ref[...] += part

    return pl.pallas_call(body, name="ada_bwd", grid=(nl, nj),
                          in_specs=[pl.BlockSpec((ADA_ROWS, D), lambda l, j: (0, 0)),
                                    pl.BlockSpec((None, D, tn), lambda l, j: (l, 0, j)),
                                    pl.BlockSpec((None, ADA_ROWS, tn), lambda l, j: (l, 0, j))],
                          out_specs=[pl.BlockSpec((None, D, tn), lambda l, j: (l, 0, j)),
                                     pl.BlockSpec((None, ADA_ROWS, D), lambda l, j: (l, 0, 0))],
                          out_shape=[jax.ShapeDtypeStruct((nl, D, n), F32), jax.ShapeDtypeStruct((nl, ADA_ROWS, D), F32)],
                          compiler_params=_params(("parallel", "arbitrary")))(cmat, ada_w, dm)


def _adamw(name, gparts, w, m, v):
    nl, R, C = w.shape
    gparts = [g if isinstance(g, tuple) else (g, 0) for g in gparts]
    n = gparts[0][0].shape[0]
    tr = R
    part_bytes = nl * n * C * gparts[0][0].dtype.itemsize * 2
    for cand in (4096, 2048, 1024, 512, 256, 128, 64, 32, 16, 8):
        if R % cand == 0 and cand * max(C, LANE) * 4 <= 2 * 1024 * 1024 and cand * part_bytes <= VMEM_LIMIT // 2:
            tr = cand
            break
    nt = R // tr
    bc1 = 1.0 - ADAM_B1 ** ADAM_STEP
    bc2 = 1.0 - ADAM_B2 ** ADAM_STEP

    def body(*refs):
        g_refs = refs[:nl]
        w_ref, m_ref, v_ref, go_ref, d_ref, mo_ref, vo_ref = refs[nl:]
        for l in range(nl):
            @pl.when(pl.program_id(0) == l)
            def _(g_ref=g_refs[l]):
                g = g_ref[0].astype(F32)
                for j in range(1, n):
                    g = g + g_ref[j].astype(F32)
                m2 = ADAM_B1 * m_ref[...] + (1.0 - ADAM_B1) * g
                v2 = ADAM_B2 * v_ref[...] + (1.0 - ADAM_B2) * (g * g)
                go_ref[...] = g
                mo_ref[...] = m2
                vo_ref[...] = v2
                d_ref[...] = -ADAM_LR * ((m2 / bc1) / (jnp.sqrt(v2 / bc2) + ADAM_EPS) + ADAM_WD * w_ref[...])

    def gspec(l):
        first = gparts[l][1] // tr
        return pl.BlockSpec((n, tr, C),
                            lambda lyr, i: (0, first + jnp.where(lyr < l, 0, jnp.where(lyr > l, nt - 1, i)), 0))

    row = pl.BlockSpec((None, tr, C), lambda lyr, i: (lyr, i, 0))
    out = jax.ShapeDtypeStruct((nl, R, C), F32)
    return _pcall(body, name=name, grid=(nl, nt), in_specs=[gspec(l) for l in range(nl)] + [row, row, row],
                  out_specs=[row, row, row, row], out_shape=[out, out, out, out],
                  ins=(*[g for g, _ in gparts], w, m, v), sem=("arbitrary", "arbitrary"))


def _sum_parts(name, parts):
    n, R, C = parts.shape

    def body(p_ref, o_ref):
        s = p_ref[0]
        for j in range(1, n):
            s = s + p_ref[j]
        o_ref[...] = s

    return pl.pallas_call(body, name=name, out_shape=jax.ShapeDtypeStruct((R, C), F32),
                          compiler_params=_params(None))(parts)


def _discretize(lam_re, lam_im, log_step, b_re, b_im):
    dt = jnp.exp(log_step)[:, None]
    mag = jnp.exp(lam_re * dt)
    abar_re = mag * jnp.cos(lam_im * dt)
    abar_im = mag * jnp.sin(lam_im * dt)
    nr, ni = abar_re - 1.0, abar_im
    den = lam_re * lam_re + lam_im * lam_im
    fr = (nr * lam_re + ni * lam_im) / den
    fi = (ni * lam_re - nr * lam_im) / den
    bbar_re = fr[..., None] * b_re - fi[..., None] * b_im
    bbar_im = fr[..., None] * b_im + fi[..., None] * b_re
    return abar_re, abar_im, bbar_re, bbar_im


def _s5_pack(abar, bbar, cmat):
    G = abar[0][0].shape[0]
    NT = G // S5_TILE_G
    a4 = jnp.stack([abar[d][r] for d in range(2) for r in range(2)]).reshape(4, NT, S5_TILE_W).transpose(1, 0, 2)
    apar = jnp.concatenate([a4, jnp.zeros((NT, 4, S5_TILE_W), F32)], axis=1)
    b4 = jnp.stack([bbar[d][r] for d in range(2) for r in range(2)]).reshape(4, NT, S5_TILE_G, S5_P, S5_CH)
    bt = b4.transpose(1, 0, 3, 2, 4).reshape(NT, 4, S5_P, S5_TILE_CH)
    c4 = jnp.stack([cmat[d][r] for d in range(2) for r in range(2)]).reshape(4, NT, S5_TILE_G, S5_CH, S5_P)
    ct = c4.transpose(1, 0, 3, 2, 4).reshape(NT, 4, S5_CH, S5_TILE_W)
    return apar, bt, ct


def _s5_unpack(dapar, dbd, dcd, G):
    NT = G // S5_TILE_G
    da = dapar[:, :4, :].reshape(NT, 2, 2, S5_TILE_G, S5_P).transpose(1, 2, 0, 3, 4).reshape(2, 2, G, S5_P)
    db = dbd.reshape(NT, 4, S5_TILE_G, S5_CH, S5_P).transpose(1, 0, 2, 4, 3).reshape(2, 2, G, S5_P, S5_CH)
    dc = dcd.reshape(NT, 4, S5_TILE_G, S5_CH, S5_P).transpose(1, 0, 2, 3, 4).reshape(2, 2, G, S5_CH, S5_P)
    return da, db, dc


def _to_segments(a):
    L, D = a.shape
    return a.reshape(NSEG, L // NSEG, D).transpose(1, 0, 2).reshape(L, D)


def _from_segments(a):
    L, D = a.shape
    return a.reshape(L // NSEG, NSEG, D).transpose(1, 0, 2).reshape(L, D)


def _pos_emb(n_tokens, dim):
    rows = n_tokens // GRID_W
    quarter = dim // 4
    omega = 1.0 / (POS_BASE ** (jnp.arange(quarter, dtype=F32) / quarter))

    def enc(p):
        ang = p[:, None] * omega[None, :]
        return jnp.concatenate([jnp.sin(ang), jnp.cos(ang)], axis=-1)

    rtab = enc(jnp.arange(rows, dtype=F32))
    ctab = enc(jnp.arange(GRID_W, dtype=F32))
    return jnp.concatenate([jnp.repeat(rtab, GRID_W, axis=0), jnp.tile(ctab, (rows, 1))], axis=-1)


def _vec(D, **rows):
    names = {"gpost": V_GPOST, "gate": V_GATE, "yscale": V_YSCALE, "gpre": V_GPRE, "shift": V_SHIFT, "scale": V_SCALE}
    out = [jnp.zeros((D,), F32)] * 8
    out[V_YSCALE] = jnp.ones((D,), F32)
    for k, v in rows.items():
        out[names[k]] = v.reshape(D).astype(F32)
    return jnp.stack(out)


def _row0(v, D):
    return jnp.concatenate([v.reshape(1, D).astype(F32), jnp.zeros((7, D), F32)], axis=0)


def _my_block(full, axis, n_local):
    return lax.dynamic_slice_in_dim(full, _my_index() * n_local, n_local, axis)


def kernel(x, c, ctx, c_ctx, ada_w, ada_b, norm_g, s5_lam_re, s5_lam_im, s5_log_step, s5_b_re, s5_b_im, s5_c_re, s5_c_im, s5_d, s5_glu_w, pool_w, pool_scale, ffn_up, ffn_conv, ffn_conv_b, ffn_down, loss_target, m_c_ctx, m_ada_w, m_ada_b, m_norm_g, m_s5_lam_re, m_s5_lam_im, m_s5_log_step, m_s5_b_re, m_s5_b_im, m_s5_c_re, m_s5_c_im, m_s5_d, m_s5_glu_w, m_pool_w, m_pool_scale, m_ffn_up, m_ffn_conv, m_ffn_conv_b, m_ffn_down, v_c_ctx, v_ada_w, v_ada_b, v_norm_g, v_s5_lam_re, v_s5_lam_im, v_s5_log_step, v_s5_b_re, v_s5_b_im, v_s5_c_re, v_s5_c_im, v_s5_d, v_s5_glu_w, v_pool_w, v_pool_scale, v_ffn_up, v_ffn_conv, v_ffn_conv_b, v_ffn_down):
    L, D = x.shape[1], x.shape[2]
    LC = ctx.shape[1]
    G = s5_lam_re.shape[2]
    n_ada = ada_w.shape[2]
    nb_up = ffn_up.shape[2]
    r_down = ffn_down.shape[1]
    FF = N_DEV * r_down
    n_pool = len(POOL_WINDOWS)
    pc = D // n_pool
    pr = pool_w.shape[2]
    ng_loc = norm_g.shape[2]
    me = _my_index()
    axes = ("x", "y", "c")

    up_b = [ffn_up[i].astype(_ACT) for i in range(2)]
    down_b = [ffn_down[i].astype(_ACT) for i in range(2)]
    glu_b = s5_glu_w[0].astype(_ACT)
    pool_b = pool_w[0].reshape(n_pool * pr, pc).astype(_ACT)

    small_loc = jnp.concatenate([c.reshape(-1), norm_g.reshape(-1), pool_scale.reshape(-1), ffn_conv.reshape(-1)])
    n_small = small_loc.shape[0]
    small_g, = _exchange([[jnp.pad(small_loc, (0, (-n_small) % LANE)).reshape(1, -1)]], mode="gather", name="gather_small")
    small_g = small_g.reshape(N_DEV, -1)
    o = 0
    c_all = small_g[:, o:o + D]
    o += D
    ng_all = small_g[:, o:o + 8 * ng_loc].reshape(N_DEV, 2, 4, ng_loc).transpose(1, 2, 0, 3).reshape(2, 4, D)
    o += 8 * ng_loc
    pscale_all = small_g[:, o:o + ng_loc].reshape(D)
    o += ng_loc
    conv_all = small_g[:, o:o + 6 * nb_up].reshape(N_DEV, 2, 3, nb_up).transpose(1, 2, 0, 3).reshape(2, 3, 2 * FF)

    cmat = jnp.concatenate([c_all, c_ctx.reshape(1, D), jnp.zeros((ADA_ROWS - N_DEV - 1, D), F32)], axis=0)
    ada_b_loc = _my_block(ada_b, 1, n_ada).reshape(2, 1, n_ada)
    mods_loc = _ada_fwd(cmat, ada_w, ada_b_loc)
    mods_g, = _exchange([[mods_loc]], mode="gather", name="gather_mods")
    mods_rows = mods_g.reshape(N_DEV, 2, ADA_ROWS, n_ada).transpose(1, 2, 0, 3).reshape(2, ADA_ROWS, 6, D)
    mod = lax.dynamic_index_in_dim(mods_rows, me, axis=1, keepdims=False)
    mod_c = mods_rows[0, N_DEV]

    def disc_all(lr, li, ls, br, bi):
        return [_discretize(lr[d], li[d], ls[d], br[d], bi[d]) for d in range(2)]

    disc, disc_vjp = jax.vjp(disc_all, s5_lam_re[0], s5_lam_im[0], s5_log_step[0], s5_b_re[0], s5_b_im[0])
    apar, bblk, cblk = _s5_pack([(disc[d][0], disc[d][1]) for d in range(2)],
                                [(disc[d][2], disc[d][3]) for d in range(2)],
                                [(s5_c_re[0, d], s5_c_im[0, d]) for d in range(2)])
    dsk = _row0(s5_d[0], D)
    cw = []
    for i in range(2):
        taps = conv_all[i].reshape(3, 2, FF).transpose(1, 0, 2)
        cw.append(jnp.concatenate([taps, ffn_conv_b[i].reshape(2, 1, FF), jnp.zeros((2, 4, FF), F32)], axis=1))

    vecs = {
        "b0": _vec(D, gpre=ng_all[0, 0], shift=mod[0, 0], scale=mod[0, 1]),
        "c0": _vec(D, gpre=ng_all[0, 0], shift=mod_c[0], scale=mod_c[1]),
        "b1": _vec(D, gpost=ng_all[0, 1], gate=mod[0, 2], gpre=ng_all[0, 2], shift=mod[0, 3], scale=mod[0, 4]),
        "b2": _vec(D, gpost=ng_all[0, 3], gate=mod[0, 5], gpre=ng_all[1, 0], shift=mod[1, 0], scale=mod[1, 1]),
        "b3": _vec(D, gpost=ng_all[1, 1], gate=mod[1, 2], yscale=pscale_all, gpre=ng_all[1, 2], shift=mod[1, 3],
                   scale=mod[1, 4]),
        "b4": _vec(D, gpost=ng_all[1, 3], gate=mod[1, 5]),
    }

    x0, u0 = _rows_fwd("rows_fwd_b0", x[0], _pos_emb(L, D), vecs["b0"], add=True, u_dtype=_ACT)
    uc, = _rows_fwd("rows_fwd_ctx", ctx[0], None, vecs["c0"], want_x=False, u_dtype=_ACT)
    u0s, ucs = _to_segments(u0), _to_segments(uc)
    (y_s5, z_s5), (glu_g, up_g0) = _s5_fwd(u0s, ucs, bblk, cblk, apar, dsk,
                                           rider=([[glu_b], [up_b[0]]], "gather2"))
    vg, (down_g0,) = _colblock_fwd("glu_fwd_mm", z_s5, glu_g, 0, _ACT, rider=([[down_b[0]]], "gather2"))
    mix0 = _from_segments(_glu_fwd("glu_fwd", vg))
    x1, un0 = _rows_fwd("rows_fwd_b1", x0, mix0, vecs["b1"], u_dtype=_ACT)
    h0, (up_g1,) = _colblock_fwd("ffn0_up", un0, up_g0, 0, _ACT, rider=([[up_b[1]]], "gather2"))
    act0, (down_g1,) = _conv_swiglu_fwd("ffn0_conv", h0, cw[0], rider=([[down_b[1]]], "gather2"))
    f0, (pool_g,) = _rowblock_fwd("ffn0_down", act0, down_g0, 0, rider=([[pool_b]], "gather2"))
    pool_full = pool_g.reshape(N_DEV, n_pool, pr, pc).transpose(1, 0, 2, 3).reshape(n_pool, pc, pc)
    x2, u1 = _rows_fwd("rows_fwd_b2", x1, f0, vecs["b2"], u_dtype=F32)
    p1 = _pool_window("pool_fwd", u1, False, _ACT)
    ypre1 = _group_mm("pool_fwd_mm", p1, pool_full, "nn", F32)
    x3, un1 = _rows_fwd("rows_fwd_b3", x2, ypre1, vecs["b3"], u_dtype=_ACT)
    h1 = _colblock_fwd("ffn1_up", un1, up_g1, 0, _ACT)
    act1 = _conv_swiglu_fwd("ffn1_conv", h1, cw[1])
    f1 = _rowblock_fwd("ffn1_down", act1, down_g1, 0)
    dx4, loss_blk = _rows_fwd("rows_fwd_b4", x3, f1, vecs["b4"], target=loss_target[0])
    loss = lax.psum(loss_blk[0, 0], axes)

    df1, red4 = _rows_bwd("rows_bwd_b4", dx4, None, None, f1, vecs["b4"], dy_dtype=_ACT, want_dx=False)
    dact1 = _rowblock_dgrad("ffn1_down_dgrad", df1, down_g1, 0)
    ddown1 = _rowblock_wgrad("ffn1_down_wgrad", act1, df1)
    ddown1 = ddown1.reshape(N_DEV, r_down, D)
    (dh1, dcw1), (gp_down1a,) = _conv_swiglu_bwd("ffn1_conv_bwd", h1, cw[1], dact1,
                                                rider=([[(ddown1, (0, r_down // 2))]], "scatter"))
    dun1, (gp_down1b,) = _colblock_dgrad("ffn1_up_dgrad", dh1, up_g1, 0, F32,
                                         rider=([[(ddown1, (r_down // 2, r_down // 2))]], "scatter"))
    dup1 = _colblock_wgrad("ffn1_up_wgrad", un1, dh1)
    dx3, dy3, red3 = _rows_bwd("rows_bwd_b3", dx4, dun1, x3, ypre1, vecs["b3"])
    dypre1, red_ps = _colscale_bwd("pool_scale_bwd", dy3, ypre1, _row0(pscale_all, D))
    dp1 = _group_mm("pool_dgrad", dypre1, pool_full, "nt", F32)
    dpool = _group_wgrad("pool_wgrad", p1, dypre1, n_pool)
    du1 = _pool_window("pool_bwd", dp1, True, F32)
    dx2, df0, red2 = _rows_bwd("rows_bwd_b2", dx3, du1, x2, f0, vecs["b2"], dy_dtype=_ACT)
    dact0 = _rowblock_dgrad("ffn0_down_dgrad", df0, down_g0, 0)
    ddown0 = _rowblock_wgrad("ffn0_down_wgrad", act0, df0)
    ddown0 = ddown0.reshape(N_DEV, r_down, D)
    (dh0, dcw0), (gp_down0a,) = _conv_swiglu_bwd("ffn0_conv_bwd", h0, cw[0], dact0,
                                                 rider=([[(ddown0, (0, r_down // 2))]], "scatter"))
    dun0, (gp_up1a,) = _colblock_dgrad("ffn0_up_dgrad", dh0, up_g0, 0, F32,
                                       rider=([[(dup1, (0, D // 2))]], "scatter"))
    dup0, (gp_up1b,) = _colblock_wgrad("ffn0_up_wgrad", un0, dh0, rider=([[(dup1, (D // 2, D // 4))]], "scatter"))
    dx1, dmix0, red1 = _rows_bwd("rows_bwd_b1", dx2, dun0, x1, mix0, vecs["b1"])
    dvg = _glu_bwd("glu_bwd", vg, _to_segments(dmix0))
    dz = _colblock_dgrad("glu_dgrad", dvg, glu_g, 0, _ACT)
    dglu = _colblock_wgrad("glu_wgrad", z_s5, dvg)
    dpool_blocks = dpool.reshape(n_pool, N_DEV, pr, pc).transpose(1, 0, 2, 3).reshape(N_DEV, n_pool * pr, pc)
    (du0s, ducs, dbblk, dcblk, dapar, ddsk), (gp_up0, gp_glu, gp_pool, gp_up1c, gp_down0b) = _s5_bwd(
        u0s, ucs, dz, y_s5, bblk, cblk, apar, dsk,
        rider=([[dup0], [dglu], [dpool_blocks], [(dup1, (3 * D // 4, D // 4))], [(ddown0, (r_down // 2, r_down // 2))]],
               "scatter"))
    grad_x, red0 = _rows_bwd("rows_bwd_b0", dx1, _from_segments(du0s), x0, None, vecs["b0"])
    redc, = _rows_bwd("rows_bwd_ctx", None, _from_segments(ducs), ctx[0], None, vecs["c0"], want_dx=False)

    zero_d = jnp.zeros((D,), F32)
    dmod = jnp.stack([
        jnp.stack([red0[R_SHIFT], red0[R_SCALE], red1[R_GATE], red1[R_SHIFT], red1[R_SCALE], red2[R_GATE]]),
        jnp.stack([red2[R_SHIFT], red2[R_SCALE], red3[R_GATE], red3[R_SHIFT], red3[R_SCALE], red4[R_GATE]])])
    dmod_c = jnp.stack([jnp.stack([redc[R_SHIFT], redc[R_SCALE]] + [zero_d] * 4), jnp.zeros((6, D), F32)])
    dm_g, = _exchange([[jnp.stack([dmod, dmod_c], axis=1).reshape(2, 2, 6 * D)]], mode="gather", name="gather_dmods")
    dm_g = dm_g.reshape(N_DEV, 2, 2, 6 * D)
    dm_ctx = _sum_parts("sum_dmod_ctx", dm_g[:, :, 1, :])
    dm_rows = jnp.concatenate([dm_g[:, :, 0, :].transpose(1, 0, 2), dm_ctx[:, None, :]], axis=1)
    grad_ada_b = _sum_parts("sum_ada_b", dm_rows.transpose(1, 0, 2))
    dm_cols = dm_rows.reshape(2, N_DEV + 1, N_DEV, n_ada)
    dm_mine = lax.dynamic_index_in_dim(dm_cols, me, axis=2, keepdims=False)
    dm_mine = jnp.concatenate([dm_mine, jnp.zeros((2, ADA_ROWS - N_DEV - 1, n_ada), F32)], axis=1)
    grad_ada_w, dcond = _ada_bwd(cmat, ada_w, dm_mine)
    dcctx_part = dcond[0, N_DEV] + dcond[1, N_DEV]

    da, db, dc = _s5_unpack(dapar, dbblk, dcblk, G)
    dnorm = jnp.stack([
        jnp.stack([red0[R_GPRE] + redc[R_GPRE], red1[R_GPOST], red1[R_GPRE], red2[R_GPOST]]),
        jnp.stack([red2[R_GPRE], red3[R_GPOST], red3[R_GPRE], red4[R_GPOST]])])
    dconv = jnp.stack([d[:, :3, :].transpose(1, 0, 2).reshape(3, 2 * FF) for d in (dcw0, dcw1)])
    dconv_b = jnp.stack([d[:, 3, :].reshape(2 * FF) for d in (dcw0, dcw1)])
    pieces = [dcctx_part, dnorm, da, db, dc, ddsk[:, 0, :], red_ps[0], dconv, dconv_b]
    flat = jnp.concatenate([p.reshape(-1) for p in pieces])
    n_flat = flat.shape[0]
    per_dev = -(-n_flat // (N_DEV * 8 * LANE)) * 8 * LANE
    flat = jnp.pad(flat, (0, N_DEV * per_dev - n_flat)).reshape(N_DEV, per_dev // LANE, LANE)
    parts, = _exchange([[flat]], mode="scatter", name="scatter_small_grads")
    mine = _sum_parts("sum_small_grads", parts.reshape(N_DEV, per_dev // LANE, LANE))
    summed, = _exchange([[mine]], mode="gather", name="gather_small_grads")
    summed = summed.reshape(-1)
    red_pieces, o = [], 0
    for p in pieces:
        red_pieces.append(summed[o:o + p.size].reshape(p.shape))
        o += p.size
    g_cctx, g_norm, g_a, g_b, g_c, g_d, g_pscale, g_conv, g_conv_b = red_pieces
    cot = [(g_a[d, 0], g_a[d, 1], g_b[d, 0], g_b[d, 1]) for d in range(2)]
    g_lam_re, g_lam_im, g_log_step, g_b_re, g_b_im = disc_vjp(cot)

    out = {}

    def put(name, res, shape):
        out[name] = tuple(r.reshape(shape) for r in res)

    gp_up0, gp_up1a = gp_up0.reshape(N_DEV, D, nb_up), gp_up1a.reshape(N_DEV, D // 2, nb_up)
    quarters = (8, D // 4, nb_up)
    put("ffn_up", _adamw("adamw_ffn_up",
                         [(gp_up0, q * D // 4) for q in range(4)] + [(gp_up1a, 0), (gp_up1a, D // 4),
                                                                     gp_up1b.reshape(N_DEV, D // 4, nb_up),
                                                                     gp_up1c.reshape(N_DEV, D // 4, nb_up)],
                         ffn_up.reshape(quarters), m_ffn_up.reshape(quarters), v_ffn_up.reshape(quarters)),
        ffn_up.shape)
    halves = (4, r_down // 2, D)
    put("ffn_down", _adamw("adamw_ffn_down",
                           [g.reshape(N_DEV, r_down // 2, D) for g in (gp_down0a, gp_down0b, gp_down1a, gp_down1b)],
                           ffn_down.reshape(halves), m_ffn_down.reshape(halves), v_ffn_down.reshape(halves)),
        ffn_down.shape)
    put("s5_glu_w", _adamw("adamw_glu", [gp_glu.reshape(N_DEV, D, -1)], s5_glu_w, m_s5_glu_w, v_s5_glu_w),
        s5_glu_w.shape)
    pool_rows = (1, n_pool * pr, pc)
    put("pool_w", _adamw("adamw_pool", [gp_pool.reshape(N_DEV, n_pool * pr, pc)], pool_w.reshape(pool_rows),
                         m_pool_w.reshape(pool_rows), v_pool_w.reshape(pool_rows)), pool_w.shape)
    put("ada_w", _adamw("adamw_ada_w", [grad_ada_w[i][None] for i in range(2)], ada_w, m_ada_w, v_ada_w), ada_w.shape)

    for nm, w, m, v, g in (("s5_b_re", s5_b_re, m_s5_b_re, v_s5_b_re, g_b_re),
                           ("s5_b_im", s5_b_im, m_s5_b_im, v_s5_b_im, g_b_im),
                           ("s5_c_re", s5_c_re, m_s5_c_re, v_s5_c_re, g_c[:, 0]),
                           ("s5_c_im", s5_c_im, m_s5_c_im, v_s5_c_im, g_c[:, 1])):
        rows = (1, w.size // w.shape[-1], w.shape[-1])
        put(nm, _adamw("adamw_" + nm, [g.reshape(rows)], w.reshape(rows), m.reshape(rows), v.reshape(rows)), w.shape)

    small = [
        ("c_ctx", c_ctx, m_c_ctx, v_c_ctx, g_cctx),
        ("ada_b", ada_b, m_ada_b, v_ada_b, grad_ada_b),
        ("norm_g", norm_g, m_norm_g, v_norm_g, _my_block(g_norm, 2, ng_loc)),
        ("s5_lam_re", s5_lam_re, m_s5_lam_re, v_s5_lam_re, g_lam_re),
        ("s5_lam_im", s5_lam_im, m_s5_lam_im, v_s5_lam_im, g_lam_im),
        ("s5_log_step", s5_log_step, m_s5_log_step, v_s5_log_step, g_log_step),
        ("s5_d", s5_d, m_s5_d, v_s5_d, g_d),
        ("pool_scale", pool_scale, m_pool_scale, v_pool_scale, _my_block(g_pscale, 0, ng_loc)),
        ("ffn_conv", ffn_conv, m_ffn_conv, v_ffn_conv, _my_block(g_conv, 2, nb_up)),
        ("ffn_conv_b", ffn_conv_b, m_ffn_conv_b, v_ffn_conv_b, g_conv_b),
    ]
    n_sm = sum(w.size for _, w, _, _, _ in small)
    rows_sm = -(-n_sm // (512 * LANE)) * 512

    def flat_of(k):
        f = jnp.concatenate([t[k].reshape(-1) for t in small])
        return jnp.pad(f, (0, rows_sm * LANE - n_sm)).reshape(rows_sm, LANE)

    res_sm = _adamw("adamw_small", [flat_of(4)[None]], flat_of(1)[None], flat_of(2)[None], flat_of(3)[None])
    o = 0
    for name, w, _, _, _ in small:
        out[name] = tuple(r.reshape(-1)[o:o + w.size].reshape(w.shape) for r in res_sm)
        o += w.size

    order = ["c_ctx", "ada_w", "ada_b", "norm_g", "s5_lam_re", "s5_lam_im", "s5_log_step", "s5_b_re", "s5_b_im",
             "s5_c_re", "s5_c_im", "s5_d", "s5_glu_w", "pool_w", "pool_scale", "ffn_up", "ffn_conv", "ffn_conv_b",
             "ffn_down"]
    return (loss, grad_x.reshape(x.shape), *[out[n][0] for n in order], *[out[n][1] for n in order],
            *[out[n][2] for n in order], *[out[n][3] for n in order])
```

```python
import functools
import math

import jax
import jax.numpy as jnp
from jax import lax
from jax.experimental import pallas as pl
from jax.experimental.pallas import tpu as pltpu

F32 = jnp.float32
_ACT = jnp.bfloat16
N_DEV = 8
NSEG = 8
S5_CH = 16
S5_P = 64
LANE = 128
S5_TILE_CH = LANE
S5_TILE_G = S5_TILE_CH // S5_CH
S5_TILE_W = S5_TILE_G * S5_P
GRID_W = 64
POOL_WINDOWS = (2, 4, 8, 16)
POOL_HALO = 64
RMS_EPS = 1e-6
POS_BASE = 10000.0
ADAM_LR, ADAM_B1, ADAM_B2, ADAM_EPS, ADAM_WD, ADAM_STEP = 0.001, 0.9, 0.999, 1e-08, 0.01, 10
VMEM_LIMIT = 48 * 1024 * 1024
VMEM_LIMIT_BIG = 58 * 1024 * 1024
MESH = pl.DeviceIdType.MESH
ANY = pl.BlockSpec(memory_space=pl.ANY)


def _params(sem, vmem=VMEM_LIMIT):
    return pltpu.CompilerParams(dimension_semantics=sem, vmem_limit_bytes=vmem)


def _my_index():
    return 4 * lax.axis_index("x") + 2 * lax.axis_index("y") + lax.axis_index("c")


def _xchg_plan(groups, mode):
    flat = [(g, l, a) for g, grp in enumerate(groups) for l, a in enumerate(grp)]
    outs = []
    for grp in groups:
        a, rows = _rows_of(grp[0])
        piece = a.shape[1:] if mode == "scatter" else a.shape
        if rows is not None:
            piece = (rows[1],) + tuple(piece[1:])
        outs.append(jax.ShapeDtypeStruct((N_DEV, len(grp)) + tuple(piece), a.dtype))
    return flat, outs


def _rows_of(entry):
    return entry if isinstance(entry, tuple) else (entry, None)


def _operands(flat):
    return [_rows_of(a)[0] for _, _, a in flat]


def _xchg_sems(n):
    return [pltpu.SemaphoreType.DMA((n, N_DEV - 1)), pltpu.SemaphoreType.DMA((n, N_DEV - 1)),
            pltpu.SemaphoreType.DMA((n,))]


def _xchg_copies(flat, mode, ins, out_refs, sems, waiting=True):
    send_sems, recv_sems, local_sems = sems
    x, y, c = lax.axis_index("x"), lax.axis_index("y"), lax.axis_index("c")
    me = 4 * x + 2 * y + c
    local, first, forwards = [], [], []

    def pair(s, j, dev):
        return dict(send_sem=send_sems.at[s, j], recv_sem=recv_sems.at[s, j], device_id=dev, device_id_type=MESH)

    def block(s, dev):
        rows = _rows_of(flat[s][2])[1]
        ref = ins[s].at[dev]
        return ref if rows is None else ref.at[pl.ds(rows[0], rows[1])]

    for s, (g, l, _) in enumerate(flat):
        src = block(s, me) if mode == "scatter" else ins[s]
        local.append(pltpu.make_async_copy(src, out_refs[g].at[me, l], local_sems.at[s]))
    if mode == "gather2":
        sib, sib_idx = (x, y, 1 - c), 4 * x + 2 * y + (1 - c)
        for s, (g, l, _) in enumerate(flat):
            slot = lambda dev, g=g, l=l: out_refs[g].at[dev, l]
            targets = [(sib, sib_idx)] + [((qx, qy, c), 4 * qx + 2 * qy + c)
                                          for qx, qy in ((1 - x, y), (x, 1 - y), (1 - x, 1 - y))]
            for j, (dev, idx) in enumerate(targets):
                send = pltpu.make_async_remote_copy(src_ref=ins[s], dst_ref=slot(me), **pair(s, j, dev))
                arrive = pltpu.make_async_remote_copy(src_ref=ins[s], dst_ref=slot(idx), **pair(s, j, dev)) if waiting else None
                first.append((send, arrive))
            if waiting:
                for j, (dev, idx) in enumerate(targets[1:]):
                    other = 4 * dev[0] + 2 * dev[1] + (1 - c)
                    send = pltpu.make_async_remote_copy(src_ref=slot(idx), dst_ref=slot(idx), **pair(s, 4 + j, sib))
                    arrive = pltpu.make_async_remote_copy(src_ref=slot(idx), dst_ref=slot(other), **pair(s, 4 + j, sib))
                    forwards.append((first[len(first) - 3 + j][1], send, arrive))
        return local, first, forwards
    for k in range(1, N_DEV):
        px = 1 - x if k & 4 else x
        py = 1 - y if k & 2 else y
        pc = 1 - c if k & 1 else c
        peer = 4 * px + 2 * py + pc
        for s, (g, l, _) in enumerate(flat):
            src = block(s, peer) if mode == "scatter" else ins[s]
            send = pltpu.make_async_remote_copy(src_ref=src, dst_ref=out_refs[g].at[me, l], **pair(s, k - 1, (px, py, pc)))
            arrive = (pltpu.make_async_remote_copy(src_ref=src, dst_ref=out_refs[g].at[peer, l],
                                                   **pair(s, k - 1, (px, py, pc))) if waiting else None)
            first.append((send, arrive))
    return local, first, forwards


def _xchg_start(local, first, forwards):
    for cp in local:
        cp.start()
    for send, _ in first:
        send.start()


def _xchg_wait(local, first, forwards):
    gates = [gate for gate, _, _ in forwards]
    for gate, send, _ in forwards:
        gate.wait_recv()
        send.start()
    for _, arrive in first:
        if not any(arrive is gate for gate in gates):
            arrive.wait_recv()
    for _, _, arrive in forwards:
        arrive.wait_recv()
    for send, _ in first:
        send.wait_send()
    for _, send, _ in forwards:
        send.wait_send()
    for cp in local:
        cp.wait()


def _exchange(groups, mode, name):
    flat, outs = _xchg_plan(groups, mode)
    n = len(flat)

    def body(*refs):
        copies = _xchg_copies(flat, mode, refs[:n], refs[n:n + len(groups)], refs[n + len(groups):])
        _xchg_start(*copies)
        _xchg_wait(*copies)

    res = pl.pallas_call(body, name=name, out_shape=outs, in_specs=[ANY] * n, out_specs=[ANY] * len(groups),
                         scratch_shapes=_xchg_sems(n))(*_operands(flat))
    return list(res)


def _pcall(body, *, name, grid, in_specs, out_specs, out_shape, ins, scratch_shapes=(), sem=None, vmem=VMEM_LIMIT,
           rider=None):
    single = not isinstance(out_shape, (list, tuple))
    if rider is None:
        return pl.pallas_call(body, name=name, grid=grid, in_specs=list(in_specs), out_specs=out_specs,
                              out_shape=out_shape, scratch_shapes=list(scratch_shapes),
                              compiler_params=_params(sem, vmem))(*ins)
    groups, mode = rider
    flat, r_outs = _xchg_plan(groups, mode)
    n_in, n_out = len(ins), 1 if single else len(out_shape)
    nr, ng, ns = len(flat), len(groups), len(scratch_shapes)

    def wrapped(*refs):
        o1 = n_in + nr
        o2 = o1 + n_out
        o3 = o2 + ng
        r_in, r_out, sems = refs[n_in:o1], refs[o2:o3], refs[o3 + ns:]
        first = functools.reduce(jnp.logical_and, [pl.program_id(d) == 0 for d in range(len(grid))])
        last = functools.reduce(jnp.logical_and, [pl.program_id(d) == grid[d] - 1 for d in range(len(grid))])

        @pl.when(first)
        def _():
            _xchg_start(*_xchg_copies(flat, mode, r_in, r_out, sems, waiting=False))

        body(*refs[:n_in], *refs[o1:o2], *refs[o3:o3 + ns])

        @pl.when(last)
        def _():
            _xchg_wait(*_xchg_copies(flat, mode, r_in, r_out, sems))

    outs = pl.pallas_call(
        wrapped, name=name, grid=grid, in_specs=list(in_specs) + [ANY] * nr,
        out_specs=([out_specs] if single else list(out_specs)) + [ANY] * ng,
        out_shape=([out_shape] if single else list(out_shape)) + r_outs,
        scratch_shapes=list(scratch_shapes) + _xchg_sems(nr),
        compiler_params=_params(("arbitrary",) * len(grid), vmem))(*ins, *_operands(flat))
    base = list(outs[:n_out])
    return (base[0] if single else base), list(outs[n_out:])


_DIMS = {"nn": (((1,), (0,)), ((), ())), "nt": (((1,), (1,)), ((), ())), "tn": (((0,), (0,)), ((), ()))}


def _mm(name, a, b, a_spec, b_spec, o_spec, out_shape, grid, dims, rider=None):
    nk = grid[2]
    acc_shape = tuple(d for d in o_spec.block_shape if d is not None)
    dn = _DIMS[dims]

    def tile(ref):
        v = ref[...]
        return v.reshape((-1, v.shape[-1])).astype(_ACT)

    def body(a_ref, b_ref, o_ref, *scratch):
        def part():
            return lax.dot_general(tile(a_ref), tile(b_ref), dn, preferred_element_type=F32)

        if nk == 1:
            o_ref[...] = part().reshape(o_ref.shape).astype(o_ref.dtype)
            return
        acc_ref, = scratch
        k = pl.program_id(2)

        @pl.when(k == 0)
        def _():
            acc_ref[...] = part()

        @pl.when(k > 0)
        def _():
            acc_ref[...] += part()

        @pl.when(k == nk - 1)
        def _():
            o_ref[...] = acc_ref[...].reshape(o_ref.shape).astype(o_ref.dtype)

    acc2d = (math.prod(acc_shape[:-1]), acc_shape[-1])
    return _pcall(body, name=name, out_shape=out_shape, grid=grid, in_specs=[a_spec, b_spec], out_specs=o_spec,
                  scratch_shapes=[] if nk == 1 else [pltpu.VMEM(acc2d, F32)], ins=(a, b),
                  sem=("parallel", "parallel", "arbitrary"), rider=rider)


def _row_tile(n, want):
    t = min(n, want)
    assert n % t == 0, (n, t)
    return t


def _colblock_fwd(name, xa, wg, layer, out_dtype, rider=None):
    L, K = xa.shape
    nb = wg.shape[3]
    half = N_DEV // 2
    tm = _row_tile(L, 512)
    return _mm(name, xa, wg,
               pl.BlockSpec((tm, K), lambda j, i, k: (i, 0)),
               pl.BlockSpec((None, None, K, nb), lambda j, i, k: (j, layer, 0, 0)),
               pl.BlockSpec((None, tm, nb), lambda j, i, k: (j // half, i, j % half)),
               jax.ShapeDtypeStruct((2, L, half * nb), out_dtype), (N_DEV, L // tm, 1), "nn", rider=rider)


def _colblock_dgrad(name, dh, wg, layer, out_dtype, rider=None):
    _, L, _ = dh.shape
    K, nb = wg.shape[2], wg.shape[3]
    half = N_DEV // 2
    tm = _row_tile(L, 512)
    return _mm(name, dh, wg,
               pl.BlockSpec((None, tm, nb), lambda i, j, k: (k // half, i, k % half)),
               pl.BlockSpec((None, None, K, nb), lambda i, j, k: (k, layer, 0, 0)),
               pl.BlockSpec((tm, K), lambda i, j, k: (i, 0)),
               jax.ShapeDtypeStruct((L, K), out_dtype), (L // tm, 1, N_DEV), "nt", rider=rider)


def _colblock_wgrad(name, xa, dh, rider=None):
    L, K = xa.shape
    half = N_DEV // 2
    nb = dh.shape[2] // half
    tm = _row_tile(K, 512)
    tk = L
    return _mm(name, xa, dh,
               pl.BlockSpec((tk, tm), lambda j, i, k: (k, i)),
               pl.BlockSpec((None, tk, nb), lambda j, i, k: (j // half, k, j % half)),
               pl.BlockSpec((None, tm, nb), lambda j, i, k: (j, i, 0)),
               jax.ShapeDtypeStruct((N_DEV, K, nb), _ACT), (N_DEV, K // tm, L // tk), "tn", rider=rider)


def _rowblock_fwd(name, xa, wg, layer, rider=None):
    L, FF = xa.shape
    r, D = wg.shape[2], wg.shape[3]
    tm = _row_tile(L, 512)
    return _mm(name, xa, wg,
               pl.BlockSpec((tm, 2 * r), lambda i, j, k: (i, k)),
               pl.BlockSpec((2, None, r, D), lambda i, j, k: (k, layer, 0, 0)),
               pl.BlockSpec((tm, D), lambda i, j, k: (i, 0)),
               jax.ShapeDtypeStruct((L, D), F32), (L // tm, 1, N_DEV // 2), "nn", rider=rider)


def _rowblock_dgrad(name, dy, wg, layer, rider=None):
    L, D = dy.shape
    r = wg.shape[2]
    tm = _row_tile(L, 512)
    return _mm(name, dy, wg,
               pl.BlockSpec((tm, D), lambda i, j, k: (i, 0)),
               pl.BlockSpec((2, None, r, D), lambda i, j, k: (j, layer, 0, 0)),
               pl.BlockSpec((tm, 2 * r), lambda i, j, k: (i, j)),
               jax.ShapeDtypeStruct((L, N_DEV * r), _ACT), (L // tm, N_DEV // 2, 1), "nt", rider=rider)


def _rowblock_wgrad(name, xa, dy, rider=None):
    L, FF = xa.shape
    D = dy.shape[1]
    tm = FF // (N_DEV // 2)
    tn = _row_tile(D, 1024)
    tk = _row_tile(L, 2048)
    return _mm(name, xa, dy,
               pl.BlockSpec((tk, tm), lambda i, j, k: (k, i)),
               pl.BlockSpec((tk, tn), lambda i, j, k: (k, j)),
               pl.BlockSpec((tm, tn), lambda i, j, k: (i, j)),
               jax.ShapeDtypeStruct((FF, D), _ACT), (FF // tm, D // tn, L // tk), "tn", rider=rider)


def _group_mm(name, xa, w, dims, out_dtype):
    L, D = xa.shape
    ng, pc, _ = w.shape
    tm = _row_tile(L, 512)
    return _mm(name, xa, w,
               pl.BlockSpec((tm, pc), lambda i, g, k: (i, g)),
               pl.BlockSpec((None, pc, pc), lambda i, g, k: (g, 0, 0)),
               pl.BlockSpec((tm, pc), lambda i, g, k: (i, g)),
               jax.ShapeDtypeStruct((L, D), out_dtype), (L // tm, ng, 1), dims)


def _group_wgrad(name, p, dy, ng):
    L, D = p.shape
    pc = D // ng
    tk = _row_tile(L, 512)
    return _mm(name, p, dy,
               pl.BlockSpec((tk, pc), lambda g, j, k: (k, g)),
               pl.BlockSpec((tk, pc), lambda g, j, k: (k, g)),
               pl.BlockSpec((None, pc, pc), lambda g, j, k: (g, 0, 0)),
               jax.ShapeDtypeStruct((ng, pc, pc), _ACT), (ng, 1, L // tk), "tn")


V_GPOST, V_GATE, V_YSCALE, V_GPRE, V_SHIFT, V_SCALE = range(6)
R_SHIFT, R_SCALE, R_GPRE, R_GATE, R_GPOST, R_YSCALE = range(6)
ROW_TILE = 256


def _rstd(v):
    return lax.rsqrt(jnp.mean(v * v, axis=-1, keepdims=True) + RMS_EPS)


def _post_norm_bwd(dyh, yh, ry):
    return ry * (dyh - yh * jnp.mean(dyh * yh, axis=-1, keepdims=True))


def _rows_fwd(name, xres, y, vec, *, add=False, target=None, want_x=True, u_dtype=None):
    L, D = xres.shape
    tm = _row_tile(L, ROW_TILE)
    has_y = y is not None
    last = target is not None
    has_u = u_dtype is not None

    def body(*refs):
        refs = list(refs)
        xres_ref = refs.pop(0)
        y_ref = refs.pop(0) if has_y else None
        vec_ref = refs.pop(0)
        tgt_ref = refs.pop(0) if last else None
        xnew = xres_ref[...]
        if has_y and add:
            xnew = xnew + y_ref[...]
        elif has_y:
            ye = y_ref[...] * vec_ref[V_YSCALE:V_YSCALE + 1, :]
            ry = _rstd(ye)
            yh = ye * ry
            gpost, gate = vec_ref[V_GPOST:V_GPOST + 1, :], vec_ref[V_GATE:V_GATE + 1, :]
            xnew = xnew + gate * (yh * gpost)
        if last:
            dx_ref, loss_ref, dy_ref, red_ref = refs
            diff = xnew - tgt_ref[...]
            dxn = diff * (1.0 / D)
            dx_ref[...] = dxn

            @pl.when(pl.program_id(0) == 0)
            def _():
                loss_ref[...] = jnp.zeros_like(loss_ref)
                red_ref[...] = jnp.zeros_like(red_ref)

            loss_ref[...] += jnp.sum(diff * diff) * (0.5 / D)
            drn2 = dxn * gate
            red_ref[R_GATE:R_GATE + 1, :] += jnp.sum(dxn * (yh * gpost), axis=0, keepdims=True)
            red_ref[R_GPOST:R_GPOST + 1, :] += jnp.sum(drn2 * yh, axis=0, keepdims=True)
            dye = _post_norm_bwd(drn2 * gpost, yh, ry)
            red_ref[R_YSCALE:R_YSCALE + 1, :] += jnp.sum(dye * y_ref[...], axis=0, keepdims=True)
            dy_ref[...] = (dye * vec_ref[V_YSCALE:V_YSCALE + 1, :]).astype(dy_ref.dtype)
            return
        if want_x:
            refs.pop(0)[...] = xnew
        if has_u:
            u_ref, = refs
            n = xnew * _rstd(xnew) * vec_ref[V_GPRE:V_GPRE + 1, :]
            u_ref[...] = (n * (1.0 + vec_ref[V_SCALE:V_SCALE + 1, :]) + vec_ref[V_SHIFT:V_SHIFT + 1, :]).astype(u_ref.dtype)

    row = pl.BlockSpec((tm, D), lambda i: (i, 0))
    vspec = pl.BlockSpec((8, D), lambda i: (0, 0))
    ins, in_specs = [xres], [row]
    if has_y:
        ins.append(y)
        in_specs.append(row)
    ins.append(vec)
    in_specs.append(vspec)
    out_shape, out_specs = [], []
    if last:
        ins.append(target)
        in_specs.append(row)
        out_shape = [jax.ShapeDtypeStruct((L, D), F32), jax.ShapeDtypeStruct((8, LANE), F32),
                     jax.ShapeDtypeStruct((L, D), _ACT), jax.ShapeDtypeStruct((8, D), F32)]
        out_specs = [row, pl.BlockSpec((8, LANE), lambda i: (0, 0)), row, vspec]
    else:
        if want_x:
            out_shape.append(jax.ShapeDtypeStruct((L, D), F32))
            out_specs.append(row)
        if has_u:
            out_shape.append(jax.ShapeDtypeStruct((L, D), u_dtype))
            out_specs.append(row)
    return pl.pallas_call(body, name=name, out_shape=out_shape, grid=(L // tm,), in_specs=in_specs,
                          out_specs=out_specs, compiler_params=_params(("arbitrary",)))(*ins)


def _rows_bwd(name, dxd, du, xnew, y, vec, dy_dtype=F32, want_dx=True):
    L, D = xnew.shape if xnew is not None else dxd.shape
    tm = _row_tile(L, ROW_TILE)
    has_dxd, has_pre, has_post = dxd is not None, du is not None, y is not None

    def body(*refs):
        refs = list(refs)
        dxd_ref = refs.pop(0) if has_dxd else None
        du_ref = refs.pop(0) if has_pre else None
        xnew_ref = refs.pop(0) if has_pre else None
        y_ref = refs.pop(0) if has_post else None
        vec_ref = refs.pop(0)
        dx_ref = refs.pop(0) if want_dx else None
        dy_ref = refs.pop(0) if has_post else None
        red_ref, = refs

        @pl.when(pl.program_id(0) == 0)
        def _():
            red_ref[...] = jnp.zeros_like(red_ref)

        def acc(rw, val):
            red_ref[rw:rw + 1, :] += jnp.sum(val, axis=0, keepdims=True)

        dxn = dxd_ref[...] if has_dxd else None
        if has_pre:
            xn = xnew_ref[...]
            r = _rstd(xn)
            nh = xn * r
            gpre = vec_ref[V_GPRE:V_GPRE + 1, :]
            dub = du_ref[...].astype(F32)
            acc(R_SHIFT, dub)
            acc(R_SCALE, dub * (nh * gpre))
            drn = dub * (1.0 + vec_ref[V_SCALE:V_SCALE + 1, :])
            acc(R_GPRE, drn * nh)
            dnh = drn * gpre
            t = r * (dnh - nh * jnp.mean(dnh * nh, axis=-1, keepdims=True))
            dxn = t if dxn is None else dxn + t
        if want_dx:
            dx_ref[...] = dxn
        if has_post:
            ye = y_ref[...] * vec_ref[V_YSCALE:V_YSCALE + 1, :]
            ry = _rstd(ye)
            yh = ye * ry
            gpost = vec_ref[V_GPOST:V_GPOST + 1, :]
            acc(R_GATE, dxn * (yh * gpost))
            drn2 = dxn * vec_ref[V_GATE:V_GATE + 1, :]
            acc(R_GPOST, drn2 * yh)
            dye = _post_norm_bwd(drn2 * gpost, yh, ry)
            acc(R_YSCALE, dye * y_ref[...])
            dy_ref[...] = (dye * vec_ref[V_YSCALE:V_YSCALE + 1, :]).astype(dy_ref.dtype)

    row = pl.BlockSpec((tm, D), lambda i: (i, 0))
    vspec = pl.BlockSpec((8, D), lambda i: (0, 0))
    ins, in_specs = [], []
    for a in ([dxd] if has_dxd else []) + ([du, xnew] if has_pre else []) + ([y] if has_post else []):
        ins.append(a)
        in_specs.append(row)
    ins.append(vec)
    in_specs.append(vspec)
    out_shape, out_specs = [], []
    if want_dx:
        out_shape.append(jax.ShapeDtypeStruct((L, D), F32))
        out_specs.append(row)
    if has_post:
        out_shape.append(jax.ShapeDtypeStruct((L, D), dy_dtype))
        out_specs.append(row)
    out_shape.append(jax.ShapeDtypeStruct((8, D), F32))
    out_specs.append(vspec)
    return pl.pallas_call(body, name=name, out_shape=out_shape, grid=(L // tm,), in_specs=in_specs,
                          out_specs=out_specs, compiler_params=_params(("arbitrary",)))(*ins)


def _sigmoid(v):
    return 1.0 / (1.0 + jnp.exp(-v))


def _glu_fwd(name, vg):
    _, L, D = vg.shape
    tm = _row_tile(L, ROW_TILE)

    def body(vg_ref, o_ref):
        o_ref[...] = vg_ref[0].astype(F32) * _sigmoid(vg_ref[1].astype(F32))

    return pl.pallas_call(body, name=name, grid=(L // tm,),
                          in_specs=[pl.BlockSpec((2, tm, D), lambda i: (0, i, 0))],
                          out_specs=pl.BlockSpec((tm, D), lambda i: (i, 0)),
                          out_shape=jax.ShapeDtypeStruct((L, D), F32),
                          compiler_params=_params(("parallel",)))(vg)


def _glu_bwd(name, vg, dout):
    _, L, D = vg.shape
    tm = _row_tile(L, ROW_TILE)

    def body(vg_ref, d_ref, o_ref):
        val, s = vg_ref[0].astype(F32), _sigmoid(vg_ref[1].astype(F32))
        d = d_ref[...]
        o_ref[0] = (d * s).astype(o_ref.dtype)
        o_ref[1] = (d * val * s * (1.0 - s)).astype(o_ref.dtype)

    return pl.pallas_call(body, name=name, grid=(L // tm,),
                          in_specs=[pl.BlockSpec((2, tm, D), lambda i: (0, i, 0)), pl.BlockSpec((tm, D), lambda i: (i, 0))],
                          out_specs=pl.BlockSpec((2, tm, D), lambda i: (0, i, 0)),
                          out_shape=jax.ShapeDtypeStruct((2, L, D), _ACT),
                          compiler_params=_params(("parallel",)))(vg, dout)


CONV_ROWS = 256


def _row_pick(blk, idx):
    rows = lax.broadcasted_iota(jnp.int32, blk.shape, 0)
    return jnp.sum(jnp.where(rows == idx, blk, 0.0), axis=0, keepdims=True)


def _shifted(ref, r0, rc, L):
    cur = ref[pl.ds(r0, rc), :].astype(F32)
    before = ref[pl.ds(pl.multiple_of(jnp.maximum(r0 - 16, 0), 16), 16), :].astype(F32)
    after = ref[pl.ds(pl.multiple_of(jnp.minimum(r0 + rc, L - 16), 16), 16), :].astype(F32)
    prev_row = jnp.where(r0 > 0, _row_pick(before, 15), 0.0)
    next_row = jnp.where(r0 + rc < L, _row_pick(after, 0), 0.0)
    rows = lax.broadcasted_iota(jnp.int32, cur.shape, 0)
    up = jnp.where(rows == 0, prev_row, pltpu.roll(cur, 1, 0))
    down = jnp.where(rows == rc - 1, next_row, pltpu.roll(cur, rc - 1, 0))
    return up, cur, down


def _silu_parts(g):
    s = _sigmoid(g)
    return g * s, s


def _conv_swiglu_fwd(name, h, cw, rider=None):
    _, L, FF = h.shape
    rc = _row_tile(L, CONV_ROWS)

    def body(h_ref, cw_ref, o_ref):
        def chunk(ci, _):
            r0 = pl.multiple_of(ci * rc, rc)
            hc = []
            for half in range(2):
                up, cur, down = _shifted(h_ref.at[half], r0, rc, L)
                hc.append(up * cw_ref[half, 0:1, :] + cur * cw_ref[half, 1:2, :] + down * cw_ref[half, 2:3, :]
                          + cw_ref[half, 3:4, :])
            o_ref[pl.ds(r0, rc), :] = (_silu_parts(hc[1])[0] * hc[0]).astype(o_ref.dtype)
            return 0

        lax.fori_loop(0, L // rc, chunk, 0)

    return _pcall(body, name=name, grid=(FF // LANE,),
                  in_specs=[pl.BlockSpec((2, L, LANE), lambda j: (0, 0, j)),
                            pl.BlockSpec((2, 8, LANE), lambda j: (0, 0, j))],
                  out_specs=pl.BlockSpec((L, LANE), lambda j: (0, j)),
                  out_shape=jax.ShapeDtypeStruct((L, FF), _ACT), ins=(h, cw), sem=("parallel",), rider=rider)


def _conv_swiglu_bwd(name, h, cw, dact, rider=None):
    _, L, FF = h.shape
    rc = _row_tile(L, CONV_ROWS)

    def body(h_ref, cw_ref, da_ref, dh_ref, dcw_ref, dhc_ref):
        def chunk(ci, acc):
            r0 = pl.multiple_of(ci * rc, rc)
            taps, hc = [], []
            for half in range(2):
                t = _shifted(h_ref.at[half], r0, rc, L)
                taps.append(t)
                hc.append(t[0] * cw_ref[half, 0:1, :] + t[1] * cw_ref[half, 1:2, :] + t[2] * cw_ref[half, 2:3, :]
                          + cw_ref[half, 3:4, :])
            d = da_ref[pl.ds(r0, rc), :].astype(F32)
            act, s = _silu_parts(hc[1])
            dhc = (d * act, d * hc[0] * (s + act * (1.0 - s)))
            new = []
            for half in range(2):
                dhc_ref[half, pl.ds(r0, rc), :] = dhc[half]
                for k in range(3):
                    new.append(acc[4 * half + k] + jnp.sum(dhc[half] * taps[half][k], axis=0, keepdims=True))
                new.append(acc[4 * half + 3] + jnp.sum(dhc[half], axis=0, keepdims=True))
            return tuple(new)

        zero = jnp.zeros((1, LANE), F32)
        acc = lax.fori_loop(0, L // rc, chunk, (zero,) * 8)
        dcw_ref[...] = jnp.zeros_like(dcw_ref)
        for half in range(2):
            for k in range(4):
                dcw_ref[half, k:k + 1, :] = acc[4 * half + k]

        def chunk2(ci, _):
            r0 = pl.multiple_of(ci * rc, rc)
            for half in range(2):
                up, cur, down = _shifted(dhc_ref.at[half], r0, rc, L)
                dh_ref[half, pl.ds(r0, rc), :] = (down * cw_ref[half, 0:1, :] + cur * cw_ref[half, 1:2, :]
                                                  + up * cw_ref[half, 2:3, :]).astype(dh_ref.dtype)
            return 0

        lax.fori_loop(0, L // rc, chunk2, 0)

    return _pcall(body, name=name, grid=(FF // LANE,),
                  in_specs=[pl.BlockSpec((2, L, LANE), lambda j: (0, 0, j)),
                            pl.BlockSpec((2, 8, LANE), lambda j: (0, 0, j)),
                            pl.BlockSpec((L, LANE), lambda j: (0, j))],
                  out_specs=[pl.BlockSpec((2, L, LANE), lambda j: (0, 0, j)),
                             pl.BlockSpec((2, 8, LANE), lambda j: (0, 0, j))],
                  out_shape=[jax.ShapeDtypeStruct((2, L, FF), _ACT), jax.ShapeDtypeStruct((2, 8, FF), F32)],
                  scratch_shapes=[pltpu.VMEM((2, L, LANE), F32)], ins=(h, cw, dact), sem=("parallel",), rider=rider)


POOL_ROWS = 256
POOL_TILE = 256


def _pool_bands(transpose):
    i = jnp.arange(POOL_ROWS)[:, None]
    j = jnp.arange(POOL_ROWS + 2 * POOL_HALO)[None, :] - POOL_HALO
    bands = []
    for w in POOL_WINDOWS:
        lo, hi = (-(w // 2 - 1), w // 2) if transpose else (-(w // 2), w // 2 - 1)
        bands.append(((j - i >= lo) & (j - i <= hi)).astype(_ACT))
    return jnp.stack(bands)


def _pool_window(name, u, transpose, out_dtype):
    L, D = u.shape
    ng = len(POOL_WINDOWS)
    pc = D // ng
    tn = min(POOL_TILE, pc)
    rc = _row_tile(L, POOL_ROWS)
    bands = _pool_bands(transpose)
    if rc != POOL_ROWS:
        bands = bands[:, :rc, :rc + 2 * POOL_HALO]
    halo = POOL_HALO

    def body(u_ref, band_ref, o_ref, hi_ref, lo_ref):
        g = (pl.program_id(0) * tn) // pc
        half = jnp.zeros((1, 1), jnp.int32)
        for k, w in enumerate(POOL_WINDOWS):
            half = jnp.where(g == k, w // 2, half)
        zeros = jnp.zeros((halo, tn), _ACT)
        for ref in (hi_ref, lo_ref):
            ref[0:halo, :] = zeros
            ref[halo + L:2 * halo + L, :] = zeros

        def inv_count(r0):
            t = r0 + lax.broadcasted_iota(jnp.int32, (rc, tn), 0)
            lo = jnp.clip(t - half, 0, L - 1)
            hi = jnp.clip(t + half - 1, 0, L - 1)
            return 1.0 / (hi - lo + 1).astype(F32)

        def split(ci, _):
            r0 = pl.multiple_of(ci * rc, rc)
            v = u_ref[pl.ds(r0, rc), :].astype(F32)
            if transpose:
                v = v * inv_count(r0)
            hi = v.astype(_ACT)
            dst = pl.ds(pl.multiple_of(r0 + halo, halo), rc)
            hi_ref[dst, :] = hi
            lo_ref[dst, :] = (v - hi.astype(F32)).astype(_ACT)
            return 0

        lax.fori_loop(0, L // rc, split, 0)
        band = band_ref[...]

        def chunk(ci, _):
            r0 = pl.multiple_of(ci * rc, rc)
            win = pl.ds(r0, rc + 2 * halo)
            s = (jnp.dot(band, hi_ref[win, :], preferred_element_type=F32)
                 + jnp.dot(band, lo_ref[win, :], preferred_element_type=F32))
            if not transpose:
                s = s * inv_count(r0)
            o_ref[pl.ds(r0, rc), :] = (s - u_ref[pl.ds(r0, rc), :].astype(F32)).astype(o_ref.dtype)
            return 0

        lax.fori_loop(0, L // rc, chunk, 0)

    return pl.pallas_call(body, name=name, grid=(D // tn,),
                          in_specs=[pl.BlockSpec((L, tn), lambda j: (0, j)),
                                    pl.BlockSpec((None, rc, rc + 2 * halo), lambda j: ((j * tn) // pc, 0, 0))],
                          out_specs=pl.BlockSpec((L, tn), lambda j: (0, j)),
                          out_shape=jax.ShapeDtypeStruct((L, D), out_dtype),
                          scratch_shapes=[pltpu.VMEM((L + 2 * halo, tn), _ACT), pltpu.VMEM((L + 2 * halo, tn), _ACT)],
                          compiler_params=_params(("parallel",)))(u, bands)


S5_ROWS = 512


def _slab(start):
    return pl.ds(start if isinstance(start, int) else pl.multiple_of(start, NSEG), NSEG)


def _cmul(ar, ai, br, bi):
    return ar * br - ai * bi, ar * bi + ai * br


def _cpow(ar, ai, n):
    rr, ri = None, None
    br, bi = ar, ai
    while n:
        if n & 1:
            rr, ri = (br, bi) if rr is None else _cmul(rr, ri, br, bi)
        n >>= 1
        if n:
            br, bi = _cmul(br, bi, br, bi)
    return rr, ri


def _pow_table(pw_ref, ar, ai, n):
    W = ar.shape[1]
    pr, pi = ar, ai
    for r in range(NSEG):
        pw_ref[0, r:r + 1, :] = pr
        pw_ref[1, r:r + 1, :] = pi
        if r < NSEG - 1:
            pr, pi = _cmul(pr, pi, ar, ai)
    a8r, a8i = (jnp.broadcast_to(v, (NSEG, W)) for v in _cpow(ar, ai, NSEG))

    def step(k, carry):
        nr, ni = _cmul(carry[0], carry[1], a8r, a8i)
        pw_ref[0, _slab(k * NSEG), :] = nr
        pw_ref[1, _slab(k * NSEG), :] = ni
        return nr, ni

    lax.fori_loop(1, n // NSEG, step, (pw_ref[0, 0:NSEG, :], pw_ref[1, 0:NSEG, :]))


def _seg_scan(sr_ref, si_ref, tmp_ref, pw_ref, row0, n, ar, ai, h0, rev, conj=False, pair_with=None):
    W = ar.shape[1]
    arb, aib = jnp.broadcast_to(ar, (NSEG, W)), jnp.broadcast_to(ai, (NSEG, W))

    def rows(s):
        t = (n - 1 - s) if rev else s
        return _slab(row0 + t * NSEG)

    def step(s, carry):
        hr, hi = carry
        sl = rows(s)
        nr = arb * hr - aib * hi + sr_ref[sl, :]
        ni = arb * hi + aib * hr + si_ref[sl, :]
        sr_ref[sl, :] = nr
        si_ref[sl, :] = ni
        return nr, ni

    zero = jnp.zeros((NSEG, W), F32)
    fr, fi = lax.fori_loop(0, n, step, (zero, zero), unroll=2)
    tmp_ref[0] = fr
    tmp_ref[1] = fi
    anr, ani = _cpow(ar, ai, n)
    cr, ci = h0
    for j in (range(NSEG - 1, -1, -1) if rev else range(NSEG)):
        tmp_ref[2, j:j + 1, :] = cr
        tmp_ref[3, j:j + 1, :] = ci
        pr, pi = _cmul(anr, ani, cr, ci)
        cr, ci = tmp_ref[0, j:j + 1, :] + pr, tmp_ref[1, j:j + 1, :] + pi
    cmr, cmi = tmp_ref[2], tmp_ref[3]

    def fix(k, acc):
        for r in range(NSEG):
            row = pl.ds(pl.multiple_of(k * NSEG, NSEG) + r, 1)
            pr = jnp.broadcast_to(pw_ref[0, row, :], (NSEG, W))
            pi = jnp.broadcast_to(pw_ref[1, row, :], (NSEG, W))
            if conj:
                pi = -pi
            s = k * NSEG + r
            sl = rows(s)
            gr = sr_ref[sl, :] + (pr * cmr - pi * cmi)
            gi = si_ref[sl, :] + (pr * cmi + pi * cmr)
            sr_ref[sl, :] = gr
            si_ref[sl, :] = gi
            if pair_with is not None:
                h_r, h_i, (hcr, hci) = pair_with
                prev = rows(jnp.minimum(s + 1, n - 1))
                first = s == n - 1
                hpr = jnp.where(first, hcr, h_r[prev, :])
                hpi = jnp.where(first, hci, h_i[prev, :])
                acc = (acc[0] + hpr * gr + hpi * gi, acc[1] + hpr * gi - hpi * gr)
        return acc

    zero_acc = (zero, zero) if pair_with is not None else 0
    total = lax.fori_loop(0, n // NSEG, fix, zero_acc)
    return (cr, ci), (cmr, cmi), (total if pair_with is not None else None)


def _gelu_tanh(y):
    k = math.sqrt(2.0 / math.pi)
    t = jnp.tanh(k * (y + 0.044715 * y * y * y))
    return 0.5 * y * (1.0 + t), t


def _s5_chunks(L):
    rc = _row_tile(L, S5_ROWS)
    return [(r, rc) for r in range(0, L, rc)]


_NT_DIMS = (((1,), (1,)), ((), ()))
_TN_DIMS = (((0,), (0,)), ((), ()))
_LOG_P, _LOG_CH = S5_P.bit_length() - 1, S5_CH.bit_length() - 1


def _same_group(shape, row_shift, col_shift):
    rows = lax.broadcasted_iota(jnp.int32, shape, 0)
    cols = lax.broadcasted_iota(jnp.int32, shape, 1)
    return lax.shift_right_logical(rows, row_shift) == lax.shift_right_logical(cols, col_shift)


def _s5_bt(bt):
    full = jnp.concatenate([bt] * S5_TILE_G, axis=0)
    return jnp.where(_same_group(full.shape, _LOG_P, _LOG_CH), full, 0.0).astype(_ACT)


def _s5_ct(ct):
    full = jnp.concatenate([ct] * S5_TILE_G, axis=0)
    return jnp.where(_same_group(full.shape, _LOG_CH, _LOG_P), full, 0.0).astype(_ACT)


def _s5_diag(m):
    m = jnp.where(_same_group(m.shape, _LOG_CH, _LOG_P), m, 0.0)
    rows = lax.broadcasted_iota(jnp.int32, (S5_TILE_W, S5_P), 0)
    cols = lax.broadcasted_iota(jnp.int32, (S5_TILE_W, S5_P), 1)
    pick = (jnp.bitwise_and(rows, S5_P - 1) == cols).astype(_ACT)
    hi = m.astype(_ACT)
    lo = (m - hi.astype(F32)).astype(_ACT)
    return jnp.dot(hi, pick, preferred_element_type=F32) + jnp.dot(lo, pick, preferred_element_type=F32)


def _s5_project(u_ref, uc_ref, bre, bim, sr_ref, si_ref, L, LC):
    for ref, base, n in ((u_ref, 0, L), (uc_ref, L, LC)):
        for r, rc in _s5_chunks(n):
            ub = ref[r:r + rc, :].astype(_ACT)
            sr_ref[base + r:base + r + rc, :] = lax.dot_general(ub, bre, _NT_DIMS, preferred_element_type=F32)
            si_ref[base + r:base + r + rc, :] = lax.dot_general(ub, bim, _NT_DIMS, preferred_element_type=F32)


def _s5_states(sr_ref, si_ref, tmp_ref, pw_ref, ar, ai, L, LC, rev):
    W = ar.shape[1]
    zero = (jnp.zeros((1, W), F32), jnp.zeros((1, W), F32))
    hctx, cm_ctx, _ = _seg_scan(sr_ref, si_ref, tmp_ref, pw_ref, L, LC // NSEG, ar, ai, zero, rev)
    _, cm_lat, _ = _seg_scan(sr_ref, si_ref, tmp_ref, pw_ref, 0, L // NSEG, ar, ai, hctx, rev)
    return cm_lat, cm_ctx


def _s5_fwd(u, uc, bblk, cblk, apar, dsk, rider=None):
    L, D = u.shape
    LC = uc.shape[0]
    NT, W, TC = D // S5_TILE_CH, S5_TILE_W, S5_TILE_CH

    def body(u_ref, uc_ref, b_ref, c_ref, a_ref, d_ref, y_ref, z_ref, sr_ref, si_ref, tmp_ref, pw_ref):
        for r, rc in _s5_chunks(L):
            y_ref[r:r + rc, :] = u_ref[r:r + rc, :].astype(F32) * d_ref[0:1, :]
        for d in range(2):
            ar, ai = a_ref[2 * d:2 * d + 1, :], a_ref[2 * d + 1:2 * d + 2, :]
            _pow_table(pw_ref, ar, ai, L // NSEG)
            _s5_project(u_ref, uc_ref, _s5_bt(b_ref[2 * d]), _s5_bt(b_ref[2 * d + 1]), sr_ref, si_ref, L, LC)
            _s5_states(sr_ref, si_ref, tmp_ref, pw_ref, ar, ai, L, LC, rev=(d == 1))
            cre, cim = _s5_ct(c_ref[2 * d]), _s5_ct(c_ref[2 * d + 1])
            for r, rc in _s5_chunks(L):
                y_ref[r:r + rc, :] += (
                    lax.dot_general(sr_ref[r:r + rc, :].astype(_ACT), cre, _NT_DIMS, preferred_element_type=F32)
                    - lax.dot_general(si_ref[r:r + rc, :].astype(_ACT), cim, _NT_DIMS, preferred_element_type=F32))
        for r, rc in _s5_chunks(L):
            z_ref[r:r + rc, :] = _gelu_tanh(y_ref[r:r + rc, :])[0].astype(z_ref.dtype)

    col = lambda n: pl.BlockSpec((n, TC), lambda j: (0, j))
    return _pcall(
        body, name="s5_fwd", grid=(NT,), ins=(u, uc, bblk, cblk, apar, dsk), sem=("parallel",), rider=rider,
        in_specs=[col(L), col(LC),
                  pl.BlockSpec((None, 4, S5_P, TC), lambda j: (j, 0, 0, 0)),
                  pl.BlockSpec((None, 4, S5_CH, W), lambda j: (j, 0, 0, 0)),
                  pl.BlockSpec((None, 8, W), lambda j: (j, 0, 0)),
                  pl.BlockSpec((8, TC), lambda j: (0, j))],
        out_specs=[col(L), col(L)],
        out_shape=[jax.ShapeDtypeStruct((L, D), F32), jax.ShapeDtypeStruct((L, D), _ACT)],
        scratch_shapes=[pltpu.VMEM((L + LC, W), F32), pltpu.VMEM((L + LC, W), F32), pltpu.VMEM((4, NSEG, W), F32),
                        pltpu.VMEM((2, L // NSEG, W), F32)])


def _s5_bwd(u, uc, dz, y, bblk, cblk, apar, dsk, rider=None):
    L, D = u.shape
    LC = uc.shape[0]
    NT, W, TC = D // S5_TILE_CH, S5_TILE_W, S5_TILE_CH
    nl, nc = L // NSEG, LC // NSEG

    def body(u_ref, uc_ref, dz_ref, y_ref, b_ref, c_ref, a_ref, d_ref,
             du_ref, duc_ref, db_ref, dc_ref, da_ref, dd_ref,
             hr_ref, hi_ref, gr_ref, gi_ref, dy_ref, tmp_ref, pw_ref):
        ddacc = jnp.zeros((1, TC), F32)
        for r, rc in _s5_chunks(L):
            yv = y_ref[r:r + rc, :]
            g, t = _gelu_tanh(yv)
            k = math.sqrt(2.0 / math.pi)
            dg = 0.5 * (1.0 + t) + 0.5 * yv * (1.0 - t * t) * k * (1.0 + 3 * 0.044715 * yv * yv)
            dy = dz_ref[r:r + rc, :].astype(F32) * dg
            uv = u_ref[r:r + rc, :].astype(F32)
            ddacc = ddacc + jnp.sum(dy * uv, axis=0, keepdims=True)
            du_ref[r:r + rc, :] = dy * d_ref[0:1, :]
            dy_ref[r:r + rc, :] = dy.astype(dy_ref.dtype)
        dd_ref[...] = jnp.zeros_like(dd_ref)
        dd_ref[0:1, :] = ddacc
        duc_ref[...] = jnp.zeros_like(duc_ref)
        da_ref[...] = jnp.zeros_like(da_ref)
        tn = _TN_DIMS
        for d in range(2):
            rev = d == 1
            ar, ai = a_ref[2 * d:2 * d + 1, :], a_ref[2 * d + 1:2 * d + 2, :]
            bre, bim = _s5_bt(b_ref[2 * d]), _s5_bt(b_ref[2 * d + 1])
            cre, cim = _s5_ct(c_ref[2 * d]), _s5_ct(c_ref[2 * d + 1])
            _pow_table(pw_ref, ar, ai, nl)
            _s5_project(u_ref, uc_ref, bre, bim, hr_ref, hi_ref, L, LC)
            cm_lat, cm_ctx = _s5_states(hr_ref, hi_ref, tmp_ref, pw_ref, ar, ai, L, LC, rev)
            cml_r, cml_i, cmc_r, cmc_i = cm_lat[0], cm_lat[1], cm_ctx[0], cm_ctx[1]
            dcr = jnp.zeros((TC, W), F32)
            dci = jnp.zeros((TC, W), F32)
            for r, rc in _s5_chunks(L):
                dyb = dy_ref[r:r + rc, :]
                gr_ref[r:r + rc, :] = jnp.dot(dyb, cre, preferred_element_type=F32)
                gi_ref[r:r + rc, :] = -jnp.dot(dyb, cim, preferred_element_type=F32)
                dcr = dcr + lax.dot_general(dyb, hr_ref[r:r + rc, :].astype(_ACT), tn, preferred_element_type=F32)
                dci = dci - lax.dot_general(dyb, hi_ref[r:r + rc, :].astype(_ACT), tn, preferred_element_type=F32)
            dc_ref[2 * d] = _s5_diag(dcr)
            dc_ref[2 * d + 1] = _s5_diag(dci)
            gr_ref[L:L + LC, :] = jnp.zeros((LC, W), F32)
            gi_ref[L:L + LC, :] = jnp.zeros((LC, W), F32)
            zero = (jnp.zeros((1, W), F32), jnp.zeros((1, W), F32))
            glat, _, (lr, li) = _seg_scan(gr_ref, gi_ref, tmp_ref, pw_ref, 0, nl, ar, -ai, zero, not rev, conj=True,
                                          pair_with=(hr_ref, hi_ref, (cml_r, cml_i)))
            _, _, (qr, qi) = _seg_scan(gr_ref, gi_ref, tmp_ref, pw_ref, L, nc, ar, -ai, glat, not rev, conj=True,
                                       pair_with=(hr_ref, hi_ref, (cmc_r, cmc_i)))
            da_ref[2 * d:2 * d + 1, :] = jnp.sum(lr + qr, axis=0, keepdims=True)
            da_ref[2 * d + 1:2 * d + 2, :] = jnp.sum(li + qi, axis=0, keepdims=True)
            dbr = jnp.zeros((TC, W), F32)
            dbi = jnp.zeros((TC, W), F32)
            for ref, oref, base, n in ((u_ref, du_ref, 0, L), (uc_ref, duc_ref, L, LC)):
                for r, rc in _s5_chunks(n):
                    ub = ref[r:r + rc, :].astype(_ACT)
                    gr = gr_ref[base + r:base + r + rc, :].astype(_ACT)
                    gi = gi_ref[base + r:base + r + rc, :].astype(_ACT)
                    dbr = dbr + lax.dot_general(ub, gr, tn, preferred_element_type=F32)
                    dbi = dbi + lax.dot_general(ub, gi, tn, preferred_element_type=F32)
                    oref[r:r + rc, :] += (jnp.dot(gr, bre, preferred_element_type=F32)
                                          + jnp.dot(gi, bim, preferred_element_type=F32))
            db_ref[2 * d] = _s5_diag(dbr)
            db_ref[2 * d + 1] = _s5_diag(dbi)

    col = lambda n: pl.BlockSpec((n, TC), lambda j: (0, j))
    bspec = pl.BlockSpec((None, 4, S5_P, TC), lambda j: (j, 0, 0, 0))
    cspec = pl.BlockSpec((None, 4, S5_CH, W), lambda j: (j, 0, 0, 0))
    gspec = pl.BlockSpec((None, 4, TC, S5_P), lambda j: (j, 0, 0, 0))
    aspec = pl.BlockSpec((None, 8, W), lambda j: (j, 0, 0))
    return _pcall(
        body, name="s5_bwd", grid=(NT,), ins=(u, uc, dz, y, bblk, cblk, apar, dsk), sem=("parallel",),
        vmem=VMEM_LIMIT_BIG, rider=rider,
        in_specs=[col(L), col(LC), col(L), col(L), bspec, cspec, aspec, pl.BlockSpec((8, TC), lambda j: (0, j))],
        out_specs=[col(L), col(LC), gspec, gspec, aspec, pl.BlockSpec((None, 8, TC), lambda j: (j, 0, 0))],
        out_shape=[jax.ShapeDtypeStruct((L, D), F32), jax.ShapeDtypeStruct((LC, D), F32),
                   jax.ShapeDtypeStruct((NT, 4, TC, S5_P), F32), jax.ShapeDtypeStruct((NT, 4, TC, S5_P), F32),
                   jax.ShapeDtypeStruct((NT, 8, W), F32), jax.ShapeDtypeStruct((NT, 8, TC), F32)],
        scratch_shapes=[pltpu.VMEM((L + LC, W), F32), pltpu.VMEM((L + LC, W), F32),
                        pltpu.VMEM((L + LC, W), F32), pltpu.VMEM((L + LC, W), F32),
                        pltpu.VMEM((L, TC), _ACT), pltpu.VMEM((4, NSEG, W), F32), pltpu.VMEM((2, nl, W), F32)])


ADA_ROWS = 16


def _silu_rows(c_ref):
    c = c_ref[...]
    return c * _sigmoid(c)


def _ada_fwd(cmat, ada_w, ada_b):
    nl, D, n = ada_w.shape
    tn = _row_tile(n, 512)

    def body(c_ref, w_ref, b_ref, o_ref):
        a = _silu_rows(c_ref).astype(_ACT)
        o_ref[...] = jnp.dot(a, w_ref[...].astype(_ACT), preferred_element_type=F32) + b_ref[...]

    return pl.pallas_call(body, name="ada_fwd", grid=(nl, n // tn),
                          in_specs=[pl.BlockSpec((ADA_ROWS, D), lambda l, j: (0, 0)),
                                    pl.BlockSpec((None, D, tn), lambda l, j: (l, 0, j)),
                                    pl.BlockSpec((None, 1, tn), lambda l, j: (l, 0, j))],
                          out_specs=pl.BlockSpec((None, ADA_ROWS, tn), lambda l, j: (l, 0, j)),
                          out_shape=jax.ShapeDtypeStruct((nl, ADA_ROWS, n), F32),
                          compiler_params=_params(("parallel", "parallel")))(cmat, ada_w, ada_b)


def _ada_bwd(cmat, ada_w, dm):
    nl, D, n = ada_w.shape
    tn = _row_tile(n, 512)
    nj = n // tn

    def body(c_ref, w_ref, dm_ref, dw_ref, dc_ref):
        c = c_ref[...]
        s = _sigmoid(c)
        a = (c * s).astype(_ACT)
        dmb = dm_ref[...].astype(_ACT)
        dw_ref[...] = lax.dot_general(a, dmb, (((0,), (0,)), ((), ())), preferred_element_type=F32)
        part = lax.dot_general(dmb, w_ref[...].astype(_ACT), (((1,), (1,)), ((), ())), preferred_element_type=F32)
        part = part * (s * (1.0 + c * (1.0 - s)))

        @pl.when(pl.program_id(1) == 0)
        def _():
            dc_ref[...] = part

        @pl.when(pl.program_id(1) > 0)
        def _():
            dc_ref[...] += part

    return pl.pallas_call(body, name="ada_bwd", grid=(nl, nj),
                          in_specs=[pl.BlockSpec((ADA_ROWS, D), lambda l, j: (0, 0)),
                                    pl.BlockSpec((None, D, tn), lambda l, j: (l, 0, j)),
                                    pl.BlockSpec((None, ADA_ROWS, tn), lambda l, j: (l, 0, j))],
                          out_specs=[pl.BlockSpec((None, D, tn), lambda l, j: (l, 0, j)),
                                     pl.BlockSpec((None, ADA_ROWS, D), lambda l, j: (l, 0, 0))],
                          out_shape=[jax.ShapeDtypeStruct((nl, D, n), F32), jax.ShapeDtypeStruct((nl, ADA_ROWS, D), F32)],
                          compiler_params=_params(("parallel", "arbitrary")))(cmat, ada_w, dm)


def _adamw(name, gparts, w, m, v):
    nl, R, C = w.shape
    gparts = [g if isinstance(g, tuple) else (g, 0) for g in gparts]
    n = gparts[0][0].shape[0]
    runs = []
    for l, (g, r0) in enumerate(gparts):
        if runs and runs[-1][0] is g and runs[-1][1] + runs[-1][3] * R == r0:
            runs[-1][3] += 1
        else:
            runs.append([g, r0, l, 1])
    run_of = [q for q, run in enumerate(runs) for _ in range(run[3])]
    tr = R
    part_bytes = len(runs) * n * C * gparts[0][0].dtype.itemsize * 2
    for cand in (4096, 2048, 1024, 512, 256, 128, 64, 32, 16, 8):
        if R % cand == 0 and cand * max(C, LANE) * 4 <= 2 * 1024 * 1024 and cand * part_bytes <= VMEM_LIMIT // 2:
            tr = cand
            break
    nt = R // tr
    bc1 = 1.0 - ADAM_B1 ** ADAM_STEP
    bc2 = 1.0 - ADAM_B2 ** ADAM_STEP

    def body(*refs):
        g_refs = refs[:len(runs)]
        w_ref, m_ref, v_ref, go_ref, d_ref, mo_ref, vo_ref = refs[len(runs):]
        for l in range(nl):
            @pl.when(pl.program_id(0) == l)
            def _(g_ref=g_refs[run_of[l]]):
                g = g_ref[0].astype(F32)
                for j in range(1, n):
                    g = g + g_ref[j].astype(F32)
                m2 = ADAM_B1 * m_ref[...] + (1.0 - ADAM_B1) * g
                v2 = ADAM_B2 * v_ref[...] + (1.0 - ADAM_B2) * (g * g)
                go_ref[...] = g
                mo_ref[...] = m2
                vo_ref[...] = v2
                d_ref[...] = -ADAM_LR * ((m2 / bc1) / (jnp.sqrt(v2 / bc2) + ADAM_EPS) + ADAM_WD * w_ref[...])

    def gspec(run):
        _, r0, l0, count = run
        return pl.BlockSpec((n, tr, C),
                            lambda lyr, i: (0, r0 // tr + jnp.clip((lyr - l0) * nt + i, 0, count * nt - 1), 0))

    row = pl.BlockSpec((None, tr, C), lambda lyr, i: (lyr, i, 0))
    out = jax.ShapeDtypeStruct((nl, R, C), F32)
    return _pcall(body, name=name, grid=(nl, nt), in_specs=[gspec(run) for run in runs] + [row, row, row],
                  out_specs=[row, row, row, row], out_shape=[out, out, out, out],
                  ins=(*[run[0] for run in runs], w, m, v), sem=("arbitrary", "arbitrary"))


def _sum_parts(name, parts):
    n, R, C = parts.shape

    def body(p_ref, o_ref):
        s = p_ref[0]
        for j in range(1, n):
            s = s + p_ref[j]
        o_ref[...] = s

    return pl.pallas_call(body, name=name, out_shape=jax.ShapeDtypeStruct((R, C), F32),
                          compiler_params=_params(None))(parts)


def _discretize(lam_re, lam_im, log_step, b_re, b_im):
    dt = jnp.exp(log_step)[:, None]
    mag = jnp.exp(lam_re * dt)
    abar_re = mag * jnp.cos(lam_im * dt)
    abar_im = mag * jnp.sin(lam_im * dt)
    nr, ni = abar_re - 1.0, abar_im
    den = lam_re * lam_re + lam_im * lam_im
    fr = (nr * lam_re + ni * lam_im) / den
    fi = (ni * lam_re - nr * lam_im) / den
    bbar_re = fr[..., None] * b_re - fi[..., None] * b_im
    bbar_im = fr[..., None] * b_im + fi[..., None] * b_re
    return abar_re, abar_im, bbar_re, bbar_im


def _s5_pack(abar, bbar, cmat):
    G = abar[0][0].shape[0]
    NT = G // S5_TILE_G
    a4 = jnp.stack([abar[d][r] for d in range(2) for r in range(2)]).reshape(4, NT, S5_TILE_W).transpose(1, 0, 2)
    apar = jnp.concatenate([a4, jnp.zeros((NT, 4, S5_TILE_W), F32)], axis=1)
    b4 = jnp.stack([bbar[d][r] for d in range(2) for r in range(2)]).reshape(4, NT, S5_TILE_G, S5_P, S5_CH)
    bt = b4.transpose(1, 0, 3, 2, 4).reshape(NT, 4, S5_P, S5_TILE_CH)
    c4 = jnp.stack([cmat[d][r] for d in range(2) for r in range(2)]).reshape(4, NT, S5_TILE_G, S5_CH, S5_P)
    ct = c4.transpose(1, 0, 3, 2, 4).reshape(NT, 4, S5_CH, S5_TILE_W)
    return apar, bt, ct


def _s5_unpack(dapar, dbd, dcd, G):
    NT = G // S5_TILE_G
    da = dapar[:, :4, :].reshape(NT, 2, 2, S5_TILE_G, S5_P).transpose(1, 2, 0, 3, 4).reshape(2, 2, G, S5_P)
    db = dbd.reshape(NT, 4, S5_TILE_G, S5_CH, S5_P).transpose(1, 0, 2, 4, 3).reshape(2, 2, G, S5_P, S5_CH)
    dc = dcd.reshape(NT, 4, S5_TILE_G, S5_CH, S5_P).transpose(1, 0, 2, 3, 4).reshape(2, 2, G, S5_CH, S5_P)
    return da, db, dc


def _to_segments(a):
    L, D = a.shape
    return a.reshape(NSEG, L // NSEG, D).transpose(1, 0, 2).reshape(L, D)


def _from_segments(a):
    L, D = a.shape
    return a.reshape(L // NSEG, NSEG, D).transpose(1, 0, 2).reshape(L, D)


def _pos_emb(n_tokens, dim):
    rows = n_tokens // GRID_W
    quarter = dim // 4
    omega = 1.0 / (POS_BASE ** (jnp.arange(quarter, dtype=F32) / quarter))

    def enc(p):
        ang = p[:, None] * omega[None, :]
        return jnp.concatenate([jnp.sin(ang), jnp.cos(ang)], axis=-1)

    rtab = enc(jnp.arange(rows, dtype=F32))
    ctab = enc(jnp.arange(GRID_W, dtype=F32))
    return jnp.concatenate([jnp.repeat(rtab, GRID_W, axis=0), jnp.tile(ctab, (rows, 1))], axis=-1)


def _vec(D, **rows):
    names = {"gpost": V_GPOST, "gate": V_GATE, "yscale": V_YSCALE, "gpre": V_GPRE, "shift": V_SHIFT, "scale": V_SCALE}
    out = [jnp.zeros((D,), F32)] * 8
    out[V_YSCALE] = jnp.ones((D,), F32)
    for k, v in rows.items():
        out[names[k]] = v.reshape(D).astype(F32)
    return jnp.stack(out)


def _row0(v, D):
    return jnp.concatenate([v.reshape(1, D).astype(F32), jnp.zeros((7, D), F32)], axis=0)


def _my_block(full, axis, n_local):
    return lax.dynamic_slice_in_dim(full, _my_index() * n_local, n_local, axis)


def kernel(x, c, ctx, c_ctx, ada_w, ada_b, norm_g, s5_lam_re, s5_lam_im, s5_log_step, s5_b_re, s5_b_im, s5_c_re, s5_c_im, s5_d, s5_glu_w, pool_w, pool_scale, ffn_up, ffn_conv, ffn_conv_b, ffn_down, loss_target, m_c_ctx, m_ada_w, m_ada_b, m_norm_g, m_s5_lam_re, m_s5_lam_im, m_s5_log_step, m_s5_b_re, m_s5_b_im, m_s5_c_re, m_s5_c_im, m_s5_d, m_s5_glu_w, m_pool_w, m_pool_scale, m_ffn_up, m_ffn_conv, m_ffn_conv_b, m_ffn_down, v_c_ctx, v_ada_w, v_ada_b, v_norm_g, v_s5_lam_re, v_s5_lam_im, v_s5_log_step, v_s5_b_re, v_s5_b_im, v_s5_c_re, v_s5_c_im, v_s5_d, v_s5_glu_w, v_pool_w, v_pool_scale, v_ffn_up, v_ffn_conv, v_ffn_conv_b, v_ffn_down):
    L, D = x.shape[1], x.shape[2]
    LC = ctx.shape[1]
    G = s5_lam_re.shape[2]
    n_ada = ada_w.shape[2]
    nb_up = ffn_up.shape[2]
    r_down = ffn_down.shape[1]
    FF = N_DEV * r_down
    n_pool = len(POOL_WINDOWS)
    pc = D // n_pool
    pr = pool_w.shape[2]
    ng_loc = norm_g.shape[2]
    me = _my_index()
    axes = ("x", "y", "c")

    up_b = [ffn_up[i].astype(_ACT) for i in range(2)]
    down_b = [ffn_down[i].astype(_ACT) for i in range(2)]
    glu_b = s5_glu_w[0].astype(_ACT)
    pool_b = pool_w[0].reshape(n_pool * pr, pc).astype(_ACT)

    small_loc = jnp.concatenate([c.reshape(-1), norm_g.reshape(-1), pool_scale.reshape(-1), ffn_conv.reshape(-1)])
    n_small = small_loc.shape[0]
    small_g, = _exchange([[jnp.pad(small_loc, (0, (-n_small) % LANE)).reshape(1, -1)]], mode="gather", name="gather_small")
    small_g = small_g.reshape(N_DEV, -1)
    o = 0
    c_all = small_g[:, o:o + D]
    o += D
    ng_all = small_g[:, o:o + 8 * ng_loc].reshape(N_DEV, 2, 4, ng_loc).transpose(1, 2, 0, 3).reshape(2, 4, D)
    o += 8 * ng_loc
    pscale_all = small_g[:, o:o + ng_loc].reshape(D)
    o += ng_loc
    conv_all = small_g[:, o:o + 6 * nb_up].reshape(N_DEV, 2, 3, nb_up).transpose(1, 2, 0, 3).reshape(2, 3, 2 * FF)

    cmat = jnp.concatenate([c_all, c_ctx.reshape(1, D), jnp.zeros((ADA_ROWS - N_DEV - 1, D), F32)], axis=0)
    ada_b_loc = _my_block(ada_b, 1, n_ada).reshape(2, 1, n_ada)
    mods_loc = _ada_fwd(cmat, ada_w, ada_b_loc)
    mods_g, = _exchange([[mods_loc]], mode="gather", name="gather_mods")
    mods_rows = mods_g.reshape(N_DEV, 2, ADA_ROWS, n_ada).transpose(1, 2, 0, 3).reshape(2, ADA_ROWS, 6, D)
    mod = lax.dynamic_index_in_dim(mods_rows, me, axis=1, keepdims=False)
    mod_c = mods_rows[0, N_DEV]

    def disc_all(lr, li, ls, br, bi):
        return [_discretize(lr[d], li[d], ls[d], br[d], bi[d]) for d in range(2)]

    disc, disc_vjp = jax.vjp(disc_all, s5_lam_re[0], s5_lam_im[0], s5_log_step[0], s5_b_re[0], s5_b_im[0])
    apar, bblk, cblk = _s5_pack([(disc[d][0], disc[d][1]) for d in range(2)],
                                [(disc[d][2], disc[d][3]) for d in range(2)],
                                [(s5_c_re[0, d], s5_c_im[0, d]) for d in range(2)])
    dsk = _row0(s5_d[0], D)
    cw = []
    for i in range(2):
        taps = conv_all[i].reshape(3, 2, FF).transpose(1, 0, 2)
        cw.append(jnp.concatenate([taps, ffn_conv_b[i].reshape(2, 1, FF), jnp.zeros((2, 4, FF), F32)], axis=1))

    vecs = {
        "b0": _vec(D, gpre=ng_all[0, 0], shift=mod[0, 0], scale=mod[0, 1]),
        "c0": _vec(D, gpre=ng_all[0, 0], shift=mod_c[0], scale=mod_c[1]),
        "b1": _vec(D, gpost=ng_all[0, 1], gate=mod[0, 2], gpre=ng_all[0, 2], shift=mod[0, 3], scale=mod[0, 4]),
        "b2": _vec(D, gpost=ng_all[0, 3], gate=mod[0, 5], gpre=ng_all[1, 0], shift=mod[1, 0], scale=mod[1, 1]),
        "b3": _vec(D, gpost=ng_all[1, 1], gate=mod[1, 2], yscale=pscale_all, gpre=ng_all[1, 2], shift=mod[1, 3],
                   scale=mod[1, 4]),
        "b4": _vec(D, gpost=ng_all[1, 3], gate=mod[1, 5]),
    }

    x0, u0 = _rows_fwd("rows_fwd_b0", x[0], _pos_emb(L, D), vecs["b0"], add=True, u_dtype=_ACT)
    uc, = _rows_fwd("rows_fwd_ctx", ctx[0], None, vecs["c0"], want_x=False, u_dtype=_ACT)
    u0s, ucs = _to_segments(u0), _to_segments(uc)
    (y_s5, z_s5), (glu_g, up_g0, down_g0) = _s5_fwd(u0s, ucs, bblk, cblk, apar, dsk,
                                                    rider=([[glu_b], [up_b[0]], [down_b[0]]], "gather2"))
    vg = _colblock_fwd("glu_fwd_mm", z_s5, glu_g, 0, _ACT)
    mix0 = _from_segments(_glu_fwd("glu_fwd", vg))
    x1, un0 = _rows_fwd("rows_fwd_b1", x0, mix0, vecs["b1"], u_dtype=_ACT)
    h0, (up_g1,) = _colblock_fwd("ffn0_up", un0, up_g0, 0, _ACT, rider=([[up_b[1]]], "gather2"))
    act0 = _conv_swiglu_fwd("ffn0_conv", h0, cw[0])
    f0, (down_g1, pool_g) = _rowblock_fwd("ffn0_down", act0, down_g0, 0, rider=([[down_b[1]], [pool_b]], "gather2"))
    pool_full = pool_g.reshape(N_DEV, n_pool, pr, pc).transpose(1, 0, 2, 3).reshape(n_pool, pc, pc)
    x2, u1 = _rows_fwd("rows_fwd_b2", x1, f0, vecs["b2"], u_dtype=F32)
    p1 = _pool_window("pool_fwd", u1, False, _ACT)
    ypre1 = _group_mm("pool_fwd_mm", p1, pool_full, "nn", F32)
    x3, un1 = _rows_fwd("rows_fwd_b3", x2, ypre1, vecs["b3"], u_dtype=_ACT)
    h1 = _colblock_fwd("ffn1_up", un1, up_g1, 0, _ACT)
    act1 = _conv_swiglu_fwd("ffn1_conv", h1, cw[1])
    f1 = _rowblock_fwd("ffn1_down", act1, down_g1, 0)
    dx4, loss_blk, df1, red4 = _rows_fwd("rows_fwd_b4", x3, f1, vecs["b4"], target=loss_target[0])
    loss = lax.psum(loss_blk[0, 0], axes)

    dact1 = _rowblock_dgrad("ffn1_down_dgrad", df1, down_g1, 0)
    ddown1 = _rowblock_wgrad("ffn1_down_wgrad", act1, df1)
    ddown1 = ddown1.reshape(N_DEV, r_down, D)
    (dh1, dcw1), (gp_down1a,) = _conv_swiglu_bwd("ffn1_conv_bwd", h1, cw[1], dact1,
                                                rider=([[(ddown1, (0, r_down // 2))]], "scatter"))
    dun1, (gp_down1b,) = _colblock_dgrad("ffn1_up_dgrad", dh1, up_g1, 0, F32,
                                         rider=([[(ddown1, (r_down // 2, r_down // 2))]], "scatter"))
    dup1 = _colblock_wgrad("ffn1_up_wgrad", un1, dh1)
    dx3, dypre1, red3 = _rows_bwd("rows_bwd_b3", dx4, dun1, x3, ypre1, vecs["b3"], dy_dtype=_ACT)
    dp1 = _group_mm("pool_dgrad", dypre1, pool_full, "nt", F32)
    dpool = _group_wgrad("pool_wgrad", p1, dypre1, n_pool)
    du1 = _pool_window("pool_bwd", dp1, True, F32)
    dx2, df0, red2 = _rows_bwd("rows_bwd_b2", dx3, du1, x2, f0, vecs["b2"], dy_dtype=_ACT)
    dact0 = _rowblock_dgrad("ffn0_down_dgrad", df0, down_g0, 0)
    ddown0 = _rowblock_wgrad("ffn0_down_wgrad", act0, df0)
    ddown0 = ddown0.reshape(N_DEV, r_down, D)
    (dh0, dcw0), (gp_down0a,) = _conv_swiglu_bwd("ffn0_conv_bwd", h0, cw[0], dact0,
                                                 rider=([[(ddown0, (0, r_down // 2))]], "scatter"))
    dun0, (gp_up1a,) = _colblock_dgrad("ffn0_up_dgrad", dh0, up_g0, 0, F32,
                                       rider=([[(dup1, (0, D // 2))]], "scatter"))
    dup0, (gp_up1b,) = _colblock_wgrad("ffn0_up_wgrad", un0, dh0, rider=([[(dup1, (D // 2, D // 4))]], "scatter"))
    dx1, dmix0, red1 = _rows_bwd("rows_bwd_b1", dx2, dun0, x1, mix0, vecs["b1"])
    dvg = _glu_bwd("glu_bwd", vg, _to_segments(dmix0))
    dz = _colblock_dgrad("glu_dgrad", dvg, glu_g, 0, _ACT)
    dglu = _colblock_wgrad("glu_wgrad", z_s5, dvg)
    dpool_blocks = dpool.reshape(n_pool, N_DEV, pr, pc).transpose(1, 0, 2, 3).reshape(N_DEV, n_pool * pr, pc)
    (du0s, ducs, dbblk, dcblk, dapar, ddsk), (gp_up0, gp_glu, gp_pool, gp_up1c, gp_down0b) = _s5_bwd(
        u0s, ucs, dz, y_s5, bblk, cblk, apar, dsk,
        rider=([[dup0], [dglu], [dpool_blocks], [(dup1, (3 * D // 4, D // 4))], [(ddown0, (r_down // 2, r_down // 2))]],
               "scatter"))
    grad_x, red0 = _rows_bwd("rows_bwd_b0", dx1, _from_segments(du0s), x0, None, vecs["b0"])
    redc, = _rows_bwd("rows_bwd_ctx", None, _from_segments(ducs), ctx[0], None, vecs["c0"], want_dx=False)

    zero_d = jnp.zeros((D,), F32)
    dmod = jnp.stack([
        jnp.stack([red0[R_SHIFT], red0[R_SCALE], red1[R_GATE], red1[R_SHIFT], red1[R_SCALE], red2[R_GATE]]),
        jnp.stack([red2[R_SHIFT], red2[R_SCALE], red3[R_GATE], red3[R_SHIFT], red3[R_SCALE], red4[R_GATE]])])
    dmod_c = jnp.stack([jnp.stack([redc[R_SHIFT], redc[R_SCALE]] + [zero_d] * 4), jnp.zeros((6, D), F32)])
    dm_g, = _exchange([[jnp.stack([dmod, dmod_c], axis=1).reshape(2, 2, 6 * D)]], mode="gather", name="gather_dmods")
    dm_g = dm_g.reshape(N_DEV, 2, 2, 6 * D)
    dm_ctx = _sum_parts("sum_dmod_ctx", dm_g[:, :, 1, :])
    dm_rows = jnp.concatenate([dm_g[:, :, 0, :].transpose(1, 0, 2), dm_ctx[:, None, :]], axis=1)
    grad_ada_b = _sum_parts("sum_ada_b", dm_rows.transpose(1, 0, 2))
    dm_cols = dm_rows.reshape(2, N_DEV + 1, N_DEV, n_ada)
    dm_mine = lax.dynamic_index_in_dim(dm_cols, me, axis=2, keepdims=False)
    dm_mine = jnp.concatenate([dm_mine, jnp.zeros((2, ADA_ROWS - N_DEV - 1, n_ada), F32)], axis=1)
    grad_ada_w, dcond = _ada_bwd(cmat, ada_w, dm_mine)
    dcctx_part = dcond[0, N_DEV] + dcond[1, N_DEV]

    da, db, dc = _s5_unpack(dapar, dbblk, dcblk, G)
    dnorm = jnp.stack([
        jnp.stack([red0[R_GPRE] + redc[R_GPRE], red1[R_GPOST], red1[R_GPRE], red2[R_GPOST]]),
        jnp.stack([red2[R_GPRE], red3[R_GPOST], red3[R_GPRE], red4[R_GPOST]])])
    dconv = jnp.stack([d[:, :3, :].transpose(1, 0, 2).reshape(3, 2 * FF) for d in (dcw0, dcw1)])
    dconv_b = jnp.stack([d[:, 3, :].reshape(2 * FF) for d in (dcw0, dcw1)])
    pieces = [dcctx_part, dnorm, da, db, dc, ddsk[:, 0, :], red3[R_YSCALE], dconv, dconv_b]
    flat = jnp.concatenate([p.reshape(-1) for p in pieces])
    n_flat = flat.shape[0]
    per_dev = -(-n_flat // (N_DEV * 8 * LANE)) * 8 * LANE
    flat = jnp.pad(flat, (0, N_DEV * per_dev - n_flat)).reshape(N_DEV, per_dev // LANE, LANE)
    parts, = _exchange([[flat]], mode="scatter", name="scatter_small_grads")
    mine = _sum_parts("sum_small_grads", parts.reshape(N_DEV, per_dev // LANE, LANE))
    summed, = _exchange([[mine]], mode="gather", name="gather_small_grads")
    summed = summed.reshape(-1)
    red_pieces, o = [], 0
    for p in pieces:
        red_pieces.append(summed[o:o + p.size].reshape(p.shape))
        o += p.size
    g_cctx, g_norm, g_a, g_b, g_c, g_d, g_pscale, g_conv, g_conv_b = red_pieces
    cot = [(g_a[d, 0], g_a[d, 1], g_b[d, 0], g_b[d, 1]) for d in range(2)]
    g_lam_re, g_lam_im, g_log_step, g_b_re, g_b_im = disc_vjp(cot)

    out = {}

    def put(name, res, shape):
        out[name] = tuple(r.reshape(shape) for r in res)

    gp_up0, gp_up1a = gp_up0.reshape(N_DEV, D, nb_up), gp_up1a.reshape(N_DEV, D // 2, nb_up)
    quarters = (8, D // 4, nb_up)
    put("ffn_up", _adamw("adamw_ffn_up",
                         [(gp_up0, q * D // 4) for q in range(4)] + [(gp_up1a, 0), (gp_up1a, D // 4),
                                                                     gp_up1b.reshape(N_DEV, D // 4, nb_up),
                                                                     gp_up1c.reshape(N_DEV, D // 4, nb_up)],
                         ffn_up.reshape(quarters), m_ffn_up.reshape(quarters), v_ffn_up.reshape(quarters)),
        ffn_up.shape)
    halves = (4, r_down // 2, D)
    put("ffn_down", _adamw("adamw_ffn_down",
                           [g.reshape(N_DEV, r_down // 2, D) for g in (gp_down0a, gp_down0b, gp_down1a, gp_down1b)],
                           ffn_down.reshape(halves), m_ffn_down.reshape(halves), v_ffn_down.reshape(halves)),
        ffn_down.shape)
    put("s5_glu_w", _adamw("adamw_glu", [gp_glu.reshape(N_DEV, D, -1)], s5_glu_w, m_s5_glu_w, v_s5_glu_w),
        s5_glu_w.shape)
    pool_rows = (1, n_pool * pr, pc)
    put("pool_w", _adamw("adamw_pool", [gp_pool.reshape(N_DEV, n_pool * pr, pc)], pool_w.reshape(pool_rows),
                         m_pool_w.reshape(pool_rows), v_pool_w.reshape(pool_rows)), pool_w.shape)
    put("ada_w", _adamw("adamw_ada_w", [grad_ada_w[i][None] for i in range(2)], ada_w, m_ada_w, v_ada_w), ada_w.shape)

    for nm, w, m, v, g in (("s5_b_re", s5_b_re, m_s5_b_re, v_s5_b_re, g_b_re),
                           ("s5_b_im", s5_b_im, m_s5_b_im, v_s5_b_im, g_b_im),
                           ("s5_c_re", s5_c_re, m_s5_c_re, v_s5_c_re, g_c[:, 0]),
                           ("s5_c_im", s5_c_im, m_s5_c_im, v_s5_c_im, g_c[:, 1])):
        rows = (1, w.size // w.shape[-1], w.shape[-1])
        put(nm, _adamw("adamw_" + nm, [g.reshape(rows)], w.reshape(rows), m.reshape(rows), v.reshape(rows)), w.shape)

    small = [
        ("c_ctx", c_ctx, m_c_ctx, v_c_ctx, g_cctx),
        ("ada_b", ada_b, m_ada_b, v_ada_b, grad_ada_b),
        ("norm_g", norm_g, m_norm_g, v_norm_g, _my_block(g_norm, 2, ng_loc)),
        ("s5_lam_re", s5_lam_re, m_s5_lam_re, v_s5_lam_re, g_lam_re),
        ("s5_lam_im", s5_lam_im, m_s5_lam_im, v_s5_lam_im, g_lam_im),
        ("s5_log_step", s5_log_step, m_s5_log_step, v_s5_log_step, g_log_step),
        ("s5_d", s5_d, m_s5_d, v_s5_d, g_d),
        ("pool_scale", pool_scale, m_pool_scale, v_pool_scale, _my_block(g_pscale, 0, ng_loc)),
        ("ffn_conv", ffn_conv, m_ffn_conv, v_ffn_conv, _my_block(g_conv, 2, nb_up)),
        ("ffn_conv_b", ffn_conv_b, m_ffn_conv_b, v_ffn_conv_b, g_conv_b),
    ]
    n_sm = sum(w.size for _, w, _, _, _ in small)
    rows_sm = -(-n_sm // (512 * LANE)) * 512

    def flat_of(k):
        f = jnp.concatenate([t[k].reshape(-1) for t in small])
        return jnp.pad(f, (0, rows_sm * LANE - n_sm)).reshape(rows_sm, LANE)

    res_sm = _adamw("adamw_small", [flat_of(4)[None]], flat_of(1)[None], flat_of(2)[None], flat_of(3)[None])
    o = 0
    for name, w, _, _, _ in small:
        out[name] = tuple(r.reshape(-1)[o:o + w.size].reshape(w.shape) for r in res_sm)
        o += w.size

    order = ["c_ctx", "ada_w", "ada_b", "norm_g", "s5_lam_re", "s5_lam_im", "s5_log_step", "s5_b_re", "s5_b_im",
             "s5_c_re", "s5_c_im", "s5_d", "s5_glu_w", "pool_w", "pool_scale", "ffn_up", "ffn_conv", "ffn_conv_b",
             "ffn_down"]
    return (loss, grad_x.reshape(x.shape), *[out[n][0] for n in order], *[out[n][1] for n in order],
            *[out[n][2] for n in order], *[out[n][3] for n in order])
```

```python
import functools
import math

import jax
import jax.numpy as jnp
from jax import lax
from jax.experimental import pallas as pl
from jax.experimental.pallas import tpu as pltpu

F32 = jnp.float32
_ACT = jnp.bfloat16
N_DEV = 8
NSEG = 8
S5_CH = 16
S5_P = 64
LANE = 128
S5_TILE_CH = LANE
S5_TILE_G = S5_TILE_CH // S5_CH
S5_TILE_W = S5_TILE_G * S5_P
GRID_W = 64
POOL_WINDOWS = (2, 4, 8, 16)
POOL_HALO = 64
RMS_EPS = 1e-6
POS_BASE = 10000.0
ADAM_LR, ADAM_B1, ADAM_B2, ADAM_EPS, ADAM_WD, ADAM_STEP = 0.001, 0.9, 0.999, 1e-08, 0.01, 10
VMEM_LIMIT = 48 * 1024 * 1024
VMEM_LIMIT_BIG = 58 * 1024 * 1024
MESH = pl.DeviceIdType.MESH
ANY = pl.BlockSpec(memory_space=pl.ANY)


def _params(sem, vmem=VMEM_LIMIT):
    return pltpu.CompilerParams(dimension_semantics=sem, vmem_limit_bytes=vmem)


def _my_index():
    return 4 * lax.axis_index("x") + 2 * lax.axis_index("y") + lax.axis_index("c")


def _xchg_plan(groups, mode):
    flat = [(g, l, a) for g, grp in enumerate(groups) for l, a in enumerate(grp)]
    outs = []
    for grp in groups:
        a, rows = _rows_of(grp[0])
        piece = a.shape[1:] if mode == "scatter" else a.shape
        if rows is not None:
            piece = (rows[1],) + tuple(piece[1:])
        outs.append(jax.ShapeDtypeStruct((N_DEV, len(grp)) + tuple(piece), a.dtype))
    return flat, outs


def _rows_of(entry):
    return entry if isinstance(entry, tuple) else (entry, None)


def _operands(flat):
    return [_rows_of(a)[0] for _, _, a in flat]


def _xchg_sems(n):
    return [pltpu.SemaphoreType.DMA((n, N_DEV - 1)), pltpu.SemaphoreType.DMA((n, N_DEV - 1)),
            pltpu.SemaphoreType.DMA((n,))]


def _xchg_copies(flat, mode, ins, out_refs, sems, waiting=True):
    send_sems, recv_sems, local_sems = sems
    x, y, c = lax.axis_index("x"), lax.axis_index("y"), lax.axis_index("c")
    me = 4 * x + 2 * y + c
    local, first, forwards = [], [], []

    def pair(s, j, dev):
        return dict(send_sem=send_sems.at[s, j], recv_sem=recv_sems.at[s, j], device_id=dev, device_id_type=MESH)

    def block(s, dev):
        rows = _rows_of(flat[s][2])[1]
        ref = ins[s].at[dev]
        return ref if rows is None else ref.at[pl.ds(rows[0], rows[1])]

    for s, (g, l, _) in enumerate(flat):
        src = block(s, me) if mode == "scatter" else ins[s]
        local.append(pltpu.make_async_copy(src, out_refs[g].at[me, l], local_sems.at[s]))
    if mode == "gather2":
        sib, sib_idx = (x, y, 1 - c), 4 * x + 2 * y + (1 - c)
        for s, (g, l, _) in enumerate(flat):
            slot = lambda dev, g=g, l=l: out_refs[g].at[dev, l]
            targets = [(sib, sib_idx)] + [((qx, qy, c), 4 * qx + 2 * qy + c)
                                          for qx, qy in ((1 - x, y), (x, 1 - y), (1 - x, 1 - y))]
            for j, (dev, idx) in enumerate(targets):
                send = pltpu.make_async_remote_copy(src_ref=ins[s], dst_ref=slot(me), **pair(s, j, dev))
                arrive = pltpu.make_async_remote_copy(src_ref=ins[s], dst_ref=slot(idx), **pair(s, j, dev)) if waiting else None
                first.append((send, arrive))
            if waiting:
                for j, (dev, idx) in enumerate(targets[1:]):
                    other = 4 * dev[0] + 2 * dev[1] + (1 - c)
                    send = pltpu.make_async_remote_copy(src_ref=slot(idx), dst_ref=slot(idx), **pair(s, 4 + j, sib))
                    arrive = pltpu.make_async_remote_copy(src_ref=slot(idx), dst_ref=slot(other), **pair(s, 4 + j, sib))
                    forwards.append((first[len(first) - 3 + j][1], send, arrive))
        return local, first, forwards
    for k in range(1, N_DEV):
        px = 1 - x if k & 4 else x
        py = 1 - y if k & 2 else y
        pc = 1 - c if k & 1 else c
        peer = 4 * px + 2 * py + pc
        for s, (g, l, _) in enumerate(flat):
            src = block(s, peer) if mode == "scatter" else ins[s]
            send = pltpu.make_async_remote_copy(src_ref=src, dst_ref=out_refs[g].at[me, l], **pair(s, k - 1, (px, py, pc)))
            arrive = (pltpu.make_async_remote_copy(src_ref=src, dst_ref=out_refs[g].at[peer, l],
                                                   **pair(s, k - 1, (px, py, pc))) if waiting else None)
            first.append((send, arrive))
    return local, first, forwards


def _xchg_start(local, first, forwards):
    for cp in local:
        cp.start()
    for send, _ in first:
        send.start()


def _xchg_wait(local, first, forwards):
    gates = [gate for gate, _, _ in forwards]
    for gate, send, _ in forwards:
        gate.wait_recv()
        send.start()
    for _, arrive in first:
        if not any(arrive is gate for gate in gates):
            arrive.wait_recv()
    for _, _, arrive in forwards:
        arrive.wait_recv()
    for send, _ in first:
        send.wait_send()
    for _, send, _ in forwards:
        send.wait_send()
    for cp in local:
        cp.wait()


def _exchange(groups, mode, name):
    flat, outs = _xchg_plan(groups, mode)
    n = len(flat)

    def body(*refs):
        copies = _xchg_copies(flat, mode, refs[:n], refs[n:n + len(groups)], refs[n + len(groups):])
        _xchg_start(*copies)
        _xchg_wait(*copies)

    res = pl.pallas_call(body, name=name, out_shape=outs, in_specs=[ANY] * n, out_specs=[ANY] * len(groups),
                         scratch_shapes=_xchg_sems(n))(*_operands(flat))
    return list(res)


def _pcall(body, *, name, grid, in_specs, out_specs, out_shape, ins, scratch_shapes=(), sem=None, vmem=VMEM_LIMIT,
           rider=None):
    single = not isinstance(out_shape, (list, tuple))
    if rider is None:
        return pl.pallas_call(body, name=name, grid=grid, in_specs=list(in_specs), out_specs=out_specs,
                              out_shape=out_shape, scratch_shapes=list(scratch_shapes),
                              compiler_params=_params(sem, vmem))(*ins)
    groups, mode = rider
    flat, r_outs = _xchg_plan(groups, mode)
    n_in, n_out = len(ins), 1 if single else len(out_shape)
    nr, ng, ns = len(flat), len(groups), len(scratch_shapes)

    def wrapped(*refs):
        o1 = n_in + nr
        o2 = o1 + n_out
        o3 = o2 + ng
        r_in, r_out, sems = refs[n_in:o1], refs[o2:o3], refs[o3 + ns:]
        first = functools.reduce(jnp.logical_and, [pl.program_id(d) == 0 for d in range(len(grid))])
        last = functools.reduce(jnp.logical_and, [pl.program_id(d) == grid[d] - 1 for d in range(len(grid))])

        @pl.when(first)
        def _():
            _xchg_start(*_xchg_copies(flat, mode, r_in, r_out, sems, waiting=False))

        body(*refs[:n_in], *refs[o1:o2], *refs[o3:o3 + ns])

        @pl.when(last)
        def _():
            _xchg_wait(*_xchg_copies(flat, mode, r_in, r_out, sems))

    outs = pl.pallas_call(
        wrapped, name=name, grid=grid, in_specs=list(in_specs) + [ANY] * nr,
        out_specs=([out_specs] if single else list(out_specs)) + [ANY] * ng,
        out_shape=([out_shape] if single else list(out_shape)) + r_outs,
        scratch_shapes=list(scratch_shapes) + _xchg_sems(nr),
        compiler_params=_params(("arbitrary",) * len(grid), vmem))(*ins, *_operands(flat))
    base = list(outs[:n_out])
    return (base[0] if single else base), list(outs[n_out:])


_DIMS = {"nn": (((1,), (0,)), ((), ())), "nt": (((1,), (1,)), ((), ())), "tn": (((0,), (0,)), ((), ()))}


def _mm(name, a, b, a_spec, b_spec, o_spec, out_shape, grid, dims, rider=None):
    nk = grid[2]
    acc_shape = tuple(d for d in o_spec.block_shape if d is not None)
    dn = _DIMS[dims]

    def tile(ref):
        v = ref[...]
        return v.reshape((-1, v.shape[-1])).astype(_ACT)

    def body(a_ref, b_ref, o_ref, *scratch):
        def part():
            return lax.dot_general(tile(a_ref), tile(b_ref), dn, preferred_element_type=F32)

        if nk == 1:
            o_ref[...] = part().reshape(o_ref.shape).astype(o_ref.dtype)
            return
        acc_ref, = scratch
        k = pl.program_id(2)

        @pl.when(k == 0)
        def _():
            acc_ref[...] = part()

        @pl.when(k > 0)
        def _():
            acc_ref[...] += part()

        @pl.when(k == nk - 1)
        def _():
            o_ref[...] = acc_ref[...].reshape(o_ref.shape).astype(o_ref.dtype)

    acc2d = (math.prod(acc_shape[:-1]), acc_shape[-1])
    return _pcall(body, name=name, out_shape=out_shape, grid=grid, in_specs=[a_spec, b_spec], out_specs=o_spec,
                  scratch_shapes=[] if nk == 1 else [pltpu.VMEM(acc2d, F32)], ins=(a, b),
                  sem=("parallel", "parallel", "arbitrary"), rider=rider)


def _row_tile(n, want):
    t = min(n, want)
    assert n % t == 0, (n, t)
    return t


def _colblock_fwd(name, xa, wg, layer, out_dtype, rider=None):
    L, K = xa.shape
    nb = wg.shape[3]
    half = N_DEV // 2
    tm = _row_tile(L, 512)
    return _mm(name, xa, wg,
               pl.BlockSpec((tm, K), lambda j, i, k: (i, 0)),
               pl.BlockSpec((None, None, K, nb), lambda j, i, k: (j, layer, 0, 0)),
               pl.BlockSpec((None, tm, nb), lambda j, i, k: (j // half, i, j % half)),
               jax.ShapeDtypeStruct((2, L, half * nb), out_dtype), (N_DEV, L // tm, 1), "nn", rider=rider)


def _colblock_dgrad(name, dh, wg, layer, out_dtype, rider=None):
    _, L, _ = dh.shape
    K, nb = wg.shape[2], wg.shape[3]
    half = N_DEV // 2
    tm = _row_tile(L, 512)
    return _mm(name, dh, wg,
               pl.BlockSpec((None, tm, nb), lambda i, j, k: (k // half, i, k % half)),
               pl.BlockSpec((None, None, K, nb), lambda i, j, k: (k, layer, 0, 0)),
               pl.BlockSpec((tm, K), lambda i, j, k: (i, 0)),
               jax.ShapeDtypeStruct((L, K), out_dtype), (L // tm, 1, N_DEV), "nt", rider=rider)


def _colblock_wgrad(name, xa, dh, rider=None):
    L, K = xa.shape
    half = N_DEV // 2
    nb = dh.shape[2] // half
    tm = _row_tile(K, 512)
    tk = L
    return _mm(name, xa, dh,
               pl.BlockSpec((tk, tm), lambda j, i, k: (k, i)),
               pl.BlockSpec((None, tk, nb), lambda j, i, k: (j // half, k, j % half)),
               pl.BlockSpec((None, tm, nb), lambda j, i, k: (j, i, 0)),
               jax.ShapeDtypeStruct((N_DEV, K, nb), _ACT), (N_DEV, K // tm, L // tk), "tn", rider=rider)


def _rowblock_fwd(name, xa, wg, layer, rider=None):
    L, FF = xa.shape
    r, D = wg.shape[2], wg.shape[3]
    tm = _row_tile(L, 512)
    return _mm(name, xa, wg,
               pl.BlockSpec((tm, 2 * r), lambda i, j, k: (i, k)),
               pl.BlockSpec((2, None, r, D), lambda i, j, k: (k, layer, 0, 0)),
               pl.BlockSpec((tm, D), lambda i, j, k: (i, 0)),
               jax.ShapeDtypeStruct((L, D), F32), (L // tm, 1, N_DEV // 2), "nn", rider=rider)


def _rowblock_dgrad(name, dy, wg, layer, rider=None):
    L, D = dy.shape
    r = wg.shape[2]
    tm = _row_tile(L, 512)
    return _mm(name, dy, wg,
               pl.BlockSpec((tm, D), lambda i, j, k: (i, 0)),
               pl.BlockSpec((2, None, r, D), lambda i, j, k: (j, layer, 0, 0)),
               pl.BlockSpec((tm, 2 * r), lambda i, j, k: (i, j)),
               jax.ShapeDtypeStruct((L, N_DEV * r), _ACT), (L // tm, N_DEV // 2, 1), "nt", rider=rider)


def _rowblock_wgrad(name, xa, dy, rider=None):
    L, FF = xa.shape
    D = dy.shape[1]
    tm = FF // (N_DEV // 2)
    tn = _row_tile(D, 1024)
    tk = _row_tile(L, 2048)
    return _mm(name, xa, dy,
               pl.BlockSpec((tk, tm), lambda i, j, k: (k, i)),
               pl.BlockSpec((tk, tn), lambda i, j, k: (k, j)),
               pl.BlockSpec((tm, tn), lambda i, j, k: (i, j)),
               jax.ShapeDtypeStruct((FF, D), _ACT), (FF // tm, D // tn, L // tk), "tn", rider=rider)


def _group_mm(name, xa, w, dims, out_dtype):
    L, D = xa.shape
    ng, pc, _ = w.shape
    tm = _row_tile(L, 512)
    return _mm(name, xa, w,
               pl.BlockSpec((tm, pc), lambda i, g, k: (i, g)),
               pl.BlockSpec((None, pc, pc), lambda i, g, k: (g, 0, 0)),
               pl.BlockSpec((tm, pc), lambda i, g, k: (i, g)),
               jax.ShapeDtypeStruct((L, D), out_dtype), (L // tm, ng, 1), dims)


def _group_wgrad(name, p, dy, ng):
    L, D = p.shape
    pc = D // ng
    tk = _row_tile(L, 512)
    return _mm(name, p, dy,
               pl.BlockSpec((tk, pc), lambda g, j, k: (k, g)),
               pl.BlockSpec((tk, pc), lambda g, j, k: (k, g)),
               pl.BlockSpec((None, pc, pc), lambda g, j, k: (g, 0, 0)),
               jax.ShapeDtypeStruct((ng, pc, pc), _ACT), (ng, 1, L // tk), "tn")


V_GPOST, V_GATE, V_YSCALE, V_GPRE, V_SHIFT, V_SCALE = range(6)
R_SHIFT, R_SCALE, R_GPRE, R_GATE, R_GPOST, R_YSCALE = range(6)
ROW_TILE = 256


def _rstd(v):
    return lax.rsqrt(jnp.mean(v * v, axis=-1, keepdims=True) + RMS_EPS)


def _post_norm_bwd(dyh, yh, ry):
    return ry * (dyh - yh * jnp.mean(dyh * yh, axis=-1, keepdims=True))


def _rows_fwd(name, xres, y, vec, *, add=False, target=None, want_x=True, u_dtype=None):
    L, D = xres.shape
    tm = _row_tile(L, ROW_TILE)
    has_y = y is not None
    last = target is not None
    has_u = u_dtype is not None

    def body(*refs):
        refs = list(refs)
        xres_ref = refs.pop(0)
        y_ref = refs.pop(0) if has_y else None
        vec_ref = refs.pop(0)
        tgt_ref = refs.pop(0) if last else None
        xnew = xres_ref[...]
        if has_y and add:
            xnew = xnew + y_ref[...]
        elif has_y:
            ye = y_ref[...] * vec_ref[V_YSCALE:V_YSCALE + 1, :]
            ry = _rstd(ye)
            yh = ye * ry
            gpost, gate = vec_ref[V_GPOST:V_GPOST + 1, :], vec_ref[V_GATE:V_GATE + 1, :]
            xnew = xnew + gate * (yh * gpost)
        if last:
            dx_ref, loss_ref, dy_ref, red_ref = refs
            diff = xnew - tgt_ref[...]
            dxn = diff * (1.0 / D)
            dx_ref[...] = dxn

            @pl.when(pl.program_id(0) == 0)
            def _():
                loss_ref[...] = jnp.zeros_like(loss_ref)
                red_ref[...] = jnp.zeros_like(red_ref)

            loss_ref[...] += jnp.sum(diff * diff) * (0.5 / D)
            drn2 = dxn * gate
            red_ref[R_GATE:R_GATE + 1, :] += jnp.sum(dxn * (yh * gpost), axis=0, keepdims=True)
            red_ref[R_GPOST:R_GPOST + 1, :] += jnp.sum(drn2 * yh, axis=0, keepdims=True)
            dye = _post_norm_bwd(drn2 * gpost, yh, ry)
            red_ref[R_YSCALE:R_YSCALE + 1, :] += jnp.sum(dye * y_ref[...], axis=0, keepdims=True)
            dy_ref[...] = (dye * vec_ref[V_YSCALE:V_YSCALE + 1, :]).astype(dy_ref.dtype)
            return
        if want_x:
            refs.pop(0)[...] = xnew
        if has_u:
            u_ref, = refs
            n = xnew * _rstd(xnew) * vec_ref[V_GPRE:V_GPRE + 1, :]
            u_ref[...] = (n * (1.0 + vec_ref[V_SCALE:V_SCALE + 1, :]) + vec_ref[V_SHIFT:V_SHIFT + 1, :]).astype(u_ref.dtype)

    row = pl.BlockSpec((tm, D), lambda i: (i, 0))
    vspec = pl.BlockSpec((8, D), lambda i: (0, 0))
    ins, in_specs = [xres], [row]
    if has_y:
        ins.append(y)
        in_specs.append(row)
    ins.append(vec)
    in_specs.append(vspec)
    out_shape, out_specs = [], []
    if last:
        ins.append(target)
        in_specs.append(row)
        out_shape = [jax.ShapeDtypeStruct((L, D), F32), jax.ShapeDtypeStruct((8, LANE), F32),
                     jax.ShapeDtypeStruct((L, D), _ACT), jax.ShapeDtypeStruct((8, D), F32)]
        out_specs = [row, pl.BlockSpec((8, LANE), lambda i: (0, 0)), row, vspec]
    else:
        if want_x:
            out_shape.append(jax.ShapeDtypeStruct((L, D), F32))
            out_specs.append(row)
        if has_u:
            out_shape.append(jax.ShapeDtypeStruct((L, D), u_dtype))
            out_specs.append(row)
    return pl.pallas_call(body, name=name, out_shape=out_shape, grid=(L // tm,), in_specs=in_specs,
                          out_specs=out_specs, compiler_params=_params(("arbitrary",)))(*ins)


def _rows_bwd(name, dxd, du, xnew, y, vec, dy_dtype=F32, want_dx=True, yscale_grad=False):
    L, D = xnew.shape if xnew is not None else dxd.shape
    tm = _row_tile(L, ROW_TILE)
    has_dxd, has_pre, has_post = dxd is not None, du is not None, y is not None

    def body(*refs):
        refs = list(refs)
        dxd_ref = refs.pop(0) if has_dxd else None
        du_ref = refs.pop(0) if has_pre else None
        xnew_ref = refs.pop(0) if has_pre else None
        y_ref = refs.pop(0) if has_post else None
        vec_ref = refs.pop(0)
        dx_ref = refs.pop(0) if want_dx else None
        dy_ref = refs.pop(0) if has_post else None
        red_ref, = refs

        @pl.when(pl.program_id(0) == 0)
        def _():
            red_ref[...] = jnp.zeros_like(red_ref)

        def acc(rw, val):
            red_ref[rw:rw + 1, :] += jnp.sum(val, axis=0, keepdims=True)

        dxn = dxd_ref[...] if has_dxd else None
        if has_pre:
            xn = xnew_ref[...]
            r = _rstd(xn)
            nh = xn * r
            gpre = vec_ref[V_GPRE:V_GPRE + 1, :]
            dub = du_ref[...].astype(F32)
            acc(R_SHIFT, dub)
            acc(R_SCALE, dub * (nh * gpre))
            drn = dub * (1.0 + vec_ref[V_SCALE:V_SCALE + 1, :])
            acc(R_GPRE, drn * nh)
            dnh = drn * gpre
            t = r * (dnh - nh * jnp.mean(dnh * nh, axis=-1, keepdims=True))
            dxn = t if dxn is None else dxn + t
        if want_dx:
            dx_ref[...] = dxn
        if has_post:
            ye = y_ref[...] * vec_ref[V_YSCALE:V_YSCALE + 1, :]
            ry = _rstd(ye)
            yh = ye * ry
            gpost = vec_ref[V_GPOST:V_GPOST + 1, :]
            acc(R_GATE, dxn * (yh * gpost))
            drn2 = dxn * vec_ref[V_GATE:V_GATE + 1, :]
            acc(R_GPOST, drn2 * yh)
            dye = _post_norm_bwd(drn2 * gpost, yh, ry)
            if yscale_grad:
                acc(R_YSCALE, dye * y_ref[...])
            dy_ref[...] = (dye * vec_ref[V_YSCALE:V_YSCALE + 1, :]).astype(dy_ref.dtype)

    row = pl.BlockSpec((tm, D), lambda i: (i, 0))
    vspec = pl.BlockSpec((8, D), lambda i: (0, 0))
    ins, in_specs = [], []
    for a in ([dxd] if has_dxd else []) + ([du, xnew] if has_pre else []) + ([y] if has_post else []):
        ins.append(a)
        in_specs.append(row)
    ins.append(vec)
    in_specs.append(vspec)
    out_shape, out_specs = [], []
    if want_dx:
        out_shape.append(jax.ShapeDtypeStruct((L, D), F32))
        out_specs.append(row)
    if has_post:
        out_shape.append(jax.ShapeDtypeStruct((L, D), dy_dtype))
        out_specs.append(row)
    out_shape.append(jax.ShapeDtypeStruct((8, D), F32))
    out_specs.append(vspec)
    return pl.pallas_call(body, name=name, out_shape=out_shape, grid=(L // tm,), in_specs=in_specs,
                          out_specs=out_specs, compiler_params=_params(("arbitrary",)))(*ins)


def _sigmoid(v):
    return 1.0 / (1.0 + jnp.exp(-v))


def _glu_fwd(name, vg):
    _, L, D = vg.shape
    tm = _row_tile(L, ROW_TILE)

    def body(vg_ref, o_ref):
        o_ref[...] = vg_ref[0].astype(F32) * _sigmoid(vg_ref[1].astype(F32))

    return pl.pallas_call(body, name=name, grid=(L // tm,),
                          in_specs=[pl.BlockSpec((2, tm, D), lambda i: (0, i, 0))],
                          out_specs=pl.BlockSpec((tm, D), lambda i: (i, 0)),
                          out_shape=jax.ShapeDtypeStruct((L, D), F32),
                          compiler_params=_params(("parallel",)))(vg)


def _glu_bwd(name, vg, dout):
    _, L, D = vg.shape
    tm = _row_tile(L, ROW_TILE)

    def body(vg_ref, d_ref, o_ref):
        val, s = vg_ref[0].astype(F32), _sigmoid(vg_ref[1].astype(F32))
        d = d_ref[...]
        o_ref[0] = (d * s).astype(o_ref.dtype)
        o_ref[1] = (d * val * s * (1.0 - s)).astype(o_ref.dtype)

    return pl.pallas_call(body, name=name, grid=(L // tm,),
                          in_specs=[pl.BlockSpec((2, tm, D), lambda i: (0, i, 0)), pl.BlockSpec((tm, D), lambda i: (i, 0))],
                          out_specs=pl.BlockSpec((2, tm, D), lambda i: (0, i, 0)),
                          out_shape=jax.ShapeDtypeStruct((2, L, D), _ACT),
                          compiler_params=_params(("parallel",)))(vg, dout)


CONV_ROWS = 256


def _row_pick(blk, idx):
    rows = lax.broadcasted_iota(jnp.int32, blk.shape, 0)
    return jnp.sum(jnp.where(rows == idx, blk, 0.0), axis=0, keepdims=True)


def _shifted(ref, r0, rc, L):
    cur = ref[pl.ds(r0, rc), :].astype(F32)
    before = ref[pl.ds(pl.multiple_of(jnp.maximum(r0 - 16, 0), 16), 16), :].astype(F32)
    after = ref[pl.ds(pl.multiple_of(jnp.minimum(r0 + rc, L - 16), 16), 16), :].astype(F32)
    prev_row = jnp.where(r0 > 0, _row_pick(before, 15), 0.0)
    next_row = jnp.where(r0 + rc < L, _row_pick(after, 0), 0.0)
    rows = lax.broadcasted_iota(jnp.int32, cur.shape, 0)
    up = jnp.where(rows == 0, prev_row, pltpu.roll(cur, 1, 0))
    down = jnp.where(rows == rc - 1, next_row, pltpu.roll(cur, rc - 1, 0))
    return up, cur, down


def _silu_parts(g):
    s = _sigmoid(g)
    return g * s, s


def _conv_swiglu_fwd(name, h, cw, rider=None):
    _, L, FF = h.shape
    rc = _row_tile(L, CONV_ROWS)

    def body(h_ref, cw_ref, o_ref):
        def chunk(ci, _):
            r0 = pl.multiple_of(ci * rc, rc)
            hc = []
            for half in range(2):
                up, cur, down = _shifted(h_ref.at[half], r0, rc, L)
                hc.append(up * cw_ref[half, 0:1, :] + cur * cw_ref[half, 1:2, :] + down * cw_ref[half, 2:3, :]
                          + cw_ref[half, 3:4, :])
            o_ref[pl.ds(r0, rc), :] = (_silu_parts(hc[1])[0] * hc[0]).astype(o_ref.dtype)
            return 0

        lax.fori_loop(0, L // rc, chunk, 0)

    return _pcall(body, name=name, grid=(FF // LANE,),
                  in_specs=[pl.BlockSpec((2, L, LANE), lambda j: (0, 0, j)),
                            pl.BlockSpec((2, 8, LANE), lambda j: (0, 0, j))],
                  out_specs=pl.BlockSpec((L, LANE), lambda j: (0, j)),
                  out_shape=jax.ShapeDtypeStruct((L, FF), _ACT), ins=(h, cw), sem=("parallel",), rider=rider)


def _conv_swiglu_bwd(name, h, cw, dact, rider=None):
    _, L, FF = h.shape
    rc = _row_tile(L, CONV_ROWS)

    def body(h_ref, cw_ref, da_ref, dh_ref, dcw_ref, dhc_ref):
        def chunk(ci, acc):
            r0 = pl.multiple_of(ci * rc, rc)
            taps, hc = [], []
            for half in range(2):
                t = _shifted(h_ref.at[half], r0, rc, L)
                taps.append(t)
                hc.append(t[0] * cw_ref[half, 0:1, :] + t[1] * cw_ref[half, 1:2, :] + t[2] * cw_ref[half, 2:3, :]
                          + cw_ref[half, 3:4, :])
            d = da_ref[pl.ds(r0, rc), :].astype(F32)
            act, s = _silu_parts(hc[1])
            dhc = (d * act, d * hc[0] * (s + act * (1.0 - s)))
            new = []
            for half in range(2):
                dhc_ref[half, pl.ds(r0, rc), :] = dhc[half]
                for k in range(3):
                    new.append(acc[4 * half + k] + jnp.sum(dhc[half] * taps[half][k], axis=0, keepdims=True))
                new.append(acc[4 * half + 3] + jnp.sum(dhc[half], axis=0, keepdims=True))
            return tuple(new)

        zero = jnp.zeros((1, LANE), F32)
        acc = lax.fori_loop(0, L // rc, chunk, (zero,) * 8)
        dcw_ref[...] = jnp.zeros_like(dcw_ref)
        for half in range(2):
            for k in range(4):
                dcw_ref[half, k:k + 1, :] = acc[4 * half + k]

        def chunk2(ci, _):
            r0 = pl.multiple_of(ci * rc, rc)
            for half in range(2):
                up, cur, down = _shifted(dhc_ref.at[half], r0, rc, L)
                dh_ref[half, pl.ds(r0, rc), :] = (down * cw_ref[half, 0:1, :] + cur * cw_ref[half, 1:2, :]
                                                  + up * cw_ref[half, 2:3, :]).astype(dh_ref.dtype)
            return 0

        lax.fori_loop(0, L // rc, chunk2, 0)

    return _pcall(body, name=name, grid=(FF // LANE,),
                  in_specs=[pl.BlockSpec((2, L, LANE), lambda j: (0, 0, j)),
                            pl.BlockSpec((2, 8, LANE), lambda j: (0, 0, j)),
                            pl.BlockSpec((L, LANE), lambda j: (0, j))],
                  out_specs=[pl.BlockSpec((2, L, LANE), lambda j: (0, 0, j)),
                             pl.BlockSpec((2, 8, LANE), lambda j: (0, 0, j))],
                  out_shape=[jax.ShapeDtypeStruct((2, L, FF), _ACT), jax.ShapeDtypeStruct((2, 8, FF), F32)],
                  scratch_shapes=[pltpu.VMEM((2, L, LANE), F32)], ins=(h, cw, dact), sem=("parallel",), rider=rider)


POOL_ROWS = 256
POOL_TILE = 256


def _pool_bands(transpose):
    i = jnp.arange(POOL_ROWS)[:, None]
    j = jnp.arange(POOL_ROWS + 2 * POOL_HALO)[None, :] - POOL_HALO
    bands = []
    for w in POOL_WINDOWS:
        lo, hi = (-(w // 2 - 1), w // 2) if transpose else (-(w // 2), w // 2 - 1)
        bands.append(((j - i >= lo) & (j - i <= hi)).astype(_ACT))
    return jnp.stack(bands)


def _pool_window(name, u, transpose, out_dtype):
    L, D = u.shape
    ng = len(POOL_WINDOWS)
    pc = D // ng
    tn = min(POOL_TILE, pc)
    rc = _row_tile(L, POOL_ROWS)
    bands = _pool_bands(transpose)
    if rc != POOL_ROWS:
        bands = bands[:, :rc, :rc + 2 * POOL_HALO]
    halo = POOL_HALO

    def body(u_ref, band_ref, o_ref, hi_ref, lo_ref):
        g = (pl.program_id(0) * tn) // pc
        half = jnp.zeros((1, 1), jnp.int32)
        for k, w in enumerate(POOL_WINDOWS):
            half = jnp.where(g == k, w // 2, half)
        zeros = jnp.zeros((halo, tn), _ACT)
        for ref in (hi_ref, lo_ref):
            ref[0:halo, :] = zeros
            ref[halo + L:2 * halo + L, :] = zeros

        def inv_count(r0):
            t = r0 + lax.broadcasted_iota(jnp.int32, (rc, tn), 0)
            lo = jnp.clip(t - half, 0, L - 1)
            hi = jnp.clip(t + half - 1, 0, L - 1)
            return 1.0 / (hi - lo + 1).astype(F32)

        def split(ci, _):
            r0 = pl.multiple_of(ci * rc, rc)
            v = u_ref[pl.ds(r0, rc), :].astype(F32)
            if transpose:
                v = v * inv_count(r0)
            hi = v.astype(_ACT)
            dst = pl.ds(pl.multiple_of(r0 + halo, halo), rc)
            hi_ref[dst, :] = hi
            lo_ref[dst, :] = (v - hi.astype(F32)).astype(_ACT)
            return 0

        lax.fori_loop(0, L // rc, split, 0)
        band = band_ref[...]

        def chunk(ci, _):
            r0 = pl.multiple_of(ci * rc, rc)
            win = pl.ds(r0, rc + 2 * halo)
            s = (jnp.dot(band, hi_ref[win, :], preferred_element_type=F32)
                 + jnp.dot(band, lo_ref[win, :], preferred_element_type=F32))
            if not transpose:
                s = s * inv_count(r0)
            o_ref[pl.ds(r0, rc), :] = (s - u_ref[pl.ds(r0, rc), :].astype(F32)).astype(o_ref.dtype)
            return 0

        lax.fori_loop(0, L // rc, chunk, 0)

    return pl.pallas_call(body, name=name, grid=(D // tn,),
                          in_specs=[pl.BlockSpec((L, tn), lambda j: (0, j)),
                                    pl.BlockSpec((None, rc, rc + 2 * halo), lambda j: ((j * tn) // pc, 0, 0))],
                          out_specs=pl.BlockSpec((L, tn), lambda j: (0, j)),
                          out_shape=jax.ShapeDtypeStruct((L, D), out_dtype),
                          scratch_shapes=[pltpu.VMEM((L + 2 * halo, tn), _ACT), pltpu.VMEM((L + 2 * halo, tn), _ACT)],
                          compiler_params=_params(("parallel",)))(u, bands)


S5_ROWS = 512


def _slab(start):
    return pl.ds(start if isinstance(start, int) else pl.multiple_of(start, NSEG), NSEG)


def _cmul(ar, ai, br, bi):
    return ar * br - ai * bi, ar * bi + ai * br


def _cpow(ar, ai, n):
    rr, ri = None, None
    br, bi = ar, ai
    while n:
        if n & 1:
            rr, ri = (br, bi) if rr is None else _cmul(rr, ri, br, bi)
        n >>= 1
        if n:
            br, bi = _cmul(br, bi, br, bi)
    return rr, ri


def _pow_table(pw_ref, ar, ai, n):
    W = ar.shape[1]
    pr, pi = ar, ai
    for r in range(NSEG):
        pw_ref[0, r:r + 1, :] = pr
        pw_ref[1, r:r + 1, :] = pi
        if r < NSEG - 1:
            pr, pi = _cmul(pr, pi, ar, ai)
    a8r, a8i = (jnp.broadcast_to(v, (NSEG, W)) for v in _cpow(ar, ai, NSEG))

    def step(k, carry):
        nr, ni = _cmul(carry[0], carry[1], a8r, a8i)
        pw_ref[0, _slab(k * NSEG), :] = nr
        pw_ref[1, _slab(k * NSEG), :] = ni
        return nr, ni

    lax.fori_loop(1, n // NSEG, step, (pw_ref[0, 0:NSEG, :], pw_ref[1, 0:NSEG, :]))


def _seg_scan(sr_ref, si_ref, tmp_ref, pw_ref, row0, n, ar, ai, h0, rev, conj=False, pair_with=None):
    W = ar.shape[1]
    arb, aib = jnp.broadcast_to(ar, (NSEG, W)), jnp.broadcast_to(ai, (NSEG, W))

    def rows(s):
        t = (n - 1 - s) if rev else s
        return _slab(row0 + t * NSEG)

    def step(s, carry):
        hr, hi = carry
        sl = rows(s)
        nr = arb * hr - aib * hi + sr_ref[sl, :]
        ni = arb * hi + aib * hr + si_ref[sl, :]
        sr_ref[sl, :] = nr
        si_ref[sl, :] = ni
        return nr, ni

    zero = jnp.zeros((NSEG, W), F32)
    fr, fi = lax.fori_loop(0, n, step, (zero, zero), unroll=2)
    tmp_ref[0] = fr
    tmp_ref[1] = fi
    anr, ani = _cpow(ar, ai, n)
    cr, ci = h0
    for j in (range(NSEG - 1, -1, -1) if rev else range(NSEG)):
        tmp_ref[2, j:j + 1, :] = cr
        tmp_ref[3, j:j + 1, :] = ci
        pr, pi = _cmul(anr, ani, cr, ci)
        cr, ci = tmp_ref[0, j:j + 1, :] + pr, tmp_ref[1, j:j + 1, :] + pi
    cmr, cmi = tmp_ref[2], tmp_ref[3]

    def fix(k, acc):
        for r in range(NSEG):
            row = pl.ds(pl.multiple_of(k * NSEG, NSEG) + r, 1)
            pr = jnp.broadcast_to(pw_ref[0, row, :], (NSEG, W))
            pi = jnp.broadcast_to(pw_ref[1, row, :], (NSEG, W))
            s = k * NSEG + r
            sl = rows(s)
            if conj:
                gr = sr_ref[sl, :] + (pr * cmr + pi * cmi)
                gi = si_ref[sl, :] + (pr * cmi - pi * cmr)
            else:
                gr = sr_ref[sl, :] + (pr * cmr - pi * cmi)
                gi = si_ref[sl, :] + (pr * cmi + pi * cmr)
            sr_ref[sl, :] = gr
            si_ref[sl, :] = gi
            if pair_with is not None:
                prev = rows(jnp.minimum(s + 1, n - 1))
                hpr, hpi = pair_with[0][prev, :], pair_with[1][prev, :]
                acc = (acc[0] + hpr * gr + hpi * gi, acc[1] + hpr * gi - hpi * gr)
        return acc

    if pair_with is None:
        lax.fori_loop(0, n // NSEG, fix, 0)
        return (cr, ci), (cmr, cmi), None
    total = lax.fori_loop(0, n // NSEG, fix, (zero, zero))
    h_r, h_i, (hcr, hci) = pair_with
    last = rows(n - 1)
    fr, fi = hcr - h_r[last, :], hci - h_i[last, :]
    gr, gi = sr_ref[last, :], si_ref[last, :]
    total = (total[0] + fr * gr + fi * gi, total[1] + fr * gi - fi * gr)
    return (cr, ci), (cmr, cmi), total


def _gelu_tanh(y):
    k = math.sqrt(2.0 / math.pi)
    t = jnp.tanh(k * (y + 0.044715 * y * y * y))
    return 0.5 * y * (1.0 + t), t


def _s5_chunks(L):
    rc = _row_tile(L, S5_ROWS)
    return [(r, rc) for r in range(0, L, rc)]


_NT_DIMS = (((1,), (1,)), ((), ()))
_TN_DIMS = (((0,), (0,)), ((), ()))
_LOG_P, _LOG_CH = S5_P.bit_length() - 1, S5_CH.bit_length() - 1


def _same_group(shape, row_shift, col_shift):
    rows = lax.broadcasted_iota(jnp.int32, shape, 0)
    cols = lax.broadcasted_iota(jnp.int32, shape, 1)
    return lax.shift_right_logical(rows, row_shift) == lax.shift_right_logical(cols, col_shift)


def _s5_bt(bt):
    full = jnp.concatenate([bt] * S5_TILE_G, axis=0)
    return jnp.where(_same_group(full.shape, _LOG_P, _LOG_CH), full, 0.0).astype(_ACT)


def _s5_ct(ct):
    full = jnp.concatenate([ct] * S5_TILE_G, axis=0)
    return jnp.where(_same_group(full.shape, _LOG_CH, _LOG_P), full, 0.0).astype(_ACT)


def _s5_diag(m):
    m = jnp.where(_same_group(m.shape, _LOG_CH, _LOG_P), m, 0.0)
    rows = lax.broadcasted_iota(jnp.int32, (S5_TILE_W, S5_P), 0)
    cols = lax.broadcasted_iota(jnp.int32, (S5_TILE_W, S5_P), 1)
    pick = (jnp.bitwise_and(rows, S5_P - 1) == cols).astype(_ACT)
    hi = m.astype(_ACT)
    lo = (m - hi.astype(F32)).astype(_ACT)
    return jnp.dot(hi, pick, preferred_element_type=F32) + jnp.dot(lo, pick, preferred_element_type=F32)


def _s5_project(u_ref, uc_ref, bre, bim, sr_ref, si_ref, L, LC):
    for ref, base, n in ((u_ref, 0, L), (uc_ref, L, LC)):
        for r, rc in _s5_chunks(n):
            ub = ref[r:r + rc, :].astype(_ACT)
            sr_ref[base + r:base + r + rc, :] = lax.dot_general(ub, bre, _NT_DIMS, preferred_element_type=F32)
            si_ref[base + r:base + r + rc, :] = lax.dot_general(ub, bim, _NT_DIMS, preferred_element_type=F32)


def _s5_states(sr_ref, si_ref, tmp_ref, pw_ref, ar, ai, L, LC, rev):
    W = ar.shape[1]
    zero = (jnp.zeros((1, W), F32), jnp.zeros((1, W), F32))
    hctx, cm_ctx, _ = _seg_scan(sr_ref, si_ref, tmp_ref, pw_ref, L, LC // NSEG, ar, ai, zero, rev)
    _, cm_lat, _ = _seg_scan(sr_ref, si_ref, tmp_ref, pw_ref, 0, L // NSEG, ar, ai, hctx, rev)
    return cm_lat, cm_ctx


def _s5_fwd(u, uc, bblk, cblk, apar, dsk, rider=None):
    L, D = u.shape
    LC = uc.shape[0]
    NT, W, TC = D // S5_TILE_CH, S5_TILE_W, S5_TILE_CH

    def body(u_ref, uc_ref, b_ref, c_ref, a_ref, d_ref, y_ref, z_ref, sr_ref, si_ref, tmp_ref, pw_ref):
        for r, rc in _s5_chunks(L):
            y_ref[r:r + rc, :] = u_ref[r:r + rc, :].astype(F32) * d_ref[0:1, :]
        for d in range(2):
            ar, ai = a_ref[2 * d:2 * d + 1, :], a_ref[2 * d + 1:2 * d + 2, :]
            _pow_table(pw_ref, ar, ai, L // NSEG)
            _s5_project(u_ref, uc_ref, _s5_bt(b_ref[2 * d]), _s5_bt(b_ref[2 * d + 1]), sr_ref, si_ref, L, LC)
            _s5_states(sr_ref, si_ref, tmp_ref, pw_ref, ar, ai, L, LC, rev=(d == 1))
            cre, cim = _s5_ct(c_ref[2 * d]), _s5_ct(c_ref[2 * d + 1])
            for r, rc in _s5_chunks(L):
                y_ref[r:r + rc, :] += (
                    lax.dot_general(sr_ref[r:r + rc, :].astype(_ACT), cre, _NT_DIMS, preferred_element_type=F32)
                    - lax.dot_general(si_ref[r:r + rc, :].astype(_ACT), cim, _NT_DIMS, preferred_element_type=F32))
        for r, rc in _s5_chunks(L):
            z_ref[r:r + rc, :] = _gelu_tanh(y_ref[r:r + rc, :])[0].astype(z_ref.dtype)

    col = lambda n: pl.BlockSpec((n, TC), lambda j: (0, j))
    return _pcall(
        body, name="s5_fwd", grid=(NT,), ins=(u, uc, bblk, cblk, apar, dsk), sem=("parallel",), rider=rider,
        in_specs=[col(L), col(LC),
                  pl.BlockSpec((None, 4, S5_P, TC), lambda j: (j, 0, 0, 0)),
                  pl.BlockSpec((None, 4, S5_CH, W), lambda j: (j, 0, 0, 0)),
                  pl.BlockSpec((None, 8, W), lambda j: (j, 0, 0)),
                  pl.BlockSpec((8, TC), lambda j: (0, j))],
        out_specs=[col(L), col(L)],
        out_shape=[jax.ShapeDtypeStruct((L, D), F32), jax.ShapeDtypeStruct((L, D), _ACT)],
        scratch_shapes=[pltpu.VMEM((L + LC, W), F32), pltpu.VMEM((L + LC, W), F32), pltpu.VMEM((4, NSEG, W), F32),
                        pltpu.VMEM((2, L // NSEG, W), F32)])


def _s5_bwd(u, uc, dz, y, bblk, cblk, apar, dsk, rider=None):
    L, D = u.shape
    LC = uc.shape[0]
    NT, W, TC = D // S5_TILE_CH, S5_TILE_W, S5_TILE_CH
    nl, nc = L // NSEG, LC // NSEG

    def body(u_ref, uc_ref, dz_ref, y_ref, b_ref, c_ref, a_ref, d_ref,
             du_ref, duc_ref, db_ref, dc_ref, da_ref, dd_ref,
             hr_ref, hi_ref, gr_ref, gi_ref, dy_ref, tmp_ref, pw_ref):
        ddacc = jnp.zeros((1, TC), F32)
        for r, rc in _s5_chunks(L):
            yv = y_ref[r:r + rc, :]
            g, t = _gelu_tanh(yv)
            k = math.sqrt(2.0 / math.pi)
            dg = 0.5 * (1.0 + t) + 0.5 * yv * (1.0 - t * t) * k * (1.0 + 3 * 0.044715 * yv * yv)
            dy = dz_ref[r:r + rc, :].astype(F32) * dg
            uv = u_ref[r:r + rc, :].astype(F32)
            ddacc = ddacc + jnp.sum(dy * uv, axis=0, keepdims=True)
            du_ref[r:r + rc, :] = dy * d_ref[0:1, :]
            dy_ref[r:r + rc, :] = dy.astype(dy_ref.dtype)
        dd_ref[...] = jnp.zeros_like(dd_ref)
        dd_ref[0:1, :] = ddacc
        duc_ref[...] = jnp.zeros_like(duc_ref)
        da_ref[...] = jnp.zeros_like(da_ref)
        tn = _TN_DIMS
        for d in range(2):
            rev = d == 1
            ar, ai = a_ref[2 * d:2 * d + 1, :], a_ref[2 * d + 1:2 * d + 2, :]
            bre, bim = _s5_bt(b_ref[2 * d]), _s5_bt(b_ref[2 * d + 1])
            cre, cim = _s5_ct(c_ref[2 * d]), _s5_ct(c_ref[2 * d + 1])
            _pow_table(pw_ref, ar, ai, nl)
            _s5_project(u_ref, uc_ref, bre, bim, hr_ref, hi_ref, L, LC)
            cm_lat, cm_ctx = _s5_states(hr_ref, hi_ref, tmp_ref, pw_ref, ar, ai, L, LC, rev)
            cml_r, cml_i, cmc_r, cmc_i = cm_lat[0], cm_lat[1], cm_ctx[0], cm_ctx[1]
            dcr = jnp.zeros((TC, W), F32)
            dci = jnp.zeros((TC, W), F32)
            for r, rc in _s5_chunks(L):
                dyb = dy_ref[r:r + rc, :]
                gr_ref[r:r + rc, :] = jnp.dot(dyb, cre, preferred_element_type=F32)
                gi_ref[r:r + rc, :] = -jnp.dot(dyb, cim, preferred_element_type=F32)
                dcr = dcr + lax.dot_general(dyb, hr_ref[r:r + rc, :].astype(_ACT), tn, preferred_element_type=F32)
                dci = dci - lax.dot_general(dyb, hi_ref[r:r + rc, :].astype(_ACT), tn, preferred_element_type=F32)
            dc_ref[2 * d] = _s5_diag(dcr)
            dc_ref[2 * d + 1] = _s5_diag(dci)
            gr_ref[L:L + LC, :] = jnp.zeros((LC, W), F32)
            gi_ref[L:L + LC, :] = jnp.zeros((LC, W), F32)
            zero = (jnp.zeros((1, W), F32), jnp.zeros((1, W), F32))
            glat, _, (lr, li) = _seg_scan(gr_ref, gi_ref, tmp_ref, pw_ref, 0, nl, ar, -ai, zero, not rev, conj=True,
                                          pair_with=(hr_ref, hi_ref, (cml_r, cml_i)))
            _, _, (qr, qi) = _seg_scan(gr_ref, gi_ref, tmp_ref, pw_ref, L, nc, ar, -ai, glat, not rev, conj=True,
                                       pair_with=(hr_ref, hi_ref, (cmc_r, cmc_i)))
            da_ref[2 * d:2 * d + 1, :] = jnp.sum(lr + qr, axis=0, keepdims=True)
            da_ref[2 * d + 1:2 * d + 2, :] = jnp.sum(li + qi, axis=0, keepdims=True)
            dbr = jnp.zeros((TC, W), F32)
            dbi = jnp.zeros((TC, W), F32)
            for ref, oref, base, n in ((u_ref, du_ref, 0, L), (uc_ref, duc_ref, L, LC)):
                for r, rc in _s5_chunks(n):
                    ub = ref[r:r + rc, :].astype(_ACT)
                    gr = gr_ref[base + r:base + r + rc, :].astype(_ACT)
                    gi = gi_ref[base + r:base + r + rc, :].astype(_ACT)
                    dbr = dbr + lax.dot_general(ub, gr, tn, preferred_element_type=F32)
                    dbi = dbi + lax.dot_general(ub, gi, tn, preferred_element_type=F32)
                    oref[r:r + rc, :] += (jnp.dot(gr, bre, preferred_element_type=F32)
                                          + jnp.dot(gi, bim, preferred_element_type=F32))
            db_ref[2 * d] = _s5_diag(dbr)
            db_ref[2 * d + 1] = _s5_diag(dbi)

    col = lambda n: pl.BlockSpec((n, TC), lambda j: (0, j))
    bspec = pl.BlockSpec((None, 4, S5_P, TC), lambda j: (j, 0, 0, 0))
    cspec = pl.BlockSpec((None, 4, S5_CH, W), lambda j: (j, 0, 0, 0))
    gspec = pl.BlockSpec((None, 4, TC, S5_P), lambda j: (j, 0, 0, 0))
    aspec = pl.BlockSpec((None, 8, W), lambda j: (j, 0, 0))
    return _pcall(
        body, name="s5_bwd", grid=(NT,), ins=(u, uc, dz, y, bblk, cblk, apar, dsk), sem=("parallel",),
        vmem=VMEM_LIMIT_BIG, rider=rider,
        in_specs=[col(L), col(LC), col(L), col(L), bspec, cspec, aspec, pl.BlockSpec((8, TC), lambda j: (0, j))],
        out_specs=[col(L), col(LC), gspec, gspec, aspec, pl.BlockSpec((None, 8, TC), lambda j: (j, 0, 0))],
        out_shape=[jax.ShapeDtypeStruct((L, D), F32), jax.ShapeDtypeStruct((LC, D), F32),
                   jax.ShapeDtypeStruct((NT, 4, TC, S5_P), F32), jax.ShapeDtypeStruct((NT, 4, TC, S5_P), F32),
                   jax.ShapeDtypeStruct((NT, 8, W), F32), jax.ShapeDtypeStruct((NT, 8, TC), F32)],
        scratch_shapes=[pltpu.VMEM((L + LC, W), F32), pltpu.VMEM((L + LC, W), F32),
                        pltpu.VMEM((L + LC, W), F32), pltpu.VMEM((L + LC, W), F32),
                        pltpu.VMEM((L, TC), _ACT), pltpu.VMEM((4, NSEG, W), F32), pltpu.VMEM((2, nl, W), F32)])


ADA_ROWS = 16


def _silu_rows(c_ref):
    c = c_ref[...]
    return c * _sigmoid(c)


def _ada_fwd(cmat, ada_w, ada_b):
    nl, D, n = ada_w.shape
    tn = _row_tile(n, 512)

    def body(c_ref, w_ref, b_ref, o_ref):
        a = _silu_rows(c_ref).astype(_ACT)
        o_ref[...] = jnp.dot(a, w_ref[...].astype(_ACT), preferred_element_type=F32) + b_ref[...]

    return pl.pallas_call(body, name="ada_fwd", grid=(nl, n // tn),
                          in_specs=[pl.BlockSpec((ADA_ROWS, D), lambda l, j: (0, 0)),
                                    pl.BlockSpec((None, D, tn), lambda l, j: (l, 0, j)),
                                    pl.BlockSpec((None, 1, tn), lambda l, j: (l, 0, j))],
                          out_specs=pl.BlockSpec((None, ADA_ROWS, tn), lambda l, j: (l, 0, j)),
                          out_shape=jax.ShapeDtypeStruct((nl, ADA_ROWS, n), F32),
                          compiler_params=_params(("parallel", "parallel")))(cmat, ada_w, ada_b)


def _ada_bwd(cmat, ada_w, dm):
    nl, D, n = ada_w.shape
    tn = _row_tile(n, 512)
    nj = n // tn

    def body(c_ref, w_ref, dm_ref, dw_ref, dc_ref):
        c = c_ref[...]
        s = _sigmoid(c)
        a = (c * s).astype(_ACT)
        dmb = dm_ref[...].astype(_ACT)
        dw_ref[...] = lax.dot_general(a, dmb, (((0,), (0,)), ((), ())), preferred_element_type=F32)
        part = lax.dot_general(dmb, w_ref[...].astype(_ACT), (((1,), (1,)), ((), ())), preferred_element_type=F32)
        part = part * (s * (1.0 + c * (1.0 - s)))

        @pl.when(pl.program_id(1) == 0)
        def _():
            dc_ref[...] = part

        @pl.when(pl.program_id(1) > 0)
        def _():
            dc_ref[...] += part

    return pl.pallas_call(body, name="ada_bwd", grid=(nl, nj),
                          in_specs=[pl.BlockSpec((ADA_ROWS, D), lambda l, j: (0, 0)),
                                    pl.BlockSpec((None, D, tn), lambda l, j: (l, 0, j)),
                                    pl.BlockSpec((None, ADA_ROWS, tn), lambda l, j: (l, 0, j))],
                          out_specs=[pl.BlockSpec((None, D, tn), lambda l, j: (l, 0, j)),
                                     pl.BlockSpec((None, ADA_ROWS, D), lambda l, j: (l, 0, 0))],
                          out_shape=[jax.ShapeDtypeStruct((nl, D, n), F32), jax.ShapeDtypeStruct((nl, ADA_ROWS, D), F32)],
                          compiler_params=_params(("parallel", "arbitrary")))(cmat, ada_w, dm)


def _adamw(name, gparts, w, m, v):
    nl, R, C = w.shape
    gparts = [g if isinstance(g, tuple) else (g, 0) for g in gparts]
    n = gparts[0][0].shape[0]
    runs = []
    for l, (g, r0) in enumerate(gparts):
        if runs and runs[-1][0] is g and runs[-1][1] + runs[-1][3] * R == r0:
            runs[-1][3] += 1
        else:
            runs.append([g, r0, l, 1])
    run_of = [q for q, run in enumerate(runs) for _ in range(run[3])]
    tr = R
    part_bytes = len(runs) * n * C * gparts[0][0].dtype.itemsize * 2
    for cand in (4096, 2048, 1024, 512, 256, 128, 64, 32, 16, 8):
        if R % cand == 0 and cand * max(C, LANE) * 4 <= 2 * 1024 * 1024 and cand * part_bytes <= VMEM_LIMIT // 2:
            tr = cand
            break
    nt = R // tr
    bc1 = 1.0 - ADAM_B1 ** ADAM_STEP
    bc2 = 1.0 - ADAM_B2 ** ADAM_STEP

    def body(*refs):
        g_refs = refs[:len(runs)]
        w_ref, m_ref, v_ref, go_ref, d_ref, mo_ref, vo_ref = refs[len(runs):]
        for l in range(nl):
            @pl.when(pl.program_id(0) == l)
            def _(g_ref=g_refs[run_of[l]]):
                g = g_ref[0].astype(F32)
                for j in range(1, n):
                    g = g + g_ref[j].astype(F32)
                m2 = ADAM_B1 * m_ref[...] + (1.0 - ADAM_B1) * g
                v2 = ADAM_B2 * v_ref[...] + (1.0 - ADAM_B2) * (g * g)
                go_ref[...] = g
                mo_ref[...] = m2
                vo_ref[...] = v2
                d_ref[...] = -ADAM_LR * ((m2 / bc1) / (jnp.sqrt(v2 / bc2) + ADAM_EPS) + ADAM_WD * w_ref[...])

    def gspec(run):
        _, r0, l0, count = run
        return pl.BlockSpec((n, tr, C),
                            lambda lyr, i: (0, r0 // tr + jnp.clip((lyr - l0) * nt + i, 0, count * nt - 1), 0))

    row = pl.BlockSpec((None, tr, C), lambda lyr, i: (lyr, i, 0))
    out = jax.ShapeDtypeStruct((nl, R, C), F32)
    return _pcall(body, name=name, grid=(nl, nt), in_specs=[gspec(run) for run in runs] + [row, row, row],
                  out_specs=[row, row, row, row], out_shape=[out, out, out, out],
                  ins=(*[run[0] for run in runs], w, m, v), sem=("arbitrary", "arbitrary"))


def _sum_parts(name, parts):
    n, R, C = parts.shape

    def body(p_ref, o_ref):
        s = p_ref[0]
        for j in range(1, n):
            s = s + p_ref[j]
        o_ref[...] = s

    return pl.pallas_call(body, name=name, out_shape=jax.ShapeDtypeStruct((R, C), F32),
                          compiler_params=_params(None))(parts)


def _discretize(lam_re, lam_im, log_step, b_re, b_im):
    dt = jnp.exp(log_step)[:, None]
    mag = jnp.exp(lam_re * dt)
    abar_re = mag * jnp.cos(lam_im * dt)
    abar_im = mag * jnp.sin(lam_im * dt)
    nr, ni = abar_re - 1.0, abar_im
    den = lam_re * lam_re + lam_im * lam_im
    fr = (nr * lam_re + ni * lam_im) / den
    fi = (ni * lam_re - nr * lam_im) / den
    bbar_re = fr[..., None] * b_re - fi[..., None] * b_im
    bbar_im = fr[..., None] * b_im + fi[..., None] * b_re
    return abar_re, abar_im, bbar_re, bbar_im


def _s5_pack(abar, bbar, cmat):
    G = abar[0][0].shape[0]
    NT = G // S5_TILE_G
    a4 = jnp.stack([abar[d][r] for d in range(2) for r in range(2)]).reshape(4, NT, S5_TILE_W).transpose(1, 0, 2)
    apar = jnp.concatenate([a4, jnp.zeros((NT, 4, S5_TILE_W), F32)], axis=1)
    b4 = jnp.stack([bbar[d][r] for d in range(2) for r in range(2)]).reshape(4, NT, S5_TILE_G, S5_P, S5_CH)
    bt = b4.transpose(1, 0, 3, 2, 4).reshape(NT, 4, S5_P, S5_TILE_CH)
    c4 = jnp.stack([cmat[d][r] for d in range(2) for r in range(2)]).reshape(4, NT, S5_TILE_G, S5_CH, S5_P)
    ct = c4.transpose(1, 0, 3, 2, 4).reshape(NT, 4, S5_CH, S5_TILE_W)
    return apar, bt, ct


def _s5_unpack(dapar, dbd, dcd, G):
    NT = G // S5_TILE_G
    da = dapar[:, :4, :].reshape(NT, 2, 2, S5_TILE_G, S5_P).transpose(1, 2, 0, 3, 4).reshape(2, 2, G, S5_P)
    db = dbd.reshape(NT, 4, S5_TILE_G, S5_CH, S5_P).transpose(1, 0, 2, 4, 3).reshape(2, 2, G, S5_P, S5_CH)
    dc = dcd.reshape(NT, 4, S5_TILE_G, S5_CH, S5_P).transpose(1, 0, 2, 3, 4).reshape(2, 2, G, S5_CH, S5_P)
    return da, db, dc


def _to_segments(a):
    L, D = a.shape
    return a.reshape(NSEG, L // NSEG, D).transpose(1, 0, 2).reshape(L, D)


def _from_segments(a):
    L, D = a.shape
    return a.reshape(L // NSEG, NSEG, D).transpose(1, 0, 2).reshape(L, D)


def _pos_emb(n_tokens, dim):
    rows = n_tokens // GRID_W
    quarter = dim // 4
    omega = 1.0 / (POS_BASE ** (jnp.arange(quarter, dtype=F32) / quarter))

    def enc(p):
        ang = p[:, None] * omega[None, :]
        return jnp.concatenate([jnp.sin(ang), jnp.cos(ang)], axis=-1)

    rtab = enc(jnp.arange(rows, dtype=F32))
    ctab = enc(jnp.arange(GRID_W, dtype=F32))
    return jnp.concatenate([jnp.repeat(rtab, GRID_W, axis=0), jnp.tile(ctab, (rows, 1))], axis=-1)


def _vec(D, **rows):
    names = {"gpost": V_GPOST, "gate": V_GATE, "yscale": V_YSCALE, "gpre": V_GPRE, "shift": V_SHIFT, "scale": V_SCALE}
    out = [jnp.zeros((D,), F32)] * 8
    out[V_YSCALE] = jnp.ones((D,), F32)
    for k, v in rows.items():
        out[names[k]] = v.reshape(D).astype(F32)
    return jnp.stack(out)


def _row0(v, D):
    return jnp.concatenate([v.reshape(1, D).astype(F32), jnp.zeros((7, D), F32)], axis=0)


def _my_block(full, axis, n_local):
    return lax.dynamic_slice_in_dim(full, _my_index() * n_local, n_local, axis)


def kernel(x, c, ctx, c_ctx, ada_w, ada_b, norm_g, s5_lam_re, s5_lam_im, s5_log_step, s5_b_re, s5_b_im, s5_c_re, s5_c_im, s5_d, s5_glu_w, pool_w, pool_scale, ffn_up, ffn_conv, ffn_conv_b, ffn_down, loss_target, m_c_ctx, m_ada_w, m_ada_b, m_norm_g, m_s5_lam_re, m_s5_lam_im, m_s5_log_step, m_s5_b_re, m_s5_b_im, m_s5_c_re, m_s5_c_im, m_s5_d, m_s5_glu_w, m_pool_w, m_pool_scale, m_ffn_up, m_ffn_conv, m_ffn_conv_b, m_ffn_down, v_c_ctx, v_ada_w, v_ada_b, v_norm_g, v_s5_lam_re, v_s5_lam_im, v_s5_log_step, v_s5_b_re, v_s5_b_im, v_s5_c_re, v_s5_c_im, v_s5_d, v_s5_glu_w, v_pool_w, v_pool_scale, v_ffn_up, v_ffn_conv, v_ffn_conv_b, v_ffn_down):
    L, D = x.shape[1], x.shape[2]
    LC = ctx.shape[1]
    G = s5_lam_re.shape[2]
    n_ada = ada_w.shape[2]
    nb_up = ffn_up.shape[2]
    r_down = ffn_down.shape[1]
    FF = N_DEV * r_down
    n_pool = len(POOL_WINDOWS)
    pc = D // n_pool
    pr = pool_w.shape[2]
    ng_loc = norm_g.shape[2]
    me = _my_index()
    axes = ("x", "y", "c")

    up_b = [ffn_up[i].astype(_ACT) for i in range(2)]
    down_b = [ffn_down[i].astype(_ACT) for i in range(2)]
    glu_b = s5_glu_w[0].astype(_ACT)
    pool_b = pool_w[0].reshape(n_pool * pr, pc).astype(_ACT)

    small_loc = jnp.concatenate([c.reshape(-1), norm_g.reshape(-1), pool_scale.reshape(-1), ffn_conv.reshape(-1)])
    n_small = small_loc.shape[0]
    small_g, = _exchange([[jnp.pad(small_loc, (0, (-n_small) % LANE)).reshape(1, -1)]], mode="gather", name="gather_small")
    small_g = small_g.reshape(N_DEV, -1)
    o = 0
    c_all = small_g[:, o:o + D]
    o += D
    ng_all = small_g[:, o:o + 8 * ng_loc].reshape(N_DEV, 2, 4, ng_loc).transpose(1, 2, 0, 3).reshape(2, 4, D)
    o += 8 * ng_loc
    pscale_all = small_g[:, o:o + ng_loc].reshape(D)
    o += ng_loc
    conv_all = small_g[:, o:o + 6 * nb_up].reshape(N_DEV, 2, 3, nb_up).transpose(1, 2, 0, 3).reshape(2, 3, 2 * FF)

    cmat = jnp.concatenate([c_all, c_ctx.reshape(1, D), jnp.zeros((ADA_ROWS - N_DEV - 1, D), F32)], axis=0)
    ada_b_loc = _my_block(ada_b, 1, n_ada).reshape(2, 1, n_ada)
    mods_loc = _ada_fwd(cmat, ada_w, ada_b_loc)
    mods_g, = _exchange([[mods_loc]], mode="gather", name="gather_mods")
    mods_rows = mods_g.reshape(N_DEV, 2, ADA_ROWS, n_ada).transpose(1, 2, 0, 3).reshape(2, ADA_ROWS, 6, D)
    mod = lax.dynamic_index_in_dim(mods_rows, me, axis=1, keepdims=False)
    mod_c = mods_rows[0, N_DEV]

    def disc_all(lr, li, ls, br, bi):
        return [_discretize(lr[d], li[d], ls[d], br[d], bi[d]) for d in range(2)]

    disc, disc_vjp = jax.vjp(disc_all, s5_lam_re[0], s5_lam_im[0], s5_log_step[0], s5_b_re[0], s5_b_im[0])
    apar, bblk, cblk = _s5_pack([(disc[d][0], disc[d][1]) for d in range(2)],
                                [(disc[d][2], disc[d][3]) for d in range(2)],
                                [(s5_c_re[0, d], s5_c_im[0, d]) for d in range(2)])
    dsk = _row0(s5_d[0], D)
    cw = []
    for i in range(2):
        taps = conv_all[i].reshape(3, 2, FF).transpose(1, 0, 2)
        cw.append(jnp.concatenate([taps, ffn_conv_b[i].reshape(2, 1, FF), jnp.zeros((2, 4, FF), F32)], axis=1))

    vecs = {
        "b0": _vec(D, gpre=ng_all[0, 0], shift=mod[0, 0], scale=mod[0, 1]),
        "c0": _vec(D, gpre=ng_all[0, 0], shift=mod_c[0], scale=mod_c[1]),
        "b1": _vec(D, gpost=ng_all[0, 1], gate=mod[0, 2], gpre=ng_all[0, 2], shift=mod[0, 3], scale=mod[0, 4]),
        "b2": _vec(D, gpost=ng_all[0, 3], gate=mod[0, 5], gpre=ng_all[1, 0], shift=mod[1, 0], scale=mod[1, 1]),
        "b3": _vec(D, gpost=ng_all[1, 1], gate=mod[1, 2], yscale=pscale_all, gpre=ng_all[1, 2], shift=mod[1, 3],
                   scale=mod[1, 4]),
        "b4": _vec(D, gpost=ng_all[1, 3], gate=mod[1, 5]),
    }

    x0, u0 = _rows_fwd("rows_fwd_b0", x[0], _pos_emb(L, D), vecs["b0"], add=True, u_dtype=_ACT)
    uc, = _rows_fwd("rows_fwd_ctx", ctx[0], None, vecs["c0"], want_x=False, u_dtype=_ACT)
    u0s, ucs = _to_segments(u0), _to_segments(uc)
    (y_s5, z_s5), (glu_g, up_g0, down_g0) = _s5_fwd(u0s, ucs, bblk, cblk, apar, dsk,
                                                    rider=([[glu_b], [up_b[0]], [down_b[0]]], "gather2"))
    vg = _colblock_fwd("glu_fwd_mm", z_s5, glu_g, 0, _ACT)
    mix0 = _from_segments(_glu_fwd("glu_fwd", vg))
    x1, un0 = _rows_fwd("rows_fwd_b1", x0, mix0, vecs["b1"], u_dtype=_ACT)
    h0, (up_g1,) = _colblock_fwd("ffn0_up", un0, up_g0, 0, _ACT, rider=([[up_b[1]]], "gather2"))
    act0 = _conv_swiglu_fwd("ffn0_conv", h0, cw[0])
    f0, (down_g1, pool_g) = _rowblock_fwd("ffn0_down", act0, down_g0, 0, rider=([[down_b[1]], [pool_b]], "gather2"))
    pool_full = pool_g.reshape(N_DEV, n_pool, pr, pc).transpose(1, 0, 2, 3).reshape(n_pool, pc, pc)
    x2, u1 = _rows_fwd("rows_fwd_b2", x1, f0, vecs["b2"], u_dtype=F32)
    p1 = _pool_window("pool_fwd", u1, False, _ACT)
    ypre1 = _group_mm("pool_fwd_mm", p1, pool_full, "nn", F32)
    x3, un1 = _rows_fwd("rows_fwd_b3", x2, ypre1, vecs["b3"], u_dtype=_ACT)
    h1 = _colblock_fwd("ffn1_up", un1, up_g1, 0, _ACT)
    act1 = _conv_swiglu_fwd("ffn1_conv", h1, cw[1])
    f1 = _rowblock_fwd("ffn1_down", act1, down_g1, 0)
    dx4, loss_blk, df1, red4 = _rows_fwd("rows_fwd_b4", x3, f1, vecs["b4"], target=loss_target[0])
    loss = lax.psum(loss_blk[0, 0], axes)

    dact1 = _rowblock_dgrad("ffn1_down_dgrad", df1, down_g1, 0)
    ddown1 = _rowblock_wgrad("ffn1_down_wgrad", act1, df1)
    ddown1 = ddown1.reshape(N_DEV, r_down, D)
    (dh1, dcw1), (gp_down1a,) = _conv_swiglu_bwd("ffn1_conv_bwd", h1, cw[1], dact1,
                                                rider=([[(ddown1, (0, r_down // 2))]], "scatter"))
    dun1, (gp_down1b,) = _colblock_dgrad("ffn1_up_dgrad", dh1, up_g1, 0, F32,
                                         rider=([[(ddown1, (r_down // 2, r_down // 2))]], "scatter"))
    dup1 = _colblock_wgrad("ffn1_up_wgrad", un1, dh1)
    dx3, dypre1, red3 = _rows_bwd("rows_bwd_b3", dx4, dun1, x3, ypre1, vecs["b3"], dy_dtype=_ACT, yscale_grad=True)
    dp1 = _group_mm("pool_dgrad", dypre1, pool_full, "nt", F32)
    dpool = _group_wgrad("pool_wgrad", p1, dypre1, n_pool)
    du1 = _pool_window("pool_bwd", dp1, True, F32)
    dx2, df0, red2 = _rows_bwd("rows_bwd_b2", dx3, du1, x2, f0, vecs["b2"], dy_dtype=_ACT)
    dact0 = _rowblock_dgrad("ffn0_down_dgrad", df0, down_g0, 0)
    ddown0 = _rowblock_wgrad("ffn0_down_wgrad", act0, df0)
    ddown0 = ddown0.reshape(N_DEV, r_down, D)
    (dh0, dcw0), (gp_down0a,) = _conv_swiglu_bwd("ffn0_conv_bwd", h0, cw[0], dact0,
                                                 rider=([[(ddown0, (0, r_down // 2))]], "scatter"))
    dun0, (gp_up1a,) = _colblock_dgrad("ffn0_up_dgrad", dh0, up_g0, 0, F32,
                                       rider=([[(dup1, (0, D // 2))]], "scatter"))
    dup0, (gp_up1b,) = _colblock_wgrad("ffn0_up_wgrad", un0, dh0, rider=([[(dup1, (D // 2, D // 4))]], "scatter"))
    dx1, dmix0, red1 = _rows_bwd("rows_bwd_b1", dx2, dun0, x1, mix0, vecs["b1"])
    dvg = _glu_bwd("glu_bwd", vg, _to_segments(dmix0))
    dz = _colblock_dgrad("glu_dgrad", dvg, glu_g, 0, _ACT)
    dglu = _colblock_wgrad("glu_wgrad", z_s5, dvg)
    dpool_blocks = dpool.reshape(n_pool, N_DEV, pr, pc).transpose(1, 0, 2, 3).reshape(N_DEV, n_pool * pr, pc)
    (du0s, ducs, dbblk, dcblk, dapar, ddsk), (gp_up0, gp_glu, gp_pool, gp_up1c, gp_down0b) = _s5_bwd(
        u0s, ucs, dz, y_s5, bblk, cblk, apar, dsk,
        rider=([[dup0], [dglu], [dpool_blocks], [(dup1, (3 * D // 4, D // 4))], [(ddown0, (r_down // 2, r_down // 2))]],
               "scatter"))
    grad_x, red0 = _rows_bwd("rows_bwd_b0", dx1, _from_segments(du0s), x0, None, vecs["b0"])
    redc, = _rows_bwd("rows_bwd_ctx", None, _from_segments(ducs), ctx[0], None, vecs["c0"], want_dx=False)

    zero_d = jnp.zeros((D,), F32)
    dmod = jnp.stack([
        jnp.stack([red0[R_SHIFT], red0[R_SCALE], red1[R_GATE], red1[R_SHIFT], red1[R_SCALE], red2[R_GATE]]),
        jnp.stack([red2[R_SHIFT], red2[R_SCALE], red3[R_GATE], red3[R_SHIFT], red3[R_SCALE], red4[R_GATE]])])
    dmod_c = jnp.stack([jnp.stack([redc[R_SHIFT], redc[R_SCALE]] + [zero_d] * 4), jnp.zeros((6, D), F32)])
    dm_g, = _exchange([[jnp.stack([dmod, dmod_c], axis=1).reshape(2, 2, 6 * D)]], mode="gather", name="gather_dmods")
    dm_g = dm_g.reshape(N_DEV, 2, 2, 6 * D)
    dm_ctx = _sum_parts("sum_dmod_ctx", dm_g[:, :, 1, :])
    dm_rows = jnp.concatenate([dm_g[:, :, 0, :].transpose(1, 0, 2), dm_ctx[:, None, :]], axis=1)
    grad_ada_b = _sum_parts("sum_ada_b", dm_rows.transpose(1, 0, 2))
    dm_cols = dm_rows.reshape(2, N_DEV + 1, N_DEV, n_ada)
    dm_mine = lax.dynamic_index_in_dim(dm_cols, me, axis=2, keepdims=False)
    dm_mine = jnp.concatenate([dm_mine, jnp.zeros((2, ADA_ROWS - N_DEV - 1, n_ada), F32)], axis=1)
    grad_ada_w, dcond = _ada_bwd(cmat, ada_w, dm_mine)
    dcctx_part = dcond[0, N_DEV] + dcond[1, N_DEV]

    da, db, dc = _s5_unpack(dapar, dbblk, dcblk, G)
    dnorm = jnp.stack([
        jnp.stack([red0[R_GPRE] + redc[R_GPRE], red1[R_GPOST], red1[R_GPRE], red2[R_GPOST]]),
        jnp.stack([red2[R_GPRE], red3[R_GPOST], red3[R_GPRE], red4[R_GPOST]])])
    dconv = jnp.stack([d[:, :3, :].transpose(1, 0, 2).reshape(3, 2 * FF) for d in (dcw0, dcw1)])
    dconv_b = jnp.stack([d[:, 3, :].reshape(2 * FF) for d in (dcw0, dcw1)])
    pieces = [dcctx_part, dnorm, da, db, dc, ddsk[:, 0, :], red3[R_YSCALE], dconv, dconv_b]
    flat = jnp.concatenate([p.reshape(-1) for p in pieces])
    n_flat = flat.shape[0]
    per_dev = -(-n_flat // (N_DEV * 8 * LANE)) * 8 * LANE
    flat = jnp.pad(flat, (0, N_DEV * per_dev - n_flat)).reshape(N_DEV, per_dev // LANE, LANE)
    parts, = _exchange([[flat]], mode="scatter", name="scatter_small_grads")
    mine = _sum_parts("sum_small_grads", parts.reshape(N_DEV, per_dev // LANE, LANE))
    summed, = _exchange([[mine]], mode="gather", name="gather_small_grads")
    summed = summed.reshape(-1)
    red_pieces, o = [], 0
    for p in pieces:
        red_pieces.append(summed[o:o + p.size].reshape(p.shape))
        o += p.size
    g_cctx, g_norm, g_a, g_b, g_c, g_d, g_pscale, g_conv, g_conv_b = red_pieces
    cot = [(g_a[d, 0], g_a[d, 1], g_b[d, 0], g_b[d, 1]) for d in range(2)]
    g_lam_re, g_lam_im, g_log_step, g_b_re, g_b_im = disc_vjp(cot)

    out = {}

    def put(name, res, shape):
        out[name] = tuple(r.reshape(shape) for r in res)

    gp_up0, gp_up1a = gp_up0.reshape(N_DEV, D, nb_up), gp_up1a.reshape(N_DEV, D // 2, nb_up)
    quarters = (8, D // 4, nb_up)
    put("ffn_up", _adamw("adamw_ffn_up",
                         [(gp_up0, q * D // 4) for q in range(4)] + [(gp_up1a, 0), (gp_up1a, D // 4),
                                                                     gp_up1b.reshape(N_DEV, D // 4, nb_up),
                                                                     gp_up1c.reshape(N_DEV, D // 4, nb_up)],
                         ffn_up.reshape(quarters), m_ffn_up.reshape(quarters), v_ffn_up.reshape(quarters)),
        ffn_up.shape)
    halves = (4, r_down // 2, D)
    put("ffn_down", _adamw("adamw_ffn_down",
                           [g.reshape(N_DEV, r_down // 2, D) for g in (gp_down0a, gp_down0b, gp_down1a, gp_down1b)],
                           ffn_down.reshape(halves), m_ffn_down.reshape(halves), v_ffn_down.reshape(halves)),
        ffn_down.shape)
    put("s5_glu_w", _adamw("adamw_glu", [gp_glu.reshape(N_DEV, D, -1)], s5_glu_w, m_s5_glu_w, v_s5_glu_w),
        s5_glu_w.shape)
    pool_rows = (1, n_pool * pr, pc)
    put("pool_w", _adamw("adamw_pool", [gp_pool.reshape(N_DEV, n_pool * pr, pc)], pool_w.reshape(pool_rows),
                         m_pool_w.reshape(pool_rows), v_pool_w.reshape(pool_rows)), pool_w.shape)
    put("ada_w", _adamw("adamw_ada_w", [grad_ada_w[i][None] for i in range(2)], ada_w, m_ada_w, v_ada_w), ada_w.shape)

    for nm, w, m, v, g in (("s5_b_re", s5_b_re, m_s5_b_re, v_s5_b_re, g_b_re),
                           ("s5_b_im", s5_b_im, m_s5_b_im, v_s5_b_im, g_b_im),
                           ("s5_c_re", s5_c_re, m_s5_c_re, v_s5_c_re, g_c[:, 0]),
                           ("s5_c_im", s5_c_im, m_s5_c_im, v_s5_c_im, g_c[:, 1])):
        rows = (1, w.size // w.shape[-1], w.shape[-1])
        put(nm, _adamw("adamw_" + nm, [g.reshape(rows)], w.reshape(rows), m.reshape(rows), v.reshape(rows)), w.shape)

    small = [
        ("c_ctx", c_ctx, m_c_ctx, v_c_ctx, g_cctx),
        ("ada_b", ada_b, m_ada_b, v_ada_b, grad_ada_b),
        ("norm_g", norm_g, m_norm_g, v_norm_g, _my_block(g_norm, 2, ng_loc)),
        ("s5_lam_re", s5_lam_re, m_s5_lam_re, v_s5_lam_re, g_lam_re),
        ("s5_lam_im", s5_lam_im, m_s5_lam_im, v_s5_lam_im, g_lam_im),
        ("s5_log_step", s5_log_step, m_s5_log_step, v_s5_log_step, g_log_step),
        ("s5_d", s5_d, m_s5_d, v_s5_d, g_d),
        ("pool_scale", pool_scale, m_pool_scale, v_pool_scale, _my_block(g_pscale, 0, ng_loc)),
        ("ffn_conv", ffn_conv, m_ffn_conv, v_ffn_conv, _my_block(g_conv, 2, nb_up)),
        ("ffn_conv_b", ffn_conv_b, m_ffn_conv_b, v_ffn_conv_b, g_conv_b),
    ]
    n_sm = sum(w.size for _, w, _, _, _ in small)
    rows_sm = -(-n_sm // (512 * LANE)) * 512

    def flat_of(k):
        f = jnp.concatenate([t[k].reshape(-1) for t in small])
        return jnp.pad(f, (0, rows_sm * LANE - n_sm)).reshape(rows_sm, LANE)

    res_sm = _adamw("adamw_small", [flat_of(4)[None]], flat_of(1)[None], flat_of(2)[None], flat_of(3)[None])
    o = 0
    for name, w, _, _, _ in small:
        out[name] = tuple(r.reshape(-1)[o:o + w.size].reshape(w.shape) for r in res_sm)
        o += w.size

    order = ["c_ctx", "ada_w", "ada_b", "norm_g", "s5_lam_re", "s5_lam_im", "s5_log_step", "s5_b_re", "s5_b_im",
             "s5_c_re", "s5_c_im", "s5_d", "s5_glu_w", "pool_w", "pool_scale", "ffn_up", "ffn_conv", "ffn_conv_b",
             "ffn_down"]
    return (loss, grad_x.reshape(x.shape), *[out[n][0] for n in order], *[out[n][1] for n in order],
            *[out[n][2] for n in order], *[out[n][3] for n in order])
```

```python
import functools
import math

import jax
import jax.numpy as jnp
from jax import lax
from jax.experimental import pallas as pl
from jax.experimental.pallas import tpu as pltpu

F32 = jnp.float32
_ACT = jnp.bfloat16
N_DEV = 8
NSEG = 8
S5_CH = 16
S5_P = 64
LANE = 128
S5_TILE_CH = LANE
S5_TILE_G = S5_TILE_CH // S5_CH
S5_TILE_W = S5_TILE_G * S5_P
GRID_W = 64
POOL_WINDOWS = (2, 4, 8, 16)
POOL_HALO = 64
RMS_EPS = 1e-6
POS_BASE = 10000.0
ADAM_LR, ADAM_B1, ADAM_B2, ADAM_EPS, ADAM_WD, ADAM_STEP = 0.001, 0.9, 0.999, 1e-08, 0.01, 10
VMEM_LIMIT = 48 * 1024 * 1024
VMEM_LIMIT_BIG = 58 * 1024 * 1024
MESH = pl.DeviceIdType.MESH
ANY = pl.BlockSpec(memory_space=pl.ANY)


def _params(sem, vmem=VMEM_LIMIT):
    return pltpu.CompilerParams(dimension_semantics=sem, vmem_limit_bytes=vmem)


def _my_index():
    return 4 * lax.axis_index("x") + 2 * lax.axis_index("y") + lax.axis_index("c")


def _xchg_plan(groups, mode):
    flat = [(g, l, a) for g, grp in enumerate(groups) for l, a in enumerate(grp)]
    outs = []
    for grp in groups:
        a, rows = _rows_of(grp[0])
        piece = a.shape[1:] if mode == "scatter" else a.shape
        if rows is not None:
            piece = (rows[1],) + tuple(piece[1:])
        outs.append(jax.ShapeDtypeStruct((N_DEV, len(grp)) + tuple(piece), a.dtype))
    return flat, outs


def _rows_of(entry):
    return entry if isinstance(entry, tuple) else (entry, None)


def _operands(flat):
    return [_rows_of(a)[0] for _, _, a in flat]


def _xchg_sems(n):
    return [pltpu.SemaphoreType.DMA((n, N_DEV - 1)), pltpu.SemaphoreType.DMA((n, N_DEV - 1)),
            pltpu.SemaphoreType.DMA((n,))]


def _xchg_copies(flat, mode, ins, out_refs, sems, waiting=True):
    send_sems, recv_sems, local_sems = sems
    x, y, c = lax.axis_index("x"), lax.axis_index("y"), lax.axis_index("c")
    me = 4 * x + 2 * y + c
    local, first, forwards = [], [], []

    def pair(s, j, dev):
        return dict(send_sem=send_sems.at[s, j], recv_sem=recv_sems.at[s, j], device_id=dev, device_id_type=MESH)

    def block(s, dev):
        rows = _rows_of(flat[s][2])[1]
        ref = ins[s].at[dev]
        return ref if rows is None else ref.at[pl.ds(rows[0], rows[1])]

    for s, (g, l, _) in enumerate(flat):
        src = block(s, me) if mode == "scatter" else ins[s]
        local.append(pltpu.make_async_copy(src, out_refs[g].at[me, l], local_sems.at[s]))
    if mode == "gather2":
        sib, sib_idx = (x, y, 1 - c), 4 * x + 2 * y + (1 - c)
        for s, (g, l, _) in enumerate(flat):
            slot = lambda dev, g=g, l=l: out_refs[g].at[dev, l]
            targets = [(sib, sib_idx)] + [((qx, qy, c), 4 * qx + 2 * qy + c)
                                          for qx, qy in ((1 - x, y), (x, 1 - y), (1 - x, 1 - y))]
            for j, (dev, idx) in enumerate(targets):
                send = pltpu.make_async_remote_copy(src_ref=ins[s], dst_ref=slot(me), **pair(s, j, dev))
                arrive = pltpu.make_async_remote_copy(src_ref=ins[s], dst_ref=slot(idx), **pair(s, j, dev)) if waiting else None
                first.append((send, arrive))
            if waiting:
                for j, (dev, idx) in enumerate(targets[1:]):
                    other = 4 * dev[0] + 2 * dev[1] + (1 - c)
                    send = pltpu.make_async_remote_copy(src_ref=slot(idx), dst_ref=slot(idx), **pair(s, 4 + j, sib))
                    arrive = pltpu.make_async_remote_copy(src_ref=slot(idx), dst_ref=slot(other), **pair(s, 4 + j, sib))
                    forwards.append((first[len(first) - 3 + j][1], send, arrive))
        return local, first, forwards
    for k in range(1, N_DEV):
        px = 1 - x if k & 4 else x
        py = 1 - y if k & 2 else y
        pc = 1 - c if k & 1 else c
        peer = 4 * px + 2 * py + pc
        for s, (g, l, _) in enumerate(flat):
            src = block(s, peer) if mode == "scatter" else ins[s]
            send = pltpu.make_async_remote_copy(src_ref=src, dst_ref=out_refs[g].at[me, l], **pair(s, k - 1, (px, py, pc)))
            arrive = (pltpu.make_async_remote_copy(src_ref=src, dst_ref=out_refs[g].at[peer, l],
                                                   **pair(s, k - 1, (px, py, pc))) if waiting else None)
            first.append((send, arrive))
    return local, first, forwards


def _xchg_start(local, first, forwards):
    for cp in local:
        cp.start()
    for send, _ in first:
        send.start()


def _xchg_wait(local, first, forwards):
    gates = [gate for gate, _, _ in forwards]
    for gate, send, _ in forwards:
        gate.wait_recv()
        send.start()
    for _, arrive in first:
        if not any(arrive is gate for gate in gates):
            arrive.wait_recv()
    for _, _, arrive in forwards:
        arrive.wait_recv()
    for send, _ in first:
        send.wait_send()
    for _, send, _ in forwards:
        send.wait_send()
    for cp in local:
        cp.wait()


def _exchange(groups, mode, name):
    flat, outs = _xchg_plan(groups, mode)
    n = len(flat)

    def body(*refs):
        copies = _xchg_copies(flat, mode, refs[:n], refs[n:n + len(groups)], refs[n + len(groups):])
        _xchg_start(*copies)
        _xchg_wait(*copies)

    res = pl.pallas_call(body, name=name, out_shape=outs, in_specs=[ANY] * n, out_specs=[ANY] * len(groups),
                         scratch_shapes=_xchg_sems(n))(*_operands(flat))
    return list(res)


def _pcall(body, *, name, grid, in_specs, out_specs, out_shape, ins, scratch_shapes=(), sem=None, vmem=VMEM_LIMIT,
           rider=None):
    single = not isinstance(out_shape, (list, tuple))
    if rider is None:
        return pl.pallas_call(body, name=name, grid=grid, in_specs=list(in_specs), out_specs=out_specs,
                              out_shape=out_shape, scratch_shapes=list(scratch_shapes),
                              compiler_params=_params(sem, vmem))(*ins)
    groups, mode = rider
    flat, r_outs = _xchg_plan(groups, mode)
    n_in, n_out = len(ins), 1 if single else len(out_shape)
    nr, ng, ns = len(flat), len(groups), len(scratch_shapes)

    def wrapped(*refs):
        o1 = n_in + nr
        o2 = o1 + n_out
        o3 = o2 + ng
        r_in, r_out, sems = refs[n_in:o1], refs[o2:o3], refs[o3 + ns:]
        first = functools.reduce(jnp.logical_and, [pl.program_id(d) == 0 for d in range(len(grid))])
        last = functools.reduce(jnp.logical_and, [pl.program_id(d) == grid[d] - 1 for d in range(len(grid))])

        @pl.when(first)
        def _():
            _xchg_start(*_xchg_copies(flat, mode, r_in, r_out, sems, waiting=False))

        body(*refs[:n_in], *refs[o1:o2], *refs[o3:o3 + ns])

        @pl.when(last)
        def _():
            _xchg_wait(*_xchg_copies(flat, mode, r_in, r_out, sems))

    outs = pl.pallas_call(
        wrapped, name=name, grid=grid, in_specs=list(in_specs) + [ANY] * nr,
        out_specs=([out_specs] if single else list(out_specs)) + [ANY] * ng,
        out_shape=([out_shape] if single else list(out_shape)) + r_outs,
        scratch_shapes=list(scratch_shapes) + _xchg_sems(nr),
        compiler_params=_params(("arbitrary",) * len(grid), vmem))(*ins, *_operands(flat))
    base = list(outs[:n_out])
    return (base[0] if single else base), list(outs[n_out:])


_DIMS = {"nn": (((1,), (0,)), ((), ())), "nt": (((1,), (1,)), ((), ())), "tn": (((0,), (0,)), ((), ()))}


def _mm(name, a, b, a_spec, b_spec, o_spec, out_shape, grid, dims, rider=None):
    nk = grid[2]
    acc_shape = tuple(d for d in o_spec.block_shape if d is not None)
    dn = _DIMS[dims]

    def tile(ref):
        v = ref[...]
        return v.reshape((-1, v.shape[-1])).astype(_ACT)

    def body(a_ref, b_ref, o_ref, *scratch):
        def part():
            return lax.dot_general(tile(a_ref), tile(b_ref), dn, preferred_element_type=F32)

        if nk == 1:
            o_ref[...] = part().reshape(o_ref.shape).astype(o_ref.dtype)
            return
        acc_ref, = scratch
        k = pl.program_id(2)

        @pl.when(k == 0)
        def _():
            acc_ref[...] = part()

        @pl.when(k > 0)
        def _():
            acc_ref[...] += part()

        @pl.when(k == nk - 1)
        def _():
            o_ref[...] = acc_ref[...].reshape(o_ref.shape).astype(o_ref.dtype)

    acc2d = (math.prod(acc_shape[:-1]), acc_shape[-1])
    return _pcall(body, name=name, out_shape=out_shape, grid=grid, in_specs=[a_spec, b_spec], out_specs=o_spec,
                  scratch_shapes=[] if nk == 1 else [pltpu.VMEM(acc2d, F32)], ins=(a, b),
                  sem=("parallel", "parallel", "arbitrary"), rider=rider)


def _row_tile(n, want):
    t = min(n, want)
    assert n % t == 0, (n, t)
    return t


def _colblock_fwd(name, xa, wg, layer, out_dtype, rider=None):
    L, K = xa.shape
    nb = wg.shape[3]
    half = N_DEV // 2
    tm = _row_tile(L, 512)
    return _mm(name, xa, wg,
               pl.BlockSpec((tm, K), lambda j, i, k: (i, 0)),
               pl.BlockSpec((None, None, K, nb), lambda j, i, k: (j, layer, 0, 0)),
               pl.BlockSpec((None, tm, nb), lambda j, i, k: (j // half, i, j % half)),
               jax.ShapeDtypeStruct((2, L, half * nb), out_dtype), (N_DEV, L // tm, 1), "nn", rider=rider)


def _colblock_dgrad(name, dh, wg, layer, out_dtype, rider=None):
    _, L, _ = dh.shape
    K, nb = wg.shape[2], wg.shape[3]
    half = N_DEV // 2
    tm = _row_tile(L, 512)
    return _mm(name, dh, wg,
               pl.BlockSpec((None, tm, nb), lambda i, j, k: (k // half, i, k % half)),
               pl.BlockSpec((None, None, K, nb), lambda i, j, k: (k, layer, 0, 0)),
               pl.BlockSpec((tm, K), lambda i, j, k: (i, 0)),
               jax.ShapeDtypeStruct((L, K), out_dtype), (L // tm, 1, N_DEV), "nt", rider=rider)


def _colblock_wgrad(name, xa, dh, rider=None):
    L, K = xa.shape
    half = N_DEV // 2
    nb = dh.shape[2] // half
    tm = _row_tile(K, 512)
    tk = L
    return _mm(name, xa, dh,
               pl.BlockSpec((tk, tm), lambda j, i, k: (k, i)),
               pl.BlockSpec((None, tk, nb), lambda j, i, k: (j // half, k, j % half)),
               pl.BlockSpec((None, tm, nb), lambda j, i, k: (j, i, 0)),
               jax.ShapeDtypeStruct((N_DEV, K, nb), _ACT), (N_DEV, K // tm, L // tk), "tn", rider=rider)


def _rowblock_fwd(name, xa, wg, layer, rider=None):
    L, FF = xa.shape
    r, D = wg.shape[2], wg.shape[3]
    tm = _row_tile(L, 512)
    return _mm(name, xa, wg,
               pl.BlockSpec((tm, 2 * r), lambda i, j, k: (i, k)),
               pl.BlockSpec((2, None, r, D), lambda i, j, k: (k, layer, 0, 0)),
               pl.BlockSpec((tm, D), lambda i, j, k: (i, 0)),
               jax.ShapeDtypeStruct((L, D), F32), (L // tm, 1, N_DEV // 2), "nn", rider=rider)


def _rowblock_dgrad(name, dy, wg, layer, rider=None):
    L, D = dy.shape
    r = wg.shape[2]
    tm = _row_tile(L, 512)
    return _mm(name, dy, wg,
               pl.BlockSpec((tm, D), lambda i, j, k: (i, 0)),
               pl.BlockSpec((2, None, r, D), lambda i, j, k: (j, layer, 0, 0)),
               pl.BlockSpec((tm, 2 * r), lambda i, j, k: (i, j)),
               jax.ShapeDtypeStruct((L, N_DEV * r), _ACT), (L // tm, N_DEV // 2, 1), "nt", rider=rider)


def _rowblock_wgrad(name, xa, dy, rider=None):
    L, FF = xa.shape
    D = dy.shape[1]
    tm = FF // (N_DEV // 2)
    tn = _row_tile(D, 1024)
    tk = _row_tile(L, 2048)
    return _mm(name, xa, dy,
               pl.BlockSpec((tk, tm), lambda i, j, k: (k, i)),
               pl.BlockSpec((tk, tn), lambda i, j, k: (k, j)),
               pl.BlockSpec((tm, tn), lambda i, j, k: (i, j)),
               jax.ShapeDtypeStruct((FF, D), _ACT), (FF // tm, D // tn, L // tk), "tn", rider=rider)


def _group_mm(name, xa, w, dims, out_dtype):
    L, D = xa.shape
    ng, pc, _ = w.shape
    tm = _row_tile(L, 512)
    return _mm(name, xa, w,
               pl.BlockSpec((tm, pc), lambda i, g, k: (i, g)),
               pl.BlockSpec((None, pc, pc), lambda i, g, k: (g, 0, 0)),
               pl.BlockSpec((tm, pc), lambda i, g, k: (i, g)),
               jax.ShapeDtypeStruct((L, D), out_dtype), (L // tm, ng, 1), dims)


def _group_wgrad(name, p, dy, ng):
    L, D = p.shape
    pc = D // ng
    tk = _row_tile(L, 512)
    return _mm(name, p, dy,
               pl.BlockSpec((tk, pc), lambda g, j, k: (k, g)),
               pl.BlockSpec((tk, pc), lambda g, j, k: (k, g)),
               pl.BlockSpec((None, pc, pc), lambda g, j, k: (g, 0, 0)),
               jax.ShapeDtypeStruct((ng, pc, pc), _ACT), (ng, 1, L // tk), "tn")


V_GPOST, V_GATE, V_YSCALE, V_GPRE, V_SHIFT, V_SCALE = range(6)
R_SHIFT, R_SCALE, R_GPRE, R_GATE, R_GPOST, R_YSCALE = range(6)
ROW_TILE = 256


def _rstd(v):
    return lax.rsqrt(jnp.mean(v * v, axis=-1, keepdims=True) + RMS_EPS)


def _post_norm_bwd(dyh, yh, ry):
    return ry * (dyh - yh * jnp.mean(dyh * yh, axis=-1, keepdims=True))


def _rows_fwd(name, xres, y, vec, *, add=False, target=None, want_x=True, u_dtype=None):
    L, D = xres.shape
    tm = _row_tile(L, ROW_TILE)
    has_y = y is not None
    last = target is not None
    has_u = u_dtype is not None

    def body(*refs):
        refs = list(refs)
        xres_ref = refs.pop(0)
        y_ref = refs.pop(0) if has_y else None
        vec_ref = refs.pop(0)
        tgt_ref = refs.pop(0) if last else None
        xnew = xres_ref[...]
        if has_y and add:
            xnew = xnew + y_ref[...]
        elif has_y:
            ye = y_ref[...] * vec_ref[V_YSCALE:V_YSCALE + 1, :]
            ry = _rstd(ye)
            yh = ye * ry
            gpost, gate = vec_ref[V_GPOST:V_GPOST + 1, :], vec_ref[V_GATE:V_GATE + 1, :]
            xnew = xnew + gate * (yh * gpost)
        if last:
            dx_ref, loss_ref, dy_ref, red_ref = refs
            diff = xnew - tgt_ref[...]
            dxn = diff * (1.0 / D)
            dx_ref[...] = dxn

            @pl.when(pl.program_id(0) == 0)
            def _():
                loss_ref[...] = jnp.zeros_like(loss_ref)
                red_ref[...] = jnp.zeros_like(red_ref)

            loss_ref[...] += jnp.sum(diff * diff) * (0.5 / D)
            drn2 = dxn * gate
            red_ref[R_GATE:R_GATE + 1, :] += jnp.sum(dxn * (yh * gpost), axis=0, keepdims=True)
            red_ref[R_GPOST:R_GPOST + 1, :] += jnp.sum(drn2 * yh, axis=0, keepdims=True)
            dye = _post_norm_bwd(drn2 * gpost, yh, ry)
            red_ref[R_YSCALE:R_YSCALE + 1, :] += jnp.sum(dye * y_ref[...], axis=0, keepdims=True)
            dy_ref[...] = (dye * vec_ref[V_YSCALE:V_YSCALE + 1, :]).astype(dy_ref.dtype)
            return
        if want_x:
            refs.pop(0)[...] = xnew
        if has_u:
            u_ref, = refs
            n = xnew * _rstd(xnew) * vec_ref[V_GPRE:V_GPRE + 1, :]
            u_ref[...] = (n * (1.0 + vec_ref[V_SCALE:V_SCALE + 1, :]) + vec_ref[V_SHIFT:V_SHIFT + 1, :]).astype(u_ref.dtype)

    row = pl.BlockSpec((tm, D), lambda i: (i, 0))
    vspec = pl.BlockSpec((8, D), lambda i: (0, 0))
    ins, in_specs = [xres], [row]
    if has_y:
        ins.append(y)
        in_specs.append(row)
    ins.append(vec)
    in_specs.append(vspec)
    out_shape, out_specs = [], []
    if last:
        ins.append(target)
        in_specs.append(row)
        out_shape = [jax.ShapeDtypeStruct((L, D), F32), jax.ShapeDtypeStruct((8, LANE), F32),
                     jax.ShapeDtypeStruct((L, D), _ACT), jax.ShapeDtypeStruct((8, D), F32)]
        out_specs = [row, pl.BlockSpec((8, LANE), lambda i: (0, 0)), row, vspec]
    else:
        if want_x:
            out_shape.append(jax.ShapeDtypeStruct((L, D), F32))
            out_specs.append(row)
        if has_u:
            out_shape.append(jax.ShapeDtypeStruct((L, D), u_dtype))
            out_specs.append(row)
    return pl.pallas_call(body, name=name, out_shape=out_shape, grid=(L // tm,), in_specs=in_specs,
                          out_specs=out_specs, compiler_params=_params(("arbitrary",)))(*ins)


def _rows_bwd(name, dxd, du, xnew, y, vec, dy_dtype=F32, want_dx=True, yscale_grad=False):
    L, D = xnew.shape if xnew is not None else dxd.shape
    tm = _row_tile(L, ROW_TILE)
    has_dxd, has_pre, has_post = dxd is not None, du is not None, y is not None

    def body(*refs):
        refs = list(refs)
        dxd_ref = refs.pop(0) if has_dxd else None
        du_ref = refs.pop(0) if has_pre else None
        xnew_ref = refs.pop(0) if has_pre else None
        y_ref = refs.pop(0) if has_post else None
        vec_ref = refs.pop(0)
        dx_ref = refs.pop(0) if want_dx else None
        dy_ref = refs.pop(0) if has_post else None
        red_ref, = refs

        @pl.when(pl.program_id(0) == 0)
        def _():
            red_ref[...] = jnp.zeros_like(red_ref)

        def acc(rw, val):
            red_ref[rw:rw + 1, :] += jnp.sum(val, axis=0, keepdims=True)

        dxn = dxd_ref[...] if has_dxd else None
        if has_pre:
            xn = xnew_ref[...]
            r = _rstd(xn)
            nh = xn * r
            gpre = vec_ref[V_GPRE:V_GPRE + 1, :]
            dub = du_ref[...].astype(F32)
            acc(R_SHIFT, dub)
            acc(R_SCALE, dub * (nh * gpre))
            drn = dub * (1.0 + vec_ref[V_SCALE:V_SCALE + 1, :])
            acc(R_GPRE, drn * nh)
            dnh = drn * gpre
            t = r * (dnh - nh * jnp.mean(dnh * nh, axis=-1, keepdims=True))
            dxn = t if dxn is None else dxn + t
        if want_dx:
            dx_ref[...] = dxn
        if has_post:
            ye = y_ref[...] * vec_ref[V_YSCALE:V_YSCALE + 1, :]
            ry = _rstd(ye)
            yh = ye * ry
            gpost = vec_ref[V_GPOST:V_GPOST + 1, :]
            acc(R_GATE, dxn * (yh * gpost))
            drn2 = dxn * vec_ref[V_GATE:V_GATE + 1, :]
            acc(R_GPOST, drn2 * yh)
            dye = _post_norm_bwd(drn2 * gpost, yh, ry)
            if yscale_grad:
                acc(R_YSCALE, dye * y_ref[...])
            dy_ref[...] = (dye * vec_ref[V_YSCALE:V_YSCALE + 1, :]).astype(dy_ref.dtype)

    row = pl.BlockSpec((tm, D), lambda i: (i, 0))
    vspec = pl.BlockSpec((8, D), lambda i: (0, 0))
    ins, in_specs = [], []
    for a in ([dxd] if has_dxd else []) + ([du, xnew] if has_pre else []) + ([y] if has_post else []):
        ins.append(a)
        in_specs.append(row)
    ins.append(vec)
    in_specs.append(vspec)
    out_shape, out_specs = [], []
    if want_dx:
        out_shape.append(jax.ShapeDtypeStruct((L, D), F32))
        out_specs.append(row)
    if has_post:
        out_shape.append(jax.ShapeDtypeStruct((L, D), dy_dtype))
        out_specs.append(row)
    out_shape.append(jax.ShapeDtypeStruct((8, D), F32))
    out_specs.append(vspec)
    return pl.pallas_call(body, name=name, out_shape=out_shape, grid=(L // tm,), in_specs=in_specs,
                          out_specs=out_specs, compiler_params=_params(("arbitrary",)))(*ins)


def _sigmoid(v):
    return 1.0 / (1.0 + jnp.exp(-v))


def _glu_fwd(name, vg):
    _, L, D = vg.shape
    tm = _row_tile(L, ROW_TILE)

    def body(vg_ref, o_ref):
        o_ref[...] = vg_ref[0].astype(F32) * _sigmoid(vg_ref[1].astype(F32))

    return pl.pallas_call(body, name=name, grid=(L // tm,),
                          in_specs=[pl.BlockSpec((2, tm, D), lambda i: (0, i, 0))],
                          out_specs=pl.BlockSpec((tm, D), lambda i: (i, 0)),
                          out_shape=jax.ShapeDtypeStruct((L, D), F32),
                          compiler_params=_params(("parallel",)))(vg)


def _glu_bwd(name, vg, dout):
    _, L, D = vg.shape
    tm = _row_tile(L, ROW_TILE)

    def body(vg_ref, d_ref, o_ref):
        val, s = vg_ref[0].astype(F32), _sigmoid(vg_ref[1].astype(F32))
        d = d_ref[...]
        o_ref[0] = (d * s).astype(o_ref.dtype)
        o_ref[1] = (d * val * s * (1.0 - s)).astype(o_ref.dtype)

    return pl.pallas_call(body, name=name, grid=(L // tm,),
                          in_specs=[pl.BlockSpec((2, tm, D), lambda i: (0, i, 0)), pl.BlockSpec((tm, D), lambda i: (i, 0))],
                          out_specs=pl.BlockSpec((2, tm, D), lambda i: (0, i, 0)),
                          out_shape=jax.ShapeDtypeStruct((2, L, D), _ACT),
                          compiler_params=_params(("parallel",)))(vg, dout)


CONV_ROWS = 256


def _row_pick(blk, idx):
    rows = lax.broadcasted_iota(jnp.int32, blk.shape, 0)
    return jnp.sum(jnp.where(rows == idx, blk, 0.0), axis=0, keepdims=True)


def _shifted(ref, r0, rc, L):
    cur = ref[pl.ds(r0, rc), :].astype(F32)
    before = ref[pl.ds(pl.multiple_of(jnp.maximum(r0 - 16, 0), 16), 16), :].astype(F32)
    after = ref[pl.ds(pl.multiple_of(jnp.minimum(r0 + rc, L - 16), 16), 16), :].astype(F32)
    prev_row = jnp.where(r0 > 0, _row_pick(before, 15), 0.0)
    next_row = jnp.where(r0 + rc < L, _row_pick(after, 0), 0.0)
    rows = lax.broadcasted_iota(jnp.int32, cur.shape, 0)
    up = jnp.where(rows == 0, prev_row, pltpu.roll(cur, 1, 0))
    down = jnp.where(rows == rc - 1, next_row, pltpu.roll(cur, rc - 1, 0))
    return up, cur, down


def _silu_parts(g):
    s = _sigmoid(g)
    return g * s, s


def _conv_swiglu_fwd(name, h, cw, rider=None):
    _, L, FF = h.shape
    rc = _row_tile(L, CONV_ROWS)

    def body(h_ref, cw_ref, o_ref):
        def chunk(ci, _):
            r0 = pl.multiple_of(ci * rc, rc)
            hc = []
            for half in range(2):
                up, cur, down = _shifted(h_ref.at[half], r0, rc, L)
                hc.append(up * cw_ref[half, 0:1, :] + cur * cw_ref[half, 1:2, :] + down * cw_ref[half, 2:3, :]
                          + cw_ref[half, 3:4, :])
            o_ref[pl.ds(r0, rc), :] = (_silu_parts(hc[1])[0] * hc[0]).astype(o_ref.dtype)
            return 0

        lax.fori_loop(0, L // rc, chunk, 0)

    return _pcall(body, name=name, grid=(FF // LANE,),
                  in_specs=[pl.BlockSpec((2, L, LANE), lambda j: (0, 0, j)),
                            pl.BlockSpec((2, 8, LANE), lambda j: (0, 0, j))],
                  out_specs=pl.BlockSpec((L, LANE), lambda j: (0, j)),
                  out_shape=jax.ShapeDtypeStruct((L, FF), _ACT), ins=(h, cw), sem=("parallel",), rider=rider)


def _conv_swiglu_bwd(name, h, cw, dact, rider=None):
    _, L, FF = h.shape
    rc = _row_tile(L, CONV_ROWS)

    def body(h_ref, cw_ref, da_ref, dh_ref, dcw_ref, dhc_ref):
        def chunk(ci, acc):
            r0 = pl.multiple_of(ci * rc, rc)
            taps, hc = [], []
            for half in range(2):
                t = _shifted(h_ref.at[half], r0, rc, L)
                taps.append(t)
                hc.append(t[0] * cw_ref[half, 0:1, :] + t[1] * cw_ref[half, 1:2, :] + t[2] * cw_ref[half, 2:3, :]
                          + cw_ref[half, 3:4, :])
            d = da_ref[pl.ds(r0, rc), :].astype(F32)
            act, s = _silu_parts(hc[1])
            dhc = (d * act, d * hc[0] * (s + act * (1.0 - s)))
            new = []
            for half in range(2):
                dhc_ref[half, pl.ds(r0, rc), :] = dhc[half]
                for k in range(3):
                    new.append(acc[4 * half + k] + jnp.sum(dhc[half] * taps[half][k], axis=0, keepdims=True))
                new.append(acc[4 * half + 3] + jnp.sum(dhc[half], axis=0, keepdims=True))
            return tuple(new)

        zero = jnp.zeros((1, LANE), F32)
        acc = lax.fori_loop(0, L // rc, chunk, (zero,) * 8)
        dcw_ref[...] = jnp.zeros_like(dcw_ref)
        for half in range(2):
            for k in range(4):
                dcw_ref[half, k:k + 1, :] = acc[4 * half + k]

        def chunk2(ci, _):
            r0 = pl.multiple_of(ci * rc, rc)
            for half in range(2):
                up, cur, down = _shifted(dhc_ref.at[half], r0, rc, L)
                dh_ref[half, pl.ds(r0, rc), :] = (down * cw_ref[half, 0:1, :] + cur * cw_ref[half, 1:2, :]
                                                  + up * cw_ref[half, 2:3, :]).astype(dh_ref.dtype)
            return 0

        lax.fori_loop(0, L // rc, chunk2, 0)

    return _pcall(body, name=name, grid=(FF // LANE,),
                  in_specs=[pl.BlockSpec((2, L, LANE), lambda j: (0, 0, j)),
                            pl.BlockSpec((2, 8, LANE), lambda j: (0, 0, j)),
                            pl.BlockSpec((L, LANE), lambda j: (0, j))],
                  out_specs=[pl.BlockSpec((2, L, LANE), lambda j: (0, 0, j)),
                             pl.BlockSpec((2, 8, LANE), lambda j: (0, 0, j))],
                  out_shape=[jax.ShapeDtypeStruct((2, L, FF), _ACT), jax.ShapeDtypeStruct((2, 8, FF), F32)],
                  scratch_shapes=[pltpu.VMEM((2, L, LANE), F32)], ins=(h, cw, dact), sem=("parallel",), rider=rider)


POOL_ROWS = 256
POOL_TILE = 256


def _pool_bands(transpose):
    i = jnp.arange(POOL_ROWS)[:, None]
    j = jnp.arange(POOL_ROWS + 2 * POOL_HALO)[None, :] - POOL_HALO
    bands = []
    for w in POOL_WINDOWS:
        lo, hi = (-(w // 2 - 1), w // 2) if transpose else (-(w // 2), w // 2 - 1)
        bands.append(((j - i >= lo) & (j - i <= hi)).astype(_ACT))
    return jnp.stack(bands)


def _pool_window(name, u, transpose, out_dtype):
    L, D = u.shape
    ng = len(POOL_WINDOWS)
    pc = D // ng
    tn = min(POOL_TILE, pc)
    rc = _row_tile(L, POOL_ROWS)
    bands = _pool_bands(transpose)
    if rc != POOL_ROWS:
        bands = bands[:, :rc, :rc + 2 * POOL_HALO]
    halo = POOL_HALO

    def body(u_ref, band_ref, o_ref, hi_ref, lo_ref):
        g = (pl.program_id(0) * tn) // pc
        half = jnp.zeros((1, 1), jnp.int32)
        for k, w in enumerate(POOL_WINDOWS):
            half = jnp.where(g == k, w // 2, half)
        zeros = jnp.zeros((halo, tn), _ACT)
        for ref in (hi_ref, lo_ref):
            ref[0:halo, :] = zeros
            ref[halo + L:2 * halo + L, :] = zeros

        def inv_count(r0):
            t = r0 + lax.broadcasted_iota(jnp.int32, (rc, tn), 0)
            lo = jnp.clip(t - half, 0, L - 1)
            hi = jnp.clip(t + half - 1, 0, L - 1)
            return 1.0 / (hi - lo + 1).astype(F32)

        def split(ci, _):
            r0 = pl.multiple_of(ci * rc, rc)
            v = u_ref[pl.ds(r0, rc), :].astype(F32)
            if transpose:
                v = v * inv_count(r0)
            hi = v.astype(_ACT)
            dst = pl.ds(pl.multiple_of(r0 + halo, halo), rc)
            hi_ref[dst, :] = hi
            lo_ref[dst, :] = (v - hi.astype(F32)).astype(_ACT)
            return 0

        lax.fori_loop(0, L // rc, split, 0)
        band = band_ref[...]

        def chunk(ci, _):
            r0 = pl.multiple_of(ci * rc, rc)
            win = pl.ds(r0, rc + 2 * halo)
            s = (jnp.dot(band, hi_ref[win, :], preferred_element_type=F32)
                 + jnp.dot(band, lo_ref[win, :], preferred_element_type=F32))
            if not transpose:
                s = s * inv_count(r0)
            o_ref[pl.ds(r0, rc), :] = (s - u_ref[pl.ds(r0, rc), :].astype(F32)).astype(o_ref.dtype)
            return 0

        lax.fori_loop(0, L // rc, chunk, 0)

    return pl.pallas_call(body, name=name, grid=(D // tn,),
                          in_specs=[pl.BlockSpec((L, tn), lambda j: (0, j)),
                                    pl.BlockSpec((None, rc, rc + 2 * halo), lambda j: ((j * tn) // pc, 0, 0))],
                          out_specs=pl.BlockSpec((L, tn), lambda j: (0, j)),
                          out_shape=jax.ShapeDtypeStruct((L, D), out_dtype),
                          scratch_shapes=[pltpu.VMEM((L + 2 * halo, tn), _ACT), pltpu.VMEM((L + 2 * halo, tn), _ACT)],
                          compiler_params=_params(("parallel",)))(u, bands)


S5_ROWS = 512


def _slab(start):
    return pl.ds(start if isinstance(start, int) else pl.multiple_of(start, NSEG), NSEG)


def _cmul(ar, ai, br, bi):
    return ar * br - ai * bi, ar * bi + ai * br


def _cpow(ar, ai, n):
    rr, ri = None, None
    br, bi = ar, ai
    while n:
        if n & 1:
            rr, ri = (br, bi) if rr is None else _cmul(rr, ri, br, bi)
        n >>= 1
        if n:
            br, bi = _cmul(br, bi, br, bi)
    return rr, ri


def _pow_table(pw_ref, ar, ai, n):
    W = ar.shape[1]
    pr, pi = ar, ai
    for r in range(NSEG):
        pw_ref[0, r:r + 1, :] = pr
        pw_ref[1, r:r + 1, :] = pi
        if r < NSEG - 1:
            pr, pi = _cmul(pr, pi, ar, ai)
    a8r, a8i = (jnp.broadcast_to(v, (NSEG, W)) for v in _cpow(ar, ai, NSEG))

    def step(k, carry):
        nr, ni = _cmul(carry[0], carry[1], a8r, a8i)
        pw_ref[0, _slab(k * NSEG), :] = nr
        pw_ref[1, _slab(k * NSEG), :] = ni
        return nr, ni

    lax.fori_loop(1, n // NSEG, step, (pw_ref[0, 0:NSEG, :], pw_ref[1, 0:NSEG, :]))


def _seg_scan(sr_ref, si_ref, tmp_ref, pw_ref, row0, n, ar, ai, h0, rev, conj=False, pair_with=None):
    W = ar.shape[1]
    arb, aib = jnp.broadcast_to(ar, (NSEG, W)), jnp.broadcast_to(ai, (NSEG, W))

    def rows(s):
        t = (n - 1 - s) if rev else s
        return _slab(row0 + t * NSEG)

    def step(s, carry):
        hr, hi = carry
        sl = rows(s)
        nr = arb * hr - aib * hi + sr_ref[sl, :]
        ni = arb * hi + aib * hr + si_ref[sl, :]
        sr_ref[sl, :] = nr
        si_ref[sl, :] = ni
        return nr, ni

    zero = jnp.zeros((NSEG, W), F32)
    fr, fi = lax.fori_loop(0, n, step, (zero, zero), unroll=2)
    tmp_ref[0] = fr
    tmp_ref[1] = fi
    anr, ani = _cpow(ar, ai, n)
    cr, ci = h0
    for j in (range(NSEG - 1, -1, -1) if rev else range(NSEG)):
        tmp_ref[2, j:j + 1, :] = cr
        tmp_ref[3, j:j + 1, :] = ci
        pr, pi = _cmul(anr, ani, cr, ci)
        cr, ci = tmp_ref[0, j:j + 1, :] + pr, tmp_ref[1, j:j + 1, :] + pi
    cmr, cmi = tmp_ref[2], tmp_ref[3]

    def fix(k, acc):
        for r in range(NSEG):
            row = pl.ds(pl.multiple_of(k * NSEG, NSEG) + r, 1)
            pr = jnp.broadcast_to(pw_ref[0, row, :], (NSEG, W))
            pi = jnp.broadcast_to(pw_ref[1, row, :], (NSEG, W))
            s = k * NSEG + r
            sl = rows(s)
            if conj:
                gr = sr_ref[sl, :] + (pr * cmr + pi * cmi)
                gi = si_ref[sl, :] + (pr * cmi - pi * cmr)
            else:
                gr = sr_ref[sl, :] + (pr * cmr - pi * cmi)
                gi = si_ref[sl, :] + (pr * cmi + pi * cmr)
            sr_ref[sl, :] = gr
            si_ref[sl, :] = gi
            if pair_with is not None:
                prev = rows(jnp.minimum(s + 1, n - 1))
                hpr, hpi = pair_with[0][prev, :], pair_with[1][prev, :]
                acc = (acc[0] + hpr * gr + hpi * gi, acc[1] + hpr * gi - hpi * gr)
        return acc

    if pair_with is None:
        lax.fori_loop(0, n // NSEG, fix, 0)
        return (cr, ci), (cmr, cmi), None
    total = lax.fori_loop(0, n // NSEG, fix, (zero, zero))
    h_r, h_i, (hcr, hci) = pair_with
    last = rows(n - 1)
    fr, fi = hcr - h_r[last, :], hci - h_i[last, :]
    gr, gi = sr_ref[last, :], si_ref[last, :]
    total = (total[0] + fr * gr + fi * gi, total[1] + fr * gi - fi * gr)
    return (cr, ci), (cmr, cmi), total


def _gelu_tanh(y):
    k = math.sqrt(2.0 / math.pi)
    t = jnp.tanh(k * (y + 0.044715 * y * y * y))
    return 0.5 * y * (1.0 + t), t


def _s5_chunks(L):
    rc = _row_tile(L, S5_ROWS)
    return [(r, rc) for r in range(0, L, rc)]


_NT_DIMS = (((1,), (1,)), ((), ()))
_TN_DIMS = (((0,), (0,)), ((), ()))
_LOG_P, _LOG_CH = S5_P.bit_length() - 1, S5_CH.bit_length() - 1


def _same_group(shape, row_shift, col_shift):
    rows = lax.broadcasted_iota(jnp.int32, shape, 0)
    cols = lax.broadcasted_iota(jnp.int32, shape, 1)
    return lax.shift_right_logical(rows, row_shift) == lax.shift_right_logical(cols, col_shift)


def _s5_bt(bt):
    full = jnp.concatenate([bt] * S5_TILE_G, axis=0)
    return jnp.where(_same_group(full.shape, _LOG_P, _LOG_CH), full, 0.0).astype(_ACT)


def _s5_ct(ct):
    full = jnp.concatenate([ct] * S5_TILE_G, axis=0)
    return jnp.where(_same_group(full.shape, _LOG_CH, _LOG_P), full, 0.0).astype(_ACT)


def _s5_diag(m):
    m = jnp.where(_same_group(m.shape, _LOG_CH, _LOG_P), m, 0.0)
    rows = lax.broadcasted_iota(jnp.int32, (S5_TILE_W, S5_P), 0)
    cols = lax.broadcasted_iota(jnp.int32, (S5_TILE_W, S5_P), 1)
    pick = (jnp.bitwise_and(rows, S5_P - 1) == cols).astype(_ACT)
    hi = m.astype(_ACT)
    lo = (m - hi.astype(F32)).astype(_ACT)
    return jnp.dot(hi, pick, preferred_element_type=F32) + jnp.dot(lo, pick, preferred_element_type=F32)


def _s5_project(u_ref, uc_ref, bre, bim, sr_ref, si_ref, L, LC):
    for ref, base, n in ((u_ref, 0, L), (uc_ref, L, LC)):
        for r, rc in _s5_chunks(n):
            ub = ref[r:r + rc, :].astype(_ACT)
            sr_ref[base + r:base + r + rc, :] = lax.dot_general(ub, bre, _NT_DIMS, preferred_element_type=F32)
            si_ref[base + r:base + r + rc, :] = lax.dot_general(ub, bim, _NT_DIMS, preferred_element_type=F32)


def _s5_states(sr_ref, si_ref, tmp_ref, pw_ref, ar, ai, L, LC, rev):
    W = ar.shape[1]
    zero = (jnp.zeros((1, W), F32), jnp.zeros((1, W), F32))
    hctx, cm_ctx, _ = _seg_scan(sr_ref, si_ref, tmp_ref, pw_ref, L, LC // NSEG, ar, ai, zero, rev)
    _, cm_lat, _ = _seg_scan(sr_ref, si_ref, tmp_ref, pw_ref, 0, L // NSEG, ar, ai, hctx, rev)
    return cm_lat, cm_ctx


def _s5_fwd(u, uc, bblk, cblk, apar, dsk, rider=None):
    L, D = u.shape
    LC = uc.shape[0]
    NT, W, TC = D // S5_TILE_CH, S5_TILE_W, S5_TILE_CH

    def body(u_ref, uc_ref, b_ref, c_ref, a_ref, d_ref, y_ref, z_ref, sr_ref, si_ref, tmp_ref, pw_ref):
        for r, rc in _s5_chunks(L):
            y_ref[r:r + rc, :] = u_ref[r:r + rc, :].astype(F32) * d_ref[0:1, :]
        for d in range(2):
            ar, ai = a_ref[2 * d:2 * d + 1, :], a_ref[2 * d + 1:2 * d + 2, :]
            _pow_table(pw_ref, ar, ai, L // NSEG)
            _s5_project(u_ref, uc_ref, _s5_bt(b_ref[2 * d]), _s5_bt(b_ref[2 * d + 1]), sr_ref, si_ref, L, LC)
            _s5_states(sr_ref, si_ref, tmp_ref, pw_ref, ar, ai, L, LC, rev=(d == 1))
            cre, cim = _s5_ct(c_ref[2 * d]), _s5_ct(c_ref[2 * d + 1])
            for r, rc in _s5_chunks(L):
                y_ref[r:r + rc, :] += (
                    lax.dot_general(sr_ref[r:r + rc, :].astype(_ACT), cre, _NT_DIMS, preferred_element_type=F32)
                    - lax.dot_general(si_ref[r:r + rc, :].astype(_ACT), cim, _NT_DIMS, preferred_element_type=F32))
        for r, rc in _s5_chunks(L):
            z_ref[r:r + rc, :] = _gelu_tanh(y_ref[r:r + rc, :])[0].astype(z_ref.dtype)

    col = lambda n: pl.BlockSpec((n, TC), lambda j: (0, j))
    return _pcall(
        body, name="s5_fwd", grid=(NT,), ins=(u, uc, bblk, cblk, apar, dsk), sem=("parallel",), rider=rider,
        in_specs=[col(L), col(LC),
                  pl.BlockSpec((None, 4, S5_P, TC), lambda j: (j, 0, 0, 0)),
                  pl.BlockSpec((None, 4, S5_CH, W), lambda j: (j, 0, 0, 0)),
                  pl.BlockSpec((None, 8, W), lambda j: (j, 0, 0)),
                  pl.BlockSpec((8, TC), lambda j: (0, j))],
        out_specs=[col(L), col(L)],
        out_shape=[jax.ShapeDtypeStruct((L, D), F32), jax.ShapeDtypeStruct((L, D), _ACT)],
        scratch_shapes=[pltpu.VMEM((L + LC, W), F32), pltpu.VMEM((L + LC, W), F32), pltpu.VMEM((4, NSEG, W), F32),
                        pltpu.VMEM((2, L // NSEG, W), F32)])


def _s5_bwd(u, uc, dz, y, bblk, cblk, apar, dsk, rider=None):
    L, D = u.shape
    LC = uc.shape[0]
    NT, W, TC = D // S5_TILE_CH, S5_TILE_W, S5_TILE_CH
    nl, nc = L // NSEG, LC // NSEG

    def body(u_ref, uc_ref, dz_ref, y_ref, b_ref, c_ref, a_ref, d_ref,
             du_ref, duc_ref, db_ref, dc_ref, da_ref, dd_ref,
             hr_ref, hi_ref, gr_ref, gi_ref, dy_ref, tmp_ref, pw_ref):
        ddacc = jnp.zeros((1, TC), F32)
        for r, rc in _s5_chunks(L):
            yv = y_ref[r:r + rc, :]
            g, t = _gelu_tanh(yv)
            k = math.sqrt(2.0 / math.pi)
            dg = 0.5 * (1.0 + t) + 0.5 * yv * (1.0 - t * t) * k * (1.0 + 3 * 0.044715 * yv * yv)
            dy = dz_ref[r:r + rc, :].astype(F32) * dg
            uv = u_ref[r:r + rc, :].astype(F32)
            ddacc = ddacc + jnp.sum(dy * uv, axis=0, keepdims=True)
            du_ref[r:r + rc, :] = dy * d_ref[0:1, :]
            dy_ref[r:r + rc, :] = dy.astype(dy_ref.dtype)
        dd_ref[...] = jnp.zeros_like(dd_ref)
        dd_ref[0:1, :] = ddacc
        duc_ref[...] = jnp.zeros_like(duc_ref)
        da_ref[...] = jnp.zeros_like(da_ref)
        tn = _TN_DIMS
        for d in range(2):
            rev = d == 1
            ar, ai = a_ref[2 * d:2 * d + 1, :], a_ref[2 * d + 1:2 * d + 2, :]
            bre, bim = _s5_bt(b_ref[2 * d]), _s5_bt(b_ref[2 * d + 1])
            cre, cim = _s5_ct(c_ref[2 * d]), _s5_ct(c_ref[2 * d + 1])
            _pow_table(pw_ref, ar, ai, nl)
            _s5_project(u_ref, uc_ref, bre, bim, hr_ref, hi_ref, L, LC)
            cm_lat, cm_ctx = _s5_states(hr_ref, hi_ref, tmp_ref, pw_ref, ar, ai, L, LC, rev)
            cml_r, cml_i, cmc_r, cmc_i = cm_lat[0], cm_lat[1], cm_ctx[0], cm_ctx[1]
            dcr = jnp.zeros((TC, W), F32)
            dci = jnp.zeros((TC, W), F32)
            for r, rc in _s5_chunks(L):
                dyb = dy_ref[r:r + rc, :]
                gr_ref[r:r + rc, :] = jnp.dot(dyb, cre, preferred_element_type=F32)
                gi_ref[r:r + rc, :] = -jnp.dot(dyb, cim, preferred_element_type=F32)
                dcr = dcr + lax.dot_general(dyb, hr_ref[r:r + rc, :].astype(_ACT), tn, preferred_element_type=F32)
                dci = dci - lax.dot_general(dyb, hi_ref[r:r + rc, :].astype(_ACT), tn, preferred_element_type=F32)
            dc_ref[2 * d] = _s5_diag(dcr)
            dc_ref[2 * d + 1] = _s5_diag(dci)
            gr_ref[L:L + LC, :] = jnp.zeros((LC, W), F32)
            gi_ref[L:L + LC, :] = jnp.zeros((LC, W), F32)
            zero = (jnp.zeros((1, W), F32), jnp.zeros((1, W), F32))
            glat, _, (lr, li) = _seg_scan(gr_ref, gi_ref, tmp_ref, pw_ref, 0, nl, ar, -ai, zero, not rev, conj=True,
                                          pair_with=(hr_ref, hi_ref, (cml_r, cml_i)))
            _, _, (qr, qi) = _seg_scan(gr_ref, gi_ref, tmp_ref, pw_ref, L, nc, ar, -ai, glat, not rev, conj=True,
                                       pair_with=(hr_ref, hi_ref, (cmc_r, cmc_i)))
            da_ref[2 * d:2 * d + 1, :] = jnp.sum(lr + qr, axis=0, keepdims=True)
            da_ref[2 * d + 1:2 * d + 2, :] = jnp.sum(li + qi, axis=0, keepdims=True)
            dbr = jnp.zeros((TC, W), F32)
            dbi = jnp.zeros((TC, W), F32)
            for ref, oref, base, n in ((u_ref, du_ref, 0, L), (uc_ref, duc_ref, L, LC)):
                for r, rc in _s5_chunks(n):
                    ub = ref[r:r + rc, :].astype(_ACT)
                    gr = gr_ref[base + r:base + r + rc, :].astype(_ACT)
                    gi = gi_ref[base + r:base + r + rc, :].astype(_ACT)
                    dbr = dbr + lax.dot_general(ub, gr, tn, preferred_element_type=F32)
                    dbi = dbi + lax.dot_general(ub, gi, tn, preferred_element_type=F32)
                    oref[r:r + rc, :] += (jnp.dot(gr, bre, preferred_element_type=F32)
                                          + jnp.dot(gi, bim, preferred_element_type=F32))
            db_ref[2 * d] = _s5_diag(dbr)
            db_ref[2 * d + 1] = _s5_diag(dbi)

    col = lambda n: pl.BlockSpec((n, TC), lambda j: (0, j))
    bspec = pl.BlockSpec((None, 4, S5_P, TC), lambda j: (j, 0, 0, 0))
    cspec = pl.BlockSpec((None, 4, S5_CH, W), lambda j: (j, 0, 0, 0))
    gspec = pl.BlockSpec((None, 4, TC, S5_P), lambda j: (j, 0, 0, 0))
    aspec = pl.BlockSpec((None, 8, W), lambda j: (j, 0, 0))
    return _pcall(
        body, name="s5_bwd", grid=(NT,), ins=(u, uc, dz, y, bblk, cblk, apar, dsk), sem=("parallel",),
        vmem=VMEM_LIMIT_BIG, rider=rider,
        in_specs=[col(L), col(LC), col(L), col(L), bspec, cspec, aspec, pl.BlockSpec((8, TC), lambda j: (0, j))],
        out_specs=[col(L), col(LC), gspec, gspec, aspec, pl.BlockSpec((None, 8, TC), lambda j: (j, 0, 0))],
        out_shape=[jax.ShapeDtypeStruct((L, D), F32), jax.ShapeDtypeStruct((LC, D), F32),
                   jax.ShapeDtypeStruct((NT, 4, TC, S5_P), F32), jax.ShapeDtypeStruct((NT, 4, TC, S5_P), F32),
                   jax.ShapeDtypeStruct((NT, 8, W), F32), jax.ShapeDtypeStruct((NT, 8, TC), F32)],
        scratch_shapes=[pltpu.VMEM((L + LC, W), F32), pltpu.VMEM((L + LC, W), F32),
                        pltpu.VMEM((L + LC, W), F32), pltpu.VMEM((L + LC, W), F32),
                        pltpu.VMEM((L, TC), _ACT), pltpu.VMEM((4, NSEG, W), F32), pltpu.VMEM((2, nl, W), F32)])


ADA_ROWS = 16


def _silu_rows(c_ref):
    c = c_ref[...]
    return c * _sigmoid(c)


def _ada_fwd(cmat, ada_w, ada_b):
    nl, D, n = ada_w.shape
    tn = _row_tile(n, 512)

    def body(c_ref, w_ref, b_ref, o_ref):
        a = _silu_rows(c_ref).astype(_ACT)
        o_ref[...] = jnp.dot(a, w_ref[...].astype(_ACT), preferred_element_type=F32) + b_ref[...]

    return pl.pallas_call(body, name="ada_fwd", grid=(nl, n // tn),
                          in_specs=[pl.BlockSpec((ADA_ROWS, D), lambda l, j: (0, 0)),
                                    pl.BlockSpec((None, D, tn), lambda l, j: (l, 0, j)),
                                    pl.BlockSpec((None, 1, tn), lambda l, j: (l, 0, j))],
                          out_specs=pl.BlockSpec((None, ADA_ROWS, tn), lambda l, j: (l, 0, j)),
                          out_shape=jax.ShapeDtypeStruct((nl, ADA_ROWS, n), F32),
                          compiler_params=_params(("parallel", "parallel")))(cmat, ada_w, ada_b)


def _ada_bwd(cmat, ada_w, dm):
    nl, D, n = ada_w.shape
    tn = _row_tile(n, 512)
    nj = n // tn

    def body(c_ref, w_ref, dm_ref, dw_ref, dc_ref):
        c = c_ref[...]
        s = _sigmoid(c)
        a = (c * s).astype(_ACT)
        dmb = dm_ref[...].astype(_ACT)
        dw_ref[...] = lax.dot_general(a, dmb, (((0,), (0,)), ((), ())), preferred_element_type=F32)
        part = lax.dot_general(dmb, w_ref[...].astype(_ACT), (((1,), (1,)), ((), ())), preferred_element_type=F32)
        part = part * (s * (1.0 + c * (1.0 - s)))

        @pl.when(pl.program_id(1) == 0)
        def _():
            dc_ref[...] = part

        @pl.when(pl.program_id(1) > 0)
        def _():
            dc_ref[...] += part

    return pl.pallas_call(body, name="ada_bwd", grid=(nl, nj),
                          in_specs=[pl.BlockSpec((ADA_ROWS, D), lambda l, j: (0, 0)),
                                    pl.BlockSpec((None, D, tn), lambda l, j: (l, 0, j)),
                                    pl.BlockSpec((None, ADA_ROWS, tn), lambda l, j: (l, 0, j))],
                          out_specs=[pl.BlockSpec((None, D, tn), lambda l, j: (l, 0, j)),
                                     pl.BlockSpec((None, ADA_ROWS, D), lambda l, j: (l, 0, 0))],
                          out_shape=[jax.ShapeDtypeStruct((nl, D, n), F32), jax.ShapeDtypeStruct((nl, ADA_ROWS, D), F32)],
                          compiler_params=_params(("parallel", "arbitrary")))(cmat, ada_w, dm)


def _adamw(name, gparts, w, m, v):
    nl, R, C = w.shape
    gparts = [g if isinstance(g, tuple) else (g, 0) for g in gparts]
    n = gparts[0][0].shape[0]
    runs = []
    for l, (g, r0) in enumerate(gparts):
        if runs and runs[-1][0] is g and runs[-1][1] + runs[-1][3] * R == r0:
            runs[-1][3] += 1
        else:
            runs.append([g, r0, l, 1])
    run_of = [q for q, run in enumerate(runs) for _ in range(run[3])]
    tr = R
    part_bytes = len(runs) * n * C * gparts[0][0].dtype.itemsize * 2
    for cand in (4096, 2048, 1024, 512, 256, 128, 64, 32, 16, 8):
        if R % cand == 0 and cand * max(C, LANE) * 4 <= 2 * 1024 * 1024 and cand * part_bytes <= VMEM_LIMIT // 2:
            tr = cand
            break
    nt = R // tr
    bc1 = 1.0 - ADAM_B1 ** ADAM_STEP
    bc2 = 1.0 - ADAM_B2 ** ADAM_STEP

    def body(*refs):
        g_refs = refs[:len(runs)]
        w_ref, m_ref, v_ref, go_ref, d_ref, mo_ref, vo_ref = refs[len(runs):]
        for l in range(nl):
            @pl.when(pl.program_id(0) == l)
            def _(g_ref=g_refs[run_of[l]]):
                g = g_ref[0].astype(F32)
                for j in range(1, n):
                    g = g + g_ref[j].astype(F32)
                m2 = ADAM_B1 * m_ref[...] + (1.0 - ADAM_B1) * g
                v2 = ADAM_B2 * v_ref[...] + (1.0 - ADAM_B2) * (g * g)
                go_ref[...] = g
                mo_ref[...] = m2
                vo_ref[...] = v2
                d_ref[...] = -ADAM_LR * ((m2 / bc1) / (jnp.sqrt(v2 / bc2) + ADAM_EPS) + ADAM_WD * w_ref[...])

    def gspec(run):
        _, r0, l0, count = run
        return pl.BlockSpec((n, tr, C),
                            lambda lyr, i: (0, r0 // tr + jnp.clip((lyr - l0) * nt + i, 0, count * nt - 1), 0))

    row = pl.BlockSpec((None, tr, C), lambda lyr, i: (lyr, i, 0))
    out = jax.ShapeDtypeStruct((nl, R, C), F32)
    return _pcall(body, name=name, grid=(nl, nt), in_specs=[gspec(run) for run in runs] + [row, row, row],
                  out_specs=[row, row, row, row], out_shape=[out, out, out, out],
                  ins=(*[run[0] for run in runs], w, m, v), sem=("arbitrary", "arbitrary"))


def _sum_parts(name, parts):
    n, R, C = parts.shape

    def body(p_ref, o_ref):
        s = p_ref[0]
        for j in range(1, n):
            s = s + p_ref[j]
        o_ref[...] = s

    return pl.pallas_call(body, name=name, out_shape=jax.ShapeDtypeStruct((R, C), F32),
                          compiler_params=_params(None))(parts)


def _discretize(lam_re, lam_im, log_step, b_re, b_im):
    dt = jnp.exp(log_step)[:, None]
    mag = jnp.exp(lam_re * dt)
    abar_re = mag * jnp.cos(lam_im * dt)
    abar_im = mag * jnp.sin(lam_im * dt)
    nr, ni = abar_re - 1.0, abar_im
    den = lam_re * lam_re + lam_im * lam_im
    fr = (nr * lam_re + ni * lam_im) / den
    fi = (ni * lam_re - nr * lam_im) / den
    bbar_re = fr[..., None] * b_re - fi[..., None] * b_im
    bbar_im = fr[..., None] * b_im + fi[..., None] * b_re
    return abar_re, abar_im, bbar_re, bbar_im


def _s5_pack(abar, bbar, cmat):
    G = abar[0][0].shape[0]
    NT = G // S5_TILE_G
    a4 = jnp.stack([abar[d][r] for d in range(2) for r in range(2)]).reshape(4, NT, S5_TILE_W).transpose(1, 0, 2)
    apar = jnp.concatenate([a4, jnp.zeros((NT, 4, S5_TILE_W), F32)], axis=1)
    b4 = jnp.stack([bbar[d][r] for d in range(2) for r in range(2)]).reshape(4, NT, S5_TILE_G, S5_P, S5_CH)
    bt = b4.transpose(1, 0, 3, 2, 4).reshape(NT, 4, S5_P, S5_TILE_CH)
    c4 = jnp.stack([cmat[d][r] for d in range(2) for r in range(2)]).reshape(4, NT, S5_TILE_G, S5_CH, S5_P)
    ct = c4.transpose(1, 0, 3, 2, 4).reshape(NT, 4, S5_CH, S5_TILE_W)
    return apar, bt, ct


def _s5_unpack(dapar, dbd, dcd, G):
    NT = G // S5_TILE_G
    da = dapar[:, :4, :].reshape(NT, 2, 2, S5_TILE_G, S5_P).transpose(1, 2, 0, 3, 4).reshape(2, 2, G, S5_P)
    db = dbd.reshape(NT, 4, S5_TILE_G, S5_CH, S5_P).transpose(1, 0, 2, 4, 3).reshape(2, 2, G, S5_P, S5_CH)
    dc = dcd.reshape(NT, 4, S5_TILE_G, S5_CH, S5_P).transpose(1, 0, 2, 3, 4).reshape(2, 2, G, S5_CH, S5_P)
    return da, db, dc


def _to_segments(a):
    L, D = a.shape
    return a.reshape(NSEG, L // NSEG, D).transpose(1, 0, 2).reshape(L, D)


def _from_segments(a):
    L, D = a.shape
    return a.reshape(L // NSEG, NSEG, D).transpose(1, 0, 2).reshape(L, D)


def _pos_emb(n_tokens, dim):
    rows = n_tokens // GRID_W
    quarter = dim // 4
    omega = 1.0 / (POS_BASE ** (jnp.arange(quarter, dtype=F32) / quarter))

    def enc(p):
        ang = p[:, None] * omega[None, :]
        return jnp.concatenate([jnp.sin(ang), jnp.cos(ang)], axis=-1)

    rtab = enc(jnp.arange(rows, dtype=F32))
    ctab = enc(jnp.arange(GRID_W, dtype=F32))
    return jnp.concatenate([jnp.repeat(rtab, GRID_W, axis=0), jnp.tile(ctab, (rows, 1))], axis=-1)


def _vec(D, **rows):
    names = {"gpost": V_GPOST, "gate": V_GATE, "yscale": V_YSCALE, "gpre": V_GPRE, "shift": V_SHIFT, "scale": V_SCALE}
    out = [jnp.zeros((D,), F32)] * 8
    out[V_YSCALE] = jnp.ones((D,), F32)
    for k, v in rows.items():
        out[names[k]] = v.reshape(D).astype(F32)
    return jnp.stack(out)


def _row0(v, D):
    return jnp.concatenate([v.reshape(1, D).astype(F32), jnp.zeros((7, D), F32)], axis=0)


def _my_block(full, axis, n_local):
    return lax.dynamic_slice_in_dim(full, _my_index() * n_local, n_local, axis)


def kernel(x, c, ctx, c_ctx, ada_w, ada_b, norm_g, s5_lam_re, s5_lam_im, s5_log_step, s5_b_re, s5_b_im, s5_c_re, s5_c_im, s5_d, s5_glu_w, pool_w, pool_scale, ffn_up, ffn_conv, ffn_conv_b, ffn_down, loss_target, m_c_ctx, m_ada_w, m_ada_b, m_norm_g, m_s5_lam_re, m_s5_lam_im, m_s5_log_step, m_s5_b_re, m_s5_b_im, m_s5_c_re, m_s5_c_im, m_s5_d, m_s5_glu_w, m_pool_w, m_pool_scale, m_ffn_up, m_ffn_conv, m_ffn_conv_b, m_ffn_down, v_c_ctx, v_ada_w, v_ada_b, v_norm_g, v_s5_lam_re, v_s5_lam_im, v_s5_log_step, v_s5_b_re, v_s5_b_im, v_s5_c_re, v_s5_c_im, v_s5_d, v_s5_glu_w, v_pool_w, v_pool_scale, v_ffn_up, v_ffn_conv, v_ffn_conv_b, v_ffn_down):
    L, D = x.shape[1], x.shape[2]
    LC = ctx.shape[1]
    G = s5_lam_re.shape[2]
    n_ada = ada_w.shape[2]
    nb_up = ffn_up.shape[2]
    r_down = ffn_down.shape[1]
    FF = N_DEV * r_down
    n_pool = len(POOL_WINDOWS)
    pc = D // n_pool
    pr = pool_w.shape[2]
    ng_loc = norm_g.shape[2]
    me = _my_index()
    axes = ("x", "y", "c")

    up_b = [ffn_up[i].astype(_ACT) for i in range(2)]
    down_b = [ffn_down[i].astype(_ACT) for i in range(2)]
    glu_b = s5_glu_w[0].astype(_ACT)
    pool_b = pool_w[0].reshape(n_pool * pr, pc).astype(_ACT)

    small_loc = jnp.concatenate([c.reshape(-1), norm_g.reshape(-1), pool_scale.reshape(-1), ffn_conv.reshape(-1)])
    n_small = small_loc.shape[0]
    small_g, = _exchange([[jnp.pad(small_loc, (0, (-n_small) % LANE)).reshape(1, -1)]], mode="gather", name="gather_small")
    small_g = small_g.reshape(N_DEV, -1)
    o = 0
    c_all = small_g[:, o:o + D]
    o += D
    ng_all = small_g[:, o:o + 8 * ng_loc].reshape(N_DEV, 2, 4, ng_loc).transpose(1, 2, 0, 3).reshape(2, 4, D)
    o += 8 * ng_loc
    pscale_all = small_g[:, o:o + ng_loc].reshape(D)
    o += ng_loc
    conv_all = small_g[:, o:o + 6 * nb_up].reshape(N_DEV, 2, 3, nb_up).transpose(1, 2, 0, 3).reshape(2, 3, 2 * FF)

    cmat = jnp.concatenate([c_all, c_ctx.reshape(1, D), jnp.zeros((ADA_ROWS - N_DEV - 1, D), F32)], axis=0)
    ada_b_loc = _my_block(ada_b, 1, n_ada).reshape(2, 1, n_ada)
    mods_loc = _ada_fwd(cmat, ada_w, ada_b_loc)
    mods_g, = _exchange([[mods_loc]], mode="gather", name="gather_mods")
    mods_rows = mods_g.reshape(N_DEV, 2, ADA_ROWS, n_ada).transpose(1, 2, 0, 3).reshape(2, ADA_ROWS, 6, D)
    mod = lax.dynamic_index_in_dim(mods_rows, me, axis=1, keepdims=False)
    mod_c = mods_rows[0, N_DEV]

    def disc_all(lr, li, ls, br, bi):
        return [_discretize(lr[d], li[d], ls[d], br[d], bi[d]) for d in range(2)]

    disc, disc_vjp = jax.vjp(disc_all, s5_lam_re[0], s5_lam_im[0], s5_log_step[0], s5_b_re[0], s5_b_im[0])
    apar, bblk, cblk = _s5_pack([(disc[d][0], disc[d][1]) for d in range(2)],
                                [(disc[d][2], disc[d][3]) for d in range(2)],
                                [(s5_c_re[0, d], s5_c_im[0, d]) for d in range(2)])
    dsk = _row0(s5_d[0], D)
    cw = []
    for i in range(2):
        taps = conv_all[i].reshape(3, 2, FF).transpose(1, 0, 2)
        cw.append(jnp.concatenate([taps, ffn_conv_b[i].reshape(2, 1, FF), jnp.zeros((2, 4, FF), F32)], axis=1))

    vecs = {
        "b0": _vec(D, gpre=ng_all[0, 0], shift=mod[0, 0], scale=mod[0, 1]),
        "c0": _vec(D, gpre=ng_all[0, 0], shift=mod_c[0], scale=mod_c[1]),
        "b1": _vec(D, gpost=ng_all[0, 1], gate=mod[0, 2], gpre=ng_all[0, 2], shift=mod[0, 3], scale=mod[0, 4]),
        "b2": _vec(D, gpost=ng_all[0, 3], gate=mod[0, 5], gpre=ng_all[1, 0], shift=mod[1, 0], scale=mod[1, 1]),
        "b3": _vec(D, gpost=ng_all[1, 1], gate=mod[1, 2], yscale=pscale_all, gpre=ng_all[1, 2], shift=mod[1, 3],
                   scale=mod[1, 4]),
        "b4": _vec(D, gpost=ng_all[1, 3], gate=mod[1, 5]),
    }

    x0, u0 = _rows_fwd("rows_fwd_b0", x[0], _pos_emb(L, D), vecs["b0"], add=True, u_dtype=_ACT)
    uc, = _rows_fwd("rows_fwd_ctx", ctx[0], None, vecs["c0"], want_x=False, u_dtype=_ACT)
    u0s, ucs = _to_segments(u0), _to_segments(uc)
    (y_s5, z_s5), (glu_g, up_g0, down_g0) = _s5_fwd(u0s, ucs, bblk, cblk, apar, dsk,
                                                    rider=([[glu_b], [up_b[0]], [down_b[0]]], "gather2"))
    vg = _colblock_fwd("glu_fwd_mm", z_s5, glu_g, 0, _ACT)
    mix0 = _from_segments(_glu_fwd("glu_fwd", vg))
    x1, un0 = _rows_fwd("rows_fwd_b1", x0, mix0, vecs["b1"], u_dtype=_ACT)
    h0, (up_g1,) = _colblock_fwd("ffn0_up", un0, up_g0, 0, _ACT, rider=([[up_b[1]]], "gather2"))
    act0 = _conv_swiglu_fwd("ffn0_conv", h0, cw[0])
    f0, (down_g1, pool_g) = _rowblock_fwd("ffn0_down", act0, down_g0, 0, rider=([[down_b[1]], [pool_b]], "gather2"))
    pool_full = pool_g.reshape(N_DEV, n_pool, pr, pc).transpose(1, 0, 2, 3).reshape(n_pool, pc, pc)
    x2, u1 = _rows_fwd("rows_fwd_b2", x1, f0, vecs["b2"], u_dtype=F32)
    p1 = _pool_window("pool_fwd", u1, False, _ACT)
    ypre1 = _group_mm("pool_fwd_mm", p1, pool_full, "nn", F32)
    x3, un1 = _rows_fwd("rows_fwd_b3", x2, ypre1, vecs["b3"], u_dtype=_ACT)
    h1 = _colblock_fwd("ffn1_up", un1, up_g1, 0, _ACT)
    act1 = _conv_swiglu_fwd("ffn1_conv", h1, cw[1])
    f1 = _rowblock_fwd("ffn1_down", act1, down_g1, 0)
    dx4, loss_blk, df1, red4 = _rows_fwd("rows_fwd_b4", x3, f1, vecs["b4"], target=loss_target[0])
    loss = lax.psum(loss_blk[0, 0], axes)

    dact1 = _rowblock_dgrad("ffn1_down_dgrad", df1, down_g1, 0)
    ddown1 = _rowblock_wgrad("ffn1_down_wgrad", act1, df1)
    ddown1 = ddown1.reshape(N_DEV, r_down, D)
    (dh1, dcw1), (gp_down1a,) = _conv_swiglu_bwd("ffn1_conv_bwd", h1, cw[1], dact1,
                                                rider=([[(ddown1, (0, r_down // 2))]], "scatter"))
    dun1, (gp_down1b,) = _colblock_dgrad("ffn1_up_dgrad", dh1, up_g1, 0, F32,
                                         rider=([[(ddown1, (r_down // 2, r_down // 2))]], "scatter"))
    dup1 = _colblock_wgrad("ffn1_up_wgrad", un1, dh1)
    dx3, dypre1, red3 = _rows_bwd("rows_bwd_b3", dx4, dun1, x3, ypre1, vecs["b3"], dy_dtype=_ACT, yscale_grad=True)
    dp1 = _group_mm("pool_dgrad", dypre1, pool_full, "nt", F32)
    dpool = _group_wgrad("pool_wgrad", p1, dypre1, n_pool)
    du1 = _pool_window("pool_bwd", dp1, True, F32)
    dx2, df0, red2 = _rows_bwd("rows_bwd_b2", dx3, du1, x2, f0, vecs["b2"], dy_dtype=_ACT)
    dact0, (gp_up1c,) = _rowblock_dgrad("ffn0_down_dgrad", df0, down_g0, 0,
                                        rider=([[(dup1, (3 * D // 4, D // 4))]], "scatter"))
    ddown0 = _rowblock_wgrad("ffn0_down_wgrad", act0, df0)
    ddown0 = ddown0.reshape(N_DEV, r_down, D)
    (dh0, dcw0), (gp_down0a,) = _conv_swiglu_bwd("ffn0_conv_bwd", h0, cw[0], dact0,
                                                 rider=([[(ddown0, (0, r_down // 2))]], "scatter"))
    dun0, (gp_up1a,) = _colblock_dgrad("ffn0_up_dgrad", dh0, up_g0, 0, F32,
                                       rider=([[(dup1, (0, D // 2))]], "scatter"))
    dup0, (gp_up1b,) = _colblock_wgrad("ffn0_up_wgrad", un0, dh0, rider=([[(dup1, (D // 2, D // 4))]], "scatter"))
    dx1, dmix0, red1 = _rows_bwd("rows_bwd_b1", dx2, dun0, x1, mix0, vecs["b1"])
    dvg = _glu_bwd("glu_bwd", vg, _to_segments(dmix0))
    dz, (gp_down0b,) = _colblock_dgrad("glu_dgrad", dvg, glu_g, 0, _ACT,
                                       rider=([[(ddown0, (r_down // 2, r_down // 2))]], "scatter"))
    dglu = _colblock_wgrad("glu_wgrad", z_s5, dvg)
    dpool_blocks = dpool.reshape(n_pool, N_DEV, pr, pc).transpose(1, 0, 2, 3).reshape(N_DEV, n_pool * pr, pc)
    (du0s, ducs, dbblk, dcblk, dapar, ddsk), (gp_up0, gp_glu, gp_pool) = _s5_bwd(
        u0s, ucs, dz, y_s5, bblk, cblk, apar, dsk, rider=([[dup0], [dglu], [dpool_blocks]], "scatter"))
    grad_x, red0 = _rows_bwd("rows_bwd_b0", dx1, _from_segments(du0s), x0, None, vecs["b0"])
    redc, = _rows_bwd("rows_bwd_ctx", None, _from_segments(ducs), ctx[0], None, vecs["c0"], want_dx=False)

    zero_d = jnp.zeros((D,), F32)
    dmod = jnp.stack([
        jnp.stack([red0[R_SHIFT], red0[R_SCALE], red1[R_GATE], red1[R_SHIFT], red1[R_SCALE], red2[R_GATE]]),
        jnp.stack([red2[R_SHIFT], red2[R_SCALE], red3[R_GATE], red3[R_SHIFT], red3[R_SCALE], red4[R_GATE]])])
    dmod_c = jnp.stack([jnp.stack([redc[R_SHIFT], redc[R_SCALE]] + [zero_d] * 4), jnp.zeros((6, D), F32)])
    dm_g, = _exchange([[jnp.stack([dmod, dmod_c], axis=1).reshape(2, 2, 6 * D)]], mode="gather", name="gather_dmods")
    dm_g = dm_g.reshape(N_DEV, 2, 2, 6 * D)
    dm_ctx = _sum_parts("sum_dmod_ctx", dm_g[:, :, 1, :])
    dm_rows = jnp.concatenate([dm_g[:, :, 0, :].transpose(1, 0, 2), dm_ctx[:, None, :]], axis=1)
    grad_ada_b = _sum_parts("sum_ada_b", dm_rows.transpose(1, 0, 2))
    dm_cols = dm_rows.reshape(2, N_DEV + 1, N_DEV, n_ada)
    dm_mine = lax.dynamic_index_in_dim(dm_cols, me, axis=2, keepdims=False)
    dm_mine = jnp.concatenate([dm_mine, jnp.zeros((2, ADA_ROWS - N_DEV - 1, n_ada), F32)], axis=1)
    grad_ada_w, dcond = _ada_bwd(cmat, ada_w, dm_mine)
    dcctx_part = dcond[0, N_DEV] + dcond[1, N_DEV]

    da, db, dc = _s5_unpack(dapar, dbblk, dcblk, G)
    dnorm = jnp.stack([
        jnp.stack([red0[R_GPRE] + redc[R_GPRE], red1[R_GPOST], red1[R_GPRE], red2[R_GPOST]]),
        jnp.stack([red2[R_GPRE], red3[R_GPOST], red3[R_GPRE], red4[R_GPOST]])])
    dconv = jnp.stack([d[:, :3, :].transpose(1, 0, 2).reshape(3, 2 * FF) for d in (dcw0, dcw1)])
    dconv_b = jnp.stack([d[:, 3, :].reshape(2 * FF) for d in (dcw0, dcw1)])
    pieces = [dcctx_part, dnorm, da, db, dc, ddsk[:, 0, :], red3[R_YSCALE], dconv, dconv_b]
    flat = jnp.concatenate([p.reshape(-1) for p in pieces])
    n_flat = flat.shape[0]
    per_dev = -(-n_flat // (N_DEV * 8 * LANE)) * 8 * LANE
    flat = jnp.pad(flat, (0, N_DEV * per_dev - n_flat)).reshape(N_DEV, per_dev // LANE, LANE)
    parts, = _exchange([[flat]], mode="scatter", name="scatter_small_grads")
    mine = _sum_parts("sum_small_grads", parts.reshape(N_DEV, per_dev // LANE, LANE))
    summed, = _exchange([[mine]], mode="gather", name="gather_small_grads")
    summed = summed.reshape(-1)
    red_pieces, o = [], 0
    for p in pieces:
        red_pieces.append(summed[o:o + p.size].reshape(p.shape))
        o += p.size
    g_cctx, g_norm, g_a, g_b, g_c, g_d, g_pscale, g_conv, g_conv_b = red_pieces
    cot = [(g_a[d, 0], g_a[d, 1], g_b[d, 0], g_b[d, 1]) for d in range(2)]
    g_lam_re, g_lam_im, g_log_step, g_b_re, g_b_im = disc_vjp(cot)

    out = {}

    def put(name, res, shape):
        out[name] = tuple(r.reshape(shape) for r in res)

    gp_up0, gp_up1a = gp_up0.reshape(N_DEV, D, nb_up), gp_up1a.reshape(N_DEV, D // 2, nb_up)
    quarters = (8, D // 4, nb_up)
    put("ffn_up", _adamw("adamw_ffn_up",
                         [(gp_up0, q * D // 4) for q in range(4)] + [(gp_up1a, 0), (gp_up1a, D // 4),
                                                                     gp_up1b.reshape(N_DEV, D // 4, nb_up),
                                                                     gp_up1c.reshape(N_DEV, D // 4, nb_up)],
                         ffn_up.reshape(quarters), m_ffn_up.reshape(quarters), v_ffn_up.reshape(quarters)),
        ffn_up.shape)
    halves = (4, r_down // 2, D)
    put("ffn_down", _adamw("adamw_ffn_down",
                           [g.reshape(N_DEV, r_down // 2, D) for g in (gp_down0a, gp_down0b, gp_down1a, gp_down1b)],
                           ffn_down.reshape(halves), m_ffn_down.reshape(halves), v_ffn_down.reshape(halves)),
        ffn_down.shape)
    put("s5_glu_w", _adamw("adamw_glu", [gp_glu.reshape(N_DEV, D, -1)], s5_glu_w, m_s5_glu_w, v_s5_glu_w),
        s5_glu_w.shape)
    pool_rows = (1, n_pool * pr, pc)
    put("pool_w", _adamw("adamw_pool", [gp_pool.reshape(N_DEV, n_pool * pr, pc)], pool_w.reshape(pool_rows),
                         m_pool_w.reshape(pool_rows), v_pool_w.reshape(pool_rows)), pool_w.shape)
    put("ada_w", _adamw("adamw_ada_w", [grad_ada_w[i][None] for i in range(2)], ada_w, m_ada_w, v_ada_w), ada_w.shape)

    for nm, w, m, v, g in (("s5_b_re", s5_b_re, m_s5_b_re, v_s5_b_re, g_b_re),
                           ("s5_b_im", s5_b_im, m_s5_b_im, v_s5_b_im, g_b_im),
                           ("s5_c_re", s5_c_re, m_s5_c_re, v_s5_c_re, g_c[:, 0]),
                           ("s5_c_im", s5_c_im, m_s5_c_im, v_s5_c_im, g_c[:, 1])):
        rows = (1, w.size // w.shape[-1], w.shape[-1])
        put(nm, _adamw("adamw_" + nm, [g.reshape(rows)], w.reshape(rows), m.reshape(rows), v.reshape(rows)), w.shape)

    small = [
        ("c_ctx", c_ctx, m_c_ctx, v_c_ctx, g_cctx),
        ("ada_b", ada_b, m_ada_b, v_ada_b, grad_ada_b),
        ("norm_g", norm_g, m_norm_g, v_norm_g, _my_block(g_norm, 2, ng_loc)),
        ("s5_lam_re", s5_lam_re, m_s5_lam_re, v_s5_lam_re, g_lam_re),
        ("s5_lam_im", s5_lam_im, m_s5_lam_im, v_s5_lam_im, g_lam_im),
        ("s5_log_step", s5_log_step, m_s5_log_step, v_s5_log_step, g_log_step),
        ("s5_d", s5_d, m_s5_d, v_s5_d, g_d),
        ("pool_scale", pool_scale, m_pool_scale, v_pool_scale, _my_block(g_pscale, 0, ng_loc)),
        ("ffn_conv", ffn_conv, m_ffn_conv, v_ffn_conv, _my_block(g_conv, 2, nb_up)),
        ("ffn_conv_b", ffn_conv_b, m_ffn_conv_b, v_ffn_conv_b, g_conv_b),
    ]
    n_sm = sum(w.size for _, w, _, _, _ in small)
    rows_sm = -(-n_sm // (512 * LANE)) * 512

    def flat_of(k):
        f = jnp.concatenate([t[k].reshape(-1) for t in small])
        return jnp.pad(f, (0, rows_sm * LANE - n_sm)).reshape(rows_sm, LANE)

    res_sm = _adamw("adamw_small", [flat_of(4)[None]], flat_of(1)[None], flat_of(2)[None], flat_of(3)[None])
    o = 0
    for name, w, _, _, _ in small:
        out[name] = tuple(r.reshape(-1)[o:o + w.size].reshape(w.shape) for r in res_sm)
        o += w.size

    order = ["c_ctx", "ada_w", "ada_b", "norm_g", "s5_lam_re", "s5_lam_im", "s5_log_step", "s5_b_re", "s5_b_im",
             "s5_c_re", "s5_c_im", "s5_d", "s5_glu_w", "pool_w", "pool_scale", "ffn_up", "ffn_conv", "ffn_conv_b",
             "ffn_down"]
    return (loss, grad_x.reshape(x.shape), *[out[n][0] for n in order], *[out[n][1] for n in order],
            *[out[n][2] for n in order], *[out[n][3] for n in order])
```

```python
import functools
import math

import jax
import jax.numpy as jnp
from jax import lax
from jax.experimental import pallas as pl
from jax.experimental.pallas import tpu as pltpu

F32 = jnp.float32
_ACT = jnp.bfloat16
N_DEV = 8
NSEG = 8
S5_CH = 16
S5_P = 64
LANE = 128
S5_TILE_CH = LANE
S5_TILE_G = S5_TILE_CH // S5_CH
S5_TILE_W = S5_TILE_G * S5_P
GRID_W = 64
POOL_WINDOWS = (2, 4, 8, 16)
POOL_HALO = 64
RMS_EPS = 1e-6
POS_BASE = 10000.0
ADAM_LR, ADAM_B1, ADAM_B2, ADAM_EPS, ADAM_WD, ADAM_STEP = 0.001, 0.9, 0.999, 1e-08, 0.01, 10
VMEM_LIMIT = 48 * 1024 * 1024
VMEM_LIMIT_BIG = 58 * 1024 * 1024
MESH = pl.DeviceIdType.MESH
ANY = pl.BlockSpec(memory_space=pl.ANY)


def _params(sem, vmem=VMEM_LIMIT):
    return pltpu.CompilerParams(dimension_semantics=sem, vmem_limit_bytes=vmem)


def _my_index():
    return 4 * lax.axis_index("x") + 2 * lax.axis_index("y") + lax.axis_index("c")


def _xchg_plan(groups, mode):
    flat = [(g, l, a) for g, grp in enumerate(groups) for l, a in enumerate(grp)]
    outs = []
    for grp in groups:
        a, rows = _rows_of(grp[0])
        piece = a.shape[1:] if mode == "scatter" else a.shape
        if rows is not None:
            piece = (rows[1],) + tuple(piece[1:])
        outs.append(jax.ShapeDtypeStruct((N_DEV, len(grp)) + tuple(piece), a.dtype))
    return flat, outs


def _rows_of(entry):
    return entry if isinstance(entry, tuple) else (entry, None)


def _operands(flat):
    return [_rows_of(a)[0] for _, _, a in flat]


def _xchg_sems(n):
    return [pltpu.SemaphoreType.DMA((n, N_DEV - 1)), pltpu.SemaphoreType.DMA((n, N_DEV - 1)),
            pltpu.SemaphoreType.DMA((n,))]


def _xchg_copies(flat, mode, ins, out_refs, sems, waiting=True):
    send_sems, recv_sems, local_sems = sems
    x, y, c = lax.axis_index("x"), lax.axis_index("y"), lax.axis_index("c")
    me = 4 * x + 2 * y + c
    local, first, forwards = [], [], []

    def pair(s, j, dev):
        return dict(send_sem=send_sems.at[s, j], recv_sem=recv_sems.at[s, j], device_id=dev, device_id_type=MESH)

    def block(s, dev):
        rows = _rows_of(flat[s][2])[1]
        ref = ins[s].at[dev]
        return ref if rows is None else ref.at[pl.ds(rows[0], rows[1])]

    for s, (g, l, _) in enumerate(flat):
        src = block(s, me) if mode == "scatter" else ins[s]
        local.append(pltpu.make_async_copy(src, out_refs[g].at[me, l], local_sems.at[s]))
    if mode == "gather2":
        sib, sib_idx = (x, y, 1 - c), 4 * x + 2 * y + (1 - c)
        for s, (g, l, _) in enumerate(flat):
            slot = lambda dev, g=g, l=l: out_refs[g].at[dev, l]
            targets = [(sib, sib_idx)] + [((qx, qy, c), 4 * qx + 2 * qy + c)
                                          for qx, qy in ((1 - x, y), (x, 1 - y), (1 - x, 1 - y))]
            for j, (dev, idx) in enumerate(targets):
                send = pltpu.make_async_remote_copy(src_ref=ins[s], dst_ref=slot(me), **pair(s, j, dev))
                arrive = pltpu.make_async_remote_copy(src_ref=ins[s], dst_ref=slot(idx), **pair(s, j, dev)) if waiting else None
                first.append((send, arrive))
            if waiting:
                for j, (dev, idx) in enumerate(targets[1:]):
                    other = 4 * dev[0] + 2 * dev[1] + (1 - c)
                    send = pltpu.make_async_remote_copy(src_ref=slot(idx), dst_ref=slot(idx), **pair(s, 4 + j, sib))
                    arrive = pltpu.make_async_remote_copy(src_ref=slot(idx), dst_ref=slot(other), **pair(s, 4 + j, sib))
                    forwards.append((first[len(first) - 3 + j][1], send, arrive))
        return local, first, forwards
    for k in range(1, N_DEV):
        px = 1 - x if k & 4 else x
        py = 1 - y if k & 2 else y
        pc = 1 - c if k & 1 else c
        peer = 4 * px + 2 * py + pc
        for s, (g, l, _) in enumerate(flat):
            src = block(s, peer) if mode == "scatter" else ins[s]
            send = pltpu.make_async_remote_copy(src_ref=src, dst_ref=out_refs[g].at[me, l], **pair(s, k - 1, (px, py, pc)))
            arrive = (pltpu.make_async_remote_copy(src_ref=src, dst_ref=out_refs[g].at[peer, l],
                                                   **pair(s, k - 1, (px, py, pc))) if waiting else None)
            first.append((send, arrive))
    return local, first, forwards


def _xchg_start(local, first, forwards):
    for cp in local:
        cp.start()
    for send, _ in first:
        send.start()


def _xchg_wait(local, first, forwards):
    gates = [gate for gate, _, _ in forwards]
    for gate, send, _ in forwards:
        gate.wait_recv()
        send.start()
    for _, arrive in first:
        if not any(arrive is gate for gate in gates):
            arrive.wait_recv()
    for _, _, arrive in forwards:
        arrive.wait_recv()
    for send, _ in first:
        send.wait_send()
    for _, send, _ in forwards:
        send.wait_send()
    for cp in local:
        cp.wait()


def _exchange(groups, mode, name):
    flat, outs = _xchg_plan(groups, mode)
    n = len(flat)

    def body(*refs):
        copies = _xchg_copies(flat, mode, refs[:n], refs[n:n + len(groups)], refs[n + len(groups):])
        _xchg_start(*copies)
        _xchg_wait(*copies)

    res = pl.pallas_call(body, name=name, out_shape=outs, in_specs=[ANY] * n, out_specs=[ANY] * len(groups),
                         scratch_shapes=_xchg_sems(n))(*_operands(flat))
    return list(res)


def _pcall(body, *, name, grid, in_specs, out_specs, out_shape, ins, scratch_shapes=(), sem=None, vmem=VMEM_LIMIT,
           rider=None):
    single = not isinstance(out_shape, (list, tuple))
    if rider is None:
        return pl.pallas_call(body, name=name, grid=grid, in_specs=list(in_specs), out_specs=out_specs,
                              out_shape=out_shape, scratch_shapes=list(scratch_shapes),
                              compiler_params=_params(sem, vmem))(*ins)
    groups, mode = rider
    flat, r_outs = _xchg_plan(groups, mode)
    n_in, n_out = len(ins), 1 if single else len(out_shape)
    nr, ng, ns = len(flat), len(groups), len(scratch_shapes)

    def wrapped(*refs):
        o1 = n_in + nr
        o2 = o1 + n_out
        o3 = o2 + ng
        r_in, r_out, sems = refs[n_in:o1], refs[o2:o3], refs[o3 + ns:]
        first = functools.reduce(jnp.logical_and, [pl.program_id(d) == 0 for d in range(len(grid))])
        last = functools.reduce(jnp.logical_and, [pl.program_id(d) == grid[d] - 1 for d in range(len(grid))])

        @pl.when(first)
        def _():
            _xchg_start(*_xchg_copies(flat, mode, r_in, r_out, sems, waiting=False))

        body(*refs[:n_in], *refs[o1:o2], *refs[o3:o3 + ns])

        @pl.when(last)
        def _():
            _xchg_wait(*_xchg_copies(flat, mode, r_in, r_out, sems))

    outs = pl.pallas_call(
        wrapped, name=name, grid=grid, in_specs=list(in_specs) + [ANY] * nr,
        out_specs=([out_specs] if single else list(out_specs)) + [ANY] * ng,
        out_shape=([out_shape] if single else list(out_shape)) + r_outs,
        scratch_shapes=list(scratch_shapes) + _xchg_sems(nr),
        compiler_params=_params(("arbitrary",) * len(grid), vmem))(*ins, *_operands(flat))
    base = list(outs[:n_out])
    return (base[0] if single else base), list(outs[n_out:])


_DIMS = {"nn": (((1,), (0,)), ((), ())), "nt": (((1,), (1,)), ((), ())), "tn": (((0,), (0,)), ((), ()))}


def _mm(name, a, b, a_spec, b_spec, o_spec, out_shape, grid, dims, rider=None):
    nk = grid[2]
    acc_shape = tuple(d for d in o_spec.block_shape if d is not None)
    dn = _DIMS[dims]

    def tile(ref):
        v = ref[...]
        return v.reshape((-1, v.shape[-1])).astype(_ACT)

    def body(a_ref, b_ref, o_ref, *scratch):
        def part():
            return lax.dot_general(tile(a_ref), tile(b_ref), dn, preferred_element_type=F32)

        if nk == 1:
            o_ref[...] = part().reshape(o_ref.shape).astype(o_ref.dtype)
            return
        acc_ref, = scratch
        k = pl.program_id(2)

        @pl.when(k == 0)
        def _():
            acc_ref[...] = part()

        @pl.when(k > 0)
        def _():
            acc_ref[...] += part()

        @pl.when(k == nk - 1)
        def _():
            o_ref[...] = acc_ref[...].reshape(o_ref.shape).astype(o_ref.dtype)

    acc2d = (math.prod(acc_shape[:-1]), acc_shape[-1])
    return _pcall(body, name=name, out_shape=out_shape, grid=grid, in_specs=[a_spec, b_spec], out_specs=o_spec,
                  scratch_shapes=[] if nk == 1 else [pltpu.VMEM(acc2d, F32)], ins=(a, b),
                  sem=("parallel", "parallel", "arbitrary"), rider=rider)


def _row_tile(n, want):
    t = min(n, want)
    assert n % t == 0, (n, t)
    return t


def _colblock_fwd(name, xa, wg, layer, out_dtype, rider=None):
    L, K = xa.shape
    nb = wg.shape[3]
    half = N_DEV // 2
    tm = _row_tile(L, 512)
    return _mm(name, xa, wg,
               pl.BlockSpec((tm, K), lambda j, i, k: (i, 0)),
               pl.BlockSpec((None, None, K, nb), lambda j, i, k: (j, layer, 0, 0)),
               pl.BlockSpec((None, tm, nb), lambda j, i, k: (j // half, i, j % half)),
               jax.ShapeDtypeStruct((2, L, half * nb), out_dtype), (N_DEV, L // tm, 1), "nn", rider=rider)


def _colblock_dgrad(name, dh, wg, layer, out_dtype, rider=None):
    _, L, _ = dh.shape
    K, nb = wg.shape[2], wg.shape[3]
    half = N_DEV // 2
    tm = _row_tile(L, 512)
    return _mm(name, dh, wg,
               pl.BlockSpec((None, tm, nb), lambda i, j, k: (k // half, i, k % half)),
               pl.BlockSpec((None, None, K, nb), lambda i, j, k: (k, layer, 0, 0)),
               pl.BlockSpec((tm, K), lambda i, j, k: (i, 0)),
               jax.ShapeDtypeStruct((L, K), out_dtype), (L // tm, 1, N_DEV), "nt", rider=rider)


def _colblock_wgrad(name, xa, dh, rider=None):
    L, K = xa.shape
    half = N_DEV // 2
    nb = dh.shape[2] // half
    tm = _row_tile(K, 512)
    tk = L
    return _mm(name, xa, dh,
               pl.BlockSpec((tk, tm), lambda j, i, k: (k, i)),
               pl.BlockSpec((None, tk, nb), lambda j, i, k: (j // half, k, j % half)),
               pl.BlockSpec((None, tm, nb), lambda j, i, k: (j, i, 0)),
               jax.ShapeDtypeStruct((N_DEV, K, nb), _ACT), (N_DEV, K // tm, L // tk), "tn", rider=rider)


def _rowblock_fwd(name, xa, wg, layer, rider=None):
    L, FF = xa.shape
    r, D = wg.shape[2], wg.shape[3]
    tm = _row_tile(L, 512)
    return _mm(name, xa, wg,
               pl.BlockSpec((tm, 2 * r), lambda i, j, k: (i, k)),
               pl.BlockSpec((2, None, r, D), lambda i, j, k: (k, layer, 0, 0)),
               pl.BlockSpec((tm, D), lambda i, j, k: (i, 0)),
               jax.ShapeDtypeStruct((L, D), F32), (L // tm, 1, N_DEV // 2), "nn", rider=rider)


def _rowblock_dgrad(name, dy, wg, layer, rider=None):
    L, D = dy.shape
    r = wg.shape[2]
    tm = _row_tile(L, 512)
    return _mm(name, dy, wg,
               pl.BlockSpec((tm, D), lambda i, j, k: (i, 0)),
               pl.BlockSpec((2, None, r, D), lambda i, j, k: (j, layer, 0, 0)),
               pl.BlockSpec((tm, 2 * r), lambda i, j, k: (i, j)),
               jax.ShapeDtypeStruct((L, N_DEV * r), _ACT), (L // tm, N_DEV // 2, 1), "nt", rider=rider)


def _rowblock_wgrad(name, xa, dy, rider=None):
    L, FF = xa.shape
    D = dy.shape[1]
    tm = FF // (N_DEV // 2)
    tn = _row_tile(D, 1024)
    tk = _row_tile(L, 2048)
    return _mm(name, xa, dy,
               pl.BlockSpec((tk, tm), lambda i, j, k: (k, i)),
               pl.BlockSpec((tk, tn), lambda i, j, k: (k, j)),
               pl.BlockSpec((tm, tn), lambda i, j, k: (i, j)),
               jax.ShapeDtypeStruct((FF, D), _ACT), (FF // tm, D // tn, L // tk), "tn", rider=rider)


def _group_mm(name, xa, w, dims, out_dtype):
    L, D = xa.shape
    ng, pc, _ = w.shape
    tm = _row_tile(L, 512)
    return _mm(name, xa, w,
               pl.BlockSpec((tm, pc), lambda i, g, k: (i, g)),
               pl.BlockSpec((None, pc, pc), lambda i, g, k: (g, 0, 0)),
               pl.BlockSpec((tm, pc), lambda i, g, k: (i, g)),
               jax.ShapeDtypeStruct((L, D), out_dtype), (L // tm, ng, 1), dims)


def _group_wgrad(name, p, dy, ng):
    L, D = p.shape
    pc = D // ng
    tk = _row_tile(L, 512)
    return _mm(name, p, dy,
               pl.BlockSpec((tk, pc), lambda g, j, k: (k, g)),
               pl.BlockSpec((tk, pc), lambda g, j, k: (k, g)),
               pl.BlockSpec((None, pc, pc), lambda g, j, k: (g, 0, 0)),
               jax.ShapeDtypeStruct((ng, pc, pc), _ACT), (ng, 1, L // tk), "tn")


V_GPOST, V_GATE, V_YSCALE, V_GPRE, V_SHIFT, V_SCALE = range(6)
R_SHIFT, R_SCALE, R_GPRE, R_GATE, R_GPOST, R_YSCALE = range(6)
ROW_TILE = 256


def _rstd(v):
    return lax.rsqrt(jnp.mean(v * v, axis=-1, keepdims=True) + RMS_EPS)


def _post_norm_bwd(dyh, yh, ry):
    return ry * (dyh - yh * jnp.mean(dyh * yh, axis=-1, keepdims=True))


def _rows_fwd(name, xres, y, vec, *, add=False, target=None, want_x=True, u_dtype=None):
    L, D = xres.shape
    tm = _row_tile(L, ROW_TILE)
    has_y = y is not None
    last = target is not None
    has_u = u_dtype is not None

    def body(*refs):
        refs = list(refs)
        xres_ref = refs.pop(0)
        y_ref = refs.pop(0) if has_y else None
        vec_ref = refs.pop(0)
        tgt_ref = refs.pop(0) if last else None
        xnew = xres_ref[...]
        if has_y and add:
            xnew = xnew + y_ref[...]
        elif has_y:
            ye = y_ref[...] * vec_ref[V_YSCALE:V_YSCALE + 1, :]
            ry = _rstd(ye)
            yh = ye * ry
            gpost, gate = vec_ref[V_GPOST:V_GPOST + 1, :], vec_ref[V_GATE:V_GATE + 1, :]
            xnew = xnew + gate * (yh * gpost)
        if last:
            dx_ref, loss_ref, dy_ref, red_ref = refs
            diff = xnew - tgt_ref[...]
            dxn = diff * (1.0 / D)
            dx_ref[...] = dxn

            @pl.when(pl.program_id(0) == 0)
            def _():
                loss_ref[...] = jnp.zeros_like(loss_ref)
                red_ref[...] = jnp.zeros_like(red_ref)

            loss_ref[...] += jnp.sum(diff * diff) * (0.5 / D)
            drn2 = dxn * gate
            red_ref[R_GATE:R_GATE + 1, :] += jnp.sum(dxn * (yh * gpost), axis=0, keepdims=True)
            red_ref[R_GPOST:R_GPOST + 1, :] += jnp.sum(drn2 * yh, axis=0, keepdims=True)
            dye = _post_norm_bwd(drn2 * gpost, yh, ry)
            red_ref[R_YSCALE:R_YSCALE + 1, :] += jnp.sum(dye * y_ref[...], axis=0, keepdims=True)
            dy_ref[...] = (dye * vec_ref[V_YSCALE:V_YSCALE + 1, :]).astype(dy_ref.dtype)
            return
        if want_x:
            refs.pop(0)[...] = xnew
        if has_u:
            u_ref, = refs
            n = xnew * _rstd(xnew) * vec_ref[V_GPRE:V_GPRE + 1, :]
            u_ref[...] = (n * (1.0 + vec_ref[V_SCALE:V_SCALE + 1, :]) + vec_ref[V_SHIFT:V_SHIFT + 1, :]).astype(u_ref.dtype)

    row = pl.BlockSpec((tm, D), lambda i: (i, 0))
    vspec = pl.BlockSpec((8, D), lambda i: (0, 0))
    ins, in_specs = [xres], [row]
    if has_y:
        ins.append(y)
        in_specs.append(row)
    ins.append(vec)
    in_specs.append(vspec)
    out_shape, out_specs = [], []
    if last:
        ins.append(target)
        in_specs.append(row)
        out_shape = [jax.ShapeDtypeStruct((L, D), F32), jax.ShapeDtypeStruct((8, LANE), F32),
                     jax.ShapeDtypeStruct((L, D), _ACT), jax.ShapeDtypeStruct((8, D), F32)]
        out_specs = [row, pl.BlockSpec((8, LANE), lambda i: (0, 0)), row, vspec]
    else:
        if want_x:
            out_shape.append(jax.ShapeDtypeStruct((L, D), F32))
            out_specs.append(row)
        if has_u:
            out_shape.append(jax.ShapeDtypeStruct((L, D), u_dtype))
            out_specs.append(row)
    return pl.pallas_call(body, name=name, out_shape=out_shape, grid=(L // tm,), in_specs=in_specs,
                          out_specs=out_specs, compiler_params=_params(("arbitrary",)))(*ins)


def _rows_bwd(name, dxd, du, xnew, y, vec, dy_dtype=F32, want_dx=True, yscale_grad=False):
    L, D = xnew.shape if xnew is not None else dxd.shape
    tm = _row_tile(L, ROW_TILE)
    has_dxd, has_pre, has_post = dxd is not None, du is not None, y is not None

    def body(*refs):
        refs = list(refs)
        dxd_ref = refs.pop(0) if has_dxd else None
        du_ref = refs.pop(0) if has_pre else None
        xnew_ref = refs.pop(0) if has_pre else None
        y_ref = refs.pop(0) if has_post else None
        vec_ref = refs.pop(0)
        dx_ref = refs.pop(0) if want_dx else None
        dy_ref = refs.pop(0) if has_post else None
        red_ref, = refs

        @pl.when(pl.program_id(0) == 0)
        def _():
            red_ref[...] = jnp.zeros_like(red_ref)

        def acc(rw, val):
            red_ref[rw:rw + 1, :] += jnp.sum(val, axis=0, keepdims=True)

        dxn = dxd_ref[...] if has_dxd else None
        if has_pre:
            xn = xnew_ref[...]
            r = _rstd(xn)
            nh = xn * r
            gpre = vec_ref[V_GPRE:V_GPRE + 1, :]
            dub = du_ref[...].astype(F32)
            acc(R_SHIFT, dub)
            acc(R_SCALE, dub * (nh * gpre))
            drn = dub * (1.0 + vec_ref[V_SCALE:V_SCALE + 1, :])
            acc(R_GPRE, drn * nh)
            dnh = drn * gpre
            t = r * (dnh - nh * jnp.mean(dnh * nh, axis=-1, keepdims=True))
            dxn = t if dxn is None else dxn + t
        if want_dx:
            dx_ref[...] = dxn
        if has_post:
            ye = y_ref[...] * vec_ref[V_YSCALE:V_YSCALE + 1, :]
            ry = _rstd(ye)
            yh = ye * ry
            gpost = vec_ref[V_GPOST:V_GPOST + 1, :]
            acc(R_GATE, dxn * (yh * gpost))
            drn2 = dxn * vec_ref[V_GATE:V_GATE + 1, :]
            acc(R_GPOST, drn2 * yh)
            dye = _post_norm_bwd(drn2 * gpost, yh, ry)
            if yscale_grad:
                acc(R_YSCALE, dye * y_ref[...])
            dy_ref[...] = (dye * vec_ref[V_YSCALE:V_YSCALE + 1, :]).astype(dy_ref.dtype)

    row = pl.BlockSpec((tm, D), lambda i: (i, 0))
    vspec = pl.BlockSpec((8, D), lambda i: (0, 0))
    ins, in_specs = [], []
    for a in ([dxd] if has_dxd else []) + ([du, xnew] if has_pre else []) + ([y] if has_post else []):
        ins.append(a)
        in_specs.append(row)
    ins.append(vec)
    in_specs.append(vspec)
    out_shape, out_specs = [], []
    if want_dx:
        out_shape.append(jax.ShapeDtypeStruct((L, D), F32))
        out_specs.append(row)
    if has_post:
        out_shape.append(jax.ShapeDtypeStruct((L, D), dy_dtype))
        out_specs.append(row)
    out_shape.append(jax.ShapeDtypeStruct((8, D), F32))
    out_specs.append(vspec)
    return pl.pallas_call(body, name=name, out_shape=out_shape, grid=(L // tm,), in_specs=in_specs,
                          out_specs=out_specs, compiler_params=_params(("arbitrary",)))(*ins)


def _sigmoid(v):
    return 1.0 / (1.0 + jnp.exp(-v))


def _glu_fwd(name, vg):
    _, L, D = vg.shape
    tm = _row_tile(L, ROW_TILE)

    def body(vg_ref, o_ref):
        o_ref[...] = vg_ref[0].astype(F32) * _sigmoid(vg_ref[1].astype(F32))

    return pl.pallas_call(body, name=name, grid=(L // tm,),
                          in_specs=[pl.BlockSpec((2, tm, D), lambda i: (0, i, 0))],
                          out_specs=pl.BlockSpec((tm, D), lambda i: (i, 0)),
                          out_shape=jax.ShapeDtypeStruct((L, D), F32),
                          compiler_params=_params(("parallel",)))(vg)


def _glu_bwd(name, vg, dout):
    _, L, D = vg.shape
    tm = _row_tile(L, ROW_TILE)

    def body(vg_ref, d_ref, o_ref):
        val, s = vg_ref[0].astype(F32), _sigmoid(vg_ref[1].astype(F32))
        d = d_ref[...]
        o_ref[0] = (d * s).astype(o_ref.dtype)
        o_ref[1] = (d * val * s * (1.0 - s)).astype(o_ref.dtype)

    return pl.pallas_call(body, name=name, grid=(L // tm,),
                          in_specs=[pl.BlockSpec((2, tm, D), lambda i: (0, i, 0)), pl.BlockSpec((tm, D), lambda i: (i, 0))],
                          out_specs=pl.BlockSpec((2, tm, D), lambda i: (0, i, 0)),
                          out_shape=jax.ShapeDtypeStruct((2, L, D), _ACT),
                          compiler_params=_params(("parallel",)))(vg, dout)


CONV_ROWS = 256


def _row_pick(blk, idx):
    rows = lax.broadcasted_iota(jnp.int32, blk.shape, 0)
    return jnp.sum(jnp.where(rows == idx, blk, 0.0), axis=0, keepdims=True)


def _shifted(ref, r0, rc, L):
    cur = ref[pl.ds(r0, rc), :].astype(F32)
    before = ref[pl.ds(pl.multiple_of(jnp.maximum(r0 - 16, 0), 16), 16), :].astype(F32)
    after = ref[pl.ds(pl.multiple_of(jnp.minimum(r0 + rc, L - 16), 16), 16), :].astype(F32)
    prev_row = jnp.where(r0 > 0, _row_pick(before, 15), 0.0)
    next_row = jnp.where(r0 + rc < L, _row_pick(after, 0), 0.0)
    rows = lax.broadcasted_iota(jnp.int32, cur.shape, 0)
    up = jnp.where(rows == 0, prev_row, pltpu.roll(cur, 1, 0))
    down = jnp.where(rows == rc - 1, next_row, pltpu.roll(cur, rc - 1, 0))
    return up, cur, down


def _silu_parts(g):
    s = _sigmoid(g)
    return g * s, s


def _conv_swiglu_fwd(name, h, cw, rider=None):
    _, L, FF = h.shape
    rc = _row_tile(L, CONV_ROWS)

    def body(h_ref, cw_ref, o_ref):
        def chunk(ci, _):
            r0 = pl.multiple_of(ci * rc, rc)
            hc = []
            for half in range(2):
                up, cur, down = _shifted(h_ref.at[half], r0, rc, L)
                hc.append(up * cw_ref[half, 0:1, :] + cur * cw_ref[half, 1:2, :] + down * cw_ref[half, 2:3, :]
                          + cw_ref[half, 3:4, :])
            o_ref[pl.ds(r0, rc), :] = (_silu_parts(hc[1])[0] * hc[0]).astype(o_ref.dtype)
            return 0

        lax.fori_loop(0, L // rc, chunk, 0)

    return _pcall(body, name=name, grid=(FF // LANE,),
                  in_specs=[pl.BlockSpec((2, L, LANE), lambda j: (0, 0, j)),
                            pl.BlockSpec((2, 8, LANE), lambda j: (0, 0, j))],
                  out_specs=pl.BlockSpec((L, LANE), lambda j: (0, j)),
                  out_shape=jax.ShapeDtypeStruct((L, FF), _ACT), ins=(h, cw), sem=("parallel",), rider=rider)


def _conv_swiglu_bwd(name, h, cw, dact, rider=None):
    _, L, FF = h.shape
    rc = _row_tile(L, CONV_ROWS)

    def body(h_ref, cw_ref, da_ref, dh_ref, dcw_ref, dhc_ref):
        def chunk(ci, acc):
            r0 = pl.multiple_of(ci * rc, rc)
            taps, hc = [], []
            for half in range(2):
                t = _shifted(h_ref.at[half], r0, rc, L)
                taps.append(t)
                hc.append(t[0] * cw_ref[half, 0:1, :] + t[1] * cw_ref[half, 1:2, :] + t[2] * cw_ref[half, 2:3, :]
                          + cw_ref[half, 3:4, :])
            d = da_ref[pl.ds(r0, rc), :].astype(F32)
            act, s = _silu_parts(hc[1])
            dhc = (d * act, d * hc[0] * (s + act * (1.0 - s)))
            new = []
            for half in range(2):
                dhc_ref[half, pl.ds(r0, rc), :] = dhc[half]
                for k in range(3):
                    new.append(acc[4 * half + k] + jnp.sum(dhc[half] * taps[half][k], axis=0, keepdims=True))
                new.append(acc[4 * half + 3] + jnp.sum(dhc[half], axis=0, keepdims=True))
            return tuple(new)

        zero = jnp.zeros((1, LANE), F32)
        acc = lax.fori_loop(0, L // rc, chunk, (zero,) * 8)
        dcw_ref[...] = jnp.zeros_like(dcw_ref)
        for half in range(2):
            for k in range(4):
                dcw_ref[half, k:k + 1, :] = acc[4 * half + k]

        def chunk2(ci, _):
            r0 = pl.multiple_of(ci * rc, rc)
            for half in range(2):
                up, cur, down = _shifted(dhc_ref.at[half], r0, rc, L)
                dh_ref[half, pl.ds(r0, rc), :] = (down * cw_ref[half, 0:1, :] + cur * cw_ref[half, 1:2, :]
                                                  + up * cw_ref[half, 2:3, :]).astype(dh_ref.dtype)
            return 0

        lax.fori_loop(0, L // rc, chunk2, 0)

    return _pcall(body, name=name, grid=(FF // LANE,),
                  in_specs=[pl.BlockSpec((2, L, LANE), lambda j: (0, 0, j)),
                            pl.BlockSpec((2, 8, LANE), lambda j: (0, 0, j)),
                            pl.BlockSpec((L, LANE), lambda j: (0, j))],
                  out_specs=[pl.BlockSpec((2, L, LANE), lambda j: (0, 0, j)),
                             pl.BlockSpec((2, 8, LANE), lambda j: (0, 0, j))],
                  out_shape=[jax.ShapeDtypeStruct((2, L, FF), _ACT), jax.ShapeDtypeStruct((2, 8, FF), F32)],
                  scratch_shapes=[pltpu.VMEM((2, L, LANE), F32)], ins=(h, cw, dact), sem=("parallel",), rider=rider)


POOL_ROWS = 256
POOL_TILE = 256


def _pool_bands(transpose):
    i = jnp.arange(POOL_ROWS)[:, None]
    j = jnp.arange(POOL_ROWS + 2 * POOL_HALO)[None, :] - POOL_HALO
    bands = []
    for w in POOL_WINDOWS:
        lo, hi = (-(w // 2 - 1), w // 2) if transpose else (-(w // 2), w // 2 - 1)
        bands.append(((j - i >= lo) & (j - i <= hi)).astype(_ACT))
    return jnp.stack(bands)


def _pool_window(name, u, transpose, out_dtype):
    L, D = u.shape
    ng = len(POOL_WINDOWS)
    pc = D // ng
    tn = min(POOL_TILE, pc)
    rc = _row_tile(L, POOL_ROWS)
    bands = _pool_bands(transpose)
    if rc != POOL_ROWS:
        bands = bands[:, :rc, :rc + 2 * POOL_HALO]
    halo = POOL_HALO

    def body(u_ref, band_ref, o_ref, hi_ref, lo_ref):
        g = (pl.program_id(0) * tn) // pc
        half = jnp.zeros((1, 1), jnp.int32)
        for k, w in enumerate(POOL_WINDOWS):
            half = jnp.where(g == k, w // 2, half)
        zeros = jnp.zeros((halo, tn), _ACT)
        for ref in (hi_ref, lo_ref):
            ref[0:halo, :] = zeros
            ref[halo + L:2 * halo + L, :] = zeros

        def inv_count(r0):
            t = r0 + lax.broadcasted_iota(jnp.int32, (rc, tn), 0)
            lo = jnp.clip(t - half, 0, L - 1)
            hi = jnp.clip(t + half - 1, 0, L - 1)
            return 1.0 / (hi - lo + 1).astype(F32)

        def split(ci, _):
            r0 = pl.multiple_of(ci * rc, rc)
            v = u_ref[pl.ds(r0, rc), :].astype(F32)
            if transpose:
                v = v * inv_count(r0)
            hi = v.astype(_ACT)
            dst = pl.ds(pl.multiple_of(r0 + halo, halo), rc)
            hi_ref[dst, :] = hi
            lo_ref[dst, :] = (v - hi.astype(F32)).astype(_ACT)
            return 0

        lax.fori_loop(0, L // rc, split, 0)
        band = band_ref[...]

        def chunk(ci, _):
            r0 = pl.multiple_of(ci * rc, rc)
            win = pl.ds(r0, rc + 2 * halo)
            s = (jnp.dot(band, hi_ref[win, :], preferred_element_type=F32)
                 + jnp.dot(band, lo_ref[win, :], preferred_element_type=F32))
            if not transpose:
                s = s * inv_count(r0)
            o_ref[pl.ds(r0, rc), :] = (s - u_ref[pl.ds(r0, rc), :].astype(F32)).astype(o_ref.dtype)
            return 0

        lax.fori_loop(0, L // rc, chunk, 0)

    return pl.pallas_call(body, name=name, grid=(D // tn,),
                          in_specs=[pl.BlockSpec((L, tn), lambda j: (0, j)),
                                    pl.BlockSpec((None, rc, rc + 2 * halo), lambda j: ((j * tn) // pc, 0, 0))],
                          out_specs=pl.BlockSpec((L, tn), lambda j: (0, j)),
                          out_shape=jax.ShapeDtypeStruct((L, D), out_dtype),
                          scratch_shapes=[pltpu.VMEM((L + 2 * halo, tn), _ACT), pltpu.VMEM((L + 2 * halo, tn), _ACT)],
                          compiler_params=_params(("parallel",)))(u, bands)


S5_ROWS = 512


def _slab(start):
    return pl.ds(start if isinstance(start, int) else pl.multiple_of(start, NSEG), NSEG)


def _cmul(ar, ai, br, bi):
    return ar * br - ai * bi, ar * bi + ai * br


def _cpow(ar, ai, n):
    rr, ri = None, None
    br, bi = ar, ai
    while n:
        if n & 1:
            rr, ri = (br, bi) if rr is None else _cmul(rr, ri, br, bi)
        n >>= 1
        if n:
            br, bi = _cmul(br, bi, br, bi)
    return rr, ri


def _pow_table(pw_ref, ar, ai, n):
    W = ar.shape[1]
    pr, pi = ar, ai
    for r in range(NSEG):
        pw_ref[0, r:r + 1, :] = pr
        pw_ref[1, r:r + 1, :] = pi
        if r < NSEG - 1:
            pr, pi = _cmul(pr, pi, ar, ai)
    a8r, a8i = (jnp.broadcast_to(v, (NSEG, W)) for v in _cpow(ar, ai, NSEG))

    def step(k, carry):
        nr, ni = _cmul(carry[0], carry[1], a8r, a8i)
        pw_ref[0, _slab(k * NSEG), :] = nr
        pw_ref[1, _slab(k * NSEG), :] = ni
        return nr, ni

    lax.fori_loop(1, n // NSEG, step, (pw_ref[0, 0:NSEG, :], pw_ref[1, 0:NSEG, :]))


def _seg_scan(sr_ref, si_ref, tmp_ref, pw_ref, row0, n, ar, ai, h0, rev, conj=False, pair_with=None):
    W = ar.shape[1]
    arb, aib = jnp.broadcast_to(ar, (NSEG, W)), jnp.broadcast_to(ai, (NSEG, W))

    def rows(s):
        t = (n - 1 - s) if rev else s
        return _slab(row0 + t * NSEG)

    def step(s, carry):
        hr, hi = carry
        sl = rows(s)
        nr = arb * hr - aib * hi + sr_ref[sl, :]
        ni = arb * hi + aib * hr + si_ref[sl, :]
        sr_ref[sl, :] = nr
        si_ref[sl, :] = ni
        return nr, ni

    zero = jnp.zeros((NSEG, W), F32)
    fr, fi = lax.fori_loop(0, n, step, (zero, zero), unroll=2)
    tmp_ref[0] = fr
    tmp_ref[1] = fi
    anr, ani = _cpow(ar, ai, n)
    cr, ci = h0
    for j in (range(NSEG - 1, -1, -1) if rev else range(NSEG)):
        tmp_ref[2, j:j + 1, :] = cr
        tmp_ref[3, j:j + 1, :] = ci
        pr, pi = _cmul(anr, ani, cr, ci)
        cr, ci = tmp_ref[0, j:j + 1, :] + pr, tmp_ref[1, j:j + 1, :] + pi
    cmr, cmi = tmp_ref[2], tmp_ref[3]

    def fix(k, acc):
        for r in range(NSEG):
            row = pl.ds(pl.multiple_of(k * NSEG, NSEG) + r, 1)
            pr = jnp.broadcast_to(pw_ref[0, row, :], (NSEG, W))
            pi = jnp.broadcast_to(pw_ref[1, row, :], (NSEG, W))
            s = k * NSEG + r
            sl = rows(s)
            if conj:
                gr = sr_ref[sl, :] + (pr * cmr + pi * cmi)
                gi = si_ref[sl, :] + (pr * cmi - pi * cmr)
            else:
                gr = sr_ref[sl, :] + (pr * cmr - pi * cmi)
                gi = si_ref[sl, :] + (pr * cmi + pi * cmr)
            sr_ref[sl, :] = gr
            si_ref[sl, :] = gi
            if pair_with is not None:
                prev = rows(jnp.minimum(s + 1, n - 1))
                hpr, hpi = pair_with[0][prev, :], pair_with[1][prev, :]
                acc = (acc[0] + hpr * gr + hpi * gi, acc[1] + hpr * gi - hpi * gr)
        return acc

    if pair_with is None:
        lax.fori_loop(0, n // NSEG, fix, 0)
        return (cr, ci), (cmr, cmi), None
    total = lax.fori_loop(0, n // NSEG, fix, (zero, zero))
    h_r, h_i, (hcr, hci) = pair_with
    last = rows(n - 1)
    fr, fi = hcr - h_r[last, :], hci - h_i[last, :]
    gr, gi = sr_ref[last, :], si_ref[last, :]
    total = (total[0] + fr * gr + fi * gi, total[1] + fr * gi - fi * gr)
    return (cr, ci), (cmr, cmi), total


def _gelu_tanh(y):
    k = math.sqrt(2.0 / math.pi)
    t = jnp.tanh(k * (y + 0.044715 * y * y * y))
    return 0.5 * y * (1.0 + t), t


def _s5_chunks(L):
    rc = _row_tile(L, S5_ROWS)
    return [(r, rc) for r in range(0, L, rc)]


_NT_DIMS = (((1,), (1,)), ((), ()))
_TN_DIMS = (((0,), (0,)), ((), ()))
_LOG_P, _LOG_CH = S5_P.bit_length() - 1, S5_CH.bit_length() - 1


def _same_group(shape, row_shift, col_shift):
    rows = lax.broadcasted_iota(jnp.int32, shape, 0)
    cols = lax.broadcasted_iota(jnp.int32, shape, 1)
    return lax.shift_right_logical(rows, row_shift) == lax.shift_right_logical(cols, col_shift)


def _s5_bt(bt):
    full = jnp.concatenate([bt] * S5_TILE_G, axis=0)
    return jnp.where(_same_group(full.shape, _LOG_P, _LOG_CH), full, 0.0).astype(_ACT)


def _s5_ct(ct):
    full = jnp.concatenate([ct] * S5_TILE_G, axis=0)
    return jnp.where(_same_group(full.shape, _LOG_CH, _LOG_P), full, 0.0).astype(_ACT)


def _s5_diag(m):
    m = jnp.where(_same_group(m.shape, _LOG_CH, _LOG_P), m, 0.0)
    rows = lax.broadcasted_iota(jnp.int32, (S5_TILE_W, S5_P), 0)
    cols = lax.broadcasted_iota(jnp.int32, (S5_TILE_W, S5_P), 1)
    pick = (jnp.bitwise_and(rows, S5_P - 1) == cols).astype(_ACT)
    hi = m.astype(_ACT)
    lo = (m - hi.astype(F32)).astype(_ACT)
    return jnp.dot(hi, pick, preferred_element_type=F32) + jnp.dot(lo, pick, preferred_element_type=F32)


def _s5_project(u_ref, uc_ref, bre, bim, sr_ref, si_ref, L, LC):
    for ref, base, n in ((u_ref, 0, L), (uc_ref, L, LC)):
        for r, rc in _s5_chunks(n):
            ub = ref[r:r + rc, :].astype(_ACT)
            sr_ref[base + r:base + r + rc, :] = lax.dot_general(ub, bre, _NT_DIMS, preferred_element_type=F32)
            si_ref[base + r:base + r + rc, :] = lax.dot_general(ub, bim, _NT_DIMS, preferred_element_type=F32)


def _s5_states(sr_ref, si_ref, tmp_ref, pw_ref, ar, ai, L, LC, rev):
    W = ar.shape[1]
    zero = (jnp.zeros((1, W), F32), jnp.zeros((1, W), F32))
    hctx, cm_ctx, _ = _seg_scan(sr_ref, si_ref, tmp_ref, pw_ref, L, LC // NSEG, ar, ai, zero, rev)
    _, cm_lat, _ = _seg_scan(sr_ref, si_ref, tmp_ref, pw_ref, 0, L // NSEG, ar, ai, hctx, rev)
    return cm_lat, cm_ctx


def _s5_fwd(u, uc, bblk, cblk, apar, dsk, rider=None):
    L, D = u.shape
    LC = uc.shape[0]
    NT, W, TC = D // S5_TILE_CH, S5_TILE_W, S5_TILE_CH

    def body(u_ref, uc_ref, b_ref, c_ref, a_ref, d_ref, y_ref, z_ref, sr_ref, si_ref, tmp_ref, pw_ref):
        for r, rc in _s5_chunks(L):
            y_ref[r:r + rc, :] = u_ref[r:r + rc, :].astype(F32) * d_ref[0:1, :]
        for d in range(2):
            ar, ai = a_ref[2 * d:2 * d + 1, :], a_ref[2 * d + 1:2 * d + 2, :]
            _pow_table(pw_ref, ar, ai, L // NSEG)
            _s5_project(u_ref, uc_ref, _s5_bt(b_ref[2 * d]), _s5_bt(b_ref[2 * d + 1]), sr_ref, si_ref, L, LC)
            _s5_states(sr_ref, si_ref, tmp_ref, pw_ref, ar, ai, L, LC, rev=(d == 1))
            cre, cim = _s5_ct(c_ref[2 * d]), _s5_ct(c_ref[2 * d + 1])
            for r, rc in _s5_chunks(L):
                y_ref[r:r + rc, :] += (
                    lax.dot_general(sr_ref[r:r + rc, :].astype(_ACT), cre, _NT_DIMS, preferred_element_type=F32)
                    - lax.dot_general(si_ref[r:r + rc, :].astype(_ACT), cim, _NT_DIMS, preferred_element_type=F32))
        for r, rc in _s5_chunks(L):
            z_ref[r:r + rc, :] = _gelu_tanh(y_ref[r:r + rc, :])[0].astype(z_ref.dtype)

    col = lambda n: pl.BlockSpec((n, TC), lambda j: (0, j))
    return _pcall(
        body, name="s5_fwd", grid=(NT,), ins=(u, uc, bblk, cblk, apar, dsk), sem=("parallel",), rider=rider,
        in_specs=[col(L), col(LC),
                  pl.BlockSpec((None, 4, S5_P, TC), lambda j: (j, 0, 0, 0)),
                  pl.BlockSpec((None, 4, S5_CH, W), lambda j: (j, 0, 0, 0)),
                  pl.BlockSpec((None, 8, W), lambda j: (j, 0, 0)),
                  pl.BlockSpec((8, TC), lambda j: (0, j))],
        out_specs=[col(L), col(L)],
        out_shape=[jax.ShapeDtypeStruct((L, D), F32), jax.ShapeDtypeStruct((L, D), _ACT)],
        scratch_shapes=[pltpu.VMEM((L + LC, W), F32), pltpu.VMEM((L + LC, W), F32), pltpu.VMEM((4, NSEG, W), F32),
                        pltpu.VMEM((2, L // NSEG, W), F32)])


def _s5_bwd(u, uc, dz, y, bblk, cblk, apar, dsk, rider=None):
    L, D = u.shape
    LC = uc.shape[0]
    NT, W, TC = D // S5_TILE_CH, S5_TILE_W, S5_TILE_CH
    nl, nc = L // NSEG, LC // NSEG

    def body(u_ref, uc_ref, dz_ref, y_ref, b_ref, c_ref, a_ref, d_ref,
             du_ref, duc_ref, db_ref, dc_ref, da_ref, dd_ref,
             hr_ref, hi_ref, gr_ref, gi_ref, dy_ref, tmp_ref, pw_ref):
        ddacc = jnp.zeros((1, TC), F32)
        for r, rc in _s5_chunks(L):
            yv = y_ref[r:r + rc, :]
            g, t = _gelu_tanh(yv)
            k = math.sqrt(2.0 / math.pi)
            dg = 0.5 * (1.0 + t) + 0.5 * yv * (1.0 - t * t) * k * (1.0 + 3 * 0.044715 * yv * yv)
            dy = dz_ref[r:r + rc, :].astype(F32) * dg
            uv = u_ref[r:r + rc, :].astype(F32)
            ddacc = ddacc + jnp.sum(dy * uv, axis=0, keepdims=True)
            du_ref[r:r + rc, :] = dy * d_ref[0:1, :]
            dy_ref[r:r + rc, :] = dy.astype(dy_ref.dtype)
        dd_ref[...] = jnp.zeros_like(dd_ref)
        dd_ref[0:1, :] = ddacc
        duc_ref[...] = jnp.zeros_like(duc_ref)
        da_ref[...] = jnp.zeros_like(da_ref)
        tn = _TN_DIMS
        for d in range(2):
            rev = d == 1
            ar, ai = a_ref[2 * d:2 * d + 1, :], a_ref[2 * d + 1:2 * d + 2, :]
            bre, bim = _s5_bt(b_ref[2 * d]), _s5_bt(b_ref[2 * d + 1])
            cre, cim = _s5_ct(c_ref[2 * d]), _s5_ct(c_ref[2 * d + 1])
            _pow_table(pw_ref, ar, ai, nl)
            _s5_project(u_ref, uc_ref, bre, bim, hr_ref, hi_ref, L, LC)
            cm_lat, cm_ctx = _s5_states(hr_ref, hi_ref, tmp_ref, pw_ref, ar, ai, L, LC, rev)
            cml_r, cml_i, cmc_r, cmc_i = cm_lat[0], cm_lat[1], cm_ctx[0], cm_ctx[1]
            dcr = jnp.zeros((TC, W), F32)
            dci = jnp.zeros((TC, W), F32)
            for r, rc in _s5_chunks(L):
                dyb = dy_ref[r:r + rc, :]
                gr_ref[r:r + rc, :] = jnp.dot(dyb, cre, preferred_element_type=F32)
                gi_ref[r:r + rc, :] = -jnp.dot(dyb, cim, preferred_element_type=F32)
                dcr = dcr + lax.dot_general(dyb, hr_ref[r:r + rc, :].astype(_ACT), tn, preferred_element_type=F32)
                dci = dci - lax.dot_general(dyb, hi_ref[r:r + rc, :].astype(_ACT), tn, preferred_element_type=F32)
            dc_ref[2 * d] = _s5_diag(dcr)
            dc_ref[2 * d + 1] = _s5_diag(dci)
            gr_ref[L:L + LC, :] = jnp.zeros((LC, W), F32)
            gi_ref[L:L + LC, :] = jnp.zeros((LC, W), F32)
            zero = (jnp.zeros((1, W), F32), jnp.zeros((1, W), F32))
            glat, _, (lr, li) = _seg_scan(gr_ref, gi_ref, tmp_ref, pw_ref, 0, nl, ar, -ai, zero, not rev, conj=True,
                                          pair_with=(hr_ref, hi_ref, (cml_r, cml_i)))
            _, _, (qr, qi) = _seg_scan(gr_ref, gi_ref, tmp_ref, pw_ref, L, nc, ar, -ai, glat, not rev, conj=True,
                                       pair_with=(hr_ref, hi_ref, (cmc_r, cmc_i)))
            da_ref[2 * d:2 * d + 1, :] = jnp.sum(lr + qr, axis=0, keepdims=True)
            da_ref[2 * d + 1:2 * d + 2, :] = jnp.sum(li + qi, axis=0, keepdims=True)
            dbr = jnp.zeros((TC, W), F32)
            dbi = jnp.zeros((TC, W), F32)
            for ref, oref, base, n in ((u_ref, du_ref, 0, L), (uc_ref, duc_ref, L, LC)):
                for r, rc in _s5_chunks(n):
                    ub = ref[r:r + rc, :].astype(_ACT)
                    gr = gr_ref[base + r:base + r + rc, :].astype(_ACT)
                    gi = gi_ref[base + r:base + r + rc, :].astype(_ACT)
                    dbr = dbr + lax.dot_general(ub, gr, tn, preferred_element_type=F32)
                    dbi = dbi + lax.dot_general(ub, gi, tn, preferred_element_type=F32)
                    oref[r:r + rc, :] += (jnp.dot(gr, bre, preferred_element_type=F32)
                                          + jnp.dot(gi, bim, preferred_element_type=F32))
            db_ref[2 * d] = _s5_diag(dbr)
            db_ref[2 * d + 1] = _s5_diag(dbi)

    col = lambda n: pl.BlockSpec((n, TC), lambda j: (0, j))
    bspec = pl.BlockSpec((None, 4, S5_P, TC), lambda j: (j, 0, 0, 0))
    cspec = pl.BlockSpec((None, 4, S5_CH, W), lambda j: (j, 0, 0, 0))
    gspec = pl.BlockSpec((None, 4, TC, S5_P), lambda j: (j, 0, 0, 0))
    aspec = pl.BlockSpec((None, 8, W), lambda j: (j, 0, 0))
    return _pcall(
        body, name="s5_bwd", grid=(NT,), ins=(u, uc, dz, y, bblk, cblk, apar, dsk), sem=("parallel",),
        vmem=VMEM_LIMIT_BIG, rider=rider,
        in_specs=[col(L), col(LC), col(L), col(L), bspec, cspec, aspec, pl.BlockSpec((8, TC), lambda j: (0, j))],
        out_specs=[col(L), col(LC), gspec, gspec, aspec, pl.BlockSpec((None, 8, TC), lambda j: (j, 0, 0))],
        out_shape=[jax.ShapeDtypeStruct((L, D), F32), jax.ShapeDtypeStruct((LC, D), F32),
                   jax.ShapeDtypeStruct((NT, 4, TC, S5_P), F32), jax.ShapeDtypeStruct((NT, 4, TC, S5_P), F32),
                   jax.ShapeDtypeStruct((NT, 8, W), F32), jax.ShapeDtypeStruct((NT, 8, TC), F32)],
        scratch_shapes=[pltpu.VMEM((L + LC, W), F32), pltpu.VMEM((L + LC, W), F32),
                        pltpu.VMEM((L + LC, W), F32), pltpu.VMEM((L + LC, W), F32),
                        pltpu.VMEM((L, TC), _ACT), pltpu.VMEM((4, NSEG, W), F32), pltpu.VMEM((2, nl, W), F32)])


ADA_ROWS = 16


def _silu_rows(c_ref):
    c = c_ref[...]
    return c * _sigmoid(c)


def _ada_fwd(cmat, ada_w, ada_b):
    nl, D, n = ada_w.shape
    tn = _row_tile(n, 512)

    def body(c_ref, w_ref, b_ref, o_ref):
        a = _silu_rows(c_ref).astype(_ACT)
        o_ref[...] = jnp.dot(a, w_ref[...].astype(_ACT), preferred_element_type=F32) + b_ref[...]

    return pl.pallas_call(body, name="ada_fwd", grid=(nl, n // tn),
                          in_specs=[pl.BlockSpec((ADA_ROWS, D), lambda l, j: (0, 0)),
                                    pl.BlockSpec((None, D, tn), lambda l, j: (l, 0, j)),
                                    pl.BlockSpec((None, 1, tn), lambda l, j: (l, 0, j))],
                          out_specs=pl.BlockSpec((None, ADA_ROWS, tn), lambda l, j: (l, 0, j)),
                          out_shape=jax.ShapeDtypeStruct((nl, ADA_ROWS, n), F32),
                          compiler_params=_params(("parallel", "parallel")))(cmat, ada_w, ada_b)


def _ada_bwd(cmat, ada_w, dm):
    nl, D, n = ada_w.shape
    tn = _row_tile(n, 512)
    nj = n // tn

    def body(c_ref, w_ref, dm_ref, dw_ref, dc_ref):
        c = c_ref[...]
        s = _sigmoid(c)
        a = (c * s).astype(_ACT)
        dmb = dm_ref[...].astype(_ACT)
        dw_ref[...] = lax.dot_general(a, dmb, (((0,), (0,)), ((), ())), preferred_element_type=F32)
        part = lax.dot_general(dmb, w_ref[...].astype(_ACT), (((1,), (1,)), ((), ())), preferred_element_type=F32)
        part = part * (s * (1.0 + c * (1.0 - s)))

        @pl.when(pl.program_id(1) == 0)
        def _():
            dc_ref[...] = part

        @pl.when(pl.program_id(1) > 0)
        def _():
            dc_ref[...] += part

    return pl.pallas_call(body, name="ada_bwd", grid=(nl, nj),
                          in_specs=[pl.BlockSpec((ADA_ROWS, D), lambda l, j: (0, 0)),
                                    pl.BlockSpec((None, D, tn), lambda l, j: (l, 0, j)),
                                    pl.BlockSpec((None, ADA_ROWS, tn), lambda l, j: (l, 0, j))],
                          out_specs=[pl.BlockSpec((None, D, tn), lambda l, j: (l, 0, j)),
                                     pl.BlockSpec((None, ADA_ROWS, D), lambda l, j: (l, 0, 0))],
                          out_shape=[jax.ShapeDtypeStruct((nl, D, n), F32), jax.ShapeDtypeStruct((nl, ADA_ROWS, D), F32)],
                          compiler_params=_params(("parallel", "arbitrary")))(cmat, ada_w, dm)


def _adamw(name, gparts, w, m, v):
    nl, R, C = w.shape
    gparts = [g if isinstance(g, tuple) else (g, 0) for g in gparts]
    n = gparts[0][0].shape[0]
    runs = []
    for l, (g, r0) in enumerate(gparts):
        if runs and runs[-1][0] is g and runs[-1][1] + runs[-1][3] * R == r0:
            runs[-1][3] += 1
        else:
            runs.append([g, r0, l, 1])
    run_of = [q for q, run in enumerate(runs) for _ in range(run[3])]
    tr = R
    part_bytes = len(runs) * n * C * gparts[0][0].dtype.itemsize * 2
    for cand in (4096, 2048, 1024, 512, 256, 128, 64, 32, 16, 8):
        if R % cand == 0 and cand * max(C, LANE) * 4 <= 2 * 1024 * 1024 and cand * part_bytes <= VMEM_LIMIT // 2:
            tr = cand
            break
    nt = R // tr
    bc1 = 1.0 - ADAM_B1 ** ADAM_STEP
    bc2 = 1.0 - ADAM_B2 ** ADAM_STEP

    def body(*refs):
        g_refs = refs[:len(runs)]
        w_ref, m_ref, v_ref, go_ref, d_ref, mo_ref, vo_ref = refs[len(runs):]
        for l in range(nl):
            @pl.when(pl.program_id(0) == l)
            def _(g_ref=g_refs[run_of[l]]):
                g = g_ref[0].astype(F32)
                for j in range(1, n):
                    g = g + g_ref[j].astype(F32)
                m2 = ADAM_B1 * m_ref[...] + (1.0 - ADAM_B1) * g
                v2 = ADAM_B2 * v_ref[...] + (1.0 - ADAM_B2) * (g * g)
                go_ref[...] = g
                mo_ref[...] = m2
                vo_ref[...] = v2
                d_ref[...] = -ADAM_LR * ((m2 / bc1) / (jnp.sqrt(v2 / bc2) + ADAM_EPS) + ADAM_WD * w_ref[...])

    def gspec(run):
        _, r0, l0, count = run
        return pl.BlockSpec((n, tr, C),
                            lambda lyr, i: (0, r0 // tr + jnp.clip((lyr - l0) * nt + i, 0, count * nt - 1), 0))

    row = pl.BlockSpec((None, tr, C), lambda lyr, i: (lyr, i, 0))
    out = jax.ShapeDtypeStruct((nl, R, C), F32)
    return _pcall(body, name=name, grid=(nl, nt), in_specs=[gspec(run) for run in runs] + [row, row, row],
                  out_specs=[row, row, row, row], out_shape=[out, out, out, out],
                  ins=(*[run[0] for run in runs], w, m, v), sem=("arbitrary", "arbitrary"))


def _sum_parts(name, parts):
    n, R, C = parts.shape

    def body(p_ref, o_ref):
        s = p_ref[0]
        for j in range(1, n):
            s = s + p_ref[j]
        o_ref[...] = s

    return pl.pallas_call(body, name=name, out_shape=jax.ShapeDtypeStruct((R, C), F32),
                          compiler_params=_params(None))(parts)


def _discretize(lam_re, lam_im, log_step, b_re, b_im):
    dt = jnp.exp(log_step)[:, None]
    mag = jnp.exp(lam_re * dt)
    abar_re = mag * jnp.cos(lam_im * dt)
    abar_im = mag * jnp.sin(lam_im * dt)
    nr, ni = abar_re - 1.0, abar_im
    den = lam_re * lam_re + lam_im * lam_im
    fr = (nr * lam_re + ni * lam_im) / den
    fi = (ni * lam_re - nr * lam_im) / den
    bbar_re = fr[..., None] * b_re - fi[..., None] * b_im
    bbar_im = fr[..., None] * b_im + fi[..., None] * b_re
    return abar_re, abar_im, bbar_re, bbar_im


def _s5_pack(abar, bbar, cmat):
    G = abar[0][0].shape[0]
    NT = G // S5_TILE_G
    a4 = jnp.stack([abar[d][r] for d in range(2) for r in range(2)]).reshape(4, NT, S5_TILE_W).transpose(1, 0, 2)
    apar = jnp.concatenate([a4, jnp.zeros((NT, 4, S5_TILE_W), F32)], axis=1)
    b4 = jnp.stack([bbar[d][r] for d in range(2) for r in range(2)]).reshape(4, NT, S5_TILE_G, S5_P, S5_CH)
    bt = b4.transpose(1, 0, 3, 2, 4).reshape(NT, 4, S5_P, S5_TILE_CH)
    c4 = jnp.stack([cmat[d][r] for d in range(2) for r in range(2)]).reshape(4, NT, S5_TILE_G, S5_CH, S5_P)
    ct = c4.transpose(1, 0, 3, 2, 4).reshape(NT, 4, S5_CH, S5_TILE_W)
    return apar, bt, ct


def _s5_unpack(dapar, dbd, dcd, G):
    NT = G // S5_TILE_G
    da = dapar[:, :4, :].reshape(NT, 2, 2, S5_TILE_G, S5_P).transpose(1, 2, 0, 3, 4).reshape(2, 2, G, S5_P)
    db = dbd.reshape(NT, 4, S5_TILE_G, S5_CH, S5_P).transpose(1, 0, 2, 4, 3).reshape(2, 2, G, S5_P, S5_CH)
    dc = dcd.reshape(NT, 4, S5_TILE_G, S5_CH, S5_P).transpose(1, 0, 2, 3, 4).reshape(2, 2, G, S5_CH, S5_P)
    return da, db, dc


def _to_segments(a):
    L, D = a.shape
    return a.reshape(NSEG, L // NSEG, D).transpose(1, 0, 2).reshape(L, D)


def _from_segments(a):
    L, D = a.shape
    return a.reshape(L // NSEG, NSEG, D).transpose(1, 0, 2).reshape(L, D)


def _pos_emb(n_tokens, dim):
    rows = n_tokens // GRID_W
    quarter = dim // 4
    omega = 1.0 / (POS_BASE ** (jnp.arange(quarter, dtype=F32) / quarter))

    def enc(p):
        ang = p[:, None] * omega[None, :]
        return jnp.concatenate([jnp.sin(ang), jnp.cos(ang)], axis=-1)

    rtab = enc(jnp.arange(rows, dtype=F32))
    ctab = enc(jnp.arange(GRID_W, dtype=F32))
    return jnp.concatenate([jnp.repeat(rtab, GRID_W, axis=0), jnp.tile(ctab, (rows, 1))], axis=-1)


def _vec(D, **rows):
    names = {"gpost": V_GPOST, "gate": V_GATE, "yscale": V_YSCALE, "gpre": V_GPRE, "shift": V_SHIFT, "scale": V_SCALE}
    out = [jnp.zeros((D,), F32)] * 8
    out[V_YSCALE] = jnp.ones((D,), F32)
    for k, v in rows.items():
        out[names[k]] = v.reshape(D).astype(F32)
    return jnp.stack(out)


def _row0(v, D):
    return jnp.concatenate([v.reshape(1, D).astype(F32), jnp.zeros((7, D), F32)], axis=0)


def _my_block(full, axis, n_local):
    return lax.dynamic_slice_in_dim(full, _my_index() * n_local, n_local, axis)


def kernel(x, c, ctx, c_ctx, ada_w, ada_b, norm_g, s5_lam_re, s5_lam_im, s5_log_step, s5_b_re, s5_b_im, s5_c_re, s5_c_im, s5_d, s5_glu_w, pool_w, pool_scale, ffn_up, ffn_conv, ffn_conv_b, ffn_down, loss_target, m_c_ctx, m_ada_w, m_ada_b, m_norm_g, m_s5_lam_re, m_s5_lam_im, m_s5_log_step, m_s5_b_re, m_s5_b_im, m_s5_c_re, m_s5_c_im, m_s5_d, m_s5_glu_w, m_pool_w, m_pool_scale, m_ffn_up, m_ffn_conv, m_ffn_conv_b, m_ffn_down, v_c_ctx, v_ada_w, v_ada_b, v_norm_g, v_s5_lam_re, v_s5_lam_im, v_s5_log_step, v_s5_b_re, v_s5_b_im, v_s5_c_re, v_s5_c_im, v_s5_d, v_s5_glu_w, v_pool_w, v_pool_scale, v_ffn_up, v_ffn_conv, v_ffn_conv_b, v_ffn_down):
    L, D = x.shape[1], x.shape[2]
    LC = ctx.shape[1]
    G = s5_lam_re.shape[2]
    n_ada = ada_w.shape[2]
    nb_up = ffn_up.shape[2]
    r_down = ffn_down.shape[1]
    FF = N_DEV * r_down
    n_pool = len(POOL_WINDOWS)
    pc = D // n_pool
    pr = pool_w.shape[2]
    ng_loc = norm_g.shape[2]
    me = _my_index()
    axes = ("x", "y", "c")

    up_b = [ffn_up[i].astype(_ACT) for i in range(2)]
    down_b = [ffn_down[i].astype(_ACT) for i in range(2)]
    glu_b = s5_glu_w[0].astype(_ACT)
    pool_b = pool_w[0].reshape(n_pool * pr, pc).astype(_ACT)

    small_loc = jnp.concatenate([c.reshape(-1), norm_g.reshape(-1), pool_scale.reshape(-1), ffn_conv.reshape(-1)])
    n_small = small_loc.shape[0]
    small_g, = _exchange([[jnp.pad(small_loc, (0, (-n_small) % LANE)).reshape(1, -1)]], mode="gather", name="gather_small")
    small_g = small_g.reshape(N_DEV, -1)
    o = 0
    c_all = small_g[:, o:o + D]
    o += D
    ng_all = small_g[:, o:o + 8 * ng_loc].reshape(N_DEV, 2, 4, ng_loc).transpose(1, 2, 0, 3).reshape(2, 4, D)
    o += 8 * ng_loc
    pscale_all = small_g[:, o:o + ng_loc].reshape(D)
    o += ng_loc
    conv_all = small_g[:, o:o + 6 * nb_up].reshape(N_DEV, 2, 3, nb_up).transpose(1, 2, 0, 3).reshape(2, 3, 2 * FF)

    cmat = jnp.concatenate([c_all, c_ctx.reshape(1, D), jnp.zeros((ADA_ROWS - N_DEV - 1, D), F32)], axis=0)
    ada_b_loc = _my_block(ada_b, 1, n_ada).reshape(2, 1, n_ada)
    mods_loc = _ada_fwd(cmat, ada_w, ada_b_loc)
    mods_g, = _exchange([[mods_loc]], mode="gather", name="gather_mods")
    mods_rows = mods_g.reshape(N_DEV, 2, ADA_ROWS, n_ada).transpose(1, 2, 0, 3).reshape(2, ADA_ROWS, 6, D)
    mod = lax.dynamic_index_in_dim(mods_rows, me, axis=1, keepdims=False)
    mod_c = mods_rows[0, N_DEV]

    def disc_all(lr, li, ls, br, bi):
        return [_discretize(lr[d], li[d], ls[d], br[d], bi[d]) for d in range(2)]

    disc, disc_vjp = jax.vjp(disc_all, s5_lam_re[0], s5_lam_im[0], s5_log_step[0], s5_b_re[0], s5_b_im[0])
    apar, bblk, cblk = _s5_pack([(disc[d][0], disc[d][1]) for d in range(2)],
                                [(disc[d][2], disc[d][3]) for d in range(2)],
                                [(s5_c_re[0, d], s5_c_im[0, d]) for d in range(2)])
    dsk = _row0(s5_d[0], D)
    cw = []
    for i in range(2):
        taps = conv_all[i].reshape(3, 2, FF).transpose(1, 0, 2)
        cw.append(jnp.concatenate([taps, ffn_conv_b[i].reshape(2, 1, FF), jnp.zeros((2, 4, FF), F32)], axis=1))

    vecs = {
        "b0": _vec(D, gpre=ng_all[0, 0], shift=mod[0, 0], scale=mod[0, 1]),
        "c0": _vec(D, gpre=ng_all[0, 0], shift=mod_c[0], scale=mod_c[1]),
        "b1": _vec(D, gpost=ng_all[0, 1], gate=mod[0, 2], gpre=ng_all[0, 2], shift=mod[0, 3], scale=mod[0, 4]),
        "b2": _vec(D, gpost=ng_all[0, 3], gate=mod[0, 5], gpre=ng_all[1, 0], shift=mod[1, 0], scale=mod[1, 1]),
        "b3": _vec(D, gpost=ng_all[1, 1], gate=mod[1, 2], yscale=pscale_all, gpre=ng_all[1, 2], shift=mod[1, 3],
                   scale=mod[1, 4]),
        "b4": _vec(D, gpost=ng_all[1, 3], gate=mod[1, 5]),
    }

    x0, u0 = _rows_fwd("rows_fwd_b0", x[0], _pos_emb(L, D), vecs["b0"], add=True, u_dtype=_ACT)
    uc, = _rows_fwd("rows_fwd_ctx", ctx[0], None, vecs["c0"], want_x=False, u_dtype=_ACT)
    u0s, ucs = _to_segments(u0), _to_segments(uc)
    (y_s5, z_s5), (glu_g, up_g0, down_g0) = _s5_fwd(u0s, ucs, bblk, cblk, apar, dsk,
                                                    rider=([[glu_b], [up_b[0]], [down_b[0]]], "gather2"))
    vg = _colblock_fwd("glu_fwd_mm", z_s5, glu_g, 0, _ACT)
    mix0 = _from_segments(_glu_fwd("glu_fwd", vg))
    x1, un0 = _rows_fwd("rows_fwd_b1", x0, mix0, vecs["b1"], u_dtype=_ACT)
    h0, (up_g1,) = _colblock_fwd("ffn0_up", un0, up_g0, 0, _ACT, rider=([[up_b[1]]], "gather2"))
    act0 = _conv_swiglu_fwd("ffn0_conv", h0, cw[0])
    f0, (down_g1, pool_g) = _rowblock_fwd("ffn0_down", act0, down_g0, 0, rider=([[down_b[1]], [pool_b]], "gather2"))
    pool_full = pool_g.reshape(N_DEV, n_pool, pr, pc).transpose(1, 0, 2, 3).reshape(n_pool, pc, pc)
    x2, u1 = _rows_fwd("rows_fwd_b2", x1, f0, vecs["b2"], u_dtype=F32)
    p1 = _pool_window("pool_fwd", u1, False, _ACT)
    ypre1 = _group_mm("pool_fwd_mm", p1, pool_full, "nn", F32)
    x3, un1 = _rows_fwd("rows_fwd_b3", x2, ypre1, vecs["b3"], u_dtype=_ACT)
    h1 = _colblock_fwd("ffn1_up", un1, up_g1, 0, _ACT)
    act1 = _conv_swiglu_fwd("ffn1_conv", h1, cw[1])
    f1 = _rowblock_fwd("ffn1_down", act1, down_g1, 0)
    dx4, loss_blk, df1, red4 = _rows_fwd("rows_fwd_b4", x3, f1, vecs["b4"], target=loss_target[0])
    loss = lax.psum(loss_blk[0, 0], axes)

    dact1 = _rowblock_dgrad("ffn1_down_dgrad", df1, down_g1, 0)
    ddown1 = _rowblock_wgrad("ffn1_down_wgrad", act1, df1)
    ddown1 = ddown1.reshape(N_DEV, r_down, D)
    (dh1, dcw1), (gp_down1a,) = _conv_swiglu_bwd("ffn1_conv_bwd", h1, cw[1], dact1,
                                                rider=([[(ddown1, (0, r_down // 2))]], "scatter"))
    dun1, (gp_down1b,) = _colblock_dgrad("ffn1_up_dgrad", dh1, up_g1, 0, F32,
                                         rider=([[(ddown1, (r_down // 2, r_down // 2))]], "scatter"))
    dup1 = _colblock_wgrad("ffn1_up_wgrad", un1, dh1)
    dx3, dypre1, red3 = _rows_bwd("rows_bwd_b3", dx4, dun1, x3, ypre1, vecs["b3"], dy_dtype=_ACT, yscale_grad=True)
    dp1 = _group_mm("pool_dgrad", dypre1, pool_full, "nt", F32)
    dpool = _group_wgrad("pool_wgrad", p1, dypre1, n_pool)
    du1 = _pool_window("pool_bwd", dp1, True, F32)
    dx2, df0, red2 = _rows_bwd("rows_bwd_b2", dx3, du1, x2, f0, vecs["b2"], dy_dtype=_ACT)
    dact0, (gp_up1c,) = _rowblock_dgrad("ffn0_down_dgrad", df0, down_g0, 0,
                                        rider=([[(dup1, (3 * D // 4, D // 4))]], "scatter"))
    ddown0 = _rowblock_wgrad("ffn0_down_wgrad", act0, df0)
    ddown0 = ddown0.reshape(N_DEV, r_down, D)
    (dh0, dcw0), (gp_down0a,) = _conv_swiglu_bwd("ffn0_conv_bwd", h0, cw[0], dact0,
                                                 rider=([[(ddown0, (0, r_down // 2))]], "scatter"))
    dun0, (gp_up1a,) = _colblock_dgrad("ffn0_up_dgrad", dh0, up_g0, 0, F32,
                                       rider=([[(dup1, (0, D // 2))]], "scatter"))
    dup0, (gp_up1b,) = _colblock_wgrad("ffn0_up_wgrad", un0, dh0, rider=([[(dup1, (D // 2, D // 4))]], "scatter"))
    dx1, dmix0, red1 = _rows_bwd("rows_bwd_b1", dx2, dun0, x1, mix0, vecs["b1"])
    dvg = _glu_bwd("glu_bwd", vg, _to_segments(dmix0))
    dz = _colblock_dgrad("glu_dgrad", dvg, glu_g, 0, _ACT)
    dglu = _colblock_wgrad("glu_wgrad", z_s5, dvg)
    dpool_blocks = dpool.reshape(n_pool, N_DEV, pr, pc).transpose(1, 0, 2, 3).reshape(N_DEV, n_pool * pr, pc)
    (du0s, ducs, dbblk, dcblk, dapar, ddsk), (gp_up0, gp_glu, gp_pool, gp_down0b) = _s5_bwd(
        u0s, ucs, dz, y_s5, bblk, cblk, apar, dsk,
        rider=([[dup0], [dglu], [dpool_blocks], [(ddown0, (r_down // 2, r_down // 2))]], "scatter"))
    grad_x, red0 = _rows_bwd("rows_bwd_b0", dx1, _from_segments(du0s), x0, None, vecs["b0"])
    redc, = _rows_bwd("rows_bwd_ctx", None, _from_segments(ducs), ctx[0], None, vecs["c0"], want_dx=False)

    zero_d = jnp.zeros((D,), F32)
    dmod = jnp.stack([
        jnp.stack([red0[R_SHIFT], red0[R_SCALE], red1[R_GATE], red1[R_SHIFT], red1[R_SCALE], red2[R_GATE]]),
        jnp.stack([red2[R_SHIFT], red2[R_SCALE], red3[R_GATE], red3[R_SHIFT], red3[R_SCALE], red4[R_GATE]])])
    dmod_c = jnp.stack([jnp.stack([redc[R_SHIFT], redc[R_SCALE]] + [zero_d] * 4), jnp.zeros((6, D), F32)])
    dm_g, = _exchange([[jnp.stack([dmod, dmod_c], axis=1).reshape(2, 2, 6 * D)]], mode="gather", name="gather_dmods")
    dm_g = dm_g.reshape(N_DEV, 2, 2, 6 * D)
    dm_ctx = _sum_parts("sum_dmod_ctx", dm_g[:, :, 1, :])
    dm_rows = jnp.concatenate([dm_g[:, :, 0, :].transpose(1, 0, 2), dm_ctx[:, None, :]], axis=1)
    grad_ada_b = _sum_parts("sum_ada_b", dm_rows.transpose(1, 0, 2))
    dm_cols = dm_rows.reshape(2, N_DEV + 1, N_DEV, n_ada)
    dm_mine = lax.dynamic_index_in_dim(dm_cols, me, axis=2, keepdims=False)
    dm_mine = jnp.concatenate([dm_mine, jnp.zeros((2, ADA_ROWS - N_DEV - 1, n_ada), F32)], axis=1)
    grad_ada_w, dcond = _ada_bwd(cmat, ada_w, dm_mine)
    dcctx_part = dcond[0, N_DEV] + dcond[1, N_DEV]

    da, db, dc = _s5_unpack(dapar, dbblk, dcblk, G)
    dnorm = jnp.stack([
        jnp.stack([red0[R_GPRE] + redc[R_GPRE], red1[R_GPOST], red1[R_GPRE], red2[R_GPOST]]),
        jnp.stack([red2[R_GPRE], red3[R_GPOST], red3[R_GPRE], red4[R_GPOST]])])
    dconv = jnp.stack([d[:, :3, :].transpose(1, 0, 2).reshape(3, 2 * FF) for d in (dcw0, dcw1)])
    dconv_b = jnp.stack([d[:, 3, :].reshape(2 * FF) for d in (dcw0, dcw1)])
    pieces = [dcctx_part, dnorm, da, db, dc, ddsk[:, 0, :], red3[R_YSCALE], dconv, dconv_b]
    flat = jnp.concatenate([p.reshape(-1) for p in pieces])
    n_flat = flat.shape[0]
    per_dev = -(-n_flat // (N_DEV * 8 * LANE)) * 8 * LANE
    flat = jnp.pad(flat, (0, N_DEV * per_dev - n_flat)).reshape(N_DEV, per_dev // LANE, LANE)
    parts, = _exchange([[flat]], mode="scatter", name="scatter_small_grads")
    mine = _sum_parts("sum_small_grads", parts.reshape(N_DEV, per_dev // LANE, LANE))
    summed, = _exchange([[mine]], mode="gather", name="gather_small_grads")
    summed = summed.reshape(-1)
    red_pieces, o = [], 0
    for p in pieces:
        red_pieces.append(summed[o:o + p.size].reshape(p.shape))
        o += p.size
    g_cctx, g_norm, g_a, g_b, g_c, g_d, g_pscale, g_conv, g_conv_b = red_pieces
    cot = [(g_a[d, 0], g_a[d, 1], g_b[d, 0], g_b[d, 1]) for d in range(2)]
    g_lam_re, g_lam_im, g_log_step, g_b_re, g_b_im = disc_vjp(cot)

    out = {}

    def put(name, res, shape):
        out[name] = tuple(r.reshape(shape) for r in res)

    gp_up0, gp_up1a = gp_up0.reshape(N_DEV, D, nb_up), gp_up1a.reshape(N_DEV, D // 2, nb_up)
    quarters = (8, D // 4, nb_up)
    put("ffn_up", _adamw("adamw_ffn_up",
                         [(gp_up0, q * D // 4) for q in range(4)] + [(gp_up1a, 0), (gp_up1a, D // 4),
                                                                     gp_up1b.reshape(N_DEV, D // 4, nb_up),
                                                                     gp_up1c.reshape(N_DEV, D // 4, nb_up)],
                         ffn_up.reshape(quarters), m_ffn_up.reshape(quarters), v_ffn_up.reshape(quarters)),
        ffn_up.shape)
    halves = (4, r_down // 2, D)
    put("ffn_down", _adamw("adamw_ffn_down",
                           [g.reshape(N_DEV, r_down // 2, D) for g in (gp_down0a, gp_down0b, gp_down1a, gp_down1b)],
                           ffn_down.reshape(halves), m_ffn_down.reshape(halves), v_ffn_down.reshape(halves)),
        ffn_down.shape)
    put("s5_glu_w", _adamw("adamw_glu", [gp_glu.reshape(N_DEV, D, -1)], s5_glu_w, m_s5_glu_w, v_s5_glu_w),
        s5_glu_w.shape)
    pool_rows = (1, n_pool * pr, pc)
    put("pool_w", _adamw("adamw_pool", [gp_pool.reshape(N_DEV, n_pool * pr, pc)], pool_w.reshape(pool_rows),
                         m_pool_w.reshape(pool_rows), v_pool_w.reshape(pool_rows)), pool_w.shape)
    put("ada_w", _adamw("adamw_ada_w", [grad_ada_w[i][None] for i in range(2)], ada_w, m_ada_w, v_ada_w), ada_w.shape)

    for nm, w, m, v, g in (("s5_b_re", s5_b_re, m_s5_b_re, v_s5_b_re, g_b_re),
                           ("s5_b_im", s5_b_im, m_s5_b_im, v_s5_b_im, g_b_im),
                           ("s5_c_re", s5_c_re, m_s5_c_re, v_s5_c_re, g_c[:, 0]),
                           ("s5_c_im", s5_c_im, m_s5_c_im, v_s5_c_im, g_c[:, 1])):
        rows = (1, w.size // w.shape[-1], w.shape[-1])
        put(nm, _adamw("adamw_" + nm, [g.reshape(rows)], w.reshape(rows), m.reshape(rows), v.reshape(rows)), w.shape)

    small = [
        ("c_ctx", c_ctx, m_c_ctx, v_c_ctx, g_cctx),
        ("ada_b", ada_b, m_ada_b, v_ada_b, grad_ada_b),
        ("norm_g", norm_g, m_norm_g, v_norm_g, _my_block(g_norm, 2, ng_loc)),
        ("s5_lam_re", s5_lam_re, m_s5_lam_re, v_s5_lam_re, g_lam_re),
        ("s5_lam_im", s5_lam_im, m_s5_lam_im, v_s5_lam_im, g_lam_im),
        ("s5_log_step", s5_log_step, m_s5_log_step, v_s5_log_step, g_log_step),
        ("s5_d", s5_d, m_s5_d, v_s5_d, g_d),
        ("pool_scale", pool_scale, m_pool_scale, v_pool_scale, _my_block(g_pscale, 0, ng_loc)),
        ("ffn_conv", ffn_conv, m_ffn_conv, v_ffn_conv, _my_block(g_conv, 2, nb_up)),
        ("ffn_conv_b", ffn_conv_b, m_ffn_conv_b, v_ffn_conv_b, g_conv_b),
    ]
    n_sm = sum(w.size for _, w, _, _, _ in small)
    rows_sm = -(-n_sm // (512 * LANE)) * 512

    def flat_of(k):
        f = jnp.concatenate([t[k].reshape(-1) for t in small])
        return jnp.pad(f, (0, rows_sm * LANE - n_sm)).reshape(rows_sm, LANE)

    res_sm = _adamw("adamw_small", [flat_of(4)[None]], flat_of(1)[None], flat_of(2)[None], flat_of(3)[None])
    o = 0
    for name, w, _, _, _ in small:
        out[name] = tuple(r.reshape(-1)[o:o + w.size].reshape(w.shape) for r in res_sm)
        o += w.size

    order = ["c_ctx", "ada_w", "ada_b", "norm_g", "s5_lam_re", "s5_lam_im", "s5_log_step", "s5_b_re", "s5_b_im",
             "s5_c_re", "s5_c_im", "s5_d", "s5_glu_w", "pool_w", "pool_scale", "ffn_up", "ffn_conv", "ffn_conv_b",
             "ffn_down"]
    return (loss, grad_x.reshape(x.shape), *[out[n][0] for n in order], *[out[n][1] for n in order],
            *[out[n][2] for n in order], *[out[n][3] for n in order])
```

```python
import functools
import math

import jax
import jax.numpy as jnp
from jax import lax
from jax.experimental import pallas as pl
from jax.experimental.pallas import tpu as pltpu

F32 = jnp.float32
_ACT = jnp.bfloat16
N_DEV = 8
NSEG = 8
S5_CH = 16
S5_P = 64
LANE = 128
S5_TILE_CH = LANE
S5_TILE_G = S5_TILE_CH // S5_CH
S5_TILE_W = S5_TILE_G * S5_P
GRID_W = 64
POOL_WINDOWS = (2, 4, 8, 16)
POOL_HALO = 64
RMS_EPS = 1e-6
POS_BASE = 10000.0
ADAM_LR, ADAM_B1, ADAM_B2, ADAM_EPS, ADAM_WD, ADAM_STEP = 0.001, 0.9, 0.999, 1e-08, 0.01, 10
VMEM_LIMIT = 48 * 1024 * 1024
VMEM_LIMIT_BIG = 58 * 1024 * 1024
MESH = pl.DeviceIdType.MESH
ANY = pl.BlockSpec(memory_space=pl.ANY)


def _params(sem, vmem=VMEM_LIMIT):
    return pltpu.CompilerParams(dimension_semantics=sem, vmem_limit_bytes=vmem)


def _my_index():
    return 4 * lax.axis_index("x") + 2 * lax.axis_index("y") + lax.axis_index("c")


def _xchg_plan(groups, mode):
    flat = [(g, l, a) for g, grp in enumerate(groups) for l, a in enumerate(grp)]
    outs = []
    for grp in groups:
        a, rows = _rows_of(grp[0])
        piece = a.shape[1:] if mode == "scatter" else a.shape
        if rows is not None:
            piece = (rows[1],) + tuple(piece[1:])
        outs.append(jax.ShapeDtypeStruct((N_DEV, len(grp)) + tuple(piece), a.dtype))
    return flat, outs


def _rows_of(entry):
    return entry if isinstance(entry, tuple) else (entry, None)


def _operands(flat):
    return [_rows_of(a)[0] for _, _, a in flat]


def _xchg_sems(n):
    return [pltpu.SemaphoreType.DMA((n, N_DEV - 1)), pltpu.SemaphoreType.DMA((n, N_DEV - 1)),
            pltpu.SemaphoreType.DMA((n,))]


def _xchg_copies(flat, mode, ins, out_refs, sems, waiting=True):
    send_sems, recv_sems, local_sems = sems
    x, y, c = lax.axis_index("x"), lax.axis_index("y"), lax.axis_index("c")
    me = 4 * x + 2 * y + c
    local, first, forwards = [], [], []

    def pair(s, j, dev):
        return dict(send_sem=send_sems.at[s, j], recv_sem=recv_sems.at[s, j], device_id=dev, device_id_type=MESH)

    def block(s, dev):
        rows = _rows_of(flat[s][2])[1]
        ref = ins[s].at[dev]
        return ref if rows is None else ref.at[pl.ds(rows[0], rows[1])]

    for s, (g, l, _) in enumerate(flat):
        src = block(s, me) if mode == "scatter" else ins[s]
        local.append(pltpu.make_async_copy(src, out_refs[g].at[me, l], local_sems.at[s]))
    if mode == "gather2":
        sib, sib_idx = (x, y, 1 - c), 4 * x + 2 * y + (1 - c)
        for s, (g, l, _) in enumerate(flat):
            slot = lambda dev, g=g, l=l: out_refs[g].at[dev, l]
            targets = [(sib, sib_idx)] + [((qx, qy, c), 4 * qx + 2 * qy + c)
                                          for qx, qy in ((1 - x, y), (x, 1 - y), (1 - x, 1 - y))]
            for j, (dev, idx) in enumerate(targets):
                send = pltpu.make_async_remote_copy(src_ref=ins[s], dst_ref=slot(me), **pair(s, j, dev))
                arrive = pltpu.make_async_remote_copy(src_ref=ins[s], dst_ref=slot(idx), **pair(s, j, dev)) if waiting else None
                first.append((send, arrive))
            if waiting:
                for j, (dev, idx) in enumerate(targets[1:]):
                    other = 4 * dev[0] + 2 * dev[1] + (1 - c)
                    send = pltpu.make_async_remote_copy(src_ref=slot(idx), dst_ref=slot(idx), **pair(s, 4 + j, sib))
                    arrive = pltpu.make_async_remote_copy(src_ref=slot(idx), dst_ref=slot(other), **pair(s, 4 + j, sib))
                    forwards.append((first[len(first) - 3 + j][1], send, arrive))
        return local, first, forwards
    for k in range(1, N_DEV):
        px = 1 - x if k & 4 else x
        py = 1 - y if k & 2 else y
        pc = 1 - c if k & 1 else c
        peer = 4 * px + 2 * py + pc
        for s, (g, l, _) in enumerate(flat):
            src = block(s, peer) if mode == "scatter" else ins[s]
            send = pltpu.make_async_remote_copy(src_ref=src, dst_ref=out_refs[g].at[me, l], **pair(s, k - 1, (px, py, pc)))
            arrive = (pltpu.make_async_remote_copy(src_ref=src, dst_ref=out_refs[g].at[peer, l],
                                                   **pair(s, k - 1, (px, py, pc))) if waiting else None)
            first.append((send, arrive))
    return local, first, forwards


def _xchg_start(local, first, forwards):
    for cp in local:
        cp.start()
    for send, _ in first:
        send.start()


def _xchg_wait(local, first, forwards):
    gates = [gate for gate, _, _ in forwards]
    for gate, send, _ in forwards:
        gate.wait_recv()
        send.start()
    for _, arrive in first:
        if not any(arrive is gate for gate in gates):
            arrive.wait_recv()
    for _, _, arrive in forwards:
        arrive.wait_recv()
    for send, _ in first:
        send.wait_send()
    for _, send, _ in forwards:
        send.wait_send()
    for cp in local:
        cp.wait()


def _exchange(groups, mode, name):
    flat, outs = _xchg_plan(groups, mode)
    n = len(flat)

    def body(*refs):
        copies = _xchg_copies(flat, mode, refs[:n], refs[n:n + len(groups)], refs[n + len(groups):])
        _xchg_start(*copies)
        _xchg_wait(*copies)

    res = pl.pallas_call(body, name=name, out_shape=outs, in_specs=[ANY] * n, out_specs=[ANY] * len(groups),
                         scratch_shapes=_xchg_sems(n))(*_operands(flat))
    return list(res)


def _pcall(body, *, name, grid, in_specs, out_specs, out_shape, ins, scratch_shapes=(), sem=None, vmem=VMEM_LIMIT,
           rider=None):
    single = not isinstance(out_shape, (list, tuple))
    if rider is None:
        return pl.pallas_call(body, name=name, grid=grid, in_specs=list(in_specs), out_specs=out_specs,
                              out_shape=out_shape, scratch_shapes=list(scratch_shapes),
                              compiler_params=_params(sem, vmem))(*ins)
    groups, mode = rider
    flat, r_outs = _xchg_plan(groups, mode)
    n_in, n_out = len(ins), 1 if single else len(out_shape)
    nr, ng, ns = len(flat), len(groups), len(scratch_shapes)

    def wrapped(*refs):
        o1 = n_in + nr
        o2 = o1 + n_out
        o3 = o2 + ng
        r_in, r_out, sems = refs[n_in:o1], refs[o2:o3], refs[o3 + ns:]
        first = functools.reduce(jnp.logical_and, [pl.program_id(d) == 0 for d in range(len(grid))])
        last = functools.reduce(jnp.logical_and, [pl.program_id(d) == grid[d] - 1 for d in range(len(grid))])

        @pl.when(first)
        def _():
            _xchg_start(*_xchg_copies(flat, mode, r_in, r_out, sems, waiting=False))

        body(*refs[:n_in], *refs[o1:o2], *refs[o3:o3 + ns])

        @pl.when(last)
        def _():
            _xchg_wait(*_xchg_copies(flat, mode, r_in, r_out, sems))

    outs = pl.pallas_call(
        wrapped, name=name, grid=grid, in_specs=list(in_specs) + [ANY] * nr,
        out_specs=([out_specs] if single else list(out_specs)) + [ANY] * ng,
        out_shape=([out_shape] if single else list(out_shape)) + r_outs,
        scratch_shapes=list(scratch_shapes) + _xchg_sems(nr),
        compiler_params=_params(("arbitrary",) * len(grid), vmem))(*ins, *_operands(flat))
    base = list(outs[:n_out])
    return (base[0] if single else base), list(outs[n_out:])


_DIMS = {"nn": (((1,), (0,)), ((), ())), "nt": (((1,), (1,)), ((), ())), "tn": (((0,), (0,)), ((), ()))}


def _mm(name, a, b, a_spec, b_spec, o_spec, out_shape, grid, dims, rider=None):
    nk = grid[2]
    acc_shape = tuple(d for d in o_spec.block_shape if d is not None)
    dn = _DIMS[dims]

    def tile(ref):
        v = ref[...]
        return v.reshape((-1, v.shape[-1])).astype(_ACT)

    def body(a_ref, b_ref, o_ref, *scratch):
        def part():
            return lax.dot_general(tile(a_ref), tile(b_ref), dn, preferred_element_type=F32)

        if nk == 1:
            o_ref[...] = part().reshape(o_ref.shape).astype(o_ref.dtype)
            return
        acc_ref, = scratch
        k = pl.program_id(2)

        @pl.when(k == 0)
        def _():
            acc_ref[...] = part()

        @pl.when(k > 0)
        def _():
            acc_ref[...] += part()

        @pl.when(k == nk - 1)
        def _():
            o_ref[...] = acc_ref[...].reshape(o_ref.shape).astype(o_ref.dtype)

    acc2d = (math.prod(acc_shape[:-1]), acc_shape[-1])
    return _pcall(body, name=name, out_shape=out_shape, grid=grid, in_specs=[a_spec, b_spec], out_specs=o_spec,
                  scratch_shapes=[] if nk == 1 else [pltpu.VMEM(acc2d, F32)], ins=(a, b),
                  sem=("parallel", "parallel", "arbitrary"), rider=rider)


def _row_tile(n, want):
    t = min(n, want)
    assert n % t == 0, (n, t)
    return t


def _colblock_fwd(name, xa, wg, layer, out_dtype, rider=None):
    L, K = xa.shape
    nb = wg.shape[3]
    half = N_DEV // 2
    tm = _row_tile(L, 512)
    return _mm(name, xa, wg,
               pl.BlockSpec((tm, K), lambda j, i, k: (i, 0)),
               pl.BlockSpec((None, None, K, nb), lambda j, i, k: (j, layer, 0, 0)),
               pl.BlockSpec((None, tm, nb), lambda j, i, k: (j // half, i, j % half)),
               jax.ShapeDtypeStruct((2, L, half * nb), out_dtype), (N_DEV, L // tm, 1), "nn", rider=rider)


def _colblock_dgrad(name, dh, wg, layer, out_dtype, rider=None):
    _, L, _ = dh.shape
    K, nb = wg.shape[2], wg.shape[3]
    half = N_DEV // 2
    tm = _row_tile(L, 512)
    return _mm(name, dh, wg,
               pl.BlockSpec((None, tm, nb), lambda i, j, k: (k // half, i, k % half)),
               pl.BlockSpec((None, None, K, nb), lambda i, j, k: (k, layer, 0, 0)),
               pl.BlockSpec((tm, K), lambda i, j, k: (i, 0)),
               jax.ShapeDtypeStruct((L, K), out_dtype), (L // tm, 1, N_DEV), "nt", rider=rider)


def _colblock_wgrad(name, xa, dh, rider=None):
    L, K = xa.shape
    half = N_DEV // 2
    nb = dh.shape[2] // half
    tm = _row_tile(K, 512)
    tk = L
    return _mm(name, xa, dh,
               pl.BlockSpec((tk, tm), lambda j, i, k: (k, i)),
               pl.BlockSpec((None, tk, nb), lambda j, i, k: (j // half, k, j % half)),
               pl.BlockSpec((None, tm, nb), lambda j, i, k: (j, i, 0)),
               jax.ShapeDtypeStruct((N_DEV, K, nb), _ACT), (N_DEV, K // tm, L // tk), "tn", rider=rider)


def _rowblock_fwd(name, xa, wg, layer, rider=None):
    L, FF = xa.shape
    r, D = wg.shape[2], wg.shape[3]
    tm = _row_tile(L, 512)
    return _mm(name, xa, wg,
               pl.BlockSpec((tm, 2 * r), lambda i, j, k: (i, k)),
               pl.BlockSpec((2, None, r, D), lambda i, j, k: (k, layer, 0, 0)),
               pl.BlockSpec((tm, D), lambda i, j, k: (i, 0)),
               jax.ShapeDtypeStruct((L, D), F32), (L // tm, 1, N_DEV // 2), "nn", rider=rider)


def _rowblock_dgrad(name, dy, wg, layer, rider=None):
    L, D = dy.shape
    r = wg.shape[2]
    tm = _row_tile(L, 512)
    return _mm(name, dy, wg,
               pl.BlockSpec((tm, D), lambda i, j, k: (i, 0)),
               pl.BlockSpec((2, None, r, D), lambda i, j, k: (j, layer, 0, 0)),
               pl.BlockSpec((tm, 2 * r), lambda i, j, k: (i, j)),
               jax.ShapeDtypeStruct((L, N_DEV * r), _ACT), (L // tm, N_DEV // 2, 1), "nt", rider=rider)


def _rowblock_wgrad(name, xa, dy, rider=None):
    L, FF = xa.shape
    D = dy.shape[1]
    tm = FF // (N_DEV // 2)
    tn = _row_tile(D, 1024)
    tk = _row_tile(L, 2048)
    return _mm(name, xa, dy,
               pl.BlockSpec((tk, tm), lambda i, j, k: (k, i)),
               pl.BlockSpec((tk, tn), lambda i, j, k: (k, j)),
               pl.BlockSpec((tm, tn), lambda i, j, k: (i, j)),
               jax.ShapeDtypeStruct((FF, D), _ACT), (FF // tm, D // tn, L // tk), "tn", rider=rider)


def _group_mm(name, xa, w, dims, out_dtype):
    L, D = xa.shape
    ng, pc, _ = w.shape
    tm = _row_tile(L, 512)
    return _mm(name, xa, w,
               pl.BlockSpec((tm, pc), lambda i, g, k: (i, g)),
               pl.BlockSpec((None, pc, pc), lambda i, g, k: (g, 0, 0)),
               pl.BlockSpec((tm, pc), lambda i, g, k: (i, g)),
               jax.ShapeDtypeStruct((L, D), out_dtype), (L // tm, ng, 1), dims)


def _group_wgrad(name, p, dy, ng):
    L, D = p.shape
    pc = D // ng
    tk = _row_tile(L, 512)
    return _mm(name, p, dy,
               pl.BlockSpec((tk, pc), lambda g, j, k: (k, g)),
               pl.BlockSpec((tk, pc), lambda g, j, k: (k, g)),
               pl.BlockSpec((None, pc, pc), lambda g, j, k: (g, 0, 0)),
               jax.ShapeDtypeStruct((ng, pc, pc), _ACT), (ng, 1, L // tk), "tn")


V_GPOST, V_GATE, V_YSCALE, V_GPRE, V_SHIFT, V_SCALE = range(6)
R_SHIFT, R_SCALE, R_GPRE, R_GATE, R_GPOST, R_YSCALE = range(6)
ROW_TILE = 256


def _rstd(v):
    return lax.rsqrt(jnp.mean(v * v, axis=-1, keepdims=True) + RMS_EPS)


def _post_norm_bwd(dyh, yh, ry):
    return ry * (dyh - yh * jnp.mean(dyh * yh, axis=-1, keepdims=True))


def _rows_fwd(name, xres, y, vec, *, add=False, target=None, want_x=True, u_dtype=None):
    L, D = xres.shape
    tm = _row_tile(L, ROW_TILE)
    has_y = y is not None
    last = target is not None
    has_u = u_dtype is not None

    def body(*refs):
        refs = list(refs)
        xres_ref = refs.pop(0)
        y_ref = refs.pop(0) if has_y else None
        vec_ref = refs.pop(0)
        tgt_ref = refs.pop(0) if last else None
        xnew = xres_ref[...]
        if has_y and add:
            xnew = xnew + y_ref[...]
        elif has_y:
            ye = y_ref[...] * vec_ref[V_YSCALE:V_YSCALE + 1, :]
            ry = _rstd(ye)
            yh = ye * ry
            gpost, gate = vec_ref[V_GPOST:V_GPOST + 1, :], vec_ref[V_GATE:V_GATE + 1, :]
            xnew = xnew + gate * (yh * gpost)
        if last:
            dx_ref, loss_ref, dy_ref, red_ref = refs
            diff = xnew - tgt_ref[...]
            dxn = diff * (1.0 / D)
            dx_ref[...] = dxn

            @pl.when(pl.program_id(0) == 0)
            def _():
                loss_ref[...] = jnp.zeros_like(loss_ref)
                red_ref[...] = jnp.zeros_like(red_ref)

            loss_ref[...] += jnp.sum(diff * diff) * (0.5 / D)
            drn2 = dxn * gate
            red_ref[R_GATE:R_GATE + 1, :] += jnp.sum(dxn * (yh * gpost), axis=0, keepdims=True)
            red_ref[R_GPOST:R_GPOST + 1, :] += jnp.sum(drn2 * yh, axis=0, keepdims=True)
            dye = _post_norm_bwd(drn2 * gpost, yh, ry)
            red_ref[R_YSCALE:R_YSCALE + 1, :] += jnp.sum(dye * y_ref[...], axis=0, keepdims=True)
            dy_ref[...] = (dye * vec_ref[V_YSCALE:V_YSCALE + 1, :]).astype(dy_ref.dtype)
            return
        if want_x:
            refs.pop(0)[...] = xnew
        if has_u:
            u_ref, = refs
            n = xnew * _rstd(xnew) * vec_ref[V_GPRE:V_GPRE + 1, :]
            u_ref[...] = (n * (1.0 + vec_ref[V_SCALE:V_SCALE + 1, :]) + vec_ref[V_SHIFT:V_SHIFT + 1, :]).astype(u_ref.dtype)

    row = pl.BlockSpec((tm, D), lambda i: (i, 0))
    vspec = pl.BlockSpec((8, D), lambda i: (0, 0))
    ins, in_specs = [xres], [row]
    if has_y:
        ins.append(y)
        in_specs.append(row)
    ins.append(vec)
    in_specs.append(vspec)
    out_shape, out_specs = [], []
    if last:
        ins.append(target)
        in_specs.append(row)
        out_shape = [jax.ShapeDtypeStruct((L, D), F32), jax.ShapeDtypeStruct((8, LANE), F32),
                     jax.ShapeDtypeStruct((L, D), _ACT), jax.ShapeDtypeStruct((8, D), F32)]
        out_specs = [row, pl.BlockSpec((8, LANE), lambda i: (0, 0)), row, vspec]
    else:
        if want_x:
            out_shape.append(jax.ShapeDtypeStruct((L, D), F32))
            out_specs.append(row)
        if has_u:
            out_shape.append(jax.ShapeDtypeStruct((L, D), u_dtype))
            out_specs.append(row)
    return pl.pallas_call(body, name=name, out_shape=out_shape, grid=(L // tm,), in_specs=in_specs,
                          out_specs=out_specs, compiler_params=_params(("arbitrary",)))(*ins)


def _rows_bwd(name, dxd, du, xnew, y, vec, dy_dtype=F32, want_dx=True, yscale_grad=False):
    L, D = xnew.shape if xnew is not None else dxd.shape
    tm = _row_tile(L, ROW_TILE)
    has_dxd, has_pre, has_post = dxd is not None, du is not None, y is not None

    def body(*refs):
        refs = list(refs)
        dxd_ref = refs.pop(0) if has_dxd else None
        du_ref = refs.pop(0) if has_pre else None
        xnew_ref = refs.pop(0) if has_pre else None
        y_ref = refs.pop(0) if has_post else None
        vec_ref = refs.pop(0)
        dx_ref = refs.pop(0) if want_dx else None
        dy_ref = refs.pop(0) if has_post else None
        red_ref, = refs

        @pl.when(pl.program_id(0) == 0)
        def _():
            red_ref[...] = jnp.zeros_like(red_ref)

        def acc(rw, val):
            red_ref[rw:rw + 1, :] += jnp.sum(val, axis=0, keepdims=True)

        dxn = dxd_ref[...] if has_dxd else None
        if has_pre:
            xn = xnew_ref[...]
            r = _rstd(xn)
            nh = xn * r
            gpre = vec_ref[V_GPRE:V_GPRE + 1, :]
            dub = du_ref[...].astype(F32)
            acc(R_SHIFT, dub)
            acc(R_SCALE, dub * (nh * gpre))
            drn = dub * (1.0 + vec_ref[V_SCALE:V_SCALE + 1, :])
            acc(R_GPRE, drn * nh)
            dnh = drn * gpre
            t = r * (dnh - nh * jnp.mean(dnh * nh, axis=-1, keepdims=True))
            dxn = t if dxn is None else dxn + t
        if want_dx:
            dx_ref[...] = dxn
        if has_post:
            ye = y_ref[...] * vec_ref[V_YSCALE:V_YSCALE + 1, :]
            ry = _rstd(ye)
            yh = ye * ry
            gpost = vec_ref[V_GPOST:V_GPOST + 1, :]
            acc(R_GATE, dxn * (yh * gpost))
            drn2 = dxn * vec_ref[V_GATE:V_GATE + 1, :]
            acc(R_GPOST, drn2 * yh)
            dye = _post_norm_bwd(drn2 * gpost, yh, ry)
            if yscale_grad:
                acc(R_YSCALE, dye * y_ref[...])
            dy_ref[...] = (dye * vec_ref[V_YSCALE:V_YSCALE + 1, :]).astype(dy_ref.dtype)

    row = pl.BlockSpec((tm, D), lambda i: (i, 0))
    vspec = pl.BlockSpec((8, D), lambda i: (0, 0))
    ins, in_specs = [], []
    for a in ([dxd] if has_dxd else []) + ([du, xnew] if has_pre else []) + ([y] if has_post else []):
        ins.append(a)
        in_specs.append(row)
    ins.append(vec)
    in_specs.append(vspec)
    out_shape, out_specs = [], []
    if want_dx:
        out_shape.append(jax.ShapeDtypeStruct((L, D), F32))
        out_specs.append(row)
    if has_post:
        out_shape.append(jax.ShapeDtypeStruct((L, D), dy_dtype))
        out_specs.append(row)
    out_shape.append(jax.ShapeDtypeStruct((8, D), F32))
    out_specs.append(vspec)
    return pl.pallas_call(body, name=name, out_shape=out_shape, grid=(L // tm,), in_specs=in_specs,
                          out_specs=out_specs, compiler_params=_params(("arbitrary",)))(*ins)


def _sigmoid(v):
    return 1.0 / (1.0 + jnp.exp(-v))


def _glu_fwd(name, vg):
    _, L, D = vg.shape
    tm = _row_tile(L, ROW_TILE)

    def body(vg_ref, o_ref):
        o_ref[...] = vg_ref[0].astype(F32) * _sigmoid(vg_ref[1].astype(F32))

    return pl.pallas_call(body, name=name, grid=(L // tm,),
                          in_specs=[pl.BlockSpec((2, tm, D), lambda i: (0, i, 0))],
                          out_specs=pl.BlockSpec((tm, D), lambda i: (i, 0)),
                          out_shape=jax.ShapeDtypeStruct((L, D), F32),
                          compiler_params=_params(("parallel",)))(vg)


def _glu_bwd(name, vg, dout):
    _, L, D = vg.shape
    tm = _row_tile(L, ROW_TILE)

    def body(vg_ref, d_ref, o_ref):
        val, s = vg_ref[0].astype(F32), _sigmoid(vg_ref[1].astype(F32))
        d = d_ref[...]
        o_ref[0] = (d * s).astype(o_ref.dtype)
        o_ref[1] = (d * val * s * (1.0 - s)).astype(o_ref.dtype)

    return pl.pallas_call(body, name=name, grid=(L // tm,),
                          in_specs=[pl.BlockSpec((2, tm, D), lambda i: (0, i, 0)), pl.BlockSpec((tm, D), lambda i: (i, 0))],
                          out_specs=pl.BlockSpec((2, tm, D), lambda i: (0, i, 0)),
                          out_shape=jax.ShapeDtypeStruct((2, L, D), _ACT),
                          compiler_params=_params(("parallel",)))(vg, dout)


CONV_ROWS = 512


def _row_pick(blk, idx):
    rows = lax.broadcasted_iota(jnp.int32, blk.shape, 0)
    return jnp.sum(jnp.where(rows == idx, blk, 0.0), axis=0, keepdims=True)


def _shifted(ref, r0, rc, L):
    cur = ref[pl.ds(r0, rc), :].astype(F32)
    before = ref[pl.ds(pl.multiple_of(jnp.maximum(r0 - 16, 0), 16), 16), :].astype(F32)
    after = ref[pl.ds(pl.multiple_of(jnp.minimum(r0 + rc, L - 16), 16), 16), :].astype(F32)
    prev_row = jnp.where(r0 > 0, _row_pick(before, 15), 0.0)
    next_row = jnp.where(r0 + rc < L, _row_pick(after, 0), 0.0)
    rows = lax.broadcasted_iota(jnp.int32, cur.shape, 0)
    up = jnp.where(rows == 0, prev_row, pltpu.roll(cur, 1, 0))
    down = jnp.where(rows == rc - 1, next_row, pltpu.roll(cur, rc - 1, 0))
    return up, cur, down


def _silu_parts(g):
    s = _sigmoid(g)
    return g * s, s


def _conv_swiglu_fwd(name, h, cw, rider=None):
    _, L, FF = h.shape
    rc = _row_tile(L, CONV_ROWS)

    def body(h_ref, cw_ref, o_ref):
        def chunk(ci, _):
            r0 = pl.multiple_of(ci * rc, rc)
            hc = []
            for half in range(2):
                up, cur, down = _shifted(h_ref.at[half], r0, rc, L)
                hc.append(up * cw_ref[half, 0:1, :] + cur * cw_ref[half, 1:2, :] + down * cw_ref[half, 2:3, :]
                          + cw_ref[half, 3:4, :])
            o_ref[pl.ds(r0, rc), :] = (_silu_parts(hc[1])[0] * hc[0]).astype(o_ref.dtype)
            return 0

        lax.fori_loop(0, L // rc, chunk, 0)

    return _pcall(body, name=name, grid=(FF // LANE,),
                  in_specs=[pl.BlockSpec((2, L, LANE), lambda j: (0, 0, j)),
                            pl.BlockSpec((2, 8, LANE), lambda j: (0, 0, j))],
                  out_specs=pl.BlockSpec((L, LANE), lambda j: (0, j)),
                  out_shape=jax.ShapeDtypeStruct((L, FF), _ACT), ins=(h, cw), sem=("parallel",), rider=rider)


def _conv_swiglu_bwd(name, h, cw, dact, rider=None):
    _, L, FF = h.shape
    rc = _row_tile(L, CONV_ROWS)

    def body(h_ref, cw_ref, da_ref, dh_ref, dcw_ref, dhc_ref):
        def chunk(ci, acc):
            r0 = pl.multiple_of(ci * rc, rc)
            taps, hc = [], []
            for half in range(2):
                t = _shifted(h_ref.at[half], r0, rc, L)
                taps.append(t)
                hc.append(t[0] * cw_ref[half, 0:1, :] + t[1] * cw_ref[half, 1:2, :] + t[2] * cw_ref[half, 2:3, :]
                          + cw_ref[half, 3:4, :])
            d = da_ref[pl.ds(r0, rc), :].astype(F32)
            act, s = _silu_parts(hc[1])
            dhc = (d * act, d * hc[0] * (s + act * (1.0 - s)))
            new = []
            for half in range(2):
                dhc_ref[half, pl.ds(r0, rc), :] = dhc[half]
                for k in range(3):
                    new.append(acc[4 * half + k] + jnp.sum(dhc[half] * taps[half][k], axis=0, keepdims=True))
                new.append(acc[4 * half + 3] + jnp.sum(dhc[half], axis=0, keepdims=True))
            return tuple(new)

        zero = jnp.zeros((1, LANE), F32)
        acc = lax.fori_loop(0, L // rc, chunk, (zero,) * 8)
        dcw_ref[...] = jnp.zeros_like(dcw_ref)
        for half in range(2):
            for k in range(4):
                dcw_ref[half, k:k + 1, :] = acc[4 * half + k]

        def chunk2(ci, _):
            r0 = pl.multiple_of(ci * rc, rc)
            for half in range(2):
                up, cur, down = _shifted(dhc_ref.at[half], r0, rc, L)
                dh_ref[half, pl.ds(r0, rc), :] = (down * cw_ref[half, 0:1, :] + cur * cw_ref[half, 1:2, :]
                                                  + up * cw_ref[half, 2:3, :]).astype(dh_ref.dtype)
            return 0

        lax.fori_loop(0, L // rc, chunk2, 0)

    return _pcall(body, name=name, grid=(FF // LANE,),
                  in_specs=[pl.BlockSpec((2, L, LANE), lambda j: (0, 0, j)),
                            pl.BlockSpec((2, 8, LANE), lambda j: (0, 0, j)),
                            pl.BlockSpec((L, LANE), lambda j: (0, j))],
                  out_specs=[pl.BlockSpec((2, L, LANE), lambda j: (0, 0, j)),
                             pl.BlockSpec((2, 8, LANE), lambda j: (0, 0, j))],
                  out_shape=[jax.ShapeDtypeStruct((2, L, FF), _ACT), jax.ShapeDtypeStruct((2, 8, FF), F32)],
                  scratch_shapes=[pltpu.VMEM((2, L, LANE), F32)], ins=(h, cw, dact), sem=("parallel",), rider=rider)


POOL_ROWS = 256
POOL_TILE = 256


def _pool_bands(transpose):
    i = jnp.arange(POOL_ROWS)[:, None]
    j = jnp.arange(POOL_ROWS + 2 * POOL_HALO)[None, :] - POOL_HALO
    bands = []
    for w in POOL_WINDOWS:
        lo, hi = (-(w // 2 - 1), w // 2) if transpose else (-(w // 2), w // 2 - 1)
        bands.append(((j - i >= lo) & (j - i <= hi)).astype(_ACT))
    return jnp.stack(bands)


def _pool_window(name, u, transpose, out_dtype):
    L, D = u.shape
    ng = len(POOL_WINDOWS)
    pc = D // ng
    tn = min(POOL_TILE, pc)
    rc = _row_tile(L, POOL_ROWS)
    bands = _pool_bands(transpose)
    if rc != POOL_ROWS:
        bands = bands[:, :rc, :rc + 2 * POOL_HALO]
    halo = POOL_HALO

    def body(u_ref, band_ref, o_ref, hi_ref, lo_ref):
        g = (pl.program_id(0) * tn) // pc
        half = jnp.zeros((1, 1), jnp.int32)
        for k, w in enumerate(POOL_WINDOWS):
            half = jnp.where(g == k, w // 2, half)
        zeros = jnp.zeros((halo, tn), _ACT)
        for ref in (hi_ref, lo_ref):
            ref[0:halo, :] = zeros
            ref[halo + L:2 * halo + L, :] = zeros

        def inv_count(r0):
            t = r0 + lax.broadcasted_iota(jnp.int32, (rc, tn), 0)
            lo = jnp.clip(t - half, 0, L - 1)
            hi = jnp.clip(t + half - 1, 0, L - 1)
            return 1.0 / (hi - lo + 1).astype(F32)

        def split(ci, _):
            r0 = pl.multiple_of(ci * rc, rc)
            v = u_ref[pl.ds(r0, rc), :].astype(F32)
            if transpose:
                v = v * inv_count(r0)
            hi = v.astype(_ACT)
            dst = pl.ds(pl.multiple_of(r0 + halo, halo), rc)
            hi_ref[dst, :] = hi
            lo_ref[dst, :] = (v - hi.astype(F32)).astype(_ACT)
            return 0

        lax.fori_loop(0, L // rc, split, 0)
        band = band_ref[...]

        def chunk(ci, _):
            r0 = pl.multiple_of(ci * rc, rc)
            win = pl.ds(r0, rc + 2 * halo)
            s = (jnp.dot(band, hi_ref[win, :], preferred_element_type=F32)
                 + jnp.dot(band, lo_ref[win, :], preferred_element_type=F32))
            if not transpose:
                s = s * inv_count(r0)
            o_ref[pl.ds(r0, rc), :] = (s - u_ref[pl.ds(r0, rc), :].astype(F32)).astype(o_ref.dtype)
            return 0

        lax.fori_loop(0, L // rc, chunk, 0)

    return pl.pallas_call(body, name=name, grid=(D // tn,),
                          in_specs=[pl.BlockSpec((L, tn), lambda j: (0, j)),
                                    pl.BlockSpec((None, rc, rc + 2 * halo), lambda j: ((j * tn) // pc, 0, 0))],
                          out_specs=pl.BlockSpec((L, tn), lambda j: (0, j)),
                          out_shape=jax.ShapeDtypeStruct((L, D), out_dtype),
                          scratch_shapes=[pltpu.VMEM((L + 2 * halo, tn), _ACT), pltpu.VMEM((L + 2 * halo, tn), _ACT)],
                          compiler_params=_params(("parallel",)))(u, bands)


S5_ROWS = 512


def _slab(start):
    return pl.ds(start if isinstance(start, int) else pl.multiple_of(start, NSEG), NSEG)


def _cmul(ar, ai, br, bi):
    return ar * br - ai * bi, ar * bi + ai * br


def _cpow(ar, ai, n):
    rr, ri = None, None
    br, bi = ar, ai
    while n:
        if n & 1:
            rr, ri = (br, bi) if rr is None else _cmul(rr, ri, br, bi)
        n >>= 1
        if n:
            br, bi = _cmul(br, bi, br, bi)
    return rr, ri


def _pow_table(pw_ref, ar, ai, n):
    W = ar.shape[1]
    pr, pi = ar, ai
    for r in range(NSEG):
        pw_ref[0, r:r + 1, :] = pr
        pw_ref[1, r:r + 1, :] = pi
        if r < NSEG - 1:
            pr, pi = _cmul(pr, pi, ar, ai)
    a8r, a8i = (jnp.broadcast_to(v, (NSEG, W)) for v in _cpow(ar, ai, NSEG))

    def step(k, carry):
        nr, ni = _cmul(carry[0], carry[1], a8r, a8i)
        pw_ref[0, _slab(k * NSEG), :] = nr
        pw_ref[1, _slab(k * NSEG), :] = ni
        return nr, ni

    lax.fori_loop(1, n // NSEG, step, (pw_ref[0, 0:NSEG, :], pw_ref[1, 0:NSEG, :]))


def _seg_scan(sr_ref, si_ref, tmp_ref, pw_ref, row0, n, ar, ai, h0, rev, conj=False, pair_with=None):
    W = ar.shape[1]
    arb, aib = jnp.broadcast_to(ar, (NSEG, W)), jnp.broadcast_to(ai, (NSEG, W))

    def rows(s):
        t = (n - 1 - s) if rev else s
        return _slab(row0 + t * NSEG)

    def step(s, carry):
        hr, hi = carry
        sl = rows(s)
        nr = arb * hr - aib * hi + sr_ref[sl, :]
        ni = arb * hi + aib * hr + si_ref[sl, :]
        sr_ref[sl, :] = nr
        si_ref[sl, :] = ni
        return nr, ni

    zero = jnp.zeros((NSEG, W), F32)
    fr, fi = lax.fori_loop(0, n, step, (zero, zero), unroll=2)
    tmp_ref[0] = fr
    tmp_ref[1] = fi
    anr, ani = _cpow(ar, ai, n)
    cr, ci = h0
    for j in (range(NSEG - 1, -1, -1) if rev else range(NSEG)):
        tmp_ref[2, j:j + 1, :] = cr
        tmp_ref[3, j:j + 1, :] = ci
        pr, pi = _cmul(anr, ani, cr, ci)
        cr, ci = tmp_ref[0, j:j + 1, :] + pr, tmp_ref[1, j:j + 1, :] + pi
    cmr, cmi = tmp_ref[2], tmp_ref[3]

    def fix(k, acc):
        for r in range(NSEG):
            row = pl.ds(pl.multiple_of(k * NSEG, NSEG) + r, 1)
            pr = jnp.broadcast_to(pw_ref[0, row, :], (NSEG, W))
            pi = jnp.broadcast_to(pw_ref[1, row, :], (NSEG, W))
            s = k * NSEG + r
            sl = rows(s)
            if conj:
                gr = sr_ref[sl, :] + (pr * cmr + pi * cmi)
                gi = si_ref[sl, :] + (pr * cmi - pi * cmr)
            else:
                gr = sr_ref[sl, :] + (pr * cmr - pi * cmi)
                gi = si_ref[sl, :] + (pr * cmi + pi * cmr)
            sr_ref[sl, :] = gr
            si_ref[sl, :] = gi
            if pair_with is not None:
                prev = rows(jnp.minimum(s + 1, n - 1))
                hpr, hpi = pair_with[0][prev, :], pair_with[1][prev, :]
                acc = (acc[0] + hpr * gr + hpi * gi, acc[1] + hpr * gi - hpi * gr)
        return acc

    if pair_with is None:
        lax.fori_loop(0, n // NSEG, fix, 0)
        return (cr, ci), (cmr, cmi), None
    total = lax.fori_loop(0, n // NSEG, fix, (zero, zero))
    h_r, h_i, (hcr, hci) = pair_with
    last = rows(n - 1)
    fr, fi = hcr - h_r[last, :], hci - h_i[last, :]
    gr, gi = sr_ref[last, :], si_ref[last, :]
    total = (total[0] + fr * gr + fi * gi, total[1] + fr * gi - fi * gr)
    return (cr, ci), (cmr, cmi), total


def _gelu_tanh(y):
    k = math.sqrt(2.0 / math.pi)
    t = jnp.tanh(k * (y + 0.044715 * y * y * y))
    return 0.5 * y * (1.0 + t), t


def _s5_chunks(L):
    rc = _row_tile(L, S5_ROWS)
    return [(r, rc) for r in range(0, L, rc)]


_NT_DIMS = (((1,), (1,)), ((), ()))
_TN_DIMS = (((0,), (0,)), ((), ()))
_LOG_P, _LOG_CH = S5_P.bit_length() - 1, S5_CH.bit_length() - 1


def _same_group(shape, row_shift, col_shift):
    rows = lax.broadcasted_iota(jnp.int32, shape, 0)
    cols = lax.broadcasted_iota(jnp.int32, shape, 1)
    return lax.shift_right_logical(rows, row_shift) == lax.shift_right_logical(cols, col_shift)


def _s5_bt(bt):
    full = jnp.concatenate([bt] * S5_TILE_G, axis=0)
    return jnp.where(_same_group(full.shape, _LOG_P, _LOG_CH), full, 0.0).astype(_ACT)


def _s5_ct(ct):
    full = jnp.concatenate([ct] * S5_TILE_G, axis=0)
    return jnp.where(_same_group(full.shape, _LOG_CH, _LOG_P), full, 0.0).astype(_ACT)


def _s5_diag(m):
    m = jnp.where(_same_group(m.shape, _LOG_CH, _LOG_P), m, 0.0)
    rows = lax.broadcasted_iota(jnp.int32, (S5_TILE_W, S5_P), 0)
    cols = lax.broadcasted_iota(jnp.int32, (S5_TILE_W, S5_P), 1)
    pick = (jnp.bitwise_and(rows, S5_P - 1) == cols).astype(_ACT)
    hi = m.astype(_ACT)
    lo = (m - hi.astype(F32)).astype(_ACT)
    return jnp.dot(hi, pick, preferred_element_type=F32) + jnp.dot(lo, pick, preferred_element_type=F32)


def _s5_project(u_ref, uc_ref, bre, bim, sr_ref, si_ref, L, LC):
    for ref, base, n in ((u_ref, 0, L), (uc_ref, L, LC)):
        for r, rc in _s5_chunks(n):
            ub = ref[r:r + rc, :].astype(_ACT)
            sr_ref[base + r:base + r + rc, :] = lax.dot_general(ub, bre, _NT_DIMS, preferred_element_type=F32)
            si_ref[base + r:base + r + rc, :] = lax.dot_general(ub, bim, _NT_DIMS, preferred_element_type=F32)


def _s5_states(sr_ref, si_ref, tmp_ref, pw_ref, ar, ai, L, LC, rev):
    W = ar.shape[1]
    zero = (jnp.zeros((1, W), F32), jnp.zeros((1, W), F32))
    hctx, cm_ctx, _ = _seg_scan(sr_ref, si_ref, tmp_ref, pw_ref, L, LC // NSEG, ar, ai, zero, rev)
    _, cm_lat, _ = _seg_scan(sr_ref, si_ref, tmp_ref, pw_ref, 0, L // NSEG, ar, ai, hctx, rev)
    return cm_lat, cm_ctx


def _s5_fwd(u, uc, bblk, cblk, apar, dsk, rider=None):
    L, D = u.shape
    LC = uc.shape[0]
    NT, W, TC = D // S5_TILE_CH, S5_TILE_W, S5_TILE_CH

    def body(u_ref, uc_ref, b_ref, c_ref, a_ref, d_ref, y_ref, z_ref, sr_ref, si_ref, tmp_ref, pw_ref):
        for r, rc in _s5_chunks(L):
            y_ref[r:r + rc, :] = u_ref[r:r + rc, :].astype(F32) * d_ref[0:1, :]
        for d in range(2):
            ar, ai = a_ref[2 * d:2 * d + 1, :], a_ref[2 * d + 1:2 * d + 2, :]
            _pow_table(pw_ref, ar, ai, L // NSEG)
            _s5_project(u_ref, uc_ref, _s5_bt(b_ref[2 * d]), _s5_bt(b_ref[2 * d + 1]), sr_ref, si_ref, L, LC)
            _s5_states(sr_ref, si_ref, tmp_ref, pw_ref, ar, ai, L, LC, rev=(d == 1))
            cre, cim = _s5_ct(c_ref[2 * d]), _s5_ct(c_ref[2 * d + 1])
            for r, rc in _s5_chunks(L):
                y_ref[r:r + rc, :] += (
                    lax.dot_general(sr_ref[r:r + rc, :].astype(_ACT), cre, _NT_DIMS, preferred_element_type=F32)
                    - lax.dot_general(si_ref[r:r + rc, :].astype(_ACT), cim, _NT_DIMS, preferred_element_type=F32))
        for r, rc in _s5_chunks(L):
            z_ref[r:r + rc, :] = _gelu_tanh(y_ref[r:r + rc, :])[0].astype(z_ref.dtype)

    col = lambda n: pl.BlockSpec((n, TC), lambda j: (0, j))
    return _pcall(
        body, name="s5_fwd", grid=(NT,), ins=(u, uc, bblk, cblk, apar, dsk), sem=("parallel",), rider=rider,
        in_specs=[col(L), col(LC),
                  pl.BlockSpec((None, 4, S5_P, TC), lambda j: (j, 0, 0, 0)),
                  pl.BlockSpec((None, 4, S5_CH, W), lambda j: (j, 0, 0, 0)),
                  pl.BlockSpec((None, 8, W), lambda j: (j, 0, 0)),
                  pl.BlockSpec((8, TC), lambda j: (0, j))],
        out_specs=[col(L), col(L)],
        out_shape=[jax.ShapeDtypeStruct((L, D), F32), jax.ShapeDtypeStruct((L, D), _ACT)],
        scratch_shapes=[pltpu.VMEM((L + LC, W), F32), pltpu.VMEM((L + LC, W), F32), pltpu.VMEM((4, NSEG, W), F32),
                        pltpu.VMEM((2, L // NSEG, W), F32)])


def _s5_bwd(u, uc, dz, y, bblk, cblk, apar, dsk, rider=None):
    L, D = u.shape
    LC = uc.shape[0]
    NT, W, TC = D // S5_TILE_CH, S5_TILE_W, S5_TILE_CH
    nl, nc = L // NSEG, LC // NSEG

    def body(u_ref, uc_ref, dz_ref, y_ref, b_ref, c_ref, a_ref, d_ref,
             du_ref, duc_ref, db_ref, dc_ref, da_ref, dd_ref,
             hr_ref, hi_ref, gr_ref, gi_ref, dy_ref, tmp_ref, pw_ref):
        ddacc = jnp.zeros((1, TC), F32)
        for r, rc in _s5_chunks(L):
            yv = y_ref[r:r + rc, :]
            g, t = _gelu_tanh(yv)
            k = math.sqrt(2.0 / math.pi)
            dg = 0.5 * (1.0 + t) + 0.5 * yv * (1.0 - t * t) * k * (1.0 + 3 * 0.044715 * yv * yv)
            dy = dz_ref[r:r + rc, :].astype(F32) * dg
            uv = u_ref[r:r + rc, :].astype(F32)
            ddacc = ddacc + jnp.sum(dy * uv, axis=0, keepdims=True)
            du_ref[r:r + rc, :] = dy * d_ref[0:1, :]
            dy_ref[r:r + rc, :] = dy.astype(dy_ref.dtype)
        dd_ref[...] = jnp.zeros_like(dd_ref)
        dd_ref[0:1, :] = ddacc
        duc_ref[...] = jnp.zeros_like(duc_ref)
        da_ref[...] = jnp.zeros_like(da_ref)
        tn = _TN_DIMS
        for d in range(2):
            rev = d == 1
            ar, ai = a_ref[2 * d:2 * d + 1, :], a_ref[2 * d + 1:2 * d + 2, :]
            bre, bim = _s5_bt(b_ref[2 * d]), _s5_bt(b_ref[2 * d + 1])
            cre, cim = _s5_ct(c_ref[2 * d]), _s5_ct(c_ref[2 * d + 1])
            _pow_table(pw_ref, ar, ai, nl)
            _s5_project(u_ref, uc_ref, bre, bim, hr_ref, hi_ref, L, LC)
            cm_lat, cm_ctx = _s5_states(hr_ref, hi_ref, tmp_ref, pw_ref, ar, ai, L, LC, rev)
            cml_r, cml_i, cmc_r, cmc_i = cm_lat[0], cm_lat[1], cm_ctx[0], cm_ctx[1]
            dcr = jnp.zeros((TC, W), F32)
            dci = jnp.zeros((TC, W), F32)
            for r, rc in _s5_chunks(L):
                dyb = dy_ref[r:r + rc, :]
                gr_ref[r:r + rc, :] = jnp.dot(dyb, cre, preferred_element_type=F32)
                gi_ref[r:r + rc, :] = -jnp.dot(dyb, cim, preferred_element_type=F32)
                dcr = dcr + lax.dot_general(dyb, hr_ref[r:r + rc, :].astype(_ACT), tn, preferred_element_type=F32)
                dci = dci - lax.dot_general(dyb, hi_ref[r:r + rc, :].astype(_ACT), tn, preferred_element_type=F32)
            dc_ref[2 * d] = _s5_diag(dcr)
            dc_ref[2 * d + 1] = _s5_diag(dci)
            gr_ref[L:L + LC, :] = jnp.zeros((LC, W), F32)
            gi_ref[L:L + LC, :] = jnp.zeros((LC, W), F32)
            zero = (jnp.zeros((1, W), F32), jnp.zeros((1, W), F32))
            glat, _, (lr, li) = _seg_scan(gr_ref, gi_ref, tmp_ref, pw_ref, 0, nl, ar, -ai, zero, not rev, conj=True,
                                          pair_with=(hr_ref, hi_ref, (cml_r, cml_i)))
            _, _, (qr, qi) = _seg_scan(gr_ref, gi_ref, tmp_ref, pw_ref, L, nc, ar, -ai, glat, not rev, conj=True,
                                       pair_with=(hr_ref, hi_ref, (cmc_r, cmc_i)))
            da_ref[2 * d:2 * d + 1, :] = jnp.sum(lr + qr, axis=0, keepdims=True)
            da_ref[2 * d + 1:2 * d + 2, :] = jnp.sum(li + qi, axis=0, keepdims=True)
            dbr = jnp.zeros((TC, W), F32)
            dbi = jnp.zeros((TC, W), F32)
            for ref, oref, base, n in ((u_ref, du_ref, 0, L), (uc_ref, duc_ref, L, LC)):
                for r, rc in _s5_chunks(n):
                    ub = ref[r:r + rc, :].astype(_ACT)
                    gr = gr_ref[base + r:base + r + rc, :].astype(_ACT)
                    gi = gi_ref[base + r:base + r + rc, :].astype(_ACT)
                    dbr = dbr + lax.dot_general(ub, gr, tn, preferred_element_type=F32)
                    dbi = dbi + lax.dot_general(ub, gi, tn, preferred_element_type=F32)
                    oref[r:r + rc, :] += (jnp.dot(gr, bre, preferred_element_type=F32)
                                          + jnp.dot(gi, bim, preferred_element_type=F32))
            db_ref[2 * d] = _s5_diag(dbr)
            db_ref[2 * d + 1] = _s5_diag(dbi)

    col = lambda n: pl.BlockSpec((n, TC), lambda j: (0, j))
    bspec = pl.BlockSpec((None, 4, S5_P, TC), lambda j: (j, 0, 0, 0))
    cspec = pl.BlockSpec((None, 4, S5_CH, W), lambda j: (j, 0, 0, 0))
    gspec = pl.BlockSpec((None, 4, TC, S5_P), lambda j: (j, 0, 0, 0))
    aspec = pl.BlockSpec((None, 8, W), lambda j: (j, 0, 0))
    return _pcall(
        body, name="s5_bwd", grid=(NT,), ins=(u, uc, dz, y, bblk, cblk, apar, dsk), sem=("parallel",),
        vmem=VMEM_LIMIT_BIG, rider=rider,
        in_specs=[col(L), col(LC), col(L), col(L), bspec, cspec, aspec, pl.BlockSpec((8, TC), lambda j: (0, j))],
        out_specs=[col(L), col(LC), gspec, gspec, aspec, pl.BlockSpec((None, 8, TC), lambda j: (j, 0, 0))],
        out_shape=[jax.ShapeDtypeStruct((L, D), F32), jax.ShapeDtypeStruct((LC, D), F32),
                   jax.ShapeDtypeStruct((NT, 4, TC, S5_P), F32), jax.ShapeDtypeStruct((NT, 4, TC, S5_P), F32),
                   jax.ShapeDtypeStruct((NT, 8, W), F32), jax.ShapeDtypeStruct((NT, 8, TC), F32)],
        scratch_shapes=[pltpu.VMEM((L + LC, W), F32), pltpu.VMEM((L + LC, W), F32),
                        pltpu.VMEM((L + LC, W), F32), pltpu.VMEM((L + LC, W), F32),
                        pltpu.VMEM((L, TC), _ACT), pltpu.VMEM((4, NSEG, W), F32), pltpu.VMEM((2, nl, W), F32)])


ADA_ROWS = 16


def _silu_rows(c_ref):
    c = c_ref[...]
    return c * _sigmoid(c)


def _ada_fwd(cmat, ada_w, ada_b):
    nl, D, n = ada_w.shape
    tn = _row_tile(n, 512)

    def body(c_ref, w_ref, b_ref, o_ref):
        a = _silu_rows(c_ref).astype(_ACT)
        o_ref[...] = jnp.dot(a, w_ref[...].astype(_ACT), preferred_element_type=F32) + b_ref[...]

    return pl.pallas_call(body, name="ada_fwd", grid=(nl, n // tn),
                          in_specs=[pl.BlockSpec((ADA_ROWS, D), lambda l, j: (0, 0)),
                                    pl.BlockSpec((None, D, tn), lambda l, j: (l, 0, j)),
                                    pl.BlockSpec((None, 1, tn), lambda l, j: (l, 0, j))],
                          out_specs=pl.BlockSpec((None, ADA_ROWS, tn), lambda l, j: (l, 0, j)),
                          out_shape=jax.ShapeDtypeStruct((nl, ADA_ROWS, n), F32),
                          compiler_params=_params(("parallel", "parallel")))(cmat, ada_w, ada_b)


def _ada_bwd(cmat, ada_w, dm):
    nl, D, n = ada_w.shape
    tn = _row_tile(n, 512)
    nj = n // tn

    def body(c_ref, w_ref, dm_ref, dw_ref, dc_ref):
        c = c_ref[...]
        s = _sigmoid(c)
        a = (c * s).astype(_ACT)
        dmb = dm_ref[...].astype(_ACT)
        dw_ref[...] = lax.dot_general(a, dmb, (((0,), (0,)), ((), ())), preferred_element_type=F32)
        part = lax.dot_general(dmb, w_ref[...].astype(_ACT), (((1,), (1,)), ((), ())), preferred_element_type=F32)
        part = part * (s * (1.0 + c * (1.0 - s)))

        @pl.when(pl.program_id(1) == 0)
        def _():
            dc_ref[...] = part

        @pl.when(pl.program_id(1) > 0)
        def _():
            dc_ref[...] += part

    return pl.pallas_call(body, name="ada_bwd", grid=(nl, nj),
                          in_specs=[pl.BlockSpec((ADA_ROWS, D), lambda l, j: (0, 0)),
                                    pl.BlockSpec((None, D, tn), lambda l, j: (l, 0, j)),
                                    pl.BlockSpec((None, ADA_ROWS, tn), lambda l, j: (l, 0, j))],
                          out_specs=[pl.BlockSpec((None, D, tn), lambda l, j: (l, 0, j)),
                                     pl.BlockSpec((None, ADA_ROWS, D), lambda l, j: (l, 0, 0))],
                          out_shape=[jax.ShapeDtypeStruct((nl, D, n), F32), jax.ShapeDtypeStruct((nl, ADA_ROWS, D), F32)],
                          compiler_params=_params(("parallel", "arbitrary")))(cmat, ada_w, dm)


def _adamw(name, gparts, w, m, v):
    nl, R, C = w.shape
    gparts = [g if isinstance(g, tuple) else (g, 0) for g in gparts]
    n = gparts[0][0].shape[0]
    runs = []
    for l, (g, r0) in enumerate(gparts):
        if runs and runs[-1][0] is g and runs[-1][1] + runs[-1][3] * R == r0:
            runs[-1][3] += 1
        else:
            runs.append([g, r0, l, 1])
    run_of = [q for q, run in enumerate(runs) for _ in range(run[3])]
    tr = R
    part_bytes = len(runs) * n * C * gparts[0][0].dtype.itemsize * 2
    for cand in (4096, 2048, 1024, 512, 256, 128, 64, 32, 16, 8):
        if R % cand == 0 and cand * max(C, LANE) * 4 <= 2 * 1024 * 1024 and cand * part_bytes <= VMEM_LIMIT // 2:
            tr = cand
            break
    nt = R // tr
    bc1 = 1.0 - ADAM_B1 ** ADAM_STEP
    bc2 = 1.0 - ADAM_B2 ** ADAM_STEP

    def body(*refs):
        g_refs = refs[:len(runs)]
        w_ref, m_ref, v_ref, go_ref, d_ref, mo_ref, vo_ref = refs[len(runs):]
        for l in range(nl):
            @pl.when(pl.program_id(0) == l)
            def _(g_ref=g_refs[run_of[l]]):
                g = g_ref[0].astype(F32)
                for j in range(1, n):
                    g = g + g_ref[j].astype(F32)
                m2 = ADAM_B1 * m_ref[...] + (1.0 - ADAM_B1) * g
                v2 = ADAM_B2 * v_ref[...] + (1.0 - ADAM_B2) * (g * g)
                go_ref[...] = g
                mo_ref[...] = m2
                vo_ref[...] = v2
                d_ref[...] = -ADAM_LR * ((m2 / bc1) / (jnp.sqrt(v2 / bc2) + ADAM_EPS) + ADAM_WD * w_ref[...])

    def gspec(run):
        _, r0, l0, count = run
        return pl.BlockSpec((n, tr, C),
                            lambda lyr, i: (0, r0 // tr + jnp.clip((lyr - l0) * nt + i, 0, count * nt - 1), 0))

    row = pl.BlockSpec((None, tr, C), lambda lyr, i: (lyr, i, 0))
    out = jax.ShapeDtypeStruct((nl, R, C), F32)
    return _pcall(body, name=name, grid=(nl, nt), in_specs=[gspec(run) for run in runs] + [row, row, row],
                  out_specs=[row, row, row, row], out_shape=[out, out, out, out],
                  ins=(*[run[0] for run in runs], w, m, v), sem=("arbitrary", "arbitrary"))


def _sum_parts(name, parts):
    n, R, C = parts.shape

    def body(p_ref, o_ref):
        s = p_ref[0]
        for j in range(1, n):
            s = s + p_ref[j]
        o_ref[...] = s

    return pl.pallas_call(body, name=name, out_shape=jax.ShapeDtypeStruct((R, C), F32),
                          compiler_params=_params(None))(parts)


def _discretize(lam_re, lam_im, log_step, b_re, b_im):
    dt = jnp.exp(log_step)[:, None]
    mag = jnp.exp(lam_re * dt)
    abar_re = mag * jnp.cos(lam_im * dt)
    abar_im = mag * jnp.sin(lam_im * dt)
    nr, ni = abar_re - 1.0, abar_im
    den = lam_re * lam_re + lam_im * lam_im
    fr = (nr * lam_re + ni * lam_im) / den
    fi = (ni * lam_re - nr * lam_im) / den
    bbar_re = fr[..., None] * b_re - fi[..., None] * b_im
    bbar_im = fr[..., None] * b_im + fi[..., None] * b_re
    return abar_re, abar_im, bbar_re, bbar_im


def _s5_pack(abar, bbar, cmat):
    G = abar[0][0].shape[0]
    NT = G // S5_TILE_G
    a4 = jnp.stack([abar[d][r] for d in range(2) for r in range(2)]).reshape(4, NT, S5_TILE_W).transpose(1, 0, 2)
    apar = jnp.concatenate([a4, jnp.zeros((NT, 4, S5_TILE_W), F32)], axis=1)
    b4 = jnp.stack([bbar[d][r] for d in range(2) for r in range(2)]).reshape(4, NT, S5_TILE_G, S5_P, S5_CH)
    bt = b4.transpose(1, 0, 3, 2, 4).reshape(NT, 4, S5_P, S5_TILE_CH)
    c4 = jnp.stack([cmat[d][r] for d in range(2) for r in range(2)]).reshape(4, NT, S5_TILE_G, S5_CH, S5_P)
    ct = c4.transpose(1, 0, 3, 2, 4).reshape(NT, 4, S5_CH, S5_TILE_W)
    return apar, bt, ct


def _s5_unpack(dapar, dbd, dcd, G):
    NT = G // S5_TILE_G
    da = dapar[:, :4, :].reshape(NT, 2, 2, S5_TILE_G, S5_P).transpose(1, 2, 0, 3, 4).reshape(2, 2, G, S5_P)
    db = dbd.reshape(NT, 4, S5_TILE_G, S5_CH, S5_P).transpose(1, 0, 2, 4, 3).reshape(2, 2, G, S5_P, S5_CH)
    dc = dcd.reshape(NT, 4, S5_TILE_G, S5_CH, S5_P).transpose(1, 0, 2, 3, 4).reshape(2, 2, G, S5_CH, S5_P)
    return da, db, dc


def _to_segments(a):
    L, D = a.shape
    return a.reshape(NSEG, L // NSEG, D).transpose(1, 0, 2).reshape(L, D)


def _from_segments(a):
    L, D = a.shape
    return a.reshape(L // NSEG, NSEG, D).transpose(1, 0, 2).reshape(L, D)


def _pos_emb(n_tokens, dim):
    rows = n_tokens // GRID_W
    quarter = dim // 4
    omega = 1.0 / (POS_BASE ** (jnp.arange(quarter, dtype=F32) / quarter))

    def enc(p):
        ang = p[:, None] * omega[None, :]
        return jnp.concatenate([jnp.sin(ang), jnp.cos(ang)], axis=-1)

    rtab = enc(jnp.arange(rows, dtype=F32))
    ctab = enc(jnp.arange(GRID_W, dtype=F32))
    return jnp.concatenate([jnp.repeat(rtab, GRID_W, axis=0), jnp.tile(ctab, (rows, 1))], axis=-1)


def _vec(D, **rows):
    names = {"gpost": V_GPOST, "gate": V_GATE, "yscale": V_YSCALE, "gpre": V_GPRE, "shift": V_SHIFT, "scale": V_SCALE}
    out = [jnp.zeros((D,), F32)] * 8
    out[V_YSCALE] = jnp.ones((D,), F32)
    for k, v in rows.items():
        out[names[k]] = v.reshape(D).astype(F32)
    return jnp.stack(out)


def _row0(v, D):
    return jnp.concatenate([v.reshape(1, D).astype(F32), jnp.zeros((7, D), F32)], axis=0)


def _my_block(full, axis, n_local):
    return lax.dynamic_slice_in_dim(full, _my_index() * n_local, n_local, axis)


def kernel(x, c, ctx, c_ctx, ada_w, ada_b, norm_g, s5_lam_re, s5_lam_im, s5_log_step, s5_b_re, s5_b_im, s5_c_re, s5_c_im, s5_d, s5_glu_w, pool_w, pool_scale, ffn_up, ffn_conv, ffn_conv_b, ffn_down, loss_target, m_c_ctx, m_ada_w, m_ada_b, m_norm_g, m_s5_lam_re, m_s5_lam_im, m_s5_log_step, m_s5_b_re, m_s5_b_im, m_s5_c_re, m_s5_c_im, m_s5_d, m_s5_glu_w, m_pool_w, m_pool_scale, m_ffn_up, m_ffn_conv, m_ffn_conv_b, m_ffn_down, v_c_ctx, v_ada_w, v_ada_b, v_norm_g, v_s5_lam_re, v_s5_lam_im, v_s5_log_step, v_s5_b_re, v_s5_b_im, v_s5_c_re, v_s5_c_im, v_s5_d, v_s5_glu_w, v_pool_w, v_pool_scale, v_ffn_up, v_ffn_conv, v_ffn_conv_b, v_ffn_down):
    L, D = x.shape[1], x.shape[2]
    LC = ctx.shape[1]
    G = s5_lam_re.shape[2]
    n_ada = ada_w.shape[2]
    nb_up = ffn_up.shape[2]
    r_down = ffn_down.shape[1]
    FF = N_DEV * r_down
    n_pool = len(POOL_WINDOWS)
    pc = D // n_pool
    pr = pool_w.shape[2]
    ng_loc = norm_g.shape[2]
    me = _my_index()
    axes = ("x", "y", "c")

    up_b = [ffn_up[i].astype(_ACT) for i in range(2)]
    down_b = [ffn_down[i].astype(_ACT) for i in range(2)]
    glu_b = s5_glu_w[0].astype(_ACT)
    pool_b = pool_w[0].reshape(n_pool * pr, pc).astype(_ACT)

    small_loc = jnp.concatenate([c.reshape(-1), norm_g.reshape(-1), pool_scale.reshape(-1), ffn_conv.reshape(-1)])
    n_small = small_loc.shape[0]
    small_g, = _exchange([[jnp.pad(small_loc, (0, (-n_small) % LANE)).reshape(1, -1)]], mode="gather", name="gather_small")
    small_g = small_g.reshape(N_DEV, -1)
    o = 0
    c_all = small_g[:, o:o + D]
    o += D
    ng_all = small_g[:, o:o + 8 * ng_loc].reshape(N_DEV, 2, 4, ng_loc).transpose(1, 2, 0, 3).reshape(2, 4, D)
    o += 8 * ng_loc
    pscale_all = small_g[:, o:o + ng_loc].reshape(D)
    o += ng_loc
    conv_all = small_g[:, o:o + 6 * nb_up].reshape(N_DEV, 2, 3, nb_up).transpose(1, 2, 0, 3).reshape(2, 3, 2 * FF)

    cmat = jnp.concatenate([c_all, c_ctx.reshape(1, D), jnp.zeros((ADA_ROWS - N_DEV - 1, D), F32)], axis=0)
    ada_b_loc = _my_block(ada_b, 1, n_ada).reshape(2, 1, n_ada)
    mods_loc = _ada_fwd(cmat, ada_w, ada_b_loc)
    mods_g, = _exchange([[mods_loc]], mode="gather", name="gather_mods")
    mods_rows = mods_g.reshape(N_DEV, 2, ADA_ROWS, n_ada).transpose(1, 2, 0, 3).reshape(2, ADA_ROWS, 6, D)
    mod = lax.dynamic_index_in_dim(mods_rows, me, axis=1, keepdims=False)
    mod_c = mods_rows[0, N_DEV]

    def disc_all(lr, li, ls, br, bi):
        return [_discretize(lr[d], li[d], ls[d], br[d], bi[d]) for d in range(2)]

    disc, disc_vjp = jax.vjp(disc_all, s5_lam_re[0], s5_lam_im[0], s5_log_step[0], s5_b_re[0], s5_b_im[0])
    apar, bblk, cblk = _s5_pack([(disc[d][0], disc[d][1]) for d in range(2)],
                                [(disc[d][2], disc[d][3]) for d in range(2)],
                                [(s5_c_re[0, d], s5_c_im[0, d]) for d in range(2)])
    dsk = _row0(s5_d[0], D)
    cw = []
    for i in range(2):
        taps = conv_all[i].reshape(3, 2, FF).transpose(1, 0, 2)
        cw.append(jnp.concatenate([taps, ffn_conv_b[i].reshape(2, 1, FF), jnp.zeros((2, 4, FF), F32)], axis=1))

    vecs = {
        "b0": _vec(D, gpre=ng_all[0, 0], shift=mod[0, 0], scale=mod[0, 1]),
        "c0": _vec(D, gpre=ng_all[0, 0], shift=mod_c[0], scale=mod_c[1]),
        "b1": _vec(D, gpost=ng_all[0, 1], gate=mod[0, 2], gpre=ng_all[0, 2], shift=mod[0, 3], scale=mod[0, 4]),
        "b2": _vec(D, gpost=ng_all[0, 3], gate=mod[0, 5], gpre=ng_all[1, 0], shift=mod[1, 0], scale=mod[1, 1]),
        "b3": _vec(D, gpost=ng_all[1, 1], gate=mod[1, 2], yscale=pscale_all, gpre=ng_all[1, 2], shift=mod[1, 3],
                   scale=mod[1, 4]),
        "b4": _vec(D, gpost=ng_all[1, 3], gate=mod[1, 5]),
    }

    x0, u0 = _rows_fwd("rows_fwd_b0", x[0], _pos_emb(L, D), vecs["b0"], add=True, u_dtype=_ACT)
    uc, = _rows_fwd("rows_fwd_ctx", ctx[0], None, vecs["c0"], want_x=False, u_dtype=_ACT)
    u0s, ucs = _to_segments(u0), _to_segments(uc)
    (y_s5, z_s5), (glu_g, up_g0, down_g0) = _s5_fwd(u0s, ucs, bblk, cblk, apar, dsk,
                                                    rider=([[glu_b], [up_b[0]], [down_b[0]]], "gather2"))
    vg = _colblock_fwd("glu_fwd_mm", z_s5, glu_g, 0, _ACT)
    mix0 = _from_segments(_glu_fwd("glu_fwd", vg))
    x1, un0 = _rows_fwd("rows_fwd_b1", x0, mix0, vecs["b1"], u_dtype=_ACT)
    h0, (up_g1,) = _colblock_fwd("ffn0_up", un0, up_g0, 0, _ACT, rider=([[up_b[1]]], "gather2"))
    act0 = _conv_swiglu_fwd("ffn0_conv", h0, cw[0])
    f0, (down_g1, pool_g) = _rowblock_fwd("ffn0_down", act0, down_g0, 0, rider=([[down_b[1]], [pool_b]], "gather2"))
    pool_full = pool_g.reshape(N_DEV, n_pool, pr, pc).transpose(1, 0, 2, 3).reshape(n_pool, pc, pc)
    x2, u1 = _rows_fwd("rows_fwd_b2", x1, f0, vecs["b2"], u_dtype=F32)
    p1 = _pool_window("pool_fwd", u1, False, _ACT)
    ypre1 = _group_mm("pool_fwd_mm", p1, pool_full, "nn", F32)
    x3, un1 = _rows_fwd("rows_fwd_b3", x2, ypre1, vecs["b3"], u_dtype=_ACT)
    h1 = _colblock_fwd("ffn1_up", un1, up_g1, 0, _ACT)
    act1 = _conv_swiglu_fwd("ffn1_conv", h1, cw[1])
    f1 = _rowblock_fwd("ffn1_down", act1, down_g1, 0)
    dx4, loss_blk, df1, red4 = _rows_fwd("rows_fwd_b4", x3, f1, vecs["b4"], target=loss_target[0])
    loss = lax.psum(loss_blk[0, 0], axes)

    dact1 = _rowblock_dgrad("ffn1_down_dgrad", df1, down_g1, 0)
    ddown1 = _rowblock_wgrad("ffn1_down_wgrad", act1, df1)
    ddown1 = ddown1.reshape(N_DEV, r_down, D)
    (dh1, dcw1), (gp_down1a,) = _conv_swiglu_bwd("ffn1_conv_bwd", h1, cw[1], dact1,
                                                rider=([[(ddown1, (0, r_down // 2))]], "scatter"))
    dun1, (gp_down1b,) = _colblock_dgrad("ffn1_up_dgrad", dh1, up_g1, 0, F32,
                                         rider=([[(ddown1, (r_down // 2, r_down // 2))]], "scatter"))
    dup1 = _colblock_wgrad("ffn1_up_wgrad", un1, dh1)
    dx3, dypre1, red3 = _rows_bwd("rows_bwd_b3", dx4, dun1, x3, ypre1, vecs["b3"], dy_dtype=_ACT, yscale_grad=True)
    dp1 = _group_mm("pool_dgrad", dypre1, pool_full, "nt", F32)
    dpool = _group_wgrad("pool_wgrad", p1, dypre1, n_pool)
    du1 = _pool_window("pool_bwd", dp1, True, F32)
    dx2, df0, red2 = _rows_bwd("rows_bwd_b2", dx3, du1, x2, f0, vecs["b2"], dy_dtype=_ACT)
    dact0, (gp_up1c,) = _rowblock_dgrad("ffn0_down_dgrad", df0, down_g0, 0,
                                        rider=([[(dup1, (3 * D // 4, D // 4))]], "scatter"))
    ddown0 = _rowblock_wgrad("ffn0_down_wgrad", act0, df0)
    ddown0 = ddown0.reshape(N_DEV, r_down, D)
    (dh0, dcw0), (gp_down0a,) = _conv_swiglu_bwd("ffn0_conv_bwd", h0, cw[0], dact0,
                                                 rider=([[(ddown0, (0, r_down // 2))]], "scatter"))
    dun0, (gp_up1a,) = _colblock_dgrad("ffn0_up_dgrad", dh0, up_g0, 0, F32,
                                       rider=([[(dup1, (0, D // 2))]], "scatter"))
    dup0, (gp_up1b,) = _colblock_wgrad("ffn0_up_wgrad", un0, dh0, rider=([[(dup1, (D // 2, D // 4))]], "scatter"))
    dx1, dmix0, red1 = _rows_bwd("rows_bwd_b1", dx2, dun0, x1, mix0, vecs["b1"])
    dvg = _glu_bwd("glu_bwd", vg, _to_segments(dmix0))
    dz = _colblock_dgrad("glu_dgrad", dvg, glu_g, 0, _ACT)
    dglu = _colblock_wgrad("glu_wgrad", z_s5, dvg)
    dpool_blocks = dpool.reshape(n_pool, N_DEV, pr, pc).transpose(1, 0, 2, 3).reshape(N_DEV, n_pool * pr, pc)
    (du0s, ducs, dbblk, dcblk, dapar, ddsk), (gp_up0, gp_glu, gp_pool, gp_down0b) = _s5_bwd(
        u0s, ucs, dz, y_s5, bblk, cblk, apar, dsk,
        rider=([[dup0], [dglu], [dpool_blocks], [(ddown0, (r_down // 2, r_down // 2))]], "scatter"))
    grad_x, red0 = _rows_bwd("rows_bwd_b0", dx1, _from_segments(du0s), x0, None, vecs["b0"])
    redc, = _rows_bwd("rows_bwd_ctx", None, _from_segments(ducs), ctx[0], None, vecs["c0"], want_dx=False)

    zero_d = jnp.zeros((D,), F32)
    dmod = jnp.stack([
        jnp.stack([red0[R_SHIFT], red0[R_SCALE], red1[R_GATE], red1[R_SHIFT], red1[R_SCALE], red2[R_GATE]]),
        jnp.stack([red2[R_SHIFT], red2[R_SCALE], red3[R_GATE], red3[R_SHIFT], red3[R_SCALE], red4[R_GATE]])])
    dmod_c = jnp.stack([jnp.stack([redc[R_SHIFT], redc[R_SCALE]] + [zero_d] * 4), jnp.zeros((6, D), F32)])
    dm_g, = _exchange([[jnp.stack([dmod, dmod_c], axis=1).reshape(2, 2, 6 * D)]], mode="gather", name="gather_dmods")
    dm_g = dm_g.reshape(N_DEV, 2, 2, 6 * D)
    dm_ctx = _sum_parts("sum_dmod_ctx", dm_g[:, :, 1, :])
    dm_rows = jnp.concatenate([dm_g[:, :, 0, :].transpose(1, 0, 2), dm_ctx[:, None, :]], axis=1)
    grad_ada_b = _sum_parts("sum_ada_b", dm_rows.transpose(1, 0, 2))
    dm_cols = dm_rows.reshape(2, N_DEV + 1, N_DEV, n_ada)
    dm_mine = lax.dynamic_index_in_dim(dm_cols, me, axis=2, keepdims=False)
    dm_mine = jnp.concatenate([dm_mine, jnp.zeros((2, ADA_ROWS - N_DEV - 1, n_ada), F32)], axis=1)
    grad_ada_w, dcond = _ada_bwd(cmat, ada_w, dm_mine)
    dcctx_part = dcond[0, N_DEV] + dcond[1, N_DEV]

    da, db, dc = _s5_unpack(dapar, dbblk, dcblk, G)
    dnorm = jnp.stack([
        jnp.stack([red0[R_GPRE] + redc[R_GPRE], red1[R_GPOST], red1[R_GPRE], red2[R_GPOST]]),
        jnp.stack([red2[R_GPRE], red3[R_GPOST], red3[R_GPRE], red4[R_GPOST]])])
    dconv = jnp.stack([d[:, :3, :].transpose(1, 0, 2).reshape(3, 2 * FF) for d in (dcw0, dcw1)])
    dconv_b = jnp.stack([d[:, 3, :].reshape(2 * FF) for d in (dcw0, dcw1)])
    pieces = [dcctx_part, dnorm, da, db, dc, ddsk[:, 0, :], red3[R_YSCALE], dconv, dconv_b]
    flat = jnp.concatenate([p.reshape(-1) for p in pieces])
    n_flat = flat.shape[0]
    per_dev = -(-n_flat // (N_DEV * 8 * LANE)) * 8 * LANE
    flat = jnp.pad(flat, (0, N_DEV * per_dev - n_flat)).reshape(N_DEV, per_dev // LANE, LANE)
    parts, = _exchange([[flat]], mode="scatter", name="scatter_small_grads")
    mine = _sum_parts("sum_small_grads", parts.reshape(N_DEV, per_dev // LANE, LANE))
    summed, = _exchange([[mine]], mode="gather", name="gather_small_grads")
    summed = summed.reshape(-1)
    red_pieces, o = [], 0
    for p in pieces:
        red_pieces.append(summed[o:o + p.size].reshape(p.shape))
        o += p.size
    g_cctx, g_norm, g_a, g_b, g_c, g_d, g_pscale, g_conv, g_conv_b = red_pieces
    cot = [(g_a[d, 0], g_a[d, 1], g_b[d, 0], g_b[d, 1]) for d in range(2)]
    g_lam_re, g_lam_im, g_log_step, g_b_re, g_b_im = disc_vjp(cot)

    out = {}

    def put(name, res, shape):
        out[name] = tuple(r.reshape(shape) for r in res)

    gp_up0, gp_up1a = gp_up0.reshape(N_DEV, D, nb_up), gp_up1a.reshape(N_DEV, D // 2, nb_up)
    quarters = (8, D // 4, nb_up)
    put("ffn_up", _adamw("adamw_ffn_up",
                         [(gp_up0, q * D // 4) for q in range(4)] + [(gp_up1a, 0), (gp_up1a, D // 4),
                                                                     gp_up1b.reshape(N_DEV, D // 4, nb_up),
                                                                     gp_up1c.reshape(N_DEV, D // 4, nb_up)],
                         ffn_up.reshape(quarters), m_ffn_up.reshape(quarters), v_ffn_up.reshape(quarters)),
        ffn_up.shape)
    halves = (4, r_down // 2, D)
    put("ffn_down", _adamw("adamw_ffn_down",
                           [g.reshape(N_DEV, r_down // 2, D) for g in (gp_down0a, gp_down0b, gp_down1a, gp_down1b)],
                           ffn_down.reshape(halves), m_ffn_down.reshape(halves), v_ffn_down.reshape(halves)),
        ffn_down.shape)
    put("s5_glu_w", _adamw("adamw_glu", [gp_glu.reshape(N_DEV, D, -1)], s5_glu_w, m_s5_glu_w, v_s5_glu_w),
        s5_glu_w.shape)
    pool_rows = (1, n_pool * pr, pc)
    put("pool_w", _adamw("adamw_pool", [gp_pool.reshape(N_DEV, n_pool * pr, pc)], pool_w.reshape(pool_rows),
                         m_pool_w.reshape(pool_rows), v_pool_w.reshape(pool_rows)), pool_w.shape)
    put("ada_w", _adamw("adamw_ada_w", [grad_ada_w[i][None] for i in range(2)], ada_w, m_ada_w, v_ada_w), ada_w.shape)

    for nm, w, m, v, g in (("s5_b_re", s5_b_re, m_s5_b_re, v_s5_b_re, g_b_re),
                           ("s5_b_im", s5_b_im, m_s5_b_im, v_s5_b_im, g_b_im),
                           ("s5_c_re", s5_c_re, m_s5_c_re, v_s5_c_re, g_c[:, 0]),
                           ("s5_c_im", s5_c_im, m_s5_c_im, v_s5_c_im, g_c[:, 1])):
        rows = (1, w.size // w.shape[-1], w.shape[-1])
        put(nm, _adamw("adamw_" + nm, [g.reshape(rows)], w.reshape(rows), m.reshape(rows), v.reshape(rows)), w.shape)

    small = [
        ("c_ctx", c_ctx, m_c_ctx, v_c_ctx, g_cctx),
        ("ada_b", ada_b, m_ada_b, v_ada_b, grad_ada_b),
        ("norm_g", norm_g, m_norm_g, v_norm_g, _my_block(g_norm, 2, ng_loc)),
        ("s5_lam_re", s5_lam_re, m_s5_lam_re, v_s5_lam_re, g_lam_re),
        ("s5_lam_im", s5_lam_im, m_s5_lam_im, v_s5_lam_im, g_lam_im),
        ("s5_log_step", s5_log_step, m_s5_log_step, v_s5_log_step, g_log_step),
        ("s5_d", s5_d, m_s5_d, v_s5_d, g_d),
        ("pool_scale", pool_scale, m_pool_scale, v_pool_scale, _my_block(g_pscale, 0, ng_loc)),
        ("ffn_conv", ffn_conv, m_ffn_conv, v_ffn_conv, _my_block(g_conv, 2, nb_up)),
        ("ffn_conv_b", ffn_conv_b, m_ffn_conv_b, v_ffn_conv_b, g_conv_b),
    ]
    n_sm = sum(w.size for _, w, _, _, _ in small)
    rows_sm = -(-n_sm // (512 * LANE)) * 512

    def flat_of(k):
        f = jnp.concatenate([t[k].reshape(-1) for t in small])
        return jnp.pad(f, (0, rows_sm * LANE - n_sm)).reshape(rows_sm, LANE)

    res_sm = _adamw("adamw_small", [flat_of(4)[None]], flat_of(1)[None], flat_of(2)[None], flat_of(3)[None])
    o = 0
    for name, w, _, _, _ in small:
        out[name] = tuple(r.reshape(-1)[o:o + w.size].reshape(w.shape) for r in res_sm)
        o += w.size

    order = ["c_ctx", "ada_w", "ada_b", "norm_g", "s5_lam_re", "s5_lam_im", "s5_log_step", "s5_b_re", "s5_b_im",
             "s5_c_re", "s5_c_im", "s5_d", "s5_glu_w", "pool_w", "pool_scale", "ffn_up", "ffn_conv", "ffn_conv_b",
             "ffn_down"]
    return (loss, grad_x.reshape(x.shape), *[out[n][0] for n in order], *[out[n][1] for n in order],
            *[out[n][2] for n in order], *[out[n][3] for n in order])
```
